```python
import math
import jax, jax.numpy as jnp
from jax import lax
import numpy as np

D_MODEL = 2048
BATCH = 8
SEQ = 4096
DEPTH = 1

D_MIX = D_MODEL
ATTN_W = D_MIX // 2
SSM_W = D_MIX - ATTN_W
HEAD_DIM = 64
N_HEADS = ATTN_W // HEAD_DIM
N_KV_HEADS = 4
KV_REP = N_HEADS // N_KV_HEADS
KV_W = N_KV_HEADS * HEAD_DIM
WINDOW = 128
BLOCK = 128
ROPE_THETA = 10000.0
SSM_H = 16
SSM_G = SSM_W // SSM_H
SSM_P = 64
SSM_CHUNK = 128
DT_MIN = 1e-3
DT_MAX = 1e-1
NORM_EPS = 1e-6
IN_W = ATTN_W + 2 * KV_W + ATTN_W + SSM_W + SSM_W

kernel_name = "hymba_swa_sink_s5_hybrid"


def rms_norm(x, w):
    xf = x.astype(jnp.float32)
    y = xf * lax.rsqrt(jnp.mean(xf * xf, axis=-1, keepdims=True) + NORM_EPS)
    return (y * w.astype(jnp.float32)).astype(x.dtype)


def rope_tables(positions):
    inv_freq = ROPE_THETA ** (-jnp.arange(0, HEAD_DIM, 2, dtype=jnp.float32) / HEAD_DIM)
    ang = positions.astype(jnp.float32)[..., None] * inv_freq
    return jnp.cos(ang)[:, :, None, :], jnp.sin(ang)[:, :, None, :]


def apply_rope(t, cos, sin):
    tf = t.astype(jnp.float32)
    t1, t2 = tf[..., : HEAD_DIM // 2], tf[..., HEAD_DIM // 2 :]
    out = jnp.concatenate([t1 * cos - t2 * sin, t2 * cos + t1 * sin], axis=-1)
    return out.astype(t.dtype)


def swa_sink_attention(q, k, v, positions, q_norm_w, k_norm_w, sinks):
    B, L = q.shape[0], q.shape[1]
    nb = L // BLOCK
    q = q.reshape(B, L, N_HEADS, HEAD_DIM)
    k = k.reshape(B, L, N_KV_HEADS, HEAD_DIM)
    v = v.reshape(B, L, N_KV_HEADS, HEAD_DIM)
    cos, sin = rope_tables(positions)
    q = apply_rope(rms_norm(q, q_norm_w), cos, sin)
    k = apply_rope(rms_norm(k, k_norm_w), cos, sin)

    qb = q.reshape(B, nb, BLOCK, N_KV_HEADS, KV_REP, HEAD_DIM)

    def with_prev(t):
        t = t.reshape(B, nb, BLOCK, N_KV_HEADS, HEAD_DIM)
        prev = jnp.pad(t[:, :-1], ((0, 0), (1, 0), (0, 0), (0, 0), (0, 0)))
        return jnp.concatenate([prev, t], axis=2)

    kb, vb = with_prev(k), with_prev(v)
    scale = 1.0 / math.sqrt(HEAD_DIM)
    s = jnp.einsum('bnqgrd,bnkgd->bngrqk', qb, kb).astype(jnp.float32) * scale

    qi = jnp.arange(BLOCK)[:, None] + BLOCK
    ki = jnp.arange(2 * BLOCK)[None, :]
    rel = qi - ki
    band = (rel >= 0) & (rel < WINDOW)
    has_prev = (jnp.arange(nb)[:, None, None] > 0) | (ki >= BLOCK)[None]
    mask = band[None] & has_prev
    s = jnp.where(mask[None, :, None, None], s, jnp.float32(-1e30))

    sink = jnp.broadcast_to(
        sinks.astype(jnp.float32).reshape(1, 1, N_KV_HEADS, KV_REP, 1, 1),
        s.shape[:-1] + (1,))
    p = jax.nn.softmax(jnp.concatenate([s, sink], axis=-1), axis=-1)[..., :-1]
    o = jnp.einsum('bngrqk,bnkgd->bnqgrd', p.astype(vb.dtype), vb)
    return o.reshape(B, L, ATTN_W)


def s5_ssm(u, a_re, a_im, log_step, b_re, b_im, c_re, c_im, d_skip):
    B, L = u.shape[0], u.shape[1]
    nc = L // SSM_CHUNK
    f32 = jnp.float32
    lam = lax.complex(a_re.astype(f32), a_im.astype(f32))
    delta = jnp.exp(log_step.astype(f32))[:, None]
    lam_bar = jnp.exp(lam * delta)
    b = lax.complex(b_re.astype(f32), b_im.astype(f32))
    b_bar = ((lam_bar - 1.0) / lam)[..., None] * b
    c = lax.complex(c_re.astype(f32), c_im.astype(f32))

    uf = u.astype(f32)
    ug = uf.reshape(B, nc, SSM_CHUNK, SSM_G, SSM_H).transpose(1, 0, 2, 3, 4)

    def combine(left, right):
        a_l, x_l = left
        a_r, x_r = right
        return a_r * a_l, a_r * x_l + x_r

    def chunk_step(h0, u_c):
        bu = jnp.einsum('gph,bigh->bigp', b_bar, u_c.astype(jnp.complex64))
        a = jnp.broadcast_to(lam_bar, bu.shape)
        a_cum, h_loc = lax.associative_scan(combine, (a, bu), axis=1)
        h = h_loc + a_cum * h0[:, None]
        y = jnp.einsum('ghp,bigp->bigh', c, h).real
        return h[:, -1], y

    h0 = jnp.zeros((B, SSM_G, SSM_P), jnp.complex64)
    _, ys = lax.scan(chunk_step, h0, ug)
    y = ys.transpose(1, 0, 2, 3, 4).reshape(B, L, SSM_W)
    return y + d_skip.astype(f32) * uf


def _fwd_setup_inputs(seed: int = 0) -> dict:
    key = jax.random.key(seed)
    ks = jax.random.split(key, 20)
    f32 = jnp.float32
    x = jax.random.normal(ks[0], (BATCH, SEQ, D_MODEL), f32)
    offs = jax.random.randint(ks[1], (BATCH, 1), 0, 1024, dtype=jnp.int32)
    positions = (jnp.arange(SEQ, dtype=jnp.int32)[None, :] + offs).astype(jnp.int32)
    norm_w = 1.0 + 0.02 * jax.random.normal(ks[2], (D_MODEL,), f32)
    w_in = jax.random.normal(ks[3], (D_MODEL, IN_W), f32) * D_MODEL ** -0.5
    q_norm_w = 1.0 + 0.02 * jax.random.normal(ks[4], (HEAD_DIM,), f32)
    k_norm_w = 1.0 + 0.02 * jax.random.normal(ks[5], (HEAD_DIM,), f32)
    sinks = jax.random.normal(ks[6], (N_HEADS,), f32)
    n = jnp.arange(SSM_P, dtype=f32)[None, :]
    a_re = -0.5 + 0.01 * jax.random.normal(ks[7], (SSM_G, SSM_P), f32)
    a_im = math.pi * n + 0.01 * jax.random.normal(ks[8], (SSM_G, SSM_P), f32)
    log_step = jax.random.uniform(ks[9], (SSM_G,), f32, math.log(DT_MIN), math.log(DT_MAX))
    b_scale = (2.0 * SSM_H) ** -0.5
    b_re = jax.random.normal(ks[10], (SSM_G, SSM_P, SSM_H), f32) * b_scale
    b_im = jax.random.normal(ks[11], (SSM_G, SSM_P, SSM_H), f32) * b_scale
    c_scale = (2.0 * SSM_P) ** -0.5
    c_re = jax.random.normal(ks[12], (SSM_G, SSM_H, SSM_P), f32) * c_scale
    c_im = jax.random.normal(ks[13], (SSM_G, SSM_H, SSM_P), f32) * c_scale
    d_skip = jax.random.normal(ks[14], (SSM_W,), f32)
    w_glu = jax.random.normal(ks[15], (SSM_W, SSM_W), f32) * SSM_W ** -0.5
    b_glu = 0.02 * jax.random.normal(ks[16], (SSM_W,), f32)
    attn_out_norm_w = 1.0 + 0.02 * jax.random.normal(ks[17], (ATTN_W,), f32)
    ssm_out_norm_w = 1.0 + 0.02 * jax.random.normal(ks[18], (SSM_W,), f32)
    w_out = jax.random.normal(ks[19], (D_MIX, D_MODEL), f32) * D_MIX ** -0.5
    return {"x": x, "positions": positions, "norm_w": norm_w, "w_in": w_in,
            "q_norm_w": q_norm_w, "k_norm_w": k_norm_w, "sinks": sinks,
            "a_re": a_re, "a_im": a_im, "log_step": log_step,
            "b_re": b_re, "b_im": b_im, "c_re": c_re, "c_im": c_im,
            "d_skip": d_skip, "w_glu": w_glu, "b_glu": b_glu,
            "attn_out_norm_w": attn_out_norm_w, "ssm_out_norm_w": ssm_out_norm_w,
            "w_out": w_out}


def _fwd_reference(x, positions, norm_w, w_in, q_norm_w, k_norm_w, sinks,
              a_re, a_im, log_step, b_re, b_im, c_re, c_im, d_skip, w_glu, b_glu,
              attn_out_norm_w, ssm_out_norm_w, w_out):
    for _ in range(DEPTH):
        h = rms_norm(x, norm_w)
        proj = jnp.einsum('bld,de->ble', h, w_in)
        splits = np.cumsum([ATTN_W, KV_W, KV_W, ATTN_W, SSM_W])
        q, k, v, z_attn, u, z_ssm = jnp.split(proj, splits, axis=-1)

        o_attn = swa_sink_attention(q, k, v, positions, q_norm_w, k_norm_w, sinks)
        o_attn = o_attn * jax.nn.silu(z_attn)

        y = s5_ssm(u, a_re, a_im, log_step, b_re, b_im, c_re, c_im, d_skip)
        y = jax.nn.gelu(y)
        y = y * jax.nn.sigmoid(y @ w_glu.astype(jnp.float32) + b_glu.astype(jnp.float32))
        o_ssm = y.astype(x.dtype) * jax.nn.silu(z_ssm)

        merged = jnp.concatenate([rms_norm(o_attn, attn_out_norm_w),
                                  rms_norm(o_ssm, ssm_out_norm_w)], axis=-1)
        x = x + jnp.einsum('ble,ed->bld', merged, w_out).astype(x.dtype)
    return x


import jax as _jax
import jax.numpy as _jnp

TWIN_FORMAT = 'train_step'
FWD_PARAMS = ['x', 'positions', 'norm_w', 'w_in', 'q_norm_w', 'k_norm_w', 'sinks', 'a_re', 'a_im', 'log_step', 'b_re', 'b_im', 'c_re', 'c_im', 'd_skip', 'w_glu', 'b_glu', 'attn_out_norm_w', 'ssm_out_norm_w', 'w_out']
TWIN_WEIGHTS = ['norm_w', 'w_in', 'q_norm_w', 'k_norm_w', 'sinks', 'a_re', 'a_im', 'log_step', 'b_re', 'b_im', 'c_re', 'c_im', 'd_skip', 'w_glu', 'b_glu', 'attn_out_norm_w', 'ssm_out_norm_w', 'w_out']
TWIN_DIFF_INPUT = 'x'
TWIN_INPUTS = ['x', 'positions', 'norm_w', 'w_in', 'q_norm_w', 'k_norm_w', 'sinks', 'a_re', 'a_im', 'log_step', 'b_re', 'b_im', 'c_re', 'c_im', 'd_skip', 'w_glu', 'b_glu', 'attn_out_norm_w', 'ssm_out_norm_w', 'w_out', 'loss_target', 'm_norm_w', 'm_w_in', 'm_q_norm_w', 'm_k_norm_w', 'm_sinks', 'm_a_re', 'm_a_im', 'm_log_step', 'm_b_re', 'm_b_im', 'm_c_re', 'm_c_im', 'm_d_skip', 'm_w_glu', 'm_b_glu', 'm_attn_out_norm_w', 'm_ssm_out_norm_w', 'm_w_out', 'v_norm_w', 'v_w_in', 'v_q_norm_w', 'v_k_norm_w', 'v_sinks', 'v_a_re', 'v_a_im', 'v_log_step', 'v_b_re', 'v_b_im', 'v_c_re', 'v_c_im', 'v_d_skip', 'v_w_glu', 'v_b_glu', 'v_attn_out_norm_w', 'v_ssm_out_norm_w', 'v_w_out']
TWIN_OUTPUTS = ['loss', 'grad_x', 'grad_norm_w', 'grad_w_in', 'grad_q_norm_w', 'grad_k_norm_w', 'grad_sinks', 'grad_a_re', 'grad_a_im', 'grad_log_step', 'grad_b_re', 'grad_b_im', 'grad_c_re', 'grad_c_im', 'grad_d_skip', 'grad_w_glu', 'grad_b_glu', 'grad_attn_out_norm_w', 'grad_ssm_out_norm_w', 'grad_w_out', 'delta_norm_w', 'delta_w_in', 'delta_q_norm_w', 'delta_k_norm_w', 'delta_sinks', 'delta_a_re', 'delta_a_im', 'delta_log_step', 'delta_b_re', 'delta_b_im', 'delta_c_re', 'delta_c_im', 'delta_d_skip', 'delta_w_glu', 'delta_b_glu', 'delta_attn_out_norm_w', 'delta_ssm_out_norm_w', 'delta_w_out', 'new_m_norm_w', 'new_m_w_in', 'new_m_q_norm_w', 'new_m_k_norm_w', 'new_m_sinks', 'new_m_a_re', 'new_m_a_im', 'new_m_log_step', 'new_m_b_re', 'new_m_b_im', 'new_m_c_re', 'new_m_c_im', 'new_m_d_skip', 'new_m_w_glu', 'new_m_b_glu', 'new_m_attn_out_norm_w', 'new_m_ssm_out_norm_w', 'new_m_w_out', 'new_v_norm_w', 'new_v_w_in', 'new_v_q_norm_w', 'new_v_k_norm_w', 'new_v_sinks', 'new_v_a_re', 'new_v_a_im', 'new_v_log_step', 'new_v_b_re', 'new_v_b_im', 'new_v_c_re', 'new_v_c_im', 'new_v_d_skip', 'new_v_w_glu', 'new_v_b_glu', 'new_v_attn_out_norm_w', 'new_v_ssm_out_norm_w', 'new_v_w_out']
TWIN_LEAF_KINDS = {'loss': 'loss', 'grad_x': 'grad_x', 'grad_norm_w': 'grad_w', 'grad_w_in': 'grad_w', 'grad_q_norm_w': 'grad_w', 'grad_k_norm_w': 'grad_w', 'grad_sinks': 'grad_w', 'grad_a_re': 'grad_w', 'grad_a_im': 'grad_w', 'grad_log_step': 'grad_w', 'grad_b_re': 'grad_w', 'grad_b_im': 'grad_w', 'grad_c_re': 'grad_w', 'grad_c_im': 'grad_w', 'grad_d_skip': 'grad_w', 'grad_w_glu': 'grad_w', 'grad_b_glu': 'grad_w', 'grad_attn_out_norm_w': 'grad_w', 'grad_ssm_out_norm_w': 'grad_w', 'grad_w_out': 'grad_w', 'delta_norm_w': 'delta_w', 'delta_w_in': 'delta_w', 'delta_q_norm_w': 'delta_w', 'delta_k_norm_w': 'delta_w', 'delta_sinks': 'delta_w', 'delta_a_re': 'delta_w', 'delta_a_im': 'delta_w', 'delta_log_step': 'delta_w', 'delta_b_re': 'delta_w', 'delta_b_im': 'delta_w', 'delta_c_re': 'delta_w', 'delta_c_im': 'delta_w', 'delta_d_skip': 'delta_w', 'delta_w_glu': 'delta_w', 'delta_b_glu': 'delta_w', 'delta_attn_out_norm_w': 'delta_w', 'delta_ssm_out_norm_w': 'delta_w', 'delta_w_out': 'delta_w', 'new_m_norm_w': 'new_m', 'new_m_w_in': 'new_m', 'new_m_q_norm_w': 'new_m', 'new_m_k_norm_w': 'new_m', 'new_m_sinks': 'new_m', 'new_m_a_re': 'new_m', 'new_m_a_im': 'new_m', 'new_m_log_step': 'new_m', 'new_m_b_re': 'new_m', 'new_m_b_im': 'new_m', 'new_m_c_re': 'new_m', 'new_m_c_im': 'new_m', 'new_m_d_skip': 'new_m', 'new_m_w_glu': 'new_m', 'new_m_b_glu': 'new_m', 'new_m_attn_out_norm_w': 'new_m', 'new_m_ssm_out_norm_w': 'new_m', 'new_m_w_out': 'new_m', 'new_v_norm_w': 'new_v', 'new_v_w_in': 'new_v', 'new_v_q_norm_w': 'new_v', 'new_v_k_norm_w': 'new_v', 'new_v_sinks': 'new_v', 'new_v_a_re': 'new_v', 'new_v_a_im': 'new_v', 'new_v_log_step': 'new_v', 'new_v_b_re': 'new_v', 'new_v_b_im': 'new_v', 'new_v_c_re': 'new_v', 'new_v_c_im': 'new_v', 'new_v_d_skip': 'new_v', 'new_v_w_glu': 'new_v', 'new_v_b_glu': 'new_v', 'new_v_attn_out_norm_w': 'new_v', 'new_v_ssm_out_norm_w': 'new_v', 'new_v_w_out': 'new_v'}


def _forward(args):
    return _fwd_reference(*[args[k] for k in FWD_PARAMS])


def _output_shape():
    def fwd():
        inp = _fwd_setup_inputs(0)
        return _fwd_reference(*[inp[k] for k in FWD_PARAMS])
    out = _jax.eval_shape(fwd)
    return out.shape, out.dtype

N_MICROBATCH = 1
ADAM_LR = 0.001
ADAM_B1 = 0.9
ADAM_B2 = 0.999
ADAM_EPS = 1e-08
ADAM_WD = 0.01
ADAM_STEP = 10
PER_EXAMPLE_BATCH_AXIS = {'x': 0, 'positions': 0, 'loss_target': 0}
SHARED_INPUTS = []
_WEIGHT_DTYPES = {'norm_w': _jnp.float32, 'w_in': _jnp.float32, 'q_norm_w': _jnp.float32, 'k_norm_w': _jnp.float32, 'sinks': _jnp.float32, 'a_re': _jnp.float32, 'a_im': _jnp.float32, 'log_step': _jnp.float32, 'b_re': _jnp.float32, 'b_im': _jnp.float32, 'c_re': _jnp.float32, 'c_im': _jnp.float32, 'd_skip': _jnp.float32, 'w_glu': _jnp.float32, 'b_glu': _jnp.float32, 'attn_out_norm_w': _jnp.float32, 'ssm_out_norm_w': _jnp.float32, 'w_out': _jnp.float32}
MOMENT_SCALE = {'norm_w': 3.097847e-01, 'w_in': 1.931440e-01, 'q_norm_w': 6.143489e-01, 'k_norm_w': 6.213514e-01, 'sinks': 6.733670e-02, 'a_re': 7.700832e-03, 'a_im': 8.048847e-03, 'log_step': 9.748351e+00, 'b_re': 5.520516e-03, 'b_im': 5.527719e-03, 'c_re': 1.114468e-02, 'c_im': 1.123905e-02, 'd_skip': 8.127332e-01, 'w_glu': 1.456363e-01, 'b_glu': 4.733383e-01, 'attn_out_norm_w': 1.589198e+01, 'ssm_out_norm_w': 2.731906e+01, 'w_out': 4.767608e-01}


def _to_microbatches(a, axis):
    t = _jnp.moveaxis(a, axis, 0)
    t = t.reshape((N_MICROBATCH, t.shape[0] // N_MICROBATCH) + t.shape[1:])
    return _jnp.moveaxis(t, 1, axis + 1)


def setup_inputs(seed: int = 0) -> dict:
    inp = _fwd_setup_inputs(seed)
    key = _jax.random.fold_in(_jax.random.key(seed), 7919)
    shape, _ = _output_shape()
    out = dict(inp)
    out["loss_target"] = _jax.random.normal(_jax.random.fold_in(key, 0), shape, _jnp.float32)
    for i, name in enumerate(TWIN_WEIGHTS):
        w = inp[name].astype(_jnp.float32)
        if MOMENT_SCALE is None:
            s = _jnp.sqrt(_jnp.mean(_jnp.square(w)) + 1e-30)
        else:
            s = MOMENT_SCALE[name]
        km, kv = _jax.random.split(_jax.random.fold_in(key, i + 1))
        out[name] = w
        out["m_" + name] = s * _jax.random.normal(km, w.shape, _jnp.float32)
        out["v_" + name] = (s * s) * _jax.random.uniform(kv, w.shape, _jnp.float32, 0.5, 1.5)
    if N_MICROBATCH > 1:
        for name, axis in PER_EXAMPLE_BATCH_AXIS.items():
            out[name] = _to_microbatches(out[name], axis)
    return {'x': out['x'], 'positions': out['positions'], 'norm_w': out['norm_w'], 'w_in': out['w_in'], 'q_norm_w': out['q_norm_w'], 'k_norm_w': out['k_norm_w'], 'sinks': out['sinks'], 'a_re': out['a_re'], 'a_im': out['a_im'], 'log_step': out['log_step'], 'b_re': out['b_re'], 'b_im': out['b_im'], 'c_re': out['c_re'], 'c_im': out['c_im'], 'd_skip': out['d_skip'], 'w_glu': out['w_glu'], 'b_glu': out['b_glu'], 'attn_out_norm_w': out['attn_out_norm_w'], 'ssm_out_norm_w': out['ssm_out_norm_w'], 'w_out': out['w_out'], 'loss_target': out['loss_target'], 'm_norm_w': out['m_norm_w'], 'm_w_in': out['m_w_in'], 'm_q_norm_w': out['m_q_norm_w'], 'm_k_norm_w': out['m_k_norm_w'], 'm_sinks': out['m_sinks'], 'm_a_re': out['m_a_re'], 'm_a_im': out['m_a_im'], 'm_log_step': out['m_log_step'], 'm_b_re': out['m_b_re'], 'm_b_im': out['m_b_im'], 'm_c_re': out['m_c_re'], 'm_c_im': out['m_c_im'], 'm_d_skip': out['m_d_skip'], 'm_w_glu': out['m_w_glu'], 'm_b_glu': out['m_b_glu'], 'm_attn_out_norm_w': out['m_attn_out_norm_w'], 'm_ssm_out_norm_w': out['m_ssm_out_norm_w'], 'm_w_out': out['m_w_out'], 'v_norm_w': out['v_norm_w'], 'v_w_in': out['v_w_in'], 'v_q_norm_w': out['v_q_norm_w'], 'v_k_norm_w': out['v_k_norm_w'], 'v_sinks': out['v_sinks'], 'v_a_re': out['v_a_re'], 'v_a_im': out['v_a_im'], 'v_log_step': out['v_log_step'], 'v_b_re': out['v_b_re'], 'v_b_im': out['v_b_im'], 'v_c_re': out['v_c_re'], 'v_c_im': out['v_c_im'], 'v_d_skip': out['v_d_skip'], 'v_w_glu': out['v_w_glu'], 'v_b_glu': out['v_b_glu'], 'v_attn_out_norm_w': out['v_attn_out_norm_w'], 'v_ssm_out_norm_w': out['v_ssm_out_norm_w'], 'v_w_out': out['v_w_out']}


def _loss(weights, diff, rest, loss_target):
    with _jax.named_scope("forward"):
        args = {**rest, TWIN_DIFF_INPUT: diff, **{k: w.astype(_WEIGHT_DTYPES[k]) for k, w in weights.items()}}
        y = _forward(args)
    with _jax.named_scope("loss_head"):
        err = _jnp.square(y.astype(_jnp.float32) - loss_target)
        return 0.5 * _jnp.sum(_jnp.mean(err, axis=-1)) if err.ndim else 0.5 * err


def _adamw(w, g, m, v):
    m = ADAM_B1 * m + (1.0 - ADAM_B1) * g
    v = ADAM_B2 * v + (1.0 - ADAM_B2) * _jnp.square(g)
    m_hat = m / (1.0 - ADAM_B1 ** ADAM_STEP)
    v_hat = v / (1.0 - ADAM_B2 ** ADAM_STEP)
    delta = -ADAM_LR * (m_hat / (_jnp.sqrt(v_hat) + ADAM_EPS) + ADAM_WD * w)
    return delta, m, v


def reference(x, positions, norm_w, w_in, q_norm_w, k_norm_w, sinks, a_re, a_im, log_step, b_re, b_im, c_re, c_im, d_skip, w_glu, b_glu, attn_out_norm_w, ssm_out_norm_w, w_out, loss_target, m_norm_w, m_w_in, m_q_norm_w, m_k_norm_w, m_sinks, m_a_re, m_a_im, m_log_step, m_b_re, m_b_im, m_c_re, m_c_im, m_d_skip, m_w_glu, m_b_glu, m_attn_out_norm_w, m_ssm_out_norm_w, m_w_out, v_norm_w, v_w_in, v_q_norm_w, v_k_norm_w, v_sinks, v_a_re, v_a_im, v_log_step, v_b_re, v_b_im, v_c_re, v_c_im, v_d_skip, v_w_glu, v_b_glu, v_attn_out_norm_w, v_ssm_out_norm_w, v_w_out):
    given = dict(x=x, positions=positions, norm_w=norm_w, w_in=w_in, q_norm_w=q_norm_w, k_norm_w=k_norm_w, sinks=sinks, a_re=a_re, a_im=a_im, log_step=log_step, b_re=b_re, b_im=b_im, c_re=c_re, c_im=c_im, d_skip=d_skip, w_glu=w_glu, b_glu=b_glu, attn_out_norm_w=attn_out_norm_w, ssm_out_norm_w=ssm_out_norm_w, w_out=w_out, loss_target=loss_target, m_norm_w=m_norm_w, m_w_in=m_w_in, m_q_norm_w=m_q_norm_w, m_k_norm_w=m_k_norm_w, m_sinks=m_sinks, m_a_re=m_a_re, m_a_im=m_a_im, m_log_step=m_log_step, m_b_re=m_b_re, m_b_im=m_b_im, m_c_re=m_c_re, m_c_im=m_c_im, m_d_skip=m_d_skip, m_w_glu=m_w_glu, m_b_glu=m_b_glu, m_attn_out_norm_w=m_attn_out_norm_w, m_ssm_out_norm_w=m_ssm_out_norm_w, m_w_out=m_w_out, v_norm_w=v_norm_w, v_w_in=v_w_in, v_q_norm_w=v_q_norm_w, v_k_norm_w=v_k_norm_w, v_sinks=v_sinks, v_a_re=v_a_re, v_a_im=v_a_im, v_log_step=v_log_step, v_b_re=v_b_re, v_b_im=v_b_im, v_c_re=v_c_re, v_c_im=v_c_im, v_d_skip=v_d_skip, v_w_glu=v_w_glu, v_b_glu=v_b_glu, v_attn_out_norm_w=v_attn_out_norm_w, v_ssm_out_norm_w=v_ssm_out_norm_w, v_w_out=v_w_out)
    weights = {n: given[n] for n in TWIN_WEIGHTS}
    shared = {n: given[n] for n in SHARED_INPUTS}
    per_example = {n: given[n] for n in ['x', 'positions']}
    grad_fn = _jax.value_and_grad(_loss, argnums=(0, 1))

    def one_microbatch(ex, loss_target):
        ex = dict(ex)
        diff = ex.pop(TWIN_DIFF_INPUT)
        return grad_fn(weights, diff, {**shared, **ex}, loss_target)

    if N_MICROBATCH == 1:
        loss, (grad_w, grad_x) = one_microbatch(per_example, given["loss_target"])
    else:
        def body(carry, xs):
            loss_sum, grad_sum = carry
            l_k, (gw_k, gx_k) = one_microbatch(xs[0], xs[1])
            with _jax.named_scope("update"):
                return (loss_sum + l_k, _jax.tree.map(_jnp.add, grad_sum, gw_k)), gx_k

        init = (_jnp.zeros((), _jnp.float32), _jax.tree.map(_jnp.zeros_like, weights))
        (loss, grad_w), grad_x = _jax.lax.scan(body, init, (per_example, given["loss_target"]))
    with _jax.named_scope("update"):
        delta_w, new_m, new_v = {}, {}, {}
        for n in TWIN_WEIGHTS:
            delta_w[n], new_m[n], new_v[n] = _adamw(weights[n], grad_w[n], given["m_" + n], given["v_" + n])
    return (loss, grad_x, *[grad_w[n] for n in TWIN_WEIGHTS], *[delta_w[n] for n in TWIN_WEIGHTS],
            *[new_m[n] for n in TWIN_WEIGHTS], *[new_v[n] for n in TWIN_WEIGHTS])
```

```python
import math

import jax
import jax.numpy as jnp
from jax import lax
from jax.experimental import pallas as pl
from jax.experimental.pallas import tpu as pltpu

F32 = jnp.float32
BF16 = jnp.bfloat16

D_MODEL = 2048
ATTN_W = 1024
SSM_W = 1024
HEAD_DIM = 64
N_HEADS = 16
N_KV_HEADS = 4
KV_W = 256
BLOCK = 128
IN_W = 4608
SHARD_W = IN_W // 4
ROPE_THETA = 10000.0
SSM_H = 16
SSM_G = 64
SSM_P = 64
NORM_EPS = 1e-6
ADAM_LR = 0.001
ADAM_B1 = 0.9
ADAM_B2 = 0.999
ADAM_EPS = 1e-08
ADAM_WD = 0.01
ADAM_STEP = 10

N_SEG = 8
SSM_GB = 4
SSM_CH = 256
SSM_ST = 1024
SCAN_ROWS = 256
SCAN_LW = 512
VMEM_LIMIT = 56 * 1024 * 1024
MESH_AXES = ("x", "y", "c")
ANY = pl.BlockSpec(memory_space=pl.ANY)

SMALL = (("norm_w", 2048), ("q_norm_w", 64), ("k_norm_w", 64), ("sinks", 16), ("a_re", 4096), ("a_im", 4096),
         ("log_step", 64), ("b_re", 65536), ("b_im", 65536), ("c_re", 65536), ("c_im", 65536), ("d_skip", 1024),
         ("b_glu", 1024), ("attn_out_norm_w", 1024), ("ssm_out_norm_w", 1024))
SMALL_SHAPES = {"norm_w": (2048,), "q_norm_w": (64,), "k_norm_w": (64,), "sinks": (16,), "a_re": (64, 64),
                "a_im": (64, 64), "log_step": (64,), "b_re": (64, 64, 16), "b_im": (64, 64, 16),
                "c_re": (64, 16, 64), "c_im": (64, 16, 64), "d_skip": (1024,), "b_glu": (1024,),
                "attn_out_norm_w": (1024,), "ssm_out_norm_w": (1024,)}
PACK_ROWS = 272
PACK_TOTAL = 8 * PACK_ROWS * 128


def _params(sem=None):
    return pltpu.CompilerParams(dimension_semantics=sem, vmem_limit_bytes=VMEM_LIMIT)


def _dot(a, b):
    return jnp.dot(a, b, preferred_element_type=F32)


def _dot_nt(a, b):
    return lax.dot_general(a, b, (((1,), (1,)), ((), ())), preferred_element_type=F32)


def _dot_tn(a, b):
    return lax.dot_general(a, b, (((0,), (0,)), ((), ())), preferred_element_type=F32)


def _sigmoid(x):
    return 1.0 / (1.0 + jnp.exp(-x))


def _silu(x):
    return x * _sigmoid(x)


def _dsilu(x):
    s = _sigmoid(x)
    return s * (1.0 + x * (1.0 - s))


_GELU_C = math.sqrt(2.0 / math.pi)


def _gelu(x):
    return 0.5 * x * (1.0 + jnp.tanh(_GELU_C * (x + 0.044715 * x * x * x)))


def _dgelu(x):
    t = jnp.tanh(_GELU_C * (x + 0.044715 * x * x * x))
    return 0.5 * (1.0 + t) + 0.5 * x * (1.0 - t * t) * _GELU_C * (1.0 + 3.0 * 0.044715 * x * x)


def _matmul(a, b, *, tm, tn, tk, name, out_dtype=F32):
    m, k = a.shape
    _, n = b.shape
    nk = k // tk

    def body(a_ref, b_ref, o_ref, acc_ref):
        kk = pl.program_id(2)

        @pl.when(kk == 0)
        def _():
            acc_ref[...] = jnp.zeros_like(acc_ref)

        acc_ref[...] += _dot(a_ref[...], b_ref[...])

        @pl.when(kk == nk - 1)
        def _():
            o_ref[...] = acc_ref[...].astype(out_dtype)

    return pl.pallas_call(
        body, name=name, grid=(m // tm, n // tn, nk),
        in_specs=[pl.BlockSpec((tm, tk), lambda i, j, kk: (i, kk)), pl.BlockSpec((tk, tn), lambda i, j, kk: (kk, j))],
        out_specs=pl.BlockSpec((tm, tn), lambda i, j, kk: (i, j)),
        out_shape=jax.ShapeDtypeStruct((m, n), out_dtype),
        scratch_shapes=[pltpu.VMEM((tm, tn), F32)],
        compiler_params=_params(("parallel", "parallel", "arbitrary")),
    )(a, b)


def _inproj(x, norm_w, w_slabs):
    t = x.shape[0]
    tm = 512

    def body(x_ref, nw_ref, w_ref, proj_ref, hn_ref, hn_s):
        @pl.when(pl.program_id(1) == 0)
        def _():
            xv = x_ref[...]
            r = lax.rsqrt(jnp.mean(xv * xv, axis=1, keepdims=True) + NORM_EPS)
            hn = (xv * r * nw_ref[...]).astype(BF16)
            hn_s[...] = hn
            hn_ref[...] = hn

        proj_ref[...] = _dot(hn_s[...], w_ref[...])

    return pl.pallas_call(
        body, name="inproj", grid=(t // tm, 4),
        in_specs=[pl.BlockSpec((tm, D_MODEL), lambda i, j: (i, 0)), pl.BlockSpec((1, D_MODEL), lambda i, j: (0, 0)),
                  pl.BlockSpec((None, D_MODEL, SHARD_W), lambda i, j: (j, 0, 0))],
        out_specs=[pl.BlockSpec((tm, SHARD_W), lambda i, j: (i, j)), pl.BlockSpec((tm, D_MODEL), lambda i, j: (i, 0))],
        out_shape=[jax.ShapeDtypeStruct((t, IN_W), F32), jax.ShapeDtypeStruct((t, D_MODEL), BF16)],
        scratch_shapes=[pltpu.VMEM((tm, D_MODEL), BF16)],
        compiler_params=_params(("parallel", "arbitrary")),
    )(x, norm_w.reshape(1, D_MODEL), w_slabs)


def _lane128():
    return lax.broadcasted_iota(jnp.int32, (1, 128), 1)


def _head_sums(v):
    lo = _lane128() < 64
    s_lo = jnp.sum(jnp.where(lo, v, 0.0), axis=1, keepdims=True)
    s_hi = jnp.sum(jnp.where(lo, 0.0, v), axis=1, keepdims=True)
    return jnp.where(lo, s_lo, s_hi)


def _rot_half(t):
    first = (_lane128() % 64) < 32
    return jnp.where(first, -pltpu.roll(t, 96, 1), pltpu.roll(t, 32, 1))


def _prep_tile(t, w, cos, sin):
    r = lax.rsqrt(_head_sums(t * t) * (1.0 / HEAD_DIM) + NORM_EPS)
    tn = t * r * w
    return tn * cos + _rot_half(tn) * sin


def _prep_tile_bwd(t, w, cos, sin, g):
    r = lax.rsqrt(_head_sums(t * t) * (1.0 / HEAD_DIM) + NORM_EPS)
    d_tn = g * cos - _rot_half(g * sin)
    th = t * r
    dw = jnp.sum(d_tn * th, axis=0, keepdims=True)
    gh = d_tn * w
    m = _head_sums(gh * th) * (1.0 / HEAD_DIM)
    return r * (gh - th * m), dw


def _band_mask(n):
    qi = lax.broadcasted_iota(jnp.int32, (BLOCK, 2 * BLOCK), 0) + BLOCK
    ki = lax.broadcasted_iota(jnp.int32, (BLOCK, 2 * BLOCK), 1)
    rel = qi - ki
    return (rel >= 0) & (rel < BLOCK) & ((n > 0) | (ki >= BLOCK))


def _half_select(tile, half):
    return jnp.where((_lane128() < 64) == (half == 0), tile, 0.0)


def _attn_specs(nb):
    last = nb - 1
    qi = lambda n: (jnp.minimum(n, last), 0)
    prev = lambda n: jnp.maximum(n - 1, 0)
    cur = lambda n: jnp.minimum(n, last)
    specs = [
        pl.BlockSpec((BLOCK, ATTN_W), qi),
        pl.BlockSpec((BLOCK, KV_W), lambda n: (cur(n), 4)),
        pl.BlockSpec((BLOCK, KV_W), lambda n: (prev(n), 4)),
        pl.BlockSpec((BLOCK, KV_W), lambda n: (cur(n), 5)),
        pl.BlockSpec((BLOCK, KV_W), lambda n: (prev(n), 5)),
        pl.BlockSpec((BLOCK, 512), lambda n: (cur(n), 3)),
        pl.BlockSpec((BLOCK, 512), lambda n: (cur(n), 4)),
        pl.BlockSpec((BLOCK, 1), lambda n: (cur(n), 0)),
        pl.BlockSpec((BLOCK, 1), lambda n: (prev(n), 0)),
        pl.BlockSpec((1, 128), lambda n: (0, 0)),
        pl.BlockSpec((1, 128), lambda n: (0, 0)),
        pl.BlockSpec((1, 128), lambda n: (0, 0)),
        pl.BlockSpec((1, N_HEADS), lambda n: (0, 0)),
    ]
    return specs


def _attn_common(n, q_ref, kc_ref, kp_ref, vc_ref, vp_ref, pq_ref, pp_ref, invf_ref, qw_ref, kw_ref):
    invf = invf_ref[...]
    ang_q = pq_ref[...] * invf
    ang_p = pp_ref[...] * invf
    cos_q, sin_q = jnp.cos(ang_q), jnp.sin(ang_q)
    cos_k = jnp.concatenate([jnp.cos(ang_p), cos_q], axis=0)
    sin_k = jnp.concatenate([jnp.sin(ang_p), sin_q], axis=0)
    k_raw = jnp.concatenate([kp_ref[...], kc_ref[...]], axis=0)
    vv = jnp.concatenate([vp_ref[...], vc_ref[...]], axis=0).astype(BF16)
    kk = [_prep_tile(k_raw[:, 128 * i:128 * i + 128], kw_ref[...], cos_k, sin_k).astype(BF16) for i in range(2)]
    vt = [vv[:, 128 * i:128 * i + 128] for i in range(2)]
    qv = q_ref[...]
    qt = [_prep_tile(qv[:, 128 * i:128 * i + 128], qw_ref[...], cos_q, sin_q) for i in range(8)]
    return cos_q, sin_q, cos_k, sin_k, k_raw, kk, vt, qt


def _head_softmax(q_sel, kk_t, sink, valid):
    s = _dot_nt(q_sel, kk_t) * (1.0 / math.sqrt(HEAD_DIM))
    s = jnp.where(valid, s, -1e30)
    m = jnp.maximum(jnp.max(s, axis=1, keepdims=True), sink)
    p = jnp.exp(s - m)
    es = jnp.exp(sink - m)
    den = jnp.sum(p, axis=1, keepdims=True) + es
    return p / den, es / den


def _attn_fwd(proj, posf, invf, qw, kw, sinks):
    t = proj.shape[0]
    nb = t // BLOCK

    def body(q_ref, kc_ref, kp_ref, vc_ref, vp_ref, za0_ref, za1_ref, pq_ref, pp_ref, invf_ref, qw_ref, kw_ref,
             sink_ref, o_ref):
        n = pl.program_id(0)
        _, _, _, _, _, kk, vt, qt = _attn_common(n, q_ref, kc_ref, kp_ref, vc_ref, vp_ref, pq_ref, pp_ref, invf_ref,
                                                 qw_ref, kw_ref)
        valid = _band_mask(n)
        tiles = [jnp.zeros((BLOCK, 128), F32) for _ in range(8)]
        for h in range(N_HEADS):
            g = h // 4
            q_sel = _half_select(qt[h // 2], h % 2)
            if h % 2 != g % 2:
                q_sel = pltpu.roll(q_sel, 64, 1)
            p, _ = _head_softmax(q_sel.astype(BF16), kk[g // 2], sink_ref[:, h:h + 1], valid)
            o = _dot(p.astype(BF16), vt[g // 2])
            if h % 2 != g % 2:
                o = pltpu.roll(o, 64, 1)
            tiles[h // 2] = tiles[h // 2] + _half_select(o, h % 2)
        za = jnp.concatenate([za0_ref[...], za1_ref[...]], axis=1)
        o_ref[...] = jnp.concatenate(tiles, axis=1) * _silu(za)

    return pl.pallas_call(
        body, name="attn_fwd", grid=(nb,), in_specs=_attn_specs(nb),
        out_specs=pl.BlockSpec((BLOCK, ATTN_W), lambda n: (n, 0)),
        out_shape=jax.ShapeDtypeStruct((t, ATTN_W), F32),
        compiler_params=_params(("parallel",)),
    )(proj, proj, proj, proj, proj, proj, proj, posf, posf, invf, qw, kw, sinks)


def _attn_bwd(proj, posf, invf, qw, kw, sinks, doa):
    t = proj.shape[0]
    nb = t // BLOCK
    last = nb - 1

    def body(q_ref, kc_ref, kp_ref, vc_ref, vp_ref, za0_ref, za1_ref, pq_ref, pp_ref, invf_ref, qw_ref, kw_ref,
             sink_ref, doa_ref, dq_ref, dk_ref, dv_ref, dza_ref, gq_ref, gk_ref, gs_ref, dkk_s, dvv_s, ck_s, cv_s):
        n = pl.program_id(0)

        @pl.when(n == 0)
        def _():
            gq_ref[...] = jnp.zeros_like(gq_ref)
            gk_ref[...] = jnp.zeros_like(gk_ref)
            gs_ref[...] = jnp.zeros_like(gs_ref)
            ck_s[...] = jnp.zeros_like(ck_s)
            cv_s[...] = jnp.zeros_like(cv_s)

        @pl.when(n == nb)
        def _():
            dkk_s[...] = jnp.zeros_like(dkk_s)
            dvv_s[...] = jnp.zeros_like(dvv_s)

        @pl.when(n < nb)
        def _():
            cos_q, sin_q, _, _, _, kk, vt, qt = _attn_common(n, q_ref, kc_ref, kp_ref, vc_ref, vp_ref, pq_ref, pp_ref,
                                                             invf_ref, qw_ref, kw_ref)
            valid = _band_mask(n)
            za = jnp.concatenate([za0_ref[...], za1_ref[...]], axis=1)
            doa_v = doa_ref[...]
            do_full = doa_v * _silu(za)
            o_tiles = [jnp.zeros((BLOCK, 128), F32) for _ in range(8)]
            dq_tiles = [jnp.zeros((BLOCK, 128), F32) for _ in range(8)]
            dkk = [jnp.zeros((2 * BLOCK, 128), F32) for _ in range(2)]
            dvv = [jnp.zeros((2 * BLOCK, 128), F32) for _ in range(2)]
            gsink = jnp.zeros((1, 128), F32)
            lane = _lane128()
            for h in range(N_HEADS):
                g = h // 4
                swap = h % 2 != g % 2
                q_sel = _half_select(qt[h // 2], h % 2)
                do_h = _half_select(do_full[:, 128 * (h // 2):128 * (h // 2) + 128], h % 2)
                if swap:
                    q_sel = pltpu.roll(q_sel, 64, 1)
                    do_h = pltpu.roll(do_h, 64, 1)
                q_b = q_sel.astype(BF16)
                do_b = do_h.astype(BF16)
                p, psink = _head_softmax(q_b, kk[g // 2], sink_ref[:, h:h + 1], valid)
                p_b = p.astype(BF16)
                o = _half_select(_dot(p_b, vt[g // 2]), g % 2)
                dp = _dot_nt(do_b, vt[g // 2])
                delta = jnp.sum(p * dp, axis=1, keepdims=True)
                ds = p * (dp - delta) * (1.0 / math.sqrt(HEAD_DIM))
                ds_b = ds.astype(BF16)
                gsink = gsink + jnp.where(lane == h, -jnp.sum(psink * delta), 0.0)
                dq_h = _half_select(_dot(ds_b, kk[g // 2]), g % 2)
                if swap:
                    o = pltpu.roll(o, 64, 1)
                    dq_h = pltpu.roll(dq_h, 64, 1)
                o_tiles[h // 2] = o_tiles[h // 2] + o
                dq_tiles[h // 2] = dq_tiles[h // 2] + dq_h
                dkk[g // 2] = dkk[g // 2] + _dot_tn(ds_b, q_b)
                dvv[g // 2] = dvv[g // 2] + _dot_tn(p_b, do_b)
            dza_ref[...] = doa_v * jnp.concatenate(o_tiles, axis=1) * _dsilu(za)
            qv = q_ref[...]
            gq = jnp.zeros((1, 128), F32)
            out = []
            for i in range(8):
                d, dw = _prep_tile_bwd(qv[:, 128 * i:128 * i + 128], qw_ref[...], cos_q, sin_q, dq_tiles[i])
                out.append(d)
                gq = gq + dw
            dq_ref[...] = jnp.concatenate(out, axis=1)
            gq_ref[...] += gq
            gs_ref[...] += gsink
            dkk_s[...] = jnp.concatenate(dkk, axis=1)
            dvv_s[...] = jnp.concatenate(dvv, axis=1)

        invf = invf_ref[...]
        ang_p = pp_ref[...] * invf
        cos_p, sin_p = jnp.cos(ang_p), jnp.sin(ang_p)
        dk_prev = ck_s[...] + dkk_s[0:BLOCK, :]
        kp = kp_ref[...]
        gk = jnp.zeros((1, 128), F32)
        out = []
        for i in range(2):
            d, dw = _prep_tile_bwd(kp[:, 128 * i:128 * i + 128], kw_ref[...], cos_p, sin_p,
                                   dk_prev[:, 128 * i:128 * i + 128])
            out.append(d)
            gk = gk + dw
        dk_ref[...] = jnp.concatenate(out, axis=1)
        dv_ref[...] = cv_s[...] + dvv_s[0:BLOCK, :]
        gk_ref[...] += gk
        ck_s[...] = dkk_s[BLOCK:2 * BLOCK, :]
        cv_s[...] = dvv_s[BLOCK:2 * BLOCK, :]

    qblk = lambda n: (jnp.minimum(n, last), 0)
    kblk = lambda n: (jnp.maximum(n - 1, 0), 0)
    vec = pl.BlockSpec((1, 128), lambda n: (0, 0))
    return pl.pallas_call(
        body, name="attn_bwd", grid=(nb + 1,),
        in_specs=_attn_specs(nb) + [pl.BlockSpec((BLOCK, ATTN_W), qblk)],
        out_specs=[pl.BlockSpec((BLOCK, ATTN_W), qblk), pl.BlockSpec((BLOCK, KV_W), kblk),
                   pl.BlockSpec((BLOCK, KV_W), kblk), pl.BlockSpec((BLOCK, ATTN_W), qblk), vec, vec, vec],
        out_shape=[jax.ShapeDtypeStruct((t, ATTN_W), F32), jax.ShapeDtypeStruct((t, KV_W), F32),
                   jax.ShapeDtypeStruct((t, KV_W), F32), jax.ShapeDtypeStruct((t, ATTN_W), F32),
                   jax.ShapeDtypeStruct((1, 128), F32), jax.ShapeDtypeStruct((1, 128), F32),
                   jax.ShapeDtypeStruct((1, 128), F32)],
        scratch_shapes=[pltpu.VMEM((2 * BLOCK, KV_W), F32), pltpu.VMEM((2 * BLOCK, KV_W), F32),
                        pltpu.VMEM((BLOCK, KV_W), F32), pltpu.VMEM((BLOCK, KV_W), F32)],
        compiler_params=_params(("arbitrary",)),
    )(proj, proj, proj, proj, proj, proj, proj, posf, posf, invf, qw, kw, sinks, doa)


def _cmul(ar, ai, br, bi):
    return ar * br - ai * bi, ar * bi + ai * br


def _zoh(a_re, a_im, delta):
    e = jnp.exp(a_re * delta)
    lr, li = e * jnp.cos(a_im * delta), e * jnp.sin(a_im * delta)
    inv = 1.0 / (a_re * a_re + a_im * a_im)
    fr, fi = _cmul(lr - 1.0, li, a_re * inv, -a_im * inv)
    return lr, li, fr, fi


def _ssm_prep(a_re, a_im, log_step, b_re, b_im, seg_len):
    n_sq = int(round(math.log2(seg_len)))
    assert 2 ** n_sq == seg_len

    def body(ar_ref, ai_ref, ls_ref, arx_ref, aix_ref, br_ref, bi_ref, lr_ref, li_ref, pr_ref, pi_ref, bbr_ref, bbi_ref):
        delta = jnp.exp(ls_ref[...])
        lr, li, _, _ = _zoh(ar_ref[...], ai_ref[...], delta)
        lr_ref[...] = lr
        li_ref[...] = li
        pr, pi = lr, li
        for _ in range(n_sq):
            pr, pi = _cmul(pr, pi, pr, pi)
        pr_ref[...] = pr
        pi_ref[...] = pi
        _, _, fr, fi = _zoh(arx_ref[...], aix_ref[...], delta)
        bbr, bbi = _cmul(fr, fi, br_ref[...], bi_ref[...])
        bbr_ref[...] = bbr
        bbi_ref[...] = bbi

    gp = jax.ShapeDtypeStruct((SSM_G, SSM_P), F32)
    gx = jax.ShapeDtypeStruct((SSM_G, SSM_P * SSM_H), F32)
    return pl.pallas_call(body, name="ssm_prep", out_shape=[gp, gp, gp, gp, gx, gx])(
        a_re, a_im, log_step.reshape(SSM_G, 1), jnp.repeat(a_re, SSM_H, axis=1), jnp.repeat(a_im, SSM_H, axis=1),
        b_re.reshape(SSM_G, SSM_P * SSM_H), b_im.reshape(SSM_G, SSM_P * SSM_H))


def _ssm_param_grads(a_re, a_im, log_step, b_re, b_im, dlam_re, dlam_im, dbb_re, dbb_im):
    def body(ar_ref, ai_ref, ls_ref, arx_ref, aix_ref, br_ref, bi_ref, dlr_ref, dli_ref, dbr_ref, dbi_ref,
             gar_ref, gai_ref, gls_ref, gbr_ref, gbi_ref):
        delta = jnp.exp(ls_ref[...])
        ar, ai = ar_ref[...], ai_ref[...]
        lr, li, fr, fi = _zoh(ar, ai, delta)
        _, _, frx, fix = _zoh(arx_ref[...], aix_ref[...], delta)
        dbr, dbi = dbr_ref[...], dbi_ref[...]
        br, bi = br_ref[...], bi_ref[...]
        gbr, gbi = _cmul(frx, -fix, dbr, dbi)
        gbr_ref[...] = gbr
        gbi_ref[...] = gbi
        tr, ti = _cmul(br, -bi, dbr, dbi)
        row = lax.broadcasted_iota(jnp.int32, (SSM_P * SSM_H, SSM_P), 0)
        col = lax.broadcasted_iota(jnp.int32, (SSM_P * SSM_H, SSM_P), 1)
        fold = (row // SSM_H == col).astype(F32)
        dfr = jnp.dot(tr, fold, precision=lax.Precision.HIGHEST, preferred_element_type=F32)
        dfi = jnp.dot(ti, fold, precision=lax.Precision.HIGHEST, preferred_element_type=F32)
        inv = 1.0 / (ar * ar + ai * ai)
        ilr, ili = ar * inv, -ai * inv
        t1r, t1i = _cmul(dfr, dfi, ilr, -ili)
        dlbr, dlbi = dlr_ref[...] + t1r, dli_ref[...] + t1i
        qr, qi = _cmul(fr, fi, ilr, ili)
        t2r, t2i = _cmul(dfr, dfi, qr, -qi)
        glr, gli = -t2r, -t2i
        dzr, dzi = _cmul(dlbr, dlbi, lr, -li)
        gar_ref[...] = glr + dzr * delta
        gai_ref[...] = gli + dzi * delta
        gls_ref[...] = jnp.sum(dzr * ar + dzi * ai, axis=1, keepdims=True) * delta

    gp = jax.ShapeDtypeStruct((SSM_G, SSM_P), F32)
    gx = jax.ShapeDtypeStruct((SSM_G, SSM_P * SSM_H), F32)
    return pl.pallas_call(body, name="ssm_param_grads",
                          out_shape=[gp, gp, jax.ShapeDtypeStruct((SSM_G, 1), F32), gx, gx])(
        a_re, a_im, log_step.reshape(SSM_G, 1), jnp.repeat(a_re, SSM_H, axis=1), jnp.repeat(a_im, SSM_H, axis=1),
        b_re.reshape(SSM_G, SSM_P * SSM_H), b_im.reshape(SSM_G, SSM_P * SSM_H), dlam_re, dlam_im, dbb_re, dbb_im)


def _block_diag_in(bb):
    eye = jnp.eye(16, dtype=F32)
    w = bb.reshape(SSM_GB, 16, SSM_P, SSM_H).transpose(0, 1, 3, 2)
    return (w[:, :, :, None, :] * eye[None, :, None, :, None]).reshape(SSM_GB, SSM_CH, SSM_ST)


def _block_diag_out(c):
    eye = jnp.eye(16, dtype=F32)
    w = c.reshape(SSM_GB, 16, SSM_H, SSM_P).transpose(0, 1, 3, 2)
    return (w[:, :, :, None, :] * eye[None, :, None, :, None]).reshape(SSM_GB, SSM_ST, SSM_CH)


def _diag_blocks(full):
    w = full.reshape(SSM_GB, 16, SSM_H, 16, SSM_P)
    idx = jnp.arange(16)
    return w[:, idx, :, idx, :].transpose(1, 0, 2, 3).reshape(SSM_G, SSM_H, SSM_P)


def _permute_rows(a):
    t, c = a.shape
    return a.reshape(N_SEG, t // N_SEG, c).transpose(1, 0, 2).reshape(t, c)


def _unpermute_rows(a):
    t, c = a.shape
    return a.reshape(t // N_SEG, N_SEG, c).transpose(1, 0, 2).reshape(t, c)


def _scan_fwd(src_ref, dst_ref, lam_r_ref, lam_i_ref, init_ref, final_ref, steps):
    for k in range(SSM_ST // SCAN_LW):
        re = pl.ds(k * SCAN_LW, SCAN_LW)
        im = pl.ds(SSM_ST + k * SCAN_LW, SCAN_LW)
        lr, li = lam_r_ref[:, re], lam_i_ref[:, re]

        def step(i, carry, re=re, im=im, lr=lr, li=li):
            hr, hi = carry
            rows = pl.ds(pl.multiple_of(i * 8, 8), 8)
            nr = lr * hr - li * hi + src_ref[rows, re]
            ni = lr * hi + li * hr + src_ref[rows, im]
            if dst_ref is not None:
                dst_ref[rows, re] = nr
                dst_ref[rows, im] = ni
            return nr, ni

        hr, hi = lax.fori_loop(0, steps, step, (init_ref[:, re], init_ref[:, im]), unroll=4)
        final_ref[:, re] = hr
        final_ref[:, im] = hi


def _ssm_specs(t):
    col = lambda g: (0, g)
    gb3 = lambda g: (g, 0, 0)
    return dict(
        rows=pl.BlockSpec((t, SSM_CH), col),
        lam=pl.BlockSpec((None, N_SEG, SSM_ST), gb3),
        w_in=pl.BlockSpec((None, SSM_CH, 2 * SSM_ST), gb3),
        w_out=pl.BlockSpec((None, 2 * SSM_ST, SSM_CH), gb3),
        vec=pl.BlockSpec((1, SSM_CH), col),
    )


def _segment_states(x_ref, pw_r_ref, pw_i_ref, out_ref, reverse):
    re, im = pl.ds(0, SSM_ST), pl.ds(SSM_ST, SSM_ST)
    pr, pi = pw_r_ref[0:1, :], pw_i_ref[0:1, :]
    first = N_SEG - 1 if reverse else 0
    out_ref[first:first + 1, :] = jnp.zeros((1, 2 * SSM_ST), F32)
    order = range(N_SEG - 1, 0, -1) if reverse else range(N_SEG - 1)
    for s in order:
        d = s - 1 if reverse else s + 1
        hr, hi = out_ref[s:s + 1, re], out_ref[s:s + 1, im]
        if reverse:
            nr, ni = pr * hr + pi * hi, pr * hi - pi * hr
        else:
            nr, ni = pr * hr - pi * hi, pr * hi + pi * hr
        out_ref[d:d + 1, re] = nr + x_ref[s:s + 1, re]
        out_ref[d:d + 1, im] = ni + x_ref[s:s + 1, im]


def _ssm_fwd(up, lam_r, lam_i, pw_r, pw_i, w_in, w_out, d_skip):
    t = up.shape[0]
    nch = t // SCAN_ROWS
    steps = SCAN_ROWS // N_SEG
    sp = _ssm_specs(t)

    def body(u_ref, lr_ref, li_ref, pr_ref, pi_ref, wi_ref, wo_ref, d_ref, y_ref, hc_ref, bu_s, car_s, seg_s):
        def load_bu(j):
            rows = pl.ds(pl.multiple_of(j * SCAN_ROWS, SCAN_ROWS), SCAN_ROWS)
            bu_s[...] = _dot(u_ref[rows, :].astype(BF16), wi_ref[...])

        car_s[...] = jnp.zeros_like(car_s)

        def chunk1(j, c):
            load_bu(j)
            _scan_fwd(bu_s, None, lr_ref, li_ref, car_s, car_s, steps)
            return c

        lax.fori_loop(0, nch, chunk1, 0)
        _segment_states(car_s, pr_ref, pi_ref, seg_s, reverse=False)
        car_s[...] = seg_s[...]

        def chunk2(j, c):
            load_bu(j)
            hc_ref[j] = car_s[...]
            _scan_fwd(bu_s, bu_s, lr_ref, li_ref, car_s, car_s, steps)
            rows = pl.ds(pl.multiple_of(j * SCAN_ROWS, SCAN_ROWS), SCAN_ROWS)
            y_ref[rows, :] = _dot(bu_s[...].astype(BF16), wo_ref[...]) + d_ref[...] * u_ref[rows, :]
            return c

        lax.fori_loop(0, nch, chunk2, 0)

    return pl.pallas_call(
        body, name="ssm_fwd", grid=(SSM_GB,),
        in_specs=[sp["rows"], sp["lam"], sp["lam"], sp["lam"], sp["lam"], sp["w_in"], sp["w_out"], sp["vec"]],
        out_specs=[sp["rows"], pl.BlockSpec((None, nch, N_SEG, 2 * SSM_ST), lambda g: (g, 0, 0, 0))],
        out_shape=[jax.ShapeDtypeStruct((t, SSM_W), F32), jax.ShapeDtypeStruct((SSM_GB, nch, N_SEG, 2 * SSM_ST), F32)],
        scratch_shapes=[pltpu.VMEM((SCAN_ROWS, 2 * SSM_ST), F32), pltpu.VMEM((N_SEG, 2 * SSM_ST), F32),
                        pltpu.VMEM((N_SEG, 2 * SSM_ST), F32)],
        compiler_params=_params(("parallel",)),
    )(up, lam_r, lam_i, pw_r, pw_i, w_in, w_out, d_skip)


def _ssm_bwd(up, dyp, hc, lam_r, lam_i, pw_r, pw_i, w_in, w_out, d_skip):
    t = up.shape[0]
    nch = t // SCAN_ROWS
    steps = SCAN_ROWS // N_SEG
    sp = _ssm_specs(t)

    def body(u_ref, dy_ref, hc_ref, lr_ref, li_ref, pr_ref, pi_ref, wi_ref, wo_ref, d_ref,
             du_ref, gwi_ref, gwo_ref, glam_ref, gd_ref, bu_s, h_s, e_s, car_s, seg_s, acc_s):
        def chunk_rows(j):
            return pl.ds(pl.multiple_of(j * SCAN_ROWS, SCAN_ROWS), SCAN_ROWS)

        def load_e(j):
            e_s[...] = _dot_nt(dy_ref[chunk_rows(j), :].astype(BF16), wo_ref[...])

        def scan_rev(j, accumulate):
            for k in range(SSM_ST // SCAN_LW):
                re = pl.ds(k * SCAN_LW, SCAN_LW)
                im = pl.ds(SSM_ST + k * SCAN_LW, SCAN_LW)
                lr, li = lr_ref[:, re], li_ref[:, re]

                def step(ii, carry, re=re, im=im, lr=lr, li=li):
                    i = steps - 1 - ii
                    rows = pl.ds(pl.multiple_of(i * 8, 8), 8)
                    if accumulate:
                        gr, gi, ar, ai = carry
                    else:
                        gr, gi = carry
                    nr = lr * gr + li * gi + e_s[rows, re]
                    ni = lr * gi - li * gr + e_s[rows, im]
                    if not accumulate:
                        return nr, ni
                    e_s[rows, re] = nr
                    e_s[rows, im] = ni
                    pr_, pi_ = h_s[rows, re], h_s[rows, im]
                    return nr, ni, ar + nr * pr_ + ni * pi_, ai + ni * pr_ - nr * pi_

                init = (car_s[:, re], car_s[:, im])
                if accumulate:
                    init = init + (acc_s[:, re], acc_s[:, im])
                out = lax.fori_loop(0, steps, step, init, unroll=4)
                car_s[:, re] = out[0]
                car_s[:, im] = out[1]
                if accumulate:
                    acc_s[:, re] = out[2]
                    acc_s[:, im] = out[3]

        car_s[...] = jnp.zeros_like(car_s)

        def pass1(jj, c):
            load_e(nch - 1 - jj)
            scan_rev(nch - 1 - jj, False)
            return c

        lax.fori_loop(0, nch, pass1, 0)
        _segment_states(car_s, pr_ref, pi_ref, seg_s, reverse=True)
        car_s[...] = seg_s[...]
        acc_s[...] = jnp.zeros_like(acc_s)
        gwi_ref[...] = jnp.zeros_like(gwi_ref)
        gwo_ref[...] = jnp.zeros_like(gwo_ref)
        gd_ref[...] = jnp.zeros_like(gd_ref)

        def pass2(jj, c):
            j = nch - 1 - jj
            rows = chunk_rows(j)
            u = u_ref[rows, :]
            dy = dy_ref[rows, :]
            u_b, dy_b = u.astype(BF16), dy.astype(BF16)
            bu_s[...] = _dot(u_b, wi_ref[...])
            h_s[0:N_SEG, :] = hc_ref[j]
            seg_s[...] = hc_ref[j]
            _scan_fwd(bu_s, h_s.at[pl.ds(N_SEG, SCAN_ROWS), :], lr_ref, li_ref, seg_s, seg_s, steps)
            load_e(j)
            scan_rev(j, True)
            g_b = e_s[...].astype(BF16)
            du_ref[rows, :] = _dot_nt(g_b, wi_ref[...]) + d_ref[...] * dy
            gwi_ref[...] += _dot_tn(u_b, g_b)
            gwo_ref[...] += _dot_tn(dy_b, h_s[pl.ds(N_SEG, SCAN_ROWS), :].astype(BF16))
            gd_ref[...] += jnp.sum(dy * u, axis=0, keepdims=True)
            return c

        lax.fori_loop(0, nch, pass2, 0)
        glam_ref[...] = jnp.sum(acc_s[...], axis=0, keepdims=True)

    mat = pl.BlockSpec((None, SSM_CH, 2 * SSM_ST), lambda g: (g, 0, 0))
    return pl.pallas_call(
        body, name="ssm_bwd", grid=(SSM_GB,),
        in_specs=[sp["rows"], sp["rows"], pl.BlockSpec((None, nch, N_SEG, 2 * SSM_ST), lambda g: (g, 0, 0, 0)),
                  sp["lam"], sp["lam"], sp["lam"], sp["lam"], sp["w_in"], sp["w_out"], sp["vec"]],
        out_specs=[sp["rows"], mat, mat, pl.BlockSpec((None, 1, 2 * SSM_ST), lambda g: (g, 0, 0)), sp["vec"]],
        out_shape=[jax.ShapeDtypeStruct((t, SSM_W), F32), jax.ShapeDtypeStruct((SSM_GB, SSM_CH, 2 * SSM_ST), F32),
                   jax.ShapeDtypeStruct((SSM_GB, SSM_CH, 2 * SSM_ST), F32),
                   jax.ShapeDtypeStruct((SSM_GB, 1, 2 * SSM_ST), F32), jax.ShapeDtypeStruct((1, SSM_W), F32)],
        scratch_shapes=[pltpu.VMEM((SCAN_ROWS, 2 * SSM_ST), F32), pltpu.VMEM((SCAN_ROWS + N_SEG, 2 * SSM_ST), F32),
                        pltpu.VMEM((SCAN_ROWS, 2 * SSM_ST), F32), pltpu.VMEM((N_SEG, 2 * SSM_ST), F32),
                        pltpu.VMEM((N_SEG, 2 * SSM_ST), F32), pltpu.VMEM((N_SEG, 2 * SSM_ST), F32)],
        compiler_params=_params(("parallel",)),
    )(up, dyp, hc, lam_r, lam_i, pw_r, pw_i, w_in, w_out, d_skip)


def _glu_fwd(y, zs, w_glu, b_glu):
    t = y.shape[0]
    tm = 512

    def body(y_ref, z_ref, w_ref, b_ref, o_ref, yg_ref):
        yg = _gelu(y_ref[...])
        yg_b = yg.astype(BF16)
        a = _dot(yg_b, w_ref[...]) + b_ref[...]
        o_ref[...] = yg * _sigmoid(a) * _silu(z_ref[...])
        yg_ref[...] = yg_b

    row = pl.BlockSpec((tm, SSM_W), lambda i: (i, 0))
    return pl.pallas_call(
        body, name="glu_fwd", grid=(t // tm,),
        in_specs=[row, row, pl.BlockSpec((SSM_W, SSM_W), lambda i: (0, 0)), pl.BlockSpec((1, SSM_W), lambda i: (0, 0))],
        out_specs=[row, row],
        out_shape=[jax.ShapeDtypeStruct((t, SSM_W), F32), jax.ShapeDtypeStruct((t, SSM_W), BF16)],
        compiler_params=_params(("parallel",)),
    )(y, zs, w_glu, b_glu)


def _glu_bwd(y, zs, dos, w_glu, b_glu):
    t = y.shape[0]
    tm = 512

    def body(y_ref, z_ref, do_ref, w_ref, b_ref, dy_ref, dz_ref, da_ref, gb_ref):
        @pl.when(pl.program_id(0) == 0)
        def _():
            gb_ref[...] = jnp.zeros_like(gb_ref)

        yv, z, do = y_ref[...], z_ref[...], do_ref[...]
        yg = _gelu(yv)
        sg = _sigmoid(_dot(yg.astype(BF16), w_ref[...]) + b_ref[...])
        dy2 = do * _silu(z)
        dz_ref[...] = do * yg * sg * _dsilu(z)
        da = dy2 * yg * sg * (1.0 - sg)
        da_b = da.astype(BF16)
        da_ref[...] = da_b
        gb_ref[...] += jnp.sum(da, axis=0, keepdims=True)
        dyg = dy2 * sg + _dot_nt(da_b, w_ref[...])
        dy_ref[...] = dyg * _dgelu(yv)

    row = pl.BlockSpec((tm, SSM_W), lambda i: (i, 0))
    vec = pl.BlockSpec((1, SSM_W), lambda i: (0, 0))
    return pl.pallas_call(
        body, name="glu_bwd", grid=(t // tm,),
        in_specs=[row, row, row, pl.BlockSpec((SSM_W, SSM_W), lambda i: (0, 0)), vec],
        out_specs=[row, row, row, vec],
        out_shape=[jax.ShapeDtypeStruct((t, SSM_W), F32), jax.ShapeDtypeStruct((t, SSM_W), F32),
                   jax.ShapeDtypeStruct((t, SSM_W), BF16), jax.ShapeDtypeStruct((1, SSM_W), F32)],
        compiler_params=_params(("arbitrary",)),
    )(y, zs, dos, w_glu, b_glu)


def _rms(o):
    return lax.rsqrt(jnp.mean(o * o, axis=1, keepdims=True) + NORM_EPS)


def _outproj(oa, os_, aw, sw, w_out, x, target):
    t = x.shape[0]
    tm = 256

    def body(oa_ref, os_ref, aw_ref, sw_ref, w_ref, x_ref, t_ref, mg_ref, do_ref, ls_ref):
        @pl.when(pl.program_id(0) == 0)
        def _():
            ls_ref[...] = jnp.zeros_like(ls_ref)

        a, s = oa_ref[...], os_ref[...]
        merged = jnp.concatenate([a * _rms(a) * aw_ref[...], s * _rms(s) * sw_ref[...]], axis=1).astype(BF16)
        mg_ref[...] = merged
        err = x_ref[...] + _dot(merged, w_ref[...]) - t_ref[...]
        do_ref[...] = err * (1.0 / D_MODEL)
        ls_ref[...] += jnp.sum(err * err)

    half = pl.BlockSpec((tm, ATTN_W), lambda i: (i, 0))
    full = pl.BlockSpec((tm, D_MODEL), lambda i: (i, 0))
    vec = pl.BlockSpec((1, ATTN_W), lambda i: (0, 0))
    return pl.pallas_call(
        body, name="outproj", grid=(t // tm,),
        in_specs=[half, half, vec, vec, pl.BlockSpec((D_MODEL, D_MODEL), lambda i: (0, 0)), full, full],
        out_specs=[full, full, pl.BlockSpec((8, 128), lambda i: (0, 0))],
        out_shape=[jax.ShapeDtypeStruct((t, D_MODEL), BF16), jax.ShapeDtypeStruct((t, D_MODEL), F32),
                   jax.ShapeDtypeStruct((8, 128), F32)],
        compiler_params=_params(("arbitrary",)),
    )(oa, os_, aw, sw, w_out, x, target)


def _outproj_bwd(dout, oa, os_, aw, sw, w_out):
    t = dout.shape[0]
    tm = 256

    def norm_bwd(o, w, dm):
        r = _rms(o)
        yh = o * r
        gh = dm * w
        return r * (gh - yh * jnp.mean(gh * yh, axis=1, keepdims=True)), jnp.sum(dm * yh, axis=0, keepdims=True)

    def body(do_ref, oa_ref, os_ref, aw_ref, sw_ref, w_ref, da_ref, ds_ref, ga_ref, gs_ref):
        @pl.when(pl.program_id(0) == 0)
        def _():
            ga_ref[...] = jnp.zeros_like(ga_ref)
            gs_ref[...] = jnp.zeros_like(gs_ref)

        dm = _dot_nt(do_ref[...].astype(BF16), w_ref[...])
        da, ga = norm_bwd(oa_ref[...], aw_ref[...], dm[:, :ATTN_W])
        ds, gs = norm_bwd(os_ref[...], sw_ref[...], dm[:, ATTN_W:])
        da_ref[...] = da
        ds_ref[...] = ds
        ga_ref[...] += ga
        gs_ref[...] += gs

    half = pl.BlockSpec((tm, ATTN_W), lambda i: (i, 0))
    full = pl.BlockSpec((tm, D_MODEL), lambda i: (i, 0))
    vec = pl.BlockSpec((1, ATTN_W), lambda i: (0, 0))
    return pl.pallas_call(
        body, name="outproj_bwd", grid=(t // tm,),
        in_specs=[full, half, half, vec, vec, pl.BlockSpec((D_MODEL, D_MODEL), lambda i: (0, 0))],
        out_specs=[half, half, vec, vec],
        out_shape=[jax.ShapeDtypeStruct((t, ATTN_W), F32), jax.ShapeDtypeStruct((t, ATTN_W), F32),
                   jax.ShapeDtypeStruct((1, ATTN_W), F32), jax.ShapeDtypeStruct((1, ATTN_W), F32)],
        compiler_params=_params(("arbitrary",)),
    )(dout, oa, os_, aw, sw, w_out)


def _inproj_bwd(dproj, w_slabs, x, norm_w, dout):
    t = x.shape[0]
    tm = 512

    def body(dp_ref, w_ref, x_ref, nw_ref, do_ref, gx_ref, gw_ref, acc_ref):
        i, j = pl.program_id(0), pl.program_id(1)

        @pl.when((i == 0) & (j == 0))
        def _():
            gw_ref[...] = jnp.zeros_like(gw_ref)

        @pl.when(j == 0)
        def _():
            acc_ref[...] = jnp.zeros_like(acc_ref)

        acc_ref[...] += _dot_nt(dp_ref[...], w_ref[...])

        @pl.when(j == 3)
        def _():
            xv = x_ref[...]
            r = lax.rsqrt(jnp.mean(xv * xv, axis=1, keepdims=True) + NORM_EPS)
            yh = xv * r
            dh = acc_ref[...]
            gh = dh * nw_ref[...]
            gx_ref[...] = do_ref[...] + r * (gh - yh * jnp.mean(gh * yh, axis=1, keepdims=True))
            gw_ref[...] += jnp.sum(dh * yh, axis=0, keepdims=True)

    full = pl.BlockSpec((tm, D_MODEL), lambda i, j: (i, 0))
    vec = pl.BlockSpec((1, D_MODEL), lambda i, j: (0, 0))
    return pl.pallas_call(
        body, name="inproj_bwd", grid=(t // tm, 4),
        in_specs=[pl.BlockSpec((tm, SHARD_W), lambda i, j: (i, j)),
                  pl.BlockSpec((None, D_MODEL, SHARD_W), lambda i, j: (j, 0, 0)), full, vec, full],
        out_specs=[full, vec],
        out_shape=[jax.ShapeDtypeStruct((t, D_MODEL), F32), jax.ShapeDtypeStruct((1, D_MODEL), F32)],
        scratch_shapes=[pltpu.VMEM((tm, D_MODEL), F32)],
        compiler_params=_params(("arbitrary", "arbitrary")),
    )(dproj, w_slabs, x, norm_w.reshape(1, D_MODEL), dout)


def _adamw(w, g, m, v, *, rows, name):
    r, c = w.shape

    def body(w_ref, g_ref, m_ref, v_ref, d_ref, nm_ref, nv_ref):
        gv = g_ref[...]
        nm = ADAM_B1 * m_ref[...] + (1.0 - ADAM_B1) * gv
        nv = ADAM_B2 * v_ref[...] + (1.0 - ADAM_B2) * (gv * gv)
        m_hat = nm / (1.0 - ADAM_B1 ** ADAM_STEP)
        v_hat = nv / (1.0 - ADAM_B2 ** ADAM_STEP)
        d_ref[...] = -ADAM_LR * (m_hat / (jnp.sqrt(v_hat) + ADAM_EPS) + ADAM_WD * w_ref[...])
        nm_ref[...] = nm
        nv_ref[...] = nv

    blk = pl.BlockSpec((rows, c), lambda i: (i, 0))
    shp = jax.ShapeDtypeStruct((r, c), F32)
    return pl.pallas_call(body, name=name, grid=(r // rows,), in_specs=[blk] * 4, out_specs=[blk] * 3,
                          out_shape=[shp] * 3, compiler_params=_params(("parallel",)))(w, g, m, v)


def _remote(src, dst, ssem, rsem, dev):
    return pltpu.make_async_remote_copy(src_ref=src, dst_ref=dst, send_sem=ssem, recv_sem=rsem, device_id=dev,
                                        device_id_type=pl.DeviceIdType.MESH)


def _mesh_pos():
    return lax.axis_index("x"), lax.axis_index("y"), lax.axis_index("c")


def _other_chips(x, y):
    return [(1 - x, y), (x, 1 - y), (1 - x, 1 - y)]


def _gather_weights(shards):
    nt = len(shards)
    halves = [s.shape[0] // 2 for s in shards]

    def body(*refs):
        sh, full = refs[:nt], refs[nt:2 * nt]
        ssem, rsem, lsem = refs[2 * nt:]
        x, y, c = _mesh_pos()
        me = 2 * x + y
        sib = (x, y, 1 - c)
        local = [pltpu.make_async_copy(sh[i], full[i].at[me], lsem.at[i]) for i in range(nt)]
        for cp in local:
            cp.start()

        def half(i, which):
            return pl.ds(pl.multiple_of(which * halves[i], 8), halves[i])

        sends = []
        for k, (px, py) in enumerate(_other_chips(x, y)):
            for i in range(nt):
                cp = _remote(sh[i].at[half(i, c)], full[i].at[me, half(i, c)], ssem.at[k * nt + i],
                             rsem.at[k * nt + i], (px, py, c))
                cp.start()
                sends.append(cp)
        for k, (px, py) in enumerate(_other_chips(x, y)):
            slot = 2 * px + py
            for i in range(nt):
                landed = full[i].at[slot, half(i, c)]
                _remote(landed, landed, ssem.at[k * nt + i], rsem.at[k * nt + i], sib).wait_recv()
                cp = _remote(landed, landed, ssem.at[(3 + k) * nt + i], rsem.at[(3 + k) * nt + i], sib)
                cp.start()
                sends.append(cp)
        for k, (px, py) in enumerate(_other_chips(x, y)):
            slot = 2 * px + py
            for i in range(nt):
                passed = full[i].at[slot, half(i, 1 - c)]
                _remote(passed, passed, ssem.at[(3 + k) * nt + i], rsem.at[(3 + k) * nt + i], sib).wait_recv()
        for cp in sends:
            cp.wait_send()
        for cp in local:
            cp.wait()

    return pl.pallas_call(
        body, name="gather_weights", in_specs=[ANY] * nt, out_specs=[ANY] * nt,
        out_shape=[jax.ShapeDtypeStruct((4,) + s.shape, s.dtype) for s in shards],
        scratch_shapes=[pltpu.SemaphoreType.DMA((6 * nt,)), pltpu.SemaphoreType.DMA((6 * nt,)),
                        pltpu.SemaphoreType.DMA((nt,))],
    )(*shards)


def _swap_with_sibling(arrays, name):
    nt = len(arrays)

    def body(*refs):
        src, dst = refs[:nt], refs[nt:2 * nt]
        ssem, rsem = refs[2 * nt:]
        x, y, c = _mesh_pos()
        cps = [_remote(src[i], dst[i], ssem.at[i], rsem.at[i], (x, y, 1 - c)) for i in range(nt)]
        for cp in cps:
            cp.start()
        for cp in cps:
            cp.wait_recv()
        for cp in cps:
            cp.wait_send()

    return pl.pallas_call(
        body, name=name, in_specs=[ANY] * nt, out_specs=[ANY] * nt,
        out_shape=[jax.ShapeDtypeStruct(a.shape, a.dtype) for a in arrays],
        scratch_shapes=[pltpu.SemaphoreType.DMA((nt,)), pltpu.SemaphoreType.DMA((nt,))],
    )(*arrays)


def _scatter_to_chips(arrays):
    nt = len(arrays)

    def body(*refs):
        src, dst = refs[:nt], refs[nt:2 * nt]
        ssem, rsem, lsem = refs[2 * nt:]
        x, y, c = _mesh_pos()
        me = 2 * x + y
        local = [pltpu.make_async_copy(src[i].at[me], dst[i].at[me], lsem.at[i]) for i in range(nt)]
        for cp in local:
            cp.start()
        cps = []
        for k, (px, py) in enumerate(_other_chips(x, y)):
            for i in range(nt):
                cp = _remote(src[i].at[2 * px + py], dst[i].at[me], ssem.at[k * nt + i], rsem.at[k * nt + i],
                             (px, py, c))
                cp.start()
                cps.append(cp)
        for k, (px, py) in enumerate(_other_chips(x, y)):
            for i in range(nt):
                slot = dst[i].at[2 * px + py]
                _remote(slot, slot, ssem.at[k * nt + i], rsem.at[k * nt + i], (px, py, c)).wait_recv()
        for cp in cps:
            cp.wait_send()
        for cp in local:
            cp.wait()

    return pl.pallas_call(
        body, name="scatter_to_chips", in_specs=[ANY] * nt, out_specs=[ANY] * nt,
        out_shape=[jax.ShapeDtypeStruct(a.shape, a.dtype) for a in arrays],
        scratch_shapes=[pltpu.SemaphoreType.DMA((3 * nt,)), pltpu.SemaphoreType.DMA((3 * nt,)),
                        pltpu.SemaphoreType.DMA((nt,))],
    )(*arrays)


def _flips():
    return [(dx, dy, dc) for dx in (0, 1) for dy in (0, 1) for dc in (0, 1) if (dx, dy, dc) != (0, 0, 0)]


def _exchange_slices(src, scatter, name):
    def body(src_ref, dst_ref, ssem, rsem, lsem):
        x, y, c = _mesh_pos()
        me = 4 * x + 2 * y + c
        local = pltpu.make_async_copy(src_ref.at[me], dst_ref.at[me], lsem)
        local.start()
        cps = []
        for k, (dx, dy, dc) in enumerate(_flips()):
            px, py, pc = jnp.bitwise_xor(x, dx), jnp.bitwise_xor(y, dy), jnp.bitwise_xor(c, dc)
            peer = 4 * px + 2 * py + pc
            cp = _remote(src_ref.at[peer if scatter else me], dst_ref.at[me], ssem.at[k], rsem.at[k], (px, py, pc))
            cp.start()
            cps.append((cp, peer))
        for k, (cp, peer) in enumerate(cps):
            slot = dst_ref.at[peer]
            _remote(slot, slot, ssem.at[k], rsem.at[k], (x, y, c)).wait_recv()
        for cp, _ in cps:
            cp.wait_send()
        local.wait()

    return pl.pallas_call(
        body, name=name, in_specs=[ANY], out_specs=ANY, out_shape=jax.ShapeDtypeStruct(src.shape, src.dtype),
        scratch_shapes=[pltpu.SemaphoreType.DMA((7,)), pltpu.SemaphoreType.DMA((7,)), pltpu.SemaphoreType.DMA],
    )(src)


def _add_halves(g, recv, c_idx, *, rows, name):
    _, _, hr, cols = g.shape

    def body(c_ref, g_ref, r_ref, o_ref):
        o_ref[...] = (g_ref[...] + r_ref[...].astype(F32)).astype(BF16)

    return pl.pallas_call(
        body, name=name,
        grid_spec=pltpu.PrefetchScalarGridSpec(
            num_scalar_prefetch=1, grid=(4, hr // rows),
            in_specs=[pl.BlockSpec((None, None, rows, cols), lambda j, i, c: (j, c[0], i, 0)),
                      pl.BlockSpec((None, rows, cols), lambda j, i, c: (j, i, 0))],
            out_specs=pl.BlockSpec((None, rows, cols), lambda j, i, c: (j, i, 0))),
        out_shape=jax.ShapeDtypeStruct((4, hr, cols), BF16),
        compiler_params=_params(("parallel", "parallel")),
    )(c_idx, g, recv)


def _sum_slots(slots, *, rows, name):
    n, r, cols = slots.shape

    def body(s_ref, o_ref):
        acc = s_ref[0].astype(F32)
        for k in range(1, n):
            acc = acc + s_ref[k].astype(F32)
        o_ref[...] = acc

    return pl.pallas_call(
        body, name=name, grid=(r // rows,),
        in_specs=[pl.BlockSpec((n, rows, cols), lambda i: (0, i, 0))],
        out_specs=pl.BlockSpec((rows, cols), lambda i: (i, 0)),
        out_shape=jax.ShapeDtypeStruct((r, cols), F32),
        compiler_params=_params(("parallel",)),
    )(slots)


def _pack_small(d):
    flat = jnp.concatenate([d[n].astype(F32).reshape(-1) for n, _ in SMALL])
    return jnp.pad(flat, (0, PACK_TOTAL - flat.shape[0])).reshape(8 * PACK_ROWS, 128)


def _unpack_small(p):
    flat = p.reshape(-1)
    out, off = {}, 0
    for n, size in SMALL:
        out[n] = flat[off:off + size].reshape(SMALL_SHAPES[n])
        off += size
    return out


def kernel(x, positions, norm_w, w_in, q_norm_w, k_norm_w, sinks, a_re, a_im, log_step, b_re, b_im, c_re, c_im, d_skip, w_glu, b_glu, attn_out_norm_w, ssm_out_norm_w, w_out, loss_target, m_norm_w, m_w_in, m_q_norm_w, m_k_norm_w, m_sinks, m_a_re, m_a_im, m_log_step, m_b_re, m_b_im, m_c_re, m_c_im, m_d_skip, m_w_glu, m_b_glu, m_attn_out_norm_w, m_ssm_out_norm_w, m_w_out, v_norm_w, v_w_in, v_q_norm_w, v_k_norm_w, v_sinks, v_a_re, v_a_im, v_log_step, v_b_re, v_b_im, v_c_re, v_c_im, v_d_skip, v_w_glu, v_b_glu, v_attn_out_norm_w, v_ssm_out_norm_w, v_w_out):
    small_w = dict(norm_w=norm_w, q_norm_w=q_norm_w, k_norm_w=k_norm_w, sinks=sinks, a_re=a_re, a_im=a_im,
                   log_step=log_step, b_re=b_re, b_im=b_im, c_re=c_re, c_im=c_im, d_skip=d_skip, b_glu=b_glu,
                   attn_out_norm_w=attn_out_norm_w, ssm_out_norm_w=ssm_out_norm_w)
    small_m = dict(norm_w=m_norm_w, q_norm_w=m_q_norm_w, k_norm_w=m_k_norm_w, sinks=m_sinks, a_re=m_a_re, a_im=m_a_im,
                   log_step=m_log_step, b_re=m_b_re, b_im=m_b_im, c_re=m_c_re, c_im=m_c_im, d_skip=m_d_skip,
                   b_glu=m_b_glu, attn_out_norm_w=m_attn_out_norm_w, ssm_out_norm_w=m_ssm_out_norm_w)
    small_v = dict(norm_w=v_norm_w, q_norm_w=v_q_norm_w, k_norm_w=v_k_norm_w, sinks=v_sinks, a_re=v_a_re, a_im=v_a_im,
                   log_step=v_log_step, b_re=v_b_re, b_im=v_b_im, c_re=v_c_re, c_im=v_c_im, d_skip=v_d_skip,
                   b_glu=v_b_glu, attn_out_norm_w=v_attn_out_norm_w, ssm_out_norm_w=v_ssm_out_norm_w)
    c_idx = lax.axis_index("c").astype(jnp.int32).reshape(1)

    xs = x[0]
    tgt = loss_target[0]
    t = xs.shape[0]
    posf = positions[0].astype(F32).reshape(t, 1)

    w_in_all, w_glu_all, w_out_all = _gather_weights([w_in.astype(BF16), w_glu.astype(BF16), w_out.astype(BF16)])
    w_glu_b = w_glu_all.reshape(SSM_W, SSM_W)
    w_out_b = w_out_all.reshape(D_MODEL, D_MODEL)

    proj, hn = _inproj(xs, norm_w, w_in_all)
    inv_freq = ROPE_THETA ** (-jnp.arange(0, HEAD_DIM, 2, dtype=F32) / HEAD_DIM)
    invf = jnp.tile(inv_freq, 4).reshape(1, 128)
    qw = jnp.tile(q_norm_w, 2).reshape(1, 128)
    kw = jnp.tile(k_norm_w, 2).reshape(1, 128)
    sink_row = sinks.reshape(1, N_HEADS)
    oa = _attn_fwd(proj, posf, invf, qw, kw, sink_row)

    lam_r, lam_i, pw_r, pw_i, bb_r, bb_i = _ssm_prep(a_re, a_im, log_step, b_re, b_im, t // N_SEG)
    rows8 = lambda a: jnp.broadcast_to(a.reshape(SSM_GB, 1, SSM_ST), (SSM_GB, N_SEG, SSM_ST))
    lam_r8, lam_i8, pw_r8, pw_i8 = rows8(lam_r), rows8(lam_i), rows8(pw_r), rows8(pw_i)
    ssm_w_in = jnp.concatenate([_block_diag_in(bb_r.reshape(SSM_G, SSM_P, SSM_H)),
                                _block_diag_in(bb_i.reshape(SSM_G, SSM_P, SSM_H))], axis=2).astype(BF16)
    ssm_w_out = jnp.concatenate([_block_diag_out(c_re), _block_diag_out(-c_im)], axis=1).astype(BF16)
    d_row = d_skip.reshape(1, SSM_W)
    uz = _permute_rows(proj[:, 2560:])
    up, zsp = uz[:, :SSM_W], uz[:, SSM_W:]
    yp, hc = _ssm_fwd(up, lam_r8, lam_i8, pw_r8, pw_i8, ssm_w_in, ssm_w_out, d_row)
    b_glu_row = b_glu.reshape(1, SSM_W)
    osp, ygp = _glu_fwd(yp, zsp, w_glu_b, b_glu_row)
    os_ = _unpermute_rows(osp)
    aw = attn_out_norm_w.reshape(1, ATTN_W)
    sw = ssm_out_norm_w.reshape(1, SSM_W)
    merged, dout, sq_err = _outproj(oa, os_, aw, sw, w_out_b, xs, tgt)
    loss = lax.psum(0.5 * sq_err[0, 0] / D_MODEL, MESH_AXES)

    doa, dos, g_aw, g_sw = _outproj_bwd(dout, oa, os_, aw, sw, w_out_b)
    dout_b = dout.astype(BF16)
    g_w_out = _matmul(merged.T, dout_b, tm=512, tn=1024, tk=1024, name="grad_w_out")
    dyp, dzsp, dap, g_b_glu = _glu_bwd(yp, zsp, _permute_rows(dos), w_glu_b, b_glu_row)
    g_w_glu = _matmul(ygp.T, dap, tm=512, tn=1024, tk=1024, name="grad_w_glu")
    dup, g_wi, g_wo, g_lam, g_d = _ssm_bwd(up, dyp, hc, lam_r8, lam_i8, pw_r8, pw_i8, ssm_w_in, ssm_w_out, d_row)
    dq, dk, dv, dza, g_qw, g_kw, g_sink = _attn_bwd(proj, posf, invf, qw, kw, sink_row, doa)
    duz = _unpermute_rows(jnp.concatenate([dup, dzsp], axis=1))
    dproj = jnp.concatenate([dq, dk, dv, dza, duz], axis=1).astype(BF16)
    grad_x, g_nw = _inproj_bwd(dproj, w_in_all, xs, norm_w, dout)
    g_w_in = _matmul(hn.T, dproj, tm=512, tn=SHARD_W, tk=1024, name="grad_w_in")

    g_bb_r = _diag_blocks(g_wi[:, :, :SSM_ST]).transpose(0, 2, 1).reshape(SSM_G, SSM_P * SSM_H)
    g_bb_i = _diag_blocks(g_wi[:, :, SSM_ST:]).transpose(0, 2, 1).reshape(SSM_G, SSM_P * SSM_H)
    g_a_re, g_a_im, g_ls, g_b_re, g_b_im = _ssm_param_grads(
        a_re, a_im, log_step, b_re, b_im, g_lam[:, 0, :SSM_ST].reshape(SSM_G, SSM_P),
        g_lam[:, 0, SSM_ST:].reshape(SSM_G, SSM_P), g_bb_r, g_bb_i)
    small_g = dict(
        norm_w=g_nw, q_norm_w=g_qw[0, :64] + g_qw[0, 64:], k_norm_w=g_kw[0, :64] + g_kw[0, 64:],
        sinks=g_sink[0, :N_HEADS], a_re=g_a_re, a_im=g_a_im, log_step=g_ls, b_re=g_b_re, b_im=g_b_im,
        c_re=_diag_blocks(g_wo[:, :, :SSM_ST]), c_im=-_diag_blocks(g_wo[:, :, SSM_ST:]), d_skip=g_d,
        b_glu=g_b_glu, attn_out_norm_w=g_aw, ssm_out_norm_w=g_sw)

    parts = [g_w_in.reshape(D_MODEL, 4, SHARD_W).transpose(1, 0, 2).reshape(4, 2, D_MODEL // 2, SHARD_W),
             g_w_glu.reshape(4, 2, 128, SSM_W), g_w_out.reshape(4, 2, 256, D_MODEL)]
    other = [lax.dynamic_index_in_dim(p, 1 - c_idx[0], axis=1, keepdims=False).astype(BF16) for p in parts]
    from_sib = _swap_with_sibling(other, "pair_swap")
    pair = [_add_halves(p, r, c_idx, rows=128, name=f"pair_sum_{i}") for i, (p, r) in enumerate(zip(parts, from_sib))]
    slots = _scatter_to_chips(pair)
    mine = [_sum_slots(s, rows=128, name=f"chip_sum_{i}") for i, s in enumerate(slots)]
    theirs = _swap_with_sibling(mine, "half_swap")
    big_g = []
    for lo_hi in zip(mine, theirs):
        both = jnp.stack(lo_hi)
        big_g.append(jnp.where(c_idx[0] == 0, both, both[::-1]).reshape(-1, both.shape[-1]))
    packed = _pack_small(small_g).reshape(8, PACK_ROWS, 128)
    summed = _sum_slots(_exchange_slices(packed, True, "small_scatter"), rows=PACK_ROWS, name="small_sum")
    mine_only = jnp.broadcast_to(summed[None], (8, PACK_ROWS, 128))
    small_red = _exchange_slices(mine_only, False, "small_gather").reshape(8 * PACK_ROWS, 128)

    g_in_sh, g_glu_sh, g_out_sh = big_g
    upd = [_adamw(w_in, g_in_sh, m_w_in, v_w_in, rows=256, name="adamw_w_in"),
           _adamw(w_glu, g_glu_sh, m_w_glu, v_w_glu, rows=256, name="adamw_w_glu"),
           _adamw(w_out, g_out_sh, m_w_out, v_w_out, rows=256, name="adamw_w_out")]
    sd, sm, sv = _adamw(_pack_small(small_w), small_red, _pack_small(small_m), _pack_small(small_v),
                        rows=8 * PACK_ROWS, name="adamw_small")
    grads = _unpack_small(small_red)
    grads.update(w_in=g_in_sh, w_glu=g_glu_sh, w_out=g_out_sh)
    deltas, new_m, new_v = _unpack_small(sd), _unpack_small(sm), _unpack_small(sv)
    for n, (d, m_, v_) in zip(("w_in", "w_glu", "w_out"), upd):
        deltas[n], new_m[n], new_v[n] = d, m_, v_
    order = ["norm_w", "w_in", "q_norm_w", "k_norm_w", "sinks", "a_re", "a_im", "log_step", "b_re", "b_im", "c_re",
             "c_im", "d_skip", "w_glu", "b_glu", "attn_out_norm_w", "ssm_out_norm_w", "w_out"]
    return (loss, grad_x[None], *[grads[n] for n in order], *[deltas[n] for n in order],
            *[new_m[n] for n in order], *[new_v[n] for n in order])
```

```python
import math

import jax
import jax.numpy as jnp
from jax import lax
from jax.experimental import pallas as pl
from jax.experimental.pallas import tpu as pltpu

F32 = jnp.float32
BF16 = jnp.bfloat16

D_MODEL = 2048
ATTN_W = 1024
SSM_W = 1024
HEAD_DIM = 64
N_HEADS = 16
N_KV_HEADS = 4
KV_W = 256
BLOCK = 128
IN_W = 4608
SHARD_W = IN_W // 4
ROPE_THETA = 10000.0
SSM_H = 16
SSM_G = 64
SSM_P = 64
NORM_EPS = 1e-6
ADAM_LR = 0.001
ADAM_B1 = 0.9
ADAM_B2 = 0.999
ADAM_EPS = 1e-08
ADAM_WD = 0.01
ADAM_STEP = 10

N_SEG = 8
SSM_GB = 4
SSM_CH = 256
SSM_ST = 1024
SCAN_ROWS = 256
SCAN_LW = 512
VMEM_LIMIT = 56 * 1024 * 1024
MESH_AXES = ("x", "y", "c")
ANY = pl.BlockSpec(memory_space=pl.ANY)

SMALL_3D = ("b_re", "b_im", "c_re", "c_im")
SMALL_FLAT = ("norm_w", "q_norm_w", "k_norm_w", "sinks", "a_re", "a_im", "log_step", "d_skip", "b_glu",
              "attn_out_norm_w", "ssm_out_norm_w")
SMALL = SMALL_3D + SMALL_FLAT
SMALL_SHAPES = {"norm_w": (2048,), "q_norm_w": (64,), "k_norm_w": (64,), "sinks": (16,), "a_re": (64, 64),
                "a_im": (64, 64), "log_step": (64,), "b_re": (64, 64, 16), "b_im": (64, 64, 16),
                "c_re": (64, 16, 64), "c_im": (64, 16, 64), "d_skip": (1024,), "b_glu": (1024,),
                "attn_out_norm_w": (1024,), "ssm_out_norm_w": (1024,)}
PACK_ROWS = 272
FLAT_ROWS = 120


def _params(sem=None):
    return pltpu.CompilerParams(dimension_semantics=sem, vmem_limit_bytes=VMEM_LIMIT)


def _dot(a, b):
    return jnp.dot(a, b, preferred_element_type=F32)


def _dot_nt(a, b):
    return lax.dot_general(a, b, (((1,), (1,)), ((), ())), preferred_element_type=F32)


def _dot_tn(a, b):
    return lax.dot_general(a, b, (((0,), (0,)), ((), ())), preferred_element_type=F32)


def _sigmoid(x):
    return 1.0 / (1.0 + jnp.exp(-x))


def _silu(x):
    return x * _sigmoid(x)


def _dsilu(x):
    s = _sigmoid(x)
    return s * (1.0 + x * (1.0 - s))


_GELU_C = math.sqrt(2.0 / math.pi)


def _gelu(x):
    return 0.5 * x * (1.0 + jnp.tanh(_GELU_C * (x + 0.044715 * x * x * x)))


def _dgelu(x):
    t = jnp.tanh(_GELU_C * (x + 0.044715 * x * x * x))
    return 0.5 * (1.0 + t) + 0.5 * x * (1.0 - t * t) * _GELU_C * (1.0 + 3.0 * 0.044715 * x * x)


def _matmul_tn(a, b, *, tm, tn, tk, name, slabs=False):
    k, m = a.shape
    _, n = b.shape
    nk = k // tk

    def body(a_ref, b_ref, o_ref, ob_ref, acc_ref):
        kk = pl.program_id(2)

        @pl.when(kk == 0)
        def _():
            acc_ref[...] = jnp.zeros_like(acc_ref)

        acc_ref[...] += _dot_tn(a_ref[...], b_ref[...])

        @pl.when(kk == nk - 1)
        def _():
            o_ref[...] = acc_ref[...]
            ob_ref[...] = acc_ref[...].astype(BF16)

    if slabs:
        out_spec = pl.BlockSpec((None, tm, tn), lambda i, j, kk: (j, i, 0))
        shape = (n // tn, m, tn)
    else:
        out_spec = pl.BlockSpec((tm, tn), lambda i, j, kk: (i, j))
        shape = (m, n)
    return pl.pallas_call(
        body, name=name, grid=(m // tm, n // tn, nk),
        in_specs=[pl.BlockSpec((tk, tm), lambda i, j, kk: (kk, i)), pl.BlockSpec((tk, tn), lambda i, j, kk: (kk, j))],
        out_specs=[out_spec, out_spec],
        out_shape=[jax.ShapeDtypeStruct(shape, F32), jax.ShapeDtypeStruct(shape, BF16)],
        scratch_shapes=[pltpu.VMEM((tm, tn), F32)],
        compiler_params=_params(("parallel", "parallel", "arbitrary")),
    )(a, b)


def _inproj(x, norm_w, w_slabs):
    t = x.shape[0]
    tm = 512

    def body(x_ref, nw_ref, w_ref, proj_ref, hn_ref, hn_s):
        @pl.when(pl.program_id(1) == 0)
        def _():
            xv = x_ref[...]
            r = lax.rsqrt(jnp.mean(xv * xv, axis=1, keepdims=True) + NORM_EPS)
            hn = (xv * r * nw_ref[...]).astype(BF16)
            hn_s[...] = hn
            hn_ref[...] = hn

        proj_ref[...] = _dot(hn_s[...], w_ref[...])

    return pl.pallas_call(
        body, name="inproj", grid=(t // tm, 4),
        in_specs=[pl.BlockSpec((tm, D_MODEL), lambda i, j: (i, 0)), pl.BlockSpec((1, D_MODEL), lambda i, j: (0, 0)),
                  pl.BlockSpec((None, D_MODEL, SHARD_W), lambda i, j: (j, 0, 0))],
        out_specs=[pl.BlockSpec((tm, SHARD_W), lambda i, j: (i, j)), pl.BlockSpec((tm, D_MODEL), lambda i, j: (i, 0))],
        out_shape=[jax.ShapeDtypeStruct((t, IN_W), F32), jax.ShapeDtypeStruct((t, D_MODEL), BF16)],
        scratch_shapes=[pltpu.VMEM((tm, D_MODEL), BF16)],
        compiler_params=_params(("parallel", "arbitrary")),
    )(x, norm_w.reshape(1, D_MODEL), w_slabs)


def _lane128():
    return lax.broadcasted_iota(jnp.int32, (1, 128), 1)


def _head_sums(v):
    lo = _lane128() < 64
    s_lo = jnp.sum(jnp.where(lo, v, 0.0), axis=1, keepdims=True)
    s_hi = jnp.sum(jnp.where(lo, 0.0, v), axis=1, keepdims=True)
    return jnp.where(lo, s_lo, s_hi)


def _rot_half(t):
    first = (_lane128() % 64) < 32
    return jnp.where(first, -pltpu.roll(t, 96, 1), pltpu.roll(t, 32, 1))


def _prep_tile(t, w, cos, sin):
    r = lax.rsqrt(_head_sums(t * t) * (1.0 / HEAD_DIM) + NORM_EPS)
    tn = t * r * w
    return tn * cos + _rot_half(tn) * sin


def _prep_tile_bwd(t, w, cos, sin, g):
    r = lax.rsqrt(_head_sums(t * t) * (1.0 / HEAD_DIM) + NORM_EPS)
    d_tn = g * cos - _rot_half(g * sin)
    th = t * r
    dw = jnp.sum(d_tn * th, axis=0, keepdims=True)
    gh = d_tn * w
    m = _head_sums(gh * th) * (1.0 / HEAD_DIM)
    return r * (gh - th * m), dw


def _band_mask(n):
    qi = lax.broadcasted_iota(jnp.int32, (BLOCK, 2 * BLOCK), 0) + BLOCK
    ki = lax.broadcasted_iota(jnp.int32, (BLOCK, 2 * BLOCK), 1)
    rel = qi - ki
    return (rel >= 0) & (rel < BLOCK) & ((n > 0) | (ki >= BLOCK))


def _half_select(tile, half):
    lo = _lane128() < 64
    return jnp.where(lo if half == 0 else jnp.logical_not(lo), tile, 0.0)


def _stack_group(tiles, kv_half):
    rows = []
    for t in tiles:
        for half in range(2):
            piece = _half_select(t, half)
            rows.append(piece if half == kv_half else pltpu.roll(piece, 64, 1))
    return jnp.concatenate(rows, axis=0)


def _unstack_group(stacked, kv_half):
    tiles = []
    for i in range(2):
        acc = None
        for half in range(2):
            piece = _half_select(stacked[BLOCK * (2 * i + half):BLOCK * (2 * i + half + 1)], kv_half)
            piece = piece if half == kv_half else pltpu.roll(piece, 64, 1)
            acc = piece if acc is None else acc + piece
        tiles.append(acc)
    return tiles


def _group_sinks(sink_ref, g):
    return jnp.concatenate([jnp.broadcast_to(sink_ref[:, 4 * g + j:4 * g + j + 1], (BLOCK, 1)) for j in range(4)], axis=0)


def _attn_specs(nb):
    last = nb - 1
    qi = lambda n: (jnp.minimum(n, last), 0)
    prev = lambda n: jnp.maximum(n - 1, 0)
    cur = lambda n: jnp.minimum(n, last)
    specs = [
        pl.BlockSpec((BLOCK, ATTN_W), qi),
        pl.BlockSpec((BLOCK, KV_W), lambda n: (cur(n), 4)),
        pl.BlockSpec((BLOCK, KV_W), lambda n: (prev(n), 4)),
        pl.BlockSpec((BLOCK, KV_W), lambda n: (cur(n), 5)),
        pl.BlockSpec((BLOCK, KV_W), lambda n: (prev(n), 5)),
        pl.BlockSpec((BLOCK, 512), lambda n: (cur(n), 3)),
        pl.BlockSpec((BLOCK, 512), lambda n: (cur(n), 4)),
        pl.BlockSpec((BLOCK, 1), lambda n: (cur(n), 0)),
        pl.BlockSpec((BLOCK, 1), lambda n: (prev(n), 0)),
        pl.BlockSpec((1, 128), lambda n: (0, 0)),
        pl.BlockSpec((1, 128), lambda n: (0, 0)),
        pl.BlockSpec((1, 128), lambda n: (0, 0)),
        pl.BlockSpec((1, N_HEADS), lambda n: (0, 0)),
    ]
    return specs


def _attn_common(n, q_ref, kc_ref, kp_ref, vc_ref, vp_ref, pq_ref, pp_ref, invf_ref, qw_ref, kw_ref):
    invf = invf_ref[...]
    ang_q = pq_ref[...] * invf
    ang_p = pp_ref[...] * invf
    cos_q, sin_q = jnp.cos(ang_q), jnp.sin(ang_q)
    cos_k = jnp.concatenate([jnp.cos(ang_p), cos_q], axis=0)
    sin_k = jnp.concatenate([jnp.sin(ang_p), sin_q], axis=0)
    k_raw = jnp.concatenate([kp_ref[...], kc_ref[...]], axis=0)
    vv = jnp.concatenate([vp_ref[...], vc_ref[...]], axis=0).astype(BF16)
    kk = [_prep_tile(k_raw[:, 128 * i:128 * i + 128], kw_ref[...], cos_k, sin_k).astype(BF16) for i in range(2)]
    vt = [vv[:, 128 * i:128 * i + 128] for i in range(2)]
    qv = q_ref[...]
    qt = [_prep_tile(qv[:, 128 * i:128 * i + 128], qw_ref[...], cos_q, sin_q) for i in range(8)]
    return cos_q, sin_q, cos_k, sin_k, k_raw, kk, vt, qt


def _head_softmax(q_sel, kk_t, sink, valid):
    s = _dot_nt(q_sel, kk_t) * (1.0 / math.sqrt(HEAD_DIM))
    s = jnp.where(valid, s, -1e30)
    m = jnp.maximum(jnp.max(s, axis=1, keepdims=True), sink)
    p = jnp.exp(s - m)
    es = jnp.exp(sink - m)
    inv = 1.0 / (jnp.sum(p, axis=1, keepdims=True) + es)
    return p * inv, es * inv


def _attn_fwd(proj, posf, invf, qw, kw, sinks):
    t = proj.shape[0]
    nb = t // BLOCK

    def body(q_ref, kc_ref, kp_ref, vc_ref, vp_ref, za0_ref, za1_ref, pq_ref, pp_ref, invf_ref, qw_ref, kw_ref,
             sink_ref, o_ref):
        n = pl.program_id(0)
        _, _, _, _, _, kk, vt, qt = _attn_common(n, q_ref, kc_ref, kp_ref, vc_ref, vp_ref, pq_ref, pp_ref, invf_ref,
                                                 qw_ref, kw_ref)
        valid = jnp.concatenate([_band_mask(n)] * 4, axis=0)
        tiles = []
        for g in range(N_KV_HEADS):
            q4 = _stack_group(qt[2 * g:2 * g + 2], g % 2).astype(BF16)
            p, _ = _head_softmax(q4, kk[g // 2], _group_sinks(sink_ref, g), valid)
            tiles += _unstack_group(_dot(p.astype(BF16), vt[g // 2]), g % 2)
        za = jnp.concatenate([za0_ref[...], za1_ref[...]], axis=1)
        o_ref[...] = jnp.concatenate(tiles, axis=1) * _silu(za)

    return pl.pallas_call(
        body, name="attn_fwd", grid=(nb,), in_specs=_attn_specs(nb),
        out_specs=pl.BlockSpec((BLOCK, ATTN_W), lambda n: (n, 0)),
        out_shape=jax.ShapeDtypeStruct((t, ATTN_W), F32),
        compiler_params=_params(("parallel",)),
    )(proj, proj, proj, proj, proj, proj, proj, posf, posf, invf, qw, kw, sinks)


def _attn_bwd(proj, posf, invf, qw, kw, sinks, doa):
    t = proj.shape[0]
    nb = t // BLOCK
    last = nb - 1

    def body(q_ref, kc_ref, kp_ref, vc_ref, vp_ref, za0_ref, za1_ref, pq_ref, pp_ref, invf_ref, qw_ref, kw_ref,
             sink_ref, doa_ref, dq_ref, dk_ref, dv_ref, dza_ref, gq_ref, gk_ref, gs_ref, dkk_s, dvv_s, ck_s, cv_s):
        n = pl.program_id(0)

        @pl.when(n == 0)
        def _():
            gq_ref[...] = jnp.zeros_like(gq_ref)
            gk_ref[...] = jnp.zeros_like(gk_ref)
            gs_ref[...] = jnp.zeros_like(gs_ref)
            ck_s[...] = jnp.zeros_like(ck_s)
            cv_s[...] = jnp.zeros_like(cv_s)

        @pl.when(n == nb)
        def _():
            dkk_s[...] = jnp.zeros_like(dkk_s)
            dvv_s[...] = jnp.zeros_like(dvv_s)

        @pl.when(n < nb)
        def _():
            cos_q, sin_q, _, _, _, kk, vt, qt = _attn_common(n, q_ref, kc_ref, kp_ref, vc_ref, vp_ref, pq_ref, pp_ref,
                                                             invf_ref, qw_ref, kw_ref)
            valid = jnp.concatenate([_band_mask(n)] * 4, axis=0)
            za = jnp.concatenate([za0_ref[...], za1_ref[...]], axis=1)
            doa_v = doa_ref[...]
            do_full = doa_v * _silu(za)
            o_tiles, dq_tiles = [], []
            dkk = [jnp.zeros((2 * BLOCK, 128), F32) for _ in range(2)]
            dvv = [jnp.zeros((2 * BLOCK, 128), F32) for _ in range(2)]
            gsink = jnp.zeros((1, 128), F32)
            lane = _lane128()
            for g in range(N_KV_HEADS):
                q_b = _stack_group(qt[2 * g:2 * g + 2], g % 2).astype(BF16)
                do_b = _stack_group([do_full[:, 128 * i:128 * i + 128] for i in (2 * g, 2 * g + 1)], g % 2).astype(BF16)
                p, psink = _head_softmax(q_b, kk[g // 2], _group_sinks(sink_ref, g), valid)
                p_b = p.astype(BF16)
                o_tiles += _unstack_group(_dot(p_b, vt[g // 2]), g % 2)
                dp = _dot_nt(do_b, vt[g // 2])
                delta = jnp.sum(p * dp, axis=1, keepdims=True)
                ds_b = (p * (dp - delta) * (1.0 / math.sqrt(HEAD_DIM))).astype(BF16)
                sd = psink * delta
                for j in range(4):
                    gsink = gsink + jnp.where(lane == 4 * g + j, -jnp.sum(sd[BLOCK * j:BLOCK * (j + 1)]), 0.0)
                dq_tiles += _unstack_group(_dot(ds_b, kk[g // 2]), g % 2)
                dkk[g // 2] = dkk[g // 2] + _dot_tn(ds_b, q_b)
                dvv[g // 2] = dvv[g // 2] + _dot_tn(p_b, do_b)
            dza_ref[...] = (doa_v * jnp.concatenate(o_tiles, axis=1) * _dsilu(za)).astype(BF16)
            qv = q_ref[...]
            gq = jnp.zeros((1, 128), F32)
            out = []
            for i in range(8):
                d, dw = _prep_tile_bwd(qv[:, 128 * i:128 * i + 128], qw_ref[...], cos_q, sin_q, dq_tiles[i])
                out.append(d)
                gq = gq + dw
            dq_ref[...] = jnp.concatenate(out, axis=1).astype(BF16)
            gq_ref[...] += gq
            gs_ref[...] += gsink
            dkk_s[...] = jnp.concatenate(dkk, axis=1)
            dvv_s[...] = jnp.concatenate(dvv, axis=1)

        invf = invf_ref[...]
        ang_p = pp_ref[...] * invf
        cos_p, sin_p = jnp.cos(ang_p), jnp.sin(ang_p)
        dk_prev = ck_s[...] + dkk_s[0:BLOCK, :]
        kp = kp_ref[...]
        gk = jnp.zeros((1, 128), F32)
        out = []
        for i in range(2):
            d, dw = _prep_tile_bwd(kp[:, 128 * i:128 * i + 128], kw_ref[...], cos_p, sin_p,
                                   dk_prev[:, 128 * i:128 * i + 128])
            out.append(d)
            gk = gk + dw
        dk_ref[...] = jnp.concatenate(out, axis=1).astype(BF16)
        dv_ref[...] = (cv_s[...] + dvv_s[0:BLOCK, :]).astype(BF16)
        gk_ref[...] += gk
        ck_s[...] = dkk_s[BLOCK:2 * BLOCK, :]
        cv_s[...] = dvv_s[BLOCK:2 * BLOCK, :]

    qblk = lambda n: (jnp.minimum(n, last), 0)
    kblk = lambda n: (jnp.maximum(n - 1, 0), 0)
    vec = pl.BlockSpec((1, 128), lambda n: (0, 0))
    return pl.pallas_call(
        body, name="attn_bwd", grid=(nb + 1,),
        in_specs=_attn_specs(nb) + [pl.BlockSpec((BLOCK, ATTN_W), qblk)],
        out_specs=[pl.BlockSpec((BLOCK, ATTN_W), qblk), pl.BlockSpec((BLOCK, KV_W), kblk),
                   pl.BlockSpec((BLOCK, KV_W), kblk), pl.BlockSpec((BLOCK, ATTN_W), qblk), vec, vec, vec],
        out_shape=[jax.ShapeDtypeStruct((t, ATTN_W), BF16), jax.ShapeDtypeStruct((t, KV_W), BF16),
                   jax.ShapeDtypeStruct((t, KV_W), BF16), jax.ShapeDtypeStruct((t, ATTN_W), BF16),
                   jax.ShapeDtypeStruct((1, 128), F32), jax.ShapeDtypeStruct((1, 128), F32),
                   jax.ShapeDtypeStruct((1, 128), F32)],
        scratch_shapes=[pltpu.VMEM((2 * BLOCK, KV_W), F32), pltpu.VMEM((2 * BLOCK, KV_W), F32),
                        pltpu.VMEM((BLOCK, KV_W), F32), pltpu.VMEM((BLOCK, KV_W), F32)],
        compiler_params=_params(("arbitrary",)),
    )(proj, proj, proj, proj, proj, proj, proj, posf, posf, invf, qw, kw, sinks, doa)


def _cmul(ar, ai, br, bi):
    return ar * br - ai * bi, ar * bi + ai * br


def _zoh(a_re, a_im, delta):
    e = jnp.exp(a_re * delta)
    lr, li = e * jnp.cos(a_im * delta), e * jnp.sin(a_im * delta)
    inv = 1.0 / (a_re * a_re + a_im * a_im)
    fr, fi = _cmul(lr - 1.0, li, a_re * inv, -a_im * inv)
    return lr, li, fr, fi


def _ssm_prep(a_re, a_im, log_step, b_re, b_im, seg_len):
    n_sq = int(round(math.log2(seg_len)))
    assert 2 ** n_sq == seg_len

    def body(ar_ref, ai_ref, ls_ref, arx_ref, aix_ref, br_ref, bi_ref, lr_ref, li_ref, pr_ref, pi_ref, bbr_ref, bbi_ref):
        delta = jnp.exp(ls_ref[...])
        lr, li, _, _ = _zoh(ar_ref[...], ai_ref[...], delta)
        lr_ref[...] = lr
        li_ref[...] = li
        pr, pi = lr, li
        for _ in range(n_sq):
            pr, pi = _cmul(pr, pi, pr, pi)
        pr_ref[...] = pr
        pi_ref[...] = pi
        _, _, fr, fi = _zoh(arx_ref[...], aix_ref[...], delta)
        bbr, bbi = _cmul(fr, fi, br_ref[...], bi_ref[...])
        bbr_ref[...] = bbr
        bbi_ref[...] = bbi

    gp = jax.ShapeDtypeStruct((SSM_G, SSM_P), F32)
    gx = jax.ShapeDtypeStruct((SSM_G, SSM_P * SSM_H), F32)
    return pl.pallas_call(body, name="ssm_prep", out_shape=[gp, gp, gp, gp, gx, gx])(
        a_re, a_im, log_step.reshape(SSM_G, 1), jnp.repeat(a_re, SSM_H, axis=1), jnp.repeat(a_im, SSM_H, axis=1),
        b_re.reshape(SSM_G, SSM_P * SSM_H), b_im.reshape(SSM_G, SSM_P * SSM_H))


def _ssm_param_grads(a_re, a_im, log_step, b_re, b_im, dlam_re, dlam_im, dbb_re, dbb_im):
    def body(ar_ref, ai_ref, ls_ref, arx_ref, aix_ref, br_ref, bi_ref, dlr_ref, dli_ref, dbr_ref, dbi_ref,
             gar_ref, gai_ref, gls_ref, gbr_ref, gbi_ref):
        delta = jnp.exp(ls_ref[...])
        ar, ai = ar_ref[...], ai_ref[...]
        lr, li, fr, fi = _zoh(ar, ai, delta)
        _, _, frx, fix = _zoh(arx_ref[...], aix_ref[...], delta)
        dbr, dbi = dbr_ref[...], dbi_ref[...]
        br, bi = br_ref[...], bi_ref[...]
        gbr, gbi = _cmul(frx, -fix, dbr, dbi)
        gbr_ref[...] = gbr
        gbi_ref[...] = gbi
        tr, ti = _cmul(br, -bi, dbr, dbi)
        row = lax.broadcasted_iota(jnp.int32, (SSM_P * SSM_H, SSM_P), 0)
        col = lax.broadcasted_iota(jnp.int32, (SSM_P * SSM_H, SSM_P), 1)
        fold = (row // SSM_H == col).astype(F32)
        dfr = jnp.dot(tr, fold, precision=lax.Precision.HIGHEST, preferred_element_type=F32)
        dfi = jnp.dot(ti, fold, precision=lax.Precision.HIGHEST, preferred_element_type=F32)
        inv = 1.0 / (ar * ar + ai * ai)
        ilr, ili = ar * inv, -ai * inv
        t1r, t1i = _cmul(dfr, dfi, ilr, -ili)
        dlbr, dlbi = dlr_ref[...] + t1r, dli_ref[...] + t1i
        qr, qi = _cmul(fr, fi, ilr, ili)
        t2r, t2i = _cmul(dfr, dfi, qr, -qi)
        glr, gli = -t2r, -t2i
        dzr, dzi = _cmul(dlbr, dlbi, lr, -li)
        gar_ref[...] = glr + dzr * delta
        gai_ref[...] = gli + dzi * delta
        gls_ref[...] = jnp.sum(dzr * ar + dzi * ai, axis=1, keepdims=True) * delta

    gp = jax.ShapeDtypeStruct((SSM_G, SSM_P), F32)
    gx = jax.ShapeDtypeStruct((SSM_G, SSM_P * SSM_H), F32)
    return pl.pallas_call(body, name="ssm_param_grads",
                          out_shape=[gp, gp, jax.ShapeDtypeStruct((SSM_G, 1), F32), gx, gx])(
        a_re, a_im, log_step.reshape(SSM_G, 1), jnp.repeat(a_re, SSM_H, axis=1), jnp.repeat(a_im, SSM_H, axis=1),
        b_re.reshape(SSM_G, SSM_P * SSM_H), b_im.reshape(SSM_G, SSM_P * SSM_H), dlam_re, dlam_im, dbb_re, dbb_im)


def _block_diag_in(bb):
    w = jnp.tile(bb.reshape(SSM_GB, SSM_ST, SSM_H), (1, 1, 16))
    row = lax.broadcasted_iota(jnp.int32, (1, SSM_ST, SSM_CH), 1) // SSM_P
    col = lax.broadcasted_iota(jnp.int32, (1, SSM_ST, SSM_CH), 2) // SSM_H
    return jnp.where(row == col, w, 0.0)


def _block_diag_out(c):
    w = jnp.tile(c.reshape(SSM_GB, SSM_CH, SSM_P), (1, 1, 16))
    row = lax.broadcasted_iota(jnp.int32, (1, SSM_CH, SSM_ST), 1) // SSM_H
    col = lax.broadcasted_iota(jnp.int32, (1, SSM_CH, SSM_ST), 2) // SSM_P
    return jnp.where(row == col, w, 0.0)


def _diag_blocks(full):
    w = full.reshape(SSM_GB, 16, SSM_H, 16, SSM_P)
    idx = jnp.arange(16)
    return w[:, idx, :, idx, :].transpose(1, 0, 2, 3).reshape(SSM_G, SSM_H, SSM_P)


def _permute_rows(a):
    t, c = a.shape
    return a.reshape(N_SEG, t // N_SEG, c).transpose(1, 0, 2).reshape(t, c)


def _unpermute_rows(a):
    t, c = a.shape
    return a.reshape(t // N_SEG, N_SEG, c).transpose(1, 0, 2).reshape(t, c)


def _scan_fwd(src_ref, dst_ref, lam_r_ref, lam_i_ref, init_ref, final_ref, steps):
    for k in range(SSM_ST // SCAN_LW):
        re = pl.ds(k * SCAN_LW, SCAN_LW)
        im = pl.ds(SSM_ST + k * SCAN_LW, SCAN_LW)
        lr, li = lam_r_ref[:, re], lam_i_ref[:, re]

        def step(i, carry, re=re, im=im, lr=lr, li=li):
            hr, hi = carry
            rows = pl.ds(pl.multiple_of(i * 8, 8), 8)
            nr = lr * hr - li * hi + src_ref[rows, re]
            ni = lr * hi + li * hr + src_ref[rows, im]
            if dst_ref is not None:
                dst_ref[rows, re] = nr
                dst_ref[rows, im] = ni
            return nr, ni

        hr, hi = lax.fori_loop(0, steps, step, (init_ref[:, re], init_ref[:, im]), unroll=4)
        final_ref[:, re] = hr
        final_ref[:, im] = hi


def _ssm_specs(t):
    col = lambda g: (0, g)
    gb3 = lambda g: (g, 0, 0)
    return dict(
        rows=pl.BlockSpec((t, SSM_CH), col),
        lam=pl.BlockSpec((None, N_SEG, SSM_ST), gb3),
        w_in=pl.BlockSpec((None, 2 * SSM_ST, SSM_CH), gb3),
        w_out=pl.BlockSpec((None, SSM_CH, 2 * SSM_ST), gb3),
        vec=pl.BlockSpec((1, SSM_CH), col),
    )


def _segment_states(x_ref, pw_r_ref, pw_i_ref, out_ref, reverse):
    re, im = pl.ds(0, SSM_ST), pl.ds(SSM_ST, SSM_ST)
    pr, pi = pw_r_ref[0:1, :], pw_i_ref[0:1, :]
    first = N_SEG - 1 if reverse else 0
    out_ref[first:first + 1, :] = jnp.zeros((1, 2 * SSM_ST), F32)
    order = range(N_SEG - 1, 0, -1) if reverse else range(N_SEG - 1)
    for s in order:
        d = s - 1 if reverse else s + 1
        hr, hi = out_ref[s:s + 1, re], out_ref[s:s + 1, im]
        if reverse:
            nr, ni = pr * hr + pi * hi, pr * hi - pi * hr
        else:
            nr, ni = pr * hr - pi * hi, pr * hi + pi * hr
        out_ref[d:d + 1, re] = nr + x_ref[s:s + 1, re]
        out_ref[d:d + 1, im] = ni + x_ref[s:s + 1, im]


def _ssm_fwd(up, lam_r, lam_i, pw_r, pw_i, w_in, w_out, d_skip):
    t = up.shape[0]
    nch = t // SCAN_ROWS
    steps = SCAN_ROWS // N_SEG
    sp = _ssm_specs(t)

    def body(u_ref, lr_ref, li_ref, pr_ref, pi_ref, wi_ref, wo_ref, d_ref, y_ref, hc_ref, bu_s, car_s, seg_s):
        def load_bu(j):
            rows = pl.ds(pl.multiple_of(j * SCAN_ROWS, SCAN_ROWS), SCAN_ROWS)
            bu_s[...] = _dot_nt(u_ref[rows, :].astype(BF16), wi_ref[...])

        car_s[...] = jnp.zeros_like(car_s)

        def chunk1(j, c):
            load_bu(j)
            _scan_fwd(bu_s, None, lr_ref, li_ref, car_s, car_s, steps)
            return c

        lax.fori_loop(0, nch, chunk1, 0)
        _segment_states(car_s, pr_ref, pi_ref, seg_s, reverse=False)
        car_s[...] = seg_s[...]

        def chunk2(j, c):
            load_bu(j)
            hc_ref[j] = car_s[...]
            _scan_fwd(bu_s, bu_s, lr_ref, li_ref, car_s, car_s, steps)
            rows = pl.ds(pl.multiple_of(j * SCAN_ROWS, SCAN_ROWS), SCAN_ROWS)
            y_ref[rows, :] = _dot_nt(bu_s[...].astype(BF16), wo_ref[...]) + d_ref[...] * u_ref[rows, :]
            return c

        lax.fori_loop(0, nch, chunk2, 0)

    return pl.pallas_call(
        body, name="ssm_fwd", grid=(SSM_GB,),
        in_specs=[sp["rows"], sp["lam"], sp["lam"], sp["lam"], sp["lam"], sp["w_in"], sp["w_out"], sp["vec"]],
        out_specs=[sp["rows"], pl.BlockSpec((None, nch, N_SEG, 2 * SSM_ST), lambda g: (g, 0, 0, 0))],
        out_shape=[jax.ShapeDtypeStruct((t, SSM_W), F32), jax.ShapeDtypeStruct((SSM_GB, nch, N_SEG, 2 * SSM_ST), F32)],
        scratch_shapes=[pltpu.VMEM((SCAN_ROWS, 2 * SSM_ST), F32), pltpu.VMEM((N_SEG, 2 * SSM_ST), F32),
                        pltpu.VMEM((N_SEG, 2 * SSM_ST), F32)],
        compiler_params=_params(("parallel",)),
    )(up, lam_r, lam_i, pw_r, pw_i, w_in, w_out, d_skip)


def _ssm_bwd(up, dyp, hc, lam_r, lam_i, pw_r, pw_i, w_in, w_out, d_skip):
    t = up.shape[0]
    nch = t // SCAN_ROWS
    steps = SCAN_ROWS // N_SEG
    sp = _ssm_specs(t)

    def body(u_ref, dy_ref, hc_ref, lr_ref, li_ref, pr_ref, pi_ref, wi_ref, wo_ref, d_ref,
             du_ref, gwi_ref, gwo_ref, glam_ref, gd_ref, bu_s, h_s, e_s, car_s, seg_s, acc_s):
        def chunk_rows(j):
            return pl.ds(pl.multiple_of(j * SCAN_ROWS, SCAN_ROWS), SCAN_ROWS)

        def load_e(j):
            e_s[...] = _dot(dy_ref[chunk_rows(j), :].astype(BF16), wo_ref[...])

        def scan_rev(j, accumulate):
            for k in range(SSM_ST // SCAN_LW):
                re = pl.ds(k * SCAN_LW, SCAN_LW)
                im = pl.ds(SSM_ST + k * SCAN_LW, SCAN_LW)
                lr, li = lr_ref[:, re], li_ref[:, re]

                def step(ii, carry, re=re, im=im, lr=lr, li=li):
                    i = steps - 1 - ii
                    rows = pl.ds(pl.multiple_of(i * 8, 8), 8)
                    if accumulate:
                        gr, gi, ar, ai = carry
                    else:
                        gr, gi = carry
                    nr = lr * gr + li * gi + e_s[rows, re]
                    ni = lr * gi - li * gr + e_s[rows, im]
                    if not accumulate:
                        return nr, ni
                    e_s[rows, re] = nr
                    e_s[rows, im] = ni
                    pr_, pi_ = h_s[rows, re], h_s[rows, im]
                    return nr, ni, ar + nr * pr_ + ni * pi_, ai + ni * pr_ - nr * pi_

                init = (car_s[:, re], car_s[:, im])
                if accumulate:
                    init = init + (acc_s[:, re], acc_s[:, im])
                out = lax.fori_loop(0, steps, step, init, unroll=4)
                car_s[:, re] = out[0]
                car_s[:, im] = out[1]
                if accumulate:
                    acc_s[:, re] = out[2]
                    acc_s[:, im] = out[3]

        car_s[...] = jnp.zeros_like(car_s)

        def pass1(jj, c):
            load_e(nch - 1 - jj)
            scan_rev(nch - 1 - jj, False)
            return c

        lax.fori_loop(0, nch, pass1, 0)
        _segment_states(car_s, pr_ref, pi_ref, seg_s, reverse=True)
        car_s[...] = seg_s[...]
        acc_s[...] = jnp.zeros_like(acc_s)
        gwi_ref[...] = jnp.zeros_like(gwi_ref)
        gwo_ref[...] = jnp.zeros_like(gwo_ref)
        gd_ref[...] = jnp.zeros_like(gd_ref)

        def pass2(jj, c):
            j = nch - 1 - jj
            rows = chunk_rows(j)
            u = u_ref[rows, :]
            dy = dy_ref[rows, :]
            u_b, dy_b = u.astype(BF16), dy.astype(BF16)
            bu_s[...] = _dot_nt(u_b, wi_ref[...])
            h_s[0:N_SEG, :] = hc_ref[j]
            seg_s[...] = hc_ref[j]
            _scan_fwd(bu_s, h_s.at[pl.ds(N_SEG, SCAN_ROWS), :], lr_ref, li_ref, seg_s, seg_s, steps)
            load_e(j)
            scan_rev(j, True)
            g_b = e_s[...].astype(BF16)
            du_ref[rows, :] = (_dot(g_b, wi_ref[...]) + d_ref[...] * dy).astype(du_ref.dtype)
            gwi_ref[...] += _dot_tn(u_b, g_b)
            gwo_ref[...] += _dot_tn(dy_b, h_s[pl.ds(N_SEG, SCAN_ROWS), :].astype(BF16))
            gd_ref[...] += jnp.sum(dy * u, axis=0, keepdims=True)
            return c

        lax.fori_loop(0, nch, pass2, 0)
        glam_ref[...] = jnp.sum(acc_s[...], axis=0, keepdims=True)

    mat = pl.BlockSpec((None, SSM_CH, 2 * SSM_ST), lambda g: (g, 0, 0))
    return pl.pallas_call(
        body, name="ssm_bwd", grid=(SSM_GB,),
        in_specs=[sp["rows"], sp["rows"], pl.BlockSpec((None, nch, N_SEG, 2 * SSM_ST), lambda g: (g, 0, 0, 0)),
                  sp["lam"], sp["lam"], sp["lam"], sp["lam"], sp["w_in"], sp["w_out"], sp["vec"]],
        out_specs=[sp["rows"], mat, mat, pl.BlockSpec((None, 1, 2 * SSM_ST), lambda g: (g, 0, 0)), sp["vec"]],
        out_shape=[jax.ShapeDtypeStruct((t, SSM_W), BF16), jax.ShapeDtypeStruct((SSM_GB, SSM_CH, 2 * SSM_ST), F32),
                   jax.ShapeDtypeStruct((SSM_GB, SSM_CH, 2 * SSM_ST), F32),
                   jax.ShapeDtypeStruct((SSM_GB, 1, 2 * SSM_ST), F32), jax.ShapeDtypeStruct((1, SSM_W), F32)],
        scratch_shapes=[pltpu.VMEM((SCAN_ROWS, 2 * SSM_ST), F32), pltpu.VMEM((SCAN_ROWS + N_SEG, 2 * SSM_ST), F32),
                        pltpu.VMEM((SCAN_ROWS, 2 * SSM_ST), F32), pltpu.VMEM((N_SEG, 2 * SSM_ST), F32),
                        pltpu.VMEM((N_SEG, 2 * SSM_ST), F32), pltpu.VMEM((N_SEG, 2 * SSM_ST), F32)],
        compiler_params=_params(("parallel",)),
    )(up, dyp, hc, lam_r, lam_i, pw_r, pw_i, w_in, w_out, d_skip)


def _glu_fwd(y, zs, w_glu, b_glu):
    t = y.shape[0]
    tm = 512

    def body(y_ref, z_ref, w_ref, b_ref, o_ref, yg_ref):
        yg = _gelu(y_ref[...])
        yg_b = yg.astype(BF16)
        a = _dot(yg_b, w_ref[...]) + b_ref[...]
        o_ref[...] = yg * _sigmoid(a) * _silu(z_ref[...])
        yg_ref[...] = yg_b

    row = pl.BlockSpec((tm, SSM_W), lambda i: (i, 0))
    return pl.pallas_call(
        body, name="glu_fwd", grid=(t // tm,),
        in_specs=[row, row, pl.BlockSpec((SSM_W, SSM_W), lambda i: (0, 0)), pl.BlockSpec((1, SSM_W), lambda i: (0, 0))],
        out_specs=[row, row],
        out_shape=[jax.ShapeDtypeStruct((t, SSM_W), F32), jax.ShapeDtypeStruct((t, SSM_W), BF16)],
        compiler_params=_params(("parallel",)),
    )(y, zs, w_glu, b_glu)


def _glu_bwd(y, zs, dos, w_glu, b_glu):
    t = y.shape[0]
    tm = 512

    def body(y_ref, z_ref, do_ref, w_ref, b_ref, dy_ref, dz_ref, da_ref, gb_ref):
        @pl.when(pl.program_id(0) == 0)
        def _():
            gb_ref[...] = jnp.zeros_like(gb_ref)

        yv, z, do = y_ref[...], z_ref[...], do_ref[...]
        yg = _gelu(yv)
        sg = _sigmoid(_dot(yg.astype(BF16), w_ref[...]) + b_ref[...])
        dy2 = do * _silu(z)
        dz_ref[...] = (do * yg * sg * _dsilu(z)).astype(BF16)
        da = dy2 * yg * sg * (1.0 - sg)
        da_b = da.astype(BF16)
        da_ref[...] = da_b
        gb_ref[...] += jnp.sum(da, axis=0, keepdims=True)
        dyg = dy2 * sg + _dot_nt(da_b, w_ref[...])
        dy_ref[...] = dyg * _dgelu(yv)

    row = pl.BlockSpec((tm, SSM_W), lambda i: (i, 0))
    vec = pl.BlockSpec((1, SSM_W), lambda i: (0, 0))
    return pl.pallas_call(
        body, name="glu_bwd", grid=(t // tm,),
        in_specs=[row, row, row, pl.BlockSpec((SSM_W, SSM_W), lambda i: (0, 0)), vec],
        out_specs=[row, row, row, vec],
        out_shape=[jax.ShapeDtypeStruct((t, SSM_W), F32), jax.ShapeDtypeStruct((t, SSM_W), BF16),
                   jax.ShapeDtypeStruct((t, SSM_W), BF16), jax.ShapeDtypeStruct((1, SSM_W), F32)],
        compiler_params=_params(("arbitrary",)),
    )(y, zs, dos, w_glu, b_glu)


def _rms(o):
    return lax.rsqrt(jnp.mean(o * o, axis=1, keepdims=True) + NORM_EPS)


def _outproj(oa, os_, aw, sw, w_out, x, target):
    t = x.shape[0]
    tm = 256

    def body(oa_ref, os_ref, aw_ref, sw_ref, w_ref, x_ref, t_ref, mg_ref, do_ref, ls_ref):
        @pl.when(pl.program_id(0) == 0)
        def _():
            ls_ref[...] = jnp.zeros_like(ls_ref)

        a, s = oa_ref[...], os_ref[...]
        merged = jnp.concatenate([a * _rms(a) * aw_ref[...], s * _rms(s) * sw_ref[...]], axis=1).astype(BF16)
        mg_ref[...] = merged
        err = x_ref[...] + _dot(merged, w_ref[...]) - t_ref[...]
        do_ref[...] = err * (1.0 / D_MODEL)
        ls_ref[...] += jnp.sum(err * err)

    half = pl.BlockSpec((tm, ATTN_W), lambda i: (i, 0))
    full = pl.BlockSpec((tm, D_MODEL), lambda i: (i, 0))
    vec = pl.BlockSpec((1, ATTN_W), lambda i: (0, 0))
    return pl.pallas_call(
        body, name="outproj", grid=(t // tm,),
        in_specs=[half, half, vec, vec, pl.BlockSpec((D_MODEL, D_MODEL), lambda i: (0, 0)), full, full],
        out_specs=[full, full, pl.BlockSpec((8, 128), lambda i: (0, 0))],
        out_shape=[jax.ShapeDtypeStruct((t, D_MODEL), BF16), jax.ShapeDtypeStruct((t, D_MODEL), F32),
                   jax.ShapeDtypeStruct((8, 128), F32)],
        compiler_params=_params(("arbitrary",)),
    )(oa, os_, aw, sw, w_out, x, target)


def _outproj_bwd(dout, oa, os_, aw, sw, w_out):
    t = dout.shape[0]
    tm = 256

    def norm_bwd(o, w, dm):
        r = _rms(o)
        yh = o * r
        gh = dm * w
        return r * (gh - yh * jnp.mean(gh * yh, axis=1, keepdims=True)), jnp.sum(dm * yh, axis=0, keepdims=True)

    def body(do_ref, oa_ref, os_ref, aw_ref, sw_ref, w_ref, da_ref, ds_ref, ga_ref, gs_ref):
        @pl.when(pl.program_id(0) == 0)
        def _():
            ga_ref[...] = jnp.zeros_like(ga_ref)
            gs_ref[...] = jnp.zeros_like(gs_ref)

        dm = _dot_nt(do_ref[...].astype(BF16), w_ref[...])
        da, ga = norm_bwd(oa_ref[...], aw_ref[...], dm[:, :ATTN_W])
        ds, gs = norm_bwd(os_ref[...], sw_ref[...], dm[:, ATTN_W:])
        da_ref[...] = da
        ds_ref[...] = ds
        ga_ref[...] += ga
        gs_ref[...] += gs

    half = pl.BlockSpec((tm, ATTN_W), lambda i: (i, 0))
    full = pl.BlockSpec((tm, D_MODEL), lambda i: (i, 0))
    vec = pl.BlockSpec((1, ATTN_W), lambda i: (0, 0))
    return pl.pallas_call(
        body, name="outproj_bwd", grid=(t // tm,),
        in_specs=[full, half, half, vec, vec, pl.BlockSpec((D_MODEL, D_MODEL), lambda i: (0, 0))],
        out_specs=[half, half, vec, vec],
        out_shape=[jax.ShapeDtypeStruct((t, ATTN_W), F32), jax.ShapeDtypeStruct((t, ATTN_W), F32),
                   jax.ShapeDtypeStruct((1, ATTN_W), F32), jax.ShapeDtypeStruct((1, ATTN_W), F32)],
        compiler_params=_params(("arbitrary",)),
    )(dout, oa, os_, aw, sw, w_out)


def _inproj_bwd(dproj, w_slabs, x, norm_w, dout):
    t = x.shape[0]
    tm = 512

    def body(dp_ref, w_ref, x_ref, nw_ref, do_ref, gx_ref, gw_ref, acc_ref):
        i, j = pl.program_id(0), pl.program_id(1)

        @pl.when((i == 0) & (j == 0))
        def _():
            gw_ref[...] = jnp.zeros_like(gw_ref)

        @pl.when(j == 0)
        def _():
            acc_ref[...] = jnp.zeros_like(acc_ref)

        acc_ref[...] += _dot_nt(dp_ref[...], w_ref[...])

        @pl.when(j == 3)
        def _():
            xv = x_ref[...]
            r = lax.rsqrt(jnp.mean(xv * xv, axis=1, keepdims=True) + NORM_EPS)
            yh = xv * r
            dh = acc_ref[...]
            gh = dh * nw_ref[...]
            gx_ref[...] = do_ref[...] + r * (gh - yh * jnp.mean(gh * yh, axis=1, keepdims=True))
            gw_ref[...] += jnp.sum(dh * yh, axis=0, keepdims=True)

    full = pl.BlockSpec((tm, D_MODEL), lambda i, j: (i, 0))
    vec = pl.BlockSpec((1, D_MODEL), lambda i, j: (0, 0))
    return pl.pallas_call(
        body, name="inproj_bwd", grid=(t // tm, 4),
        in_specs=[pl.BlockSpec((tm, SHARD_W), lambda i, j: (i, j)),
                  pl.BlockSpec((None, D_MODEL, SHARD_W), lambda i, j: (j, 0, 0)), full, vec, full],
        out_specs=[full, vec],
        out_shape=[jax.ShapeDtypeStruct((t, D_MODEL), F32), jax.ShapeDtypeStruct((1, D_MODEL), F32)],
        scratch_shapes=[pltpu.VMEM((tm, D_MODEL), F32)],
        compiler_params=_params(("arbitrary", "arbitrary")),
    )(dproj, w_slabs, x, norm_w.reshape(1, D_MODEL), dout)


def _adamw_math(w_ref, g_ref, m_ref, v_ref, d_ref, nm_ref, nv_ref):
    gv = g_ref[...]
    nm = ADAM_B1 * m_ref[...] + (1.0 - ADAM_B1) * gv
    nv = ADAM_B2 * v_ref[...] + (1.0 - ADAM_B2) * (gv * gv)
    m_hat = nm / (1.0 - ADAM_B1 ** ADAM_STEP)
    v_hat = nv / (1.0 - ADAM_B2 ** ADAM_STEP)
    d_ref[...] = -ADAM_LR * (m_hat / (jnp.sqrt(v_hat) + ADAM_EPS) + ADAM_WD * w_ref[...])
    nm_ref[...] = nm
    nv_ref[...] = nv


def _adamw(w, g, m, v, *, rows, name):
    r, c = w.shape

    def body(w_ref, g_ref, m_ref, v_ref, d_ref, nm_ref, nv_ref):
        _adamw_math(w_ref, g_ref, m_ref, v_ref, d_ref, nm_ref, nv_ref)

    blk = pl.BlockSpec((rows, c), lambda i: (i, 0))
    shp = jax.ShapeDtypeStruct((r, c), F32)
    return pl.pallas_call(body, name=name, grid=(r // rows,), in_specs=[blk] * 4, out_specs=[blk] * 3,
                          out_shape=[shp] * 3, compiler_params=_params(("parallel",)))(w, g, m, v)


def _remote(src, dst, ssem, rsem, dev):
    return pltpu.make_async_remote_copy(src_ref=src, dst_ref=dst, send_sem=ssem, recv_sem=rsem, device_id=dev,
                                        device_id_type=pl.DeviceIdType.MESH)


def _mesh_pos():
    return lax.axis_index("x"), lax.axis_index("y"), lax.axis_index("c")


def _other_chips(x, y):
    return [(1 - x, y), (x, 1 - y), (1 - x, 1 - y)]


def _gather_weights(shards):
    nt = len(shards)
    halves = [s.shape[0] // 2 for s in shards]

    def body(*refs):
        sh, full = refs[:nt], refs[nt:2 * nt]
        ssem, rsem, lsem = refs[2 * nt:]
        x, y, c = _mesh_pos()
        me = 2 * x + y
        sib = (x, y, 1 - c)
        local = [pltpu.make_async_copy(sh[i], full[i].at[me], lsem.at[i]) for i in range(nt)]
        for cp in local:
            cp.start()

        def half(i, which):
            return pl.ds(pl.multiple_of(which * halves[i], 8), halves[i])

        sends = []
        for k, (px, py) in enumerate(_other_chips(x, y)):
            for i in range(nt):
                cp = _remote(sh[i].at[half(i, c)], full[i].at[me, half(i, c)], ssem.at[k * nt + i],
                             rsem.at[k * nt + i], (px, py, c))
                cp.start()
                sends.append(cp)
        for k, (px, py) in enumerate(_other_chips(x, y)):
            slot = 2 * px + py
            for i in range(nt):
                landed = full[i].at[slot, half(i, c)]
                _remote(landed, landed, ssem.at[k * nt + i], rsem.at[k * nt + i], sib).wait_recv()
                cp = _remote(landed, landed, ssem.at[(3 + k) * nt + i], rsem.at[(3 + k) * nt + i], sib)
                cp.start()
                sends.append(cp)
        for k, (px, py) in enumerate(_other_chips(x, y)):
            slot = 2 * px + py
            for i in range(nt):
                passed = full[i].at[slot, half(i, 1 - c)]
                _remote(passed, passed, ssem.at[(3 + k) * nt + i], rsem.at[(3 + k) * nt + i], sib).wait_recv()
        for cp in sends:
            cp.wait_send()
        for cp in local:
            cp.wait()

    return pl.pallas_call(
        body, name="gather_weights", in_specs=[ANY] * nt, out_specs=[ANY] * nt,
        out_shape=[jax.ShapeDtypeStruct((4,) + s.shape, s.dtype) for s in shards],
        scratch_shapes=[pltpu.SemaphoreType.DMA((6 * nt,)), pltpu.SemaphoreType.DMA((6 * nt,)),
                        pltpu.SemaphoreType.DMA((nt,))],
    )(*shards)


def _pair_swap(arrays):
    nt = len(arrays)

    def body(*refs):
        src, dst = refs[:nt], refs[nt:2 * nt]
        ssem, rsem = refs[2 * nt:]
        x, y, c = _mesh_pos()
        cps = [_remote(src[i].at[:, 1 - c], dst[i], ssem.at[i], rsem.at[i], (x, y, 1 - c)) for i in range(nt)]
        for cp in cps:
            cp.start()
        for cp in cps:
            cp.wait_recv()
        for cp in cps:
            cp.wait_send()

    return pl.pallas_call(
        body, name="pair_swap", in_specs=[ANY] * nt, out_specs=[ANY] * nt,
        out_shape=[jax.ShapeDtypeStruct((4,) + a.shape[2:], a.dtype) for a in arrays],
        scratch_shapes=[pltpu.SemaphoreType.DMA((nt,)), pltpu.SemaphoreType.DMA((nt,))],
    )(*arrays)


def _join_halves(arrays):
    nt = len(arrays)

    def body(*refs):
        src, dst = refs[:nt], refs[nt:2 * nt]
        ssem, rsem, lsem = refs[2 * nt:]
        x, y, c = _mesh_pos()
        local = [pltpu.make_async_copy(src[i], dst[i].at[c], lsem.at[i]) for i in range(nt)]
        cps = [_remote(src[i], dst[i].at[c], ssem.at[i], rsem.at[i], (x, y, 1 - c)) for i in range(nt)]
        for cp in local + cps:
            cp.start()
        for i in range(nt):
            other = dst[i].at[1 - c]
            _remote(other, other, ssem.at[i], rsem.at[i], (x, y, 1 - c)).wait_recv()
        for cp in cps:
            cp.wait_send()
        for cp in local:
            cp.wait()

    return pl.pallas_call(
        body, name="join_halves", in_specs=[ANY] * nt, out_specs=[ANY] * nt,
        out_shape=[jax.ShapeDtypeStruct((2,) + a.shape, a.dtype) for a in arrays],
        scratch_shapes=[pltpu.SemaphoreType.DMA((nt,)), pltpu.SemaphoreType.DMA((nt,)),
                        pltpu.SemaphoreType.DMA((nt,))],
    )(*arrays)


def _scatter_to_chips(arrays):
    nt = len(arrays)

    def body(*refs):
        src, dst = refs[:nt], refs[nt:2 * nt]
        ssem, rsem, lsem = refs[2 * nt:]
        x, y, c = _mesh_pos()
        me = 2 * x + y
        local = [pltpu.make_async_copy(src[i].at[me], dst[i].at[me], lsem.at[i]) for i in range(nt)]
        for cp in local:
            cp.start()
        cps = []
        for k, (px, py) in enumerate(_other_chips(x, y)):
            for i in range(nt):
                cp = _remote(src[i].at[2 * px + py], dst[i].at[me], ssem.at[k * nt + i], rsem.at[k * nt + i],
                             (px, py, c))
                cp.start()
                cps.append(cp)
        for k, (px, py) in enumerate(_other_chips(x, y)):
            for i in range(nt):
                slot = dst[i].at[2 * px + py]
                _remote(slot, slot, ssem.at[k * nt + i], rsem.at[k * nt + i], (px, py, c)).wait_recv()
        for cp in cps:
            cp.wait_send()
        for cp in local:
            cp.wait()

    return pl.pallas_call(
        body, name="scatter_to_chips", in_specs=[ANY] * nt, out_specs=[ANY] * nt,
        out_shape=[jax.ShapeDtypeStruct(a.shape, a.dtype) for a in arrays],
        scratch_shapes=[pltpu.SemaphoreType.DMA((3 * nt,)), pltpu.SemaphoreType.DMA((3 * nt,)),
                        pltpu.SemaphoreType.DMA((nt,))],
    )(*arrays)


def _flips():
    return [(dx, dy, dc) for dx in (0, 1) for dy in (0, 1) for dc in (0, 1) if (dx, dy, dc) != (0, 0, 0)]


def _exchange_slices(src, scatter, name):
    def body(src_ref, dst_ref, ssem, rsem, lsem):
        x, y, c = _mesh_pos()
        me = 4 * x + 2 * y + c
        local = pltpu.make_async_copy(src_ref.at[me] if scatter else src_ref, dst_ref.at[me], lsem)
        local.start()
        cps = []
        for k, (dx, dy, dc) in enumerate(_flips()):
            px, py, pc = jnp.bitwise_xor(x, dx), jnp.bitwise_xor(y, dy), jnp.bitwise_xor(c, dc)
            peer = 4 * px + 2 * py + pc
            cp = _remote(src_ref.at[peer] if scatter else src_ref, dst_ref.at[me], ssem.at[k], rsem.at[k],
                         (px, py, pc))
            cp.start()
            cps.append((cp, peer))
        for k, (cp, peer) in enumerate(cps):
            slot = dst_ref.at[peer]
            _remote(slot, slot, ssem.at[k], rsem.at[k], (x, y, c)).wait_recv()
        for cp, _ in cps:
            cp.wait_send()
        local.wait()

    return pl.pallas_call(
        body, name=name, in_specs=[ANY], out_specs=ANY,
        out_shape=jax.ShapeDtypeStruct((8,) + src.shape[-2:], src.dtype),
        scratch_shapes=[pltpu.SemaphoreType.DMA((7,)), pltpu.SemaphoreType.DMA((7,)), pltpu.SemaphoreType.DMA],
    )(src)


def _add_halves(g, recv, c_idx, *, rows, name):
    _, _, hr, cols = g.shape

    def body(c_ref, g_ref, r_ref, o_ref):
        o_ref[...] = (g_ref[...] + r_ref[...].astype(F32)).astype(BF16)

    return pl.pallas_call(
        body, name=name,
        grid_spec=pltpu.PrefetchScalarGridSpec(
            num_scalar_prefetch=1, grid=(4, hr // rows),
            in_specs=[pl.BlockSpec((None, None, rows, cols), lambda j, i, c: (j, c[0], i, 0)),
                      pl.BlockSpec((None, rows, cols), lambda j, i, c: (j, i, 0))],
            out_specs=pl.BlockSpec((None, rows, cols), lambda j, i, c: (j, i, 0))),
        out_shape=jax.ShapeDtypeStruct((4, hr, cols), BF16),
        compiler_params=_params(("parallel", "parallel")),
    )(c_idx, g, recv)


def _sum_slots(slots, *, rows, name):
    n, r, cols = slots.shape

    def body(s_ref, o_ref):
        acc = s_ref[0].astype(F32)
        for k in range(1, n):
            acc = acc + s_ref[k].astype(F32)
        o_ref[...] = acc

    return pl.pallas_call(
        body, name=name, grid=(r // rows,),
        in_specs=[pl.BlockSpec((n, rows, cols), lambda i: (0, i, 0))],
        out_specs=pl.BlockSpec((rows, cols), lambda i: (i, 0)),
        out_shape=jax.ShapeDtypeStruct((r, cols), F32),
        compiler_params=_params(("parallel",)),
    )(slots)


def _pack_small(d, names, rows):
    flat = jnp.concatenate([d[n].astype(F32).reshape(-1) for n in names])
    return jnp.pad(flat, (0, rows * 128 - flat.shape[0])).reshape(rows, 128)


def _unpack_small(p, names):
    flat = p.reshape(-1)
    out, off = {}, 0
    for n in names:
        size = math.prod(SMALL_SHAPES[n])
        out[n] = flat[off:off + size].reshape(SMALL_SHAPES[n])
        off += size
    return out


def _adamw_3d(w, g, m, v, *, name):
    def body(w_ref, g_ref, m_ref, v_ref, d_ref, nm_ref, nv_ref):
        _adamw_math(w_ref, g_ref, m_ref, v_ref, d_ref, nm_ref, nv_ref)

    blk = pl.BlockSpec((8,) + w.shape[1:], lambda i: (i, 0, 0))
    shp = jax.ShapeDtypeStruct(w.shape, F32)
    return pl.pallas_call(body, name=name, grid=(w.shape[0] // 8,), in_specs=[blk] * 4, out_specs=[blk] * 3,
                          out_shape=[shp] * 3, compiler_params=_params(("parallel",)))(w, g, m, v)


def kernel(x, positions, norm_w, w_in, q_norm_w, k_norm_w, sinks, a_re, a_im, log_step, b_re, b_im, c_re, c_im, d_skip, w_glu, b_glu, attn_out_norm_w, ssm_out_norm_w, w_out, loss_target, m_norm_w, m_w_in, m_q_norm_w, m_k_norm_w, m_sinks, m_a_re, m_a_im, m_log_step, m_b_re, m_b_im, m_c_re, m_c_im, m_d_skip, m_w_glu, m_b_glu, m_attn_out_norm_w, m_ssm_out_norm_w, m_w_out, v_norm_w, v_w_in, v_q_norm_w, v_k_norm_w, v_sinks, v_a_re, v_a_im, v_log_step, v_b_re, v_b_im, v_c_re, v_c_im, v_d_skip, v_w_glu, v_b_glu, v_attn_out_norm_w, v_ssm_out_norm_w, v_w_out):
    small_w = dict(norm_w=norm_w, q_norm_w=q_norm_w, k_norm_w=k_norm_w, sinks=sinks, a_re=a_re, a_im=a_im,
                   log_step=log_step, b_re=b_re, b_im=b_im, c_re=c_re, c_im=c_im, d_skip=d_skip, b_glu=b_glu,
                   attn_out_norm_w=attn_out_norm_w, ssm_out_norm_w=ssm_out_norm_w)
    small_m = dict(norm_w=m_norm_w, q_norm_w=m_q_norm_w, k_norm_w=m_k_norm_w, sinks=m_sinks, a_re=m_a_re, a_im=m_a_im,
                   log_step=m_log_step, b_re=m_b_re, b_im=m_b_im, c_re=m_c_re, c_im=m_c_im, d_skip=m_d_skip,
                   b_glu=m_b_glu, attn_out_norm_w=m_attn_out_norm_w, ssm_out_norm_w=m_ssm_out_norm_w)
    small_v = dict(norm_w=v_norm_w, q_norm_w=v_q_norm_w, k_norm_w=v_k_norm_w, sinks=v_sinks, a_re=v_a_re, a_im=v_a_im,
                   log_step=v_log_step, b_re=v_b_re, b_im=v_b_im, c_re=v_c_re, c_im=v_c_im, d_skip=v_d_skip,
                   b_glu=v_b_glu, attn_out_norm_w=v_attn_out_norm_w, ssm_out_norm_w=v_ssm_out_norm_w)
    c_idx = lax.axis_index("c").astype(jnp.int32).reshape(1)

    xs = x[0]
    tgt = loss_target[0]
    t = xs.shape[0]
    posf = positions[0].astype(F32).reshape(t, 1)

    w_in_all, w_glu_all, w_out_all = _gather_weights([w_in.astype(BF16), w_glu.astype(BF16), w_out.astype(BF16)])
    w_glu_b = w_glu_all.reshape(SSM_W, SSM_W)
    w_out_b = w_out_all.reshape(D_MODEL, D_MODEL)

    proj, hn = _inproj(xs, norm_w, w_in_all)
    inv_freq = ROPE_THETA ** (-jnp.arange(0, HEAD_DIM, 2, dtype=F32) / HEAD_DIM)
    invf = jnp.tile(inv_freq, 4).reshape(1, 128)
    qw = jnp.tile(q_norm_w, 2).reshape(1, 128)
    kw = jnp.tile(k_norm_w, 2).reshape(1, 128)
    sink_row = sinks.reshape(1, N_HEADS)
    oa = _attn_fwd(proj, posf, invf, qw, kw, sink_row)

    lam_r, lam_i, pw_r, pw_i, bb_r, bb_i = _ssm_prep(a_re, a_im, log_step, b_re, b_im, t // N_SEG)
    rows8 = lambda a: jnp.broadcast_to(a.reshape(SSM_GB, 1, SSM_ST), (SSM_GB, N_SEG, SSM_ST))
    lam_r8, lam_i8, pw_r8, pw_i8 = rows8(lam_r), rows8(lam_i), rows8(pw_r), rows8(pw_i)
    ssm_w_in = jnp.concatenate([_block_diag_in(bb_r), _block_diag_in(bb_i)], axis=1).astype(BF16)
    ssm_w_out = jnp.concatenate([_block_diag_out(c_re), _block_diag_out(-c_im)], axis=2).astype(BF16)
    d_row = d_skip.reshape(1, SSM_W)
    uz = _permute_rows(proj[:, 2560:])
    up, zsp = uz[:, :SSM_W], uz[:, SSM_W:]
    yp, hc = _ssm_fwd(up, lam_r8, lam_i8, pw_r8, pw_i8, ssm_w_in, ssm_w_out, d_row)
    b_glu_row = b_glu.reshape(1, SSM_W)
    osp, ygp = _glu_fwd(yp, zsp, w_glu_b, b_glu_row)
    os_ = _unpermute_rows(osp)
    aw = attn_out_norm_w.reshape(1, ATTN_W)
    sw = ssm_out_norm_w.reshape(1, SSM_W)
    merged, dout, sq_err = _outproj(oa, os_, aw, sw, w_out_b, xs, tgt)
    loss = lax.psum(0.5 * sq_err[0, 0] / D_MODEL, MESH_AXES)

    doa, dos, g_aw, g_sw = _outproj_bwd(dout, oa, os_, aw, sw, w_out_b)
    dout_b = dout.astype(BF16)
    g_w_out, g_w_out_b = _matmul_tn(merged, dout_b, tm=512, tn=1024, tk=1024, name="grad_w_out")
    dyp, dzsp, dap, g_b_glu = _glu_bwd(yp, zsp, _permute_rows(dos), w_glu_b, b_glu_row)
    g_w_glu, g_w_glu_b = _matmul_tn(ygp, dap, tm=512, tn=1024, tk=1024, name="grad_w_glu")
    dup, g_wi, g_wo, g_lam, g_d = _ssm_bwd(up, dyp, hc, lam_r8, lam_i8, pw_r8, pw_i8, ssm_w_in, ssm_w_out, d_row)
    dq, dk, dv, dza, g_qw, g_kw, g_sink = _attn_bwd(proj, posf, invf, qw, kw, sink_row, doa)
    duz = _unpermute_rows(jnp.concatenate([dup, dzsp], axis=1))
    dproj = jnp.concatenate([dq, dk, dv, dza, duz], axis=1)
    grad_x, g_nw = _inproj_bwd(dproj, w_in_all, xs, norm_w, dout)
    g_w_in, g_w_in_b = _matmul_tn(hn, dproj, tm=512, tn=SHARD_W, tk=1024, name="grad_w_in", slabs=True)

    g_bb_r = _diag_blocks(g_wi[:, :, :SSM_ST]).transpose(0, 2, 1).reshape(SSM_G, SSM_P * SSM_H)
    g_bb_i = _diag_blocks(g_wi[:, :, SSM_ST:]).transpose(0, 2, 1).reshape(SSM_G, SSM_P * SSM_H)
    g_a_re, g_a_im, g_ls, g_b_re, g_b_im = _ssm_param_grads(
        a_re, a_im, log_step, b_re, b_im, g_lam[:, 0, :SSM_ST].reshape(SSM_G, SSM_P),
        g_lam[:, 0, SSM_ST:].reshape(SSM_G, SSM_P), g_bb_r, g_bb_i)
    small_g = dict(
        norm_w=g_nw, q_norm_w=g_qw[0, :64] + g_qw[0, 64:], k_norm_w=g_kw[0, :64] + g_kw[0, 64:],
        sinks=g_sink[0, :N_HEADS], a_re=g_a_re, a_im=g_a_im, log_step=g_ls, b_re=g_b_re, b_im=g_b_im,
        c_re=_diag_blocks(g_wo[:, :, :SSM_ST]), c_im=-_diag_blocks(g_wo[:, :, SSM_ST:]), d_skip=g_d,
        b_glu=g_b_glu, attn_out_norm_w=g_aw, ssm_out_norm_w=g_sw)

    shapes = [(4, 2, D_MODEL // 2, SHARD_W), (4, 2, 128, SSM_W), (4, 2, 256, D_MODEL)]
    parts = [g.reshape(s) for g, s in zip((g_w_in, g_w_glu, g_w_out), shapes)]
    from_sib = _pair_swap([g.reshape(s) for g, s in zip((g_w_in_b, g_w_glu_b, g_w_out_b), shapes)])
    pair = [_add_halves(p, r, c_idx, rows=128, name=f"pair_sum_{i}") for i, (p, r) in enumerate(zip(parts, from_sib))]
    slots = _scatter_to_chips(pair)
    mine = [_sum_slots(s, rows=128, name=f"chip_sum_{i}") for i, s in enumerate(slots)]
    big_g = [both.reshape(-1, both.shape[-1]) for both in _join_halves(mine)]
    packed = _pack_small(small_g, SMALL, 8 * PACK_ROWS).reshape(8, PACK_ROWS, 128)
    summed = _sum_slots(_exchange_slices(packed, True, "small_scatter"), rows=PACK_ROWS, name="small_sum")
    small_red = _exchange_slices(summed, False, "small_gather").reshape(8 * PACK_ROWS, 128)

    g_in_sh, g_glu_sh, g_out_sh = big_g
    upd = [_adamw(w_in, g_in_sh, m_w_in, v_w_in, rows=256, name="adamw_w_in"),
           _adamw(w_glu, g_glu_sh, m_w_glu, v_w_glu, rows=256, name="adamw_w_glu"),
           _adamw(w_out, g_out_sh, m_w_out, v_w_out, rows=256, name="adamw_w_out")]
    grads = _unpack_small(small_red, SMALL)
    flat_first = sum(math.prod(SMALL_SHAPES[n]) for n in SMALL_3D) // 128
    sd, sm, sv = _adamw(_pack_small(small_w, SMALL_FLAT, FLAT_ROWS), small_red[flat_first:flat_first + FLAT_ROWS],
                        _pack_small(small_m, SMALL_FLAT, FLAT_ROWS), _pack_small(small_v, SMALL_FLAT, FLAT_ROWS),
                        rows=FLAT_ROWS, name="adamw_small")
    deltas, new_m, new_v = (_unpack_small(a, SMALL_FLAT) for a in (sd, sm, sv))
    for n in SMALL_3D:
        deltas[n], new_m[n], new_v[n] = _adamw_3d(small_w[n], grads[n], small_m[n], small_v[n], name="adamw_" + n)
    grads.update(w_in=g_in_sh, w_glu=g_glu_sh, w_out=g_out_sh)
    for n, (d, m_, v_) in zip(("w_in", "w_glu", "w_out"), upd):
        deltas[n], new_m[n], new_v[n] = d, m_, v_
    order = ["norm_w", "w_in", "q_norm_w", "k_norm_w", "sinks", "a_re", "a_im", "log_step", "b_re", "b_im", "c_re",
             "c_im", "d_skip", "w_glu", "b_glu", "attn_out_norm_w", "ssm_out_norm_w", "w_out"]
    return (loss, grad_x[None], *[grads[n] for n in order], *[deltas[n] for n in order],
            *[new_m[n] for n in order], *[new_v[n] for n in order])
```

```python
import math

import jax
import jax.numpy as jnp
from jax import lax
from jax.experimental import pallas as pl
from jax.experimental.pallas import tpu as pltpu

F32 = jnp.float32
BF16 = jnp.bfloat16

D_MODEL = 2048
ATTN_W = 1024
SSM_W = 1024
HEAD_DIM = 64
N_HEADS = 16
N_KV_HEADS = 4
KV_W = 256
BLOCK = 128
IN_W = 4608
SHARD_W = IN_W // 4
ROPE_THETA = 10000.0
SSM_H = 16
SSM_G = 64
SSM_P = 64
NORM_EPS = 1e-6
ADAM_LR = 0.001
ADAM_B1 = 0.9
ADAM_B2 = 0.999
ADAM_EPS = 1e-08
ADAM_WD = 0.01
ADAM_STEP = 10

N_SEG = 8
SSM_GB = 4
SSM_CH = 256
SSM_ST = 1024
SCAN_ROWS = 256
SCAN_LW = 512
VMEM_LIMIT = 56 * 1024 * 1024
MESH_AXES = ("x", "y", "c")
ANY = pl.BlockSpec(memory_space=pl.ANY)

SMALL_3D = ("b_re", "b_im", "c_re", "c_im")
SMALL_FLAT = ("norm_w", "q_norm_w", "k_norm_w", "sinks", "a_re", "a_im", "log_step", "d_skip", "b_glu",
              "attn_out_norm_w", "ssm_out_norm_w")
SMALL = SMALL_3D + SMALL_FLAT
SMALL_SHAPES = {"norm_w": (2048,), "q_norm_w": (64,), "k_norm_w": (64,), "sinks": (16,), "a_re": (64, 64),
                "a_im": (64, 64), "log_step": (64,), "b_re": (64, 64, 16), "b_im": (64, 64, 16),
                "c_re": (64, 16, 64), "c_im": (64, 16, 64), "d_skip": (1024,), "b_glu": (1024,),
                "attn_out_norm_w": (1024,), "ssm_out_norm_w": (1024,)}
PACK_ROWS = 272
FLAT_ROWS = 120


def _params(sem=None):
    return pltpu.CompilerParams(dimension_semantics=sem, vmem_limit_bytes=VMEM_LIMIT)


def _dot(a, b):
    return jnp.dot(a, b, preferred_element_type=F32)


def _dot_nt(a, b):
    return lax.dot_general(a, b, (((1,), (1,)), ((), ())), preferred_element_type=F32)


def _dot_tn(a, b):
    return lax.dot_general(a, b, (((0,), (0,)), ((), ())), preferred_element_type=F32)


def _sigmoid(x):
    return 1.0 / (1.0 + jnp.exp(-x))


def _silu(x):
    return x * _sigmoid(x)


def _dsilu(x):
    s = _sigmoid(x)
    return s * (1.0 + x * (1.0 - s))


_GELU_C = math.sqrt(2.0 / math.pi)


def _gelu(x):
    return 0.5 * x * (1.0 + jnp.tanh(_GELU_C * (x + 0.044715 * x * x * x)))


def _dgelu(x):
    t = jnp.tanh(_GELU_C * (x + 0.044715 * x * x * x))
    return 0.5 * (1.0 + t) + 0.5 * x * (1.0 - t * t) * _GELU_C * (1.0 + 3.0 * 0.044715 * x * x)


def _matmul_tn(a, b, *, tm, tn, tk, name, slabs=False):
    k, m = a.shape
    _, n = b.shape
    nk = k // tk

    def body(a_ref, b_ref, o_ref, ob_ref, acc_ref):
        kk = pl.program_id(2)

        @pl.when(kk == 0)
        def _():
            acc_ref[...] = jnp.zeros_like(acc_ref)

        acc_ref[...] += _dot_tn(a_ref[...], b_ref[...])

        @pl.when(kk == nk - 1)
        def _():
            o_ref[...] = acc_ref[...]
            ob_ref[...] = acc_ref[...].astype(BF16)

    if slabs:
        out_spec = pl.BlockSpec((None, tm, tn), lambda i, j, kk: (j, i, 0))
        shape = (n // tn, m, tn)
    else:
        out_spec = pl.BlockSpec((tm, tn), lambda i, j, kk: (i, j))
        shape = (m, n)
    return pl.pallas_call(
        body, name=name, grid=(m // tm, n // tn, nk),
        in_specs=[pl.BlockSpec((tk, tm), lambda i, j, kk: (kk, i)), pl.BlockSpec((tk, tn), lambda i, j, kk: (kk, j))],
        out_specs=[out_spec, out_spec],
        out_shape=[jax.ShapeDtypeStruct(shape, F32), jax.ShapeDtypeStruct(shape, BF16)],
        scratch_shapes=[pltpu.VMEM((tm, tn), F32)],
        compiler_params=_params(("parallel", "parallel", "arbitrary")),
    )(a, b)


def _inproj(x, norm_w, w_slabs):
    t = x.shape[0]
    tm = 512

    def body(x_ref, nw_ref, w_ref, proj_ref, hn_ref, hn_s):
        @pl.when(pl.program_id(1) == 0)
        def _():
            xv = x_ref[...]
            r = lax.rsqrt(jnp.mean(xv * xv, axis=1, keepdims=True) + NORM_EPS)
            hn = (xv * r * nw_ref[...]).astype(BF16)
            hn_s[...] = hn
            hn_ref[...] = hn

        proj_ref[...] = _dot(hn_s[...], w_ref[...])

    return pl.pallas_call(
        body, name="inproj", grid=(t // tm, 4),
        in_specs=[pl.BlockSpec((tm, D_MODEL), lambda i, j: (i, 0)), pl.BlockSpec((1, D_MODEL), lambda i, j: (0, 0)),
                  pl.BlockSpec((None, D_MODEL, SHARD_W), lambda i, j: (j, 0, 0))],
        out_specs=[pl.BlockSpec((tm, SHARD_W), lambda i, j: (i, j)), pl.BlockSpec((tm, D_MODEL), lambda i, j: (i, 0))],
        out_shape=[jax.ShapeDtypeStruct((t, IN_W), F32), jax.ShapeDtypeStruct((t, D_MODEL), BF16)],
        scratch_shapes=[pltpu.VMEM((tm, D_MODEL), BF16)],
        compiler_params=_params(("parallel", "arbitrary")),
    )(x, norm_w.reshape(1, D_MODEL), w_slabs)


def _lane128():
    return lax.broadcasted_iota(jnp.int32, (1, 128), 1)


def _head_sums(v):
    lo = _lane128() < 64
    s_lo = jnp.sum(jnp.where(lo, v, 0.0), axis=1, keepdims=True)
    s_hi = jnp.sum(jnp.where(lo, 0.0, v), axis=1, keepdims=True)
    return jnp.where(lo, s_lo, s_hi)


def _rot_half(t):
    first = (_lane128() % 64) < 32
    return jnp.where(first, -pltpu.roll(t, 96, 1), pltpu.roll(t, 32, 1))


def _prep_tile(t, w, cos, sin):
    r = lax.rsqrt(_head_sums(t * t) * (1.0 / HEAD_DIM) + NORM_EPS)
    tn = t * r * w
    return tn * cos + _rot_half(tn) * sin


def _prep_tile_bwd(t, w, cos, sin, g):
    r = lax.rsqrt(_head_sums(t * t) * (1.0 / HEAD_DIM) + NORM_EPS)
    d_tn = g * cos - _rot_half(g * sin)
    th = t * r
    dw = jnp.sum(d_tn * th, axis=0, keepdims=True)
    gh = d_tn * w
    m = _head_sums(gh * th) * (1.0 / HEAD_DIM)
    return r * (gh - th * m), dw


def _band_mask(n):
    qi = lax.broadcasted_iota(jnp.int32, (BLOCK, 2 * BLOCK), 0) + BLOCK
    ki = lax.broadcasted_iota(jnp.int32, (BLOCK, 2 * BLOCK), 1)
    rel = qi - ki
    return (rel >= 0) & (rel < BLOCK) & ((n > 0) | (ki >= BLOCK))


def _half_select(tile, half):
    lo = _lane128() < 64
    return jnp.where(lo if half == 0 else jnp.logical_not(lo), tile, 0.0)


def _stack_group(tiles, kv_half):
    rows = []
    for t in tiles:
        for half in range(2):
            piece = _half_select(t, half)
            rows.append(piece if half == kv_half else pltpu.roll(piece, 64, 1))
    return jnp.concatenate(rows, axis=0)


def _unstack_group(stacked, kv_half):
    tiles = []
    for i in range(2):
        acc = None
        for half in range(2):
            piece = _half_select(stacked[BLOCK * (2 * i + half):BLOCK * (2 * i + half + 1)], kv_half)
            piece = piece if half == kv_half else pltpu.roll(piece, 64, 1)
            acc = piece if acc is None else acc + piece
        tiles.append(acc)
    return tiles


def _group_sinks(sink_ref, g):
    return jnp.concatenate([jnp.broadcast_to(sink_ref[:, 4 * g + j:4 * g + j + 1], (BLOCK, 1)) for j in range(4)], axis=0)


def _attn_specs(nb):
    last = nb - 1
    qi = lambda n: (jnp.minimum(n, last), 0)
    prev = lambda n: jnp.maximum(n - 1, 0)
    cur = lambda n: jnp.minimum(n, last)
    specs = [
        pl.BlockSpec((BLOCK, ATTN_W), qi),
        pl.BlockSpec((BLOCK, KV_W), lambda n: (cur(n), 4)),
        pl.BlockSpec((BLOCK, KV_W), lambda n: (prev(n), 4)),
        pl.BlockSpec((BLOCK, KV_W), lambda n: (cur(n), 5)),
        pl.BlockSpec((BLOCK, KV_W), lambda n: (prev(n), 5)),
        pl.BlockSpec((BLOCK, 512), lambda n: (cur(n), 3)),
        pl.BlockSpec((BLOCK, 512), lambda n: (cur(n), 4)),
        pl.BlockSpec((BLOCK, 1), lambda n: (cur(n), 0)),
        pl.BlockSpec((BLOCK, 1), lambda n: (prev(n), 0)),
        pl.BlockSpec((1, 128), lambda n: (0, 0)),
        pl.BlockSpec((1, 128), lambda n: (0, 0)),
        pl.BlockSpec((1, 128), lambda n: (0, 0)),
        pl.BlockSpec((1, N_HEADS), lambda n: (0, 0)),
    ]
    return specs


def _attn_common(n, q_ref, kc_ref, kp_ref, vc_ref, vp_ref, pq_ref, pp_ref, invf_ref, qw_ref, kw_ref):
    invf = invf_ref[...]
    ang_q = pq_ref[...] * invf
    ang_p = pp_ref[...] * invf
    cos_q, sin_q = jnp.cos(ang_q), jnp.sin(ang_q)
    cos_k = jnp.concatenate([jnp.cos(ang_p), cos_q], axis=0)
    sin_k = jnp.concatenate([jnp.sin(ang_p), sin_q], axis=0)
    k_raw = jnp.concatenate([kp_ref[...], kc_ref[...]], axis=0)
    vv = jnp.concatenate([vp_ref[...], vc_ref[...]], axis=0).astype(BF16)
    kk = [_prep_tile(k_raw[:, 128 * i:128 * i + 128], kw_ref[...], cos_k, sin_k).astype(BF16) for i in range(2)]
    vt = [vv[:, 128 * i:128 * i + 128] for i in range(2)]
    qv = q_ref[...]
    qt = [_prep_tile(qv[:, 128 * i:128 * i + 128], qw_ref[...], cos_q, sin_q) for i in range(8)]
    return cos_q, sin_q, cos_k, sin_k, k_raw, kk, vt, qt


def _head_softmax(q_sel, kk_t, sink, valid):
    s = _dot_nt(q_sel, kk_t) * (1.0 / math.sqrt(HEAD_DIM))
    s = jnp.where(valid, s, -1e30)
    m = jnp.maximum(jnp.max(s, axis=1, keepdims=True), sink)
    p = jnp.exp(s - m)
    es = jnp.exp(sink - m)
    inv = 1.0 / (jnp.sum(p, axis=1, keepdims=True) + es)
    return p * inv, es * inv


def _attn_fwd(proj, posf, invf, qw, kw, sinks):
    t = proj.shape[0]
    nb = t // BLOCK

    def body(q_ref, kc_ref, kp_ref, vc_ref, vp_ref, za0_ref, za1_ref, pq_ref, pp_ref, invf_ref, qw_ref, kw_ref,
             sink_ref, o_ref):
        n = pl.program_id(0)
        _, _, _, _, _, kk, vt, qt = _attn_common(n, q_ref, kc_ref, kp_ref, vc_ref, vp_ref, pq_ref, pp_ref, invf_ref,
                                                 qw_ref, kw_ref)
        valid = jnp.concatenate([_band_mask(n)] * 4, axis=0)
        tiles = []
        for g in range(N_KV_HEADS):
            q4 = _stack_group(qt[2 * g:2 * g + 2], g % 2).astype(BF16)
            p, _ = _head_softmax(q4, kk[g // 2], _group_sinks(sink_ref, g), valid)
            tiles += _unstack_group(_dot(p.astype(BF16), vt[g // 2]), g % 2)
        za = jnp.concatenate([za0_ref[...], za1_ref[...]], axis=1)
        o_ref[...] = jnp.concatenate(tiles, axis=1) * _silu(za)

    return pl.pallas_call(
        body, name="attn_fwd", grid=(nb,), in_specs=_attn_specs(nb),
        out_specs=pl.BlockSpec((BLOCK, ATTN_W), lambda n: (n, 0)),
        out_shape=jax.ShapeDtypeStruct((t, ATTN_W), F32),
        compiler_params=_params(("parallel",)),
    )(proj, proj, proj, proj, proj, proj, proj, posf, posf, invf, qw, kw, sinks)


def _attn_bwd(proj, posf, invf, qw, kw, sinks, doa):
    t = proj.shape[0]
    nb = t // BLOCK
    last = nb - 1

    def body(q_ref, kc_ref, kp_ref, vc_ref, vp_ref, za0_ref, za1_ref, pq_ref, pp_ref, invf_ref, qw_ref, kw_ref,
             sink_ref, doa_ref, dq_ref, dk_ref, dv_ref, dza_ref, gq_ref, gk_ref, gs_ref, dkk_s, dvv_s, ck_s, cv_s):
        n = pl.program_id(0)

        @pl.when(n == 0)
        def _():
            gq_ref[...] = jnp.zeros_like(gq_ref)
            gk_ref[...] = jnp.zeros_like(gk_ref)
            gs_ref[...] = jnp.zeros_like(gs_ref)
            ck_s[...] = jnp.zeros_like(ck_s)
            cv_s[...] = jnp.zeros_like(cv_s)

        @pl.when(n == nb)
        def _():
            dkk_s[...] = jnp.zeros_like(dkk_s)
            dvv_s[...] = jnp.zeros_like(dvv_s)

        @pl.when(n < nb)
        def _():
            cos_q, sin_q, _, _, _, kk, vt, qt = _attn_common(n, q_ref, kc_ref, kp_ref, vc_ref, vp_ref, pq_ref, pp_ref,
                                                             invf_ref, qw_ref, kw_ref)
            valid = jnp.concatenate([_band_mask(n)] * 4, axis=0)
            za = jnp.concatenate([za0_ref[...], za1_ref[...]], axis=1)
            doa_v = doa_ref[...]
            do_full = doa_v * _silu(za)
            o_tiles, dq_tiles = [], []
            dkk = [jnp.zeros((2 * BLOCK, 128), F32) for _ in range(2)]
            dvv = [jnp.zeros((2 * BLOCK, 128), F32) for _ in range(2)]
            gsink = jnp.zeros((1, 128), F32)
            lane = _lane128()
            for g in range(N_KV_HEADS):
                q_b = _stack_group(qt[2 * g:2 * g + 2], g % 2).astype(BF16)
                do_b = _stack_group([do_full[:, 128 * i:128 * i + 128] for i in (2 * g, 2 * g + 1)], g % 2).astype(BF16)
                p, psink = _head_softmax(q_b, kk[g // 2], _group_sinks(sink_ref, g), valid)
                p_b = p.astype(BF16)
                o_tiles += _unstack_group(_dot(p_b, vt[g // 2]), g % 2)
                dp = _dot_nt(do_b, vt[g // 2])
                delta = jnp.sum(p * dp, axis=1, keepdims=True)
                ds_b = (p * (dp - delta) * (1.0 / math.sqrt(HEAD_DIM))).astype(BF16)
                sd = psink * delta
                for j in range(4):
                    gsink = gsink + jnp.where(lane == 4 * g + j, -jnp.sum(sd[BLOCK * j:BLOCK * (j + 1)]), 0.0)
                dq_tiles += _unstack_group(_dot(ds_b, kk[g // 2]), g % 2)
                dkk[g // 2] = dkk[g // 2] + _dot_tn(ds_b, q_b)
                dvv[g // 2] = dvv[g // 2] + _dot_tn(p_b, do_b)
            dza_ref[...] = (doa_v * jnp.concatenate(o_tiles, axis=1) * _dsilu(za)).astype(BF16)
            qv = q_ref[...]
            gq = jnp.zeros((1, 128), F32)
            out = []
            for i in range(8):
                d, dw = _prep_tile_bwd(qv[:, 128 * i:128 * i + 128], qw_ref[...], cos_q, sin_q, dq_tiles[i])
                out.append(d)
                gq = gq + dw
            dq_ref[...] = jnp.concatenate(out, axis=1).astype(BF16)
            gq_ref[...] += gq
            gs_ref[...] += gsink
            dkk_s[...] = jnp.concatenate(dkk, axis=1)
            dvv_s[...] = jnp.concatenate(dvv, axis=1)

        invf = invf_ref[...]
        ang_p = pp_ref[...] * invf
        cos_p, sin_p = jnp.cos(ang_p), jnp.sin(ang_p)
        dk_prev = ck_s[...] + dkk_s[0:BLOCK, :]
        kp = kp_ref[...]
        gk = jnp.zeros((1, 128), F32)
        out = []
        for i in range(2):
            d, dw = _prep_tile_bwd(kp[:, 128 * i:128 * i + 128], kw_ref[...], cos_p, sin_p,
                                   dk_prev[:, 128 * i:128 * i + 128])
            out.append(d)
            gk = gk + dw
        dk_ref[...] = jnp.concatenate(out, axis=1).astype(BF16)
        dv_ref[...] = (cv_s[...] + dvv_s[0:BLOCK, :]).astype(BF16)
        gk_ref[...] += gk
        ck_s[...] = dkk_s[BLOCK:2 * BLOCK, :]
        cv_s[...] = dvv_s[BLOCK:2 * BLOCK, :]

    qblk = lambda n: (jnp.minimum(n, last), 0)
    kblk = lambda n: (jnp.maximum(n - 1, 0), 0)
    vec = pl.BlockSpec((1, 128), lambda n: (0, 0))
    return pl.pallas_call(
        body, name="attn_bwd", grid=(nb + 1,),
        in_specs=_attn_specs(nb) + [pl.BlockSpec((BLOCK, ATTN_W), qblk)],
        out_specs=[pl.BlockSpec((BLOCK, ATTN_W), qblk), pl.BlockSpec((BLOCK, KV_W), kblk),
                   pl.BlockSpec((BLOCK, KV_W), kblk), pl.BlockSpec((BLOCK, ATTN_W), qblk), vec, vec, vec],
        out_shape=[jax.ShapeDtypeStruct((t, ATTN_W), BF16), jax.ShapeDtypeStruct((t, KV_W), BF16),
                   jax.ShapeDtypeStruct((t, KV_W), BF16), jax.ShapeDtypeStruct((t, ATTN_W), BF16),
                   jax.ShapeDtypeStruct((1, 128), F32), jax.ShapeDtypeStruct((1, 128), F32),
                   jax.ShapeDtypeStruct((1, 128), F32)],
        scratch_shapes=[pltpu.VMEM((2 * BLOCK, KV_W), F32), pltpu.VMEM((2 * BLOCK, KV_W), F32),
                        pltpu.VMEM((BLOCK, KV_W), F32), pltpu.VMEM((BLOCK, KV_W), F32)],
        compiler_params=_params(("arbitrary",)),
    )(proj, proj, proj, proj, proj, proj, proj, posf, posf, invf, qw, kw, sinks, doa)


def _cmul(ar, ai, br, bi):
    return ar * br - ai * bi, ar * bi + ai * br


def _zoh(a_re, a_im, delta):
    e = jnp.exp(a_re * delta)
    lr, li = e * jnp.cos(a_im * delta), e * jnp.sin(a_im * delta)
    inv = 1.0 / (a_re * a_re + a_im * a_im)
    fr, fi = _cmul(lr - 1.0, li, a_re * inv, -a_im * inv)
    return lr, li, fr, fi


def _ssm_prep(a_re, a_im, log_step, b_re, b_im, seg_len):
    n_sq = int(round(math.log2(seg_len)))
    assert 2 ** n_sq == seg_len

    def body(ar_ref, ai_ref, ls_ref, arx_ref, aix_ref, br_ref, bi_ref, lr_ref, li_ref, pr_ref, pi_ref, bbr_ref, bbi_ref):
        delta = jnp.exp(ls_ref[...])
        lr, li, _, _ = _zoh(ar_ref[...], ai_ref[...], delta)
        lr_ref[...] = lr
        li_ref[...] = li
        pr, pi = lr, li
        for _ in range(n_sq):
            pr, pi = _cmul(pr, pi, pr, pi)
        pr_ref[...] = pr
        pi_ref[...] = pi
        _, _, fr, fi = _zoh(arx_ref[...], aix_ref[...], delta)
        bbr, bbi = _cmul(fr, fi, br_ref[...], bi_ref[...])
        bbr_ref[...] = bbr
        bbi_ref[...] = bbi

    gp = jax.ShapeDtypeStruct((SSM_G, SSM_P), F32)
    gx = jax.ShapeDtypeStruct((SSM_G, SSM_P * SSM_H), F32)
    return pl.pallas_call(body, name="ssm_prep", out_shape=[gp, gp, gp, gp, gx, gx])(
        a_re, a_im, log_step.reshape(SSM_G, 1), jnp.repeat(a_re, SSM_H, axis=1), jnp.repeat(a_im, SSM_H, axis=1),
        b_re.reshape(SSM_G, SSM_P * SSM_H), b_im.reshape(SSM_G, SSM_P * SSM_H))


def _ssm_param_grads(a_re, a_im, log_step, b_re, b_im, dlam_re, dlam_im, dbb_re, dbb_im):
    def body(ar_ref, ai_ref, ls_ref, arx_ref, aix_ref, br_ref, bi_ref, dlr_ref, dli_ref, dbr_ref, dbi_ref,
             gar_ref, gai_ref, gls_ref, gbr_ref, gbi_ref):
        delta = jnp.exp(ls_ref[...])
        ar, ai = ar_ref[...], ai_ref[...]
        lr, li, fr, fi = _zoh(ar, ai, delta)
        _, _, frx, fix = _zoh(arx_ref[...], aix_ref[...], delta)
        dbr, dbi = dbr_ref[...], dbi_ref[...]
        br, bi = br_ref[...], bi_ref[...]
        gbr, gbi = _cmul(frx, -fix, dbr, dbi)
        gbr_ref[...] = gbr
        gbi_ref[...] = gbi
        tr, ti = _cmul(br, -bi, dbr, dbi)
        row = lax.broadcasted_iota(jnp.int32, (SSM_P * SSM_H, SSM_P), 0)
        col = lax.broadcasted_iota(jnp.int32, (SSM_P * SSM_H, SSM_P), 1)
        fold = (row // SSM_H == col).astype(F32)
        dfr = jnp.dot(tr, fold, precision=lax.Precision.HIGHEST, preferred_element_type=F32)
        dfi = jnp.dot(ti, fold, precision=lax.Precision.HIGHEST, preferred_element_type=F32)
        inv = 1.0 / (ar * ar + ai * ai)
        ilr, ili = ar * inv, -ai * inv
        t1r, t1i = _cmul(dfr, dfi, ilr, -ili)
        dlbr, dlbi = dlr_ref[...] + t1r, dli_ref[...] + t1i
        qr, qi = _cmul(fr, fi, ilr, ili)
        t2r, t2i = _cmul(dfr, dfi, qr, -qi)
        glr, gli = -t2r, -t2i
        dzr, dzi = _cmul(dlbr, dlbi, lr, -li)
        gar_ref[...] = glr + dzr * delta
        gai_ref[...] = gli + dzi * delta
        gls_ref[...] = jnp.sum(dzr * ar + dzi * ai, axis=1, keepdims=True) * delta

    gp = jax.ShapeDtypeStruct((SSM_G, SSM_P), F32)
    gx = jax.ShapeDtypeStruct((SSM_G, SSM_P * SSM_H), F32)
    return pl.pallas_call(body, name="ssm_param_grads",
                          out_shape=[gp, gp, jax.ShapeDtypeStruct((SSM_G, 1), F32), gx, gx])(
        a_re, a_im, log_step.reshape(SSM_G, 1), jnp.repeat(a_re, SSM_H, axis=1), jnp.repeat(a_im, SSM_H, axis=1),
        b_re.reshape(SSM_G, SSM_P * SSM_H), b_im.reshape(SSM_G, SSM_P * SSM_H), dlam_re, dlam_im, dbb_re, dbb_im)


def _block_diag_in(bb):
    w = jnp.tile(bb.reshape(SSM_GB, SSM_ST, SSM_H), (1, 1, 16))
    row = lax.broadcasted_iota(jnp.int32, (1, SSM_ST, SSM_CH), 1) // SSM_P
    col = lax.broadcasted_iota(jnp.int32, (1, SSM_ST, SSM_CH), 2) // SSM_H
    return jnp.where(row == col, w, 0.0)


def _block_diag_out(c):
    w = jnp.tile(c.reshape(SSM_GB, SSM_CH, SSM_P), (1, 1, 16))
    row = lax.broadcasted_iota(jnp.int32, (1, SSM_CH, SSM_ST), 1) // SSM_H
    col = lax.broadcasted_iota(jnp.int32, (1, SSM_CH, SSM_ST), 2) // SSM_P
    return jnp.where(row == col, w, 0.0)


def _diag_blocks(full):
    w = full.reshape(SSM_GB, 16, SSM_H, 16, SSM_P)
    idx = jnp.arange(16)
    return w[:, idx, :, idx, :].transpose(1, 0, 2, 3).reshape(SSM_G, SSM_H, SSM_P)


def _permute_rows(a):
    t, c = a.shape
    return a.reshape(N_SEG, t // N_SEG, c).transpose(1, 0, 2).reshape(t, c)


def _unpermute_rows(a):
    t, c = a.shape
    return a.reshape(t // N_SEG, N_SEG, c).transpose(1, 0, 2).reshape(t, c)


def _scan_fwd(src_ref, dst_ref, lam_r_ref, lam_i_ref, init_ref, final_ref, steps):
    for k in range(SSM_ST // SCAN_LW):
        re = pl.ds(k * SCAN_LW, SCAN_LW)
        im = pl.ds(SSM_ST + k * SCAN_LW, SCAN_LW)
        lr, li = lam_r_ref[:, re], lam_i_ref[:, re]

        def step(i, carry, re=re, im=im, lr=lr, li=li):
            hr, hi = carry
            rows = pl.ds(pl.multiple_of(i * 8, 8), 8)
            nr = lr * hr - li * hi + src_ref[rows, re]
            ni = lr * hi + li * hr + src_ref[rows, im]
            if dst_ref is not None:
                dst_ref[rows, re] = nr
                dst_ref[rows, im] = ni
            return nr, ni

        hr, hi = lax.fori_loop(0, steps, step, (init_ref[:, re], init_ref[:, im]), unroll=4)
        final_ref[:, re] = hr
        final_ref[:, im] = hi


def _ssm_specs(t):
    col = lambda g: (0, g)
    gb3 = lambda g: (g, 0, 0)
    return dict(
        rows=pl.BlockSpec((t, SSM_CH), col),
        lam=pl.BlockSpec((None, N_SEG, SSM_ST), gb3),
        w_in=pl.BlockSpec((None, 2 * SSM_ST, SSM_CH), gb3),
        w_out=pl.BlockSpec((None, SSM_CH, 2 * SSM_ST), gb3),
        vec=pl.BlockSpec((1, SSM_CH), col),
    )


def _segment_states(x_ref, pw_r_ref, pw_i_ref, out_ref, reverse):
    re, im = pl.ds(0, SSM_ST), pl.ds(SSM_ST, SSM_ST)
    pr, pi = pw_r_ref[0:1, :], pw_i_ref[0:1, :]
    first = N_SEG - 1 if reverse else 0
    out_ref[first:first + 1, :] = jnp.zeros((1, 2 * SSM_ST), F32)
    order = range(N_SEG - 1, 0, -1) if reverse else range(N_SEG - 1)
    for s in order:
        d = s - 1 if reverse else s + 1
        hr, hi = out_ref[s:s + 1, re], out_ref[s:s + 1, im]
        if reverse:
            nr, ni = pr * hr + pi * hi, pr * hi - pi * hr
        else:
            nr, ni = pr * hr - pi * hi, pr * hi + pi * hr
        out_ref[d:d + 1, re] = nr + x_ref[s:s + 1, re]
        out_ref[d:d + 1, im] = ni + x_ref[s:s + 1, im]


def _ssm_fwd(up, lam_r, lam_i, pw_r, pw_i, w_in, w_out, d_skip):
    t = up.shape[0]
    nch = t // SCAN_ROWS
    steps = SCAN_ROWS // N_SEG
    sp = _ssm_specs(t)

    def body(u_ref, lr_ref, li_ref, pr_ref, pi_ref, wi_ref, wo_ref, d_ref, y_ref, hc_ref, bu_s, car_s, seg_s):
        def load_bu(j):
            rows = pl.ds(pl.multiple_of(j * SCAN_ROWS, SCAN_ROWS), SCAN_ROWS)
            bu_s[...] = _dot_nt(u_ref[rows, :].astype(BF16), wi_ref[...])

        car_s[...] = jnp.zeros_like(car_s)

        def chunk1(j, c):
            load_bu(j)
            _scan_fwd(bu_s, None, lr_ref, li_ref, car_s, car_s, steps)
            return c

        lax.fori_loop(0, nch, chunk1, 0)
        _segment_states(car_s, pr_ref, pi_ref, seg_s, reverse=False)
        car_s[...] = seg_s[...]

        def chunk2(j, c):
            load_bu(j)
            hc_ref[j] = car_s[...]
            _scan_fwd(bu_s, bu_s, lr_ref, li_ref, car_s, car_s, steps)
            rows = pl.ds(pl.multiple_of(j * SCAN_ROWS, SCAN_ROWS), SCAN_ROWS)
            y_ref[rows, :] = _dot_nt(bu_s[...].astype(BF16), wo_ref[...]) + d_ref[...] * u_ref[rows, :]
            return c

        lax.fori_loop(0, nch, chunk2, 0)

    return pl.pallas_call(
        body, name="ssm_fwd", grid=(SSM_GB,),
        in_specs=[sp["rows"], sp["lam"], sp["lam"], sp["lam"], sp["lam"], sp["w_in"], sp["w_out"], sp["vec"]],
        out_specs=[sp["rows"], pl.BlockSpec((None, nch, N_SEG, 2 * SSM_ST), lambda g: (g, 0, 0, 0))],
        out_shape=[jax.ShapeDtypeStruct((t, SSM_W), F32), jax.ShapeDtypeStruct((SSM_GB, nch, N_SEG, 2 * SSM_ST), F32)],
        scratch_shapes=[pltpu.VMEM((SCAN_ROWS, 2 * SSM_ST), F32), pltpu.VMEM((N_SEG, 2 * SSM_ST), F32),
                        pltpu.VMEM((N_SEG, 2 * SSM_ST), F32)],
        compiler_params=_params(("parallel",)),
    )(up, lam_r, lam_i, pw_r, pw_i, w_in, w_out, d_skip)


def _ssm_bwd(up, dyp, hc, lam_r, lam_i, pw_r, pw_i, w_in, w_out, d_skip):
    t = up.shape[0]
    nch = t // SCAN_ROWS
    steps = SCAN_ROWS // N_SEG
    sp = _ssm_specs(t)

    def body(u_ref, dy_ref, hc_ref, lr_ref, li_ref, pr_ref, pi_ref, wi_ref, wo_ref, d_ref,
             du_ref, gwi_ref, gwo_ref, glam_ref, gd_ref, bu_s, h_s, e_s, car_s, seg_s, acc_s):
        def chunk_rows(j):
            return pl.ds(pl.multiple_of(j * SCAN_ROWS, SCAN_ROWS), SCAN_ROWS)

        def load_e(j):
            e_s[...] = _dot(dy_ref[chunk_rows(j), :].astype(BF16), wo_ref[...])

        def scan_rev(j, accumulate):
            for k in range(SSM_ST // SCAN_LW):
                re = pl.ds(k * SCAN_LW, SCAN_LW)
                im = pl.ds(SSM_ST + k * SCAN_LW, SCAN_LW)
                lr, li = lr_ref[:, re], li_ref[:, re]

                def step(ii, carry, re=re, im=im, lr=lr, li=li):
                    i = steps - 1 - ii
                    rows = pl.ds(pl.multiple_of(i * 8, 8), 8)
                    if accumulate:
                        gr, gi, ar, ai = carry
                    else:
                        gr, gi = carry
                    nr = lr * gr + li * gi + e_s[rows, re]
                    ni = lr * gi - li * gr + e_s[rows, im]
                    if not accumulate:
                        return nr, ni
                    e_s[rows, re] = nr
                    e_s[rows, im] = ni
                    pr_, pi_ = h_s[rows, re], h_s[rows, im]
                    return nr, ni, ar + nr * pr_ + ni * pi_, ai + ni * pr_ - nr * pi_

                init = (car_s[:, re], car_s[:, im])
                if accumulate:
                    init = init + (acc_s[:, re], acc_s[:, im])
                out = lax.fori_loop(0, steps, step, init, unroll=4)
                car_s[:, re] = out[0]
                car_s[:, im] = out[1]
                if accumulate:
                    acc_s[:, re] = out[2]
                    acc_s[:, im] = out[3]

        car_s[...] = jnp.zeros_like(car_s)

        def pass1(jj, c):
            load_e(nch - 1 - jj)
            scan_rev(nch - 1 - jj, False)
            return c

        lax.fori_loop(0, nch, pass1, 0)
        _segment_states(car_s, pr_ref, pi_ref, seg_s, reverse=True)
        car_s[...] = seg_s[...]
        acc_s[...] = jnp.zeros_like(acc_s)
        gwi_ref[...] = jnp.zeros_like(gwi_ref)
        gwo_ref[...] = jnp.zeros_like(gwo_ref)
        gd_ref[...] = jnp.zeros_like(gd_ref)

        def pass2(jj, c):
            j = nch - 1 - jj
            rows = chunk_rows(j)
            u = u_ref[rows, :]
            dy = dy_ref[rows, :]
            u_b, dy_b = u.astype(BF16), dy.astype(BF16)
            bu_s[...] = _dot_nt(u_b, wi_ref[...])
            h_s[0:N_SEG, :] = hc_ref[j]
            seg_s[...] = hc_ref[j]
            _scan_fwd(bu_s, h_s.at[pl.ds(N_SEG, SCAN_ROWS), :], lr_ref, li_ref, seg_s, seg_s, steps)
            load_e(j)
            scan_rev(j, True)
            g_b = e_s[...].astype(BF16)
            du_ref[rows, :] = (_dot(g_b, wi_ref[...]) + d_ref[...] * dy).astype(du_ref.dtype)
            gwi_ref[...] += _dot_tn(u_b, g_b)
            gwo_ref[...] += _dot_tn(dy_b, h_s[pl.ds(N_SEG, SCAN_ROWS), :].astype(BF16))
            gd_ref[...] += jnp.sum(dy * u, axis=0, keepdims=True)
            return c

        lax.fori_loop(0, nch, pass2, 0)
        glam_ref[...] = jnp.sum(acc_s[...], axis=0, keepdims=True)

    mat = pl.BlockSpec((None, SSM_CH, 2 * SSM_ST), lambda g: (g, 0, 0))
    return pl.pallas_call(
        body, name="ssm_bwd", grid=(SSM_GB,),
        in_specs=[sp["rows"], sp["rows"], pl.BlockSpec((None, nch, N_SEG, 2 * SSM_ST), lambda g: (g, 0, 0, 0)),
                  sp["lam"], sp["lam"], sp["lam"], sp["lam"], sp["w_in"], sp["w_out"], sp["vec"]],
        out_specs=[sp["rows"], mat, mat, pl.BlockSpec((None, 1, 2 * SSM_ST), lambda g: (g, 0, 0)), sp["vec"]],
        out_shape=[jax.ShapeDtypeStruct((t, SSM_W), BF16), jax.ShapeDtypeStruct((SSM_GB, SSM_CH, 2 * SSM_ST), F32),
                   jax.ShapeDtypeStruct((SSM_GB, SSM_CH, 2 * SSM_ST), F32),
                   jax.ShapeDtypeStruct((SSM_GB, 1, 2 * SSM_ST), F32), jax.ShapeDtypeStruct((1, SSM_W), F32)],
        scratch_shapes=[pltpu.VMEM((SCAN_ROWS, 2 * SSM_ST), F32), pltpu.VMEM((SCAN_ROWS + N_SEG, 2 * SSM_ST), F32),
                        pltpu.VMEM((SCAN_ROWS, 2 * SSM_ST), F32), pltpu.VMEM((N_SEG, 2 * SSM_ST), F32),
                        pltpu.VMEM((N_SEG, 2 * SSM_ST), F32), pltpu.VMEM((N_SEG, 2 * SSM_ST), F32)],
        compiler_params=_params(("parallel",)),
    )(up, dyp, hc, lam_r, lam_i, pw_r, pw_i, w_in, w_out, d_skip)


def _glu_fwd(y, zs, w_glu, b_glu):
    t = y.shape[0]
    tm = 512

    def body(y_ref, z_ref, w_ref, b_ref, o_ref, yg_ref):
        yg = _gelu(y_ref[...])
        yg_b = yg.astype(BF16)
        a = _dot(yg_b, w_ref[...]) + b_ref[...]
        o_ref[...] = yg * _sigmoid(a) * _silu(z_ref[...])
        yg_ref[...] = yg_b

    row = pl.BlockSpec((tm, SSM_W), lambda i: (i, 0))
    return pl.pallas_call(
        body, name="glu_fwd", grid=(t // tm,),
        in_specs=[row, row, pl.BlockSpec((SSM_W, SSM_W), lambda i: (0, 0)), pl.BlockSpec((1, SSM_W), lambda i: (0, 0))],
        out_specs=[row, row],
        out_shape=[jax.ShapeDtypeStruct((t, SSM_W), F32), jax.ShapeDtypeStruct((t, SSM_W), BF16)],
        compiler_params=_params(("parallel",)),
    )(y, zs, w_glu, b_glu)


def _glu_bwd(y, zs, dos, w_glu, b_glu):
    t = y.shape[0]
    tm = 512

    def body(y_ref, z_ref, do_ref, w_ref, b_ref, dy_ref, dz_ref, da_ref, gb_ref):
        @pl.when(pl.program_id(0) == 0)
        def _():
            gb_ref[...] = jnp.zeros_like(gb_ref)

        yv, z, do = y_ref[...], z_ref[...], do_ref[...]
        yg = _gelu(yv)
        sg = _sigmoid(_dot(yg.astype(BF16), w_ref[...]) + b_ref[...])
        dy2 = do * _silu(z)
        dz_ref[...] = (do * yg * sg * _dsilu(z)).astype(BF16)
        da = dy2 * yg * sg * (1.0 - sg)
        da_b = da.astype(BF16)
        da_ref[...] = da_b
        gb_ref[...] += jnp.sum(da, axis=0, keepdims=True)
        dyg = dy2 * sg + _dot_nt(da_b, w_ref[...])
        dy_ref[...] = dyg * _dgelu(yv)

    row = pl.BlockSpec((tm, SSM_W), lambda i: (i, 0))
    vec = pl.BlockSpec((1, SSM_W), lambda i: (0, 0))
    return pl.pallas_call(
        body, name="glu_bwd", grid=(t // tm,),
        in_specs=[row, row, row, pl.BlockSpec((SSM_W, SSM_W), lambda i: (0, 0)), vec],
        out_specs=[row, row, row, vec],
        out_shape=[jax.ShapeDtypeStruct((t, SSM_W), F32), jax.ShapeDtypeStruct((t, SSM_W), BF16),
                   jax.ShapeDtypeStruct((t, SSM_W), BF16), jax.ShapeDtypeStruct((1, SSM_W), F32)],
        compiler_params=_params(("arbitrary",)),
    )(y, zs, dos, w_glu, b_glu)


def _rms(o):
    return lax.rsqrt(jnp.mean(o * o, axis=1, keepdims=True) + NORM_EPS)


def _outproj(oa, os_, aw, sw, w_out, x, target):
    t = x.shape[0]
    tm = 256

    def body(oa_ref, os_ref, aw_ref, sw_ref, w_ref, x_ref, t_ref, mg_ref, do_ref, ls_ref):
        @pl.when(pl.program_id(0) == 0)
        def _():
            ls_ref[...] = jnp.zeros_like(ls_ref)

        a, s = oa_ref[...], os_ref[...]
        merged = jnp.concatenate([a * _rms(a) * aw_ref[...], s * _rms(s) * sw_ref[...]], axis=1).astype(BF16)
        mg_ref[...] = merged
        err = x_ref[...] + _dot(merged, w_ref[...]) - t_ref[...]
        do_ref[...] = err * (1.0 / D_MODEL)
        ls_ref[...] += jnp.sum(err * err)

    half = pl.BlockSpec((tm, ATTN_W), lambda i: (i, 0))
    full = pl.BlockSpec((tm, D_MODEL), lambda i: (i, 0))
    vec = pl.BlockSpec((1, ATTN_W), lambda i: (0, 0))
    return pl.pallas_call(
        body, name="outproj", grid=(t // tm,),
        in_specs=[half, half, vec, vec, pl.BlockSpec((D_MODEL, D_MODEL), lambda i: (0, 0)), full, full],
        out_specs=[full, full, pl.BlockSpec((8, 128), lambda i: (0, 0))],
        out_shape=[jax.ShapeDtypeStruct((t, D_MODEL), BF16), jax.ShapeDtypeStruct((t, D_MODEL), F32),
                   jax.ShapeDtypeStruct((8, 128), F32)],
        compiler_params=_params(("arbitrary",)),
    )(oa, os_, aw, sw, w_out, x, target)


def _outproj_bwd(dout, oa, os_, aw, sw, w_out):
    t = dout.shape[0]
    tm = 256

    def norm_bwd(o, w, dm):
        r = _rms(o)
        yh = o * r
        gh = dm * w
        return r * (gh - yh * jnp.mean(gh * yh, axis=1, keepdims=True)), jnp.sum(dm * yh, axis=0, keepdims=True)

    def body(do_ref, oa_ref, os_ref, aw_ref, sw_ref, w_ref, da_ref, ds_ref, ga_ref, gs_ref):
        @pl.when(pl.program_id(0) == 0)
        def _():
            ga_ref[...] = jnp.zeros_like(ga_ref)
            gs_ref[...] = jnp.zeros_like(gs_ref)

        dm = _dot_nt(do_ref[...].astype(BF16), w_ref[...])
        da, ga = norm_bwd(oa_ref[...], aw_ref[...], dm[:, :ATTN_W])
        ds, gs = norm_bwd(os_ref[...], sw_ref[...], dm[:, ATTN_W:])
        da_ref[...] = da
        ds_ref[...] = ds
        ga_ref[...] += ga
        gs_ref[...] += gs

    half = pl.BlockSpec((tm, ATTN_W), lambda i: (i, 0))
    full = pl.BlockSpec((tm, D_MODEL), lambda i: (i, 0))
    vec = pl.BlockSpec((1, ATTN_W), lambda i: (0, 0))
    return pl.pallas_call(
        body, name="outproj_bwd", grid=(t // tm,),
        in_specs=[full, half, half, vec, vec, pl.BlockSpec((D_MODEL, D_MODEL), lambda i: (0, 0))],
        out_specs=[half, half, vec, vec],
        out_shape=[jax.ShapeDtypeStruct((t, ATTN_W), F32), jax.ShapeDtypeStruct((t, ATTN_W), F32),
                   jax.ShapeDtypeStruct((1, ATTN_W), F32), jax.ShapeDtypeStruct((1, ATTN_W), F32)],
        compiler_params=_params(("arbitrary",)),
    )(dout, oa, os_, aw, sw, w_out)


def _inproj_bwd(dproj, w_slabs, x, norm_w, dout):
    t = x.shape[0]
    tm = 512

    def body(dp_ref, w_ref, x_ref, nw_ref, do_ref, gx_ref, gw_ref, acc_ref):
        i, j = pl.program_id(0), pl.program_id(1)

        @pl.when((i == 0) & (j == 0))
        def _():
            gw_ref[...] = jnp.zeros_like(gw_ref)

        @pl.when(j == 0)
        def _():
            acc_ref[...] = jnp.zeros_like(acc_ref)

        acc_ref[...] += _dot_nt(dp_ref[...], w_ref[...])

        @pl.when(j == 3)
        def _():
            xv = x_ref[...]
            r = lax.rsqrt(jnp.mean(xv * xv, axis=1, keepdims=True) + NORM_EPS)
            yh = xv * r
            dh = acc_ref[...]
            gh = dh * nw_ref[...]
            gx_ref[...] = do_ref[...] + r * (gh - yh * jnp.mean(gh * yh, axis=1, keepdims=True))
            gw_ref[...] += jnp.sum(dh * yh, axis=0, keepdims=True)

    full = pl.BlockSpec((tm, D_MODEL), lambda i, j: (i, 0))
    vec = pl.BlockSpec((1, D_MODEL), lambda i, j: (0, 0))
    return pl.pallas_call(
        body, name="inproj_bwd", grid=(t // tm, 4),
        in_specs=[pl.BlockSpec((tm, SHARD_W), lambda i, j: (i, j)),
                  pl.BlockSpec((None, D_MODEL, SHARD_W), lambda i, j: (j, 0, 0)), full, vec, full],
        out_specs=[full, vec],
        out_shape=[jax.ShapeDtypeStruct((t, D_MODEL), F32), jax.ShapeDtypeStruct((1, D_MODEL), F32)],
        scratch_shapes=[pltpu.VMEM((tm, D_MODEL), F32)],
        compiler_params=_params(("arbitrary", "arbitrary")),
    )(dproj, w_slabs, x, norm_w.reshape(1, D_MODEL), dout)


def _adamw_math(w_ref, g_ref, m_ref, v_ref, d_ref, nm_ref, nv_ref):
    gv = g_ref[...]
    nm = ADAM_B1 * m_ref[...] + (1.0 - ADAM_B1) * gv
    nv = ADAM_B2 * v_ref[...] + (1.0 - ADAM_B2) * (gv * gv)
    m_hat = nm / (1.0 - ADAM_B1 ** ADAM_STEP)
    v_hat = nv / (1.0 - ADAM_B2 ** ADAM_STEP)
    d_ref[...] = -ADAM_LR * (m_hat / (jnp.sqrt(v_hat) + ADAM_EPS) + ADAM_WD * w_ref[...])
    nm_ref[...] = nm
    nv_ref[...] = nv


def _adamw_halves(w, mine, theirs, m, v, c_idx, *, rows, name):
    hr, cols = mine.shape
    nblk = hr // rows

    def body(c_ref, w_ref, a_ref, b_ref, m_ref, v_ref, g_ref, d_ref, nm_ref, nv_ref):
        g_ref[...] = jnp.where(pl.program_id(0) == c_ref[0], a_ref[...], b_ref[...])
        _adamw_math(w_ref, g_ref, m_ref, v_ref, d_ref, nm_ref, nv_ref)

    full = pl.BlockSpec((rows, cols), lambda h, i, c: (h * nblk + i, 0))
    part = pl.BlockSpec((rows, cols), lambda h, i, c: (i, 0))
    shp = jax.ShapeDtypeStruct((2 * hr, cols), F32)
    return pl.pallas_call(
        body, name=name,
        grid_spec=pltpu.PrefetchScalarGridSpec(num_scalar_prefetch=1, grid=(2, nblk),
                                               in_specs=[full, part, part, full, full], out_specs=[full] * 4),
        out_shape=[shp] * 4, compiler_params=_params(("parallel", "parallel")),
    )(c_idx, w, mine, theirs, m, v)


def _adamw(w, g, m, v, *, rows, name):
    r, c = w.shape

    def body(w_ref, g_ref, m_ref, v_ref, d_ref, nm_ref, nv_ref):
        _adamw_math(w_ref, g_ref, m_ref, v_ref, d_ref, nm_ref, nv_ref)

    blk = pl.BlockSpec((rows, c), lambda i: (i, 0))
    shp = jax.ShapeDtypeStruct((r, c), F32)
    return pl.pallas_call(body, name=name, grid=(r // rows,), in_specs=[blk] * 4, out_specs=[blk] * 3,
                          out_shape=[shp] * 3, compiler_params=_params(("parallel",)))(w, g, m, v)


def _remote(src, dst, ssem, rsem, dev):
    return pltpu.make_async_remote_copy(src_ref=src, dst_ref=dst, send_sem=ssem, recv_sem=rsem, device_id=dev,
                                        device_id_type=pl.DeviceIdType.MESH)


def _mesh_pos():
    return lax.axis_index("x"), lax.axis_index("y"), lax.axis_index("c")


def _other_chips(x, y):
    return [(1 - x, y), (x, 1 - y), (1 - x, 1 - y)]


def _gather_weights(shards):
    nt = len(shards)
    halves = [s.shape[0] // 2 for s in shards]

    def body(*refs):
        sh, full = refs[:nt], refs[nt:2 * nt]
        ssem, rsem = refs[2 * nt:]
        x, y, c = _mesh_pos()
        me = 2 * x + y
        sib = (x, y, 1 - c)

        def half(i, which):
            return pl.ds(pl.multiple_of(which * halves[i], 8), halves[i])

        sends = []
        for i in range(nt):
            cp = _remote(sh[i], full[i].at[me], ssem.at[6 * nt + i], rsem.at[6 * nt + i], sib)
            cp.start()
            sends.append(cp)
        for k, (px, py) in enumerate(_other_chips(x, y)):
            for i in range(nt):
                cp = _remote(sh[i].at[half(i, c)], full[i].at[me, half(i, c)], ssem.at[k * nt + i],
                             rsem.at[k * nt + i], (px, py, c))
                cp.start()
                sends.append(cp)
        for k, (px, py) in enumerate(_other_chips(x, y)):
            slot = 2 * px + py
            for i in range(nt):
                landed = full[i].at[slot, half(i, c)]
                _remote(landed, landed, ssem.at[k * nt + i], rsem.at[k * nt + i], sib).wait_recv()
                cp = _remote(landed, landed, ssem.at[(3 + k) * nt + i], rsem.at[(3 + k) * nt + i], sib)
                cp.start()
                sends.append(cp)
        for k, (px, py) in enumerate(_other_chips(x, y)):
            slot = 2 * px + py
            for i in range(nt):
                passed = full[i].at[slot, half(i, 1 - c)]
                _remote(passed, passed, ssem.at[(3 + k) * nt + i], rsem.at[(3 + k) * nt + i], sib).wait_recv()
        for i in range(nt):
            own = full[i].at[me]
            _remote(own, own, ssem.at[6 * nt + i], rsem.at[6 * nt + i], sib).wait_recv()
        for cp in sends:
            cp.wait_send()

    return pl.pallas_call(
        body, name="gather_weights", in_specs=[ANY] * nt, out_specs=[ANY] * nt,
        out_shape=[jax.ShapeDtypeStruct((4,) + s.shape, s.dtype) for s in shards],
        scratch_shapes=[pltpu.SemaphoreType.DMA((7 * nt,)), pltpu.SemaphoreType.DMA((7 * nt,))],
    )(*shards)


def _pair_swap(arrays):
    nt = len(arrays)

    def body(*refs):
        src, dst = refs[:nt], refs[nt:2 * nt]
        ssem, rsem = refs[2 * nt:]
        x, y, c = _mesh_pos()
        cps = [_remote(src[i].at[:, 1 - c], dst[i], ssem.at[i], rsem.at[i], (x, y, 1 - c)) for i in range(nt)]
        for cp in cps:
            cp.start()
        for cp in cps:
            cp.wait_recv()
        for cp in cps:
            cp.wait_send()

    return pl.pallas_call(
        body, name="pair_swap", in_specs=[ANY] * nt, out_specs=[ANY] * nt,
        out_shape=[jax.ShapeDtypeStruct((4,) + a.shape[2:], a.dtype) for a in arrays],
        scratch_shapes=[pltpu.SemaphoreType.DMA((nt,)), pltpu.SemaphoreType.DMA((nt,))],
    )(*arrays)


def _half_swap(arrays):
    nt = len(arrays)

    def body(*refs):
        src, dst = refs[:nt], refs[nt:2 * nt]
        ssem, rsem = refs[2 * nt:]
        x, y, c = _mesh_pos()
        cps = [_remote(src[i], dst[i], ssem.at[i], rsem.at[i], (x, y, 1 - c)) for i in range(nt)]
        for cp in cps:
            cp.start()
        for cp in cps:
            cp.wait_recv()
        for cp in cps:
            cp.wait_send()

    return pl.pallas_call(
        body, name="half_swap", in_specs=[ANY] * nt, out_specs=[ANY] * nt,
        out_shape=[jax.ShapeDtypeStruct(a.shape, a.dtype) for a in arrays],
        scratch_shapes=[pltpu.SemaphoreType.DMA((nt,)), pltpu.SemaphoreType.DMA((nt,))],
    )(*arrays)


def _scatter_to_chips(arrays):
    nt = len(arrays)

    def body(*refs):
        src, dst = refs[:nt], refs[nt:2 * nt]
        ssem, rsem = refs[2 * nt:]
        x, y, c = _mesh_pos()
        me = 2 * x + y
        cps = []
        for k, (px, py) in enumerate(_other_chips(x, y)):
            for i in range(nt):
                cp = _remote(src[i].at[2 * px + py], dst[i].at[me], ssem.at[k * nt + i], rsem.at[k * nt + i],
                             (px, py, c))
                cp.start()
                cps.append(cp)
        for k, (px, py) in enumerate(_other_chips(x, y)):
            for i in range(nt):
                slot = dst[i].at[2 * px + py]
                _remote(slot, slot, ssem.at[k * nt + i], rsem.at[k * nt + i], (px, py, c)).wait_recv()
        for cp in cps:
            cp.wait_send()

    return pl.pallas_call(
        body, name="scatter_to_chips", in_specs=[ANY] * nt, out_specs=[ANY] * nt,
        out_shape=[jax.ShapeDtypeStruct(a.shape, a.dtype) for a in arrays],
        scratch_shapes=[pltpu.SemaphoreType.DMA((3 * nt,)), pltpu.SemaphoreType.DMA((3 * nt,))],
    )(*arrays)


def _flips():
    return [(dx, dy, dc) for dx in (0, 1) for dy in (0, 1) for dc in (0, 1) if (dx, dy, dc) != (0, 0, 0)]


def _exchange_slices(src, scatter, name):
    def body(src_ref, dst_ref, ssem, rsem, lsem):
        x, y, c = _mesh_pos()
        me = 4 * x + 2 * y + c
        local = pltpu.make_async_copy(src_ref.at[me] if scatter else src_ref, dst_ref.at[me], lsem)
        local.start()
        cps = []
        for k, (dx, dy, dc) in enumerate(_flips()):
            px, py, pc = jnp.bitwise_xor(x, dx), jnp.bitwise_xor(y, dy), jnp.bitwise_xor(c, dc)
            peer = 4 * px + 2 * py + pc
            cp = _remote(src_ref.at[peer] if scatter else src_ref, dst_ref.at[me], ssem.at[k], rsem.at[k],
                         (px, py, pc))
            cp.start()
            cps.append((cp, peer))
        for k, (cp, peer) in enumerate(cps):
            slot = dst_ref.at[peer]
            _remote(slot, slot, ssem.at[k], rsem.at[k], (x, y, c)).wait_recv()
        for cp, _ in cps:
            cp.wait_send()
        local.wait()

    return pl.pallas_call(
        body, name=name, in_specs=[ANY], out_specs=ANY,
        out_shape=jax.ShapeDtypeStruct((8,) + src.shape[-2:], src.dtype),
        scratch_shapes=[pltpu.SemaphoreType.DMA((7,)), pltpu.SemaphoreType.DMA((7,)), pltpu.SemaphoreType.DMA],
    )(src)


def _add_halves(g, recv, c_idx, *, rows, name):
    _, _, hr, cols = g.shape

    def body(c_ref, g_ref, r_ref, o_ref):
        o_ref[...] = (g_ref[...] + r_ref[...].astype(F32)).astype(BF16)

    return pl.pallas_call(
        body, name=name,
        grid_spec=pltpu.PrefetchScalarGridSpec(
            num_scalar_prefetch=1, grid=(4, hr // rows),
            in_specs=[pl.BlockSpec((None, None, rows, cols), lambda j, i, c: (j, c[0], i, 0)),
                      pl.BlockSpec((None, rows, cols), lambda j, i, c: (j, i, 0))],
            out_specs=pl.BlockSpec((None, rows, cols), lambda j, i, c: (j, i, 0))),
        out_shape=jax.ShapeDtypeStruct((4, hr, cols), BF16),
        compiler_params=_params(("parallel", "parallel")),
    )(c_idx, g, recv)


def _sum_chips(slots, own, chip_idx, *, rows, name):
    _, r, cols = slots.shape

    def body(me_ref, s0, s1, s2, s3, own_ref, o_ref):
        me = me_ref[0]
        mine = own_ref[...].astype(F32)
        acc = None
        for k, s in enumerate((s0, s1, s2, s3)):
            term = jnp.where(me == k, mine, s[...].astype(F32))
            acc = term if acc is None else acc + term
        o_ref[...] = acc

    def slot_spec(k):
        return pl.BlockSpec((None, rows, cols), lambda i, me: (jnp.where(me[0] == k, (k + 1) % 4, k), i, 0))

    return pl.pallas_call(
        body, name=name,
        grid_spec=pltpu.PrefetchScalarGridSpec(
            num_scalar_prefetch=1, grid=(r // rows,),
            in_specs=[slot_spec(k) for k in range(4)] + [pl.BlockSpec((None, rows, cols), lambda i, me: (me[0], i, 0))],
            out_specs=pl.BlockSpec((rows, cols), lambda i, me: (i, 0))),
        out_shape=jax.ShapeDtypeStruct((r, cols), F32),
        compiler_params=_params(("parallel",)),
    )(chip_idx, slots, slots, slots, slots, own)


def _sum_slots(slots, *, rows, name):
    n, r, cols = slots.shape

    def body(s_ref, o_ref):
        acc = s_ref[0].astype(F32)
        for k in range(1, n):
            acc = acc + s_ref[k].astype(F32)
        o_ref[...] = acc

    return pl.pallas_call(
        body, name=name, grid=(r // rows,),
        in_specs=[pl.BlockSpec((n, rows, cols), lambda i: (0, i, 0))],
        out_specs=pl.BlockSpec((rows, cols), lambda i: (i, 0)),
        out_shape=jax.ShapeDtypeStruct((r, cols), F32),
        compiler_params=_params(("parallel",)),
    )(slots)


def _pack_small(d, names, rows):
    flat = jnp.concatenate([d[n].astype(F32).reshape(-1) for n in names])
    return jnp.pad(flat, (0, rows * 128 - flat.shape[0])).reshape(rows, 128)


def _unpack_small(p, names):
    flat = p.reshape(-1)
    out, off = {}, 0
    for n in names:
        size = math.prod(SMALL_SHAPES[n])
        out[n] = flat[off:off + size].reshape(SMALL_SHAPES[n])
        off += size
    return out


def _adamw_3d(w, g, m, v, *, name):
    def body(w_ref, g_ref, m_ref, v_ref, d_ref, nm_ref, nv_ref):
        _adamw_math(w_ref, g_ref, m_ref, v_ref, d_ref, nm_ref, nv_ref)

    blk = pl.BlockSpec((8,) + w.shape[1:], lambda i: (i, 0, 0))
    shp = jax.ShapeDtypeStruct(w.shape, F32)
    return pl.pallas_call(body, name=name, grid=(w.shape[0] // 8,), in_specs=[blk] * 4, out_specs=[blk] * 3,
                          out_shape=[shp] * 3, compiler_params=_params(("parallel",)))(w, g, m, v)


def kernel(x, positions, norm_w, w_in, q_norm_w, k_norm_w, sinks, a_re, a_im, log_step, b_re, b_im, c_re, c_im, d_skip, w_glu, b_glu, attn_out_norm_w, ssm_out_norm_w, w_out, loss_target, m_norm_w, m_w_in, m_q_norm_w, m_k_norm_w, m_sinks, m_a_re, m_a_im, m_log_step, m_b_re, m_b_im, m_c_re, m_c_im, m_d_skip, m_w_glu, m_b_glu, m_attn_out_norm_w, m_ssm_out_norm_w, m_w_out, v_norm_w, v_w_in, v_q_norm_w, v_k_norm_w, v_sinks, v_a_re, v_a_im, v_log_step, v_b_re, v_b_im, v_c_re, v_c_im, v_d_skip, v_w_glu, v_b_glu, v_attn_out_norm_w, v_ssm_out_norm_w, v_w_out):
    small_w = dict(norm_w=norm_w, q_norm_w=q_norm_w, k_norm_w=k_norm_w, sinks=sinks, a_re=a_re, a_im=a_im,
                   log_step=log_step, b_re=b_re, b_im=b_im, c_re=c_re, c_im=c_im, d_skip=d_skip, b_glu=b_glu,
                   attn_out_norm_w=attn_out_norm_w, ssm_out_norm_w=ssm_out_norm_w)
    small_m = dict(norm_w=m_norm_w, q_norm_w=m_q_norm_w, k_norm_w=m_k_norm_w, sinks=m_sinks, a_re=m_a_re, a_im=m_a_im,
                   log_step=m_log_step, b_re=m_b_re, b_im=m_b_im, c_re=m_c_re, c_im=m_c_im, d_skip=m_d_skip,
                   b_glu=m_b_glu, attn_out_norm_w=m_attn_out_norm_w, ssm_out_norm_w=m_ssm_out_norm_w)
    small_v = dict(norm_w=v_norm_w, q_norm_w=v_q_norm_w, k_norm_w=v_k_norm_w, sinks=v_sinks, a_re=v_a_re, a_im=v_a_im,
                   log_step=v_log_step, b_re=v_b_re, b_im=v_b_im, c_re=v_c_re, c_im=v_c_im, d_skip=v_d_skip,
                   b_glu=v_b_glu, attn_out_norm_w=v_attn_out_norm_w, ssm_out_norm_w=v_ssm_out_norm_w)
    c_idx = lax.axis_index("c").astype(jnp.int32).reshape(1)
    chip_idx = (2 * lax.axis_index("x") + lax.axis_index("y")).astype(jnp.int32).reshape(1)

    xs = x[0]
    tgt = loss_target[0]
    t = xs.shape[0]
    posf = positions[0].astype(F32).reshape(t, 1)

    w_in_all, w_glu_all, w_out_all = _gather_weights([w_in.astype(BF16), w_glu.astype(BF16), w_out.astype(BF16)])
    w_glu_b = w_glu_all.reshape(SSM_W, SSM_W)
    w_out_b = w_out_all.reshape(D_MODEL, D_MODEL)

    proj, hn = _inproj(xs, norm_w, w_in_all)
    inv_freq = ROPE_THETA ** (-jnp.arange(0, HEAD_DIM, 2, dtype=F32) / HEAD_DIM)
    invf = jnp.tile(inv_freq, 4).reshape(1, 128)
    qw = jnp.tile(q_norm_w, 2).reshape(1, 128)
    kw = jnp.tile(k_norm_w, 2).reshape(1, 128)
    sink_row = sinks.reshape(1, N_HEADS)
    oa = _attn_fwd(proj, posf, invf, qw, kw, sink_row)

    lam_r, lam_i, pw_r, pw_i, bb_r, bb_i = _ssm_prep(a_re, a_im, log_step, b_re, b_im, t // N_SEG)
    rows8 = lambda a: jnp.broadcast_to(a.reshape(SSM_GB, 1, SSM_ST), (SSM_GB, N_SEG, SSM_ST))
    lam_r8, lam_i8, pw_r8, pw_i8 = rows8(lam_r), rows8(lam_i), rows8(pw_r), rows8(pw_i)
    ssm_w_in = jnp.concatenate([_block_diag_in(bb_r), _block_diag_in(bb_i)], axis=1).astype(BF16)
    ssm_w_out = jnp.concatenate([_block_diag_out(c_re), _block_diag_out(-c_im)], axis=2).astype(BF16)
    d_row = d_skip.reshape(1, SSM_W)
    uz = _permute_rows(proj[:, 2560:])
    up, zsp = uz[:, :SSM_W], uz[:, SSM_W:]
    yp, hc = _ssm_fwd(up, lam_r8, lam_i8, pw_r8, pw_i8, ssm_w_in, ssm_w_out, d_row)
    b_glu_row = b_glu.reshape(1, SSM_W)
    osp, ygp = _glu_fwd(yp, zsp, w_glu_b, b_glu_row)
    os_ = _unpermute_rows(osp)
    aw = attn_out_norm_w.reshape(1, ATTN_W)
    sw = ssm_out_norm_w.reshape(1, SSM_W)
    merged, dout, sq_err = _outproj(oa, os_, aw, sw, w_out_b, xs, tgt)
    loss = lax.psum(0.5 * sq_err[0, 0] / D_MODEL, MESH_AXES)

    doa, dos, g_aw, g_sw = _outproj_bwd(dout, oa, os_, aw, sw, w_out_b)
    dout_b = dout.astype(BF16)
    g_w_out, g_w_out_b = _matmul_tn(merged, dout_b, tm=512, tn=1024, tk=1024, name="grad_w_out")
    dyp, dzsp, dap, g_b_glu = _glu_bwd(yp, zsp, _permute_rows(dos), w_glu_b, b_glu_row)
    g_w_glu, g_w_glu_b = _matmul_tn(ygp, dap, tm=512, tn=1024, tk=1024, name="grad_w_glu")
    dup, g_wi, g_wo, g_lam, g_d = _ssm_bwd(up, dyp, hc, lam_r8, lam_i8, pw_r8, pw_i8, ssm_w_in, ssm_w_out, d_row)
    dq, dk, dv, dza, g_qw, g_kw, g_sink = _attn_bwd(proj, posf, invf, qw, kw, sink_row, doa)
    duz = _unpermute_rows(jnp.concatenate([dup, dzsp], axis=1))
    dproj = jnp.concatenate([dq, dk, dv, dza, duz], axis=1)
    grad_x, g_nw = _inproj_bwd(dproj, w_in_all, xs, norm_w, dout)
    g_w_in, g_w_in_b = _matmul_tn(hn, dproj, tm=512, tn=SHARD_W, tk=1024, name="grad_w_in", slabs=True)

    g_bb_r = _diag_blocks(g_wi[:, :, :SSM_ST]).transpose(0, 2, 1).reshape(SSM_G, SSM_P * SSM_H)
    g_bb_i = _diag_blocks(g_wi[:, :, SSM_ST:]).transpose(0, 2, 1).reshape(SSM_G, SSM_P * SSM_H)
    g_a_re, g_a_im, g_ls, g_b_re, g_b_im = _ssm_param_grads(
        a_re, a_im, log_step, b_re, b_im, g_lam[:, 0, :SSM_ST].reshape(SSM_G, SSM_P),
        g_lam[:, 0, SSM_ST:].reshape(SSM_G, SSM_P), g_bb_r, g_bb_i)
    small_g = dict(
        norm_w=g_nw, q_norm_w=g_qw[0, :64] + g_qw[0, 64:], k_norm_w=g_kw[0, :64] + g_kw[0, 64:],
        sinks=g_sink[0, :N_HEADS], a_re=g_a_re, a_im=g_a_im, log_step=g_ls, b_re=g_b_re, b_im=g_b_im,
        c_re=_diag_blocks(g_wo[:, :, :SSM_ST]), c_im=-_diag_blocks(g_wo[:, :, SSM_ST:]), d_skip=g_d,
        b_glu=g_b_glu, attn_out_norm_w=g_aw, ssm_out_norm_w=g_sw)

    shapes = [(4, 2, D_MODEL // 2, SHARD_W), (4, 2, 128, SSM_W), (4, 2, 256, D_MODEL)]
    parts = [g.reshape(s) for g, s in zip((g_w_in, g_w_glu, g_w_out), shapes)]
    from_sib = _pair_swap([g.reshape(s) for g, s in zip((g_w_in_b, g_w_glu_b, g_w_out_b), shapes)])
    pair = [_add_halves(p, r, c_idx, rows=128, name=f"pair_sum_{i}") for i, (p, r) in enumerate(zip(parts, from_sib))]
    slots = _scatter_to_chips(pair)
    mine = [_sum_chips(s, p, chip_idx, rows=128, name=f"chip_sum_{i}") for i, (s, p) in enumerate(zip(slots, pair))]
    theirs = _half_swap(mine)
    packed = _pack_small(small_g, SMALL, 8 * PACK_ROWS).reshape(8, PACK_ROWS, 128)
    summed = _sum_slots(_exchange_slices(packed, True, "small_scatter"), rows=PACK_ROWS, name="small_sum")
    small_red = _exchange_slices(summed, False, "small_gather").reshape(8 * PACK_ROWS, 128)

    big = [_adamw_halves(w_in, mine[0], theirs[0], m_w_in, v_w_in, c_idx, rows=256, name="adamw_w_in"),
           _adamw_halves(w_glu, mine[1], theirs[1], m_w_glu, v_w_glu, c_idx, rows=128, name="adamw_w_glu"),
           _adamw_halves(w_out, mine[2], theirs[2], m_w_out, v_w_out, c_idx, rows=256, name="adamw_w_out")]
    g_in_sh, g_glu_sh, g_out_sh = (b[0] for b in big)
    upd = [b[1:] for b in big]
    grads = _unpack_small(small_red, SMALL)
    flat_first = sum(math.prod(SMALL_SHAPES[n]) for n in SMALL_3D) // 128
    sd, sm, sv = _adamw(_pack_small(small_w, SMALL_FLAT, FLAT_ROWS), small_red[flat_first:flat_first + FLAT_ROWS],
                        _pack_small(small_m, SMALL_FLAT, FLAT_ROWS), _pack_small(small_v, SMALL_FLAT, FLAT_ROWS),
                        rows=FLAT_ROWS, name="adamw_small")
    deltas, new_m, new_v = (_unpack_small(a, SMALL_FLAT) for a in (sd, sm, sv))
    for n in SMALL_3D:
        deltas[n], new_m[n], new_v[n] = _adamw_3d(small_w[n], grads[n], small_m[n], small_v[n], name="adamw_" + n)
    grads.update(w_in=g_in_sh, w_glu=g_glu_sh, w_out=g_out_sh)
    for n, (d, m_, v_) in zip(("w_in", "w_glu", "w_out"), upd):
        deltas[n], new_m[n], new_v[n] = d, m_, v_
    order = ["norm_w", "w_in", "q_norm_w", "k_norm_w", "sinks", "a_re", "a_im", "log_step", "b_re", "b_im", "c_re",
             "c_im", "d_skip", "w_glu", "b_glu", "attn_out_norm_w", "ssm_out_norm_w", "w_out"]
    return (loss, grad_x[None], *[grads[n] for n in order], *[deltas[n] for n in order],
            *[new_m[n] for n in order], *[new_v[n] for n in order])
```

```python
import math

import jax
import jax.numpy as jnp
from jax import lax
from jax.experimental import pallas as pl
from jax.experimental.pallas import tpu as pltpu

F32 = jnp.float32
BF16 = jnp.bfloat16

D_MODEL = 2048
ATTN_W = 1024
SSM_W = 1024
HEAD_DIM = 64
N_HEADS = 16
N_KV_HEADS = 4
KV_W = 256
BLOCK = 128
IN_W = 4608
SHARD_W = IN_W // 4
ROPE_THETA = 10000.0
SSM_H = 16
SSM_G = 64
SSM_P = 64
NORM_EPS = 1e-6
ADAM_LR = 0.001
ADAM_B1 = 0.9
ADAM_B2 = 0.999
ADAM_EPS = 1e-08
ADAM_WD = 0.01
ADAM_STEP = 10

N_SEG = 8
SSM_GB = 4
SSM_CH = 256
SSM_ST = 1024
SCAN_ROWS = 256
SCAN_LW = 512
VMEM_LIMIT = 56 * 1024 * 1024
MESH_AXES = ("x", "y", "c")
ANY = pl.BlockSpec(memory_space=pl.ANY)

SMALL_3D = ("b_re", "b_im", "c_re", "c_im")
SMALL_FLAT = ("norm_w", "q_norm_w", "k_norm_w", "sinks", "a_re", "a_im", "log_step", "d_skip", "b_glu",
              "attn_out_norm_w", "ssm_out_norm_w")
SMALL = SMALL_3D + SMALL_FLAT
SMALL_SHAPES = {"norm_w": (2048,), "q_norm_w": (64,), "k_norm_w": (64,), "sinks": (16,), "a_re": (64, 64),
                "a_im": (64, 64), "log_step": (64,), "b_re": (64, 64, 16), "b_im": (64, 64, 16),
                "c_re": (64, 16, 64), "c_im": (64, 16, 64), "d_skip": (1024,), "b_glu": (1024,),
                "attn_out_norm_w": (1024,), "ssm_out_norm_w": (1024,)}
PACK_ROWS = 272
FLAT_ROWS = 120


def _params(sem=None):
    return pltpu.CompilerParams(dimension_semantics=sem, vmem_limit_bytes=VMEM_LIMIT)


def _dot(a, b):
    return jnp.dot(a, b, preferred_element_type=F32)


def _dot_nt(a, b):
    return lax.dot_general(a, b, (((1,), (1,)), ((), ())), preferred_element_type=F32)


def _dot_tn(a, b):
    return lax.dot_general(a, b, (((0,), (0,)), ((), ())), preferred_element_type=F32)


def _sigmoid(x):
    return 1.0 / (1.0 + jnp.exp(-x))


def _silu(x):
    return x * _sigmoid(x)


def _dsilu(x):
    s = _sigmoid(x)
    return s * (1.0 + x * (1.0 - s))


_GELU_C = math.sqrt(2.0 / math.pi)


def _gelu(x):
    return 0.5 * x * (1.0 + jnp.tanh(_GELU_C * (x + 0.044715 * x * x * x)))


def _dgelu(x):
    t = jnp.tanh(_GELU_C * (x + 0.044715 * x * x * x))
    return 0.5 * (1.0 + t) + 0.5 * x * (1.0 - t * t) * _GELU_C * (1.0 + 3.0 * 0.044715 * x * x)


def _matmul_tn(a, b, *, tm, tn, tk, name, slabs=False):
    k, m = a.shape
    _, n = b.shape
    nk = k // tk

    def body(a_ref, b_ref, o_ref, ob_ref, acc_ref):
        kk = pl.program_id(2)

        @pl.when(kk == 0)
        def _():
            acc_ref[...] = jnp.zeros_like(acc_ref)

        acc_ref[...] += _dot_tn(a_ref[...], b_ref[...])

        @pl.when(kk == nk - 1)
        def _():
            o_ref[...] = acc_ref[...]
            ob_ref[...] = acc_ref[...].astype(BF16)

    if slabs:
        out_spec = pl.BlockSpec((None, tm, tn), lambda i, j, kk: (j, i, 0))
        shape = (n // tn, m, tn)
    else:
        out_spec = pl.BlockSpec((tm, tn), lambda i, j, kk: (i, j))
        shape = (m, n)
    return pl.pallas_call(
        body, name=name, grid=(m // tm, n // tn, nk),
        in_specs=[pl.BlockSpec((tk, tm), lambda i, j, kk: (kk, i)), pl.BlockSpec((tk, tn), lambda i, j, kk: (kk, j))],
        out_specs=[out_spec, out_spec],
        out_shape=[jax.ShapeDtypeStruct(shape, F32), jax.ShapeDtypeStruct(shape, BF16)],
        scratch_shapes=[pltpu.VMEM((tm, tn), F32)],
        compiler_params=_params(("parallel", "parallel", "arbitrary")),
    )(a, b)


def _inproj(x, norm_w, w_slabs, later_shards):
    t = x.shape[0]
    tm = 512
    nt = len(later_shards)
    ni = t // tm

    def body(x_ref, nw_ref, w_ref, *rest):
        sh, (proj_ref, hn_ref), full = rest[:nt], rest[nt:nt + 2], rest[nt + 2:2 * nt + 2]
        hn_s, ssem, rsem = rest[2 * nt + 2:]
        i, j = pl.program_id(0), pl.program_id(1)
        start, wait = _bg_gather(sh, full, ssem, rsem)

        @pl.when((i == 0) & (j == 0))
        def _():
            start()

        @pl.when(j == 0)
        def _():
            xv = x_ref[...]
            r = lax.rsqrt(jnp.mean(xv * xv, axis=1, keepdims=True) + NORM_EPS)
            hn = (xv * r * nw_ref[...]).astype(BF16)
            hn_s[...] = hn
            hn_ref[...] = hn

        proj_ref[...] = _dot(hn_s[...], w_ref[...])

        @pl.when((i == ni - 1) & (j == 3))
        def _():
            wait()

    return pl.pallas_call(
        body, name="inproj", grid=(ni, 4),
        in_specs=[pl.BlockSpec((tm, D_MODEL), lambda i, j: (i, 0)), pl.BlockSpec((1, D_MODEL), lambda i, j: (0, 0)),
                  pl.BlockSpec((None, D_MODEL, SHARD_W), lambda i, j: (j, 0, 0))] + [ANY] * nt,
        out_specs=[pl.BlockSpec((tm, SHARD_W), lambda i, j: (i, j)), pl.BlockSpec((tm, D_MODEL), lambda i, j: (i, 0))]
        + [ANY] * nt,
        out_shape=[jax.ShapeDtypeStruct((t, IN_W), F32), jax.ShapeDtypeStruct((t, D_MODEL), BF16)]
        + [jax.ShapeDtypeStruct((4,) + s.shape, s.dtype) for s in later_shards],
        scratch_shapes=[pltpu.VMEM((tm, D_MODEL), BF16), pltpu.SemaphoreType.DMA((4 * nt,)),
                        pltpu.SemaphoreType.DMA((4 * nt,))],
        compiler_params=_params(("arbitrary", "arbitrary")),
    )(x, norm_w.reshape(1, D_MODEL), w_slabs, *later_shards)


def _lane128():
    return lax.broadcasted_iota(jnp.int32, (1, 128), 1)


def _head_sums(v):
    lo = _lane128() < 64
    s_lo = jnp.sum(jnp.where(lo, v, 0.0), axis=1, keepdims=True)
    s_hi = jnp.sum(jnp.where(lo, 0.0, v), axis=1, keepdims=True)
    return jnp.where(lo, s_lo, s_hi)


def _rot_half(t):
    first = (_lane128() % 64) < 32
    return jnp.where(first, -pltpu.roll(t, 96, 1), pltpu.roll(t, 32, 1))


def _prep_tile(t, w, cos, sin):
    r = lax.rsqrt(_head_sums(t * t) * (1.0 / HEAD_DIM) + NORM_EPS)
    tn = t * r * w
    return tn * cos + _rot_half(tn) * sin


def _prep_tile_bwd(t, w, cos, sin, g):
    r = lax.rsqrt(_head_sums(t * t) * (1.0 / HEAD_DIM) + NORM_EPS)
    d_tn = g * cos - _rot_half(g * sin)
    th = t * r
    dw = jnp.sum(d_tn * th, axis=0, keepdims=True)
    gh = d_tn * w
    m = _head_sums(gh * th) * (1.0 / HEAD_DIM)
    return r * (gh - th * m), dw


def _band_mask(n):
    qi = lax.broadcasted_iota(jnp.int32, (BLOCK, 2 * BLOCK), 0) + BLOCK
    ki = lax.broadcasted_iota(jnp.int32, (BLOCK, 2 * BLOCK), 1)
    rel = qi - ki
    return (rel >= 0) & (rel < BLOCK) & ((n > 0) | (ki >= BLOCK))


def _half_select(tile, half):
    lo = _lane128() < 64
    return jnp.where(lo if half == 0 else jnp.logical_not(lo), tile, 0.0)


def _stack_group(tiles, kv_half):
    rows = []
    for t in tiles:
        for half in range(2):
            piece = _half_select(t, half)
            rows.append(piece if half == kv_half else pltpu.roll(piece, 64, 1))
    return jnp.concatenate(rows, axis=0)


def _unstack_group(stacked, kv_half):
    tiles = []
    for i in range(2):
        acc = None
        for half in range(2):
            piece = _half_select(stacked[BLOCK * (2 * i + half):BLOCK * (2 * i + half + 1)], kv_half)
            piece = piece if half == kv_half else pltpu.roll(piece, 64, 1)
            acc = piece if acc is None else acc + piece
        tiles.append(acc)
    return tiles


def _attn_specs(nb):
    last = nb - 1
    qi = lambda n: (jnp.minimum(n, last), 0)
    prev = lambda n: jnp.maximum(n - 1, 0)
    cur = lambda n: jnp.minimum(n, last)
    specs = [
        pl.BlockSpec((BLOCK, ATTN_W), qi),
        pl.BlockSpec((BLOCK, KV_W), lambda n: (cur(n), 4)),
        pl.BlockSpec((BLOCK, KV_W), lambda n: (prev(n), 4)),
        pl.BlockSpec((BLOCK, KV_W), lambda n: (cur(n), 5)),
        pl.BlockSpec((BLOCK, KV_W), lambda n: (prev(n), 5)),
        pl.BlockSpec((BLOCK, 512), lambda n: (cur(n), 3)),
        pl.BlockSpec((BLOCK, 512), lambda n: (cur(n), 4)),
        pl.BlockSpec((BLOCK, 1), lambda n: (cur(n), 0)),
        pl.BlockSpec((BLOCK, 1), lambda n: (prev(n), 0)),
        pl.BlockSpec((1, 128), lambda n: (0, 0)),
        pl.BlockSpec((1, 128), lambda n: (0, 0)),
        pl.BlockSpec((1, 128), lambda n: (0, 0)),
        pl.BlockSpec((1, N_HEADS), lambda n: (0, 0)),
    ]
    return specs


def _attn_common(n, q_ref, kc_ref, kp_ref, vc_ref, vp_ref, pq_ref, pp_ref, invf_ref, qw_ref, kw_ref):
    invf = invf_ref[...]
    ang_q = pq_ref[...] * invf
    ang_p = pp_ref[...] * invf
    cos_q, sin_q = jnp.cos(ang_q), jnp.sin(ang_q)
    cos_k = jnp.concatenate([jnp.cos(ang_p), cos_q], axis=0)
    sin_k = jnp.concatenate([jnp.sin(ang_p), sin_q], axis=0)
    k_raw = jnp.concatenate([kp_ref[...], kc_ref[...]], axis=0)
    vv = jnp.concatenate([vp_ref[...], vc_ref[...]], axis=0).astype(BF16)
    kk = [_prep_tile(k_raw[:, 128 * i:128 * i + 128], kw_ref[...], cos_k, sin_k).astype(BF16) for i in range(2)]
    vt = [vv[:, 128 * i:128 * i + 128] for i in range(2)]
    qv = q_ref[...]
    qt = [_prep_tile(qv[:, 128 * i:128 * i + 128], qw_ref[...], cos_q, sin_q) for i in range(8)]
    return cos_q, sin_q, cos_k, sin_k, k_raw, kk, vt, qt


QK_SCALE = 1.0 / math.sqrt(HEAD_DIM)


def _group_sinks(sink_ref, g):
    return jnp.concatenate([jnp.broadcast_to(sink_ref[:, 4 * g + j:4 * g + j + 1], (BLOCK, 1)) for j in range(4)], axis=0)


def _group_softmax(q4, kk_t, sink, bias):
    s = _dot_nt(q4, kk_t) + bias
    m = jnp.maximum(jnp.max(s, axis=1, keepdims=True), sink)
    p = jnp.exp(s - m)
    es = jnp.exp(sink - m)
    inv = 1.0 / (jnp.sum(p, axis=1, keepdims=True) + es)
    return p * inv, es * inv


def _group_bias(n):
    return jnp.concatenate([jnp.where(_band_mask(n), 0.0, -1e30)] * 4, axis=0)


def _attn_fwd(proj, posf, invf, qw, kw, sinks):
    t = proj.shape[0]
    nb = t // BLOCK

    def body(q_ref, kc_ref, kp_ref, vc_ref, vp_ref, za0_ref, za1_ref, pq_ref, pp_ref, invf_ref, qw_ref, kw_ref,
             sink_ref, o_ref):
        n = pl.program_id(0)
        _, _, _, _, _, kk, vt, qt = _attn_common(n, q_ref, kc_ref, kp_ref, vc_ref, vp_ref, pq_ref, pp_ref, invf_ref,
                                                 qw_ref, kw_ref)
        bias = _group_bias(n)
        tiles = []
        for g in range(N_KV_HEADS):
            q4 = (_stack_group(qt[2 * g:2 * g + 2], g % 2) * QK_SCALE).astype(BF16)
            p, _ = _group_softmax(q4, kk[g // 2], _group_sinks(sink_ref, g), bias)
            tiles += _unstack_group(_dot(p.astype(BF16), vt[g // 2]), g % 2)
        za = jnp.concatenate([za0_ref[...], za1_ref[...]], axis=1)
        o_ref[...] = jnp.concatenate(tiles, axis=1) * _silu(za)

    return pl.pallas_call(
        body, name="attn_fwd", grid=(nb,), in_specs=_attn_specs(nb),
        out_specs=pl.BlockSpec((BLOCK, ATTN_W), lambda n: (n, 0)),
        out_shape=jax.ShapeDtypeStruct((t, ATTN_W), F32),
        compiler_params=_params(("parallel",)),
    )(proj, proj, proj, proj, proj, proj, proj, posf, posf, invf, qw, kw, sinks)


def _attn_bwd(proj, posf, invf, qw, kw, sinks, doa, outgoing):
    t = proj.shape[0]
    nb = t // BLOCK
    last = nb - 1
    nt = len(outgoing)

    def body(q_ref, kc_ref, kp_ref, vc_ref, vp_ref, za0_ref, za1_ref, pq_ref, pp_ref, invf_ref, qw_ref, kw_ref,
             sink_ref, doa_ref, *rest):
        src = rest[:nt]
        dq_ref, dk_ref, dv_ref, dza_ref, gq_ref, gk_ref, gs_ref = rest[nt:nt + 7]
        dst = rest[nt + 7:2 * nt + 7]
        dkk_s, dvv_s, ck_s, cv_s, ssem, rsem = rest[2 * nt + 7:]
        n = pl.program_id(0)
        start, wait = _bg_scatter_devices(src, dst, ssem, rsem)

        @pl.when(n == 0)
        def _():
            start()
            gq_ref[...] = jnp.zeros_like(gq_ref)
            gk_ref[...] = jnp.zeros_like(gk_ref)
            gs_ref[...] = jnp.zeros_like(gs_ref)
            ck_s[...] = jnp.zeros_like(ck_s)
            cv_s[...] = jnp.zeros_like(cv_s)

        @pl.when(n == nb)
        def _():
            dkk_s[...] = jnp.zeros_like(dkk_s)
            dvv_s[...] = jnp.zeros_like(dvv_s)

        @pl.when(n < nb)
        def _():
            cos_q, sin_q, _, _, _, kk, vt, qt = _attn_common(n, q_ref, kc_ref, kp_ref, vc_ref, vp_ref, pq_ref, pp_ref,
                                                             invf_ref, qw_ref, kw_ref)
            bias = _group_bias(n)
            za = jnp.concatenate([za0_ref[...], za1_ref[...]], axis=1)
            doa_v = doa_ref[...]
            do_full = doa_v * _silu(za)
            o_tiles, dq_tiles = [], []
            dkk = [jnp.zeros((2 * BLOCK, 128), F32) for _ in range(2)]
            dvv = [jnp.zeros((2 * BLOCK, 128), F32) for _ in range(2)]
            gsink = jnp.zeros((1, 128), F32)
            lane = _lane128()
            for g in range(N_KV_HEADS):
                q_b = (_stack_group(qt[2 * g:2 * g + 2], g % 2) * QK_SCALE).astype(BF16)
                do_b = _stack_group([do_full[:, 128 * i:128 * i + 128] for i in (2 * g, 2 * g + 1)], g % 2).astype(BF16)
                p, psink = _group_softmax(q_b, kk[g // 2], _group_sinks(sink_ref, g), bias)
                p_b = p.astype(BF16)
                dp = _dot_nt(do_b, vt[g // 2])
                delta = jnp.sum(p * dp, axis=1, keepdims=True)
                ds_b = (p * (dp - delta)).astype(BF16)
                sd = psink * delta
                for j in range(4):
                    gsink = gsink + jnp.where(lane == 4 * g + j, -jnp.sum(sd[BLOCK * j:BLOCK * (j + 1)]), 0.0)
                o_tiles += _unstack_group(_dot(p_b, vt[g // 2]), g % 2)
                dq_tiles += [d * QK_SCALE for d in _unstack_group(_dot(ds_b, kk[g // 2]), g % 2)]
                dkk[g // 2] = dkk[g // 2] + _dot_tn(ds_b, q_b)
                dvv[g // 2] = dvv[g // 2] + _dot_tn(p_b, do_b)
            dza_ref[...] = (doa_v * jnp.concatenate(o_tiles, axis=1) * _dsilu(za)).astype(BF16)
            qv = q_ref[...]
            gq = jnp.zeros((1, 128), F32)
            out = []
            for i in range(8):
                d, dw = _prep_tile_bwd(qv[:, 128 * i:128 * i + 128], qw_ref[...], cos_q, sin_q, dq_tiles[i])
                out.append(d)
                gq = gq + dw
            dq_ref[...] = jnp.concatenate(out, axis=1).astype(BF16)
            gq_ref[...] += gq
            gs_ref[...] += gsink
            dkk_s[...] = jnp.concatenate(dkk, axis=1)
            dvv_s[...] = jnp.concatenate(dvv, axis=1)

        invf = invf_ref[...]
        ang_p = pp_ref[...] * invf
        cos_p, sin_p = jnp.cos(ang_p), jnp.sin(ang_p)
        dk_prev = ck_s[...] + dkk_s[0:BLOCK, :]
        kp = kp_ref[...]
        gk = jnp.zeros((1, 128), F32)
        out = []
        for i in range(2):
            d, dw = _prep_tile_bwd(kp[:, 128 * i:128 * i + 128], kw_ref[...], cos_p, sin_p,
                                   dk_prev[:, 128 * i:128 * i + 128])
            out.append(d)
            gk = gk + dw
        dk_ref[...] = jnp.concatenate(out, axis=1).astype(BF16)
        dv_ref[...] = (cv_s[...] + dvv_s[0:BLOCK, :]).astype(BF16)
        gk_ref[...] += gk
        ck_s[...] = dkk_s[BLOCK:2 * BLOCK, :]
        cv_s[...] = dvv_s[BLOCK:2 * BLOCK, :]

        @pl.when(n == nb)
        def _():
            wait()

    qblk = lambda n: (jnp.minimum(n, last), 0)
    kblk = lambda n: (jnp.maximum(n - 1, 0), 0)
    vec = pl.BlockSpec((1, 128), lambda n: (0, 0))
    return pl.pallas_call(
        body, name="attn_bwd", grid=(nb + 1,),
        in_specs=_attn_specs(nb) + [pl.BlockSpec((BLOCK, ATTN_W), qblk)] + [ANY] * nt,
        out_specs=[pl.BlockSpec((BLOCK, ATTN_W), qblk), pl.BlockSpec((BLOCK, KV_W), kblk),
                   pl.BlockSpec((BLOCK, KV_W), kblk), pl.BlockSpec((BLOCK, ATTN_W), qblk), vec, vec, vec] + [ANY] * nt,
        out_shape=[jax.ShapeDtypeStruct((t, ATTN_W), BF16), jax.ShapeDtypeStruct((t, KV_W), BF16),
                   jax.ShapeDtypeStruct((t, KV_W), BF16), jax.ShapeDtypeStruct((t, ATTN_W), BF16),
                   jax.ShapeDtypeStruct((1, 128), F32), jax.ShapeDtypeStruct((1, 128), F32),
                   jax.ShapeDtypeStruct((1, 128), F32)] + [jax.ShapeDtypeStruct(a.shape, a.dtype) for a in outgoing],
        scratch_shapes=[pltpu.VMEM((2 * BLOCK, KV_W), F32), pltpu.VMEM((2 * BLOCK, KV_W), F32),
                        pltpu.VMEM((BLOCK, KV_W), F32), pltpu.VMEM((BLOCK, KV_W), F32),
                        pltpu.SemaphoreType.DMA((7 * nt,)), pltpu.SemaphoreType.DMA((7 * nt,))],
        compiler_params=_params(("arbitrary",)),
    )(proj, proj, proj, proj, proj, proj, proj, posf, posf, invf, qw, kw, sinks, doa, *outgoing)


def _cmul(ar, ai, br, bi):
    return ar * br - ai * bi, ar * bi + ai * br


def _zoh(a_re, a_im, delta):
    e = jnp.exp(a_re * delta)
    lr, li = e * jnp.cos(a_im * delta), e * jnp.sin(a_im * delta)
    inv = 1.0 / (a_re * a_re + a_im * a_im)
    fr, fi = _cmul(lr - 1.0, li, a_re * inv, -a_im * inv)
    return lr, li, fr, fi


def _ssm_prep(a_re, a_im, log_step, b_re, b_im, seg_len):
    n_sq = int(round(math.log2(seg_len)))
    assert 2 ** n_sq == seg_len

    def body(ar_ref, ai_ref, ls_ref, arx_ref, aix_ref, br_ref, bi_ref, lr_ref, li_ref, pr_ref, pi_ref, bbr_ref, bbi_ref):
        delta = jnp.exp(ls_ref[...])
        lr, li, _, _ = _zoh(ar_ref[...], ai_ref[...], delta)
        lr_ref[...] = lr
        li_ref[...] = li
        pr, pi = lr, li
        for _ in range(n_sq):
            pr, pi = _cmul(pr, pi, pr, pi)
        pr_ref[...] = pr
        pi_ref[...] = pi
        _, _, fr, fi = _zoh(arx_ref[...], aix_ref[...], delta)
        bbr, bbi = _cmul(fr, fi, br_ref[...], bi_ref[...])
        bbr_ref[...] = bbr
        bbi_ref[...] = bbi

    gp = jax.ShapeDtypeStruct((SSM_G, SSM_P), F32)
    gx = jax.ShapeDtypeStruct((SSM_G, SSM_P * SSM_H), F32)
    return pl.pallas_call(body, name="ssm_prep", out_shape=[gp, gp, gp, gp, gx, gx])(
        a_re, a_im, log_step.reshape(SSM_G, 1), jnp.repeat(a_re, SSM_H, axis=1), jnp.repeat(a_im, SSM_H, axis=1),
        b_re.reshape(SSM_G, SSM_P * SSM_H), b_im.reshape(SSM_G, SSM_P * SSM_H))


def _ssm_param_grads(a_re, a_im, log_step, b_re, b_im, dlam_re, dlam_im, dbb_re, dbb_im):
    def body(ar_ref, ai_ref, ls_ref, arx_ref, aix_ref, br_ref, bi_ref, dlr_ref, dli_ref, dbr_ref, dbi_ref,
             gar_ref, gai_ref, gls_ref, gbr_ref, gbi_ref):
        delta = jnp.exp(ls_ref[...])
        ar, ai = ar_ref[...], ai_ref[...]
        lr, li, fr, fi = _zoh(ar, ai, delta)
        _, _, frx, fix = _zoh(arx_ref[...], aix_ref[...], delta)
        dbr, dbi = dbr_ref[...], dbi_ref[...]
        br, bi = br_ref[...], bi_ref[...]
        gbr, gbi = _cmul(frx, -fix, dbr, dbi)
        gbr_ref[...] = gbr
        gbi_ref[...] = gbi
        tr, ti = _cmul(br, -bi, dbr, dbi)
        row = lax.broadcasted_iota(jnp.int32, (SSM_P * SSM_H, SSM_P), 0)
        col = lax.broadcasted_iota(jnp.int32, (SSM_P * SSM_H, SSM_P), 1)
        fold = (row // SSM_H == col).astype(F32)
        dfr = jnp.dot(tr, fold, precision=lax.Precision.HIGHEST, preferred_element_type=F32)
        dfi = jnp.dot(ti, fold, precision=lax.Precision.HIGHEST, preferred_element_type=F32)
        inv = 1.0 / (ar * ar + ai * ai)
        ilr, ili = ar * inv, -ai * inv
        t1r, t1i = _cmul(dfr, dfi, ilr, -ili)
        dlbr, dlbi = dlr_ref[...] + t1r, dli_ref[...] + t1i
        qr, qi = _cmul(fr, fi, ilr, ili)
        t2r, t2i = _cmul(dfr, dfi, qr, -qi)
        glr, gli = -t2r, -t2i
        dzr, dzi = _cmul(dlbr, dlbi, lr, -li)
        gar_ref[...] = glr + dzr * delta
        gai_ref[...] = gli + dzi * delta
        gls_ref[...] = jnp.sum(dzr * ar + dzi * ai, axis=1, keepdims=True) * delta

    gp = jax.ShapeDtypeStruct((SSM_G, SSM_P), F32)
    gx = jax.ShapeDtypeStruct((SSM_G, SSM_P * SSM_H), F32)
    return pl.pallas_call(body, name="ssm_param_grads",
                          out_shape=[gp, gp, jax.ShapeDtypeStruct((SSM_G, 1), F32), gx, gx])(
        a_re, a_im, log_step.reshape(SSM_G, 1), jnp.repeat(a_re, SSM_H, axis=1), jnp.repeat(a_im, SSM_H, axis=1),
        b_re.reshape(SSM_G, SSM_P * SSM_H), b_im.reshape(SSM_G, SSM_P * SSM_H), dlam_re, dlam_im, dbb_re, dbb_im)


def _block_diag_in(bb):
    w = jnp.tile(bb.reshape(SSM_GB, SSM_ST, SSM_H), (1, 1, 16))
    row = lax.broadcasted_iota(jnp.int32, (1, SSM_ST, SSM_CH), 1) // SSM_P
    col = lax.broadcasted_iota(jnp.int32, (1, SSM_ST, SSM_CH), 2) // SSM_H
    return jnp.where(row == col, w, 0.0)


def _block_diag_out(c):
    w = jnp.tile(c.reshape(SSM_GB, SSM_CH, SSM_P), (1, 1, 16))
    row = lax.broadcasted_iota(jnp.int32, (1, SSM_CH, SSM_ST), 1) // SSM_H
    col = lax.broadcasted_iota(jnp.int32, (1, SSM_CH, SSM_ST), 2) // SSM_P
    return jnp.where(row == col, w, 0.0)


def _diag_blocks(full):
    w = full.reshape(SSM_GB, 16, SSM_H, 16, SSM_P)
    idx = jnp.arange(16)
    return w[:, idx, :, idx, :].transpose(1, 0, 2, 3).reshape(SSM_G, SSM_H, SSM_P)


def _permute_rows(a):
    t, c = a.shape
    return a.reshape(N_SEG, t // N_SEG, c).transpose(1, 0, 2).reshape(t, c)


def _unpermute_rows(a):
    t, c = a.shape
    return a.reshape(t // N_SEG, N_SEG, c).transpose(1, 0, 2).reshape(t, c)


def _scan_fwd(src_ref, dst_ref, lam_r_ref, lam_i_ref, init_ref, final_ref, steps):
    for k in range(SSM_ST // SCAN_LW):
        re = pl.ds(k * SCAN_LW, SCAN_LW)
        im = pl.ds(SSM_ST + k * SCAN_LW, SCAN_LW)
        lr, li = lam_r_ref[:, re], lam_i_ref[:, re]

        def step(i, carry, re=re, im=im, lr=lr, li=li):
            hr, hi = carry
            rows = pl.ds(pl.multiple_of(i * 8, 8), 8)
            nr = lr * hr - li * hi + src_ref[rows, re]
            ni = lr * hi + li * hr + src_ref[rows, im]
            if dst_ref is not None:
                dst_ref[rows, re] = nr
                dst_ref[rows, im] = ni
            return nr, ni

        hr, hi = lax.fori_loop(0, steps, step, (init_ref[:, re], init_ref[:, im]), unroll=4)
        final_ref[:, re] = hr
        final_ref[:, im] = hi


def _ssm_specs(t):
    col = lambda g: (0, g)
    gb3 = lambda g: (g, 0, 0)
    return dict(
        rows=pl.BlockSpec((t, SSM_CH), col),
        lam=pl.BlockSpec((None, N_SEG, SSM_ST), gb3),
        w_in=pl.BlockSpec((None, 2 * SSM_ST, SSM_CH), gb3),
        w_out=pl.BlockSpec((None, SSM_CH, 2 * SSM_ST), gb3),
        vec=pl.BlockSpec((1, SSM_CH), col),
    )


def _segment_states(x_ref, pw_r_ref, pw_i_ref, out_ref, reverse):
    re, im = pl.ds(0, SSM_ST), pl.ds(SSM_ST, SSM_ST)
    pr, pi = pw_r_ref[0:1, :], pw_i_ref[0:1, :]
    first = N_SEG - 1 if reverse else 0
    out_ref[first:first + 1, :] = jnp.zeros((1, 2 * SSM_ST), F32)
    order = range(N_SEG - 1, 0, -1) if reverse else range(N_SEG - 1)
    for s in order:
        d = s - 1 if reverse else s + 1
        hr, hi = out_ref[s:s + 1, re], out_ref[s:s + 1, im]
        if reverse:
            nr, ni = pr * hr + pi * hi, pr * hi - pi * hr
        else:
            nr, ni = pr * hr - pi * hi, pr * hi + pi * hr
        out_ref[d:d + 1, re] = nr + x_ref[s:s + 1, re]
        out_ref[d:d + 1, im] = ni + x_ref[s:s + 1, im]


def _ssm_fwd(up, lam_r, lam_i, pw_r, pw_i, w_in, w_out, d_skip):
    t = up.shape[0]
    nch = t // SCAN_ROWS
    steps = SCAN_ROWS // N_SEG
    sp = _ssm_specs(t)

    def body(u_ref, lr_ref, li_ref, pr_ref, pi_ref, wi_ref, wo_ref, d_ref, y_ref, hc_ref, bu_s, car_s, seg_s):
        def load_bu(j):
            rows = pl.ds(pl.multiple_of(j * SCAN_ROWS, SCAN_ROWS), SCAN_ROWS)
            bu_s[...] = _dot_nt(u_ref[rows, :].astype(BF16), wi_ref[...])

        car_s[...] = jnp.zeros_like(car_s)

        def chunk1(j, c):
            load_bu(j)
            _scan_fwd(bu_s, None, lr_ref, li_ref, car_s, car_s, steps)
            return c

        lax.fori_loop(0, nch, chunk1, 0)
        _segment_states(car_s, pr_ref, pi_ref, seg_s, reverse=False)
        car_s[...] = seg_s[...]

        def chunk2(j, c):
            load_bu(j)
            hc_ref[j] = car_s[...]
            _scan_fwd(bu_s, bu_s, lr_ref, li_ref, car_s, car_s, steps)
            rows = pl.ds(pl.multiple_of(j * SCAN_ROWS, SCAN_ROWS), SCAN_ROWS)
            y_ref[rows, :] = _dot_nt(bu_s[...].astype(BF16), wo_ref[...]) + d_ref[...] * u_ref[rows, :]
            return c

        lax.fori_loop(0, nch, chunk2, 0)

    return pl.pallas_call(
        body, name="ssm_fwd", grid=(SSM_GB,),
        in_specs=[sp["rows"], sp["lam"], sp["lam"], sp["lam"], sp["lam"], sp["w_in"], sp["w_out"], sp["vec"]],
        out_specs=[sp["rows"], pl.BlockSpec((None, nch, N_SEG, 2 * SSM_ST), lambda g: (g, 0, 0, 0))],
        out_shape=[jax.ShapeDtypeStruct((t, SSM_W), F32), jax.ShapeDtypeStruct((SSM_GB, nch, N_SEG, 2 * SSM_ST), F32)],
        scratch_shapes=[pltpu.VMEM((SCAN_ROWS, 2 * SSM_ST), F32), pltpu.VMEM((N_SEG, 2 * SSM_ST), F32),
                        pltpu.VMEM((N_SEG, 2 * SSM_ST), F32)],
        compiler_params=_params(("parallel",)),
    )(up, lam_r, lam_i, pw_r, pw_i, w_in, w_out, d_skip)


def _ssm_bwd(up, dyp, hc, lam_r, lam_i, pw_r, pw_i, w_in, w_out, d_skip):
    t = up.shape[0]
    nch = t // SCAN_ROWS
    steps = SCAN_ROWS // N_SEG
    sp = _ssm_specs(t)

    def body(u_ref, dy_ref, hc_ref, lr_ref, li_ref, pr_ref, pi_ref, wi_ref, wo_ref, d_ref,
             du_ref, gwi_ref, gwo_ref, glam_ref, gd_ref, bu_s, h_s, e_s, car_s, seg_s, acc_s):
        def chunk_rows(j):
            return pl.ds(pl.multiple_of(j * SCAN_ROWS, SCAN_ROWS), SCAN_ROWS)

        def load_e(j):
            e_s[...] = _dot(dy_ref[chunk_rows(j), :].astype(BF16), wo_ref[...])

        def scan_rev(j, accumulate):
            for k in range(SSM_ST // SCAN_LW):
                re = pl.ds(k * SCAN_LW, SCAN_LW)
                im = pl.ds(SSM_ST + k * SCAN_LW, SCAN_LW)
                lr, li = lr_ref[:, re], li_ref[:, re]

                def step(ii, carry, re=re, im=im, lr=lr, li=li):
                    i = steps - 1 - ii
                    rows = pl.ds(pl.multiple_of(i * 8, 8), 8)
                    if accumulate:
                        gr, gi, ar, ai = carry
                    else:
                        gr, gi = carry
                    nr = lr * gr + li * gi + e_s[rows, re]
                    ni = lr * gi - li * gr + e_s[rows, im]
                    if not accumulate:
                        return nr, ni
                    e_s[rows, re] = nr
                    e_s[rows, im] = ni
                    pr_, pi_ = h_s[rows, re], h_s[rows, im]
                    return nr, ni, ar + nr * pr_ + ni * pi_, ai + ni * pr_ - nr * pi_

                init = (car_s[:, re], car_s[:, im])
                if accumulate:
                    init = init + (acc_s[:, re], acc_s[:, im])
                out = lax.fori_loop(0, steps, step, init, unroll=4)
                car_s[:, re] = out[0]
                car_s[:, im] = out[1]
                if accumulate:
                    acc_s[:, re] = out[2]
                    acc_s[:, im] = out[3]

        car_s[...] = jnp.zeros_like(car_s)

        def pass1(jj, c):
            load_e(nch - 1 - jj)
            scan_rev(nch - 1 - jj, False)
            return c

        lax.fori_loop(0, nch, pass1, 0)
        _segment_states(car_s, pr_ref, pi_ref, seg_s, reverse=True)
        car_s[...] = seg_s[...]
        acc_s[...] = jnp.zeros_like(acc_s)
        gwi_ref[...] = jnp.zeros_like(gwi_ref)
        gwo_ref[...] = jnp.zeros_like(gwo_ref)
        gd_ref[...] = jnp.zeros_like(gd_ref)

        def pass2(jj, c):
            j = nch - 1 - jj
            rows = chunk_rows(j)
            u = u_ref[rows, :]
            dy = dy_ref[rows, :]
            u_b, dy_b = u.astype(BF16), dy.astype(BF16)
            bu_s[...] = _dot_nt(u_b, wi_ref[...])
            h_s[0:N_SEG, :] = hc_ref[j]
            seg_s[...] = hc_ref[j]
            _scan_fwd(bu_s, h_s.at[pl.ds(N_SEG, SCAN_ROWS), :], lr_ref, li_ref, seg_s, seg_s, steps)
            load_e(j)
            scan_rev(j, True)
            g_b = e_s[...].astype(BF16)
            du_ref[rows, :] = (_dot(g_b, wi_ref[...]) + d_ref[...] * dy).astype(du_ref.dtype)
            gwi_ref[...] += _dot_tn(u_b, g_b)
            gwo_ref[...] += _dot_tn(dy_b, h_s[pl.ds(N_SEG, SCAN_ROWS), :].astype(BF16))
            gd_ref[...] += jnp.sum(dy * u, axis=0, keepdims=True)
            return c

        lax.fori_loop(0, nch, pass2, 0)
        glam_ref[...] = jnp.sum(acc_s[...], axis=0, keepdims=True)

    mat = pl.BlockSpec((None, SSM_CH, 2 * SSM_ST), lambda g: (g, 0, 0))
    return pl.pallas_call(
        body, name="ssm_bwd", grid=(SSM_GB,),
        in_specs=[sp["rows"], sp["rows"], pl.BlockSpec((None, nch, N_SEG, 2 * SSM_ST), lambda g: (g, 0, 0, 0)),
                  sp["lam"], sp["lam"], sp["lam"], sp["lam"], sp["w_in"], sp["w_out"], sp["vec"]],
        out_specs=[sp["rows"], mat, mat, pl.BlockSpec((None, 1, 2 * SSM_ST), lambda g: (g, 0, 0)), sp["vec"]],
        out_shape=[jax.ShapeDtypeStruct((t, SSM_W), BF16), jax.ShapeDtypeStruct((SSM_GB, SSM_CH, 2 * SSM_ST), F32),
                   jax.ShapeDtypeStruct((SSM_GB, SSM_CH, 2 * SSM_ST), F32),
                   jax.ShapeDtypeStruct((SSM_GB, 1, 2 * SSM_ST), F32), jax.ShapeDtypeStruct((1, SSM_W), F32)],
        scratch_shapes=[pltpu.VMEM((SCAN_ROWS, 2 * SSM_ST), F32), pltpu.VMEM((SCAN_ROWS + N_SEG, 2 * SSM_ST), F32),
                        pltpu.VMEM((SCAN_ROWS, 2 * SSM_ST), F32), pltpu.VMEM((N_SEG, 2 * SSM_ST), F32),
                        pltpu.VMEM((N_SEG, 2 * SSM_ST), F32), pltpu.VMEM((N_SEG, 2 * SSM_ST), F32)],
        compiler_params=_params(("parallel",)),
    )(up, dyp, hc, lam_r, lam_i, pw_r, pw_i, w_in, w_out, d_skip)


def _glu_fwd(y, zs, w_glu, b_glu):
    t = y.shape[0]
    tm = 512

    def body(y_ref, z_ref, w_ref, b_ref, o_ref, yg_ref):
        yg = _gelu(y_ref[...])
        yg_b = yg.astype(BF16)
        a = _dot(yg_b, w_ref[...]) + b_ref[...]
        o_ref[...] = yg * _sigmoid(a) * _silu(z_ref[...])
        yg_ref[...] = yg_b

    row = pl.BlockSpec((tm, SSM_W), lambda i: (i, 0))
    return pl.pallas_call(
        body, name="glu_fwd", grid=(t // tm,),
        in_specs=[row, row, pl.BlockSpec((SSM_W, SSM_W), lambda i: (0, 0)), pl.BlockSpec((1, SSM_W), lambda i: (0, 0))],
        out_specs=[row, row],
        out_shape=[jax.ShapeDtypeStruct((t, SSM_W), F32), jax.ShapeDtypeStruct((t, SSM_W), BF16)],
        compiler_params=_params(("parallel",)),
    )(y, zs, w_glu, b_glu)


def _glu_bwd(y, zs, dos, w_glu, b_glu):
    t = y.shape[0]
    tm = 512

    def body(y_ref, z_ref, do_ref, w_ref, b_ref, dy_ref, dz_ref, da_ref, gb_ref):
        @pl.when(pl.program_id(0) == 0)
        def _():
            gb_ref[...] = jnp.zeros_like(gb_ref)

        yv, z, do = y_ref[...], z_ref[...], do_ref[...]
        yg = _gelu(yv)
        sg = _sigmoid(_dot(yg.astype(BF16), w_ref[...]) + b_ref[...])
        dy2 = do * _silu(z)
        dz_ref[...] = (do * yg * sg * _dsilu(z)).astype(BF16)
        da = dy2 * yg * sg * (1.0 - sg)
        da_b = da.astype(BF16)
        da_ref[...] = da_b
        gb_ref[...] += jnp.sum(da, axis=0, keepdims=True)
        dyg = dy2 * sg + _dot_nt(da_b, w_ref[...])
        dy_ref[...] = dyg * _dgelu(yv)

    row = pl.BlockSpec((tm, SSM_W), lambda i: (i, 0))
    vec = pl.BlockSpec((1, SSM_W), lambda i: (0, 0))
    return pl.pallas_call(
        body, name="glu_bwd", grid=(t // tm,),
        in_specs=[row, row, row, pl.BlockSpec((SSM_W, SSM_W), lambda i: (0, 0)), vec],
        out_specs=[row, row, row, vec],
        out_shape=[jax.ShapeDtypeStruct((t, SSM_W), F32), jax.ShapeDtypeStruct((t, SSM_W), BF16),
                   jax.ShapeDtypeStruct((t, SSM_W), BF16), jax.ShapeDtypeStruct((1, SSM_W), F32)],
        compiler_params=_params(("arbitrary",)),
    )(y, zs, dos, w_glu, b_glu)


def _rms(o):
    return lax.rsqrt(jnp.mean(o * o, axis=1, keepdims=True) + NORM_EPS)


def _outproj(oa, os_, aw, sw, w_out, x, target):
    t = x.shape[0]
    tm = 256

    def body(oa_ref, os_ref, aw_ref, sw_ref, w_ref, x_ref, t_ref, mg_ref, do_ref, ls_ref):
        @pl.when(pl.program_id(0) == 0)
        def _():
            ls_ref[...] = jnp.zeros_like(ls_ref)

        a, s = oa_ref[...], os_ref[...]
        merged = jnp.concatenate([a * _rms(a) * aw_ref[...], s * _rms(s) * sw_ref[...]], axis=1).astype(BF16)
        mg_ref[...] = merged
        err = x_ref[...] + _dot(merged, w_ref[...]) - t_ref[...]
        do_ref[...] = err * (1.0 / D_MODEL)
        ls_ref[...] += jnp.sum(err * err)

    half = pl.BlockSpec((tm, ATTN_W), lambda i: (i, 0))
    full = pl.BlockSpec((tm, D_MODEL), lambda i: (i, 0))
    vec = pl.BlockSpec((1, ATTN_W), lambda i: (0, 0))
    return pl.pallas_call(
        body, name="outproj", grid=(t // tm,),
        in_specs=[half, half, vec, vec, pl.BlockSpec((D_MODEL, D_MODEL), lambda i: (0, 0)), full, full],
        out_specs=[full, full, pl.BlockSpec((8, 128), lambda i: (0, 0))],
        out_shape=[jax.ShapeDtypeStruct((t, D_MODEL), BF16), jax.ShapeDtypeStruct((t, D_MODEL), F32),
                   jax.ShapeDtypeStruct((8, 128), F32)],
        compiler_params=_params(("arbitrary",)),
    )(oa, os_, aw, sw, w_out, x, target)


def _outproj_bwd(dout, oa, os_, aw, sw, w_out):
    t = dout.shape[0]
    tm = 256

    def norm_bwd(o, w, dm):
        r = _rms(o)
        yh = o * r
        gh = dm * w
        return r * (gh - yh * jnp.mean(gh * yh, axis=1, keepdims=True)), jnp.sum(dm * yh, axis=0, keepdims=True)

    def body(do_ref, oa_ref, os_ref, aw_ref, sw_ref, w_ref, da_ref, ds_ref, ga_ref, gs_ref):
        @pl.when(pl.program_id(0) == 0)
        def _():
            ga_ref[...] = jnp.zeros_like(ga_ref)
            gs_ref[...] = jnp.zeros_like(gs_ref)

        dm = _dot_nt(do_ref[...].astype(BF16), w_ref[...])
        da, ga = norm_bwd(oa_ref[...], aw_ref[...], dm[:, :ATTN_W])
        ds, gs = norm_bwd(os_ref[...], sw_ref[...], dm[:, ATTN_W:])
        da_ref[...] = da
        ds_ref[...] = ds
        ga_ref[...] += ga
        gs_ref[...] += gs

    half = pl.BlockSpec((tm, ATTN_W), lambda i: (i, 0))
    full = pl.BlockSpec((tm, D_MODEL), lambda i: (i, 0))
    vec = pl.BlockSpec((1, ATTN_W), lambda i: (0, 0))
    return pl.pallas_call(
        body, name="outproj_bwd", grid=(t // tm,),
        in_specs=[full, half, half, vec, vec, pl.BlockSpec((D_MODEL, D_MODEL), lambda i: (0, 0))],
        out_specs=[half, half, vec, vec],
        out_shape=[jax.ShapeDtypeStruct((t, ATTN_W), F32), jax.ShapeDtypeStruct((t, ATTN_W), F32),
                   jax.ShapeDtypeStruct((1, ATTN_W), F32), jax.ShapeDtypeStruct((1, ATTN_W), F32)],
        compiler_params=_params(("arbitrary",)),
    )(dout, oa, os_, aw, sw, w_out)


def _inproj_bwd(dproj, w_slabs, x, norm_w, dout, outgoing):
    t = x.shape[0]
    tm = 512
    nt = len(outgoing)
    ni = t // tm

    def body(dp_ref, w_ref, x_ref, nw_ref, do_ref, *rest):
        src, (gx_ref, gw_ref), dst = rest[:nt], rest[nt:nt + 2], rest[nt + 2:2 * nt + 2]
        acc_ref, ssem, rsem = rest[2 * nt + 2:]
        i, j = pl.program_id(0), pl.program_id(1)
        start, wait = _bg_scatter_chips(src, dst, ssem, rsem)

        @pl.when((i == 0) & (j == 0))
        def _():
            start()
            gw_ref[...] = jnp.zeros_like(gw_ref)

        @pl.when(j == 0)
        def _():
            acc_ref[...] = jnp.zeros_like(acc_ref)

        acc_ref[...] += _dot_nt(dp_ref[...], w_ref[...])

        @pl.when(j == 3)
        def _():
            xv = x_ref[...]
            r = lax.rsqrt(jnp.mean(xv * xv, axis=1, keepdims=True) + NORM_EPS)
            yh = xv * r
            dh = acc_ref[...]
            gh = dh * nw_ref[...]
            gx_ref[...] = do_ref[...] + r * (gh - yh * jnp.mean(gh * yh, axis=1, keepdims=True))
            gw_ref[...] += jnp.sum(dh * yh, axis=0, keepdims=True)

        @pl.when((i == ni - 1) & (j == 3))
        def _():
            wait()

    full = pl.BlockSpec((tm, D_MODEL), lambda i, j: (i, 0))
    vec = pl.BlockSpec((1, D_MODEL), lambda i, j: (0, 0))
    return pl.pallas_call(
        body, name="inproj_bwd", grid=(ni, 4),
        in_specs=[pl.BlockSpec((tm, SHARD_W), lambda i, j: (i, j)),
                  pl.BlockSpec((None, D_MODEL, SHARD_W), lambda i, j: (j, 0, 0)), full, vec, full] + [ANY] * nt,
        out_specs=[full, vec] + [ANY] * nt,
        out_shape=[jax.ShapeDtypeStruct((t, D_MODEL), F32), jax.ShapeDtypeStruct((1, D_MODEL), F32)]
        + [jax.ShapeDtypeStruct(a.shape, a.dtype) for a in outgoing],
        scratch_shapes=[pltpu.VMEM((tm, D_MODEL), F32), pltpu.SemaphoreType.DMA((3 * nt,)),
                        pltpu.SemaphoreType.DMA((3 * nt,))],
        compiler_params=_params(("arbitrary", "arbitrary")),
    )(dproj, w_slabs, x, norm_w.reshape(1, D_MODEL), dout, *outgoing)


def _adamw_math(w_ref, g_ref, m_ref, v_ref, d_ref, nm_ref, nv_ref):
    gv = g_ref[...]
    nm = ADAM_B1 * m_ref[...] + (1.0 - ADAM_B1) * gv
    nv = ADAM_B2 * v_ref[...] + (1.0 - ADAM_B2) * (gv * gv)
    m_hat = nm / (1.0 - ADAM_B1 ** ADAM_STEP)
    v_hat = nv / (1.0 - ADAM_B2 ** ADAM_STEP)
    d_ref[...] = -ADAM_LR * (m_hat / (jnp.sqrt(v_hat) + ADAM_EPS) + ADAM_WD * w_ref[...])
    nm_ref[...] = nm
    nv_ref[...] = nv


def _adamw_halves(w, mine, theirs, m, v, c_idx, *, rows, name):
    hr, cols = mine.shape
    nblk = hr // rows

    def body(c_ref, w_ref, a_ref, b_ref, m_ref, v_ref, g_ref, d_ref, nm_ref, nv_ref):
        g_ref[...] = jnp.where(pl.program_id(0) == c_ref[0], a_ref[...], b_ref[...])
        _adamw_math(w_ref, g_ref, m_ref, v_ref, d_ref, nm_ref, nv_ref)

    full = pl.BlockSpec((rows, cols), lambda h, i, c: (h * nblk + i, 0))
    part = pl.BlockSpec((rows, cols), lambda h, i, c: (i, 0))
    shp = jax.ShapeDtypeStruct((2 * hr, cols), F32)
    return pl.pallas_call(
        body, name=name,
        grid_spec=pltpu.PrefetchScalarGridSpec(num_scalar_prefetch=1, grid=(2, nblk),
                                               in_specs=[full, part, part, full, full], out_specs=[full] * 4),
        out_shape=[shp] * 4, compiler_params=_params(("parallel", "parallel")),
    )(c_idx, w, mine, theirs, m, v)


def _adamw(w, g, m, v, *, rows, name):
    r, c = w.shape

    def body(w_ref, g_ref, m_ref, v_ref, d_ref, nm_ref, nv_ref):
        _adamw_math(w_ref, g_ref, m_ref, v_ref, d_ref, nm_ref, nv_ref)

    blk = pl.BlockSpec((rows, c), lambda i: (i, 0))
    shp = jax.ShapeDtypeStruct((r, c), F32)
    return pl.pallas_call(body, name=name, grid=(r // rows,), in_specs=[blk] * 4, out_specs=[blk] * 3,
                          out_shape=[shp] * 3, compiler_params=_params(("parallel",)))(w, g, m, v)


def _remote(src, dst, ssem, rsem, dev):
    return pltpu.make_async_remote_copy(src_ref=src, dst_ref=dst, send_sem=ssem, recv_sem=rsem, device_id=dev,
                                        device_id_type=pl.DeviceIdType.MESH)


def _mesh_pos():
    return lax.axis_index("x"), lax.axis_index("y"), lax.axis_index("c")


def _other_chips(x, y):
    return [(1 - x, y), (x, 1 - y), (1 - x, 1 - y)]


def _flips():
    return [(dx, dy, dc) for dx in (0, 1) for dy in (0, 1) for dc in (0, 1) if (dx, dy, dc) != (0, 0, 0)]


def _background(sends, arrivals):
    def start():
        for cp in sends():
            cp.start()

    def wait():
        for cp in arrivals():
            cp.wait_recv()
        for cp in sends():
            cp.wait_send()

    return start, wait


def _bg_gather(sh, full, ssem, rsem):
    x, y, c = _mesh_pos()
    me = 2 * x + y
    peers = [(px, py, c) for px, py in _other_chips(x, y)] + [(x, y, 1 - c)]
    slots = [2 * px + py for px, py in _other_chips(x, y)] + [me]
    pairs = [(i, k) for i in range(len(sh)) for k in range(4)]
    return _background(
        lambda: [_remote(sh[i], full[i].at[me], ssem.at[4 * i + k], rsem.at[4 * i + k], peers[k]) for i, k in pairs],
        lambda: [_remote(full[i].at[slots[k]], full[i].at[slots[k]], ssem.at[4 * i + k], rsem.at[4 * i + k], peers[k])
                 for i, k in pairs])


def _bg_scatter_devices(src, dst, ssem, rsem):
    x, y, c = _mesh_pos()
    me = 4 * x + 2 * y + c
    peers = []
    for dx, dy, dc in _flips():
        px, py, pc = jnp.bitwise_xor(x, dx), jnp.bitwise_xor(y, dy), jnp.bitwise_xor(c, dc)
        peers.append(((px, py, pc), 4 * px + 2 * py + pc))
    pairs = [(i, k) for i in range(len(src)) for k in range(7)]
    return _background(
        lambda: [_remote(src[i].at[peers[k][1]], dst[i].at[me], ssem.at[7 * i + k], rsem.at[7 * i + k], peers[k][0])
                 for i, k in pairs],
        lambda: [_remote(dst[i].at[peers[k][1]], dst[i].at[peers[k][1]], ssem.at[7 * i + k], rsem.at[7 * i + k],
                         peers[k][0]) for i, k in pairs])


def _bg_scatter_chips(src, dst, ssem, rsem):
    x, y, c = _mesh_pos()
    me = 2 * x + y
    chips = _other_chips(x, y)
    pairs = [(i, k) for i in range(len(src)) for k in range(3)]
    slot = lambda k: 2 * chips[k][0] + chips[k][1]
    return _background(
        lambda: [_remote(src[i].at[slot(k)], dst[i].at[me], ssem.at[3 * i + k], rsem.at[3 * i + k], (*chips[k], c))
                 for i, k in pairs],
        lambda: [_remote(dst[i].at[slot(k)], dst[i].at[slot(k)], ssem.at[3 * i + k], rsem.at[3 * i + k], (*chips[k], c))
                 for i, k in pairs])


def _gather_weights(shards):
    nt = len(shards)
    halves = [s.shape[0] // 2 for s in shards]

    def body(*refs):
        sh, full = refs[:nt], refs[nt:2 * nt]
        ssem, rsem = refs[2 * nt:]
        x, y, c = _mesh_pos()
        me = 2 * x + y
        sib = (x, y, 1 - c)

        def half(i, which):
            return pl.ds(pl.multiple_of(which * halves[i], 8), halves[i])

        sends = []
        for i in range(nt):
            cp = _remote(sh[i], full[i].at[me], ssem.at[6 * nt + i], rsem.at[6 * nt + i], sib)
            cp.start()
            sends.append(cp)
        for k, (px, py) in enumerate(_other_chips(x, y)):
            for i in range(nt):
                cp = _remote(sh[i].at[half(i, c)], full[i].at[me, half(i, c)], ssem.at[k * nt + i],
                             rsem.at[k * nt + i], (px, py, c))
                cp.start()
                sends.append(cp)
        for k, (px, py) in enumerate(_other_chips(x, y)):
            slot = 2 * px + py
            for i in range(nt):
                landed = full[i].at[slot, half(i, c)]
                _remote(landed, landed, ssem.at[k * nt + i], rsem.at[k * nt + i], sib).wait_recv()
                cp = _remote(landed, landed, ssem.at[(3 + k) * nt + i], rsem.at[(3 + k) * nt + i], sib)
                cp.start()
                sends.append(cp)
        for k, (px, py) in enumerate(_other_chips(x, y)):
            slot = 2 * px + py
            for i in range(nt):
                passed = full[i].at[slot, half(i, 1 - c)]
                _remote(passed, passed, ssem.at[(3 + k) * nt + i], rsem.at[(3 + k) * nt + i], sib).wait_recv()
        for i in range(nt):
            own = full[i].at[me]
            _remote(own, own, ssem.at[6 * nt + i], rsem.at[6 * nt + i], sib).wait_recv()
        for cp in sends:
            cp.wait_send()

    return pl.pallas_call(
        body, name="gather_weights", in_specs=[ANY] * nt, out_specs=[ANY] * nt,
        out_shape=[jax.ShapeDtypeStruct((4,) + s.shape, s.dtype) for s in shards],
        scratch_shapes=[pltpu.SemaphoreType.DMA((7 * nt,)), pltpu.SemaphoreType.DMA((7 * nt,))],
    )(*shards)


def _pair_swap(arrays):
    nt = len(arrays)

    def body(*refs):
        src, dst = refs[:nt], refs[nt:2 * nt]
        ssem, rsem = refs[2 * nt:]
        x, y, c = _mesh_pos()
        cps = [_remote(src[i].at[:, 1 - c], dst[i], ssem.at[i], rsem.at[i], (x, y, 1 - c)) for i in range(nt)]
        for cp in cps:
            cp.start()
        for cp in cps:
            cp.wait_recv()
        for cp in cps:
            cp.wait_send()

    return pl.pallas_call(
        body, name="pair_swap", in_specs=[ANY] * nt, out_specs=[ANY] * nt,
        out_shape=[jax.ShapeDtypeStruct((4,) + a.shape[2:], a.dtype) for a in arrays],
        scratch_shapes=[pltpu.SemaphoreType.DMA((nt,)), pltpu.SemaphoreType.DMA((nt,))],
    )(*arrays)


def _half_swap(arrays):
    nt = len(arrays)

    def body(*refs):
        src, dst = refs[:nt], refs[nt:2 * nt]
        ssem, rsem = refs[2 * nt:]
        x, y, c = _mesh_pos()
        cps = [_remote(src[i], dst[i], ssem.at[i], rsem.at[i], (x, y, 1 - c)) for i in range(nt)]
        for cp in cps:
            cp.start()
        for cp in cps:
            cp.wait_recv()
        for cp in cps:
            cp.wait_send()

    return pl.pallas_call(
        body, name="half_swap", in_specs=[ANY] * nt, out_specs=[ANY] * nt,
        out_shape=[jax.ShapeDtypeStruct(a.shape, a.dtype) for a in arrays],
        scratch_shapes=[pltpu.SemaphoreType.DMA((nt,)), pltpu.SemaphoreType.DMA((nt,))],
    )(*arrays)


def _exchange_slices(src, scatter, name):
    def body(src_ref, dst_ref, ssem, rsem, lsem):
        x, y, c = _mesh_pos()
        me = 4 * x + 2 * y + c
        local = pltpu.make_async_copy(src_ref.at[me] if scatter else src_ref, dst_ref.at[me], lsem)
        local.start()
        cps = []
        for k, (dx, dy, dc) in enumerate(_flips()):
            px, py, pc = jnp.bitwise_xor(x, dx), jnp.bitwise_xor(y, dy), jnp.bitwise_xor(c, dc)
            peer = 4 * px + 2 * py + pc
            cp = _remote(src_ref.at[peer] if scatter else src_ref, dst_ref.at[me], ssem.at[k], rsem.at[k],
                         (px, py, pc))
            cp.start()
            cps.append((cp, peer))
        for k, (cp, peer) in enumerate(cps):
            slot = dst_ref.at[peer]
            _remote(slot, slot, ssem.at[k], rsem.at[k], (x, y, c)).wait_recv()
        for cp, _ in cps:
            cp.wait_send()
        local.wait()

    return pl.pallas_call(
        body, name=name, in_specs=[ANY], out_specs=ANY,
        out_shape=jax.ShapeDtypeStruct((8,) + src.shape[-2:], src.dtype),
        scratch_shapes=[pltpu.SemaphoreType.DMA((7,)), pltpu.SemaphoreType.DMA((7,)), pltpu.SemaphoreType.DMA],
    )(src)


def _add_halves(g, recv, c_idx, *, rows, name):
    _, _, hr, cols = g.shape

    def body(c_ref, g_ref, r_ref, o_ref):
        o_ref[...] = (g_ref[...] + r_ref[...].astype(F32)).astype(BF16)

    return pl.pallas_call(
        body, name=name,
        grid_spec=pltpu.PrefetchScalarGridSpec(
            num_scalar_prefetch=1, grid=(4, hr // rows),
            in_specs=[pl.BlockSpec((None, None, rows, cols), lambda j, i, c: (j, c[0], i, 0)),
                      pl.BlockSpec((None, rows, cols), lambda j, i, c: (j, i, 0))],
            out_specs=pl.BlockSpec((None, rows, cols), lambda j, i, c: (j, i, 0))),
        out_shape=jax.ShapeDtypeStruct((4, hr, cols), BF16),
        compiler_params=_params(("parallel", "parallel")),
    )(c_idx, g, recv)


def _sum_peers(slots, own, idx, *, rows, name):
    n, r, cols = slots.shape

    def body(me_ref, *refs):
        me = me_ref[0]
        mine = refs[n][...].astype(F32)
        acc = None
        for k in range(n):
            term = jnp.where(me == k, mine, refs[k][...].astype(F32))
            acc = term if acc is None else acc + term
        refs[n + 1][...] = acc

    def slot_spec(k):
        return pl.BlockSpec((None, rows, cols), lambda i, me: (jnp.where(me[0] == k, (k + 1) % n, k), i, 0))

    return pl.pallas_call(
        body, name=name,
        grid_spec=pltpu.PrefetchScalarGridSpec(
            num_scalar_prefetch=1, grid=(r // rows,),
            in_specs=[slot_spec(k) for k in range(n)] + [pl.BlockSpec((None, rows, cols), lambda i, me: (me[0], i, 0))],
            out_specs=pl.BlockSpec((rows, cols), lambda i, me: (i, 0))),
        out_shape=jax.ShapeDtypeStruct((r, cols), F32),
        compiler_params=_params(("parallel",)),
    )(idx, *([slots] * n), own)


def _sum_slots(slots, *, rows, name):
    n, r, cols = slots.shape

    def body(s_ref, o_ref):
        acc = s_ref[0].astype(F32)
        for k in range(1, n):
            acc = acc + s_ref[k].astype(F32)
        o_ref[...] = acc

    return pl.pallas_call(
        body, name=name, grid=(r // rows,),
        in_specs=[pl.BlockSpec((n, rows, cols), lambda i: (0, i, 0))],
        out_specs=pl.BlockSpec((rows, cols), lambda i: (i, 0)),
        out_shape=jax.ShapeDtypeStruct((r, cols), F32),
        compiler_params=_params(("parallel",)),
    )(slots)


def _pack_small(d, names, rows):
    flat = jnp.concatenate([d[n].astype(F32).reshape(-1) for n in names])
    return jnp.pad(flat, (0, rows * 128 - flat.shape[0])).reshape(rows, 128)


def _unpack_small(p, names):
    flat = p.reshape(-1)
    out, off = {}, 0
    for n in names:
        size = math.prod(SMALL_SHAPES[n])
        out[n] = flat[off:off + size].reshape(SMALL_SHAPES[n])
        off += size
    return out


def _adamw_3d(w, g, m, v, *, name):
    def body(w_ref, g_ref, m_ref, v_ref, d_ref, nm_ref, nv_ref):
        _adamw_math(w_ref, g_ref, m_ref, v_ref, d_ref, nm_ref, nv_ref)

    blk = pl.BlockSpec((8,) + w.shape[1:], lambda i: (i, 0, 0))
    shp = jax.ShapeDtypeStruct(w.shape, F32)
    return pl.pallas_call(body, name=name, grid=(w.shape[0] // 8,), in_specs=[blk] * 4, out_specs=[blk] * 3,
                          out_shape=[shp] * 3, compiler_params=_params(("parallel",)))(w, g, m, v)


def kernel(x, positions, norm_w, w_in, q_norm_w, k_norm_w, sinks, a_re, a_im, log_step, b_re, b_im, c_re, c_im, d_skip, w_glu, b_glu, attn_out_norm_w, ssm_out_norm_w, w_out, loss_target, m_norm_w, m_w_in, m_q_norm_w, m_k_norm_w, m_sinks, m_a_re, m_a_im, m_log_step, m_b_re, m_b_im, m_c_re, m_c_im, m_d_skip, m_w_glu, m_b_glu, m_attn_out_norm_w, m_ssm_out_norm_w, m_w_out, v_norm_w, v_w_in, v_q_norm_w, v_k_norm_w, v_sinks, v_a_re, v_a_im, v_log_step, v_b_re, v_b_im, v_c_re, v_c_im, v_d_skip, v_w_glu, v_b_glu, v_attn_out_norm_w, v_ssm_out_norm_w, v_w_out):
    small_w = dict(norm_w=norm_w, q_norm_w=q_norm_w, k_norm_w=k_norm_w, sinks=sinks, a_re=a_re, a_im=a_im,
                   log_step=log_step, b_re=b_re, b_im=b_im, c_re=c_re, c_im=c_im, d_skip=d_skip, b_glu=b_glu,
                   attn_out_norm_w=attn_out_norm_w, ssm_out_norm_w=ssm_out_norm_w)
    small_m = dict(norm_w=m_norm_w, q_norm_w=m_q_norm_w, k_norm_w=m_k_norm_w, sinks=m_sinks, a_re=m_a_re, a_im=m_a_im,
                   log_step=m_log_step, b_re=m_b_re, b_im=m_b_im, c_re=m_c_re, c_im=m_c_im, d_skip=m_d_skip,
                   b_glu=m_b_glu, attn_out_norm_w=m_attn_out_norm_w, ssm_out_norm_w=m_ssm_out_norm_w)
    small_v = dict(norm_w=v_norm_w, q_norm_w=v_q_norm_w, k_norm_w=v_k_norm_w, sinks=v_sinks, a_re=v_a_re, a_im=v_a_im,
                   log_step=v_log_step, b_re=v_b_re, b_im=v_b_im, c_re=v_c_re, c_im=v_c_im, d_skip=v_d_skip,
                   b_glu=v_b_glu, attn_out_norm_w=v_attn_out_norm_w, ssm_out_norm_w=v_ssm_out_norm_w)
    c_idx = lax.axis_index("c").astype(jnp.int32).reshape(1)
    chip_idx = (2 * lax.axis_index("x") + lax.axis_index("y")).astype(jnp.int32).reshape(1)
    dev_idx = 2 * chip_idx + c_idx

    xs = x[0]
    tgt = loss_target[0]
    t = xs.shape[0]
    posf = positions[0].astype(F32).reshape(t, 1)

    (w_in_all,) = _gather_weights([w_in.astype(BF16)])

    proj, hn, w_glu_all, w_out_all = _inproj(xs, norm_w, w_in_all, [w_glu.astype(BF16), w_out.astype(BF16)])
    w_glu_b = w_glu_all.reshape(SSM_W, SSM_W)
    w_out_b = w_out_all.reshape(D_MODEL, D_MODEL)
    inv_freq = ROPE_THETA ** (-jnp.arange(0, HEAD_DIM, 2, dtype=F32) / HEAD_DIM)
    invf = jnp.tile(inv_freq, 4).reshape(1, 128)
    qw = jnp.tile(q_norm_w, 2).reshape(1, 128)
    kw = jnp.tile(k_norm_w, 2).reshape(1, 128)
    sink_row = sinks.reshape(1, N_HEADS)
    oa = _attn_fwd(proj, posf, invf, qw, kw, sink_row)

    lam_r, lam_i, pw_r, pw_i, bb_r, bb_i = _ssm_prep(a_re, a_im, log_step, b_re, b_im, t // N_SEG)
    rows8 = lambda a: jnp.broadcast_to(a.reshape(SSM_GB, 1, SSM_ST), (SSM_GB, N_SEG, SSM_ST))
    lam_r8, lam_i8, pw_r8, pw_i8 = rows8(lam_r), rows8(lam_i), rows8(pw_r), rows8(pw_i)
    ssm_w_in = jnp.concatenate([_block_diag_in(bb_r), _block_diag_in(bb_i)], axis=1).astype(BF16)
    ssm_w_out = jnp.concatenate([_block_diag_out(c_re), _block_diag_out(-c_im)], axis=2).astype(BF16)
    d_row = d_skip.reshape(1, SSM_W)
    uz = _permute_rows(proj[:, 2560:])
    up, zsp = uz[:, :SSM_W], uz[:, SSM_W:]
    yp, hc = _ssm_fwd(up, lam_r8, lam_i8, pw_r8, pw_i8, ssm_w_in, ssm_w_out, d_row)
    b_glu_row = b_glu.reshape(1, SSM_W)
    osp, ygp = _glu_fwd(yp, zsp, w_glu_b, b_glu_row)
    os_ = _unpermute_rows(osp)
    aw = attn_out_norm_w.reshape(1, ATTN_W)
    sw = ssm_out_norm_w.reshape(1, SSM_W)
    merged, dout, sq_err = _outproj(oa, os_, aw, sw, w_out_b, xs, tgt)
    loss = lax.psum(0.5 * sq_err[0, 0] / D_MODEL, MESH_AXES)

    doa, dos, g_aw, g_sw = _outproj_bwd(dout, oa, os_, aw, sw, w_out_b)
    dout_b = dout.astype(BF16)
    g_w_out, g_w_out_b = _matmul_tn(merged, dout_b, tm=512, tn=1024, tk=1024, name="grad_w_out")
    dyp, dzsp, dap, g_b_glu = _glu_bwd(yp, zsp, _permute_rows(dos), w_glu_b, b_glu_row)
    g_w_glu, g_w_glu_b = _matmul_tn(ygp, dap, tm=512, tn=1024, tk=1024, name="grad_w_glu")
    dup, g_wi, g_wo, g_lam, g_d = _ssm_bwd(up, dyp, hc, lam_r8, lam_i8, pw_r8, pw_i8, ssm_w_in, ssm_w_out, d_row)
    early = [g_w_glu_b.reshape(8, 128, SSM_W), g_w_out_b.reshape(8, 256, D_MODEL)]
    dq, dk, dv, dza, g_qw, g_kw, g_sink, *early_slots = _attn_bwd(proj, posf, invf, qw, kw, sink_row, doa, early)
    duz = _unpermute_rows(jnp.concatenate([dup, dzsp], axis=1))
    dproj = jnp.concatenate([dq, dk, dv, dza, duz], axis=1)
    g_w_in, g_w_in_b = _matmul_tn(hn, dproj, tm=512, tn=SHARD_W, tk=1024, name="grad_w_in", slabs=True)
    in_shape = (4, 2, D_MODEL // 2, SHARD_W)
    (from_sib,) = _pair_swap([g_w_in_b.reshape(in_shape)])
    pair_in = _add_halves(g_w_in.reshape(in_shape), from_sib, c_idx, rows=128, name="pair_sum")
    grad_x, g_nw, in_slots = _inproj_bwd(dproj, w_in_all, xs, norm_w, dout, [pair_in])

    g_bb_r = _diag_blocks(g_wi[:, :, :SSM_ST]).transpose(0, 2, 1).reshape(SSM_G, SSM_P * SSM_H)
    g_bb_i = _diag_blocks(g_wi[:, :, SSM_ST:]).transpose(0, 2, 1).reshape(SSM_G, SSM_P * SSM_H)
    g_a_re, g_a_im, g_ls, g_b_re, g_b_im = _ssm_param_grads(
        a_re, a_im, log_step, b_re, b_im, g_lam[:, 0, :SSM_ST].reshape(SSM_G, SSM_P),
        g_lam[:, 0, SSM_ST:].reshape(SSM_G, SSM_P), g_bb_r, g_bb_i)
    small_g = dict(
        norm_w=g_nw, q_norm_w=g_qw[0, :64] + g_qw[0, 64:], k_norm_w=g_kw[0, :64] + g_kw[0, 64:],
        sinks=g_sink[0, :N_HEADS], a_re=g_a_re, a_im=g_a_im, log_step=g_ls, b_re=g_b_re, b_im=g_b_im,
        c_re=_diag_blocks(g_wo[:, :, :SSM_ST]), c_im=-_diag_blocks(g_wo[:, :, SSM_ST:]), d_skip=g_d,
        b_glu=g_b_glu, attn_out_norm_w=g_aw, ssm_out_norm_w=g_sw)

    mine = [_sum_peers(in_slots, pair_in, chip_idx, rows=128, name="sum_w_in"),
            _sum_peers(early_slots[0], early[0], dev_idx, rows=128, name="sum_w_glu"),
            _sum_peers(early_slots[1], early[1], dev_idx, rows=128, name="sum_w_out")]
    theirs = _half_swap(mine)
    packed = _pack_small(small_g, SMALL, 8 * PACK_ROWS).reshape(8, PACK_ROWS, 128)
    summed = _sum_slots(_exchange_slices(packed, True, "small_scatter"), rows=PACK_ROWS, name="small_sum")
    small_red = _exchange_slices(summed, False, "small_gather").reshape(8 * PACK_ROWS, 128)

    big = [_adamw_halves(w_in, mine[0], theirs[0], m_w_in, v_w_in, c_idx, rows=256, name="adamw_w_in"),
           _adamw_halves(w_glu, mine[1], theirs[1], m_w_glu, v_w_glu, c_idx, rows=128, name="adamw_w_glu"),
           _adamw_halves(w_out, mine[2], theirs[2], m_w_out, v_w_out, c_idx, rows=256, name="adamw_w_out")]
    g_in_sh, g_glu_sh, g_out_sh = (b[0] for b in big)
    upd = [b[1:] for b in big]
    grads = _unpack_small(small_red, SMALL)
    flat_first = sum(math.prod(SMALL_SHAPES[n]) for n in SMALL_3D) // 128
    sd, sm, sv = _adamw(_pack_small(small_w, SMALL_FLAT, FLAT_ROWS), small_red[flat_first:flat_first + FLAT_ROWS],
                        _pack_small(small_m, SMALL_FLAT, FLAT_ROWS), _pack_small(small_v, SMALL_FLAT, FLAT_ROWS),
                        rows=FLAT_ROWS, name="adamw_small")
    deltas, new_m, new_v = (_unpack_small(a, SMALL_FLAT) for a in (sd, sm, sv))
    for n in SMALL_3D:
        deltas[n], new_m[n], new_v[n] = _adamw_3d(small_w[n], grads[n], small_m[n], small_v[n], name="adamw_" + n)
    grads.update(w_in=g_in_sh, w_glu=g_glu_sh, w_out=g_out_sh)
    for n, (d, m_, v_) in zip(("w_in", "w_glu", "w_out"), upd):
        deltas[n], new_m[n], new_v[n] = d, m_, v_
    order = ["norm_w", "w_in", "q_norm_w", "k_norm_w", "sinks", "a_re", "a_im", "log_step", "b_re", "b_im", "c_re",
             "c_im", "d_skip", "w_glu", "b_glu", "attn_out_norm_w", "ssm_out_norm_w", "w_out"]
    return (loss, grad_x[None], *[grads[n] for n in order], *[deltas[n] for n in order],
            *[new_m[n] for n in order], *[new_v[n] for n in order])
```

```python
import math

import jax
import jax.numpy as jnp
from jax import lax
from jax.experimental import pallas as pl
from jax.experimental.pallas import tpu as pltpu

F32 = jnp.float32
BF16 = jnp.bfloat16

D_MODEL = 2048
ATTN_W = 1024
SSM_W = 1024
HEAD_DIM = 64
N_HEADS = 16
N_KV_HEADS = 4
KV_W = 256
BLOCK = 128
IN_W = 4608
SHARD_W = IN_W // 4
ROPE_THETA = 10000.0
SSM_H = 16
SSM_G = 64
SSM_P = 64
NORM_EPS = 1e-6
ADAM_LR = 0.001
ADAM_B1 = 0.9
ADAM_B2 = 0.999
ADAM_EPS = 1e-08
ADAM_WD = 0.01
ADAM_STEP = 10

N_SEG = 8
SSM_GB = 4
SSM_CH = 256
SSM_ST = 1024
SCAN_ROWS = 256
SCAN_LW = 512
VMEM_LIMIT = 56 * 1024 * 1024
MESH_AXES = ("x", "y", "c")
ANY = pl.BlockSpec(memory_space=pl.ANY)

SMALL_3D = ("b_re", "b_im", "c_re", "c_im")
SMALL_FLAT = ("norm_w", "q_norm_w", "k_norm_w", "sinks", "a_re", "a_im", "log_step", "d_skip", "b_glu",
              "attn_out_norm_w", "ssm_out_norm_w")
SMALL = SMALL_3D + SMALL_FLAT
SMALL_SHAPES = {"norm_w": (2048,), "q_norm_w": (64,), "k_norm_w": (64,), "sinks": (16,), "a_re": (64, 64),
                "a_im": (64, 64), "log_step": (64,), "b_re": (64, 64, 16), "b_im": (64, 64, 16),
                "c_re": (64, 16, 64), "c_im": (64, 16, 64), "d_skip": (1024,), "b_glu": (1024,),
                "attn_out_norm_w": (1024,), "ssm_out_norm_w": (1024,)}
PACK_ROWS = 272
FLAT_ROWS = 120


def _params(sem=None):
    return pltpu.CompilerParams(dimension_semantics=sem, vmem_limit_bytes=VMEM_LIMIT)


def _dot(a, b):
    return jnp.dot(a, b, preferred_element_type=F32)


def _dot_nt(a, b):
    return lax.dot_general(a, b, (((1,), (1,)), ((), ())), preferred_element_type=F32)


def _dot_tn(a, b):
    return lax.dot_general(a, b, (((0,), (0,)), ((), ())), preferred_element_type=F32)


def _sigmoid(x):
    return 1.0 / (1.0 + jnp.exp(-x))


def _silu(x):
    return x * _sigmoid(x)


def _dsilu(x):
    s = _sigmoid(x)
    return s * (1.0 + x * (1.0 - s))


_GELU_C = math.sqrt(2.0 / math.pi)


def _gelu(x):
    return 0.5 * x * (1.0 + jnp.tanh(_GELU_C * (x + 0.044715 * x * x * x)))


def _dgelu(x):
    t = jnp.tanh(_GELU_C * (x + 0.044715 * x * x * x))
    return 0.5 * (1.0 + t) + 0.5 * x * (1.0 - t * t) * _GELU_C * (1.0 + 3.0 * 0.044715 * x * x)


def _matmul_tn(a, b, *, tm, tn, tk, name, slabs=False):
    k, m = a.shape
    _, n = b.shape
    nk = k // tk

    def body(a_ref, b_ref, o_ref, ob_ref, acc_ref):
        kk = pl.program_id(2)

        @pl.when(kk == 0)
        def _():
            acc_ref[...] = jnp.zeros_like(acc_ref)

        acc_ref[...] += _dot_tn(a_ref[...], b_ref[...])

        @pl.when(kk == nk - 1)
        def _():
            o_ref[...] = acc_ref[...]
            ob_ref[...] = acc_ref[...].astype(BF16)

    if slabs:
        out_spec = pl.BlockSpec((None, tm, tn), lambda i, j, kk: (j, i, 0))
        shape = (n // tn, m, tn)
    else:
        out_spec = pl.BlockSpec((tm, tn), lambda i, j, kk: (i, j))
        shape = (m, n)
    return pl.pallas_call(
        body, name=name, grid=(m // tm, n // tn, nk),
        in_specs=[pl.BlockSpec((tk, tm), lambda i, j, kk: (kk, i)), pl.BlockSpec((tk, tn), lambda i, j, kk: (kk, j))],
        out_specs=[out_spec, out_spec],
        out_shape=[jax.ShapeDtypeStruct(shape, F32), jax.ShapeDtypeStruct(shape, BF16)],
        scratch_shapes=[pltpu.VMEM((tm, tn), F32)],
        compiler_params=_params(("parallel", "parallel", "arbitrary")),
    )(a, b)


def _inproj(x, norm_w, w_sh, order):
    t = x.shape[0]
    tm = 512
    ni = t // tm
    hr = D_MODEL // 2

    def body(ord_ref, x_ref, nw_ref, sh_ref, proj_ref, hn_ref, full_ref, wbuf, hn_s, ssem, rsem, lsem):
        s, i = pl.program_id(0), pl.program_id(1)
        mx, my, c = _mesh_pos()
        me = 2 * mx + my
        sib = (mx, my, 1 - c)
        chips = _other_chips(mx, my)

        def half(which):
            return pl.ds(pl.multiple_of(which * hr, 8), hr)

        def slot(k):
            return 2 * chips[k][0] + chips[k][1]

        def ici(k):
            return _remote(sh_ref.at[half(c)], full_ref.at[me, half(c)], ssem.at[k], rsem.at[k], (*chips[k], c))

        def own():
            return _remote(sh_ref, full_ref.at[me], ssem.at[6], rsem.at[6], sib)

        def landed(k, which, sem):
            ref = full_ref.at[slot(k), half(which)]
            return _remote(ref, ref, ssem.at[sem], rsem.at[sem], sib)

        def fetch(src, b):
            return pltpu.make_async_copy(src, wbuf.at[b], lsem.at[b])

        @pl.when((s == 0) & (i == 0))
        def _():
            for k in range(3):
                ici(k).start()
            own().start()
            cp = fetch(sh_ref, 0)
            cp.start()
            cp.wait()

        for k in range(3):
            @pl.when((s == k) & (i == max(ni - 2, 0)))
            def _(k=k):
                landed(k, c, k).wait_recv()
                landed(k, c, 3 + k).start()
                landed(k, 1 - c, 3 + k).wait_recv()
                fetch(full_ref.at[slot(k)], (k + 1) % 2).start()

            @pl.when((s == k + 1) & (i == 0))
            def _(k=k):
                fetch(full_ref.at[slot(k)], (k + 1) % 2).wait()

        xv = x_ref[...]
        r = lax.rsqrt(jnp.mean(xv * xv, axis=1, keepdims=True) + NORM_EPS)
        hn = (xv * r * nw_ref[...]).astype(BF16)
        proj_ref[...] = _dot(hn, wbuf[s % 2])

        def hn_out(tile):
            return pltpu.make_async_copy(hn_s, hn_ref.at[pl.ds(pl.multiple_of(tile * tm, tm), tm), :], lsem.at[2])

        @pl.when(((s == 0) & (i > 0)) | ((s == 1) & (i == 0)))
        def _():
            hn_out(jnp.where(s == 0, i - 1, ni - 1)).wait()

        @pl.when(s == 0)
        def _():
            hn_s[...] = hn
            hn_out(i).start()

        @pl.when((s == 3) & (i == ni - 1))
        def _():
            mine = full_ref.at[me]
            _remote(mine, mine, ssem.at[6], rsem.at[6], sib).wait_recv()
            for k in range(3):
                ici(k).wait_send()
                landed(k, c, 3 + k).wait_send()
            own().wait_send()

    return pl.pallas_call(
        body, name="inproj",
        grid_spec=pltpu.PrefetchScalarGridSpec(
            num_scalar_prefetch=1, grid=(4, ni),
            in_specs=[pl.BlockSpec((tm, D_MODEL), lambda s, i, o: (i, 0)),
                      pl.BlockSpec((1, D_MODEL), lambda s, i, o: (0, 0)), ANY],
            out_specs=[pl.BlockSpec((tm, SHARD_W), lambda s, i, o: (i, o[s])), ANY, ANY],
            scratch_shapes=[pltpu.VMEM((2, D_MODEL, SHARD_W), BF16), pltpu.VMEM((tm, D_MODEL), BF16),
                            pltpu.SemaphoreType.DMA((7,)), pltpu.SemaphoreType.DMA((7,)),
                            pltpu.SemaphoreType.DMA((3,))]),
        out_shape=[jax.ShapeDtypeStruct((t, IN_W), F32), jax.ShapeDtypeStruct((t, D_MODEL), BF16),
                   jax.ShapeDtypeStruct((4, D_MODEL, SHARD_W), BF16)],
        compiler_params=_params(("arbitrary", "arbitrary")),
    )(order, x, norm_w.reshape(1, D_MODEL), w_sh)


def _lane128():
    return lax.broadcasted_iota(jnp.int32, (1, 128), 1)


def _head_sums(v):
    lo = _lane128() < 64
    s_lo = jnp.sum(jnp.where(lo, v, 0.0), axis=1, keepdims=True)
    s_hi = jnp.sum(jnp.where(lo, 0.0, v), axis=1, keepdims=True)
    return jnp.where(lo, s_lo, s_hi)


def _rot_half(t):
    first = (_lane128() % 64) < 32
    return jnp.where(first, -pltpu.roll(t, 96, 1), pltpu.roll(t, 32, 1))


def _prep_tile(t, w, cos, sin):
    r = lax.rsqrt(_head_sums(t * t) * (1.0 / HEAD_DIM) + NORM_EPS)
    tn = t * r * w
    return tn * cos + _rot_half(tn) * sin


def _prep_tile_bwd(t, w, cos, sin, g):
    r = lax.rsqrt(_head_sums(t * t) * (1.0 / HEAD_DIM) + NORM_EPS)
    d_tn = g * cos - _rot_half(g * sin)
    th = t * r
    dw = jnp.sum(d_tn * th, axis=0, keepdims=True)
    gh = d_tn * w
    m = _head_sums(gh * th) * (1.0 / HEAD_DIM)
    return r * (gh - th * m), dw


def _band_mask(n):
    qi = lax.broadcasted_iota(jnp.int32, (BLOCK, 2 * BLOCK), 0) + BLOCK
    ki = lax.broadcasted_iota(jnp.int32, (BLOCK, 2 * BLOCK), 1)
    rel = qi - ki
    return (rel >= 0) & (rel < BLOCK) & ((n > 0) | (ki >= BLOCK))


def _half_select(tile, half):
    lo = _lane128() < 64
    return jnp.where(lo if half == 0 else jnp.logical_not(lo), tile, 0.0)


def _stack_group(tiles, kv_half):
    rows = []
    for t in tiles:
        for half in range(2):
            piece = _half_select(t, half)
            rows.append(piece if half == kv_half else pltpu.roll(piece, 64, 1))
    return jnp.concatenate(rows, axis=0)


def _unstack_group(stacked, kv_half):
    tiles = []
    for i in range(2):
        acc = None
        for half in range(2):
            piece = _half_select(stacked[BLOCK * (2 * i + half):BLOCK * (2 * i + half + 1)], kv_half)
            piece = piece if half == kv_half else pltpu.roll(piece, 64, 1)
            acc = piece if acc is None else acc + piece
        tiles.append(acc)
    return tiles


def _attn_specs(nb):
    last = nb - 1
    qi = lambda n: (jnp.minimum(n, last), 0)
    prev = lambda n: jnp.maximum(n - 1, 0)
    cur = lambda n: jnp.minimum(n, last)
    specs = [
        pl.BlockSpec((BLOCK, ATTN_W), qi),
        pl.BlockSpec((BLOCK, KV_W), lambda n: (cur(n), 4)),
        pl.BlockSpec((BLOCK, KV_W), lambda n: (prev(n), 4)),
        pl.BlockSpec((BLOCK, KV_W), lambda n: (cur(n), 5)),
        pl.BlockSpec((BLOCK, KV_W), lambda n: (prev(n), 5)),
        pl.BlockSpec((BLOCK, 512), lambda n: (cur(n), 3)),
        pl.BlockSpec((BLOCK, 512), lambda n: (cur(n), 4)),
        pl.BlockSpec((BLOCK, 1), lambda n: (cur(n), 0)),
        pl.BlockSpec((BLOCK, 1), lambda n: (prev(n), 0)),
        pl.BlockSpec((1, 128), lambda n: (0, 0)),
        pl.BlockSpec((1, 128), lambda n: (0, 0)),
        pl.BlockSpec((1, 128), lambda n: (0, 0)),
        pl.BlockSpec((1, N_HEADS), lambda n: (0, 0)),
    ]
    return specs


def _attn_common(n, q_ref, kc_ref, kp_ref, vc_ref, vp_ref, pq_ref, pp_ref, invf_ref, qw_ref, kw_ref):
    invf = invf_ref[...]
    ang_q = pq_ref[...] * invf
    ang_p = pp_ref[...] * invf
    cos_q, sin_q = jnp.cos(ang_q), jnp.sin(ang_q)
    cos_k = jnp.concatenate([jnp.cos(ang_p), cos_q], axis=0)
    sin_k = jnp.concatenate([jnp.sin(ang_p), sin_q], axis=0)
    k_raw = jnp.concatenate([kp_ref[...], kc_ref[...]], axis=0)
    vv = jnp.concatenate([vp_ref[...], vc_ref[...]], axis=0).astype(BF16)
    kk = [_prep_tile(k_raw[:, 128 * i:128 * i + 128], kw_ref[...], cos_k, sin_k).astype(BF16) for i in range(2)]
    vt = [vv[:, 128 * i:128 * i + 128] for i in range(2)]
    qv = q_ref[...]
    qt = [_prep_tile(qv[:, 128 * i:128 * i + 128], qw_ref[...], cos_q, sin_q) for i in range(8)]
    return cos_q, sin_q, cos_k, sin_k, k_raw, kk, vt, qt


QK_SCALE = 1.0 / math.sqrt(HEAD_DIM)


def _group_sinks(sink_ref, g):
    return jnp.concatenate([jnp.broadcast_to(sink_ref[:, 4 * g + j:4 * g + j + 1], (BLOCK, 1)) for j in range(4)], axis=0)


def _group_softmax(q4, kk_t, sink, bias):
    s = _dot_nt(q4, kk_t) + bias
    m = jnp.maximum(jnp.max(s, axis=1, keepdims=True), sink)
    p = jnp.exp(s - m)
    es = jnp.exp(sink - m)
    inv = 1.0 / (jnp.sum(p, axis=1, keepdims=True) + es)
    return p * inv, es * inv


def _group_bias(n):
    return jnp.concatenate([jnp.where(_band_mask(n), 0.0, -1e30)] * 4, axis=0)


def _attn_fwd(proj, posf, invf, qw, kw, sinks, later_shards):
    t = proj.shape[0]
    nb = t // BLOCK
    nt = len(later_shards)

    def body(q_ref, kc_ref, kp_ref, vc_ref, vp_ref, za0_ref, za1_ref, pq_ref, pp_ref, invf_ref, qw_ref, kw_ref,
             sink_ref, *rest):
        sh, o_ref, full = rest[:nt], rest[nt], rest[nt + 1:2 * nt + 1]
        ssem, rsem = rest[2 * nt + 1:]
        n = pl.program_id(0)
        start, wait = _bg_gather(sh, full, ssem, rsem)

        @pl.when(n == 0)
        def _():
            start()

        _, _, _, _, _, kk, vt, qt = _attn_common(n, q_ref, kc_ref, kp_ref, vc_ref, vp_ref, pq_ref, pp_ref, invf_ref,
                                                 qw_ref, kw_ref)
        bias = _group_bias(n)
        tiles = []
        for g in range(N_KV_HEADS):
            q4 = (_stack_group(qt[2 * g:2 * g + 2], g % 2) * QK_SCALE).astype(BF16)
            p, _ = _group_softmax(q4, kk[g // 2], _group_sinks(sink_ref, g), bias)
            tiles += _unstack_group(_dot(p.astype(BF16), vt[g // 2]), g % 2)
        za = jnp.concatenate([za0_ref[...], za1_ref[...]], axis=1)
        o_ref[...] = jnp.concatenate(tiles, axis=1) * _silu(za)

        @pl.when(n == nb - 1)
        def _():
            wait()

    return pl.pallas_call(
        body, name="attn_fwd", grid=(nb,), in_specs=_attn_specs(nb) + [ANY] * nt,
        out_specs=[pl.BlockSpec((BLOCK, ATTN_W), lambda n: (n, 0))] + [ANY] * nt,
        out_shape=[jax.ShapeDtypeStruct((t, ATTN_W), F32)]
        + [jax.ShapeDtypeStruct((4,) + s.shape, s.dtype) for s in later_shards],
        scratch_shapes=[pltpu.SemaphoreType.DMA((4 * nt,)), pltpu.SemaphoreType.DMA((4 * nt,))],
        compiler_params=_params(("arbitrary",)),
    )(proj, proj, proj, proj, proj, proj, proj, posf, posf, invf, qw, kw, sinks, *later_shards)


def _attn_bwd(proj, posf, invf, qw, kw, sinks, doa, outgoing):
    t = proj.shape[0]
    nb = t // BLOCK
    last = nb - 1
    nt = len(outgoing)

    def body(q_ref, kc_ref, kp_ref, vc_ref, vp_ref, za0_ref, za1_ref, pq_ref, pp_ref, invf_ref, qw_ref, kw_ref,
             sink_ref, doa_ref, *rest):
        src = rest[:nt]
        dq_ref, dk_ref, dv_ref, dza_ref, gq_ref, gk_ref, gs_ref = rest[nt:nt + 7]
        dst = rest[nt + 7:2 * nt + 7]
        dkk_s, dvv_s, ck_s, cv_s, ssem, rsem = rest[2 * nt + 7:]
        n = pl.program_id(0)
        start, wait = _bg_scatter_devices(src, dst, ssem, rsem)

        @pl.when(n == 0)
        def _():
            start()
            gq_ref[...] = jnp.zeros_like(gq_ref)
            gk_ref[...] = jnp.zeros_like(gk_ref)
            gs_ref[...] = jnp.zeros_like(gs_ref)
            ck_s[...] = jnp.zeros_like(ck_s)
            cv_s[...] = jnp.zeros_like(cv_s)

        @pl.when(n == nb)
        def _():
            dkk_s[...] = jnp.zeros_like(dkk_s)
            dvv_s[...] = jnp.zeros_like(dvv_s)

        @pl.when(n < nb)
        def _():
            cos_q, sin_q, _, _, _, kk, vt, qt = _attn_common(n, q_ref, kc_ref, kp_ref, vc_ref, vp_ref, pq_ref, pp_ref,
                                                             invf_ref, qw_ref, kw_ref)
            bias = _group_bias(n)
            za = jnp.concatenate([za0_ref[...], za1_ref[...]], axis=1)
            doa_v = doa_ref[...]
            do_full = doa_v * _silu(za)
            o_tiles, dq_tiles = [], []
            dkk = [jnp.zeros((2 * BLOCK, 128), F32) for _ in range(2)]
            dvv = [jnp.zeros((2 * BLOCK, 128), F32) for _ in range(2)]
            gsink = jnp.zeros((1, 128), F32)
            lane = _lane128()
            for g in range(N_KV_HEADS):
                q_b = (_stack_group(qt[2 * g:2 * g + 2], g % 2) * QK_SCALE).astype(BF16)
                do_b = _stack_group([do_full[:, 128 * i:128 * i + 128] for i in (2 * g, 2 * g + 1)], g % 2).astype(BF16)
                p, psink = _group_softmax(q_b, kk[g // 2], _group_sinks(sink_ref, g), bias)
                p_b = p.astype(BF16)
                dp = _dot_nt(do_b, vt[g // 2])
                delta = jnp.sum(p * dp, axis=1, keepdims=True)
                ds_b = (p * (dp - delta)).astype(BF16)
                sd = psink * delta
                for j in range(4):
                    gsink = gsink + jnp.where(lane == 4 * g + j, -jnp.sum(sd[BLOCK * j:BLOCK * (j + 1)]), 0.0)
                o_tiles += _unstack_group(_dot(p_b, vt[g // 2]), g % 2)
                dq_tiles += [d * QK_SCALE for d in _unstack_group(_dot(ds_b, kk[g // 2]), g % 2)]
                dkk[g // 2] = dkk[g // 2] + _dot_tn(ds_b, q_b)
                dvv[g // 2] = dvv[g // 2] + _dot_tn(p_b, do_b)
            dza_ref[...] = (doa_v * jnp.concatenate(o_tiles, axis=1) * _dsilu(za)).astype(BF16)
            qv = q_ref[...]
            gq = jnp.zeros((1, 128), F32)
            out = []
            for i in range(8):
                d, dw = _prep_tile_bwd(qv[:, 128 * i:128 * i + 128], qw_ref[...], cos_q, sin_q, dq_tiles[i])
                out.append(d)
                gq = gq + dw
            dq_ref[...] = jnp.concatenate(out, axis=1).astype(BF16)
            gq_ref[...] += gq
            gs_ref[...] += gsink
            dkk_s[...] = jnp.concatenate(dkk, axis=1)
            dvv_s[...] = jnp.concatenate(dvv, axis=1)

        invf = invf_ref[...]
        ang_p = pp_ref[...] * invf
        cos_p, sin_p = jnp.cos(ang_p), jnp.sin(ang_p)
        dk_prev = ck_s[...] + dkk_s[0:BLOCK, :]
        kp = kp_ref[...]
        gk = jnp.zeros((1, 128), F32)
        out = []
        for i in range(2):
            d, dw = _prep_tile_bwd(kp[:, 128 * i:128 * i + 128], kw_ref[...], cos_p, sin_p,
                                   dk_prev[:, 128 * i:128 * i + 128])
            out.append(d)
            gk = gk + dw
        dk_ref[...] = jnp.concatenate(out, axis=1).astype(BF16)
        dv_ref[...] = (cv_s[...] + dvv_s[0:BLOCK, :]).astype(BF16)
        gk_ref[...] += gk
        ck_s[...] = dkk_s[BLOCK:2 * BLOCK, :]
        cv_s[...] = dvv_s[BLOCK:2 * BLOCK, :]

        @pl.when(n == nb)
        def _():
            wait()

    qblk = lambda n: (jnp.minimum(n, last), 0)
    kblk = lambda n: (jnp.maximum(n - 1, 0), 0)
    vec = pl.BlockSpec((1, 128), lambda n: (0, 0))
    return pl.pallas_call(
        body, name="attn_bwd", grid=(nb + 1,),
        in_specs=_attn_specs(nb) + [pl.BlockSpec((BLOCK, ATTN_W), qblk)] + [ANY] * nt,
        out_specs=[pl.BlockSpec((BLOCK, ATTN_W), qblk), pl.BlockSpec((BLOCK, KV_W), kblk),
                   pl.BlockSpec((BLOCK, KV_W), kblk), pl.BlockSpec((BLOCK, ATTN_W), qblk), vec, vec, vec] + [ANY] * nt,
        out_shape=[jax.ShapeDtypeStruct((t, ATTN_W), BF16), jax.ShapeDtypeStruct((t, KV_W), BF16),
                   jax.ShapeDtypeStruct((t, KV_W), BF16), jax.ShapeDtypeStruct((t, ATTN_W), BF16),
                   jax.ShapeDtypeStruct((1, 128), F32), jax.ShapeDtypeStruct((1, 128), F32),
                   jax.ShapeDtypeStruct((1, 128), F32)] + [jax.ShapeDtypeStruct(a.shape, a.dtype) for a in outgoing],
        scratch_shapes=[pltpu.VMEM((2 * BLOCK, KV_W), F32), pltpu.VMEM((2 * BLOCK, KV_W), F32),
                        pltpu.VMEM((BLOCK, KV_W), F32), pltpu.VMEM((BLOCK, KV_W), F32),
                        pltpu.SemaphoreType.DMA((7 * nt,)), pltpu.SemaphoreType.DMA((7 * nt,))],
        compiler_params=_params(("arbitrary",)),
    )(proj, proj, proj, proj, proj, proj, proj, posf, posf, invf, qw, kw, sinks, doa, *outgoing)


def _cmul(ar, ai, br, bi):
    return ar * br - ai * bi, ar * bi + ai * br


def _zoh(a_re, a_im, delta):
    e = jnp.exp(a_re * delta)
    lr, li = e * jnp.cos(a_im * delta), e * jnp.sin(a_im * delta)
    inv = 1.0 / (a_re * a_re + a_im * a_im)
    fr, fi = _cmul(lr - 1.0, li, a_re * inv, -a_im * inv)
    return lr, li, fr, fi


def _ssm_prep(a_re, a_im, log_step, b_re, b_im, seg_len):
    n_sq = int(round(math.log2(seg_len)))
    assert 2 ** n_sq == seg_len

    def body(ar_ref, ai_ref, ls_ref, arx_ref, aix_ref, br_ref, bi_ref, lr_ref, li_ref, pr_ref, pi_ref, bbr_ref, bbi_ref):
        delta = jnp.exp(ls_ref[...])
        lr, li, _, _ = _zoh(ar_ref[...], ai_ref[...], delta)
        lr_ref[...] = lr
        li_ref[...] = li
        pr, pi = lr, li
        for _ in range(n_sq):
            pr, pi = _cmul(pr, pi, pr, pi)
        pr_ref[...] = pr
        pi_ref[...] = pi
        _, _, fr, fi = _zoh(arx_ref[...], aix_ref[...], delta)
        bbr, bbi = _cmul(fr, fi, br_ref[...], bi_ref[...])
        bbr_ref[...] = bbr
        bbi_ref[...] = bbi

    gp = jax.ShapeDtypeStruct((SSM_G, SSM_P), F32)
    gx = jax.ShapeDtypeStruct((SSM_G, SSM_P * SSM_H), F32)
    return pl.pallas_call(body, name="ssm_prep", out_shape=[gp, gp, gp, gp, gx, gx])(
        a_re, a_im, log_step.reshape(SSM_G, 1), jnp.repeat(a_re, SSM_H, axis=1), jnp.repeat(a_im, SSM_H, axis=1),
        b_re.reshape(SSM_G, SSM_P * SSM_H), b_im.reshape(SSM_G, SSM_P * SSM_H))


def _ssm_param_grads(a_re, a_im, log_step, b_re, b_im, dlam_re, dlam_im, dbb_re, dbb_im):
    def body(ar_ref, ai_ref, ls_ref, arx_ref, aix_ref, br_ref, bi_ref, dlr_ref, dli_ref, dbr_ref, dbi_ref,
             gar_ref, gai_ref, gls_ref, gbr_ref, gbi_ref):
        delta = jnp.exp(ls_ref[...])
        ar, ai = ar_ref[...], ai_ref[...]
        lr, li, fr, fi = _zoh(ar, ai, delta)
        _, _, frx, fix = _zoh(arx_ref[...], aix_ref[...], delta)
        dbr, dbi = dbr_ref[...], dbi_ref[...]
        br, bi = br_ref[...], bi_ref[...]
        gbr, gbi = _cmul(frx, -fix, dbr, dbi)
        gbr_ref[...] = gbr
        gbi_ref[...] = gbi
        tr, ti = _cmul(br, -bi, dbr, dbi)
        row = lax.broadcasted_iota(jnp.int32, (SSM_P * SSM_H, SSM_P), 0)
        col = lax.broadcasted_iota(jnp.int32, (SSM_P * SSM_H, SSM_P), 1)
        fold = (row // SSM_H == col).astype(F32)
        dfr = jnp.dot(tr, fold, precision=lax.Precision.HIGHEST, preferred_element_type=F32)
        dfi = jnp.dot(ti, fold, precision=lax.Precision.HIGHEST, preferred_element_type=F32)
        inv = 1.0 / (ar * ar + ai * ai)
        ilr, ili = ar * inv, -ai * inv
        t1r, t1i = _cmul(dfr, dfi, ilr, -ili)
        dlbr, dlbi = dlr_ref[...] + t1r, dli_ref[...] + t1i
        qr, qi = _cmul(fr, fi, ilr, ili)
        t2r, t2i = _cmul(dfr, dfi, qr, -qi)
        glr, gli = -t2r, -t2i
        dzr, dzi = _cmul(dlbr, dlbi, lr, -li)
        gar_ref[...] = glr + dzr * delta
        gai_ref[...] = gli + dzi * delta
        gls_ref[...] = jnp.sum(dzr * ar + dzi * ai, axis=1, keepdims=True) * delta

    gp = jax.ShapeDtypeStruct((SSM_G, SSM_P), F32)
    gx = jax.ShapeDtypeStruct((SSM_G, SSM_P * SSM_H), F32)
    return pl.pallas_call(body, name="ssm_param_grads",
                          out_shape=[gp, gp, jax.ShapeDtypeStruct((SSM_G, 1), F32), gx, gx])(
        a_re, a_im, log_step.reshape(SSM_G, 1), jnp.repeat(a_re, SSM_H, axis=1), jnp.repeat(a_im, SSM_H, axis=1),
        b_re.reshape(SSM_G, SSM_P * SSM_H), b_im.reshape(SSM_G, SSM_P * SSM_H), dlam_re, dlam_im, dbb_re, dbb_im)


def _block_diag_in(bb):
    w = jnp.tile(bb.reshape(SSM_GB, SSM_ST, SSM_H), (1, 1, 16))
    row = lax.broadcasted_iota(jnp.int32, (1, SSM_ST, SSM_CH), 1) // SSM_P
    col = lax.broadcasted_iota(jnp.int32, (1, SSM_ST, SSM_CH), 2) // SSM_H
    return jnp.where(row == col, w, 0.0)


def _block_diag_out(c):
    w = jnp.tile(c.reshape(SSM_GB, SSM_CH, SSM_P), (1, 1, 16))
    row = lax.broadcasted_iota(jnp.int32, (1, SSM_CH, SSM_ST), 1) // SSM_H
    col = lax.broadcasted_iota(jnp.int32, (1, SSM_CH, SSM_ST), 2) // SSM_P
    return jnp.where(row == col, w, 0.0)


def _diag_blocks(full):
    w = full.reshape(SSM_GB, 16, SSM_H, 16, SSM_P)
    idx = jnp.arange(16)
    return w[:, idx, :, idx, :].transpose(1, 0, 2, 3).reshape(SSM_G, SSM_H, SSM_P)


def _permute_rows(a):
    t, c = a.shape
    return a.reshape(N_SEG, t // N_SEG, c).transpose(1, 0, 2).reshape(t, c)


def _unpermute_rows(a):
    t, c = a.shape
    return a.reshape(t // N_SEG, N_SEG, c).transpose(1, 0, 2).reshape(t, c)


def _scan_fwd(src_ref, dst_ref, lam_r_ref, lam_i_ref, init_ref, final_ref, steps):
    for k in range(SSM_ST // SCAN_LW):
        re = pl.ds(k * SCAN_LW, SCAN_LW)
        im = pl.ds(SSM_ST + k * SCAN_LW, SCAN_LW)
        lr, li = lam_r_ref[:, re], lam_i_ref[:, re]

        def step(i, carry, re=re, im=im, lr=lr, li=li):
            hr, hi = carry
            rows = pl.ds(pl.multiple_of(i * 8, 8), 8)
            nr = lr * hr - li * hi + src_ref[rows, re]
            ni = lr * hi + li * hr + src_ref[rows, im]
            if dst_ref is not None:
                dst_ref[rows, re] = nr
                dst_ref[rows, im] = ni
            return nr, ni

        hr, hi = lax.fori_loop(0, steps, step, (init_ref[:, re], init_ref[:, im]), unroll=4)
        final_ref[:, re] = hr
        final_ref[:, im] = hi


def _ssm_specs(t):
    col = lambda g: (0, g)
    gb3 = lambda g: (g, 0, 0)
    return dict(
        rows=pl.BlockSpec((t, SSM_CH), col),
        lam=pl.BlockSpec((None, N_SEG, SSM_ST), gb3),
        w_in=pl.BlockSpec((None, 2 * SSM_ST, SSM_CH), gb3),
        w_out=pl.BlockSpec((None, SSM_CH, 2 * SSM_ST), gb3),
        vec=pl.BlockSpec((1, SSM_CH), col),
    )


def _segment_states(x_ref, pw_r_ref, pw_i_ref, out_ref, reverse):
    re, im = pl.ds(0, SSM_ST), pl.ds(SSM_ST, SSM_ST)
    pr, pi = pw_r_ref[0:1, :], pw_i_ref[0:1, :]
    first = N_SEG - 1 if reverse else 0
    out_ref[first:first + 1, :] = jnp.zeros((1, 2 * SSM_ST), F32)
    order = range(N_SEG - 1, 0, -1) if reverse else range(N_SEG - 1)
    for s in order:
        d = s - 1 if reverse else s + 1
        hr, hi = out_ref[s:s + 1, re], out_ref[s:s + 1, im]
        if reverse:
            nr, ni = pr * hr + pi * hi, pr * hi - pi * hr
        else:
            nr, ni = pr * hr - pi * hi, pr * hi + pi * hr
        out_ref[d:d + 1, re] = nr + x_ref[s:s + 1, re]
        out_ref[d:d + 1, im] = ni + x_ref[s:s + 1, im]


def _ssm_fwd(up, lam_r, lam_i, pw_r, pw_i, w_in, w_out, d_skip):
    t = up.shape[0]
    nch = t // SCAN_ROWS
    steps = SCAN_ROWS // N_SEG
    sp = _ssm_specs(t)

    def body(u_ref, lr_ref, li_ref, pr_ref, pi_ref, wi_ref, wo_ref, d_ref, y_ref, hc_ref, bu_s, car_s, seg_s):
        def load_bu(j):
            rows = pl.ds(pl.multiple_of(j * SCAN_ROWS, SCAN_ROWS), SCAN_ROWS)
            bu_s[...] = _dot_nt(u_ref[rows, :].astype(BF16), wi_ref[...])

        car_s[...] = jnp.zeros_like(car_s)

        def chunk1(j, c):
            load_bu(j)
            _scan_fwd(bu_s, None, lr_ref, li_ref, car_s, car_s, steps)
            return c

        lax.fori_loop(0, nch, chunk1, 0)
        _segment_states(car_s, pr_ref, pi_ref, seg_s, reverse=False)
        car_s[...] = seg_s[...]

        def chunk2(j, c):
            load_bu(j)
            hc_ref[j] = car_s[...]
            _scan_fwd(bu_s, bu_s, lr_ref, li_ref, car_s, car_s, steps)
            rows = pl.ds(pl.multiple_of(j * SCAN_ROWS, SCAN_ROWS), SCAN_ROWS)
            y_ref[rows, :] = _dot_nt(bu_s[...].astype(BF16), wo_ref[...]) + d_ref[...] * u_ref[rows, :]
            return c

        lax.fori_loop(0, nch, chunk2, 0)

    return pl.pallas_call(
        body, name="ssm_fwd", grid=(SSM_GB,),
        in_specs=[sp["rows"], sp["lam"], sp["lam"], sp["lam"], sp["lam"], sp["w_in"], sp["w_out"], sp["vec"]],
        out_specs=[sp["rows"], pl.BlockSpec((None, nch, N_SEG, 2 * SSM_ST), lambda g: (g, 0, 0, 0))],
        out_shape=[jax.ShapeDtypeStruct((t, SSM_W), F32), jax.ShapeDtypeStruct((SSM_GB, nch, N_SEG, 2 * SSM_ST), F32)],
        scratch_shapes=[pltpu.VMEM((SCAN_ROWS, 2 * SSM_ST), F32), pltpu.VMEM((N_SEG, 2 * SSM_ST), F32),
                        pltpu.VMEM((N_SEG, 2 * SSM_ST), F32)],
        compiler_params=_params(("parallel",)),
    )(up, lam_r, lam_i, pw_r, pw_i, w_in, w_out, d_skip)


def _ssm_bwd(up, dyp, hc, lam_r, lam_i, pw_r, pw_i, w_in, w_out, d_skip):
    t = up.shape[0]
    nch = t // SCAN_ROWS
    steps = SCAN_ROWS // N_SEG
    sp = _ssm_specs(t)

    def body(u_ref, dy_ref, hc_ref, lr_ref, li_ref, pr_ref, pi_ref, wi_ref, wo_ref, d_ref,
             du_ref, gwi_ref, gwo_ref, glam_ref, gd_ref, bu_s, h_s, e_s, car_s, seg_s, acc_s):
        def chunk_rows(j):
            return pl.ds(pl.multiple_of(j * SCAN_ROWS, SCAN_ROWS), SCAN_ROWS)

        def load_e(j):
            e_s[...] = _dot(dy_ref[chunk_rows(j), :].astype(BF16), wo_ref[...])

        def scan_rev(j, accumulate):
            for k in range(SSM_ST // SCAN_LW):
                re = pl.ds(k * SCAN_LW, SCAN_LW)
                im = pl.ds(SSM_ST + k * SCAN_LW, SCAN_LW)
                lr, li = lr_ref[:, re], li_ref[:, re]

                def step(ii, carry, re=re, im=im, lr=lr, li=li):
                    i = steps - 1 - ii
                    rows = pl.ds(pl.multiple_of(i * 8, 8), 8)
                    if accumulate:
                        gr, gi, ar, ai = carry
                    else:
                        gr, gi = carry
                    nr = lr * gr + li * gi + e_s[rows, re]
                    ni = lr * gi - li * gr + e_s[rows, im]
                    if not accumulate:
                        return nr, ni
                    e_s[rows, re] = nr
                    e_s[rows, im] = ni
                    pr_, pi_ = h_s[rows, re], h_s[rows, im]
                    return nr, ni, ar + nr * pr_ + ni * pi_, ai + ni * pr_ - nr * pi_

                init = (car_s[:, re], car_s[:, im])
                if accumulate:
                    init = init + (acc_s[:, re], acc_s[:, im])
                out = lax.fori_loop(0, steps, step, init, unroll=4)
                car_s[:, re] = out[0]
                car_s[:, im] = out[1]
                if accumulate:
                    acc_s[:, re] = out[2]
                    acc_s[:, im] = out[3]

        car_s[...] = jnp.zeros_like(car_s)

        def pass1(jj, c):
            load_e(nch - 1 - jj)
            scan_rev(nch - 1 - jj, False)
            return c

        lax.fori_loop(0, nch, pass1, 0)
        _segment_states(car_s, pr_ref, pi_ref, seg_s, reverse=True)
        car_s[...] = seg_s[...]
        acc_s[...] = jnp.zeros_like(acc_s)
        gwi_ref[...] = jnp.zeros_like(gwi_ref)
        gwo_ref[...] = jnp.zeros_like(gwo_ref)
        gd_ref[...] = jnp.zeros_like(gd_ref)

        def pass2(jj, c):
            j = nch - 1 - jj
            rows = chunk_rows(j)
            u = u_ref[rows, :]
            dy = dy_ref[rows, :]
            u_b, dy_b = u.astype(BF16), dy.astype(BF16)
            bu_s[...] = _dot_nt(u_b, wi_ref[...])
            h_s[0:N_SEG, :] = hc_ref[j]
            seg_s[...] = hc_ref[j]
            _scan_fwd(bu_s, h_s.at[pl.ds(N_SEG, SCAN_ROWS), :], lr_ref, li_ref, seg_s, seg_s, steps)
            load_e(j)
            scan_rev(j, True)
            g_b = e_s[...].astype(BF16)
            du_ref[rows, :] = (_dot(g_b, wi_ref[...]) + d_ref[...] * dy).astype(du_ref.dtype)
            gwi_ref[...] += _dot_tn(u_b, g_b)
            gwo_ref[...] += _dot_tn(dy_b, h_s[pl.ds(N_SEG, SCAN_ROWS), :].astype(BF16))
            gd_ref[...] += jnp.sum(dy * u, axis=0, keepdims=True)
            return c

        lax.fori_loop(0, nch, pass2, 0)
        glam_ref[...] = jnp.sum(acc_s[...], axis=0, keepdims=True)

    mat = pl.BlockSpec((None, SSM_CH, 2 * SSM_ST), lambda g: (g, 0, 0))
    return pl.pallas_call(
        body, name="ssm_bwd", grid=(SSM_GB,),
        in_specs=[sp["rows"], sp["rows"], pl.BlockSpec((None, nch, N_SEG, 2 * SSM_ST), lambda g: (g, 0, 0, 0)),
                  sp["lam"], sp["lam"], sp["lam"], sp["lam"], sp["w_in"], sp["w_out"], sp["vec"]],
        out_specs=[sp["rows"], mat, mat, pl.BlockSpec((None, 1, 2 * SSM_ST), lambda g: (g, 0, 0)), sp["vec"]],
        out_shape=[jax.ShapeDtypeStruct((t, SSM_W), BF16), jax.ShapeDtypeStruct((SSM_GB, SSM_CH, 2 * SSM_ST), F32),
                   jax.ShapeDtypeStruct((SSM_GB, SSM_CH, 2 * SSM_ST), F32),
                   jax.ShapeDtypeStruct((SSM_GB, 1, 2 * SSM_ST), F32), jax.ShapeDtypeStruct((1, SSM_W), F32)],
        scratch_shapes=[pltpu.VMEM((SCAN_ROWS, 2 * SSM_ST), F32), pltpu.VMEM((SCAN_ROWS + N_SEG, 2 * SSM_ST), F32),
                        pltpu.VMEM((SCAN_ROWS, 2 * SSM_ST), F32), pltpu.VMEM((N_SEG, 2 * SSM_ST), F32),
                        pltpu.VMEM((N_SEG, 2 * SSM_ST), F32), pltpu.VMEM((N_SEG, 2 * SSM_ST), F32)],
        compiler_params=_params(("parallel",)),
    )(up, dyp, hc, lam_r, lam_i, pw_r, pw_i, w_in, w_out, d_skip)


def _glu_fwd(y, zs, w_glu, b_glu):
    t = y.shape[0]
    tm = 512

    def body(y_ref, z_ref, w_ref, b_ref, o_ref, yg_ref):
        yg = _gelu(y_ref[...])
        yg_b = yg.astype(BF16)
        a = _dot(yg_b, w_ref[...]) + b_ref[...]
        o_ref[...] = yg * _sigmoid(a) * _silu(z_ref[...])
        yg_ref[...] = yg_b

    row = pl.BlockSpec((tm, SSM_W), lambda i: (i, 0))
    return pl.pallas_call(
        body, name="glu_fwd", grid=(t // tm,),
        in_specs=[row, row, pl.BlockSpec((SSM_W, SSM_W), lambda i: (0, 0)), pl.BlockSpec((1, SSM_W), lambda i: (0, 0))],
        out_specs=[row, row],
        out_shape=[jax.ShapeDtypeStruct((t, SSM_W), F32), jax.ShapeDtypeStruct((t, SSM_W), BF16)],
        compiler_params=_params(("parallel",)),
    )(y, zs, w_glu, b_glu)


def _glu_bwd(y, zs, dos, w_glu, b_glu):
    t = y.shape[0]
    tm = 512

    def body(y_ref, z_ref, do_ref, w_ref, b_ref, dy_ref, dz_ref, da_ref, gb_ref):
        @pl.when(pl.program_id(0) == 0)
        def _():
            gb_ref[...] = jnp.zeros_like(gb_ref)

        yv, z, do = y_ref[...], z_ref[...], do_ref[...]
        yg = _gelu(yv)
        sg = _sigmoid(_dot(yg.astype(BF16), w_ref[...]) + b_ref[...])
        dy2 = do * _silu(z)
        dz_ref[...] = (do * yg * sg * _dsilu(z)).astype(BF16)
        da = dy2 * yg * sg * (1.0 - sg)
        da_b = da.astype(BF16)
        da_ref[...] = da_b
        gb_ref[...] += jnp.sum(da, axis=0, keepdims=True)
        dyg = dy2 * sg + _dot_nt(da_b, w_ref[...])
        dy_ref[...] = dyg * _dgelu(yv)

    row = pl.BlockSpec((tm, SSM_W), lambda i: (i, 0))
    vec = pl.BlockSpec((1, SSM_W), lambda i: (0, 0))
    return pl.pallas_call(
        body, name="glu_bwd", grid=(t // tm,),
        in_specs=[row, row, row, pl.BlockSpec((SSM_W, SSM_W), lambda i: (0, 0)), vec],
        out_specs=[row, row, row, vec],
        out_shape=[jax.ShapeDtypeStruct((t, SSM_W), F32), jax.ShapeDtypeStruct((t, SSM_W), BF16),
                   jax.ShapeDtypeStruct((t, SSM_W), BF16), jax.ShapeDtypeStruct((1, SSM_W), F32)],
        compiler_params=_params(("arbitrary",)),
    )(y, zs, dos, w_glu, b_glu)


def _rms(o):
    return lax.rsqrt(jnp.mean(o * o, axis=1, keepdims=True) + NORM_EPS)


def _outproj(oa, os_, aw, sw, w_out, x, target):
    t = x.shape[0]
    tm = 256

    def body(oa_ref, os_ref, aw_ref, sw_ref, w_ref, x_ref, t_ref, mg_ref, do_ref, ls_ref):
        @pl.when(pl.program_id(0) == 0)
        def _():
            ls_ref[...] = jnp.zeros_like(ls_ref)

        a, s = oa_ref[...], os_ref[...]
        merged = jnp.concatenate([a * _rms(a) * aw_ref[...], s * _rms(s) * sw_ref[...]], axis=1).astype(BF16)
        mg_ref[...] = merged
        err = x_ref[...] + _dot(merged, w_ref[...]) - t_ref[...]
        do_ref[...] = err * (1.0 / D_MODEL)
        ls_ref[...] += jnp.sum(err * err)

    half = pl.BlockSpec((tm, ATTN_W), lambda i: (i, 0))
    full = pl.BlockSpec((tm, D_MODEL), lambda i: (i, 0))
    vec = pl.BlockSpec((1, ATTN_W), lambda i: (0, 0))
    return pl.pallas_call(
        body, name="outproj", grid=(t // tm,),
        in_specs=[half, half, vec, vec, pl.BlockSpec((D_MODEL, D_MODEL), lambda i: (0, 0)), full, full],
        out_specs=[full, full, pl.BlockSpec((8, 128), lambda i: (0, 0))],
        out_shape=[jax.ShapeDtypeStruct((t, D_MODEL), BF16), jax.ShapeDtypeStruct((t, D_MODEL), F32),
                   jax.ShapeDtypeStruct((8, 128), F32)],
        compiler_params=_params(("arbitrary",)),
    )(oa, os_, aw, sw, w_out, x, target)


def _outproj_bwd(dout, oa, os_, aw, sw, w_out):
    t = dout.shape[0]
    tm = 256

    def norm_bwd(o, w, dm):
        r = _rms(o)
        yh = o * r
        gh = dm * w
        return r * (gh - yh * jnp.mean(gh * yh, axis=1, keepdims=True)), jnp.sum(dm * yh, axis=0, keepdims=True)

    def body(do_ref, oa_ref, os_ref, aw_ref, sw_ref, w_ref, da_ref, ds_ref, ga_ref, gs_ref):
        @pl.when(pl.program_id(0) == 0)
        def _():
            ga_ref[...] = jnp.zeros_like(ga_ref)
            gs_ref[...] = jnp.zeros_like(gs_ref)

        dm = _dot_nt(do_ref[...].astype(BF16), w_ref[...])
        da, ga = norm_bwd(oa_ref[...], aw_ref[...], dm[:, :ATTN_W])
        ds, gs = norm_bwd(os_ref[...], sw_ref[...], dm[:, ATTN_W:])
        da_ref[...] = da
        ds_ref[...] = ds
        ga_ref[...] += ga
        gs_ref[...] += gs

    half = pl.BlockSpec((tm, ATTN_W), lambda i: (i, 0))
    full = pl.BlockSpec((tm, D_MODEL), lambda i: (i, 0))
    vec = pl.BlockSpec((1, ATTN_W), lambda i: (0, 0))
    return pl.pallas_call(
        body, name="outproj_bwd", grid=(t // tm,),
        in_specs=[full, half, half, vec, vec, pl.BlockSpec((D_MODEL, D_MODEL), lambda i: (0, 0))],
        out_specs=[half, half, vec, vec],
        out_shape=[jax.ShapeDtypeStruct((t, ATTN_W), F32), jax.ShapeDtypeStruct((t, ATTN_W), F32),
                   jax.ShapeDtypeStruct((1, ATTN_W), F32), jax.ShapeDtypeStruct((1, ATTN_W), F32)],
        compiler_params=_params(("arbitrary",)),
    )(dout, oa, os_, aw, sw, w_out)


def _inproj_bwd(dproj, w_slabs, x, norm_w, dout, outgoing):
    t = x.shape[0]
    tm = 512
    nt = len(outgoing)
    ni = t // tm

    def body(dp_ref, w_ref, x_ref, nw_ref, do_ref, *rest):
        src, (gx_ref, gw_ref), dst = rest[:nt], rest[nt:nt + 2], rest[nt + 2:2 * nt + 2]
        acc_ref, ssem, rsem = rest[2 * nt + 2:]
        i, j = pl.program_id(0), pl.program_id(1)
        start, wait = _bg_scatter_chips(src, dst, ssem, rsem)

        @pl.when((i == 0) & (j == 0))
        def _():
            start()
            gw_ref[...] = jnp.zeros_like(gw_ref)

        @pl.when(j == 0)
        def _():
            acc_ref[...] = jnp.zeros_like(acc_ref)

        acc_ref[...] += _dot_nt(dp_ref[...], w_ref[...])

        @pl.when(j == 3)
        def _():
            xv = x_ref[...]
            r = lax.rsqrt(jnp.mean(xv * xv, axis=1, keepdims=True) + NORM_EPS)
            yh = xv * r
            dh = acc_ref[...]
            gh = dh * nw_ref[...]
            gx_ref[...] = do_ref[...] + r * (gh - yh * jnp.mean(gh * yh, axis=1, keepdims=True))
            gw_ref[...] += jnp.sum(dh * yh, axis=0, keepdims=True)

        @pl.when((i == ni - 1) & (j == 3))
        def _():
            wait()

    full = pl.BlockSpec((tm, D_MODEL), lambda i, j: (i, 0))
    vec = pl.BlockSpec((1, D_MODEL), lambda i, j: (0, 0))
    return pl.pallas_call(
        body, name="inproj_bwd", grid=(ni, 4),
        in_specs=[pl.BlockSpec((tm, SHARD_W), lambda i, j: (i, j)),
                  pl.BlockSpec((None, D_MODEL, SHARD_W), lambda i, j: (j, 0, 0)), full, vec, full] + [ANY] * nt,
        out_specs=[full, vec] + [ANY] * nt,
        out_shape=[jax.ShapeDtypeStruct((t, D_MODEL), F32), jax.ShapeDtypeStruct((1, D_MODEL), F32)]
        + [jax.ShapeDtypeStruct(a.shape, a.dtype) for a in outgoing],
        scratch_shapes=[pltpu.VMEM((tm, D_MODEL), F32), pltpu.SemaphoreType.DMA((3 * nt,)),
                        pltpu.SemaphoreType.DMA((3 * nt,))],
        compiler_params=_params(("arbitrary", "arbitrary")),
    )(dproj, w_slabs, x, norm_w.reshape(1, D_MODEL), dout, *outgoing)


def _adamw_math(w_ref, g_ref, m_ref, v_ref, d_ref, nm_ref, nv_ref):
    gv = g_ref[...]
    nm = ADAM_B1 * m_ref[...] + (1.0 - ADAM_B1) * gv
    nv = ADAM_B2 * v_ref[...] + (1.0 - ADAM_B2) * (gv * gv)
    m_hat = nm / (1.0 - ADAM_B1 ** ADAM_STEP)
    v_hat = nv / (1.0 - ADAM_B2 ** ADAM_STEP)
    d_ref[...] = -ADAM_LR * (m_hat / (jnp.sqrt(v_hat) + ADAM_EPS) + ADAM_WD * w_ref[...])
    nm_ref[...] = nm
    nv_ref[...] = nv


def _adamw_halves(w, mine, theirs, m, v, c_idx, *, rows, name):
    hr, cols = mine.shape
    nblk = hr // rows

    def body(c_ref, w_ref, a_ref, b_ref, m_ref, v_ref, g_ref, d_ref, nm_ref, nv_ref):
        g_ref[...] = jnp.where(pl.program_id(0) == c_ref[0], a_ref[...], b_ref[...])
        _adamw_math(w_ref, g_ref, m_ref, v_ref, d_ref, nm_ref, nv_ref)

    full = pl.BlockSpec((rows, cols), lambda h, i, c: (h * nblk + i, 0))
    part = pl.BlockSpec((rows, cols), lambda h, i, c: (i, 0))
    shp = jax.ShapeDtypeStruct((2 * hr, cols), F32)
    return pl.pallas_call(
        body, name=name,
        grid_spec=pltpu.PrefetchScalarGridSpec(num_scalar_prefetch=1, grid=(2, nblk),
                                               in_specs=[full, part, part, full, full], out_specs=[full] * 4),
        out_shape=[shp] * 4, compiler_params=_params(("parallel", "parallel")),
    )(c_idx, w, mine, theirs, m, v)


def _adamw(w, g, m, v, *, rows, name):
    r, c = w.shape

    def body(w_ref, g_ref, m_ref, v_ref, d_ref, nm_ref, nv_ref):
        _adamw_math(w_ref, g_ref, m_ref, v_ref, d_ref, nm_ref, nv_ref)

    blk = pl.BlockSpec((rows, c), lambda i: (i, 0))
    shp = jax.ShapeDtypeStruct((r, c), F32)
    return pl.pallas_call(body, name=name, grid=(r // rows,), in_specs=[blk] * 4, out_specs=[blk] * 3,
                          out_shape=[shp] * 3, compiler_params=_params(("parallel",)))(w, g, m, v)


def _remote(src, dst, ssem, rsem, dev):
    return pltpu.make_async_remote_copy(src_ref=src, dst_ref=dst, send_sem=ssem, recv_sem=rsem, device_id=dev,
                                        device_id_type=pl.DeviceIdType.MESH)


def _mesh_pos():
    return lax.axis_index("x"), lax.axis_index("y"), lax.axis_index("c")


def _other_chips(x, y):
    return [(1 - x, y), (x, 1 - y), (1 - x, 1 - y)]


def _flips():
    return [(dx, dy, dc) for dx in (0, 1) for dy in (0, 1) for dc in (0, 1) if (dx, dy, dc) != (0, 0, 0)]


def _background(sends, arrivals):
    def start():
        for cp in sends():
            cp.start()

    def wait():
        for cp in arrivals():
            cp.wait_recv()
        for cp in sends():
            cp.wait_send()

    return start, wait


def _bg_gather(sh, full, ssem, rsem):
    x, y, c = _mesh_pos()
    me = 2 * x + y
    peers = [(px, py, c) for px, py in _other_chips(x, y)] + [(x, y, 1 - c)]
    slots = [2 * px + py for px, py in _other_chips(x, y)] + [me]
    pairs = [(i, k) for i in range(len(sh)) for k in range(4)]
    return _background(
        lambda: [_remote(sh[i], full[i].at[me], ssem.at[4 * i + k], rsem.at[4 * i + k], peers[k]) for i, k in pairs],
        lambda: [_remote(full[i].at[slots[k]], full[i].at[slots[k]], ssem.at[4 * i + k], rsem.at[4 * i + k], peers[k])
                 for i, k in pairs])


def _bg_scatter_devices(src, dst, ssem, rsem):
    x, y, c = _mesh_pos()
    me = 4 * x + 2 * y + c
    peers = []
    for dx, dy, dc in _flips():
        px, py, pc = jnp.bitwise_xor(x, dx), jnp.bitwise_xor(y, dy), jnp.bitwise_xor(c, dc)
        peers.append(((px, py, pc), 4 * px + 2 * py + pc))
    pairs = [(i, k) for i in range(len(src)) for k in range(7)]
    return _background(
        lambda: [_remote(src[i].at[peers[k][1]], dst[i].at[me], ssem.at[7 * i + k], rsem.at[7 * i + k], peers[k][0])
                 for i, k in pairs],
        lambda: [_remote(dst[i].at[peers[k][1]], dst[i].at[peers[k][1]], ssem.at[7 * i + k], rsem.at[7 * i + k],
                         peers[k][0]) for i, k in pairs])


def _bg_scatter_chips(src, dst, ssem, rsem):
    x, y, c = _mesh_pos()
    me = 2 * x + y
    chips = _other_chips(x, y)
    pairs = [(i, k) for i in range(len(src)) for k in range(3)]
    slot = lambda k: 2 * chips[k][0] + chips[k][1]
    return _background(
        lambda: [_remote(src[i].at[slot(k)], dst[i].at[me], ssem.at[3 * i + k], rsem.at[3 * i + k], (*chips[k], c))
                 for i, k in pairs],
        lambda: [_remote(dst[i].at[slot(k)], dst[i].at[slot(k)], ssem.at[3 * i + k], rsem.at[3 * i + k], (*chips[k], c))
                 for i, k in pairs])


def _pair_swap(arrays):
    nt = len(arrays)

    def body(*refs):
        src, dst = refs[:nt], refs[nt:2 * nt]
        ssem, rsem = refs[2 * nt:]
        x, y, c = _mesh_pos()
        cps = [_remote(src[i].at[:, 1 - c], dst[i], ssem.at[i], rsem.at[i], (x, y, 1 - c)) for i in range(nt)]
        for cp in cps:
            cp.start()
        for cp in cps:
            cp.wait_recv()
        for cp in cps:
            cp.wait_send()

    return pl.pallas_call(
        body, name="pair_swap", in_specs=[ANY] * nt, out_specs=[ANY] * nt,
        out_shape=[jax.ShapeDtypeStruct((4,) + a.shape[2:], a.dtype) for a in arrays],
        scratch_shapes=[pltpu.SemaphoreType.DMA((nt,)), pltpu.SemaphoreType.DMA((nt,))],
    )(*arrays)


def _half_swap(arrays):
    nt = len(arrays)

    def body(*refs):
        src, dst = refs[:nt], refs[nt:2 * nt]
        ssem, rsem = refs[2 * nt:]
        x, y, c = _mesh_pos()
        cps = [_remote(src[i], dst[i], ssem.at[i], rsem.at[i], (x, y, 1 - c)) for i in range(nt)]
        for cp in cps:
            cp.start()
        for cp in cps:
            cp.wait_recv()
        for cp in cps:
            cp.wait_send()

    return pl.pallas_call(
        body, name="half_swap", in_specs=[ANY] * nt, out_specs=[ANY] * nt,
        out_shape=[jax.ShapeDtypeStruct(a.shape, a.dtype) for a in arrays],
        scratch_shapes=[pltpu.SemaphoreType.DMA((nt,)), pltpu.SemaphoreType.DMA((nt,))],
    )(*arrays)


def _exchange_slices(src, scatter, name):
    def body(src_ref, dst_ref, ssem, rsem, lsem):
        x, y, c = _mesh_pos()
        me = 4 * x + 2 * y + c
        local = pltpu.make_async_copy(src_ref.at[me] if scatter else src_ref, dst_ref.at[me], lsem)
        local.start()
        cps = []
        for k, (dx, dy, dc) in enumerate(_flips()):
            px, py, pc = jnp.bitwise_xor(x, dx), jnp.bitwise_xor(y, dy), jnp.bitwise_xor(c, dc)
            peer = 4 * px + 2 * py + pc
            cp = _remote(src_ref.at[peer] if scatter else src_ref, dst_ref.at[me], ssem.at[k], rsem.at[k],
                         (px, py, pc))
            cp.start()
            cps.append((cp, peer))
        for k, (cp, peer) in enumerate(cps):
            slot = dst_ref.at[peer]
            _remote(slot, slot, ssem.at[k], rsem.at[k], (x, y, c)).wait_recv()
        for cp, _ in cps:
            cp.wait_send()
        local.wait()

    return pl.pallas_call(
        body, name=name, in_specs=[ANY], out_specs=ANY,
        out_shape=jax.ShapeDtypeStruct((8,) + src.shape[-2:], src.dtype),
        scratch_shapes=[pltpu.SemaphoreType.DMA((7,)), pltpu.SemaphoreType.DMA((7,)), pltpu.SemaphoreType.DMA],
    )(src)


def _add_halves(g, recv, c_idx, *, rows, name):
    _, _, hr, cols = g.shape

    def body(c_ref, g_ref, r_ref, o_ref):
        o_ref[...] = (g_ref[...] + r_ref[...].astype(F32)).astype(BF16)

    return pl.pallas_call(
        body, name=name,
        grid_spec=pltpu.PrefetchScalarGridSpec(
            num_scalar_prefetch=1, grid=(4, hr // rows),
            in_specs=[pl.BlockSpec((None, None, rows, cols), lambda j, i, c: (j, c[0], i, 0)),
                      pl.BlockSpec((None, rows, cols), lambda j, i, c: (j, i, 0))],
            out_specs=pl.BlockSpec((None, rows, cols), lambda j, i, c: (j, i, 0))),
        out_shape=jax.ShapeDtypeStruct((4, hr, cols), BF16),
        compiler_params=_params(("parallel", "parallel")),
    )(c_idx, g, recv)


def _sum_peers(slots, own, idx, *, rows, name):
    n, r, cols = slots.shape

    def body(me_ref, *refs):
        me = me_ref[0]
        mine = refs[n][...].astype(F32)
        acc = None
        for k in range(n):
            term = jnp.where(me == k, mine, refs[k][...].astype(F32))
            acc = term if acc is None else acc + term
        refs[n + 1][...] = acc

    def slot_spec(k):
        return pl.BlockSpec((None, rows, cols), lambda i, me: (jnp.where(me[0] == k, (k + 1) % n, k), i, 0))

    return pl.pallas_call(
        body, name=name,
        grid_spec=pltpu.PrefetchScalarGridSpec(
            num_scalar_prefetch=1, grid=(r // rows,),
            in_specs=[slot_spec(k) for k in range(n)] + [pl.BlockSpec((None, rows, cols), lambda i, me: (me[0], i, 0))],
            out_specs=pl.BlockSpec((rows, cols), lambda i, me: (i, 0))),
        out_shape=jax.ShapeDtypeStruct((r, cols), F32),
        compiler_params=_params(("parallel",)),
    )(idx, *([slots] * n), own)


def _sum_slots(slots, *, rows, name):
    n, r, cols = slots.shape

    def body(s_ref, o_ref):
        acc = s_ref[0].astype(F32)
        for k in range(1, n):
            acc = acc + s_ref[k].astype(F32)
        o_ref[...] = acc

    return pl.pallas_call(
        body, name=name, grid=(r // rows,),
        in_specs=[pl.BlockSpec((n, rows, cols), lambda i: (0, i, 0))],
        out_specs=pl.BlockSpec((rows, cols), lambda i: (i, 0)),
        out_shape=jax.ShapeDtypeStruct((r, cols), F32),
        compiler_params=_params(("parallel",)),
    )(slots)


def _pack_small(d, names, rows):
    flat = jnp.concatenate([d[n].astype(F32).reshape(-1) for n in names])
    return jnp.pad(flat, (0, rows * 128 - flat.shape[0])).reshape(rows, 128)


def _unpack_small(p, names):
    flat = p.reshape(-1)
    out, off = {}, 0
    for n in names:
        size = math.prod(SMALL_SHAPES[n])
        out[n] = flat[off:off + size].reshape(SMALL_SHAPES[n])
        off += size
    return out


def _adamw_3d(w, g, m, v, *, name):
    def body(w_ref, g_ref, m_ref, v_ref, d_ref, nm_ref, nv_ref):
        _adamw_math(w_ref, g_ref, m_ref, v_ref, d_ref, nm_ref, nv_ref)

    blk = pl.BlockSpec((8,) + w.shape[1:], lambda i: (i, 0, 0))
    shp = jax.ShapeDtypeStruct(w.shape, F32)
    return pl.pallas_call(body, name=name, grid=(w.shape[0] // 8,), in_specs=[blk] * 4, out_specs=[blk] * 3,
                          out_shape=[shp] * 3, compiler_params=_params(("parallel",)))(w, g, m, v)


def kernel(x, positions, norm_w, w_in, q_norm_w, k_norm_w, sinks, a_re, a_im, log_step, b_re, b_im, c_re, c_im, d_skip, w_glu, b_glu, attn_out_norm_w, ssm_out_norm_w, w_out, loss_target, m_norm_w, m_w_in, m_q_norm_w, m_k_norm_w, m_sinks, m_a_re, m_a_im, m_log_step, m_b_re, m_b_im, m_c_re, m_c_im, m_d_skip, m_w_glu, m_b_glu, m_attn_out_norm_w, m_ssm_out_norm_w, m_w_out, v_norm_w, v_w_in, v_q_norm_w, v_k_norm_w, v_sinks, v_a_re, v_a_im, v_log_step, v_b_re, v_b_im, v_c_re, v_c_im, v_d_skip, v_w_glu, v_b_glu, v_attn_out_norm_w, v_ssm_out_norm_w, v_w_out):
    small_w = dict(norm_w=norm_w, q_norm_w=q_norm_w, k_norm_w=k_norm_w, sinks=sinks, a_re=a_re, a_im=a_im,
                   log_step=log_step, b_re=b_re, b_im=b_im, c_re=c_re, c_im=c_im, d_skip=d_skip, b_glu=b_glu,
                   attn_out_norm_w=attn_out_norm_w, ssm_out_norm_w=ssm_out_norm_w)
    small_m = dict(norm_w=m_norm_w, q_norm_w=m_q_norm_w, k_norm_w=m_k_norm_w, sinks=m_sinks, a_re=m_a_re, a_im=m_a_im,
                   log_step=m_log_step, b_re=m_b_re, b_im=m_b_im, c_re=m_c_re, c_im=m_c_im, d_skip=m_d_skip,
                   b_glu=m_b_glu, attn_out_norm_w=m_attn_out_norm_w, ssm_out_norm_w=m_ssm_out_norm_w)
    small_v = dict(norm_w=v_norm_w, q_norm_w=v_q_norm_w, k_norm_w=v_k_norm_w, sinks=v_sinks, a_re=v_a_re, a_im=v_a_im,
                   log_step=v_log_step, b_re=v_b_re, b_im=v_b_im, c_re=v_c_re, c_im=v_c_im, d_skip=v_d_skip,
                   b_glu=v_b_glu, attn_out_norm_w=v_attn_out_norm_w, ssm_out_norm_w=v_ssm_out_norm_w)
    c_idx = lax.axis_index("c").astype(jnp.int32).reshape(1)
    chip_idx = (2 * lax.axis_index("x") + lax.axis_index("y")).astype(jnp.int32).reshape(1)
    dev_idx = 2 * chip_idx + c_idx

    xs = x[0]
    tgt = loss_target[0]
    t = xs.shape[0]
    posf = positions[0].astype(F32).reshape(t, 1)

    mx, my = lax.axis_index("x"), lax.axis_index("y")
    slab_order = jnp.stack([2 * mx + my, 2 * (1 - mx) + my, 2 * mx + (1 - my), 2 * (1 - mx) + (1 - my)]).astype(jnp.int32)
    proj, hn, w_in_all = _inproj(xs, norm_w, w_in.astype(BF16), slab_order)
    inv_freq = ROPE_THETA ** (-jnp.arange(0, HEAD_DIM, 2, dtype=F32) / HEAD_DIM)
    invf = jnp.tile(inv_freq, 4).reshape(1, 128)
    qw = jnp.tile(q_norm_w, 2).reshape(1, 128)
    kw = jnp.tile(k_norm_w, 2).reshape(1, 128)
    sink_row = sinks.reshape(1, N_HEADS)
    oa, w_glu_all, w_out_all = _attn_fwd(proj, posf, invf, qw, kw, sink_row, [w_glu.astype(BF16), w_out.astype(BF16)])
    w_glu_b = w_glu_all.reshape(SSM_W, SSM_W)
    w_out_b = w_out_all.reshape(D_MODEL, D_MODEL)

    lam_r, lam_i, pw_r, pw_i, bb_r, bb_i = _ssm_prep(a_re, a_im, log_step, b_re, b_im, t // N_SEG)
    rows8 = lambda a: jnp.broadcast_to(a.reshape(SSM_GB, 1, SSM_ST), (SSM_GB, N_SEG, SSM_ST))
    lam_r8, lam_i8, pw_r8, pw_i8 = rows8(lam_r), rows8(lam_i), rows8(pw_r), rows8(pw_i)
    ssm_w_in = jnp.concatenate([_block_diag_in(bb_r), _block_diag_in(bb_i)], axis=1).astype(BF16)
    ssm_w_out = jnp.concatenate([_block_diag_out(c_re), _block_diag_out(-c_im)], axis=2).astype(BF16)
    d_row = d_skip.reshape(1, SSM_W)
    uz = _permute_rows(proj[:, 2560:])
    up, zsp = uz[:, :SSM_W], uz[:, SSM_W:]
    yp, hc = _ssm_fwd(up, lam_r8, lam_i8, pw_r8, pw_i8, ssm_w_in, ssm_w_out, d_row)
    b_glu_row = b_glu.reshape(1, SSM_W)
    osp, ygp = _glu_fwd(yp, zsp, w_glu_b, b_glu_row)
    os_ = _unpermute_rows(osp)
    aw = attn_out_norm_w.reshape(1, ATTN_W)
    sw = ssm_out_norm_w.reshape(1, SSM_W)
    merged, dout, sq_err = _outproj(oa, os_, aw, sw, w_out_b, xs, tgt)
    loss = lax.psum(0.5 * sq_err[0, 0] / D_MODEL, MESH_AXES)

    doa, dos, g_aw, g_sw = _outproj_bwd(dout, oa, os_, aw, sw, w_out_b)
    dout_b = dout.astype(BF16)
    g_w_out, g_w_out_b = _matmul_tn(merged, dout_b, tm=512, tn=1024, tk=1024, name="grad_w_out")
    dyp, dzsp, dap, g_b_glu = _glu_bwd(yp, zsp, _permute_rows(dos), w_glu_b, b_glu_row)
    g_w_glu, g_w_glu_b = _matmul_tn(ygp, dap, tm=512, tn=1024, tk=1024, name="grad_w_glu")
    dup, g_wi, g_wo, g_lam, g_d = _ssm_bwd(up, dyp, hc, lam_r8, lam_i8, pw_r8, pw_i8, ssm_w_in, ssm_w_out, d_row)
    early = [g_w_glu_b.reshape(8, 128, SSM_W), g_w_out_b.reshape(8, 256, D_MODEL)]
    dq, dk, dv, dza, g_qw, g_kw, g_sink, *early_slots = _attn_bwd(proj, posf, invf, qw, kw, sink_row, doa, early)
    duz = _unpermute_rows(jnp.concatenate([dup, dzsp], axis=1))
    dproj = jnp.concatenate([dq, dk, dv, dza, duz], axis=1)
    g_w_in, g_w_in_b = _matmul_tn(hn, dproj, tm=512, tn=SHARD_W, tk=1024, name="grad_w_in", slabs=True)
    in_shape = (4, 2, D_MODEL // 2, SHARD_W)
    (from_sib,) = _pair_swap([g_w_in_b.reshape(in_shape)])
    pair_in = _add_halves(g_w_in.reshape(in_shape), from_sib, c_idx, rows=128, name="pair_sum")
    grad_x, g_nw, in_slots = _inproj_bwd(dproj, w_in_all, xs, norm_w, dout, [pair_in])

    g_bb_r = _diag_blocks(g_wi[:, :, :SSM_ST]).transpose(0, 2, 1).reshape(SSM_G, SSM_P * SSM_H)
    g_bb_i = _diag_blocks(g_wi[:, :, SSM_ST:]).transpose(0, 2, 1).reshape(SSM_G, SSM_P * SSM_H)
    g_a_re, g_a_im, g_ls, g_b_re, g_b_im = _ssm_param_grads(
        a_re, a_im, log_step, b_re, b_im, g_lam[:, 0, :SSM_ST].reshape(SSM_G, SSM_P),
        g_lam[:, 0, SSM_ST:].reshape(SSM_G, SSM_P), g_bb_r, g_bb_i)
    small_g = dict(
        norm_w=g_nw, q_norm_w=g_qw[0, :64] + g_qw[0, 64:], k_norm_w=g_kw[0, :64] + g_kw[0, 64:],
        sinks=g_sink[0, :N_HEADS], a_re=g_a_re, a_im=g_a_im, log_step=g_ls, b_re=g_b_re, b_im=g_b_im,
        c_re=_diag_blocks(g_wo[:, :, :SSM_ST]), c_im=-_diag_blocks(g_wo[:, :, SSM_ST:]), d_skip=g_d,
        b_glu=g_b_glu, attn_out_norm_w=g_aw, ssm_out_norm_w=g_sw)

    mine = [_sum_peers(in_slots, pair_in, chip_idx, rows=128, name="sum_w_in"),
            _sum_peers(early_slots[0], early[0], dev_idx, rows=128, name="sum_w_glu"),
            _sum_peers(early_slots[1], early[1], dev_idx, rows=128, name="sum_w_out")]
    theirs = _half_swap(mine)
    packed = _pack_small(small_g, SMALL, 8 * PACK_ROWS).reshape(8, PACK_ROWS, 128)
    summed = _sum_slots(_exchange_slices(packed, True, "small_scatter"), rows=PACK_ROWS, name="small_sum")
    small_red = _exchange_slices(summed, False, "small_gather").reshape(8 * PACK_ROWS, 128)

    big = [_adamw_halves(w_in, mine[0], theirs[0], m_w_in, v_w_in, c_idx, rows=256, name="adamw_w_in"),
           _adamw_halves(w_glu, mine[1], theirs[1], m_w_glu, v_w_glu, c_idx, rows=128, name="adamw_w_glu"),
           _adamw_halves(w_out, mine[2], theirs[2], m_w_out, v_w_out, c_idx, rows=256, name="adamw_w_out")]
    g_in_sh, g_glu_sh, g_out_sh = (b[0] for b in big)
    upd = [b[1:] for b in big]
    grads = _unpack_small(small_red, SMALL)
    flat_first = sum(math.prod(SMALL_SHAPES[n]) for n in SMALL_3D) // 128
    sd, sm, sv = _adamw(_pack_small(small_w, SMALL_FLAT, FLAT_ROWS), small_red[flat_first:flat_first + FLAT_ROWS],
                        _pack_small(small_m, SMALL_FLAT, FLAT_ROWS), _pack_small(small_v, SMALL_FLAT, FLAT_ROWS),
                        rows=FLAT_ROWS, name="adamw_small")
    deltas, new_m, new_v = (_unpack_small(a, SMALL_FLAT) for a in (sd, sm, sv))
    for n in SMALL_3D:
        deltas[n], new_m[n], new_v[n] = _adamw_3d(small_w[n], grads[n], small_m[n], small_v[n], name="adamw_" + n)
    grads.update(w_in=g_in_sh, w_glu=g_glu_sh, w_out=g_out_sh)
    for n, (d, m_, v_) in zip(("w_in", "w_glu", "w_out"), upd):
        deltas[n], new_m[n], new_v[n] = d, m_, v_
    order = ["norm_w", "w_in", "q_norm_w", "k_norm_w", "sinks", "a_re", "a_im", "log_step", "b_re", "b_im", "c_re",
             "c_im", "d_skip", "w_glu", "b_glu", "attn_out_norm_w", "ssm_out_norm_w", "w_out"]
    return (loss, grad_x[None], *[grads[n] for n in order], *[deltas[n] for n in order],
            *[new_m[n] for n in order], *[new_v[n] for n in order])
```

```python
import math

import jax
import jax.numpy as jnp
from jax import lax
from jax.experimental import pallas as pl
from jax.experimental.pallas import tpu as pltpu

F32 = jnp.float32
BF16 = jnp.bfloat16

D_MODEL = 2048
ATTN_W = 1024
SSM_W = 1024
HEAD_DIM = 64
N_HEADS = 16
N_KV_HEADS = 4
KV_W = 256
BLOCK = 128
IN_W = 4608
SHARD_W = IN_W // 4
ROPE_THETA = 10000.0
SSM_H = 16
SSM_G = 64
SSM_P = 64
NORM_EPS = 1e-6
ADAM_LR = 0.001
ADAM_B1 = 0.9
ADAM_B2 = 0.999
ADAM_EPS = 1e-08
ADAM_WD = 0.01
ADAM_STEP = 10

N_SEG = 8
SSM_GB = 4
SSM_CH = 256
SSM_ST = 1024
SCAN_ROWS = 256
SCAN_LW = 512
VMEM_LIMIT = 56 * 1024 * 1024
MESH_AXES = ("x", "y", "c")
ANY = pl.BlockSpec(memory_space=pl.ANY)

SMALL_3D = ("b_re", "b_im", "c_re", "c_im")
SMALL_FLAT = ("norm_w", "q_norm_w", "k_norm_w", "sinks", "a_re", "a_im", "log_step", "d_skip", "b_glu",
              "attn_out_norm_w", "ssm_out_norm_w")
SMALL = SMALL_3D + SMALL_FLAT
SMALL_SHAPES = {"norm_w": (2048,), "q_norm_w": (64,), "k_norm_w": (64,), "sinks": (16,), "a_re": (64, 64),
                "a_im": (64, 64), "log_step": (64,), "b_re": (64, 64, 16), "b_im": (64, 64, 16),
                "c_re": (64, 16, 64), "c_im": (64, 16, 64), "d_skip": (1024,), "b_glu": (1024,),
                "attn_out_norm_w": (1024,), "ssm_out_norm_w": (1024,)}
PACK_ROWS = 272
FLAT_ROWS = 120


def _params(sem=None):
    return pltpu.CompilerParams(dimension_semantics=sem, vmem_limit_bytes=VMEM_LIMIT)


def _dot(a, b):
    return jnp.dot(a, b, preferred_element_type=F32)


def _dot_nt(a, b):
    return lax.dot_general(a, b, (((1,), (1,)), ((), ())), preferred_element_type=F32)


def _dot_tn(a, b):
    return lax.dot_general(a, b, (((0,), (0,)), ((), ())), preferred_element_type=F32)


def _sigmoid(x):
    return 1.0 / (1.0 + jnp.exp(-x))


def _silu(x):
    return x * _sigmoid(x)


def _dsilu(x):
    s = _sigmoid(x)
    return s * (1.0 + x * (1.0 - s))


_GELU_C = math.sqrt(2.0 / math.pi)


def _gelu(x):
    return 0.5 * x * (1.0 + jnp.tanh(_GELU_C * (x + 0.044715 * x * x * x)))


def _dgelu(x):
    t = jnp.tanh(_GELU_C * (x + 0.044715 * x * x * x))
    return 0.5 * (1.0 + t) + 0.5 * x * (1.0 - t * t) * _GELU_C * (1.0 + 3.0 * 0.044715 * x * x)


def _matmul_tn(a, b, *, tm, tn, name, slabs=False, dtypes=(F32, BF16)):
    k, m = a.shape
    _, n = b.shape

    def body(a_ref, b_ref, *o_refs):
        acc = _dot_tn(a_ref[...], b_ref[...])
        for o_ref in o_refs:
            o_ref[...] = acc.astype(o_ref.dtype)

    if slabs:
        out_spec = pl.BlockSpec((None, tm, tn), lambda j, i: (j, i, 0))
        shape = (n // tn, m, tn)
    else:
        out_spec = pl.BlockSpec((tm, tn), lambda j, i: (i, j))
        shape = (m, n)
    return pl.pallas_call(
        body, name=name, grid=(n // tn, m // tm),
        in_specs=[pl.BlockSpec((k, tm), lambda j, i: (0, i)), pl.BlockSpec((k, tn), lambda j, i: (0, j))],
        out_specs=[out_spec] * len(dtypes),
        out_shape=[jax.ShapeDtypeStruct(shape, d) for d in dtypes],
        compiler_params=_params(("parallel", "parallel")),
    )(a, b)


def _inproj(x, norm_w, w_sh, order):
    t = x.shape[0]
    tm = 512
    ni = t // tm
    hr = D_MODEL // 2

    def body(ord_ref, x_ref, nw_ref, sh_ref, proj_ref, hn_ref, full_ref, wbuf, hn_s, ssem, rsem, lsem):
        s, i = pl.program_id(0), pl.program_id(1)
        mx, my, c = _mesh_pos()
        me = 2 * mx + my
        sib = (mx, my, 1 - c)
        chips = _other_chips(mx, my)

        def half(which):
            return pl.ds(pl.multiple_of(which * hr, 8), hr)

        def slot(k):
            return 2 * chips[k][0] + chips[k][1]

        def ici(k):
            return _remote(sh_ref.at[half(c)], full_ref.at[me, half(c)], ssem.at[k], rsem.at[k], (*chips[k], c))

        def own():
            return _remote(sh_ref, full_ref.at[me], ssem.at[6], rsem.at[6], sib)

        def landed(k, which, sem):
            ref = full_ref.at[slot(k), half(which)]
            return _remote(ref, ref, ssem.at[sem], rsem.at[sem], sib)

        def fetch(src, b):
            return pltpu.make_async_copy(src, wbuf.at[b], lsem.at[b])

        @pl.when((s == 0) & (i == 0))
        def _():
            for k in range(3):
                ici(k).start()
            own().start()
            cp = fetch(sh_ref, 0)
            cp.start()
            cp.wait()

        for k in range(3):
            @pl.when((s == k) & (i == max(ni - 2, 0)))
            def _(k=k):
                landed(k, c, k).wait_recv()
                landed(k, c, 3 + k).start()
                landed(k, 1 - c, 3 + k).wait_recv()
                fetch(full_ref.at[slot(k)], (k + 1) % 2).start()

            @pl.when((s == k + 1) & (i == 0))
            def _(k=k):
                fetch(full_ref.at[slot(k)], (k + 1) % 2).wait()

        xv = x_ref[...]
        r = lax.rsqrt(jnp.mean(xv * xv, axis=1, keepdims=True) + NORM_EPS)
        hn = (xv * r * nw_ref[...]).astype(BF16)
        proj_ref[...] = _dot(hn, wbuf[s % 2])

        def hn_out(tile):
            return pltpu.make_async_copy(hn_s, hn_ref.at[pl.ds(pl.multiple_of(tile * tm, tm), tm), :], lsem.at[2])

        @pl.when(((s == 0) & (i > 0)) | ((s == 1) & (i == 0)))
        def _():
            hn_out(jnp.where(s == 0, i - 1, ni - 1)).wait()

        @pl.when(s == 0)
        def _():
            hn_s[...] = hn
            hn_out(i).start()

        @pl.when((s == 3) & (i == ni - 1))
        def _():
            mine = full_ref.at[me]
            _remote(mine, mine, ssem.at[6], rsem.at[6], sib).wait_recv()
            for k in range(3):
                ici(k).wait_send()
                landed(k, c, 3 + k).wait_send()
            own().wait_send()

    return pl.pallas_call(
        body, name="inproj",
        grid_spec=pltpu.PrefetchScalarGridSpec(
            num_scalar_prefetch=1, grid=(4, ni),
            in_specs=[pl.BlockSpec((tm, D_MODEL), lambda s, i, o: (i, 0)),
                      pl.BlockSpec((1, D_MODEL), lambda s, i, o: (0, 0)), ANY],
            out_specs=[pl.BlockSpec((tm, SHARD_W), lambda s, i, o: (i, o[s])), ANY, ANY],
            scratch_shapes=[pltpu.VMEM((2, D_MODEL, SHARD_W), BF16), pltpu.VMEM((tm, D_MODEL), BF16),
                            pltpu.SemaphoreType.DMA((7,)), pltpu.SemaphoreType.DMA((7,)),
                            pltpu.SemaphoreType.DMA((3,))]),
        out_shape=[jax.ShapeDtypeStruct((t, IN_W), F32), jax.ShapeDtypeStruct((t, D_MODEL), BF16),
                   jax.ShapeDtypeStruct((4, D_MODEL, SHARD_W), BF16)],
        compiler_params=_params(("arbitrary", "arbitrary")),
    )(order, x, norm_w.reshape(1, D_MODEL), w_sh)


def _lane128():
    return lax.broadcasted_iota(jnp.int32, (1, 128), 1)


def _head_sums(v):
    lo = _lane128() < 64
    s_lo = jnp.sum(jnp.where(lo, v, 0.0), axis=1, keepdims=True)
    s_hi = jnp.sum(jnp.where(lo, 0.0, v), axis=1, keepdims=True)
    return jnp.where(lo, s_lo, s_hi)


def _rot_half(t):
    first = (_lane128() % 64) < 32
    return jnp.where(first, -pltpu.roll(t, 96, 1), pltpu.roll(t, 32, 1))


def _head_rstd(t):
    return lax.rsqrt(_head_sums(t * t) * (1.0 / HEAD_DIM) + NORM_EPS)


def _prep_tile(t, w, cos, sin, r=None):
    r = _head_rstd(t) if r is None else r
    tn = t * r * w
    return tn * cos + _rot_half(tn) * sin


def _prep_tile_bwd(t, w, cos, sin, g, r=None):
    r = _head_rstd(t) if r is None else r
    d_tn = g * cos - _rot_half(g * sin)
    th = t * r
    dw = jnp.sum(d_tn * th, axis=0, keepdims=True)
    gh = d_tn * w
    m = _head_sums(gh * th) * (1.0 / HEAD_DIM)
    return r * (gh - th * m), dw


def _band_mask(n):
    qi = lax.broadcasted_iota(jnp.int32, (BLOCK, 2 * BLOCK), 0) + BLOCK
    ki = lax.broadcasted_iota(jnp.int32, (BLOCK, 2 * BLOCK), 1)
    rel = qi - ki
    return (rel >= 0) & (rel < BLOCK) & ((n > 0) | (ki >= BLOCK))


def _half_select(tile, half):
    lo = _lane128() < 64
    return jnp.where(lo if half == 0 else jnp.logical_not(lo), tile, 0.0)


def _stack_group(tiles, kv_half):
    rows = []
    for t in tiles:
        for half in range(2):
            piece = _half_select(t, half)
            rows.append(piece if half == kv_half else pltpu.roll(piece, 64, 1))
    return jnp.concatenate(rows, axis=0)


def _unstack_group(stacked, kv_half):
    tiles = []
    for i in range(2):
        acc = None
        for half in range(2):
            piece = _half_select(stacked[BLOCK * (2 * i + half):BLOCK * (2 * i + half + 1)], kv_half)
            piece = piece if half == kv_half else pltpu.roll(piece, 64, 1)
            acc = piece if acc is None else acc + piece
        tiles.append(acc)
    return tiles


def _attn_specs(nb):
    last = nb - 1
    qi = lambda n: (jnp.minimum(n, last), 0)
    prev = lambda n: jnp.maximum(n - 1, 0)
    cur = lambda n: jnp.minimum(n, last)
    specs = [
        pl.BlockSpec((BLOCK, ATTN_W), qi),
        pl.BlockSpec((BLOCK, KV_W), lambda n: (cur(n), 4)),
        pl.BlockSpec((BLOCK, KV_W), lambda n: (prev(n), 4)),
        pl.BlockSpec((BLOCK, KV_W), lambda n: (cur(n), 5)),
        pl.BlockSpec((BLOCK, KV_W), lambda n: (prev(n), 5)),
        pl.BlockSpec((BLOCK, 512), lambda n: (cur(n), 3)),
        pl.BlockSpec((BLOCK, 512), lambda n: (cur(n), 4)),
        pl.BlockSpec((BLOCK, 1), lambda n: (cur(n), 0)),
        pl.BlockSpec((BLOCK, 1), lambda n: (prev(n), 0)),
        pl.BlockSpec((1, 128), lambda n: (0, 0)),
        pl.BlockSpec((1, 128), lambda n: (0, 0)),
        pl.BlockSpec((1, 128), lambda n: (0, 0)),
        pl.BlockSpec((1, N_HEADS), lambda n: (0, 0)),
    ]
    return specs


def _attn_common(n, q_ref, kc_ref, kp_ref, vc_ref, vp_ref, pq_ref, pp_ref, invf_ref, qw_ref, kw_ref):
    invf = invf_ref[...]
    ang_q = pq_ref[...] * invf
    ang_p = pp_ref[...] * invf
    cos_q, sin_q = jnp.cos(ang_q), jnp.sin(ang_q)
    cos_k = jnp.concatenate([jnp.cos(ang_p), cos_q], axis=0)
    sin_k = jnp.concatenate([jnp.sin(ang_p), sin_q], axis=0)
    k_raw = jnp.concatenate([kp_ref[...], kc_ref[...]], axis=0)
    vv = jnp.concatenate([vp_ref[...], vc_ref[...]], axis=0).astype(BF16)
    kk = [_prep_tile(k_raw[:, 128 * i:128 * i + 128], kw_ref[...], cos_k, sin_k).astype(BF16) for i in range(2)]
    vt = [vv[:, 128 * i:128 * i + 128] for i in range(2)]
    qv = q_ref[...]
    qr = [_head_rstd(qv[:, 128 * i:128 * i + 128]) for i in range(8)]
    qt = [_prep_tile(qv[:, 128 * i:128 * i + 128], qw_ref[...], cos_q, sin_q, qr[i]) for i in range(8)]
    return cos_q, sin_q, qr, kk, vt, qt


QK_SCALE = 1.0 / math.sqrt(HEAD_DIM)


def _group_sinks(sink_ref, g):
    return jnp.concatenate([jnp.broadcast_to(sink_ref[:, 4 * g + j:4 * g + j + 1], (BLOCK, 1)) for j in range(4)], axis=0)


def _group_softmax(q4, kk_t, sink, bias):
    s = _dot_nt(q4, kk_t) + bias
    m = jnp.maximum(jnp.max(s, axis=1, keepdims=True), sink)
    p = jnp.exp(s - m)
    es = jnp.exp(sink - m)
    inv = 1.0 / (jnp.sum(p, axis=1, keepdims=True) + es)
    return p * inv, es * inv


def _group_bias(n):
    return jnp.concatenate([jnp.where(_band_mask(n), 0.0, -1e30)] * 4, axis=0)


def _attn_fwd(proj, posf, invf, qw, kw, sinks, later_shards):
    t = proj.shape[0]
    nb = t // BLOCK
    nt = len(later_shards)

    def body(q_ref, kc_ref, kp_ref, vc_ref, vp_ref, za0_ref, za1_ref, pq_ref, pp_ref, invf_ref, qw_ref, kw_ref,
             sink_ref, *rest):
        sh, o_ref, full = rest[:nt], rest[nt], rest[nt + 1:2 * nt + 1]
        ssem, rsem = rest[2 * nt + 1:]
        n = pl.program_id(0)
        start, wait = _bg_gather(sh, full, ssem, rsem)

        @pl.when(n == 0)
        def _():
            start()

        _, _, _, kk, vt, qt = _attn_common(n, q_ref, kc_ref, kp_ref, vc_ref, vp_ref, pq_ref, pp_ref, invf_ref,
                                           qw_ref, kw_ref)
        bias = _group_bias(n)
        tiles = []
        for g in range(N_KV_HEADS):
            q4 = (_stack_group(qt[2 * g:2 * g + 2], g % 2) * QK_SCALE).astype(BF16)
            p, _ = _group_softmax(q4, kk[g // 2], _group_sinks(sink_ref, g), bias)
            tiles += _unstack_group(_dot(p.astype(BF16), vt[g // 2]), g % 2)
        za = jnp.concatenate([za0_ref[...], za1_ref[...]], axis=1)
        o_ref[...] = jnp.concatenate(tiles, axis=1) * _silu(za)

        @pl.when(n == nb - 1)
        def _():
            wait()

    return pl.pallas_call(
        body, name="attn_fwd", grid=(nb,), in_specs=_attn_specs(nb) + [ANY] * nt,
        out_specs=[pl.BlockSpec((BLOCK, ATTN_W), lambda n: (n, 0))] + [ANY] * nt,
        out_shape=[jax.ShapeDtypeStruct((t, ATTN_W), F32)]
        + [jax.ShapeDtypeStruct((4,) + s.shape, s.dtype) for s in later_shards],
        scratch_shapes=[pltpu.SemaphoreType.DMA((4 * nt,)), pltpu.SemaphoreType.DMA((4 * nt,))],
        compiler_params=_params(("arbitrary",)),
    )(proj, proj, proj, proj, proj, proj, proj, posf, posf, invf, qw, kw, sinks, *later_shards)


def _attn_bwd(proj, posf, invf, qw, kw, sinks, doa, outgoing):
    t = proj.shape[0]
    nb = t // BLOCK
    last = nb - 1
    nt = len(outgoing)

    def body(q_ref, kc_ref, kp_ref, vc_ref, vp_ref, za0_ref, za1_ref, pq_ref, pp_ref, invf_ref, qw_ref, kw_ref,
             sink_ref, doa_ref, *rest):
        src = rest[:nt]
        dq_ref, dk_ref, dv_ref, dza_ref, gq_ref, gk_ref, gs_ref = rest[nt:nt + 7]
        dst = rest[nt + 7:2 * nt + 7]
        dkk_s, dvv_s, ck_s, cv_s, ssem, rsem = rest[2 * nt + 7:]
        n = pl.program_id(0)
        start, wait = _bg_scatter_devices(src, dst, ssem, rsem)

        @pl.when(n == 0)
        def _():
            start()
            gq_ref[...] = jnp.zeros_like(gq_ref)
            gk_ref[...] = jnp.zeros_like(gk_ref)
            gs_ref[...] = jnp.zeros_like(gs_ref)
            ck_s[...] = jnp.zeros_like(ck_s)
            cv_s[...] = jnp.zeros_like(cv_s)

        @pl.when(n == nb)
        def _():
            dkk_s[...] = jnp.zeros_like(dkk_s)
            dvv_s[...] = jnp.zeros_like(dvv_s)

        @pl.when(n < nb)
        def _():
            cos_q, sin_q, qr, kk, vt, qt = _attn_common(n, q_ref, kc_ref, kp_ref, vc_ref, vp_ref, pq_ref, pp_ref,
                                                        invf_ref, qw_ref, kw_ref)
            bias = _group_bias(n)
            za = jnp.concatenate([za0_ref[...], za1_ref[...]], axis=1)
            doa_v = doa_ref[...]
            do_full = doa_v * _silu(za)
            o_tiles, dq_tiles = [], []
            dkk = [jnp.zeros((2 * BLOCK, 128), F32) for _ in range(2)]
            dvv = [jnp.zeros((2 * BLOCK, 128), F32) for _ in range(2)]
            gsink = jnp.zeros((1, 128), F32)
            lane = _lane128()
            for g in range(N_KV_HEADS):
                q_b = (_stack_group(qt[2 * g:2 * g + 2], g % 2) * QK_SCALE).astype(BF16)
                do_b = _stack_group([do_full[:, 128 * i:128 * i + 128] for i in (2 * g, 2 * g + 1)], g % 2).astype(BF16)
                p, psink = _group_softmax(q_b, kk[g // 2], _group_sinks(sink_ref, g), bias)
                p_b = p.astype(BF16)
                dp = _dot_nt(do_b, vt[g // 2])
                delta = jnp.sum(p * dp, axis=1, keepdims=True)
                ds_b = (p * (dp - delta)).astype(BF16)
                sd = psink * delta
                for j in range(4):
                    gsink = gsink + jnp.where(lane == 4 * g + j, -jnp.sum(sd[BLOCK * j:BLOCK * (j + 1)]), 0.0)
                o_tiles += _unstack_group(_dot(p_b, vt[g // 2]), g % 2)
                dq_tiles += [d * QK_SCALE for d in _unstack_group(_dot(ds_b, kk[g // 2]), g % 2)]
                dkk[g // 2] = dkk[g // 2] + _dot_tn(ds_b, q_b)
                dvv[g // 2] = dvv[g // 2] + _dot_tn(p_b, do_b)
            dza_ref[...] = (doa_v * jnp.concatenate(o_tiles, axis=1) * _dsilu(za)).astype(BF16)
            qv = q_ref[...]
            gq = jnp.zeros((1, 128), F32)
            out = []
            for i in range(8):
                d, dw = _prep_tile_bwd(qv[:, 128 * i:128 * i + 128], qw_ref[...], cos_q, sin_q, dq_tiles[i], qr[i])
                out.append(d)
                gq = gq + dw
            dq_ref[...] = jnp.concatenate(out, axis=1).astype(BF16)
            gq_ref[...] += gq
            gs_ref[...] += gsink
            dkk_s[...] = jnp.concatenate(dkk, axis=1)
            dvv_s[...] = jnp.concatenate(dvv, axis=1)

        invf = invf_ref[...]
        ang_p = pp_ref[...] * invf
        cos_p, sin_p = jnp.cos(ang_p), jnp.sin(ang_p)
        dk_prev = ck_s[...] + dkk_s[0:BLOCK, :]
        kp = kp_ref[...]
        gk = jnp.zeros((1, 128), F32)
        out = []
        for i in range(2):
            d, dw = _prep_tile_bwd(kp[:, 128 * i:128 * i + 128], kw_ref[...], cos_p, sin_p,
                                   dk_prev[:, 128 * i:128 * i + 128])
            out.append(d)
            gk = gk + dw
        dk_ref[...] = jnp.concatenate(out, axis=1).astype(BF16)
        dv_ref[...] = (cv_s[...] + dvv_s[0:BLOCK, :]).astype(BF16)
        gk_ref[...] += gk
        ck_s[...] = dkk_s[BLOCK:2 * BLOCK, :]
        cv_s[...] = dvv_s[BLOCK:2 * BLOCK, :]

        @pl.when(n == nb)
        def _():
            wait()

    qblk = lambda n: (jnp.minimum(n, last), 0)
    kblk = lambda n: (jnp.maximum(n - 1, 0), 0)
    vec = pl.BlockSpec((1, 128), lambda n: (0, 0))
    return pl.pallas_call(
        body, name="attn_bwd", grid=(nb + 1,),
        in_specs=_attn_specs(nb) + [pl.BlockSpec((BLOCK, ATTN_W), qblk)] + [ANY] * nt,
        out_specs=[pl.BlockSpec((BLOCK, ATTN_W), qblk), pl.BlockSpec((BLOCK, KV_W), kblk),
                   pl.BlockSpec((BLOCK, KV_W), kblk), pl.BlockSpec((BLOCK, ATTN_W), qblk), vec, vec, vec] + [ANY] * nt,
        out_shape=[jax.ShapeDtypeStruct((t, ATTN_W), BF16), jax.ShapeDtypeStruct((t, KV_W), BF16),
                   jax.ShapeDtypeStruct((t, KV_W), BF16), jax.ShapeDtypeStruct((t, ATTN_W), BF16),
                   jax.ShapeDtypeStruct((1, 128), F32), jax.ShapeDtypeStruct((1, 128), F32),
                   jax.ShapeDtypeStruct((1, 128), F32)] + [jax.ShapeDtypeStruct(a.shape, a.dtype) for a in outgoing],
        scratch_shapes=[pltpu.VMEM((2 * BLOCK, KV_W), F32), pltpu.VMEM((2 * BLOCK, KV_W), F32),
                        pltpu.VMEM((BLOCK, KV_W), F32), pltpu.VMEM((BLOCK, KV_W), F32),
                        pltpu.SemaphoreType.DMA((7 * nt,)), pltpu.SemaphoreType.DMA((7 * nt,))],
        compiler_params=_params(("arbitrary",)),
    )(proj, proj, proj, proj, proj, proj, proj, posf, posf, invf, qw, kw, sinks, doa, *outgoing)


def _cmul(ar, ai, br, bi):
    return ar * br - ai * bi, ar * bi + ai * br


def _zoh(a_re, a_im, delta):
    e = jnp.exp(a_re * delta)
    lr, li = e * jnp.cos(a_im * delta), e * jnp.sin(a_im * delta)
    inv = 1.0 / (a_re * a_re + a_im * a_im)
    fr, fi = _cmul(lr - 1.0, li, a_re * inv, -a_im * inv)
    return lr, li, fr, fi


def _ssm_prep(a_re, a_im, log_step, b_re, b_im, seg_len):
    n_sq = int(round(math.log2(seg_len)))
    assert 2 ** n_sq == seg_len

    def body(ar_ref, ai_ref, ls_ref, arx_ref, aix_ref, br_ref, bi_ref, lr_ref, li_ref, pr_ref, pi_ref, bbr_ref, bbi_ref):
        delta = jnp.exp(ls_ref[...])
        lr, li, _, _ = _zoh(ar_ref[...], ai_ref[...], delta)
        lr_ref[...] = lr
        li_ref[...] = li
        pr, pi = lr, li
        for _ in range(n_sq):
            pr, pi = _cmul(pr, pi, pr, pi)
        pr_ref[...] = pr
        pi_ref[...] = pi
        _, _, fr, fi = _zoh(arx_ref[...], aix_ref[...], delta)
        bbr, bbi = _cmul(fr, fi, br_ref[...], bi_ref[...])
        bbr_ref[...] = bbr
        bbi_ref[...] = bbi

    gp = jax.ShapeDtypeStruct((SSM_G, SSM_P), F32)
    gx = jax.ShapeDtypeStruct((SSM_G, SSM_P * SSM_H), F32)
    return pl.pallas_call(body, name="ssm_prep", out_shape=[gp, gp, gp, gp, gx, gx])(
        a_re, a_im, log_step.reshape(SSM_G, 1), jnp.repeat(a_re, SSM_H, axis=1), jnp.repeat(a_im, SSM_H, axis=1),
        b_re.reshape(SSM_G, SSM_P * SSM_H), b_im.reshape(SSM_G, SSM_P * SSM_H))


def _ssm_param_grads(a_re, a_im, log_step, b_re, b_im, dlam_re, dlam_im, dbb_re, dbb_im):
    def body(ar_ref, ai_ref, ls_ref, arx_ref, aix_ref, br_ref, bi_ref, dlr_ref, dli_ref, dbr_ref, dbi_ref,
             gar_ref, gai_ref, gls_ref, gbr_ref, gbi_ref):
        delta = jnp.exp(ls_ref[...])
        ar, ai = ar_ref[...], ai_ref[...]
        lr, li, fr, fi = _zoh(ar, ai, delta)
        _, _, frx, fix = _zoh(arx_ref[...], aix_ref[...], delta)
        dbr, dbi = dbr_ref[...], dbi_ref[...]
        br, bi = br_ref[...], bi_ref[...]
        gbr, gbi = _cmul(frx, -fix, dbr, dbi)
        gbr_ref[...] = gbr
        gbi_ref[...] = gbi
        tr, ti = _cmul(br, -bi, dbr, dbi)
        row = lax.broadcasted_iota(jnp.int32, (SSM_P * SSM_H, SSM_P), 0)
        col = lax.broadcasted_iota(jnp.int32, (SSM_P * SSM_H, SSM_P), 1)
        fold = (row // SSM_H == col).astype(F32)
        dfr = jnp.dot(tr, fold, precision=lax.Precision.HIGHEST, preferred_element_type=F32)
        dfi = jnp.dot(ti, fold, precision=lax.Precision.HIGHEST, preferred_element_type=F32)
        inv = 1.0 / (ar * ar + ai * ai)
        ilr, ili = ar * inv, -ai * inv
        t1r, t1i = _cmul(dfr, dfi, ilr, -ili)
        dlbr, dlbi = dlr_ref[...] + t1r, dli_ref[...] + t1i
        qr, qi = _cmul(fr, fi, ilr, ili)
        t2r, t2i = _cmul(dfr, dfi, qr, -qi)
        glr, gli = -t2r, -t2i
        dzr, dzi = _cmul(dlbr, dlbi, lr, -li)
        gar_ref[...] = glr + dzr * delta
        gai_ref[...] = gli + dzi * delta
        gls_ref[...] = jnp.sum(dzr * ar + dzi * ai, axis=1, keepdims=True) * delta

    gp = jax.ShapeDtypeStruct((SSM_G, SSM_P), F32)
    gx = jax.ShapeDtypeStruct((SSM_G, SSM_P * SSM_H), F32)
    return pl.pallas_call(body, name="ssm_param_grads",
                          out_shape=[gp, gp, jax.ShapeDtypeStruct((SSM_G, 1), F32), gx, gx])(
        a_re, a_im, log_step.reshape(SSM_G, 1), jnp.repeat(a_re, SSM_H, axis=1), jnp.repeat(a_im, SSM_H, axis=1),
        b_re.reshape(SSM_G, SSM_P * SSM_H), b_im.reshape(SSM_G, SSM_P * SSM_H), dlam_re, dlam_im, dbb_re, dbb_im)


def _block_diag_in(bb):
    w = jnp.tile(bb.reshape(SSM_GB, SSM_ST, SSM_H), (1, 1, 16))
    row = lax.broadcasted_iota(jnp.int32, (1, SSM_ST, SSM_CH), 1) // SSM_P
    col = lax.broadcasted_iota(jnp.int32, (1, SSM_ST, SSM_CH), 2) // SSM_H
    return jnp.where(row == col, w, 0.0)


def _block_diag_out(c):
    w = jnp.tile(c.reshape(SSM_GB, SSM_CH, SSM_P), (1, 1, 16))
    row = lax.broadcasted_iota(jnp.int32, (1, SSM_CH, SSM_ST), 1) // SSM_H
    col = lax.broadcasted_iota(jnp.int32, (1, SSM_CH, SSM_ST), 2) // SSM_P
    return jnp.where(row == col, w, 0.0)


def _diag_blocks(full):
    w = full.reshape(SSM_GB, 16, SSM_H, 16, SSM_P)
    idx = jnp.arange(16)
    return w[:, idx, :, idx, :].transpose(1, 0, 2, 3).reshape(SSM_G, SSM_H, SSM_P)


def _permute_rows(a):
    t, c = a.shape
    return a.reshape(N_SEG, t // N_SEG, c).transpose(1, 0, 2).reshape(t, c)


def _unpermute_rows(a):
    t, c = a.shape
    return a.reshape(t // N_SEG, N_SEG, c).transpose(1, 0, 2).reshape(t, c)


def _scan_fwd(src_ref, dst_ref, lam_r_ref, lam_i_ref, init_ref, final_ref, steps):
    for k in range(SSM_ST // SCAN_LW):
        re = pl.ds(k * SCAN_LW, SCAN_LW)
        im = pl.ds(SSM_ST + k * SCAN_LW, SCAN_LW)
        lr, li = lam_r_ref[:, re], lam_i_ref[:, re]

        def step(i, carry, re=re, im=im, lr=lr, li=li):
            hr, hi = carry
            rows = pl.ds(pl.multiple_of(i * 8, 8), 8)
            nr = lr * hr - li * hi + src_ref[rows, re]
            ni = lr * hi + li * hr + src_ref[rows, im]
            if dst_ref is not None:
                dst_ref[rows, re] = nr
                dst_ref[rows, im] = ni
            return nr, ni

        hr, hi = lax.fori_loop(0, steps, step, (init_ref[:, re], init_ref[:, im]), unroll=4)
        final_ref[:, re] = hr
        final_ref[:, im] = hi


def _ssm_specs(t):
    col = lambda g: (0, g)
    gb3 = lambda g: (g, 0, 0)
    return dict(
        rows=pl.BlockSpec((t, SSM_CH), col),
        lam=pl.BlockSpec((None, N_SEG, SSM_ST), gb3),
        w_in=pl.BlockSpec((None, 2 * SSM_ST, SSM_CH), gb3),
        w_out=pl.BlockSpec((None, SSM_CH, 2 * SSM_ST), gb3),
        vec=pl.BlockSpec((1, SSM_CH), col),
    )


def _segment_states(x_ref, pw_r_ref, pw_i_ref, out_ref, reverse):
    re, im = pl.ds(0, SSM_ST), pl.ds(SSM_ST, SSM_ST)
    pr, pi = pw_r_ref[0:1, :], pw_i_ref[0:1, :]
    first = N_SEG - 1 if reverse else 0
    out_ref[first:first + 1, :] = jnp.zeros((1, 2 * SSM_ST), F32)
    order = range(N_SEG - 1, 0, -1) if reverse else range(N_SEG - 1)
    for s in order:
        d = s - 1 if reverse else s + 1
        hr, hi = out_ref[s:s + 1, re], out_ref[s:s + 1, im]
        if reverse:
            nr, ni = pr * hr + pi * hi, pr * hi - pi * hr
        else:
            nr, ni = pr * hr - pi * hi, pr * hi + pi * hr
        out_ref[d:d + 1, re] = nr + x_ref[s:s + 1, re]
        out_ref[d:d + 1, im] = ni + x_ref[s:s + 1, im]


def _ssm_fwd(up, lam_r, lam_i, pw_r, pw_i, w_in, w_out, d_skip):
    t = up.shape[0]
    nch = t // SCAN_ROWS
    steps = SCAN_ROWS // N_SEG
    sp = _ssm_specs(t)

    def body(u_ref, lr_ref, li_ref, pr_ref, pi_ref, wi_ref, wo_ref, d_ref, y_ref, hc_ref, bu_s, car_s, seg_s):
        def load_bu(j):
            rows = pl.ds(pl.multiple_of(j * SCAN_ROWS, SCAN_ROWS), SCAN_ROWS)
            bu_s[...] = _dot_nt(u_ref[rows, :].astype(BF16), wi_ref[...])

        car_s[...] = jnp.zeros_like(car_s)

        def chunk1(j, c):
            load_bu(j)
            _scan_fwd(bu_s, None, lr_ref, li_ref, car_s, car_s, steps)
            return c

        lax.fori_loop(0, nch, chunk1, 0)
        _segment_states(car_s, pr_ref, pi_ref, seg_s, reverse=False)
        car_s[...] = seg_s[...]

        def chunk2(j, c):
            load_bu(j)
            hc_ref[j] = car_s[...]
            _scan_fwd(bu_s, bu_s, lr_ref, li_ref, car_s, car_s, steps)
            rows = pl.ds(pl.multiple_of(j * SCAN_ROWS, SCAN_ROWS), SCAN_ROWS)
            y_ref[rows, :] = _dot_nt(bu_s[...].astype(BF16), wo_ref[...]) + d_ref[...] * u_ref[rows, :]
            return c

        lax.fori_loop(0, nch, chunk2, 0)

    return pl.pallas_call(
        body, name="ssm_fwd", grid=(SSM_GB,),
        in_specs=[sp["rows"], sp["lam"], sp["lam"], sp["lam"], sp["lam"], sp["w_in"], sp["w_out"], sp["vec"]],
        out_specs=[sp["rows"], pl.BlockSpec((None, nch, N_SEG, 2 * SSM_ST), lambda g: (g, 0, 0, 0))],
        out_shape=[jax.ShapeDtypeStruct((t, SSM_W), F32), jax.ShapeDtypeStruct((SSM_GB, nch, N_SEG, 2 * SSM_ST), F32)],
        scratch_shapes=[pltpu.VMEM((SCAN_ROWS, 2 * SSM_ST), F32), pltpu.VMEM((N_SEG, 2 * SSM_ST), F32),
                        pltpu.VMEM((N_SEG, 2 * SSM_ST), F32)],
        compiler_params=_params(("parallel",)),
    )(up, lam_r, lam_i, pw_r, pw_i, w_in, w_out, d_skip)


def _ssm_bwd(up, dyp, hc, lam_r, lam_i, pw_r, pw_i, w_in, w_out, d_skip):
    t = up.shape[0]
    nch = t // SCAN_ROWS
    steps = SCAN_ROWS // N_SEG
    sp = _ssm_specs(t)

    def body(u_ref, dy_ref, hc_ref, lr_ref, li_ref, pr_ref, pi_ref, wi_ref, wo_ref, d_ref,
             du_ref, gwi_ref, gwo_ref, glam_ref, gd_ref, bu_s, h_s, e_s, car_s, seg_s, acc_s):
        def chunk_rows(j):
            return pl.ds(pl.multiple_of(j * SCAN_ROWS, SCAN_ROWS), SCAN_ROWS)

        def load_e(j):
            e_s[...] = _dot(dy_ref[chunk_rows(j), :].astype(BF16), wo_ref[...])

        def scan_rev(j, accumulate):
            for k in range(SSM_ST // SCAN_LW):
                re = pl.ds(k * SCAN_LW, SCAN_LW)
                im = pl.ds(SSM_ST + k * SCAN_LW, SCAN_LW)
                lr, li = lr_ref[:, re], li_ref[:, re]

                def step(ii, carry, re=re, im=im, lr=lr, li=li):
                    i = steps - 1 - ii
                    rows = pl.ds(pl.multiple_of(i * 8, 8), 8)
                    if accumulate:
                        gr, gi, ar, ai = carry
                    else:
                        gr, gi = carry
                    nr = lr * gr + li * gi + e_s[rows, re]
                    ni = lr * gi - li * gr + e_s[rows, im]
                    if not accumulate:
                        return nr, ni
                    e_s[rows, re] = nr
                    e_s[rows, im] = ni
                    pr_, pi_ = h_s[rows, re], h_s[rows, im]
                    return nr, ni, ar + nr * pr_ + ni * pi_, ai + ni * pr_ - nr * pi_

                init = (car_s[:, re], car_s[:, im])
                if accumulate:
                    init = init + (acc_s[:, re], acc_s[:, im])
                out = lax.fori_loop(0, steps, step, init, unroll=4)
                car_s[:, re] = out[0]
                car_s[:, im] = out[1]
                if accumulate:
                    acc_s[:, re] = out[2]
                    acc_s[:, im] = out[3]

        car_s[...] = jnp.zeros_like(car_s)

        def pass1(jj, c):
            load_e(nch - 1 - jj)
            scan_rev(nch - 1 - jj, False)
            return c

        lax.fori_loop(0, nch, pass1, 0)
        _segment_states(car_s, pr_ref, pi_ref, seg_s, reverse=True)
        car_s[...] = seg_s[...]
        acc_s[...] = jnp.zeros_like(acc_s)
        gwi_ref[...] = jnp.zeros_like(gwi_ref)
        gwo_ref[...] = jnp.zeros_like(gwo_ref)
        gd_ref[...] = jnp.zeros_like(gd_ref)

        def pass2(jj, c):
            j = nch - 1 - jj
            rows = chunk_rows(j)
            u = u_ref[rows, :]
            dy = dy_ref[rows, :]
            u_b, dy_b = u.astype(BF16), dy.astype(BF16)
            bu_s[...] = _dot_nt(u_b, wi_ref[...])
            h_s[0:N_SEG, :] = hc_ref[j]
            seg_s[...] = hc_ref[j]
            _scan_fwd(bu_s, h_s.at[pl.ds(N_SEG, SCAN_ROWS), :], lr_ref, li_ref, seg_s, seg_s, steps)
            load_e(j)
            scan_rev(j, True)
            g_b = e_s[...].astype(BF16)
            du_ref[rows, :] = (_dot(g_b, wi_ref[...]) + d_ref[...] * dy).astype(du_ref.dtype)
            gwi_ref[...] += _dot_tn(u_b, g_b)
            gwo_ref[...] += _dot_tn(dy_b, h_s[pl.ds(N_SEG, SCAN_ROWS), :].astype(BF16))
            gd_ref[...] += jnp.sum(dy * u, axis=0, keepdims=True)
            return c

        lax.fori_loop(0, nch, pass2, 0)
        glam_ref[...] = jnp.sum(acc_s[...], axis=0, keepdims=True)

    mat = pl.BlockSpec((None, SSM_CH, 2 * SSM_ST), lambda g: (g, 0, 0))
    return pl.pallas_call(
        body, name="ssm_bwd", grid=(SSM_GB,),
        in_specs=[sp["rows"], sp["rows"], pl.BlockSpec((None, nch, N_SEG, 2 * SSM_ST), lambda g: (g, 0, 0, 0)),
                  sp["lam"], sp["lam"], sp["lam"], sp["lam"], sp["w_in"], sp["w_out"], sp["vec"]],
        out_specs=[sp["rows"], mat, mat, pl.BlockSpec((None, 1, 2 * SSM_ST), lambda g: (g, 0, 0)), sp["vec"]],
        out_shape=[jax.ShapeDtypeStruct((t, SSM_W), BF16), jax.ShapeDtypeStruct((SSM_GB, SSM_CH, 2 * SSM_ST), F32),
                   jax.ShapeDtypeStruct((SSM_GB, SSM_CH, 2 * SSM_ST), F32),
                   jax.ShapeDtypeStruct((SSM_GB, 1, 2 * SSM_ST), F32), jax.ShapeDtypeStruct((1, SSM_W), F32)],
        scratch_shapes=[pltpu.VMEM((SCAN_ROWS, 2 * SSM_ST), F32), pltpu.VMEM((SCAN_ROWS + N_SEG, 2 * SSM_ST), F32),
                        pltpu.VMEM((SCAN_ROWS, 2 * SSM_ST), F32), pltpu.VMEM((N_SEG, 2 * SSM_ST), F32),
                        pltpu.VMEM((N_SEG, 2 * SSM_ST), F32), pltpu.VMEM((N_SEG, 2 * SSM_ST), F32)],
        compiler_params=_params(("parallel",)),
    )(up, dyp, hc, lam_r, lam_i, pw_r, pw_i, w_in, w_out, d_skip)


def _glu_fwd(y, zs, w_glu, b_glu):
    t = y.shape[0]
    tm = 512

    def body(y_ref, z_ref, w_ref, b_ref, o_ref, yg_ref):
        yg = _gelu(y_ref[...])
        yg_b = yg.astype(BF16)
        a = _dot(yg_b, w_ref[...]) + b_ref[...]
        o_ref[...] = yg * _sigmoid(a) * _silu(z_ref[...])
        yg_ref[...] = yg_b

    row = pl.BlockSpec((tm, SSM_W), lambda i: (i, 0))
    return pl.pallas_call(
        body, name="glu_fwd", grid=(t // tm,),
        in_specs=[row, row, pl.BlockSpec((SSM_W, SSM_W), lambda i: (0, 0)), pl.BlockSpec((1, SSM_W), lambda i: (0, 0))],
        out_specs=[row, row],
        out_shape=[jax.ShapeDtypeStruct((t, SSM_W), F32), jax.ShapeDtypeStruct((t, SSM_W), BF16)],
        compiler_params=_params(("parallel",)),
    )(y, zs, w_glu, b_glu)


def _glu_bwd(y, zs, dos, w_glu, b_glu):
    t = y.shape[0]
    tm = 512

    def body(y_ref, z_ref, do_ref, w_ref, b_ref, dy_ref, dz_ref, da_ref, gb_ref):
        @pl.when(pl.program_id(0) == 0)
        def _():
            gb_ref[...] = jnp.zeros_like(gb_ref)

        yv, z, do = y_ref[...], z_ref[...], do_ref[...]
        yg = _gelu(yv)
        sg = _sigmoid(_dot(yg.astype(BF16), w_ref[...]) + b_ref[...])
        dy2 = do * _silu(z)
        dz_ref[...] = (do * yg * sg * _dsilu(z)).astype(BF16)
        da = dy2 * yg * sg * (1.0 - sg)
        da_b = da.astype(BF16)
        da_ref[...] = da_b
        gb_ref[...] += jnp.sum(da, axis=0, keepdims=True)
        dyg = dy2 * sg + _dot_nt(da_b, w_ref[...])
        dy_ref[...] = dyg * _dgelu(yv)

    row = pl.BlockSpec((tm, SSM_W), lambda i: (i, 0))
    vec = pl.BlockSpec((1, SSM_W), lambda i: (0, 0))
    return pl.pallas_call(
        body, name="glu_bwd", grid=(t // tm,),
        in_specs=[row, row, row, pl.BlockSpec((SSM_W, SSM_W), lambda i: (0, 0)), vec],
        out_specs=[row, row, row, vec],
        out_shape=[jax.ShapeDtypeStruct((t, SSM_W), F32), jax.ShapeDtypeStruct((t, SSM_W), BF16),
                   jax.ShapeDtypeStruct((t, SSM_W), BF16), jax.ShapeDtypeStruct((1, SSM_W), F32)],
        compiler_params=_params(("arbitrary",)),
    )(y, zs, dos, w_glu, b_glu)


def _rms(o):
    return lax.rsqrt(jnp.mean(o * o, axis=1, keepdims=True) + NORM_EPS)


def _outproj(oa, os_, aw, sw, w_out, x, target):
    t = x.shape[0]
    tm = 256

    def body(oa_ref, os_ref, aw_ref, sw_ref, w_ref, x_ref, t_ref, mg_ref, do_ref, ls_ref):
        @pl.when(pl.program_id(0) == 0)
        def _():
            ls_ref[...] = jnp.zeros_like(ls_ref)

        a, s = oa_ref[...], os_ref[...]
        merged = jnp.concatenate([a * _rms(a) * aw_ref[...], s * _rms(s) * sw_ref[...]], axis=1).astype(BF16)
        mg_ref[...] = merged
        err = x_ref[...] + _dot(merged, w_ref[...]) - t_ref[...]
        do_ref[...] = err * (1.0 / D_MODEL)
        ls_ref[...] += jnp.sum(err * err)

    half = pl.BlockSpec((tm, ATTN_W), lambda i: (i, 0))
    full = pl.BlockSpec((tm, D_MODEL), lambda i: (i, 0))
    vec = pl.BlockSpec((1, ATTN_W), lambda i: (0, 0))
    return pl.pallas_call(
        body, name="outproj", grid=(t // tm,),
        in_specs=[half, half, vec, vec, pl.BlockSpec((D_MODEL, D_MODEL), lambda i: (0, 0)), full, full],
        out_specs=[full, full, pl.BlockSpec((8, 128), lambda i: (0, 0))],
        out_shape=[jax.ShapeDtypeStruct((t, D_MODEL), BF16), jax.ShapeDtypeStruct((t, D_MODEL), F32),
                   jax.ShapeDtypeStruct((8, 128), F32)],
        compiler_params=_params(("arbitrary",)),
    )(oa, os_, aw, sw, w_out, x, target)


def _outproj_bwd(dout, oa, os_, aw, sw, w_out):
    t = dout.shape[0]
    tm = 256

    def norm_bwd(o, w, dm):
        r = _rms(o)
        yh = o * r
        gh = dm * w
        return r * (gh - yh * jnp.mean(gh * yh, axis=1, keepdims=True)), jnp.sum(dm * yh, axis=0, keepdims=True)

    def body(do_ref, oa_ref, os_ref, aw_ref, sw_ref, w_ref, da_ref, ds_ref, ga_ref, gs_ref):
        @pl.when(pl.program_id(0) == 0)
        def _():
            ga_ref[...] = jnp.zeros_like(ga_ref)
            gs_ref[...] = jnp.zeros_like(gs_ref)

        dm = _dot_nt(do_ref[...].astype(BF16), w_ref[...])
        da, ga = norm_bwd(oa_ref[...], aw_ref[...], dm[:, :ATTN_W])
        ds, gs = norm_bwd(os_ref[...], sw_ref[...], dm[:, ATTN_W:])
        da_ref[...] = da
        ds_ref[...] = ds
        ga_ref[...] += ga
        gs_ref[...] += gs

    half = pl.BlockSpec((tm, ATTN_W), lambda i: (i, 0))
    full = pl.BlockSpec((tm, D_MODEL), lambda i: (i, 0))
    vec = pl.BlockSpec((1, ATTN_W), lambda i: (0, 0))
    return pl.pallas_call(
        body, name="outproj_bwd", grid=(t // tm,),
        in_specs=[full, half, half, vec, vec, pl.BlockSpec((D_MODEL, D_MODEL), lambda i: (0, 0))],
        out_specs=[half, half, vec, vec],
        out_shape=[jax.ShapeDtypeStruct((t, ATTN_W), F32), jax.ShapeDtypeStruct((t, ATTN_W), F32),
                   jax.ShapeDtypeStruct((1, ATTN_W), F32), jax.ShapeDtypeStruct((1, ATTN_W), F32)],
        compiler_params=_params(("arbitrary",)),
    )(dout, oa, os_, aw, sw, w_out)


def _inproj_bwd(dproj, w_slabs, x, norm_w, dout, outgoing):
    t = x.shape[0]
    tm = 512
    tc = 256
    nc = D_MODEL // tc
    nt = len(outgoing)
    ni = t // tm

    def body(dp_ref, w_ref, x_ref, nw_ref, do_ref, *rest):
        src, (gx_ref, gw_ref), dst = rest[:nt], rest[nt:nt + 2], rest[nt + 2:2 * nt + 2]
        acc_ref, ssem, rsem = rest[2 * nt + 2:]
        i, j = pl.program_id(0), pl.program_id(1)
        start, wait = _bg_scatter_chips(src, dst, ssem, rsem)

        @pl.when((i == 0) & (j == 0))
        def _():
            start()
            gw_ref[...] = jnp.zeros_like(gw_ref)

        part = _dot_nt(dp_ref[:, 0:SHARD_W], w_ref[0])
        for sl in range(1, 4):
            part = part + _dot_nt(dp_ref[:, sl * SHARD_W:(sl + 1) * SHARD_W], w_ref[sl])
        acc_ref[:, pl.ds(pl.multiple_of(j * tc, tc), tc)] = part

        @pl.when(j == nc - 1)
        def _():
            xv = x_ref[...]
            r = lax.rsqrt(jnp.mean(xv * xv, axis=1, keepdims=True) + NORM_EPS)
            yh = xv * r
            dh = acc_ref[...]
            gh = dh * nw_ref[...]
            gx_ref[...] = do_ref[...] + r * (gh - yh * jnp.mean(gh * yh, axis=1, keepdims=True))
            gw_ref[...] += jnp.sum(dh * yh, axis=0, keepdims=True)

        @pl.when((i == ni - 1) & (j == nc - 1))
        def _():
            wait()

    full = pl.BlockSpec((tm, D_MODEL), lambda i, j: (i, 0))
    vec = pl.BlockSpec((1, D_MODEL), lambda i, j: (0, 0))
    return pl.pallas_call(
        body, name="inproj_bwd", grid=(ni, nc),
        in_specs=[pl.BlockSpec((tm, IN_W), lambda i, j: (i, 0)),
                  pl.BlockSpec((4, tc, SHARD_W), lambda i, j: (0, j, 0)), full, vec, full] + [ANY] * nt,
        out_specs=[full, vec] + [ANY] * nt,
        out_shape=[jax.ShapeDtypeStruct((t, D_MODEL), F32), jax.ShapeDtypeStruct((1, D_MODEL), F32)]
        + [jax.ShapeDtypeStruct(a.shape, a.dtype) for a in outgoing],
        scratch_shapes=[pltpu.VMEM((tm, D_MODEL), F32), pltpu.SemaphoreType.DMA((3 * nt,)),
                        pltpu.SemaphoreType.DMA((3 * nt,))],
        compiler_params=_params(("arbitrary", "arbitrary")),
    )(dproj, w_slabs, x, norm_w.reshape(1, D_MODEL), dout, *outgoing)


def _adamw_math(w_ref, g_ref, m_ref, v_ref, d_ref, nm_ref, nv_ref):
    gv = g_ref[...]
    nm = ADAM_B1 * m_ref[...] + (1.0 - ADAM_B1) * gv
    nv = ADAM_B2 * v_ref[...] + (1.0 - ADAM_B2) * (gv * gv)
    m_hat = nm / (1.0 - ADAM_B1 ** ADAM_STEP)
    v_hat = nv / (1.0 - ADAM_B2 ** ADAM_STEP)
    d_ref[...] = -ADAM_LR * (m_hat / (jnp.sqrt(v_hat) + ADAM_EPS) + ADAM_WD * w_ref[...])
    nm_ref[...] = nm
    nv_ref[...] = nv


def _adamw_halves(w, mine, theirs, m, v, c_idx, *, rows, name):
    hr, cols = mine.shape
    nblk = hr // rows

    def body(c_ref, w_ref, a_ref, b_ref, m_ref, v_ref, g_ref, d_ref, nm_ref, nv_ref):
        g_ref[...] = jnp.where(pl.program_id(0) == c_ref[0], a_ref[...], b_ref[...])
        _adamw_math(w_ref, g_ref, m_ref, v_ref, d_ref, nm_ref, nv_ref)

    full = pl.BlockSpec((rows, cols), lambda h, i, c: (h * nblk + i, 0))
    part = pl.BlockSpec((rows, cols), lambda h, i, c: (i, 0))
    shp = jax.ShapeDtypeStruct((2 * hr, cols), F32)
    return pl.pallas_call(
        body, name=name,
        grid_spec=pltpu.PrefetchScalarGridSpec(num_scalar_prefetch=1, grid=(2, nblk),
                                               in_specs=[full, part, part, full, full], out_specs=[full] * 4),
        out_shape=[shp] * 4, compiler_params=_params(("parallel", "parallel")),
    )(c_idx, w, mine, theirs, m, v)


def _adamw(w, g, m, v, *, rows, name):
    r, c = w.shape

    def body(w_ref, g_ref, m_ref, v_ref, d_ref, nm_ref, nv_ref):
        _adamw_math(w_ref, g_ref, m_ref, v_ref, d_ref, nm_ref, nv_ref)

    blk = pl.BlockSpec((rows, c), lambda i: (i, 0))
    shp = jax.ShapeDtypeStruct((r, c), F32)
    return pl.pallas_call(body, name=name, grid=(r // rows,), in_specs=[blk] * 4, out_specs=[blk] * 3,
                          out_shape=[shp] * 3, compiler_params=_params(("parallel",)))(w, g, m, v)


def _remote(src, dst, ssem, rsem, dev):
    return pltpu.make_async_remote_copy(src_ref=src, dst_ref=dst, send_sem=ssem, recv_sem=rsem, device_id=dev,
                                        device_id_type=pl.DeviceIdType.MESH)


def _mesh_pos():
    return lax.axis_index("x"), lax.axis_index("y"), lax.axis_index("c")


def _other_chips(x, y):
    return [(1 - x, y), (x, 1 - y), (1 - x, 1 - y)]


def _flips():
    return [(dx, dy, dc) for dx in (0, 1) for dy in (0, 1) for dc in (0, 1) if (dx, dy, dc) != (0, 0, 0)]


def _background(sends, arrivals):
    def start():
        for cp in sends():
            cp.start()

    def wait():
        for cp in arrivals():
            cp.wait_recv()
        for cp in sends():
            cp.wait_send()

    return start, wait


def _bg_gather(sh, full, ssem, rsem):
    x, y, c = _mesh_pos()
    me = 2 * x + y
    peers = [(px, py, c) for px, py in _other_chips(x, y)] + [(x, y, 1 - c)]
    slots = [2 * px + py for px, py in _other_chips(x, y)] + [me]
    pairs = [(i, k) for i in range(len(sh)) for k in range(4)]
    return _background(
        lambda: [_remote(sh[i], full[i].at[me], ssem.at[4 * i + k], rsem.at[4 * i + k], peers[k]) for i, k in pairs],
        lambda: [_remote(full[i].at[slots[k]], full[i].at[slots[k]], ssem.at[4 * i + k], rsem.at[4 * i + k], peers[k])
                 for i, k in pairs])


def _bg_scatter_devices(src, dst, ssem, rsem):
    x, y, c = _mesh_pos()
    me = 4 * x + 2 * y + c
    peers = []
    for dx, dy, dc in _flips():
        px, py, pc = jnp.bitwise_xor(x, dx), jnp.bitwise_xor(y, dy), jnp.bitwise_xor(c, dc)
        peers.append(((px, py, pc), 4 * px + 2 * py + pc))
    pairs = [(i, k) for i in range(len(src)) for k in range(7)]
    return _background(
        lambda: [_remote(src[i].at[peers[k][1]], dst[i].at[me], ssem.at[7 * i + k], rsem.at[7 * i + k], peers[k][0])
                 for i, k in pairs],
        lambda: [_remote(dst[i].at[peers[k][1]], dst[i].at[peers[k][1]], ssem.at[7 * i + k], rsem.at[7 * i + k],
                         peers[k][0]) for i, k in pairs])


def _bg_scatter_chips(src, dst, ssem, rsem):
    x, y, c = _mesh_pos()
    me = 2 * x + y
    chips = _other_chips(x, y)
    pairs = [(i, k) for i in range(len(src)) for k in range(3)]
    slot = lambda k: 2 * chips[k][0] + chips[k][1]
    return _background(
        lambda: [_remote(src[i].at[slot(k)], dst[i].at[me], ssem.at[3 * i + k], rsem.at[3 * i + k], (*chips[k], c))
                 for i, k in pairs],
        lambda: [_remote(dst[i].at[slot(k)], dst[i].at[slot(k)], ssem.at[3 * i + k], rsem.at[3 * i + k], (*chips[k], c))
                 for i, k in pairs])


def _pair_swap(arrays):
    nt = len(arrays)

    def body(*refs):
        src, dst = refs[:nt], refs[nt:2 * nt]
        ssem, rsem = refs[2 * nt:]
        x, y, c = _mesh_pos()
        cps = [_remote(src[i].at[:, 1 - c], dst[i], ssem.at[i], rsem.at[i], (x, y, 1 - c)) for i in range(nt)]
        for cp in cps:
            cp.start()
        for cp in cps:
            cp.wait_recv()
        for cp in cps:
            cp.wait_send()

    return pl.pallas_call(
        body, name="pair_swap", in_specs=[ANY] * nt, out_specs=[ANY] * nt,
        out_shape=[jax.ShapeDtypeStruct((4,) + a.shape[2:], a.dtype) for a in arrays],
        scratch_shapes=[pltpu.SemaphoreType.DMA((nt,)), pltpu.SemaphoreType.DMA((nt,))],
    )(*arrays)


def _half_swap(arrays):
    nt = len(arrays)

    def body(*refs):
        src, dst = refs[:nt], refs[nt:2 * nt]
        ssem, rsem = refs[2 * nt:]
        x, y, c = _mesh_pos()
        cps = [_remote(src[i], dst[i], ssem.at[i], rsem.at[i], (x, y, 1 - c)) for i in range(nt)]
        for cp in cps:
            cp.start()
        for cp in cps:
            cp.wait_recv()
        for cp in cps:
            cp.wait_send()

    return pl.pallas_call(
        body, name="half_swap", in_specs=[ANY] * nt, out_specs=[ANY] * nt,
        out_shape=[jax.ShapeDtypeStruct(a.shape, a.dtype) for a in arrays],
        scratch_shapes=[pltpu.SemaphoreType.DMA((nt,)), pltpu.SemaphoreType.DMA((nt,))],
    )(*arrays)


def _exchange_slices(src, scatter, name):
    def body(src_ref, dst_ref, ssem, rsem, lsem):
        x, y, c = _mesh_pos()
        me = 4 * x + 2 * y + c
        local = pltpu.make_async_copy(src_ref.at[me] if scatter else src_ref, dst_ref.at[me], lsem)
        local.start()
        cps = []
        for k, (dx, dy, dc) in enumerate(_flips()):
            px, py, pc = jnp.bitwise_xor(x, dx), jnp.bitwise_xor(y, dy), jnp.bitwise_xor(c, dc)
            peer = 4 * px + 2 * py + pc
            cp = _remote(src_ref.at[peer] if scatter else src_ref, dst_ref.at[me], ssem.at[k], rsem.at[k],
                         (px, py, pc))
            cp.start()
            cps.append((cp, peer))
        for k, (cp, peer) in enumerate(cps):
            slot = dst_ref.at[peer]
            _remote(slot, slot, ssem.at[k], rsem.at[k], (x, y, c)).wait_recv()
        for cp, _ in cps:
            cp.wait_send()
        local.wait()

    return pl.pallas_call(
        body, name=name, in_specs=[ANY], out_specs=ANY,
        out_shape=jax.ShapeDtypeStruct((8,) + src.shape[-2:], src.dtype),
        scratch_shapes=[pltpu.SemaphoreType.DMA((7,)), pltpu.SemaphoreType.DMA((7,)), pltpu.SemaphoreType.DMA],
    )(src)


def _add_halves(g, recv, c_idx, *, rows, name):
    _, _, hr, cols = g.shape

    def body(c_ref, g_ref, r_ref, o_ref):
        o_ref[...] = (g_ref[...] + r_ref[...].astype(F32)).astype(BF16)

    return pl.pallas_call(
        body, name=name,
        grid_spec=pltpu.PrefetchScalarGridSpec(
            num_scalar_prefetch=1, grid=(4, hr // rows),
            in_specs=[pl.BlockSpec((None, None, rows, cols), lambda j, i, c: (j, c[0], i, 0)),
                      pl.BlockSpec((None, rows, cols), lambda j, i, c: (j, i, 0))],
            out_specs=pl.BlockSpec((None, rows, cols), lambda j, i, c: (j, i, 0))),
        out_shape=jax.ShapeDtypeStruct((4, hr, cols), BF16),
        compiler_params=_params(("parallel", "parallel")),
    )(c_idx, g, recv)


def _sum_peers(slots, own, idx, *, rows, name):
    n, r, cols = slots.shape

    def body(me_ref, *refs):
        me = me_ref[0]
        mine = refs[n][...].astype(F32)
        acc = None
        for k in range(n):
            term = jnp.where(me == k, mine, refs[k][...].astype(F32))
            acc = term if acc is None else acc + term
        refs[n + 1][...] = acc

    def slot_spec(k):
        return pl.BlockSpec((None, rows, cols), lambda i, me: (jnp.where(me[0] == k, (k + 1) % n, k), i, 0))

    return pl.pallas_call(
        body, name=name,
        grid_spec=pltpu.PrefetchScalarGridSpec(
            num_scalar_prefetch=1, grid=(r // rows,),
            in_specs=[slot_spec(k) for k in range(n)] + [pl.BlockSpec((None, rows, cols), lambda i, me: (me[0], i, 0))],
            out_specs=pl.BlockSpec((rows, cols), lambda i, me: (i, 0))),
        out_shape=jax.ShapeDtypeStruct((r, cols), F32),
        compiler_params=_params(("parallel",)),
    )(idx, *([slots] * n), own)


def _sum_slots(slots, *, rows, name):
    n, r, cols = slots.shape

    def body(s_ref, o_ref):
        acc = s_ref[0].astype(F32)
        for k in range(1, n):
            acc = acc + s_ref[k].astype(F32)
        o_ref[...] = acc

    return pl.pallas_call(
        body, name=name, grid=(r // rows,),
        in_specs=[pl.BlockSpec((n, rows, cols), lambda i: (0, i, 0))],
        out_specs=pl.BlockSpec((rows, cols), lambda i: (i, 0)),
        out_shape=jax.ShapeDtypeStruct((r, cols), F32),
        compiler_params=_params(("parallel",)),
    )(slots)


def _pack_small(d, names, rows):
    flat = jnp.concatenate([d[n].astype(F32).reshape(-1) for n in names])
    return jnp.pad(flat, (0, rows * 128 - flat.shape[0])).reshape(rows, 128)


def _unpack_small(p, names):
    flat = p.reshape(-1)
    out, off = {}, 0
    for n in names:
        size = math.prod(SMALL_SHAPES[n])
        out[n] = flat[off:off + size].reshape(SMALL_SHAPES[n])
        off += size
    return out


def _adamw_3d(w, g, m, v, *, name):
    def body(w_ref, g_ref, m_ref, v_ref, d_ref, nm_ref, nv_ref):
        _adamw_math(w_ref, g_ref, m_ref, v_ref, d_ref, nm_ref, nv_ref)

    blk = pl.BlockSpec((8,) + w.shape[1:], lambda i: (i, 0, 0))
    shp = jax.ShapeDtypeStruct(w.shape, F32)
    return pl.pallas_call(body, name=name, grid=(w.shape[0] // 8,), in_specs=[blk] * 4, out_specs=[blk] * 3,
                          out_shape=[shp] * 3, compiler_params=_params(("parallel",)))(w, g, m, v)


def kernel(x, positions, norm_w, w_in, q_norm_w, k_norm_w, sinks, a_re, a_im, log_step, b_re, b_im, c_re, c_im, d_skip, w_glu, b_glu, attn_out_norm_w, ssm_out_norm_w, w_out, loss_target, m_norm_w, m_w_in, m_q_norm_w, m_k_norm_w, m_sinks, m_a_re, m_a_im, m_log_step, m_b_re, m_b_im, m_c_re, m_c_im, m_d_skip, m_w_glu, m_b_glu, m_attn_out_norm_w, m_ssm_out_norm_w, m_w_out, v_norm_w, v_w_in, v_q_norm_w, v_k_norm_w, v_sinks, v_a_re, v_a_im, v_log_step, v_b_re, v_b_im, v_c_re, v_c_im, v_d_skip, v_w_glu, v_b_glu, v_attn_out_norm_w, v_ssm_out_norm_w, v_w_out):
    small_w = dict(norm_w=norm_w, q_norm_w=q_norm_w, k_norm_w=k_norm_w, sinks=sinks, a_re=a_re, a_im=a_im,
                   log_step=log_step, b_re=b_re, b_im=b_im, c_re=c_re, c_im=c_im, d_skip=d_skip, b_glu=b_glu,
                   attn_out_norm_w=attn_out_norm_w, ssm_out_norm_w=ssm_out_norm_w)
    small_m = dict(norm_w=m_norm_w, q_norm_w=m_q_norm_w, k_norm_w=m_k_norm_w, sinks=m_sinks, a_re=m_a_re, a_im=m_a_im,
                   log_step=m_log_step, b_re=m_b_re, b_im=m_b_im, c_re=m_c_re, c_im=m_c_im, d_skip=m_d_skip,
                   b_glu=m_b_glu, attn_out_norm_w=m_attn_out_norm_w, ssm_out_norm_w=m_ssm_out_norm_w)
    small_v = dict(norm_w=v_norm_w, q_norm_w=v_q_norm_w, k_norm_w=v_k_norm_w, sinks=v_sinks, a_re=v_a_re, a_im=v_a_im,
                   log_step=v_log_step, b_re=v_b_re, b_im=v_b_im, c_re=v_c_re, c_im=v_c_im, d_skip=v_d_skip,
                   b_glu=v_b_glu, attn_out_norm_w=v_attn_out_norm_w, ssm_out_norm_w=v_ssm_out_norm_w)
    c_idx = lax.axis_index("c").astype(jnp.int32).reshape(1)
    chip_idx = (2 * lax.axis_index("x") + lax.axis_index("y")).astype(jnp.int32).reshape(1)
    dev_idx = 2 * chip_idx + c_idx

    xs = x[0]
    tgt = loss_target[0]
    t = xs.shape[0]
    posf = positions[0].astype(F32).reshape(t, 1)

    mx, my = lax.axis_index("x"), lax.axis_index("y")
    slab_order = jnp.stack([2 * mx + my, 2 * (1 - mx) + my, 2 * mx + (1 - my), 2 * (1 - mx) + (1 - my)]).astype(jnp.int32)
    proj, hn, w_in_all = _inproj(xs, norm_w, w_in.astype(BF16), slab_order)
    inv_freq = ROPE_THETA ** (-jnp.arange(0, HEAD_DIM, 2, dtype=F32) / HEAD_DIM)
    invf = jnp.tile(inv_freq, 4).reshape(1, 128)
    qw = jnp.tile(q_norm_w, 2).reshape(1, 128)
    kw = jnp.tile(k_norm_w, 2).reshape(1, 128)
    sink_row = sinks.reshape(1, N_HEADS)
    oa, w_glu_all, w_out_all = _attn_fwd(proj, posf, invf, qw, kw, sink_row, [w_glu.astype(BF16), w_out.astype(BF16)])
    w_glu_b = w_glu_all.reshape(SSM_W, SSM_W)
    w_out_b = w_out_all.reshape(D_MODEL, D_MODEL)

    lam_r, lam_i, pw_r, pw_i, bb_r, bb_i = _ssm_prep(a_re, a_im, log_step, b_re, b_im, t // N_SEG)
    rows8 = lambda a: jnp.broadcast_to(a.reshape(SSM_GB, 1, SSM_ST), (SSM_GB, N_SEG, SSM_ST))
    lam_r8, lam_i8, pw_r8, pw_i8 = rows8(lam_r), rows8(lam_i), rows8(pw_r), rows8(pw_i)
    ssm_w_in = jnp.concatenate([_block_diag_in(bb_r), _block_diag_in(bb_i)], axis=1).astype(BF16)
    ssm_w_out = jnp.concatenate([_block_diag_out(c_re), _block_diag_out(-c_im)], axis=2).astype(BF16)
    d_row = d_skip.reshape(1, SSM_W)
    uz = _permute_rows(proj[:, 2560:])
    up, zsp = uz[:, :SSM_W], uz[:, SSM_W:]
    yp, hc = _ssm_fwd(up, lam_r8, lam_i8, pw_r8, pw_i8, ssm_w_in, ssm_w_out, d_row)
    b_glu_row = b_glu.reshape(1, SSM_W)
    osp, ygp = _glu_fwd(yp, zsp, w_glu_b, b_glu_row)
    os_ = _unpermute_rows(osp)
    aw = attn_out_norm_w.reshape(1, ATTN_W)
    sw = ssm_out_norm_w.reshape(1, SSM_W)
    merged, dout, sq_err = _outproj(oa, os_, aw, sw, w_out_b, xs, tgt)
    loss = lax.psum(0.5 * sq_err[0, 0] / D_MODEL, MESH_AXES)

    doa, dos, g_aw, g_sw = _outproj_bwd(dout, oa, os_, aw, sw, w_out_b)
    dout_b = dout.astype(BF16)
    (g_w_out_b,) = _matmul_tn(merged, dout_b, tm=512, tn=1024, name="grad_w_out", dtypes=(BF16,))
    dyp, dzsp, dap, g_b_glu = _glu_bwd(yp, zsp, _permute_rows(dos), w_glu_b, b_glu_row)
    (g_w_glu_b,) = _matmul_tn(ygp, dap, tm=512, tn=1024, name="grad_w_glu", dtypes=(BF16,))
    dup, g_wi, g_wo, g_lam, g_d = _ssm_bwd(up, dyp, hc, lam_r8, lam_i8, pw_r8, pw_i8, ssm_w_in, ssm_w_out, d_row)
    early = [g_w_glu_b.reshape(8, 128, SSM_W), g_w_out_b.reshape(8, 256, D_MODEL)]
    dq, dk, dv, dza, g_qw, g_kw, g_sink, *early_slots = _attn_bwd(proj, posf, invf, qw, kw, sink_row, doa, early)
    duz = _unpermute_rows(jnp.concatenate([dup, dzsp], axis=1))
    dproj = jnp.concatenate([dq, dk, dv, dza, duz], axis=1)
    g_w_in, g_w_in_b = _matmul_tn(hn, dproj, tm=512, tn=SHARD_W, name="grad_w_in", slabs=True)
    in_shape = (4, 2, D_MODEL // 2, SHARD_W)
    (from_sib,) = _pair_swap([g_w_in_b.reshape(in_shape)])
    pair_in = _add_halves(g_w_in.reshape(in_shape), from_sib, c_idx, rows=128, name="pair_sum")
    grad_x, g_nw, in_slots = _inproj_bwd(dproj, w_in_all, xs, norm_w, dout, [pair_in])

    g_bb_r = _diag_blocks(g_wi[:, :, :SSM_ST]).transpose(0, 2, 1).reshape(SSM_G, SSM_P * SSM_H)
    g_bb_i = _diag_blocks(g_wi[:, :, SSM_ST:]).transpose(0, 2, 1).reshape(SSM_G, SSM_P * SSM_H)
    g_a_re, g_a_im, g_ls, g_b_re, g_b_im = _ssm_param_grads(
        a_re, a_im, log_step, b_re, b_im, g_lam[:, 0, :SSM_ST].reshape(SSM_G, SSM_P),
        g_lam[:, 0, SSM_ST:].reshape(SSM_G, SSM_P), g_bb_r, g_bb_i)
    small_g = dict(
        norm_w=g_nw, q_norm_w=g_qw[0, :64] + g_qw[0, 64:], k_norm_w=g_kw[0, :64] + g_kw[0, 64:],
        sinks=g_sink[0, :N_HEADS], a_re=g_a_re, a_im=g_a_im, log_step=g_ls, b_re=g_b_re, b_im=g_b_im,
        c_re=_diag_blocks(g_wo[:, :, :SSM_ST]), c_im=-_diag_blocks(g_wo[:, :, SSM_ST:]), d_skip=g_d,
        b_glu=g_b_glu, attn_out_norm_w=g_aw, ssm_out_norm_w=g_sw)

    mine = [_sum_peers(in_slots, pair_in, chip_idx, rows=128, name="sum_w_in"),
            _sum_peers(early_slots[0], early[0], dev_idx, rows=128, name="sum_w_glu"),
            _sum_peers(early_slots[1], early[1], dev_idx, rows=128, name="sum_w_out")]
    theirs = _half_swap(mine)
    packed = _pack_small(small_g, SMALL, 8 * PACK_ROWS).reshape(8, PACK_ROWS, 128)
    summed = _sum_slots(_exchange_slices(packed, True, "small_scatter"), rows=PACK_ROWS, name="small_sum")
    small_red = _exchange_slices(summed, False, "small_gather").reshape(8 * PACK_ROWS, 128)

    big = [_adamw_halves(w_in, mine[0], theirs[0], m_w_in, v_w_in, c_idx, rows=256, name="adamw_w_in"),
           _adamw_halves(w_glu, mine[1], theirs[1], m_w_glu, v_w_glu, c_idx, rows=128, name="adamw_w_glu"),
           _adamw_halves(w_out, mine[2], theirs[2], m_w_out, v_w_out, c_idx, rows=256, name="adamw_w_out")]
    g_in_sh, g_glu_sh, g_out_sh = (b[0] for b in big)
    upd = [b[1:] for b in big]
    grads = _unpack_small(small_red, SMALL)
    flat_first = sum(math.prod(SMALL_SHAPES[n]) for n in SMALL_3D) // 128
    sd, sm, sv = _adamw(_pack_small(small_w, SMALL_FLAT, FLAT_ROWS), small_red[flat_first:flat_first + FLAT_ROWS],
                        _pack_small(small_m, SMALL_FLAT, FLAT_ROWS), _pack_small(small_v, SMALL_FLAT, FLAT_ROWS),
                        rows=FLAT_ROWS, name="adamw_small")
    deltas, new_m, new_v = (_unpack_small(a, SMALL_FLAT) for a in (sd, sm, sv))
    for n in SMALL_3D:
        deltas[n], new_m[n], new_v[n] = _adamw_3d(small_w[n], grads[n], small_m[n], small_v[n], name="adamw_" + n)
    grads.update(w_in=g_in_sh, w_glu=g_glu_sh, w_out=g_out_sh)
    for n, (d, m_, v_) in zip(("w_in", "w_glu", "w_out"), upd):
        deltas[n], new_m[n], new_v[n] = d, m_, v_
    order = ["norm_w", "w_in", "q_norm_w", "k_norm_w", "sinks", "a_re", "a_im", "log_step", "b_re", "b_im", "c_re",
             "c_im", "d_skip", "w_glu", "b_glu", "attn_out_norm_w", "ssm_out_norm_w", "w_out"]
    return (loss, grad_x[None], *[grads[n] for n in order], *[deltas[n] for n in order],
            *[new_m[n] for n in order], *[new_v[n] for n in order])
```

```python
import math

import jax
import jax.numpy as jnp
from jax import lax
from jax.experimental import pallas as pl
from jax.experimental.pallas import tpu as pltpu

F32 = jnp.float32
BF16 = jnp.bfloat16

D_MODEL = 2048
ATTN_W = 1024
SSM_W = 1024
HEAD_DIM = 64
N_HEADS = 16
N_KV_HEADS = 4
KV_W = 256
BLOCK = 128
IN_W = 4608
SHARD_W = IN_W // 4
ROPE_THETA = 10000.0
SSM_H = 16
SSM_G = 64
SSM_P = 64
NORM_EPS = 1e-6
ADAM_LR = 0.001
ADAM_B1 = 0.9
ADAM_B2 = 0.999
ADAM_EPS = 1e-08
ADAM_WD = 0.01
ADAM_STEP = 10

N_SEG = 8
SSM_GB = 4
SSM_CH = 256
SSM_ST = 1024
SCAN_ROWS = 256
SCAN_LW = 512
VMEM_LIMIT = 56 * 1024 * 1024
MESH_AXES = ("x", "y", "c")
ANY = pl.BlockSpec(memory_space=pl.ANY)

SMALL_3D = ("b_re", "b_im", "c_re", "c_im")
SMALL_FLAT = ("norm_w", "q_norm_w", "k_norm_w", "sinks", "a_re", "a_im", "log_step", "d_skip", "b_glu",
              "attn_out_norm_w", "ssm_out_norm_w")
SMALL = SMALL_3D + SMALL_FLAT
SMALL_SHAPES = {"norm_w": (2048,), "q_norm_w": (64,), "k_norm_w": (64,), "sinks": (16,), "a_re": (64, 64),
                "a_im": (64, 64), "log_step": (64,), "b_re": (64, 64, 16), "b_im": (64, 64, 16),
                "c_re": (64, 16, 64), "c_im": (64, 16, 64), "d_skip": (1024,), "b_glu": (1024,),
                "attn_out_norm_w": (1024,), "ssm_out_norm_w": (1024,)}
PACK_ROWS = 272
FLAT_ROWS = 120


def _params(sem=None):
    return pltpu.CompilerParams(dimension_semantics=sem, vmem_limit_bytes=VMEM_LIMIT)


def _dot(a, b):
    return jnp.dot(a, b, preferred_element_type=F32)


def _dot_nt(a, b):
    return lax.dot_general(a, b, (((1,), (1,)), ((), ())), preferred_element_type=F32)


def _dot_tn(a, b):
    return lax.dot_general(a, b, (((0,), (0,)), ((), ())), preferred_element_type=F32)


def _sigmoid(x):
    return 1.0 / (1.0 + jnp.exp(-x))


def _silu(x):
    return x * _sigmoid(x)


def _dsilu(x):
    s = _sigmoid(x)
    return s * (1.0 + x * (1.0 - s))


_GELU_C = math.sqrt(2.0 / math.pi)


def _gelu(x):
    return 0.5 * x * (1.0 + jnp.tanh(_GELU_C * (x + 0.044715 * x * x * x)))


def _dgelu(x):
    t = jnp.tanh(_GELU_C * (x + 0.044715 * x * x * x))
    return 0.5 * (1.0 + t) + 0.5 * x * (1.0 - t * t) * _GELU_C * (1.0 + 3.0 * 0.044715 * x * x)


def _matmul_tn(a, b, *, tm, tn, name, slabs=False, dtypes=(F32, BF16)):
    k, m = a.shape
    _, n = b.shape

    def body(a_ref, b_ref, *o_refs):
        acc = _dot_tn(a_ref[...], b_ref[...])
        for o_ref in o_refs:
            o_ref[...] = acc.astype(o_ref.dtype)

    if slabs:
        out_spec = pl.BlockSpec((None, tm, tn), lambda j, i: (j, i, 0))
        shape = (n // tn, m, tn)
    else:
        out_spec = pl.BlockSpec((tm, tn), lambda j, i: (i, j))
        shape = (m, n)
    return pl.pallas_call(
        body, name=name, grid=(n // tn, m // tm),
        in_specs=[pl.BlockSpec((k, tm), lambda j, i: (0, i)), pl.BlockSpec((k, tn), lambda j, i: (0, j))],
        out_specs=[out_spec] * len(dtypes),
        out_shape=[jax.ShapeDtypeStruct(shape, d) for d in dtypes],
        compiler_params=_params(("parallel", "parallel")),
    )(a, b)


def _inproj(x, norm_w, w_sh, order):
    t = x.shape[0]
    tm = 512
    ni = t // tm
    hr = D_MODEL // 2

    def body(ord_ref, x_ref, nw_ref, sh_ref, proj_ref, hn_ref, full_ref, wbuf, hn_s, ssem, rsem, lsem):
        s, i = pl.program_id(0), pl.program_id(1)
        mx, my, c = _mesh_pos()
        me = 2 * mx + my
        sib = (mx, my, 1 - c)
        chips = _other_chips(mx, my)

        def half(which):
            return pl.ds(pl.multiple_of(which * hr, 8), hr)

        def slot(k):
            return 2 * chips[k][0] + chips[k][1]

        def ici(k):
            return _remote(sh_ref.at[half(c)], full_ref.at[me, half(c)], ssem.at[k], rsem.at[k], (*chips[k], c))

        def own():
            return _remote(sh_ref, full_ref.at[me], ssem.at[6], rsem.at[6], sib)

        def landed(k, which, sem):
            ref = full_ref.at[slot(k), half(which)]
            return _remote(ref, ref, ssem.at[sem], rsem.at[sem], sib)

        def fetch(src, b):
            return pltpu.make_async_copy(src, wbuf.at[b], lsem.at[b])

        @pl.when((s == 0) & (i == 0))
        def _():
            for k in range(3):
                ici(k).start()
            own().start()
            cp = fetch(sh_ref, 0)
            cp.start()
            cp.wait()

        for k in range(3):
            @pl.when((s == k) & (i == max(ni - 2, 0)))
            def _(k=k):
                landed(k, c, k).wait_recv()
                landed(k, c, 3 + k).start()
                landed(k, 1 - c, 3 + k).wait_recv()
                fetch(full_ref.at[slot(k)], (k + 1) % 2).start()

            @pl.when((s == k + 1) & (i == 0))
            def _(k=k):
                fetch(full_ref.at[slot(k)], (k + 1) % 2).wait()

        xv = x_ref[...]
        r = lax.rsqrt(jnp.mean(xv * xv, axis=1, keepdims=True) + NORM_EPS)
        hn = (xv * r * nw_ref[...]).astype(BF16)
        proj_ref[...] = _dot(hn, wbuf[s % 2])

        def hn_out(tile):
            return pltpu.make_async_copy(hn_s, hn_ref.at[pl.ds(pl.multiple_of(tile * tm, tm), tm), :], lsem.at[2])

        @pl.when(((s == 0) & (i > 0)) | ((s == 1) & (i == 0)))
        def _():
            hn_out(jnp.where(s == 0, i - 1, ni - 1)).wait()

        @pl.when(s == 0)
        def _():
            hn_s[...] = hn
            hn_out(i).start()

        @pl.when((s == 3) & (i == ni - 1))
        def _():
            mine = full_ref.at[me]
            _remote(mine, mine, ssem.at[6], rsem.at[6], sib).wait_recv()
            for k in range(3):
                ici(k).wait_send()
                landed(k, c, 3 + k).wait_send()
            own().wait_send()

    return pl.pallas_call(
        body, name="inproj",
        grid_spec=pltpu.PrefetchScalarGridSpec(
            num_scalar_prefetch=1, grid=(4, ni),
            in_specs=[pl.BlockSpec((tm, D_MODEL), lambda s, i, o: (i, 0)),
                      pl.BlockSpec((1, D_MODEL), lambda s, i, o: (0, 0)), ANY],
            out_specs=[pl.BlockSpec((tm, SHARD_W), lambda s, i, o: (i, o[s])), ANY, ANY],
            scratch_shapes=[pltpu.VMEM((2, D_MODEL, SHARD_W), BF16), pltpu.VMEM((tm, D_MODEL), BF16),
                            pltpu.SemaphoreType.DMA((7,)), pltpu.SemaphoreType.DMA((7,)),
                            pltpu.SemaphoreType.DMA((3,))]),
        out_shape=[jax.ShapeDtypeStruct((t, IN_W), F32), jax.ShapeDtypeStruct((t, D_MODEL), BF16),
                   jax.ShapeDtypeStruct((4, D_MODEL, SHARD_W), BF16)],
        compiler_params=_params(("arbitrary", "arbitrary")),
    )(order, x, norm_w.reshape(1, D_MODEL), w_sh)


def _lane128():
    return lax.broadcasted_iota(jnp.int32, (1, 128), 1)


def _head_sums(v):
    lo = _lane128() < 64
    s_lo = jnp.sum(jnp.where(lo, v, 0.0), axis=1, keepdims=True)
    s_hi = jnp.sum(jnp.where(lo, 0.0, v), axis=1, keepdims=True)
    return jnp.where(lo, s_lo, s_hi)


def _rot_half(t):
    first = (_lane128() % 64) < 32
    return jnp.where(first, -pltpu.roll(t, 96, 1), pltpu.roll(t, 32, 1))


def _head_rstd(t):
    return lax.rsqrt(_head_sums(t * t) * (1.0 / HEAD_DIM) + NORM_EPS)


def _prep_tile(t, w, cos, sin, r=None):
    r = _head_rstd(t) if r is None else r
    tn = t * r * w
    return tn * cos + _rot_half(tn) * sin


def _prep_tile_bwd(t, w, cos, sin, g, r=None):
    r = _head_rstd(t) if r is None else r
    d_tn = g * cos - _rot_half(g * sin)
    th = t * r
    dw = jnp.sum(d_tn * th, axis=0, keepdims=True)
    gh = d_tn * w
    m = _head_sums(gh * th) * (1.0 / HEAD_DIM)
    return r * (gh - th * m), dw


def _band_mask(n):
    qi = lax.broadcasted_iota(jnp.int32, (BLOCK, 2 * BLOCK), 0) + BLOCK
    ki = lax.broadcasted_iota(jnp.int32, (BLOCK, 2 * BLOCK), 1)
    rel = qi - ki
    return (rel >= 0) & (rel < BLOCK) & ((n > 0) | (ki >= BLOCK))


def _half_select(tile, half):
    lo = _lane128() < 64
    return jnp.where(lo if half == 0 else jnp.logical_not(lo), tile, 0.0)


def _stack_group(tiles, kv_half):
    rows = []
    for t in tiles:
        for half in range(2):
            piece = _half_select(t, half)
            rows.append(piece if half == kv_half else pltpu.roll(piece, 64, 1))
    return jnp.concatenate(rows, axis=0)


def _unstack_group(stacked, kv_half):
    tiles = []
    for i in range(2):
        acc = None
        for half in range(2):
            piece = _half_select(stacked[BLOCK * (2 * i + half):BLOCK * (2 * i + half + 1)], kv_half)
            piece = piece if half == kv_half else pltpu.roll(piece, 64, 1)
            acc = piece if acc is None else acc + piece
        tiles.append(acc)
    return tiles


def _attn_specs(nb):
    last = nb - 1
    qi = lambda n: (jnp.minimum(n, last), 0)
    prev = lambda n: jnp.maximum(n - 1, 0)
    cur = lambda n: jnp.minimum(n, last)
    specs = [
        pl.BlockSpec((BLOCK, ATTN_W), qi),
        pl.BlockSpec((BLOCK, KV_W), lambda n: (cur(n), 4)),
        pl.BlockSpec((BLOCK, KV_W), lambda n: (prev(n), 4)),
        pl.BlockSpec((BLOCK, KV_W), lambda n: (cur(n), 5)),
        pl.BlockSpec((BLOCK, KV_W), lambda n: (prev(n), 5)),
        pl.BlockSpec((BLOCK, 512), lambda n: (cur(n), 3)),
        pl.BlockSpec((BLOCK, 512), lambda n: (cur(n), 4)),
        pl.BlockSpec((BLOCK, 1), lambda n: (cur(n), 0)),
        pl.BlockSpec((BLOCK, 1), lambda n: (prev(n), 0)),
        pl.BlockSpec((1, 128), lambda n: (0, 0)),
        pl.BlockSpec((1, 128), lambda n: (0, 0)),
        pl.BlockSpec((1, 128), lambda n: (0, 0)),
        pl.BlockSpec((1, N_HEADS), lambda n: (0, 0)),
    ]
    return specs


def _attn_common(n, q_ref, kc_ref, kp_ref, vc_ref, vp_ref, pq_ref, pp_ref, invf_ref, qw_ref, kw_ref):
    invf = invf_ref[...]
    ang_q = pq_ref[...] * invf
    ang_p = pp_ref[...] * invf
    cos_q, sin_q = jnp.cos(ang_q), jnp.sin(ang_q)
    cos_k = jnp.concatenate([jnp.cos(ang_p), cos_q], axis=0)
    sin_k = jnp.concatenate([jnp.sin(ang_p), sin_q], axis=0)
    k_raw = jnp.concatenate([kp_ref[...], kc_ref[...]], axis=0)
    vv = jnp.concatenate([vp_ref[...], vc_ref[...]], axis=0).astype(BF16)
    kk = [_prep_tile(k_raw[:, 128 * i:128 * i + 128], kw_ref[...], cos_k, sin_k).astype(BF16) for i in range(2)]
    vt = [vv[:, 128 * i:128 * i + 128] for i in range(2)]
    qv = q_ref[...]
    qr = [_head_rstd(qv[:, 128 * i:128 * i + 128]) for i in range(8)]
    qt = [_prep_tile(qv[:, 128 * i:128 * i + 128], qw_ref[...], cos_q, sin_q, qr[i]) for i in range(8)]
    return cos_q, sin_q, qr, kk, vt, qt


QK_SCALE = 1.0 / math.sqrt(HEAD_DIM)


def _group_sinks(sink_ref, g):
    return jnp.concatenate([jnp.broadcast_to(sink_ref[:, 4 * g + j:4 * g + j + 1], (BLOCK, 1)) for j in range(4)], axis=0)


def _group_softmax(q4, kk_t, sink, bias):
    s = _dot_nt(q4, kk_t) + bias
    m = jnp.maximum(jnp.max(s, axis=1, keepdims=True), sink)
    p = jnp.exp(s - m)
    es = jnp.exp(sink - m)
    inv = 1.0 / (jnp.sum(p, axis=1, keepdims=True) + es)
    return p * inv, es * inv


def _group_bias(n):
    return jnp.concatenate([jnp.where(_band_mask(n), 0.0, -1e30)] * 4, axis=0)


def _attn_fwd(proj, posf, invf, qw, kw, sinks, later_shards):
    t = proj.shape[0]
    nb = t // BLOCK
    nt = len(later_shards)

    def body(q_ref, kc_ref, kp_ref, vc_ref, vp_ref, za0_ref, za1_ref, pq_ref, pp_ref, invf_ref, qw_ref, kw_ref,
             sink_ref, *rest):
        sh, o_ref, full = rest[:nt], rest[nt], rest[nt + 1:2 * nt + 1]
        ssem, rsem = rest[2 * nt + 1:]
        n = pl.program_id(0)
        start, wait = _bg_gather(sh, full, ssem, rsem)

        @pl.when(n == 0)
        def _():
            start()

        _, _, _, kk, vt, qt = _attn_common(n, q_ref, kc_ref, kp_ref, vc_ref, vp_ref, pq_ref, pp_ref, invf_ref,
                                           qw_ref, kw_ref)
        bias = _group_bias(n)
        tiles = []
        for g in range(N_KV_HEADS):
            q4 = (_stack_group(qt[2 * g:2 * g + 2], g % 2) * QK_SCALE).astype(BF16)
            p, _ = _group_softmax(q4, kk[g // 2], _group_sinks(sink_ref, g), bias)
            tiles += _unstack_group(_dot(p.astype(BF16), vt[g // 2]), g % 2)
        za = jnp.concatenate([za0_ref[...], za1_ref[...]], axis=1)
        o_ref[...] = jnp.concatenate(tiles, axis=1) * _silu(za)

        @pl.when(n == nb - 1)
        def _():
            wait()

    return pl.pallas_call(
        body, name="attn_fwd", grid=(nb,), in_specs=_attn_specs(nb) + [ANY] * nt,
        out_specs=[pl.BlockSpec((BLOCK, ATTN_W), lambda n: (n, 0))] + [ANY] * nt,
        out_shape=[jax.ShapeDtypeStruct((t, ATTN_W), F32)]
        + [jax.ShapeDtypeStruct((4,) + s.shape, s.dtype) for s in later_shards],
        scratch_shapes=[pltpu.SemaphoreType.DMA((4 * nt,)), pltpu.SemaphoreType.DMA((4 * nt,))],
        compiler_params=_params(("arbitrary",)),
    )(proj, proj, proj, proj, proj, proj, proj, posf, posf, invf, qw, kw, sinks, *later_shards)


def _attn_bwd(proj, posf, invf, qw, kw, sinks, doa, outgoing):
    t = proj.shape[0]
    nb = t // BLOCK
    last = nb - 1
    nt = len(outgoing)

    def body(q_ref, kc_ref, kp_ref, vc_ref, vp_ref, za0_ref, za1_ref, pq_ref, pp_ref, invf_ref, qw_ref, kw_ref,
             sink_ref, doa_ref, *rest):
        src = rest[:nt]
        dq_ref, dk_ref, dv_ref, dza_ref, gq_ref, gk_ref, gs_ref = rest[nt:nt + 7]
        dst = rest[nt + 7:2 * nt + 7]
        dkk_s, dvv_s, ck_s, cv_s, ssem, rsem = rest[2 * nt + 7:]
        n = pl.program_id(0)
        start, wait = _bg_scatter_devices(src, dst, ssem, rsem)

        @pl.when(n == 0)
        def _():
            start()
            gq_ref[...] = jnp.zeros_like(gq_ref)
            gk_ref[...] = jnp.zeros_like(gk_ref)
            gs_ref[...] = jnp.zeros_like(gs_ref)
            ck_s[...] = jnp.zeros_like(ck_s)
            cv_s[...] = jnp.zeros_like(cv_s)

        @pl.when(n == nb)
        def _():
            dkk_s[...] = jnp.zeros_like(dkk_s)
            dvv_s[...] = jnp.zeros_like(dvv_s)

        @pl.when(n < nb)
        def _():
            cos_q, sin_q, qr, kk, vt, qt = _attn_common(n, q_ref, kc_ref, kp_ref, vc_ref, vp_ref, pq_ref, pp_ref,
                                                        invf_ref, qw_ref, kw_ref)
            bias = _group_bias(n)
            za = jnp.concatenate([za0_ref[...], za1_ref[...]], axis=1)
            doa_v = doa_ref[...]
            do_full = doa_v * _silu(za)
            o_tiles, dq_tiles = [], []
            dkk = [jnp.zeros((2 * BLOCK, 128), F32) for _ in range(2)]
            dvv = [jnp.zeros((2 * BLOCK, 128), F32) for _ in range(2)]
            gsink = jnp.zeros((1, 128), F32)
            lane = _lane128()
            for g in range(N_KV_HEADS):
                q_b = (_stack_group(qt[2 * g:2 * g + 2], g % 2) * QK_SCALE).astype(BF16)
                do_b = _stack_group([do_full[:, 128 * i:128 * i + 128] for i in (2 * g, 2 * g + 1)], g % 2).astype(BF16)
                p, psink = _group_softmax(q_b, kk[g // 2], _group_sinks(sink_ref, g), bias)
                p_b = p.astype(BF16)
                dp = _dot_nt(do_b, vt[g // 2])
                delta = jnp.sum(p * dp, axis=1, keepdims=True)
                ds_b = (p * (dp - delta)).astype(BF16)
                sd = psink * delta
                for j in range(4):
                    gsink = gsink + jnp.where(lane == 4 * g + j, -jnp.sum(sd[BLOCK * j:BLOCK * (j + 1)]), 0.0)
                o_tiles += _unstack_group(_dot(p_b, vt[g // 2]), g % 2)
                dq_tiles += [d * QK_SCALE for d in _unstack_group(_dot(ds_b, kk[g // 2]), g % 2)]
                dkk[g // 2] = dkk[g // 2] + _dot_tn(ds_b, q_b)
                dvv[g // 2] = dvv[g // 2] + _dot_tn(p_b, do_b)
            dza_ref[...] = (doa_v * jnp.concatenate(o_tiles, axis=1) * _dsilu(za)).astype(BF16)
            qv = q_ref[...]
            gq = jnp.zeros((1, 128), F32)
            out = []
            for i in range(8):
                d, dw = _prep_tile_bwd(qv[:, 128 * i:128 * i + 128], qw_ref[...], cos_q, sin_q, dq_tiles[i], qr[i])
                out.append(d)
                gq = gq + dw
            dq_ref[...] = jnp.concatenate(out, axis=1).astype(BF16)
            gq_ref[...] += gq
            gs_ref[...] += gsink
            dkk_s[...] = jnp.concatenate(dkk, axis=1)
            dvv_s[...] = jnp.concatenate(dvv, axis=1)

        invf = invf_ref[...]
        ang_p = pp_ref[...] * invf
        cos_p, sin_p = jnp.cos(ang_p), jnp.sin(ang_p)
        dk_prev = ck_s[...] + dkk_s[0:BLOCK, :]
        kp = kp_ref[...]
        gk = jnp.zeros((1, 128), F32)
        out = []
        for i in range(2):
            d, dw = _prep_tile_bwd(kp[:, 128 * i:128 * i + 128], kw_ref[...], cos_p, sin_p,
                                   dk_prev[:, 128 * i:128 * i + 128])
            out.append(d)
            gk = gk + dw
        dk_ref[...] = jnp.concatenate(out, axis=1).astype(BF16)
        dv_ref[...] = (cv_s[...] + dvv_s[0:BLOCK, :]).astype(BF16)
        gk_ref[...] += gk
        ck_s[...] = dkk_s[BLOCK:2 * BLOCK, :]
        cv_s[...] = dvv_s[BLOCK:2 * BLOCK, :]

        @pl.when(n == nb)
        def _():
            wait()

    qblk = lambda n: (jnp.minimum(n, last), 0)
    kblk = lambda n: (jnp.maximum(n - 1, 0), 0)
    vec = pl.BlockSpec((1, 128), lambda n: (0, 0))
    return pl.pallas_call(
        body, name="attn_bwd", grid=(nb + 1,),
        in_specs=_attn_specs(nb) + [pl.BlockSpec((BLOCK, ATTN_W), qblk)] + [ANY] * nt,
        out_specs=[pl.BlockSpec((BLOCK, ATTN_W), qblk), pl.BlockSpec((BLOCK, KV_W), kblk),
                   pl.BlockSpec((BLOCK, KV_W), kblk), pl.BlockSpec((BLOCK, ATTN_W), qblk), vec, vec, vec] + [ANY] * nt,
        out_shape=[jax.ShapeDtypeStruct((t, ATTN_W), BF16), jax.ShapeDtypeStruct((t, KV_W), BF16),
                   jax.ShapeDtypeStruct((t, KV_W), BF16), jax.ShapeDtypeStruct((t, ATTN_W), BF16),
                   jax.ShapeDtypeStruct((1, 128), F32), jax.ShapeDtypeStruct((1, 128), F32),
                   jax.ShapeDtypeStruct((1, 128), F32)] + [jax.ShapeDtypeStruct(a.shape, a.dtype) for a in outgoing],
        scratch_shapes=[pltpu.VMEM((2 * BLOCK, KV_W), F32), pltpu.VMEM((2 * BLOCK, KV_W), F32),
                        pltpu.VMEM((BLOCK, KV_W), F32), pltpu.VMEM((BLOCK, KV_W), F32),
                        pltpu.SemaphoreType.DMA((7 * nt,)), pltpu.SemaphoreType.DMA((7 * nt,))],
        compiler_params=_params(("arbitrary",)),
    )(proj, proj, proj, proj, proj, proj, proj, posf, posf, invf, qw, kw, sinks, doa, *outgoing)


def _cmul(ar, ai, br, bi):
    return ar * br - ai * bi, ar * bi + ai * br


def _zoh(a_re, a_im, delta):
    e = jnp.exp(a_re * delta)
    lr, li = e * jnp.cos(a_im * delta), e * jnp.sin(a_im * delta)
    inv = 1.0 / (a_re * a_re + a_im * a_im)
    fr, fi = _cmul(lr - 1.0, li, a_re * inv, -a_im * inv)
    return lr, li, fr, fi


def _ssm_prep(a_re, a_im, log_step, b_re, b_im, seg_len):
    n_sq = int(round(math.log2(seg_len)))
    assert 2 ** n_sq == seg_len

    def body(ar_ref, ai_ref, ls_ref, arx_ref, aix_ref, br_ref, bi_ref, lr_ref, li_ref, pr_ref, pi_ref, bbr_ref, bbi_ref):
        delta = jnp.exp(ls_ref[...])
        lr, li, _, _ = _zoh(ar_ref[...], ai_ref[...], delta)
        lr_ref[...] = lr
        li_ref[...] = li
        pr, pi = lr, li
        for _ in range(n_sq):
            pr, pi = _cmul(pr, pi, pr, pi)
        pr_ref[...] = pr
        pi_ref[...] = pi
        _, _, fr, fi = _zoh(arx_ref[...], aix_ref[...], delta)
        bbr, bbi = _cmul(fr, fi, br_ref[...], bi_ref[...])
        bbr_ref[...] = bbr
        bbi_ref[...] = bbi

    gp = jax.ShapeDtypeStruct((SSM_G, SSM_P), F32)
    gx = jax.ShapeDtypeStruct((SSM_G, SSM_P * SSM_H), F32)
    return pl.pallas_call(body, name="ssm_prep", out_shape=[gp, gp, gp, gp, gx, gx])(
        a_re, a_im, log_step.reshape(SSM_G, 1), jnp.repeat(a_re, SSM_H, axis=1), jnp.repeat(a_im, SSM_H, axis=1),
        b_re.reshape(SSM_G, SSM_P * SSM_H), b_im.reshape(SSM_G, SSM_P * SSM_H))


def _ssm_param_grads(a_re, a_im, log_step, b_re, b_im, dlam_re, dlam_im, dbb_re, dbb_im):
    def body(ar_ref, ai_ref, ls_ref, arx_ref, aix_ref, br_ref, bi_ref, dlr_ref, dli_ref, dbr_ref, dbi_ref,
             gar_ref, gai_ref, gls_ref, gbr_ref, gbi_ref):
        delta = jnp.exp(ls_ref[...])
        ar, ai = ar_ref[...], ai_ref[...]
        lr, li, fr, fi = _zoh(ar, ai, delta)
        _, _, frx, fix = _zoh(arx_ref[...], aix_ref[...], delta)
        dbr, dbi = dbr_ref[...], dbi_ref[...]
        br, bi = br_ref[...], bi_ref[...]
        gbr, gbi = _cmul(frx, -fix, dbr, dbi)
        gbr_ref[...] = gbr
        gbi_ref[...] = gbi
        tr, ti = _cmul(br, -bi, dbr, dbi)
        row = lax.broadcasted_iota(jnp.int32, (SSM_P * SSM_H, SSM_P), 0)
        col = lax.broadcasted_iota(jnp.int32, (SSM_P * SSM_H, SSM_P), 1)
        fold = (row // SSM_H == col).astype(F32)
        dfr = jnp.dot(tr, fold, precision=lax.Precision.HIGHEST, preferred_element_type=F32)
        dfi = jnp.dot(ti, fold, precision=lax.Precision.HIGHEST, preferred_element_type=F32)
        inv = 1.0 / (ar * ar + ai * ai)
        ilr, ili = ar * inv, -ai * inv
        t1r, t1i = _cmul(dfr, dfi, ilr, -ili)
        dlbr, dlbi = dlr_ref[...] + t1r, dli_ref[...] + t1i
        qr, qi = _cmul(fr, fi, ilr, ili)
        t2r, t2i = _cmul(dfr, dfi, qr, -qi)
        glr, gli = -t2r, -t2i
        dzr, dzi = _cmul(dlbr, dlbi, lr, -li)
        gar_ref[...] = glr + dzr * delta
        gai_ref[...] = gli + dzi * delta
        gls_ref[...] = jnp.sum(dzr * ar + dzi * ai, axis=1, keepdims=True) * delta

    gp = jax.ShapeDtypeStruct((SSM_G, SSM_P), F32)
    gx = jax.ShapeDtypeStruct((SSM_G, SSM_P * SSM_H), F32)
    return pl.pallas_call(body, name="ssm_param_grads",
                          out_shape=[gp, gp, jax.ShapeDtypeStruct((SSM_G, 1), F32), gx, gx])(
        a_re, a_im, log_step.reshape(SSM_G, 1), jnp.repeat(a_re, SSM_H, axis=1), jnp.repeat(a_im, SSM_H, axis=1),
        b_re.reshape(SSM_G, SSM_P * SSM_H), b_im.reshape(SSM_G, SSM_P * SSM_H), dlam_re, dlam_im, dbb_re, dbb_im)


def _block_diag_in(bb):
    w = jnp.tile(bb.reshape(SSM_GB, SSM_ST, SSM_H), (1, 1, 16))
    row = lax.broadcasted_iota(jnp.int32, (1, SSM_ST, SSM_CH), 1) // SSM_P
    col = lax.broadcasted_iota(jnp.int32, (1, SSM_ST, SSM_CH), 2) // SSM_H
    return jnp.where(row == col, w, 0.0)


def _block_diag_out(c):
    w = jnp.tile(c.reshape(SSM_GB, SSM_CH, SSM_P), (1, 1, 16))
    row = lax.broadcasted_iota(jnp.int32, (1, SSM_CH, SSM_ST), 1) // SSM_H
    col = lax.broadcasted_iota(jnp.int32, (1, SSM_CH, SSM_ST), 2) // SSM_P
    return jnp.where(row == col, w, 0.0)


def _permute_rows(a):
    t, c = a.shape
    return a.reshape(N_SEG, t // N_SEG, c).transpose(1, 0, 2).reshape(t, c)


def _unpermute_rows(a):
    t, c = a.shape
    return a.reshape(t // N_SEG, N_SEG, c).transpose(1, 0, 2).reshape(t, c)


def _scan_fwd(src_ref, dst_ref, lam_r_ref, lam_i_ref, init_ref, final_ref, steps):
    for k in range(SSM_ST // SCAN_LW):
        re = pl.ds(k * SCAN_LW, SCAN_LW)
        im = pl.ds(SSM_ST + k * SCAN_LW, SCAN_LW)
        lr, li = lam_r_ref[:, re], lam_i_ref[:, re]

        def step(i, carry, re=re, im=im, lr=lr, li=li):
            hr, hi = carry
            rows = pl.ds(pl.multiple_of(i * 8, 8), 8)
            nr = lr * hr - li * hi + src_ref[rows, re]
            ni = lr * hi + li * hr + src_ref[rows, im]
            if dst_ref is not None:
                dst_ref[rows, re] = nr
                dst_ref[rows, im] = ni
            return nr, ni

        hr, hi = lax.fori_loop(0, steps, step, (init_ref[:, re], init_ref[:, im]), unroll=4)
        final_ref[:, re] = hr
        final_ref[:, im] = hi


def _ssm_specs(t):
    col = lambda g: (0, g)
    gb3 = lambda g: (g, 0, 0)
    return dict(
        rows=pl.BlockSpec((t, SSM_CH), col),
        lam=pl.BlockSpec((None, N_SEG, SSM_ST), gb3),
        w_in=pl.BlockSpec((None, 2 * SSM_ST, SSM_CH), gb3),
        w_out=pl.BlockSpec((None, SSM_CH, 2 * SSM_ST), gb3),
        vec=pl.BlockSpec((1, SSM_CH), col),
    )


def _segment_states(x_ref, pw_r_ref, pw_i_ref, out_ref, reverse):
    re, im = pl.ds(0, SSM_ST), pl.ds(SSM_ST, SSM_ST)
    pr, pi = pw_r_ref[0:1, :], pw_i_ref[0:1, :]
    first = N_SEG - 1 if reverse else 0
    out_ref[first:first + 1, :] = jnp.zeros((1, 2 * SSM_ST), F32)
    order = range(N_SEG - 1, 0, -1) if reverse else range(N_SEG - 1)
    for s in order:
        d = s - 1 if reverse else s + 1
        hr, hi = out_ref[s:s + 1, re], out_ref[s:s + 1, im]
        if reverse:
            nr, ni = pr * hr + pi * hi, pr * hi - pi * hr
        else:
            nr, ni = pr * hr - pi * hi, pr * hi + pi * hr
        out_ref[d:d + 1, re] = nr + x_ref[s:s + 1, re]
        out_ref[d:d + 1, im] = ni + x_ref[s:s + 1, im]


def _ssm_fwd(up, lam_r, lam_i, pw_r, pw_i, w_in, w_out, d_skip):
    t = up.shape[0]
    nch = t // SCAN_ROWS
    steps = SCAN_ROWS // N_SEG
    sp = _ssm_specs(t)

    def body(u_ref, lr_ref, li_ref, pr_ref, pi_ref, wi_ref, wo_ref, d_ref, y_ref, hc_ref, bu_s, car_s, seg_s):
        def load_bu(j):
            rows = pl.ds(pl.multiple_of(j * SCAN_ROWS, SCAN_ROWS), SCAN_ROWS)
            bu_s[...] = _dot_nt(u_ref[rows, :].astype(BF16), wi_ref[...])

        car_s[...] = jnp.zeros_like(car_s)

        def chunk1(j, c):
            load_bu(j)
            _scan_fwd(bu_s, None, lr_ref, li_ref, car_s, car_s, steps)
            return c

        lax.fori_loop(0, nch, chunk1, 0)
        _segment_states(car_s, pr_ref, pi_ref, seg_s, reverse=False)
        car_s[...] = seg_s[...]

        def chunk2(j, c):
            load_bu(j)
            hc_ref[j] = car_s[...]
            _scan_fwd(bu_s, bu_s, lr_ref, li_ref, car_s, car_s, steps)
            rows = pl.ds(pl.multiple_of(j * SCAN_ROWS, SCAN_ROWS), SCAN_ROWS)
            y_ref[rows, :] = _dot_nt(bu_s[...].astype(BF16), wo_ref[...]) + d_ref[...] * u_ref[rows, :]
            return c

        lax.fori_loop(0, nch, chunk2, 0)

    return pl.pallas_call(
        body, name="ssm_fwd", grid=(SSM_GB,),
        in_specs=[sp["rows"], sp["lam"], sp["lam"], sp["lam"], sp["lam"], sp["w_in"], sp["w_out"], sp["vec"]],
        out_specs=[sp["rows"], pl.BlockSpec((None, nch, N_SEG, 2 * SSM_ST), lambda g: (g, 0, 0, 0))],
        out_shape=[jax.ShapeDtypeStruct((t, SSM_W), F32), jax.ShapeDtypeStruct((SSM_GB, nch, N_SEG, 2 * SSM_ST), F32)],
        scratch_shapes=[pltpu.VMEM((SCAN_ROWS, 2 * SSM_ST), F32), pltpu.VMEM((N_SEG, 2 * SSM_ST), F32),
                        pltpu.VMEM((N_SEG, 2 * SSM_ST), F32)],
        compiler_params=_params(("parallel",)),
    )(up, lam_r, lam_i, pw_r, pw_i, w_in, w_out, d_skip)


def _group_blocks(full):
    row_g = lax.broadcasted_iota(jnp.int32, (SSM_CH, SSM_ST), 0) // SSM_H
    col_g = lax.broadcasted_iota(jnp.int32, (SSM_CH, SSM_ST), 1) // SSM_P
    fold = (lax.broadcasted_iota(jnp.int32, (SSM_ST, SSM_P), 0) % SSM_P
            == lax.broadcasted_iota(jnp.int32, (SSM_ST, SSM_P), 1)).astype(F32)
    parts = [jnp.dot(jnp.where(row_g == col_g, full[:, k * SSM_ST:(k + 1) * SSM_ST], 0.0), fold,
                     precision=lax.Precision.HIGHEST, preferred_element_type=F32) for k in range(2)]
    return jnp.concatenate(parts, axis=1)


def _ssm_bwd(up, dyp, hc, lam_r, lam_i, pw_r, pw_i, w_in, w_out, d_skip):
    t = up.shape[0]
    nch = t // SCAN_ROWS
    steps = SCAN_ROWS // N_SEG
    sp = _ssm_specs(t)

    def body(u_ref, dy_ref, hc_ref, lr_ref, li_ref, pr_ref, pi_ref, wi_ref, wo_ref, d_ref,
             du_ref, gbi_ref, gbo_ref, glam_ref, gd_ref, bu_s, h_s, e_s, car_s, seg_s, acc_s, gwi_ref, gwo_ref):
        def chunk_rows(j):
            return pl.ds(pl.multiple_of(j * SCAN_ROWS, SCAN_ROWS), SCAN_ROWS)

        def load_e(j):
            e_s[...] = _dot(dy_ref[chunk_rows(j), :].astype(BF16), wo_ref[...])

        def scan_rev(j, accumulate):
            for k in range(SSM_ST // SCAN_LW):
                re = pl.ds(k * SCAN_LW, SCAN_LW)
                im = pl.ds(SSM_ST + k * SCAN_LW, SCAN_LW)
                lr, li = lr_ref[:, re], li_ref[:, re]

                def step(ii, carry, re=re, im=im, lr=lr, li=li):
                    i = steps - 1 - ii
                    rows = pl.ds(pl.multiple_of(i * 8, 8), 8)
                    if accumulate:
                        gr, gi, ar, ai = carry
                    else:
                        gr, gi = carry
                    nr = lr * gr + li * gi + e_s[rows, re]
                    ni = lr * gi - li * gr + e_s[rows, im]
                    if not accumulate:
                        return nr, ni
                    e_s[rows, re] = nr
                    e_s[rows, im] = ni
                    pr_, pi_ = h_s[rows, re], h_s[rows, im]
                    return nr, ni, ar + nr * pr_ + ni * pi_, ai + ni * pr_ - nr * pi_

                init = (car_s[:, re], car_s[:, im])
                if accumulate:
                    init = init + (acc_s[:, re], acc_s[:, im])
                out = lax.fori_loop(0, steps, step, init, unroll=4)
                car_s[:, re] = out[0]
                car_s[:, im] = out[1]
                if accumulate:
                    acc_s[:, re] = out[2]
                    acc_s[:, im] = out[3]

        car_s[...] = jnp.zeros_like(car_s)

        def pass1(jj, c):
            load_e(nch - 1 - jj)
            scan_rev(nch - 1 - jj, False)
            return c

        lax.fori_loop(0, nch, pass1, 0)
        _segment_states(car_s, pr_ref, pi_ref, seg_s, reverse=True)
        car_s[...] = seg_s[...]
        acc_s[...] = jnp.zeros_like(acc_s)
        gwi_ref[...] = jnp.zeros_like(gwi_ref)
        gwo_ref[...] = jnp.zeros_like(gwo_ref)
        gd_ref[...] = jnp.zeros_like(gd_ref)

        def pass2(jj, c):
            j = nch - 1 - jj
            rows = chunk_rows(j)
            u = u_ref[rows, :]
            dy = dy_ref[rows, :]
            u_b, dy_b = u.astype(BF16), dy.astype(BF16)
            bu_s[...] = _dot_nt(u_b, wi_ref[...])
            h_s[0:N_SEG, :] = hc_ref[j]
            seg_s[...] = hc_ref[j]
            _scan_fwd(bu_s, h_s.at[pl.ds(N_SEG, SCAN_ROWS), :], lr_ref, li_ref, seg_s, seg_s, steps)
            load_e(j)
            scan_rev(j, True)
            g_b = e_s[...].astype(BF16)
            du_ref[rows, :] = (_dot(g_b, wi_ref[...]) + d_ref[...] * dy).astype(du_ref.dtype)
            gwi_ref[...] += _dot_tn(u_b, g_b)
            gwo_ref[...] += _dot_tn(dy_b, h_s[pl.ds(N_SEG, SCAN_ROWS), :].astype(BF16))
            gd_ref[...] += jnp.sum(dy * u, axis=0, keepdims=True)
            return c

        lax.fori_loop(0, nch, pass2, 0)
        glam_ref[...] = jnp.sum(acc_s[...], axis=0, keepdims=True)
        gbi_ref[...] = _group_blocks(gwi_ref[...])
        gbo_ref[...] = _group_blocks(gwo_ref[...])

    mat = pl.BlockSpec((None, SSM_CH, 2 * SSM_P), lambda g: (g, 0, 0))
    return pl.pallas_call(
        body, name="ssm_bwd", grid=(SSM_GB,),
        in_specs=[sp["rows"], sp["rows"], pl.BlockSpec((None, nch, N_SEG, 2 * SSM_ST), lambda g: (g, 0, 0, 0)),
                  sp["lam"], sp["lam"], sp["lam"], sp["lam"], sp["w_in"], sp["w_out"], sp["vec"]],
        out_specs=[sp["rows"], mat, mat, pl.BlockSpec((None, 1, 2 * SSM_ST), lambda g: (g, 0, 0)), sp["vec"]],
        out_shape=[jax.ShapeDtypeStruct((t, SSM_W), BF16), jax.ShapeDtypeStruct((SSM_GB, SSM_CH, 2 * SSM_P), F32),
                   jax.ShapeDtypeStruct((SSM_GB, SSM_CH, 2 * SSM_P), F32),
                   jax.ShapeDtypeStruct((SSM_GB, 1, 2 * SSM_ST), F32), jax.ShapeDtypeStruct((1, SSM_W), F32)],
        scratch_shapes=[pltpu.VMEM((SCAN_ROWS, 2 * SSM_ST), F32), pltpu.VMEM((SCAN_ROWS + N_SEG, 2 * SSM_ST), F32),
                        pltpu.VMEM((SCAN_ROWS, 2 * SSM_ST), F32), pltpu.VMEM((N_SEG, 2 * SSM_ST), F32),
                        pltpu.VMEM((N_SEG, 2 * SSM_ST), F32), pltpu.VMEM((N_SEG, 2 * SSM_ST), F32),
                        pltpu.VMEM((SSM_CH, 2 * SSM_ST), F32), pltpu.VMEM((SSM_CH, 2 * SSM_ST), F32)],
        compiler_params=_params(("parallel",)),
    )(up, dyp, hc, lam_r, lam_i, pw_r, pw_i, w_in, w_out, d_skip)


def _glu_fwd(y, uz, w_glu, b_glu):
    t = y.shape[0]
    tm = 512

    def body(y_ref, z_ref, w_ref, b_ref, o_ref, yg_ref):
        yg = _gelu(y_ref[...])
        yg_b = yg.astype(BF16)
        a = _dot(yg_b, w_ref[...]) + b_ref[...]
        o_ref[...] = yg * _sigmoid(a) * _silu(z_ref[...])
        yg_ref[...] = yg_b

    row = pl.BlockSpec((tm, SSM_W), lambda i: (i, 0))
    zcol = pl.BlockSpec((tm, SSM_W), lambda i: (i, 1))
    return pl.pallas_call(
        body, name="glu_fwd", grid=(t // tm,),
        in_specs=[row, zcol, pl.BlockSpec((SSM_W, SSM_W), lambda i: (0, 0)), pl.BlockSpec((1, SSM_W), lambda i: (0, 0))],
        out_specs=[row, row],
        out_shape=[jax.ShapeDtypeStruct((t, SSM_W), F32), jax.ShapeDtypeStruct((t, SSM_W), BF16)],
        compiler_params=_params(("parallel",)),
    )(y, uz, w_glu, b_glu)


def _glu_bwd(y, uz, dos, w_glu, b_glu):
    t = y.shape[0]
    tm = 512

    def body(y_ref, z_ref, do_ref, w_ref, b_ref, dy_ref, dz_ref, da_ref, gb_ref):
        @pl.when(pl.program_id(0) == 0)
        def _():
            gb_ref[...] = jnp.zeros_like(gb_ref)

        yv, z, do = y_ref[...], z_ref[...], do_ref[...]
        yg = _gelu(yv)
        sg = _sigmoid(_dot(yg.astype(BF16), w_ref[...]) + b_ref[...])
        dy2 = do * _silu(z)
        dz_ref[...] = (do * yg * sg * _dsilu(z)).astype(BF16)
        da = dy2 * yg * sg * (1.0 - sg)
        da_b = da.astype(BF16)
        da_ref[...] = da_b
        gb_ref[...] += jnp.sum(da, axis=0, keepdims=True)
        dyg = dy2 * sg + _dot_nt(da_b, w_ref[...])
        dy_ref[...] = dyg * _dgelu(yv)

    row = pl.BlockSpec((tm, SSM_W), lambda i: (i, 0))
    zcol = pl.BlockSpec((tm, SSM_W), lambda i: (i, 1))
    vec = pl.BlockSpec((1, SSM_W), lambda i: (0, 0))
    return pl.pallas_call(
        body, name="glu_bwd", grid=(t // tm,),
        in_specs=[row, zcol, row, pl.BlockSpec((SSM_W, SSM_W), lambda i: (0, 0)), vec],
        out_specs=[row, row, row, vec],
        out_shape=[jax.ShapeDtypeStruct((t, SSM_W), F32), jax.ShapeDtypeStruct((t, SSM_W), BF16),
                   jax.ShapeDtypeStruct((t, SSM_W), BF16), jax.ShapeDtypeStruct((1, SSM_W), F32)],
        compiler_params=_params(("arbitrary",)),
    )(y, uz, dos, w_glu, b_glu)


def _rms(o):
    return lax.rsqrt(jnp.mean(o * o, axis=1, keepdims=True) + NORM_EPS)


def _outproj(oa, os_, aw, sw, w_out, x, target):
    t = x.shape[0]
    tm = 256

    def body(oa_ref, os_ref, aw_ref, sw_ref, w_ref, x_ref, t_ref, mg_ref, do_ref, ls_ref):
        @pl.when(pl.program_id(0) == 0)
        def _():
            ls_ref[...] = jnp.zeros_like(ls_ref)

        a, s = oa_ref[...], os_ref[...]
        merged = jnp.concatenate([a * _rms(a) * aw_ref[...], s * _rms(s) * sw_ref[...]], axis=1).astype(BF16)
        mg_ref[...] = merged
        err = x_ref[...] + _dot(merged, w_ref[...]) - t_ref[...]
        do_ref[...] = err * (1.0 / D_MODEL)
        ls_ref[...] += jnp.sum(err * err)

    half = pl.BlockSpec((tm, ATTN_W), lambda i: (i, 0))
    full = pl.BlockSpec((tm, D_MODEL), lambda i: (i, 0))
    vec = pl.BlockSpec((1, ATTN_W), lambda i: (0, 0))
    return pl.pallas_call(
        body, name="outproj", grid=(t // tm,),
        in_specs=[half, half, vec, vec, pl.BlockSpec((D_MODEL, D_MODEL), lambda i: (0, 0)), full, full],
        out_specs=[full, full, pl.BlockSpec((8, 128), lambda i: (0, 0))],
        out_shape=[jax.ShapeDtypeStruct((t, D_MODEL), BF16), jax.ShapeDtypeStruct((t, D_MODEL), F32),
                   jax.ShapeDtypeStruct((8, 128), F32)],
        compiler_params=_params(("arbitrary",)),
    )(oa, os_, aw, sw, w_out, x, target)


def _outproj_bwd(dout, oa, os_, aw, sw, w_out):
    t = dout.shape[0]
    tm = 256

    def norm_bwd(o, w, dm):
        r = _rms(o)
        yh = o * r
        gh = dm * w
        return r * (gh - yh * jnp.mean(gh * yh, axis=1, keepdims=True)), jnp.sum(dm * yh, axis=0, keepdims=True)

    def body(do_ref, oa_ref, os_ref, aw_ref, sw_ref, w_ref, da_ref, ds_ref, ga_ref, gs_ref):
        @pl.when(pl.program_id(0) == 0)
        def _():
            ga_ref[...] = jnp.zeros_like(ga_ref)
            gs_ref[...] = jnp.zeros_like(gs_ref)

        dm = _dot_nt(do_ref[...].astype(BF16), w_ref[...])
        da, ga = norm_bwd(oa_ref[...], aw_ref[...], dm[:, :ATTN_W])
        ds, gs = norm_bwd(os_ref[...], sw_ref[...], dm[:, ATTN_W:])
        da_ref[...] = da
        ds_ref[...] = ds
        ga_ref[...] += ga
        gs_ref[...] += gs

    half = pl.BlockSpec((tm, ATTN_W), lambda i: (i, 0))
    full = pl.BlockSpec((tm, D_MODEL), lambda i: (i, 0))
    vec = pl.BlockSpec((1, ATTN_W), lambda i: (0, 0))
    return pl.pallas_call(
        body, name="outproj_bwd", grid=(t // tm,),
        in_specs=[full, half, half, vec, vec, pl.BlockSpec((D_MODEL, D_MODEL), lambda i: (0, 0))],
        out_specs=[half, half, vec, vec],
        out_shape=[jax.ShapeDtypeStruct((t, ATTN_W), F32), jax.ShapeDtypeStruct((t, ATTN_W), F32),
                   jax.ShapeDtypeStruct((1, ATTN_W), F32), jax.ShapeDtypeStruct((1, ATTN_W), F32)],
        compiler_params=_params(("arbitrary",)),
    )(dout, oa, os_, aw, sw, w_out)


def _inproj_bwd(dproj, w_slabs, x, norm_w, dout, outgoing):
    t = x.shape[0]
    tm = 512
    nc = 4
    nt = len(outgoing)
    ni = t // tm

    def body(dp_ref, w_ref, x_ref, nw_ref, do_ref, *rest):
        src, (gx_ref, gw_ref), dst = rest[:nt], rest[nt:nt + 2], rest[nt + 2:2 * nt + 2]
        acc_ref, ssem, rsem = rest[2 * nt + 2:]
        i, j = pl.program_id(0), pl.program_id(1)
        start, wait = _bg_scatter_chips(src, dst, ssem, rsem)

        @pl.when((i == 0) & (j == 0))
        def _():
            start()
            gw_ref[...] = jnp.zeros_like(gw_ref)

        @pl.when(j == 0)
        def _():
            acc_ref[...] = jnp.zeros_like(acc_ref)

        acc_ref[...] += _dot_nt(dp_ref[...], w_ref[...])

        @pl.when(j == nc - 1)
        def _():
            xv = x_ref[...]
            r = lax.rsqrt(jnp.mean(xv * xv, axis=1, keepdims=True) + NORM_EPS)
            yh = xv * r
            dh = acc_ref[...]
            gh = dh * nw_ref[...]
            gx_ref[...] = do_ref[...] + r * (gh - yh * jnp.mean(gh * yh, axis=1, keepdims=True))
            gw_ref[...] += jnp.sum(dh * yh, axis=0, keepdims=True)

        @pl.when((i == ni - 1) & (j == nc - 1))
        def _():
            wait()

    full = pl.BlockSpec((tm, D_MODEL), lambda i, j: (i, 0))
    vec = pl.BlockSpec((1, D_MODEL), lambda i, j: (0, 0))
    return pl.pallas_call(
        body, name="inproj_bwd", grid=(ni, nc),
        in_specs=[pl.BlockSpec((tm, SHARD_W), lambda i, j: (i, j)),
                  pl.BlockSpec((None, D_MODEL, SHARD_W), lambda i, j: (j, 0, 0)), full, vec, full] + [ANY] * nt,
        out_specs=[full, vec] + [ANY] * nt,
        out_shape=[jax.ShapeDtypeStruct((t, D_MODEL), F32), jax.ShapeDtypeStruct((1, D_MODEL), F32)]
        + [jax.ShapeDtypeStruct(a.shape, a.dtype) for a in outgoing],
        scratch_shapes=[pltpu.VMEM((tm, D_MODEL), F32), pltpu.SemaphoreType.DMA((3 * nt,)),
                        pltpu.SemaphoreType.DMA((3 * nt,))],
        compiler_params=_params(("arbitrary", "arbitrary")),
    )(dproj, w_slabs, x, norm_w.reshape(1, D_MODEL), dout, *outgoing)


def _adamw_math(w_ref, g_ref, m_ref, v_ref, d_ref, nm_ref, nv_ref):
    gv = g_ref[...]
    nm = ADAM_B1 * m_ref[...] + (1.0 - ADAM_B1) * gv
    nv = ADAM_B2 * v_ref[...] + (1.0 - ADAM_B2) * (gv * gv)
    m_hat = nm / (1.0 - ADAM_B1 ** ADAM_STEP)
    v_hat = nv / (1.0 - ADAM_B2 ** ADAM_STEP)
    d_ref[...] = -ADAM_LR * (m_hat / (jnp.sqrt(v_hat) + ADAM_EPS) + ADAM_WD * w_ref[...])
    nm_ref[...] = nm
    nv_ref[...] = nv


def _adamw_halves(w, mine, theirs, m, v, c_idx, *, rows, name):
    hr, cols = mine.shape
    nblk = hr // rows

    def body(c_ref, w_ref, a_ref, b_ref, m_ref, v_ref, g_ref, d_ref, nm_ref, nv_ref):
        g_ref[...] = jnp.where(pl.program_id(0) == c_ref[0], a_ref[...], b_ref[...])
        _adamw_math(w_ref, g_ref, m_ref, v_ref, d_ref, nm_ref, nv_ref)

    full = pl.BlockSpec((rows, cols), lambda h, i, c: (h * nblk + i, 0))
    part = pl.BlockSpec((rows, cols), lambda h, i, c: (i, 0))
    shp = jax.ShapeDtypeStruct((2 * hr, cols), F32)
    return pl.pallas_call(
        body, name=name,
        grid_spec=pltpu.PrefetchScalarGridSpec(num_scalar_prefetch=1, grid=(2, nblk),
                                               in_specs=[full, part, part, full, full], out_specs=[full] * 4),
        out_shape=[shp] * 4, compiler_params=_params(("parallel", "parallel")),
    )(c_idx, w, mine, theirs, m, v)


def _adamw(w, g, m, v, *, rows, name):
    r, c = w.shape

    def body(w_ref, g_ref, m_ref, v_ref, d_ref, nm_ref, nv_ref):
        _adamw_math(w_ref, g_ref, m_ref, v_ref, d_ref, nm_ref, nv_ref)

    blk = pl.BlockSpec((rows, c), lambda i: (i, 0))
    shp = jax.ShapeDtypeStruct((r, c), F32)
    return pl.pallas_call(body, name=name, grid=(r // rows,), in_specs=[blk] * 4, out_specs=[blk] * 3,
                          out_shape=[shp] * 3, compiler_params=_params(("parallel",)))(w, g, m, v)


def _remote(src, dst, ssem, rsem, dev):
    return pltpu.make_async_remote_copy(src_ref=src, dst_ref=dst, send_sem=ssem, recv_sem=rsem, device_id=dev,
                                        device_id_type=pl.DeviceIdType.MESH)


def _mesh_pos():
    return lax.axis_index("x"), lax.axis_index("y"), lax.axis_index("c")


def _other_chips(x, y):
    return [(1 - x, y), (x, 1 - y), (1 - x, 1 - y)]


def _flips():
    return [(dx, dy, dc) for dx in (0, 1) for dy in (0, 1) for dc in (0, 1) if (dx, dy, dc) != (0, 0, 0)]


def _background(sends, arrivals):
    def start():
        for cp in sends():
            cp.start()

    def wait():
        for cp in arrivals():
            cp.wait_recv()
        for cp in sends():
            cp.wait_send()

    return start, wait


def _bg_gather(sh, full, ssem, rsem):
    x, y, c = _mesh_pos()
    me = 2 * x + y
    peers = [(px, py, c) for px, py in _other_chips(x, y)] + [(x, y, 1 - c)]
    slots = [2 * px + py for px, py in _other_chips(x, y)] + [me]
    pairs = [(i, k) for i in range(len(sh)) for k in range(4)]
    return _background(
        lambda: [_remote(sh[i], full[i].at[me], ssem.at[4 * i + k], rsem.at[4 * i + k], peers[k]) for i, k in pairs],
        lambda: [_remote(full[i].at[slots[k]], full[i].at[slots[k]], ssem.at[4 * i + k], rsem.at[4 * i + k], peers[k])
                 for i, k in pairs])


def _bg_scatter_devices(src, dst, ssem, rsem):
    x, y, c = _mesh_pos()
    me = 4 * x + 2 * y + c
    peers = []
    for dx, dy, dc in _flips():
        px, py, pc = jnp.bitwise_xor(x, dx), jnp.bitwise_xor(y, dy), jnp.bitwise_xor(c, dc)
        peers.append(((px, py, pc), 4 * px + 2 * py + pc))
    pairs = [(i, k) for i in range(len(src)) for k in range(7)]
    return _background(
        lambda: [_remote(src[i].at[peers[k][1]], dst[i].at[me], ssem.at[7 * i + k], rsem.at[7 * i + k], peers[k][0])
                 for i, k in pairs],
        lambda: [_remote(dst[i].at[peers[k][1]], dst[i].at[peers[k][1]], ssem.at[7 * i + k], rsem.at[7 * i + k],
                         peers[k][0]) for i, k in pairs])


def _bg_scatter_chips(src, dst, ssem, rsem):
    x, y, c = _mesh_pos()
    me = 2 * x + y
    chips = _other_chips(x, y)
    pairs = [(i, k) for i in range(len(src)) for k in range(3)]
    slot = lambda k: 2 * chips[k][0] + chips[k][1]
    return _background(
        lambda: [_remote(src[i].at[slot(k)], dst[i].at[me], ssem.at[3 * i + k], rsem.at[3 * i + k], (*chips[k], c))
                 for i, k in pairs],
        lambda: [_remote(dst[i].at[slot(k)], dst[i].at[slot(k)], ssem.at[3 * i + k], rsem.at[3 * i + k], (*chips[k], c))
                 for i, k in pairs])


def _pair_swap(arrays):
    nt = len(arrays)

    def body(*refs):
        src, dst = refs[:nt], refs[nt:2 * nt]
        ssem, rsem = refs[2 * nt:]
        x, y, c = _mesh_pos()
        cps = [_remote(src[i].at[:, 1 - c], dst[i], ssem.at[i], rsem.at[i], (x, y, 1 - c)) for i in range(nt)]
        for cp in cps:
            cp.start()
        for cp in cps:
            cp.wait_recv()
        for cp in cps:
            cp.wait_send()

    return pl.pallas_call(
        body, name="pair_swap", in_specs=[ANY] * nt, out_specs=[ANY] * nt,
        out_shape=[jax.ShapeDtypeStruct((4,) + a.shape[2:], a.dtype) for a in arrays],
        scratch_shapes=[pltpu.SemaphoreType.DMA((nt,)), pltpu.SemaphoreType.DMA((nt,))],
    )(*arrays)


def _half_swap(arrays):
    nt = len(arrays)

    def body(*refs):
        src, dst = refs[:nt], refs[nt:2 * nt]
        ssem, rsem = refs[2 * nt:]
        x, y, c = _mesh_pos()
        cps = [_remote(src[i], dst[i], ssem.at[i], rsem.at[i], (x, y, 1 - c)) for i in range(nt)]
        for cp in cps:
            cp.start()
        for cp in cps:
            cp.wait_recv()
        for cp in cps:
            cp.wait_send()

    return pl.pallas_call(
        body, name="half_swap", in_specs=[ANY] * nt, out_specs=[ANY] * nt,
        out_shape=[jax.ShapeDtypeStruct(a.shape, a.dtype) for a in arrays],
        scratch_shapes=[pltpu.SemaphoreType.DMA((nt,)), pltpu.SemaphoreType.DMA((nt,))],
    )(*arrays)


def _exchange_slices(src, scatter, name):
    def body(src_ref, dst_ref, ssem, rsem, lsem):
        x, y, c = _mesh_pos()
        me = 4 * x + 2 * y + c
        local = pltpu.make_async_copy(src_ref.at[me] if scatter else src_ref, dst_ref.at[me], lsem)
        local.start()
        cps = []
        for k, (dx, dy, dc) in enumerate(_flips()):
            px, py, pc = jnp.bitwise_xor(x, dx), jnp.bitwise_xor(y, dy), jnp.bitwise_xor(c, dc)
            peer = 4 * px + 2 * py + pc
            cp = _remote(src_ref.at[peer] if scatter else src_ref, dst_ref.at[me], ssem.at[k], rsem.at[k],
                         (px, py, pc))
            cp.start()
            cps.append((cp, peer))
        for k, (cp, peer) in enumerate(cps):
            slot = dst_ref.at[peer]
            _remote(slot, slot, ssem.at[k], rsem.at[k], (x, y, c)).wait_recv()
        for cp, _ in cps:
            cp.wait_send()
        local.wait()

    return pl.pallas_call(
        body, name=name, in_specs=[ANY], out_specs=ANY,
        out_shape=jax.ShapeDtypeStruct((8,) + src.shape[-2:], src.dtype),
        scratch_shapes=[pltpu.SemaphoreType.DMA((7,)), pltpu.SemaphoreType.DMA((7,)), pltpu.SemaphoreType.DMA],
    )(src)


def _add_halves(g, recv, c_idx, *, rows, name):
    _, _, hr, cols = g.shape

    def body(c_ref, g_ref, r_ref, o_ref):
        o_ref[...] = (g_ref[...] + r_ref[...].astype(F32)).astype(BF16)

    return pl.pallas_call(
        body, name=name,
        grid_spec=pltpu.PrefetchScalarGridSpec(
            num_scalar_prefetch=1, grid=(4, hr // rows),
            in_specs=[pl.BlockSpec((None, None, rows, cols), lambda j, i, c: (j, c[0], i, 0)),
                      pl.BlockSpec((None, rows, cols), lambda j, i, c: (j, i, 0))],
            out_specs=pl.BlockSpec((None, rows, cols), lambda j, i, c: (j, i, 0))),
        out_shape=jax.ShapeDtypeStruct((4, hr, cols), BF16),
        compiler_params=_params(("parallel", "parallel")),
    )(c_idx, g, recv)


def _sum_peers(slots, own, idx, *, rows, name):
    n, r, cols = slots.shape

    def body(me_ref, *refs):
        me = me_ref[0]
        mine = refs[n][...].astype(F32)
        acc = None
        for k in range(n):
            term = jnp.where(me == k, mine, refs[k][...].astype(F32))
            acc = term if acc is None else acc + term
        refs[n + 1][...] = acc

    def slot_spec(k):
        return pl.BlockSpec((None, rows, cols), lambda i, me: (jnp.where(me[0] == k, (k + 1) % n, k), i, 0))

    return pl.pallas_call(
        body, name=name,
        grid_spec=pltpu.PrefetchScalarGridSpec(
            num_scalar_prefetch=1, grid=(r // rows,),
            in_specs=[slot_spec(k) for k in range(n)] + [pl.BlockSpec((None, rows, cols), lambda i, me: (me[0], i, 0))],
            out_specs=pl.BlockSpec((rows, cols), lambda i, me: (i, 0))),
        out_shape=jax.ShapeDtypeStruct((r, cols), F32),
        compiler_params=_params(("parallel",)),
    )(idx, *([slots] * n), own)


def _sum_slots(slots, *, rows, name):
    n, r, cols = slots.shape

    def body(s_ref, o_ref):
        acc = s_ref[0].astype(F32)
        for k in range(1, n):
            acc = acc + s_ref[k].astype(F32)
        o_ref[...] = acc

    return pl.pallas_call(
        body, name=name, grid=(r // rows,),
        in_specs=[pl.BlockSpec((n, rows, cols), lambda i: (0, i, 0))],
        out_specs=pl.BlockSpec((rows, cols), lambda i: (i, 0)),
        out_shape=jax.ShapeDtypeStruct((r, cols), F32),
        compiler_params=_params(("parallel",)),
    )(slots)


def _pack_small(d, names, rows):
    flat = jnp.concatenate([d[n].astype(F32).reshape(-1) for n in names])
    return jnp.pad(flat, (0, rows * 128 - flat.shape[0])).reshape(rows, 128)


def _unpack_small(p, names):
    flat = p.reshape(-1)
    out, off = {}, 0
    for n in names:
        size = math.prod(SMALL_SHAPES[n])
        out[n] = flat[off:off + size].reshape(SMALL_SHAPES[n])
        off += size
    return out


def _adamw_3d(w, g, m, v, *, name):
    def body(w_ref, g_ref, m_ref, v_ref, d_ref, nm_ref, nv_ref):
        _adamw_math(w_ref, g_ref, m_ref, v_ref, d_ref, nm_ref, nv_ref)

    blk = pl.BlockSpec((8,) + w.shape[1:], lambda i: (i, 0, 0))
    shp = jax.ShapeDtypeStruct(w.shape, F32)
    return pl.pallas_call(body, name=name, grid=(w.shape[0] // 8,), in_specs=[blk] * 4, out_specs=[blk] * 3,
                          out_shape=[shp] * 3, compiler_params=_params(("parallel",)))(w, g, m, v)


def kernel(x, positions, norm_w, w_in, q_norm_w, k_norm_w, sinks, a_re, a_im, log_step, b_re, b_im, c_re, c_im, d_skip, w_glu, b_glu, attn_out_norm_w, ssm_out_norm_w, w_out, loss_target, m_norm_w, m_w_in, m_q_norm_w, m_k_norm_w, m_sinks, m_a_re, m_a_im, m_log_step, m_b_re, m_b_im, m_c_re, m_c_im, m_d_skip, m_w_glu, m_b_glu, m_attn_out_norm_w, m_ssm_out_norm_w, m_w_out, v_norm_w, v_w_in, v_q_norm_w, v_k_norm_w, v_sinks, v_a_re, v_a_im, v_log_step, v_b_re, v_b_im, v_c_re, v_c_im, v_d_skip, v_w_glu, v_b_glu, v_attn_out_norm_w, v_ssm_out_norm_w, v_w_out):
    small_w = dict(norm_w=norm_w, q_norm_w=q_norm_w, k_norm_w=k_norm_w, sinks=sinks, a_re=a_re, a_im=a_im,
                   log_step=log_step, b_re=b_re, b_im=b_im, c_re=c_re, c_im=c_im, d_skip=d_skip, b_glu=b_glu,
                   attn_out_norm_w=attn_out_norm_w, ssm_out_norm_w=ssm_out_norm_w)
    small_m = dict(norm_w=m_norm_w, q_norm_w=m_q_norm_w, k_norm_w=m_k_norm_w, sinks=m_sinks, a_re=m_a_re, a_im=m_a_im,
                   log_step=m_log_step, b_re=m_b_re, b_im=m_b_im, c_re=m_c_re, c_im=m_c_im, d_skip=m_d_skip,
                   b_glu=m_b_glu, attn_out_norm_w=m_attn_out_norm_w, ssm_out_norm_w=m_ssm_out_norm_w)
    small_v = dict(norm_w=v_norm_w, q_norm_w=v_q_norm_w, k_norm_w=v_k_norm_w, sinks=v_sinks, a_re=v_a_re, a_im=v_a_im,
                   log_step=v_log_step, b_re=v_b_re, b_im=v_b_im, c_re=v_c_re, c_im=v_c_im, d_skip=v_d_skip,
                   b_glu=v_b_glu, attn_out_norm_w=v_attn_out_norm_w, ssm_out_norm_w=v_ssm_out_norm_w)
    c_idx = lax.axis_index("c").astype(jnp.int32).reshape(1)
    chip_idx = (2 * lax.axis_index("x") + lax.axis_index("y")).astype(jnp.int32).reshape(1)
    dev_idx = 2 * chip_idx + c_idx

    xs = x[0]
    tgt = loss_target[0]
    t = xs.shape[0]
    posf = positions[0].astype(F32).reshape(t, 1)

    mx, my = lax.axis_index("x"), lax.axis_index("y")
    slab_order = jnp.stack([2 * mx + my, 2 * (1 - mx) + my, 2 * mx + (1 - my), 2 * (1 - mx) + (1 - my)]).astype(jnp.int32)
    proj, hn, w_in_all = _inproj(xs, norm_w, w_in.astype(BF16), slab_order)
    inv_freq = ROPE_THETA ** (-jnp.arange(0, HEAD_DIM, 2, dtype=F32) / HEAD_DIM)
    invf = jnp.tile(inv_freq, 4).reshape(1, 128)
    qw = jnp.tile(q_norm_w, 2).reshape(1, 128)
    kw = jnp.tile(k_norm_w, 2).reshape(1, 128)
    sink_row = sinks.reshape(1, N_HEADS)
    oa, w_glu_all, w_out_all = _attn_fwd(proj, posf, invf, qw, kw, sink_row, [w_glu.astype(BF16), w_out.astype(BF16)])
    w_glu_b = w_glu_all.reshape(SSM_W, SSM_W)
    w_out_b = w_out_all.reshape(D_MODEL, D_MODEL)

    lam_r, lam_i, pw_r, pw_i, bb_r, bb_i = _ssm_prep(a_re, a_im, log_step, b_re, b_im, t // N_SEG)
    rows8 = lambda a: jnp.broadcast_to(a.reshape(SSM_GB, 1, SSM_ST), (SSM_GB, N_SEG, SSM_ST))
    lam_r8, lam_i8, pw_r8, pw_i8 = rows8(lam_r), rows8(lam_i), rows8(pw_r), rows8(pw_i)
    ssm_w_in = jnp.concatenate([_block_diag_in(bb_r), _block_diag_in(bb_i)], axis=1).astype(BF16)
    ssm_w_out = jnp.concatenate([_block_diag_out(c_re), _block_diag_out(-c_im)], axis=2).astype(BF16)
    d_row = d_skip.reshape(1, SSM_W)
    uz = _permute_rows(proj[:, 2560:])
    yp, hc = _ssm_fwd(uz, lam_r8, lam_i8, pw_r8, pw_i8, ssm_w_in, ssm_w_out, d_row)
    b_glu_row = b_glu.reshape(1, SSM_W)
    osp, ygp = _glu_fwd(yp, uz, w_glu_b, b_glu_row)
    os_ = _unpermute_rows(osp)
    aw = attn_out_norm_w.reshape(1, ATTN_W)
    sw = ssm_out_norm_w.reshape(1, SSM_W)
    merged, dout, sq_err = _outproj(oa, os_, aw, sw, w_out_b, xs, tgt)
    loss = lax.psum(0.5 * sq_err[0, 0] / D_MODEL, MESH_AXES)

    doa, dos, g_aw, g_sw = _outproj_bwd(dout, oa, os_, aw, sw, w_out_b)
    dout_b = dout.astype(BF16)
    (g_w_out_b,) = _matmul_tn(merged, dout_b, tm=512, tn=1024, name="grad_w_out", dtypes=(BF16,))
    dyp, dzsp, dap, g_b_glu = _glu_bwd(yp, uz, _permute_rows(dos), w_glu_b, b_glu_row)
    (g_w_glu_b,) = _matmul_tn(ygp, dap, tm=512, tn=1024, name="grad_w_glu", dtypes=(BF16,))
    dup, g_wi, g_wo, g_lam, g_d = _ssm_bwd(uz, dyp, hc, lam_r8, lam_i8, pw_r8, pw_i8, ssm_w_in, ssm_w_out, d_row)
    early = [g_w_glu_b.reshape(8, 128, SSM_W), g_w_out_b.reshape(8, 256, D_MODEL)]
    dq, dk, dv, dza, g_qw, g_kw, g_sink, *early_slots = _attn_bwd(proj, posf, invf, qw, kw, sink_row, doa, early)
    duz = _unpermute_rows(jnp.concatenate([dup, dzsp], axis=1))
    dproj = jnp.concatenate([dq, dk, dv, dza, duz], axis=1)
    g_w_in, g_w_in_b = _matmul_tn(hn, dproj, tm=512, tn=SHARD_W, name="grad_w_in", slabs=True)
    in_shape = (4, 2, D_MODEL // 2, SHARD_W)
    (from_sib,) = _pair_swap([g_w_in_b.reshape(in_shape)])
    pair_in = _add_halves(g_w_in.reshape(in_shape), from_sib, c_idx, rows=128, name="pair_sum")
    grad_x, g_nw, in_slots = _inproj_bwd(dproj, w_in_all, xs, norm_w, dout, [pair_in])

    g_wi = g_wi.reshape(SSM_G, SSM_H, 2 * SSM_P)
    g_wo = g_wo.reshape(SSM_G, SSM_H, 2 * SSM_P)
    g_bb_r = g_wi[:, :, :SSM_P].transpose(0, 2, 1).reshape(SSM_G, SSM_P * SSM_H)
    g_bb_i = g_wi[:, :, SSM_P:].transpose(0, 2, 1).reshape(SSM_G, SSM_P * SSM_H)
    g_a_re, g_a_im, g_ls, g_b_re, g_b_im = _ssm_param_grads(
        a_re, a_im, log_step, b_re, b_im, g_lam[:, 0, :SSM_ST].reshape(SSM_G, SSM_P),
        g_lam[:, 0, SSM_ST:].reshape(SSM_G, SSM_P), g_bb_r, g_bb_i)
    small_g = dict(
        norm_w=g_nw, q_norm_w=g_qw[0, :64] + g_qw[0, 64:], k_norm_w=g_kw[0, :64] + g_kw[0, 64:],
        sinks=g_sink[0, :N_HEADS], a_re=g_a_re, a_im=g_a_im, log_step=g_ls, b_re=g_b_re, b_im=g_b_im,
        c_re=g_wo[:, :, :SSM_P], c_im=-g_wo[:, :, SSM_P:], d_skip=g_d,
        b_glu=g_b_glu, attn_out_norm_w=g_aw, ssm_out_norm_w=g_sw)

    mine = [_sum_peers(in_slots, pair_in, chip_idx, rows=128, name="sum_w_in"),
            _sum_peers(early_slots[0], early[0], dev_idx, rows=128, name="sum_w_glu"),
            _sum_peers(early_slots[1], early[1], dev_idx, rows=128, name="sum_w_out")]
    theirs = _half_swap(mine)
    packed = _pack_small(small_g, SMALL, 8 * PACK_ROWS).reshape(8, PACK_ROWS, 128)
    summed = _sum_slots(_exchange_slices(packed, True, "small_scatter"), rows=PACK_ROWS, name="small_sum")
    small_red = _exchange_slices(summed, False, "small_gather").reshape(8 * PACK_ROWS, 128)

    big = [_adamw_halves(w_in, mine[0], theirs[0], m_w_in, v_w_in, c_idx, rows=256, name="adamw_w_in"),
           _adamw_halves(w_glu, mine[1], theirs[1], m_w_glu, v_w_glu, c_idx, rows=128, name="adamw_w_glu"),
           _adamw_halves(w_out, mine[2], theirs[2], m_w_out, v_w_out, c_idx, rows=256, name="adamw_w_out")]
    g_in_sh, g_glu_sh, g_out_sh = (b[0] for b in big)
    upd = [b[1:] for b in big]
    grads = _unpack_small(small_red, SMALL)
    flat_first = sum(math.prod(SMALL_SHAPES[n]) for n in SMALL_3D) // 128
    sd, sm, sv = _adamw(_pack_small(small_w, SMALL_FLAT, FLAT_ROWS), small_red[flat_first:flat_first + FLAT_ROWS],
                        _pack_small(small_m, SMALL_FLAT, FLAT_ROWS), _pack_small(small_v, SMALL_FLAT, FLAT_ROWS),
                        rows=FLAT_ROWS, name="adamw_small")
    deltas, new_m, new_v = (_unpack_small(a, SMALL_FLAT) for a in (sd, sm, sv))
    for n in SMALL_3D:
        deltas[n], new_m[n], new_v[n] = _adamw_3d(small_w[n], grads[n], small_m[n], small_v[n], name="adamw_" + n)
    grads.update(w_in=g_in_sh, w_glu=g_glu_sh, w_out=g_out_sh)
    for n, (d, m_, v_) in zip(("w_in", "w_glu", "w_out"), upd):
        deltas[n], new_m[n], new_v[n] = d, m_, v_
    order = ["norm_w", "w_in", "q_norm_w", "k_norm_w", "sinks", "a_re", "a_im", "log_step", "b_re", "b_im", "c_re",
             "c_im", "d_skip", "w_glu", "b_glu", "attn_out_norm_w", "ssm_out_norm_w", "w_out"]
    return (loss, grad_x[None], *[grads[n] for n in order], *[deltas[n] for n in order],
            *[new_m[n] for n in order], *[new_v[n] for n in order])
```

```python
import math

import jax
import jax.numpy as jnp
from jax import lax
from jax.experimental import pallas as pl
from jax.experimental.pallas import tpu as pltpu

F32 = jnp.float32
BF16 = jnp.bfloat16

D_MODEL = 2048
ATTN_W = 1024
SSM_W = 1024
HEAD_DIM = 64
N_HEADS = 16
N_KV_HEADS = 4
KV_W = 256
BLOCK = 128
IN_W = 4608
SHARD_W = IN_W // 4
ROPE_THETA = 10000.0
SSM_H = 16
SSM_G = 64
SSM_P = 64
NORM_EPS = 1e-6
ADAM_LR = 0.001
ADAM_B1 = 0.9
ADAM_B2 = 0.999
ADAM_EPS = 1e-08
ADAM_WD = 0.01
ADAM_STEP = 10

N_SEG = 8
SSM_GB = 4
SSM_CH = 256
SSM_ST = 1024
SCAN_ROWS = 256
SCAN_LW = 512
VMEM_LIMIT = 56 * 1024 * 1024
MESH_AXES = ("x", "y", "c")
ANY = pl.BlockSpec(memory_space=pl.ANY)

SMALL_3D = ("b_re", "b_im", "c_re", "c_im")
SMALL_FLAT = ("norm_w", "q_norm_w", "k_norm_w", "sinks", "a_re", "a_im", "log_step", "d_skip", "b_glu",
              "attn_out_norm_w", "ssm_out_norm_w")
SMALL = SMALL_3D + SMALL_FLAT
SMALL_SHAPES = {"norm_w": (2048,), "q_norm_w": (64,), "k_norm_w": (64,), "sinks": (16,), "a_re": (64, 64),
                "a_im": (64, 64), "log_step": (64,), "b_re": (64, 64, 16), "b_im": (64, 64, 16),
                "c_re": (64, 16, 64), "c_im": (64, 16, 64), "d_skip": (1024,), "b_glu": (1024,),
                "attn_out_norm_w": (1024,), "ssm_out_norm_w": (1024,)}
PACK_ROWS = 272
FLAT_ROWS = 120


def _params(sem=None):
    return pltpu.CompilerParams(dimension_semantics=sem, vmem_limit_bytes=VMEM_LIMIT)


def _dot(a, b):
    return jnp.dot(a, b, preferred_element_type=F32)


def _dot_nt(a, b):
    return lax.dot_general(a, b, (((1,), (1,)), ((), ())), preferred_element_type=F32)


def _dot_tn(a, b):
    return lax.dot_general(a, b, (((0,), (0,)), ((), ())), preferred_element_type=F32)


def _sigmoid(x):
    return 1.0 / (1.0 + jnp.exp(-x))


def _silu(x):
    return x * _sigmoid(x)


def _dsilu(x):
    s = _sigmoid(x)
    return s * (1.0 + x * (1.0 - s))


_GELU_C = math.sqrt(2.0 / math.pi)


def _gelu(x):
    return 0.5 * x * (1.0 + jnp.tanh(_GELU_C * (x + 0.044715 * x * x * x)))


def _dgelu(x):
    t = jnp.tanh(_GELU_C * (x + 0.044715 * x * x * x))
    return 0.5 * (1.0 + t) + 0.5 * x * (1.0 - t * t) * _GELU_C * (1.0 + 3.0 * 0.044715 * x * x)


def _matmul_tn(a, b, *, tm, tn, name, slabs=False, dtypes=(F32, BF16)):
    k, m = a.shape
    _, n = b.shape

    def body(a_ref, b_ref, *o_refs):
        acc = _dot_tn(a_ref[...], b_ref[...])
        for o_ref in o_refs:
            o_ref[...] = acc.astype(o_ref.dtype)

    if slabs:
        out_spec = pl.BlockSpec((None, tm, tn), lambda j, i: (j, i, 0))
        shape = (n // tn, m, tn)
    else:
        out_spec = pl.BlockSpec((tm, tn), lambda j, i: (i, j))
        shape = (m, n)
    return pl.pallas_call(
        body, name=name, grid=(n // tn, m // tm),
        in_specs=[pl.BlockSpec((k, tm), lambda j, i: (0, i)), pl.BlockSpec((k, tn), lambda j, i: (0, j))],
        out_specs=[out_spec] * len(dtypes),
        out_shape=[jax.ShapeDtypeStruct(shape, d) for d in dtypes],
        compiler_params=_params(("parallel", "parallel")),
    )(a, b)


def _inproj(x, norm_w, w_sh, order):
    t = x.shape[0]
    tm = 512
    ni = t // tm
    hr = D_MODEL // 2

    def body(ord_ref, x_ref, nw_ref, sh_ref, proj_ref, hn_ref, full_ref, wbuf, hn_s, ssem, rsem, lsem):
        s, i = pl.program_id(0), pl.program_id(1)
        mx, my, c = _mesh_pos()
        me = 2 * mx + my
        sib = (mx, my, 1 - c)
        chips = _other_chips(mx, my)

        def half(which):
            return pl.ds(pl.multiple_of(which * hr, 8), hr)

        def slot(k):
            return 2 * chips[k][0] + chips[k][1]

        def ici(k):
            return _remote(sh_ref.at[half(c)], full_ref.at[me, half(c)], ssem.at[k], rsem.at[k], (*chips[k], c))

        def own():
            return _remote(sh_ref, full_ref.at[me], ssem.at[6], rsem.at[6], sib)

        def landed(k, which, sem):
            ref = full_ref.at[slot(k), half(which)]
            return _remote(ref, ref, ssem.at[sem], rsem.at[sem], sib)

        def fetch(src, b):
            return pltpu.make_async_copy(src, wbuf.at[b], lsem.at[b])

        @pl.when((s == 0) & (i == 0))
        def _():
            for k in range(3):
                ici(k).start()
            own().start()
            cp = fetch(sh_ref, 0)
            cp.start()
            cp.wait()

        for k in range(3):
            @pl.when((s == k) & (i == max(ni - 2, 0)))
            def _(k=k):
                landed(k, c, k).wait_recv()
                landed(k, c, 3 + k).start()
                landed(k, 1 - c, 3 + k).wait_recv()
                fetch(full_ref.at[slot(k)], (k + 1) % 2).start()

            @pl.when((s == k + 1) & (i == 0))
            def _(k=k):
                fetch(full_ref.at[slot(k)], (k + 1) % 2).wait()

        xv = x_ref[...]
        r = lax.rsqrt(jnp.mean(xv * xv, axis=1, keepdims=True) + NORM_EPS)
        hn = (xv * r * nw_ref[...]).astype(BF16)
        proj_ref[...] = _dot(hn, wbuf[s % 2])

        def hn_out(tile):
            return pltpu.make_async_copy(hn_s, hn_ref.at[pl.ds(pl.multiple_of(tile * tm, tm), tm), :], lsem.at[2])

        @pl.when(((s == 0) & (i > 0)) | ((s == 1) & (i == 0)))
        def _():
            hn_out(jnp.where(s == 0, i - 1, ni - 1)).wait()

        @pl.when(s == 0)
        def _():
            hn_s[...] = hn
            hn_out(i).start()

        @pl.when((s == 3) & (i == ni - 1))
        def _():
            mine = full_ref.at[me]
            _remote(mine, mine, ssem.at[6], rsem.at[6], sib).wait_recv()
            for k in range(3):
                ici(k).wait_send()
                landed(k, c, 3 + k).wait_send()
            own().wait_send()

    return pl.pallas_call(
        body, name="inproj",
        grid_spec=pltpu.PrefetchScalarGridSpec(
            num_scalar_prefetch=1, grid=(4, ni),
            in_specs=[pl.BlockSpec((tm, D_MODEL), lambda s, i, o: (i, 0)),
                      pl.BlockSpec((1, D_MODEL), lambda s, i, o: (0, 0)), ANY],
            out_specs=[pl.BlockSpec((tm, SHARD_W), lambda s, i, o: (i, o[s])), ANY, ANY],
            scratch_shapes=[pltpu.VMEM((2, D_MODEL, SHARD_W), BF16), pltpu.VMEM((tm, D_MODEL), BF16),
                            pltpu.SemaphoreType.DMA((7,)), pltpu.SemaphoreType.DMA((7,)),
                            pltpu.SemaphoreType.DMA((3,))]),
        out_shape=[jax.ShapeDtypeStruct((t, IN_W), F32), jax.ShapeDtypeStruct((t, D_MODEL), BF16),
                   jax.ShapeDtypeStruct((4, D_MODEL, SHARD_W), BF16)],
        compiler_params=_params(("arbitrary", "arbitrary")),
    )(order, x, norm_w.reshape(1, D_MODEL), w_sh)


def _lane128():
    return lax.broadcasted_iota(jnp.int32, (1, 128), 1)


def _head_sums(v):
    lo = _lane128() < 64
    s_lo = jnp.sum(jnp.where(lo, v, 0.0), axis=1, keepdims=True)
    s_hi = jnp.sum(jnp.where(lo, 0.0, v), axis=1, keepdims=True)
    return jnp.where(lo, s_lo, s_hi)


def _rot_half(t):
    first = (_lane128() % 64) < 32
    return jnp.where(first, -pltpu.roll(t, 96, 1), pltpu.roll(t, 32, 1))


def _head_rstd(t):
    return lax.rsqrt(_head_sums(t * t) * (1.0 / HEAD_DIM) + NORM_EPS)


def _prep_tile(t, w, cos, sin, r=None):
    r = _head_rstd(t) if r is None else r
    tn = t * r * w
    return tn * cos + _rot_half(tn) * sin


def _prep_tile_bwd(t, w, cos, sin, g, r=None):
    r = _head_rstd(t) if r is None else r
    d_tn = g * cos - _rot_half(g * sin)
    th = t * r
    dw = jnp.sum(d_tn * th, axis=0, keepdims=True)
    gh = d_tn * w
    m = _head_sums(gh * th) * (1.0 / HEAD_DIM)
    return r * (gh - th * m), dw


def _band_mask(n):
    qi = lax.broadcasted_iota(jnp.int32, (BLOCK, 2 * BLOCK), 0) + BLOCK
    ki = lax.broadcasted_iota(jnp.int32, (BLOCK, 2 * BLOCK), 1)
    rel = qi - ki
    return (rel >= 0) & (rel < BLOCK) & ((n > 0) | (ki >= BLOCK))


def _half_select(tile, half):
    lo = _lane128() < 64
    return jnp.where(lo if half == 0 else jnp.logical_not(lo), tile, 0.0)


def _stack_group(tiles, kv_half):
    rows = []
    for t in tiles:
        for half in range(2):
            piece = _half_select(t, half)
            rows.append(piece if half == kv_half else pltpu.roll(piece, 64, 1))
    return jnp.concatenate(rows, axis=0)


def _unstack_group(stacked, kv_half):
    tiles = []
    for i in range(2):
        acc = None
        for half in range(2):
            piece = _half_select(stacked[BLOCK * (2 * i + half):BLOCK * (2 * i + half + 1)], kv_half)
            piece = piece if half == kv_half else pltpu.roll(piece, 64, 1)
            acc = piece if acc is None else acc + piece
        tiles.append(acc)
    return tiles


def _attn_specs(nb):
    last = nb - 1
    qi = lambda n: (jnp.minimum(n, last), 0)
    prev = lambda n: jnp.maximum(n - 1, 0)
    cur = lambda n: jnp.minimum(n, last)
    specs = [
        pl.BlockSpec((BLOCK, ATTN_W), qi),
        pl.BlockSpec((BLOCK, KV_W), lambda n: (cur(n), 4)),
        pl.BlockSpec((BLOCK, KV_W), lambda n: (prev(n), 4)),
        pl.BlockSpec((BLOCK, KV_W), lambda n: (cur(n), 5)),
        pl.BlockSpec((BLOCK, KV_W), lambda n: (prev(n), 5)),
        pl.BlockSpec((BLOCK, 512), lambda n: (cur(n), 3)),
        pl.BlockSpec((BLOCK, 512), lambda n: (cur(n), 4)),
        pl.BlockSpec((BLOCK, 1), lambda n: (cur(n), 0)),
        pl.BlockSpec((BLOCK, 1), lambda n: (prev(n), 0)),
        pl.BlockSpec((1, 128), lambda n: (0, 0)),
        pl.BlockSpec((1, 128), lambda n: (0, 0)),
        pl.BlockSpec((1, 128), lambda n: (0, 0)),
        pl.BlockSpec((1, N_HEADS), lambda n: (0, 0)),
    ]
    return specs


def _attn_common(n, q_ref, kc_ref, kp_ref, vc_ref, vp_ref, pq_ref, pp_ref, invf_ref, qw_ref, kw_ref):
    invf = invf_ref[...]
    ang_q = pq_ref[...] * invf
    ang_p = pp_ref[...] * invf
    cos_q, sin_q = jnp.cos(ang_q), jnp.sin(ang_q)
    cos_k = jnp.concatenate([jnp.cos(ang_p), cos_q], axis=0)
    sin_k = jnp.concatenate([jnp.sin(ang_p), sin_q], axis=0)
    k_raw = jnp.concatenate([kp_ref[...], kc_ref[...]], axis=0)
    vv = jnp.concatenate([vp_ref[...], vc_ref[...]], axis=0).astype(BF16)
    kk = [_prep_tile(k_raw[:, 128 * i:128 * i + 128], kw_ref[...], cos_k, sin_k).astype(BF16) for i in range(2)]
    vt = [vv[:, 128 * i:128 * i + 128] for i in range(2)]
    qv = q_ref[...]
    qr = [_head_rstd(qv[:, 128 * i:128 * i + 128]) for i in range(8)]
    qt = [_prep_tile(qv[:, 128 * i:128 * i + 128], qw_ref[...], cos_q, sin_q, qr[i]) for i in range(8)]
    return cos_q, sin_q, qr, kk, vt, qt


QK_SCALE = 1.0 / math.sqrt(HEAD_DIM)


def _group_sinks(sink_ref, g):
    return jnp.concatenate([jnp.broadcast_to(sink_ref[:, 4 * g + j:4 * g + j + 1], (BLOCK, 1)) for j in range(4)], axis=0)


def _group_softmax(q4, kk_t, sink, bias):
    s = _dot_nt(q4, kk_t) + bias
    m = jnp.maximum(jnp.max(s, axis=1, keepdims=True), sink)
    p = jnp.exp(s - m)
    es = jnp.exp(sink - m)
    inv = 1.0 / (jnp.sum(p, axis=1, keepdims=True) + es)
    return p * inv, es * inv


def _group_bias(n):
    return jnp.concatenate([jnp.where(_band_mask(n), 0.0, -1e30)] * 4, axis=0)


def _attn_fwd(proj, posf, invf, qw, kw, sinks, later_shards):
    t = proj.shape[0]
    nb = t // BLOCK
    nt = len(later_shards)

    def body(q_ref, kc_ref, kp_ref, vc_ref, vp_ref, za0_ref, za1_ref, pq_ref, pp_ref, invf_ref, qw_ref, kw_ref,
             sink_ref, *rest):
        sh, o_ref, full = rest[:nt], rest[nt], rest[nt + 1:2 * nt + 1]
        ssem, rsem = rest[2 * nt + 1:]
        n = pl.program_id(0)
        start, wait = _bg_gather(sh, full, ssem, rsem)

        @pl.when(n == 0)
        def _():
            start()

        _, _, _, kk, vt, qt = _attn_common(n, q_ref, kc_ref, kp_ref, vc_ref, vp_ref, pq_ref, pp_ref, invf_ref,
                                           qw_ref, kw_ref)
        bias = _group_bias(n)
        tiles = []
        for g in range(N_KV_HEADS):
            q4 = (_stack_group(qt[2 * g:2 * g + 2], g % 2) * QK_SCALE).astype(BF16)
            p, _ = _group_softmax(q4, kk[g // 2], _group_sinks(sink_ref, g), bias)
            tiles += _unstack_group(_dot(p.astype(BF16), vt[g // 2]), g % 2)
        za = jnp.concatenate([za0_ref[...], za1_ref[...]], axis=1)
        o_ref[...] = jnp.concatenate(tiles, axis=1) * _silu(za)

        @pl.when(n == nb - 1)
        def _():
            wait()

    return pl.pallas_call(
        body, name="attn_fwd", grid=(nb,), in_specs=_attn_specs(nb) + [ANY] * nt,
        out_specs=[pl.BlockSpec((BLOCK, ATTN_W), lambda n: (n, 0))] + [ANY] * nt,
        out_shape=[jax.ShapeDtypeStruct((t, ATTN_W), F32)]
        + [jax.ShapeDtypeStruct((4,) + s.shape, s.dtype) for s in later_shards],
        scratch_shapes=[pltpu.SemaphoreType.DMA((4 * nt,)), pltpu.SemaphoreType.DMA((4 * nt,))],
        compiler_params=_params(("arbitrary",)),
    )(proj, proj, proj, proj, proj, proj, proj, posf, posf, invf, qw, kw, sinks, *later_shards)


def _attn_bwd(proj, posf, invf, qw, kw, sinks, doa, duz, outgoing):
    t = proj.shape[0]
    nb = t // BLOCK
    last = nb - 1
    nt = len(outgoing)

    def body(q_ref, kc_ref, kp_ref, vc_ref, vp_ref, za0_ref, za1_ref, pq_ref, pp_ref, invf_ref, qw_ref, kw_ref,
             sink_ref, doa_ref, duz_ref, *rest):
        src = rest[:nt]
        dp_ref, gq_ref, gk_ref, gs_ref = rest[nt:nt + 4]
        dst = rest[nt + 4:2 * nt + 4]
        dkk_s, dvv_s, ck_s, cv_s, dq_s, dza_s, ssem, rsem = rest[2 * nt + 4:]
        n = pl.program_id(0)
        start, wait = _bg_scatter_devices(src, dst, ssem, rsem)

        @pl.when(n == 0)
        def _():
            start()
            gq_ref[...] = jnp.zeros_like(gq_ref)
            gk_ref[...] = jnp.zeros_like(gk_ref)
            gs_ref[...] = jnp.zeros_like(gs_ref)
            ck_s[...] = jnp.zeros_like(ck_s)
            cv_s[...] = jnp.zeros_like(cv_s)
            dq_s[...] = jnp.zeros_like(dq_s)
            dza_s[...] = jnp.zeros_like(dza_s)

        dp_ref[:, 0:ATTN_W] = dq_s[...]
        dp_ref[:, ATTN_W + 2 * KV_W:2 * ATTN_W + 2 * KV_W] = dza_s[...]
        dp_ref[:, 2 * ATTN_W + 2 * KV_W:IN_W] = duz_ref[...]

        @pl.when(n == nb)
        def _():
            dkk_s[...] = jnp.zeros_like(dkk_s)
            dvv_s[...] = jnp.zeros_like(dvv_s)

        @pl.when(n < nb)
        def _():
            cos_q, sin_q, qr, kk, vt, qt = _attn_common(n, q_ref, kc_ref, kp_ref, vc_ref, vp_ref, pq_ref, pp_ref,
                                                        invf_ref, qw_ref, kw_ref)
            bias = _group_bias(n)
            za = jnp.concatenate([za0_ref[...], za1_ref[...]], axis=1)
            doa_v = doa_ref[...]
            do_full = doa_v * _silu(za)
            o_tiles, dq_tiles = [], []
            dkk = [jnp.zeros((2 * BLOCK, 128), F32) for _ in range(2)]
            dvv = [jnp.zeros((2 * BLOCK, 128), F32) for _ in range(2)]
            gsink = jnp.zeros((1, 128), F32)
            lane = _lane128()
            for g in range(N_KV_HEADS):
                q_b = (_stack_group(qt[2 * g:2 * g + 2], g % 2) * QK_SCALE).astype(BF16)
                do_b = _stack_group([do_full[:, 128 * i:128 * i + 128] for i in (2 * g, 2 * g + 1)], g % 2).astype(BF16)
                p, psink = _group_softmax(q_b, kk[g // 2], _group_sinks(sink_ref, g), bias)
                p_b = p.astype(BF16)
                dp = _dot_nt(do_b, vt[g // 2])
                delta = jnp.sum(p * dp, axis=1, keepdims=True)
                ds_b = (p * (dp - delta)).astype(BF16)
                sd = psink * delta
                for j in range(4):
                    gsink = gsink + jnp.where(lane == 4 * g + j, -jnp.sum(sd[BLOCK * j:BLOCK * (j + 1)]), 0.0)
                o_tiles += _unstack_group(_dot(p_b, vt[g // 2]), g % 2)
                dq_tiles += [d * QK_SCALE for d in _unstack_group(_dot(ds_b, kk[g // 2]), g % 2)]
                dkk[g // 2] = dkk[g // 2] + _dot_tn(ds_b, q_b)
                dvv[g // 2] = dvv[g // 2] + _dot_tn(p_b, do_b)
            dza_s[...] = (doa_v * jnp.concatenate(o_tiles, axis=1) * _dsilu(za)).astype(BF16)
            qv = q_ref[...]
            gq = jnp.zeros((1, 128), F32)
            out = []
            for i in range(8):
                d, dw = _prep_tile_bwd(qv[:, 128 * i:128 * i + 128], qw_ref[...], cos_q, sin_q, dq_tiles[i], qr[i])
                out.append(d)
                gq = gq + dw
            dq_s[...] = jnp.concatenate(out, axis=1).astype(BF16)
            gq_ref[...] += gq
            gs_ref[...] += gsink
            dkk_s[...] = jnp.concatenate(dkk, axis=1)
            dvv_s[...] = jnp.concatenate(dvv, axis=1)

        invf = invf_ref[...]
        ang_p = pp_ref[...] * invf
        cos_p, sin_p = jnp.cos(ang_p), jnp.sin(ang_p)
        dk_prev = ck_s[...] + dkk_s[0:BLOCK, :]
        kp = kp_ref[...]
        gk = jnp.zeros((1, 128), F32)
        out = []
        for i in range(2):
            d, dw = _prep_tile_bwd(kp[:, 128 * i:128 * i + 128], kw_ref[...], cos_p, sin_p,
                                   dk_prev[:, 128 * i:128 * i + 128])
            out.append(d)
            gk = gk + dw
        dp_ref[:, ATTN_W:ATTN_W + KV_W] = jnp.concatenate(out, axis=1).astype(BF16)
        dp_ref[:, ATTN_W + KV_W:ATTN_W + 2 * KV_W] = (cv_s[...] + dvv_s[0:BLOCK, :]).astype(BF16)
        gk_ref[...] += gk
        ck_s[...] = dkk_s[BLOCK:2 * BLOCK, :]
        cv_s[...] = dvv_s[BLOCK:2 * BLOCK, :]

        @pl.when(n == nb)
        def _():
            wait()

    qblk = lambda n: (jnp.minimum(n, last), 0)
    kblk = lambda n: (jnp.maximum(n - 1, 0), 0)
    vec = pl.BlockSpec((1, 128), lambda n: (0, 0))
    return pl.pallas_call(
        body, name="attn_bwd", grid=(nb + 1,),
        in_specs=_attn_specs(nb) + [pl.BlockSpec((BLOCK, ATTN_W), qblk), pl.BlockSpec((BLOCK, 2 * SSM_W), kblk)]
        + [ANY] * nt,
        out_specs=[pl.BlockSpec((BLOCK, IN_W), kblk), vec, vec, vec] + [ANY] * nt,
        out_shape=[jax.ShapeDtypeStruct((t, IN_W), BF16), jax.ShapeDtypeStruct((1, 128), F32),
                   jax.ShapeDtypeStruct((1, 128), F32), jax.ShapeDtypeStruct((1, 128), F32)]
        + [jax.ShapeDtypeStruct(a.shape, a.dtype) for a in outgoing],
        scratch_shapes=[pltpu.VMEM((2 * BLOCK, KV_W), F32), pltpu.VMEM((2 * BLOCK, KV_W), F32),
                        pltpu.VMEM((BLOCK, KV_W), F32), pltpu.VMEM((BLOCK, KV_W), F32),
                        pltpu.VMEM((BLOCK, ATTN_W), BF16), pltpu.VMEM((BLOCK, ATTN_W), BF16),
                        pltpu.SemaphoreType.DMA((7 * nt,)), pltpu.SemaphoreType.DMA((7 * nt,))],
        compiler_params=_params(("arbitrary",)),
    )(proj, proj, proj, proj, proj, proj, proj, posf, posf, invf, qw, kw, sinks, doa, duz, *outgoing)


def _cmul(ar, ai, br, bi):
    return ar * br - ai * bi, ar * bi + ai * br


def _zoh(a_re, a_im, delta):
    e = jnp.exp(a_re * delta)
    lr, li = e * jnp.cos(a_im * delta), e * jnp.sin(a_im * delta)
    inv = 1.0 / (a_re * a_re + a_im * a_im)
    fr, fi = _cmul(lr - 1.0, li, a_re * inv, -a_im * inv)
    return lr, li, fr, fi


def _ssm_prep(a_re, a_im, log_step, b_re, b_im, seg_len):
    n_sq = int(round(math.log2(seg_len)))
    assert 2 ** n_sq == seg_len

    def body(ar_ref, ai_ref, ls_ref, arx_ref, aix_ref, br_ref, bi_ref, lr_ref, li_ref, pr_ref, pi_ref, bbr_ref, bbi_ref):
        delta = jnp.exp(ls_ref[...])
        lr, li, _, _ = _zoh(ar_ref[...], ai_ref[...], delta)
        lr_ref[...] = lr
        li_ref[...] = li
        pr, pi = lr, li
        for _ in range(n_sq):
            pr, pi = _cmul(pr, pi, pr, pi)
        pr_ref[...] = pr
        pi_ref[...] = pi
        _, _, fr, fi = _zoh(arx_ref[...], aix_ref[...], delta)
        bbr, bbi = _cmul(fr, fi, br_ref[...], bi_ref[...])
        bbr_ref[...] = bbr
        bbi_ref[...] = bbi

    gp = jax.ShapeDtypeStruct((SSM_G, SSM_P), F32)
    gx = jax.ShapeDtypeStruct((SSM_G, SSM_P * SSM_H), F32)
    return pl.pallas_call(body, name="ssm_prep", out_shape=[gp, gp, gp, gp, gx, gx])(
        a_re, a_im, log_step.reshape(SSM_G, 1), jnp.repeat(a_re, SSM_H, axis=1), jnp.repeat(a_im, SSM_H, axis=1),
        b_re.reshape(SSM_G, SSM_P * SSM_H), b_im.reshape(SSM_G, SSM_P * SSM_H))


def _ssm_param_grads(a_re, a_im, log_step, b_re, b_im, dlam_re, dlam_im, dbb_re, dbb_im):
    def body(ar_ref, ai_ref, ls_ref, arx_ref, aix_ref, br_ref, bi_ref, dlr_ref, dli_ref, dbr_ref, dbi_ref,
             gar_ref, gai_ref, gls_ref, gbr_ref, gbi_ref):
        delta = jnp.exp(ls_ref[...])
        ar, ai = ar_ref[...], ai_ref[...]
        lr, li, fr, fi = _zoh(ar, ai, delta)
        _, _, frx, fix = _zoh(arx_ref[...], aix_ref[...], delta)
        dbr, dbi = dbr_ref[...], dbi_ref[...]
        br, bi = br_ref[...], bi_ref[...]
        gbr, gbi = _cmul(frx, -fix, dbr, dbi)
        gbr_ref[...] = gbr
        gbi_ref[...] = gbi
        tr, ti = _cmul(br, -bi, dbr, dbi)
        row = lax.broadcasted_iota(jnp.int32, (SSM_P * SSM_H, SSM_P), 0)
        col = lax.broadcasted_iota(jnp.int32, (SSM_P * SSM_H, SSM_P), 1)
        fold = (row // SSM_H == col).astype(F32)
        dfr = jnp.dot(tr, fold, precision=lax.Precision.HIGHEST, preferred_element_type=F32)
        dfi = jnp.dot(ti, fold, precision=lax.Precision.HIGHEST, preferred_element_type=F32)
        inv = 1.0 / (ar * ar + ai * ai)
        ilr, ili = ar * inv, -ai * inv
        t1r, t1i = _cmul(dfr, dfi, ilr, -ili)
        dlbr, dlbi = dlr_ref[...] + t1r, dli_ref[...] + t1i
        qr, qi = _cmul(fr, fi, ilr, ili)
        t2r, t2i = _cmul(dfr, dfi, qr, -qi)
        glr, gli = -t2r, -t2i
        dzr, dzi = _cmul(dlbr, dlbi, lr, -li)
        gar_ref[...] = glr + dzr * delta
        gai_ref[...] = gli + dzi * delta
        gls_ref[...] = jnp.sum(dzr * ar + dzi * ai, axis=1, keepdims=True) * delta

    gp = jax.ShapeDtypeStruct((SSM_G, SSM_P), F32)
    gx = jax.ShapeDtypeStruct((SSM_G, SSM_P * SSM_H), F32)
    return pl.pallas_call(body, name="ssm_param_grads",
                          out_shape=[gp, gp, jax.ShapeDtypeStruct((SSM_G, 1), F32), gx, gx])(
        a_re, a_im, log_step.reshape(SSM_G, 1), jnp.repeat(a_re, SSM_H, axis=1), jnp.repeat(a_im, SSM_H, axis=1),
        b_re.reshape(SSM_G, SSM_P * SSM_H), b_im.reshape(SSM_G, SSM_P * SSM_H), dlam_re, dlam_im, dbb_re, dbb_im)


def _block_diag_in(bb):
    w = jnp.tile(bb.reshape(SSM_GB, SSM_ST, SSM_H), (1, 1, 16))
    row = lax.broadcasted_iota(jnp.int32, (1, SSM_ST, SSM_CH), 1) // SSM_P
    col = lax.broadcasted_iota(jnp.int32, (1, SSM_ST, SSM_CH), 2) // SSM_H
    return jnp.where(row == col, w, 0.0)


def _block_diag_out(c):
    w = jnp.tile(c.reshape(SSM_GB, SSM_CH, SSM_P), (1, 1, 16))
    row = lax.broadcasted_iota(jnp.int32, (1, SSM_CH, SSM_ST), 1) // SSM_H
    col = lax.broadcasted_iota(jnp.int32, (1, SSM_CH, SSM_ST), 2) // SSM_P
    return jnp.where(row == col, w, 0.0)


def _permute_rows(a):
    t, c = a.shape
    return a.reshape(N_SEG, t // N_SEG, c).transpose(1, 0, 2).reshape(t, c)


def _unpermute_rows(a):
    t, c = a.shape
    return a.reshape(t // N_SEG, N_SEG, c).transpose(1, 0, 2).reshape(t, c)


def _scan_fwd(src_ref, dst_ref, lam_r_ref, lam_i_ref, init_ref, final_ref, steps):
    for k in range(SSM_ST // SCAN_LW):
        re = pl.ds(k * SCAN_LW, SCAN_LW)
        im = pl.ds(SSM_ST + k * SCAN_LW, SCAN_LW)
        lr, li = lam_r_ref[:, re], lam_i_ref[:, re]

        def step(i, carry, re=re, im=im, lr=lr, li=li):
            hr, hi = carry
            rows = pl.ds(pl.multiple_of(i * 8, 8), 8)
            nr = lr * hr - li * hi + src_ref[rows, re]
            ni = lr * hi + li * hr + src_ref[rows, im]
            if dst_ref is not None:
                dst_ref[rows, re] = nr
                dst_ref[rows, im] = ni
            return nr, ni

        hr, hi = lax.fori_loop(0, steps, step, (init_ref[:, re], init_ref[:, im]), unroll=4)
        final_ref[:, re] = hr
        final_ref[:, im] = hi


def _ssm_specs(t):
    col = lambda g: (0, g)
    gb3 = lambda g: (g, 0, 0)
    return dict(
        rows=pl.BlockSpec((t, SSM_CH), col),
        lam=pl.BlockSpec((None, N_SEG, SSM_ST), gb3),
        w_in=pl.BlockSpec((None, 2 * SSM_ST, SSM_CH), gb3),
        w_out=pl.BlockSpec((None, SSM_CH, 2 * SSM_ST), gb3),
        vec=pl.BlockSpec((1, SSM_CH), col),
    )


def _segment_states(x_ref, pw_r_ref, pw_i_ref, out_ref, reverse):
    re, im = pl.ds(0, SSM_ST), pl.ds(SSM_ST, SSM_ST)
    pr, pi = pw_r_ref[0:1, :], pw_i_ref[0:1, :]
    first = N_SEG - 1 if reverse else 0
    out_ref[first:first + 1, :] = jnp.zeros((1, 2 * SSM_ST), F32)
    order = range(N_SEG - 1, 0, -1) if reverse else range(N_SEG - 1)
    for s in order:
        d = s - 1 if reverse else s + 1
        hr, hi = out_ref[s:s + 1, re], out_ref[s:s + 1, im]
        if reverse:
            nr, ni = pr * hr + pi * hi, pr * hi - pi * hr
        else:
            nr, ni = pr * hr - pi * hi, pr * hi + pi * hr
        out_ref[d:d + 1, re] = nr + x_ref[s:s + 1, re]
        out_ref[d:d + 1, im] = ni + x_ref[s:s + 1, im]


def _ssm_fwd(up, lam_r, lam_i, pw_r, pw_i, w_in, w_out, d_skip):
    t = up.shape[0]
    nch = t // SCAN_ROWS
    steps = SCAN_ROWS // N_SEG
    sp = _ssm_specs(t)

    def body(u_ref, lr_ref, li_ref, pr_ref, pi_ref, wi_ref, wo_ref, d_ref, y_ref, hc_ref, bu_s, car_s, seg_s):
        def load_bu(j):
            rows = pl.ds(pl.multiple_of(j * SCAN_ROWS, SCAN_ROWS), SCAN_ROWS)
            bu_s[...] = _dot_nt(u_ref[rows, :].astype(BF16), wi_ref[...])

        car_s[...] = jnp.zeros_like(car_s)

        def chunk1(j, c):
            load_bu(j)
            _scan_fwd(bu_s, None, lr_ref, li_ref, car_s, car_s, steps)
            return c

        lax.fori_loop(0, nch, chunk1, 0)
        _segment_states(car_s, pr_ref, pi_ref, seg_s, reverse=False)
        car_s[...] = seg_s[...]

        def chunk2(j, c):
            load_bu(j)
            hc_ref[j] = car_s[...]
            _scan_fwd(bu_s, bu_s, lr_ref, li_ref, car_s, car_s, steps)
            rows = pl.ds(pl.multiple_of(j * SCAN_ROWS, SCAN_ROWS), SCAN_ROWS)
            y_ref[rows, :] = _dot_nt(bu_s[...].astype(BF16), wo_ref[...]) + d_ref[...] * u_ref[rows, :]
            return c

        lax.fori_loop(0, nch, chunk2, 0)

    return pl.pallas_call(
        body, name="ssm_fwd", grid=(SSM_GB,),
        in_specs=[sp["rows"], sp["lam"], sp["lam"], sp["lam"], sp["lam"], sp["w_in"], sp["w_out"], sp["vec"]],
        out_specs=[sp["rows"], pl.BlockSpec((None, nch, N_SEG, 2 * SSM_ST), lambda g: (g, 0, 0, 0))],
        out_shape=[jax.ShapeDtypeStruct((t, SSM_W), F32), jax.ShapeDtypeStruct((SSM_GB, nch, N_SEG, 2 * SSM_ST), F32)],
        scratch_shapes=[pltpu.VMEM((SCAN_ROWS, 2 * SSM_ST), F32), pltpu.VMEM((N_SEG, 2 * SSM_ST), F32),
                        pltpu.VMEM((N_SEG, 2 * SSM_ST), F32)],
        compiler_params=_params(("parallel",)),
    )(up, lam_r, lam_i, pw_r, pw_i, w_in, w_out, d_skip)


def _group_blocks(full):
    row_g = lax.broadcasted_iota(jnp.int32, (SSM_CH, SSM_ST), 0) // SSM_H
    col_g = lax.broadcasted_iota(jnp.int32, (SSM_CH, SSM_ST), 1) // SSM_P
    fold = (lax.broadcasted_iota(jnp.int32, (SSM_ST, SSM_P), 0) % SSM_P
            == lax.broadcasted_iota(jnp.int32, (SSM_ST, SSM_P), 1)).astype(F32)
    parts = [jnp.dot(jnp.where(row_g == col_g, full[:, k * SSM_ST:(k + 1) * SSM_ST], 0.0), fold,
                     precision=lax.Precision.HIGHEST, preferred_element_type=F32) for k in range(2)]
    return jnp.concatenate(parts, axis=1)


def _ssm_bwd(up, dyp, hc, lam_r, lam_i, pw_r, pw_i, w_in, w_out, d_skip):
    t = up.shape[0]
    nch = t // SCAN_ROWS
    steps = SCAN_ROWS // N_SEG
    sp = _ssm_specs(t)

    def body(u_ref, dy_ref, hc_ref, lr_ref, li_ref, pr_ref, pi_ref, wi_ref, wo_ref, d_ref,
             du_ref, gbi_ref, gbo_ref, glam_ref, gd_ref, bu_s, h_s, e_s, car_s, seg_s, acc_s, gwi_ref, gwo_ref):
        def chunk_rows(j):
            return pl.ds(pl.multiple_of(j * SCAN_ROWS, SCAN_ROWS), SCAN_ROWS)

        def load_e(j):
            e_s[...] = _dot(dy_ref[chunk_rows(j), :].astype(BF16), wo_ref[...])

        def scan_rev(j, accumulate):
            for k in range(SSM_ST // SCAN_LW):
                re = pl.ds(k * SCAN_LW, SCAN_LW)
                im = pl.ds(SSM_ST + k * SCAN_LW, SCAN_LW)
                lr, li = lr_ref[:, re], li_ref[:, re]

                def step(ii, carry, re=re, im=im, lr=lr, li=li):
                    i = steps - 1 - ii
                    rows = pl.ds(pl.multiple_of(i * 8, 8), 8)
                    if accumulate:
                        gr, gi, ar, ai = carry
                    else:
                        gr, gi = carry
                    nr = lr * gr + li * gi + e_s[rows, re]
                    ni = lr * gi - li * gr + e_s[rows, im]
                    if not accumulate:
                        return nr, ni
                    e_s[rows, re] = nr
                    e_s[rows, im] = ni
                    pr_, pi_ = h_s[rows, re], h_s[rows, im]
                    return nr, ni, ar + nr * pr_ + ni * pi_, ai + ni * pr_ - nr * pi_

                init = (car_s[:, re], car_s[:, im])
                if accumulate:
                    init = init + (acc_s[:, re], acc_s[:, im])
                out = lax.fori_loop(0, steps, step, init, unroll=4)
                car_s[:, re] = out[0]
                car_s[:, im] = out[1]
                if accumulate:
                    acc_s[:, re] = out[2]
                    acc_s[:, im] = out[3]

        car_s[...] = jnp.zeros_like(car_s)

        def pass1(jj, c):
            load_e(nch - 1 - jj)
            scan_rev(nch - 1 - jj, False)
            return c

        lax.fori_loop(0, nch, pass1, 0)
        _segment_states(car_s, pr_ref, pi_ref, seg_s, reverse=True)
        car_s[...] = seg_s[...]
        acc_s[...] = jnp.zeros_like(acc_s)
        gwi_ref[...] = jnp.zeros_like(gwi_ref)
        gwo_ref[...] = jnp.zeros_like(gwo_ref)
        gd_ref[...] = jnp.zeros_like(gd_ref)

        def pass2(jj, c):
            j = nch - 1 - jj
            rows = chunk_rows(j)
            u = u_ref[rows, :]
            dy = dy_ref[rows, :]
            u_b, dy_b = u.astype(BF16), dy.astype(BF16)
            bu_s[...] = _dot_nt(u_b, wi_ref[...])
            h_s[0:N_SEG, :] = hc_ref[j]
            seg_s[...] = hc_ref[j]
            _scan_fwd(bu_s, h_s.at[pl.ds(N_SEG, SCAN_ROWS), :], lr_ref, li_ref, seg_s, seg_s, steps)
            load_e(j)
            scan_rev(j, True)
            g_b = e_s[...].astype(BF16)
            du_ref[rows, :] = (_dot(g_b, wi_ref[...]) + d_ref[...] * dy).astype(du_ref.dtype)
            gwi_ref[...] += _dot_tn(u_b, g_b)
            gwo_ref[...] += _dot_tn(dy_b, h_s[pl.ds(N_SEG, SCAN_ROWS), :].astype(BF16))
            gd_ref[...] += jnp.sum(dy * u, axis=0, keepdims=True)
            return c

        lax.fori_loop(0, nch, pass2, 0)
        glam_ref[...] = jnp.sum(acc_s[...], axis=0, keepdims=True)
        gbi_ref[...] = _group_blocks(gwi_ref[...])
        gbo_ref[...] = _group_blocks(gwo_ref[...])

    mat = pl.BlockSpec((None, SSM_CH, 2 * SSM_P), lambda g: (g, 0, 0))
    return pl.pallas_call(
        body, name="ssm_bwd", grid=(SSM_GB,),
        in_specs=[sp["rows"], sp["rows"], pl.BlockSpec((None, nch, N_SEG, 2 * SSM_ST), lambda g: (g, 0, 0, 0)),
                  sp["lam"], sp["lam"], sp["lam"], sp["lam"], sp["w_in"], sp["w_out"], sp["vec"]],
        out_specs=[sp["rows"], mat, mat, pl.BlockSpec((None, 1, 2 * SSM_ST), lambda g: (g, 0, 0)), sp["vec"]],
        out_shape=[jax.ShapeDtypeStruct((t, SSM_W), BF16), jax.ShapeDtypeStruct((SSM_GB, SSM_CH, 2 * SSM_P), F32),
                   jax.ShapeDtypeStruct((SSM_GB, SSM_CH, 2 * SSM_P), F32),
                   jax.ShapeDtypeStruct((SSM_GB, 1, 2 * SSM_ST), F32), jax.ShapeDtypeStruct((1, SSM_W), F32)],
        scratch_shapes=[pltpu.VMEM((SCAN_ROWS, 2 * SSM_ST), F32), pltpu.VMEM((SCAN_ROWS + N_SEG, 2 * SSM_ST), F32),
                        pltpu.VMEM((SCAN_ROWS, 2 * SSM_ST), F32), pltpu.VMEM((N_SEG, 2 * SSM_ST), F32),
                        pltpu.VMEM((N_SEG, 2 * SSM_ST), F32), pltpu.VMEM((N_SEG, 2 * SSM_ST), F32),
                        pltpu.VMEM((SSM_CH, 2 * SSM_ST), F32), pltpu.VMEM((SSM_CH, 2 * SSM_ST), F32)],
        compiler_params=_params(("parallel",)),
    )(up, dyp, hc, lam_r, lam_i, pw_r, pw_i, w_in, w_out, d_skip)


def _glu_fwd(y, uz, w_glu, b_glu):
    t = y.shape[0]
    tm = 512

    def body(y_ref, z_ref, w_ref, b_ref, o_ref, yg_ref):
        yg = _gelu(y_ref[...])
        yg_b = yg.astype(BF16)
        a = _dot(yg_b, w_ref[...]) + b_ref[...]
        o_ref[...] = yg * _sigmoid(a) * _silu(z_ref[...])
        yg_ref[...] = yg_b

    row = pl.BlockSpec((tm, SSM_W), lambda i: (i, 0))
    zcol = pl.BlockSpec((tm, SSM_W), lambda i: (i, 1))
    return pl.pallas_call(
        body, name="glu_fwd", grid=(t // tm,),
        in_specs=[row, zcol, pl.BlockSpec((SSM_W, SSM_W), lambda i: (0, 0)), pl.BlockSpec((1, SSM_W), lambda i: (0, 0))],
        out_specs=[row, row],
        out_shape=[jax.ShapeDtypeStruct((t, SSM_W), F32), jax.ShapeDtypeStruct((t, SSM_W), BF16)],
        compiler_params=_params(("parallel",)),
    )(y, uz, w_glu, b_glu)


def _glu_bwd(y, uz, dos, w_glu, b_glu):
    t = y.shape[0]
    tm = 512

    def body(y_ref, z_ref, do_ref, w_ref, b_ref, dy_ref, dz_ref, da_ref, gb_ref):
        @pl.when(pl.program_id(0) == 0)
        def _():
            gb_ref[...] = jnp.zeros_like(gb_ref)

        yv, z, do = y_ref[...], z_ref[...], do_ref[...]
        yg = _gelu(yv)
        sg = _sigmoid(_dot(yg.astype(BF16), w_ref[...]) + b_ref[...])
        dy2 = do * _silu(z)
        dz_ref[...] = (do * yg * sg * _dsilu(z)).astype(BF16)
        da = dy2 * yg * sg * (1.0 - sg)
        da_b = da.astype(BF16)
        da_ref[...] = da_b
        gb_ref[...] += jnp.sum(da, axis=0, keepdims=True)
        dyg = dy2 * sg + _dot_nt(da_b, w_ref[...])
        dy_ref[...] = dyg * _dgelu(yv)

    row = pl.BlockSpec((tm, SSM_W), lambda i: (i, 0))
    zcol = pl.BlockSpec((tm, SSM_W), lambda i: (i, 1))
    vec = pl.BlockSpec((1, SSM_W), lambda i: (0, 0))
    return pl.pallas_call(
        body, name="glu_bwd", grid=(t // tm,),
        in_specs=[row, zcol, row, pl.BlockSpec((SSM_W, SSM_W), lambda i: (0, 0)), vec],
        out_specs=[row, row, row, vec],
        out_shape=[jax.ShapeDtypeStruct((t, SSM_W), F32), jax.ShapeDtypeStruct((t, SSM_W), BF16),
                   jax.ShapeDtypeStruct((t, SSM_W), BF16), jax.ShapeDtypeStruct((1, SSM_W), F32)],
        compiler_params=_params(("arbitrary",)),
    )(y, uz, dos, w_glu, b_glu)


def _rms(o):
    return lax.rsqrt(jnp.mean(o * o, axis=1, keepdims=True) + NORM_EPS)


def _outproj(oa, os_, aw, sw, w_out, x, target):
    t = x.shape[0]
    tm = 256

    def body(oa_ref, os_ref, aw_ref, sw_ref, w_ref, x_ref, t_ref, mg_ref, do_ref, ls_ref):
        @pl.when(pl.program_id(0) == 0)
        def _():
            ls_ref[...] = jnp.zeros_like(ls_ref)

        a, s = oa_ref[...], os_ref[...]
        merged = jnp.concatenate([a * _rms(a) * aw_ref[...], s * _rms(s) * sw_ref[...]], axis=1).astype(BF16)
        mg_ref[...] = merged
        err = x_ref[...] + _dot(merged, w_ref[...]) - t_ref[...]
        do_ref[...] = err * (1.0 / D_MODEL)
        ls_ref[...] += jnp.sum(err * err)

    half = pl.BlockSpec((tm, ATTN_W), lambda i: (i, 0))
    full = pl.BlockSpec((tm, D_MODEL), lambda i: (i, 0))
    vec = pl.BlockSpec((1, ATTN_W), lambda i: (0, 0))
    return pl.pallas_call(
        body, name="outproj", grid=(t // tm,),
        in_specs=[half, half, vec, vec, pl.BlockSpec((D_MODEL, D_MODEL), lambda i: (0, 0)), full, full],
        out_specs=[full, full, pl.BlockSpec((8, 128), lambda i: (0, 0))],
        out_shape=[jax.ShapeDtypeStruct((t, D_MODEL), BF16), jax.ShapeDtypeStruct((t, D_MODEL), F32),
                   jax.ShapeDtypeStruct((8, 128), F32)],
        compiler_params=_params(("arbitrary",)),
    )(oa, os_, aw, sw, w_out, x, target)


def _outproj_bwd(dout, oa, os_, aw, sw, w_out):
    t = dout.shape[0]
    tm = 256

    def norm_bwd(o, w, dm):
        r = _rms(o)
        yh = o * r
        gh = dm * w
        return r * (gh - yh * jnp.mean(gh * yh, axis=1, keepdims=True)), jnp.sum(dm * yh, axis=0, keepdims=True)

    def body(do_ref, oa_ref, os_ref, aw_ref, sw_ref, w_ref, da_ref, ds_ref, ga_ref, gs_ref):
        @pl.when(pl.program_id(0) == 0)
        def _():
            ga_ref[...] = jnp.zeros_like(ga_ref)
            gs_ref[...] = jnp.zeros_like(gs_ref)

        dm = _dot_nt(do_ref[...].astype(BF16), w_ref[...])
        da, ga = norm_bwd(oa_ref[...], aw_ref[...], dm[:, :ATTN_W])
        ds, gs = norm_bwd(os_ref[...], sw_ref[...], dm[:, ATTN_W:])
        da_ref[...] = da
        ds_ref[...] = ds
        ga_ref[...] += ga
        gs_ref[...] += gs

    half = pl.BlockSpec((tm, ATTN_W), lambda i: (i, 0))
    full = pl.BlockSpec((tm, D_MODEL), lambda i: (i, 0))
    vec = pl.BlockSpec((1, ATTN_W), lambda i: (0, 0))
    return pl.pallas_call(
        body, name="outproj_bwd", grid=(t // tm,),
        in_specs=[full, half, half, vec, vec, pl.BlockSpec((D_MODEL, D_MODEL), lambda i: (0, 0))],
        out_specs=[half, half, vec, vec],
        out_shape=[jax.ShapeDtypeStruct((t, ATTN_W), F32), jax.ShapeDtypeStruct((t, ATTN_W), F32),
                   jax.ShapeDtypeStruct((1, ATTN_W), F32), jax.ShapeDtypeStruct((1, ATTN_W), F32)],
        compiler_params=_params(("arbitrary",)),
    )(dout, oa, os_, aw, sw, w_out)


def _inproj_bwd(dproj, w_slabs, x, norm_w, dout, outgoing):
    t = x.shape[0]
    tm = 512
    nc = 4
    nt = len(outgoing)
    ni = t // tm

    def body(dp_ref, w_ref, x_ref, nw_ref, do_ref, *rest):
        src, (gx_ref, gw_ref), dst = rest[:nt], rest[nt:nt + 2], rest[nt + 2:2 * nt + 2]
        acc_ref, ssem, rsem = rest[2 * nt + 2:]
        i, j = pl.program_id(0), pl.program_id(1)
        start, wait = _bg_scatter_chips(src, dst, ssem, rsem)

        @pl.when((i == 0) & (j == 0))
        def _():
            start()
            gw_ref[...] = jnp.zeros_like(gw_ref)

        @pl.when(j == 0)
        def _():
            acc_ref[...] = jnp.zeros_like(acc_ref)

        acc_ref[...] += _dot_nt(dp_ref[...], w_ref[...])

        @pl.when(j == nc - 1)
        def _():
            xv = x_ref[...]
            r = lax.rsqrt(jnp.mean(xv * xv, axis=1, keepdims=True) + NORM_EPS)
            yh = xv * r
            dh = acc_ref[...]
            gh = dh * nw_ref[...]
            gx_ref[...] = do_ref[...] + r * (gh - yh * jnp.mean(gh * yh, axis=1, keepdims=True))
            gw_ref[...] += jnp.sum(dh * yh, axis=0, keepdims=True)

        @pl.when((i == ni - 1) & (j == nc - 1))
        def _():
            wait()

    full = pl.BlockSpec((tm, D_MODEL), lambda i, j: (i, 0))
    vec = pl.BlockSpec((1, D_MODEL), lambda i, j: (0, 0))
    return pl.pallas_call(
        body, name="inproj_bwd", grid=(ni, nc),
        in_specs=[pl.BlockSpec((tm, SHARD_W), lambda i, j: (i, j)),
                  pl.BlockSpec((None, D_MODEL, SHARD_W), lambda i, j: (j, 0, 0)), full, vec, full] + [ANY] * nt,
        out_specs=[full, vec] + [ANY] * nt,
        out_shape=[jax.ShapeDtypeStruct((t, D_MODEL), F32), jax.ShapeDtypeStruct((1, D_MODEL), F32)]
        + [jax.ShapeDtypeStruct(a.shape, a.dtype) for a in outgoing],
        scratch_shapes=[pltpu.VMEM((tm, D_MODEL), F32), pltpu.SemaphoreType.DMA((3 * nt,)),
                        pltpu.SemaphoreType.DMA((3 * nt,))],
        compiler_params=_params(("arbitrary", "arbitrary")),
    )(dproj, w_slabs, x, norm_w.reshape(1, D_MODEL), dout, *outgoing)


def _adamw_math(w_ref, g_ref, m_ref, v_ref, d_ref, nm_ref, nv_ref):
    gv = g_ref[...]
    nm = ADAM_B1 * m_ref[...] + (1.0 - ADAM_B1) * gv
    nv = ADAM_B2 * v_ref[...] + (1.0 - ADAM_B2) * (gv * gv)
    m_hat = nm / (1.0 - ADAM_B1 ** ADAM_STEP)
    v_hat = nv / (1.0 - ADAM_B2 ** ADAM_STEP)
    d_ref[...] = -ADAM_LR * (m_hat / (jnp.sqrt(v_hat) + ADAM_EPS) + ADAM_WD * w_ref[...])
    nm_ref[...] = nm
    nv_ref[...] = nv


def _adamw_halves(w, mine, theirs, m, v, c_idx, *, rows, name):
    hr, cols = mine.shape
    nblk = hr // rows

    def body(c_ref, w_ref, a_ref, b_ref, m_ref, v_ref, g_ref, d_ref, nm_ref, nv_ref):
        g_ref[...] = jnp.where(pl.program_id(0) == c_ref[0], a_ref[...], b_ref[...])
        _adamw_math(w_ref, g_ref, m_ref, v_ref, d_ref, nm_ref, nv_ref)

    full = pl.BlockSpec((rows, cols), lambda h, i, c: (h * nblk + i, 0))
    part = pl.BlockSpec((rows, cols), lambda h, i, c: (i, 0))
    shp = jax.ShapeDtypeStruct((2 * hr, cols), F32)
    return pl.pallas_call(
        body, name=name,
        grid_spec=pltpu.PrefetchScalarGridSpec(num_scalar_prefetch=1, grid=(2, nblk),
                                               in_specs=[full, part, part, full, full], out_specs=[full] * 4),
        out_shape=[shp] * 4, compiler_params=_params(("parallel", "parallel")),
    )(c_idx, w, mine, theirs, m, v)


def _adamw(w, g, m, v, *, rows, name):
    r, c = w.shape

    def body(w_ref, g_ref, m_ref, v_ref, d_ref, nm_ref, nv_ref):
        _adamw_math(w_ref, g_ref, m_ref, v_ref, d_ref, nm_ref, nv_ref)

    blk = pl.BlockSpec((rows, c), lambda i: (i, 0))
    shp = jax.ShapeDtypeStruct((r, c), F32)
    return pl.pallas_call(body, name=name, grid=(r // rows,), in_specs=[blk] * 4, out_specs=[blk] * 3,
                          out_shape=[shp] * 3, compiler_params=_params(("parallel",)))(w, g, m, v)


def _remote(src, dst, ssem, rsem, dev):
    return pltpu.make_async_remote_copy(src_ref=src, dst_ref=dst, send_sem=ssem, recv_sem=rsem, device_id=dev,
                                        device_id_type=pl.DeviceIdType.MESH)


def _mesh_pos():
    return lax.axis_index("x"), lax.axis_index("y"), lax.axis_index("c")


def _other_chips(x, y):
    return [(1 - x, y), (x, 1 - y), (1 - x, 1 - y)]


def _flips():
    return [(dx, dy, dc) for dx in (0, 1) for dy in (0, 1) for dc in (0, 1) if (dx, dy, dc) != (0, 0, 0)]


def _background(sends, arrivals):
    def start():
        for cp in sends():
            cp.start()

    def wait():
        for cp in arrivals():
            cp.wait_recv()
        for cp in sends():
            cp.wait_send()

    return start, wait


def _bg_gather(sh, full, ssem, rsem):
    x, y, c = _mesh_pos()
    me = 2 * x + y
    peers = [(px, py, c) for px, py in _other_chips(x, y)] + [(x, y, 1 - c)]
    slots = [2 * px + py for px, py in _other_chips(x, y)] + [me]
    pairs = [(i, k) for i in range(len(sh)) for k in range(4)]
    return _background(
        lambda: [_remote(sh[i], full[i].at[me], ssem.at[4 * i + k], rsem.at[4 * i + k], peers[k]) for i, k in pairs],
        lambda: [_remote(full[i].at[slots[k]], full[i].at[slots[k]], ssem.at[4 * i + k], rsem.at[4 * i + k], peers[k])
                 for i, k in pairs])


def _bg_scatter_devices(src, dst, ssem, rsem):
    x, y, c = _mesh_pos()
    me = 4 * x + 2 * y + c
    peers = []
    for dx, dy, dc in _flips():
        px, py, pc = jnp.bitwise_xor(x, dx), jnp.bitwise_xor(y, dy), jnp.bitwise_xor(c, dc)
        peers.append(((px, py, pc), 4 * px + 2 * py + pc))
    pairs = [(i, k) for i in range(len(src)) for k in range(7)]
    return _background(
        lambda: [_remote(src[i].at[peers[k][1]], dst[i].at[me], ssem.at[7 * i + k], rsem.at[7 * i + k], peers[k][0])
                 for i, k in pairs],
        lambda: [_remote(dst[i].at[peers[k][1]], dst[i].at[peers[k][1]], ssem.at[7 * i + k], rsem.at[7 * i + k],
                         peers[k][0]) for i, k in pairs])


def _bg_scatter_chips(src, dst, ssem, rsem):
    x, y, c = _mesh_pos()
    me = 2 * x + y
    chips = _other_chips(x, y)
    pairs = [(i, k) for i in range(len(src)) for k in range(3)]
    slot = lambda k: 2 * chips[k][0] + chips[k][1]
    return _background(
        lambda: [_remote(src[i].at[slot(k)], dst[i].at[me], ssem.at[3 * i + k], rsem.at[3 * i + k], (*chips[k], c))
                 for i, k in pairs],
        lambda: [_remote(dst[i].at[slot(k)], dst[i].at[slot(k)], ssem.at[3 * i + k], rsem.at[3 * i + k], (*chips[k], c))
                 for i, k in pairs])


def _pair_swap(arrays):
    nt = len(arrays)

    def body(*refs):
        src, dst = refs[:nt], refs[nt:2 * nt]
        ssem, rsem = refs[2 * nt:]
        x, y, c = _mesh_pos()
        cps = [_remote(src[i].at[:, 1 - c], dst[i], ssem.at[i], rsem.at[i], (x, y, 1 - c)) for i in range(nt)]
        for cp in cps:
            cp.start()
        for cp in cps:
            cp.wait_recv()
        for cp in cps:
            cp.wait_send()

    return pl.pallas_call(
        body, name="pair_swap", in_specs=[ANY] * nt, out_specs=[ANY] * nt,
        out_shape=[jax.ShapeDtypeStruct((4,) + a.shape[2:], a.dtype) for a in arrays],
        scratch_shapes=[pltpu.SemaphoreType.DMA((nt,)), pltpu.SemaphoreType.DMA((nt,))],
    )(*arrays)


def _half_swap(arrays):
    nt = len(arrays)

    def body(*refs):
        src, dst = refs[:nt], refs[nt:2 * nt]
        ssem, rsem = refs[2 * nt:]
        x, y, c = _mesh_pos()
        cps = [_remote(src[i], dst[i], ssem.at[i], rsem.at[i], (x, y, 1 - c)) for i in range(nt)]
        for cp in cps:
            cp.start()
        for cp in cps:
            cp.wait_recv()
        for cp in cps:
            cp.wait_send()

    return pl.pallas_call(
        body, name="half_swap", in_specs=[ANY] * nt, out_specs=[ANY] * nt,
        out_shape=[jax.ShapeDtypeStruct(a.shape, a.dtype) for a in arrays],
        scratch_shapes=[pltpu.SemaphoreType.DMA((nt,)), pltpu.SemaphoreType.DMA((nt,))],
    )(*arrays)


def _exchange_slices(src, scatter, name):
    def body(src_ref, dst_ref, ssem, rsem, lsem):
        x, y, c = _mesh_pos()
        me = 4 * x + 2 * y + c
        local = pltpu.make_async_copy(src_ref.at[me] if scatter else src_ref, dst_ref.at[me], lsem)
        local.start()
        cps = []
        for k, (dx, dy, dc) in enumerate(_flips()):
            px, py, pc = jnp.bitwise_xor(x, dx), jnp.bitwise_xor(y, dy), jnp.bitwise_xor(c, dc)
            peer = 4 * px + 2 * py + pc
            cp = _remote(src_ref.at[peer] if scatter else src_ref, dst_ref.at[me], ssem.at[k], rsem.at[k],
                         (px, py, pc))
            cp.start()
            cps.append((cp, peer))
        for k, (cp, peer) in enumerate(cps):
            slot = dst_ref.at[peer]
            _remote(slot, slot, ssem.at[k], rsem.at[k], (x, y, c)).wait_recv()
        for cp, _ in cps:
            cp.wait_send()
        local.wait()

    return pl.pallas_call(
        body, name=name, in_specs=[ANY], out_specs=ANY,
        out_shape=jax.ShapeDtypeStruct((8,) + src.shape[-2:], src.dtype),
        scratch_shapes=[pltpu.SemaphoreType.DMA((7,)), pltpu.SemaphoreType.DMA((7,)), pltpu.SemaphoreType.DMA],
    )(src)


def _add_halves(g, recv, c_idx, *, rows, name):
    _, _, hr, cols = g.shape

    def body(c_ref, g_ref, r_ref, o_ref):
        o_ref[...] = (g_ref[...] + r_ref[...].astype(F32)).astype(BF16)

    return pl.pallas_call(
        body, name=name,
        grid_spec=pltpu.PrefetchScalarGridSpec(
            num_scalar_prefetch=1, grid=(4, hr // rows),
            in_specs=[pl.BlockSpec((None, None, rows, cols), lambda j, i, c: (j, c[0], i, 0)),
                      pl.BlockSpec((None, rows, cols), lambda j, i, c: (j, i, 0))],
            out_specs=pl.BlockSpec((None, rows, cols), lambda j, i, c: (j, i, 0))),
        out_shape=jax.ShapeDtypeStruct((4, hr, cols), BF16),
        compiler_params=_params(("parallel", "parallel")),
    )(c_idx, g, recv)


def _sum_peers(slots, own, idx, *, rows, name):
    n, r, cols = slots.shape

    def body(me_ref, *refs):
        me = me_ref[0]
        mine = refs[n][...].astype(F32)
        acc = None
        for k in range(n):
            term = jnp.where(me == k, mine, refs[k][...].astype(F32))
            acc = term if acc is None else acc + term
        refs[n + 1][...] = acc

    def slot_spec(k):
        return pl.BlockSpec((None, rows, cols), lambda i, me: (jnp.where(me[0] == k, (k + 1) % n, k), i, 0))

    return pl.pallas_call(
        body, name=name,
        grid_spec=pltpu.PrefetchScalarGridSpec(
            num_scalar_prefetch=1, grid=(r // rows,),
            in_specs=[slot_spec(k) for k in range(n)] + [pl.BlockSpec((None, rows, cols), lambda i, me: (me[0], i, 0))],
            out_specs=pl.BlockSpec((rows, cols), lambda i, me: (i, 0))),
        out_shape=jax.ShapeDtypeStruct((r, cols), F32),
        compiler_params=_params(("parallel",)),
    )(idx, *([slots] * n), own)


def _sum_slots(slots, *, rows, name):
    n, r, cols = slots.shape

    def body(s_ref, o_ref):
        acc = s_ref[0].astype(F32)
        for k in range(1, n):
            acc = acc + s_ref[k].astype(F32)
        o_ref[...] = acc

    return pl.pallas_call(
        body, name=name, grid=(r // rows,),
        in_specs=[pl.BlockSpec((n, rows, cols), lambda i: (0, i, 0))],
        out_specs=pl.BlockSpec((rows, cols), lambda i: (i, 0)),
        out_shape=jax.ShapeDtypeStruct((r, cols), F32),
        compiler_params=_params(("parallel",)),
    )(slots)


def _pack_small(d, names, rows):
    flat = jnp.concatenate([d[n].astype(F32).reshape(-1) for n in names])
    return jnp.pad(flat, (0, rows * 128 - flat.shape[0])).reshape(rows, 128)


def _unpack_small(p, names):
    flat = p.reshape(-1)
    out, off = {}, 0
    for n in names:
        size = math.prod(SMALL_SHAPES[n])
        out[n] = flat[off:off + size].reshape(SMALL_SHAPES[n])
        off += size
    return out


def _adamw_3d(w, g, m, v, *, name):
    def body(w_ref, g_ref, m_ref, v_ref, d_ref, nm_ref, nv_ref):
        _adamw_math(w_ref, g_ref, m_ref, v_ref, d_ref, nm_ref, nv_ref)

    blk = pl.BlockSpec((8,) + w.shape[1:], lambda i: (i, 0, 0))
    shp = jax.ShapeDtypeStruct(w.shape, F32)
    return pl.pallas_call(body, name=name, grid=(w.shape[0] // 8,), in_specs=[blk] * 4, out_specs=[blk] * 3,
                          out_shape=[shp] * 3, compiler_params=_params(("parallel",)))(w, g, m, v)


def kernel(x, positions, norm_w, w_in, q_norm_w, k_norm_w, sinks, a_re, a_im, log_step, b_re, b_im, c_re, c_im, d_skip, w_glu, b_glu, attn_out_norm_w, ssm_out_norm_w, w_out, loss_target, m_norm_w, m_w_in, m_q_norm_w, m_k_norm_w, m_sinks, m_a_re, m_a_im, m_log_step, m_b_re, m_b_im, m_c_re, m_c_im, m_d_skip, m_w_glu, m_b_glu, m_attn_out_norm_w, m_ssm_out_norm_w, m_w_out, v_norm_w, v_w_in, v_q_norm_w, v_k_norm_w, v_sinks, v_a_re, v_a_im, v_log_step, v_b_re, v_b_im, v_c_re, v_c_im, v_d_skip, v_w_glu, v_b_glu, v_attn_out_norm_w, v_ssm_out_norm_w, v_w_out):
    small_w = dict(norm_w=norm_w, q_norm_w=q_norm_w, k_norm_w=k_norm_w, sinks=sinks, a_re=a_re, a_im=a_im,
                   log_step=log_step, b_re=b_re, b_im=b_im, c_re=c_re, c_im=c_im, d_skip=d_skip, b_glu=b_glu,
                   attn_out_norm_w=attn_out_norm_w, ssm_out_norm_w=ssm_out_norm_w)
    small_m = dict(norm_w=m_norm_w, q_norm_w=m_q_norm_w, k_norm_w=m_k_norm_w, sinks=m_sinks, a_re=m_a_re, a_im=m_a_im,
                   log_step=m_log_step, b_re=m_b_re, b_im=m_b_im, c_re=m_c_re, c_im=m_c_im, d_skip=m_d_skip,
                   b_glu=m_b_glu, attn_out_norm_w=m_attn_out_norm_w, ssm_out_norm_w=m_ssm_out_norm_w)
    small_v = dict(norm_w=v_norm_w, q_norm_w=v_q_norm_w, k_norm_w=v_k_norm_w, sinks=v_sinks, a_re=v_a_re, a_im=v_a_im,
                   log_step=v_log_step, b_re=v_b_re, b_im=v_b_im, c_re=v_c_re, c_im=v_c_im, d_skip=v_d_skip,
                   b_glu=v_b_glu, attn_out_norm_w=v_attn_out_norm_w, ssm_out_norm_w=v_ssm_out_norm_w)
    c_idx = lax.axis_index("c").astype(jnp.int32).reshape(1)
    chip_idx = (2 * lax.axis_index("x") + lax.axis_index("y")).astype(jnp.int32).reshape(1)
    dev_idx = 2 * chip_idx + c_idx

    xs = x[0]
    tgt = loss_target[0]
    t = xs.shape[0]
    posf = positions[0].astype(F32).reshape(t, 1)

    mx, my = lax.axis_index("x"), lax.axis_index("y")
    slab_order = jnp.stack([2 * mx + my, 2 * (1 - mx) + my, 2 * mx + (1 - my), 2 * (1 - mx) + (1 - my)]).astype(jnp.int32)
    proj, hn, w_in_all = _inproj(xs, norm_w, w_in.astype(BF16), slab_order)
    inv_freq = ROPE_THETA ** (-jnp.arange(0, HEAD_DIM, 2, dtype=F32) / HEAD_DIM)
    invf = jnp.tile(inv_freq, 4).reshape(1, 128)
    qw = jnp.tile(q_norm_w, 2).reshape(1, 128)
    kw = jnp.tile(k_norm_w, 2).reshape(1, 128)
    sink_row = sinks.reshape(1, N_HEADS)
    oa, w_glu_all, w_out_all = _attn_fwd(proj, posf, invf, qw, kw, sink_row, [w_glu.astype(BF16), w_out.astype(BF16)])
    w_glu_b = w_glu_all.reshape(SSM_W, SSM_W)
    w_out_b = w_out_all.reshape(D_MODEL, D_MODEL)

    lam_r, lam_i, pw_r, pw_i, bb_r, bb_i = _ssm_prep(a_re, a_im, log_step, b_re, b_im, t // N_SEG)
    rows8 = lambda a: jnp.broadcast_to(a.reshape(SSM_GB, 1, SSM_ST), (SSM_GB, N_SEG, SSM_ST))
    lam_r8, lam_i8, pw_r8, pw_i8 = rows8(lam_r), rows8(lam_i), rows8(pw_r), rows8(pw_i)
    ssm_w_in = jnp.concatenate([_block_diag_in(bb_r), _block_diag_in(bb_i)], axis=1).astype(BF16)
    ssm_w_out = jnp.concatenate([_block_diag_out(c_re), _block_diag_out(-c_im)], axis=2).astype(BF16)
    d_row = d_skip.reshape(1, SSM_W)
    uz = _permute_rows(proj[:, 2560:])
    yp, hc = _ssm_fwd(uz, lam_r8, lam_i8, pw_r8, pw_i8, ssm_w_in, ssm_w_out, d_row)
    b_glu_row = b_glu.reshape(1, SSM_W)
    osp, ygp = _glu_fwd(yp, uz, w_glu_b, b_glu_row)
    os_ = _unpermute_rows(osp)
    aw = attn_out_norm_w.reshape(1, ATTN_W)
    sw = ssm_out_norm_w.reshape(1, SSM_W)
    merged, dout, sq_err = _outproj(oa, os_, aw, sw, w_out_b, xs, tgt)
    loss = lax.psum(0.5 * sq_err[0, 0] / D_MODEL, MESH_AXES)

    doa, dos, g_aw, g_sw = _outproj_bwd(dout, oa, os_, aw, sw, w_out_b)
    dout_b = dout.astype(BF16)
    (g_w_out_b,) = _matmul_tn(merged, dout_b, tm=512, tn=1024, name="grad_w_out", dtypes=(BF16,))
    dyp, dzsp, dap, g_b_glu = _glu_bwd(yp, uz, _permute_rows(dos), w_glu_b, b_glu_row)
    (g_w_glu_b,) = _matmul_tn(ygp, dap, tm=512, tn=1024, name="grad_w_glu", dtypes=(BF16,))
    dup, g_wi, g_wo, g_lam, g_d = _ssm_bwd(uz, dyp, hc, lam_r8, lam_i8, pw_r8, pw_i8, ssm_w_in, ssm_w_out, d_row)
    early = [g_w_glu_b.reshape(8, 128, SSM_W), g_w_out_b.reshape(8, 256, D_MODEL)]
    duz = _unpermute_rows(jnp.concatenate([dup, dzsp], axis=1))
    dproj, g_qw, g_kw, g_sink, *early_slots = _attn_bwd(proj, posf, invf, qw, kw, sink_row, doa, duz, early)
    g_w_in, g_w_in_b = _matmul_tn(hn, dproj, tm=512, tn=SHARD_W, name="grad_w_in", slabs=True)
    in_shape = (4, 2, D_MODEL // 2, SHARD_W)
    (from_sib,) = _pair_swap([g_w_in_b.reshape(in_shape)])
    pair_in = _add_halves(g_w_in.reshape(in_shape), from_sib, c_idx, rows=128, name="pair_sum")
    grad_x, g_nw, in_slots = _inproj_bwd(dproj, w_in_all, xs, norm_w, dout, [pair_in])

    g_wi = g_wi.reshape(SSM_G, SSM_H, 2 * SSM_P)
    g_wo = g_wo.reshape(SSM_G, SSM_H, 2 * SSM_P)
    g_bb_r = g_wi[:, :, :SSM_P].transpose(0, 2, 1).reshape(SSM_G, SSM_P * SSM_H)
    g_bb_i = g_wi[:, :, SSM_P:].transpose(0, 2, 1).reshape(SSM_G, SSM_P * SSM_H)
    g_a_re, g_a_im, g_ls, g_b_re, g_b_im = _ssm_param_grads(
        a_re, a_im, log_step, b_re, b_im, g_lam[:, 0, :SSM_ST].reshape(SSM_G, SSM_P),
        g_lam[:, 0, SSM_ST:].reshape(SSM_G, SSM_P), g_bb_r, g_bb_i)
    small_g = dict(
        norm_w=g_nw, q_norm_w=g_qw[0, :64] + g_qw[0, 64:], k_norm_w=g_kw[0, :64] + g_kw[0, 64:],
        sinks=g_sink[0, :N_HEADS], a_re=g_a_re, a_im=g_a_im, log_step=g_ls, b_re=g_b_re, b_im=g_b_im,
        c_re=g_wo[:, :, :SSM_P], c_im=-g_wo[:, :, SSM_P:], d_skip=g_d,
        b_glu=g_b_glu, attn_out_norm_w=g_aw, ssm_out_norm_w=g_sw)

    mine = [_sum_peers(in_slots, pair_in, chip_idx, rows=128, name="sum_w_in"),
            _sum_peers(early_slots[0], early[0], dev_idx, rows=128, name="sum_w_glu"),
            _sum_peers(early_slots[1], early[1], dev_idx, rows=128, name="sum_w_out")]
    theirs = _half_swap(mine)
    packed = _pack_small(small_g, SMALL, 8 * PACK_ROWS).reshape(8, PACK_ROWS, 128)
    summed = _sum_slots(_exchange_slices(packed, True, "small_scatter"), rows=PACK_ROWS, name="small_sum")
    small_red = _exchange_slices(summed, False, "small_gather").reshape(8 * PACK_ROWS, 128)

    big = [_adamw_halves(w_in, mine[0], theirs[0], m_w_in, v_w_in, c_idx, rows=256, name="adamw_w_in"),
           _adamw_halves(w_glu, mine[1], theirs[1], m_w_glu, v_w_glu, c_idx, rows=128, name="adamw_w_glu"),
           _adamw_halves(w_out, mine[2], theirs[2], m_w_out, v_w_out, c_idx, rows=256, name="adamw_w_out")]
    g_in_sh, g_glu_sh, g_out_sh = (b[0] for b in big)
    upd = [b[1:] for b in big]
    grads = _unpack_small(small_red, SMALL)
    flat_first = sum(math.prod(SMALL_SHAPES[n]) for n in SMALL_3D) // 128
    sd, sm, sv = _adamw(_pack_small(small_w, SMALL_FLAT, FLAT_ROWS), small_red[flat_first:flat_first + FLAT_ROWS],
                        _pack_small(small_m, SMALL_FLAT, FLAT_ROWS), _pack_small(small_v, SMALL_FLAT, FLAT_ROWS),
                        rows=FLAT_ROWS, name="adamw_small")
    deltas, new_m, new_v = (_unpack_small(a, SMALL_FLAT) for a in (sd, sm, sv))
    for n in SMALL_3D:
        deltas[n], new_m[n], new_v[n] = _adamw_3d(small_w[n], grads[n], small_m[n], small_v[n], name="adamw_" + n)
    grads.update(w_in=g_in_sh, w_glu=g_glu_sh, w_out=g_out_sh)
    for n, (d, m_, v_) in zip(("w_in", "w_glu", "w_out"), upd):
        deltas[n], new_m[n], new_v[n] = d, m_, v_
    order = ["norm_w", "w_in", "q_norm_w", "k_norm_w", "sinks", "a_re", "a_im", "log_step", "b_re", "b_im", "c_re",
             "c_im", "d_skip", "w_glu", "b_glu", "attn_out_norm_w", "ssm_out_norm_w", "w_out"]
    return (loss, grad_x[None], *[grads[n] for n in order], *[deltas[n] for n in order],
            *[new_m[n] for n in order], *[new_v[n] for n in order])
```

```python
import math

import jax
import jax.numpy as jnp
from jax import lax
from jax.experimental import pallas as pl
from jax.experimental.pallas import tpu as pltpu

F32 = jnp.float32
BF16 = jnp.bfloat16

D_MODEL = 2048
ATTN_W = 1024
SSM_W = 1024
HEAD_DIM = 64
N_HEADS = 16
N_KV_HEADS = 4
KV_W = 256
BLOCK = 128
IN_W = 4608
SHARD_W = IN_W // 4
ROPE_THETA = 10000.0
SSM_H = 16
SSM_G = 64
SSM_P = 64
NORM_EPS = 1e-6
ADAM_LR = 0.001
ADAM_B1 = 0.9
ADAM_B2 = 0.999
ADAM_EPS = 1e-08
ADAM_WD = 0.01
ADAM_STEP = 10

N_SEG = 8
SSM_GB = 4
SSM_CH = 256
SSM_ST = 1024
SCAN_ROWS = 256
SCAN_LW = 512
VMEM_LIMIT = 56 * 1024 * 1024
MESH_AXES = ("x", "y", "c")
ANY = pl.BlockSpec(memory_space=pl.ANY)

SMALL_3D = ("b_re", "b_im", "c_re", "c_im")
SMALL_FLAT = ("norm_w", "q_norm_w", "k_norm_w", "sinks", "a_re", "a_im", "log_step", "d_skip", "b_glu",
              "attn_out_norm_w", "ssm_out_norm_w")
SMALL = SMALL_3D + SMALL_FLAT
SMALL_SHAPES = {"norm_w": (2048,), "q_norm_w": (64,), "k_norm_w": (64,), "sinks": (16,), "a_re": (64, 64),
                "a_im": (64, 64), "log_step": (64,), "b_re": (64, 64, 16), "b_im": (64, 64, 16),
                "c_re": (64, 16, 64), "c_im": (64, 16, 64), "d_skip": (1024,), "b_glu": (1024,),
                "attn_out_norm_w": (1024,), "ssm_out_norm_w": (1024,)}
PACK_ROWS = 272
FLAT_ROWS = 120


def _params(sem=None):
    return pltpu.CompilerParams(dimension_semantics=sem, vmem_limit_bytes=VMEM_LIMIT)


def _dot(a, b):
    return jnp.dot(a, b, preferred_element_type=F32)


def _dot_nt(a, b):
    return lax.dot_general(a, b, (((1,), (1,)), ((), ())), preferred_element_type=F32)


def _dot_tn(a, b):
    return lax.dot_general(a, b, (((0,), (0,)), ((), ())), preferred_element_type=F32)


def _sigmoid(x):
    return 1.0 / (1.0 + jnp.exp(-x))


def _silu(x):
    return x * _sigmoid(x)


def _dsilu(x):
    s = _sigmoid(x)
    return s * (1.0 + x * (1.0 - s))


_GELU_C = math.sqrt(2.0 / math.pi)


def _gelu(x):
    return 0.5 * x * (1.0 + jnp.tanh(_GELU_C * (x + 0.044715 * x * x * x)))


def _dgelu(x):
    t = jnp.tanh(_GELU_C * (x + 0.044715 * x * x * x))
    return 0.5 * (1.0 + t) + 0.5 * x * (1.0 - t * t) * _GELU_C * (1.0 + 3.0 * 0.044715 * x * x)


def _matmul_tn(a, b, *, tm, tn, name, slabs=False, dtypes=(F32, BF16)):
    k, m = a.shape
    _, n = b.shape

    def body(a_ref, b_ref, *o_refs):
        acc = _dot_tn(a_ref[...], b_ref[...])
        for o_ref in o_refs:
            o_ref[...] = acc.astype(o_ref.dtype)

    if slabs:
        out_spec = pl.BlockSpec((None, tm, tn), lambda j, i: (j, i, 0))
        shape = (n // tn, m, tn)
    else:
        out_spec = pl.BlockSpec((tm, tn), lambda j, i: (i, j))
        shape = (m, n)
    return pl.pallas_call(
        body, name=name, grid=(n // tn, m // tm),
        in_specs=[pl.BlockSpec((k, tm), lambda j, i: (0, i)), pl.BlockSpec((k, tn), lambda j, i: (0, j))],
        out_specs=[out_spec] * len(dtypes),
        out_shape=[jax.ShapeDtypeStruct(shape, d) for d in dtypes],
        compiler_params=_params(("parallel", "parallel")),
    )(a, b)


def _inproj(x, norm_w, w_sh, order):
    t = x.shape[0]
    tm = 512
    ni = t // tm
    hr = D_MODEL // 2

    def body(ord_ref, x_ref, nw_ref, sh_ref, proj_ref, hn_ref, full_ref, wbuf, hn_s, ssem, rsem, lsem):
        s, i = pl.program_id(0), pl.program_id(1)
        mx, my, c = _mesh_pos()
        me = 2 * mx + my
        sib = (mx, my, 1 - c)
        chips = _other_chips(mx, my)

        def half(which):
            return pl.ds(pl.multiple_of(which * hr, 8), hr)

        def slot(k):
            return 2 * chips[k][0] + chips[k][1]

        def ici(k):
            return _remote(sh_ref.at[half(c)], full_ref.at[me, half(c)], ssem.at[k], rsem.at[k], (*chips[k], c))

        def own():
            return _remote(sh_ref, full_ref.at[me], ssem.at[6], rsem.at[6], sib)

        def landed(k, which, sem):
            ref = full_ref.at[slot(k), half(which)]
            return _remote(ref, ref, ssem.at[sem], rsem.at[sem], sib)

        def fetch(src, b):
            return pltpu.make_async_copy(src, wbuf.at[b], lsem.at[b])

        @pl.when((s == 0) & (i == 0))
        def _():
            for k in range(3):
                ici(k).start()
            own().start()
            cp = fetch(sh_ref, 0)
            cp.start()
            cp.wait()

        for k in range(3):
            @pl.when((s == k) & (i == max(ni - 2, 0)))
            def _(k=k):
                landed(k, c, k).wait_recv()
                landed(k, c, 3 + k).start()
                landed(k, 1 - c, 3 + k).wait_recv()
                fetch(full_ref.at[slot(k)], (k + 1) % 2).start()

            @pl.when((s == k + 1) & (i == 0))
            def _(k=k):
                fetch(full_ref.at[slot(k)], (k + 1) % 2).wait()

        xv = x_ref[...]
        r = lax.rsqrt(jnp.mean(xv * xv, axis=1, keepdims=True) + NORM_EPS)
        hn = (xv * r * nw_ref[...]).astype(BF16)
        proj_ref[...] = _dot(hn, wbuf[s % 2])

        def hn_out(tile):
            return pltpu.make_async_copy(hn_s, hn_ref.at[pl.ds(pl.multiple_of(tile * tm, tm), tm), :], lsem.at[2])

        @pl.when(((s == 0) & (i > 0)) | ((s == 1) & (i == 0)))
        def _():
            hn_out(jnp.where(s == 0, i - 1, ni - 1)).wait()

        @pl.when(s == 0)
        def _():
            hn_s[...] = hn
            hn_out(i).start()

        @pl.when((s == 3) & (i == ni - 1))
        def _():
            mine = full_ref.at[me]
            _remote(mine, mine, ssem.at[6], rsem.at[6], sib).wait_recv()
            for k in range(3):
                ici(k).wait_send()
                landed(k, c, 3 + k).wait_send()
            own().wait_send()

    return pl.pallas_call(
        body, name="inproj",
        grid_spec=pltpu.PrefetchScalarGridSpec(
            num_scalar_prefetch=1, grid=(4, ni),
            in_specs=[pl.BlockSpec((tm, D_MODEL), lambda s, i, o: (i, 0)),
                      pl.BlockSpec((1, D_MODEL), lambda s, i, o: (0, 0)), ANY],
            out_specs=[pl.BlockSpec((tm, SHARD_W), lambda s, i, o: (i, o[s])), ANY, ANY],
            scratch_shapes=[pltpu.VMEM((2, D_MODEL, SHARD_W), BF16), pltpu.VMEM((tm, D_MODEL), BF16),
                            pltpu.SemaphoreType.DMA((7,)), pltpu.SemaphoreType.DMA((7,)),
                            pltpu.SemaphoreType.DMA((3,))]),
        out_shape=[jax.ShapeDtypeStruct((t, IN_W), F32), jax.ShapeDtypeStruct((t, D_MODEL), BF16),
                   jax.ShapeDtypeStruct((4, D_MODEL, SHARD_W), BF16)],
        compiler_params=_params(("arbitrary", "arbitrary")),
    )(order, x, norm_w.reshape(1, D_MODEL), w_sh)


def _lane128():
    return lax.broadcasted_iota(jnp.int32, (1, 128), 1)


def _head_sums(v):
    lo = _lane128() < 64
    s_lo = jnp.sum(jnp.where(lo, v, 0.0), axis=1, keepdims=True)
    s_hi = jnp.sum(jnp.where(lo, 0.0, v), axis=1, keepdims=True)
    return jnp.where(lo, s_lo, s_hi)


def _rot_half(t):
    first = (_lane128() % 64) < 32
    return jnp.where(first, -pltpu.roll(t, 96, 1), pltpu.roll(t, 32, 1))


def _head_rstd(t):
    return lax.rsqrt(_head_sums(t * t) * (1.0 / HEAD_DIM) + NORM_EPS)


def _prep_tile(t, w, cos, sin, r=None):
    r = _head_rstd(t) if r is None else r
    tn = t * r * w
    return tn * cos + _rot_half(tn) * sin


def _prep_tile_bwd(t, w, cos, sin, g, r=None):
    r = _head_rstd(t) if r is None else r
    d_tn = g * cos - _rot_half(g * sin)
    th = t * r
    dw = jnp.sum(d_tn * th, axis=0, keepdims=True)
    gh = d_tn * w
    m = _head_sums(gh * th) * (1.0 / HEAD_DIM)
    return r * (gh - th * m), dw


def _band_mask(n):
    qi = lax.broadcasted_iota(jnp.int32, (BLOCK, 2 * BLOCK), 0) + BLOCK
    ki = lax.broadcasted_iota(jnp.int32, (BLOCK, 2 * BLOCK), 1)
    rel = qi - ki
    return (rel >= 0) & (rel < BLOCK) & ((n > 0) | (ki >= BLOCK))


def _half_select(tile, half):
    lo = _lane128() < 64
    return jnp.where(lo if half == 0 else jnp.logical_not(lo), tile, 0.0)


def _stack_group(tiles, kv_half):
    rows = []
    for t in tiles:
        for half in range(2):
            piece = _half_select(t, half)
            rows.append(piece if half == kv_half else pltpu.roll(piece, 64, 1))
    return jnp.concatenate(rows, axis=0)


def _unstack_group(stacked, kv_half):
    tiles = []
    for i in range(2):
        acc = None
        for half in range(2):
            piece = _half_select(stacked[BLOCK * (2 * i + half):BLOCK * (2 * i + half + 1)], kv_half)
            piece = piece if half == kv_half else pltpu.roll(piece, 64, 1)
            acc = piece if acc is None else acc + piece
        tiles.append(acc)
    return tiles


def _attn_specs(nb):
    last = nb - 1
    qi = lambda n: (jnp.minimum(n, last), 0)
    prev = lambda n: jnp.maximum(n - 1, 0)
    cur = lambda n: jnp.minimum(n, last)
    specs = [
        pl.BlockSpec((BLOCK, ATTN_W), qi),
        pl.BlockSpec((BLOCK, KV_W), lambda n: (cur(n), 4)),
        pl.BlockSpec((BLOCK, KV_W), lambda n: (prev(n), 4)),
        pl.BlockSpec((BLOCK, KV_W), lambda n: (cur(n), 5)),
        pl.BlockSpec((BLOCK, KV_W), lambda n: (prev(n), 5)),
        pl.BlockSpec((BLOCK, 512), lambda n: (cur(n), 3)),
        pl.BlockSpec((BLOCK, 512), lambda n: (cur(n), 4)),
        pl.BlockSpec((BLOCK, 256), lambda n: (cur(n), 0)),
        pl.BlockSpec((BLOCK, 256), lambda n: (prev(n), 0)),
        pl.BlockSpec((1, 128), lambda n: (0, 0)),
        pl.BlockSpec((1, 128), lambda n: (0, 0)),
        pl.BlockSpec((1, N_HEADS), lambda n: (0, 0)),
    ]
    return specs


def _rope_table(posf, invf):
    t = posf.shape[0]

    def body(p_ref, f_ref, o_ref):
        ang = p_ref[...] * f_ref[...]
        o_ref[...] = jnp.concatenate([jnp.cos(ang), jnp.sin(ang)], axis=1)

    return pl.pallas_call(
        body, name="rope_table", grid=(t // 512,),
        in_specs=[pl.BlockSpec((512, 1), lambda i: (i, 0)), pl.BlockSpec((1, 128), lambda i: (0, 0))],
        out_specs=pl.BlockSpec((512, 256), lambda i: (i, 0)),
        out_shape=jax.ShapeDtypeStruct((t, 256), F32), compiler_params=_params(("parallel",)),
    )(posf, invf)


def _attn_common(n, q_ref, kc_ref, kp_ref, vc_ref, vp_ref, rq_ref, rp_ref, qw_ref, kw_ref):
    cos_q, sin_q = rq_ref[:, 0:128], rq_ref[:, 128:256]
    cos_k = jnp.concatenate([rp_ref[:, 0:128], cos_q], axis=0)
    sin_k = jnp.concatenate([rp_ref[:, 128:256], sin_q], axis=0)
    k_raw = jnp.concatenate([kp_ref[...], kc_ref[...]], axis=0)
    vv = jnp.concatenate([vp_ref[...], vc_ref[...]], axis=0).astype(BF16)
    kk = [_prep_tile(k_raw[:, 128 * i:128 * i + 128], kw_ref[...], cos_k, sin_k).astype(BF16) for i in range(2)]
    vt = [vv[:, 128 * i:128 * i + 128] for i in range(2)]
    qv = q_ref[...]
    qr = [_head_rstd(qv[:, 128 * i:128 * i + 128]) for i in range(8)]
    qt = [_prep_tile(qv[:, 128 * i:128 * i + 128], qw_ref[...], cos_q, sin_q, qr[i]) for i in range(8)]
    return cos_q, sin_q, qr, kk, vt, qt


QK_SCALE = 1.0 / math.sqrt(HEAD_DIM)


def _group_sinks(sink_ref, g):
    return jnp.concatenate([jnp.broadcast_to(sink_ref[:, 4 * g + j:4 * g + j + 1], (BLOCK, 1)) for j in range(4)], axis=0)


def _group_softmax(q4, kk_t, sink, bias):
    s = _dot_nt(q4, kk_t) + bias
    m = jnp.maximum(jnp.max(s, axis=1, keepdims=True), sink)
    p = jnp.exp(s - m)
    es = jnp.exp(sink - m)
    inv = 1.0 / (jnp.sum(p, axis=1, keepdims=True) + es)
    return p * inv, es * inv


def _group_bias(n):
    return jnp.concatenate([jnp.where(_band_mask(n), 0.0, -1e30)] * 4, axis=0)


def _attn_fwd(proj, rope, qw, kw, sinks, later_shards):
    t = proj.shape[0]
    nb = t // BLOCK
    nt = len(later_shards)

    def body(q_ref, kc_ref, kp_ref, vc_ref, vp_ref, za0_ref, za1_ref, rq_ref, rp_ref, qw_ref, kw_ref,
             sink_ref, *rest):
        sh, o_ref, full = rest[:nt], rest[nt], rest[nt + 1:2 * nt + 1]
        ssem, rsem = rest[2 * nt + 1:]
        n = pl.program_id(0)
        start, wait = _bg_gather(sh, full, ssem, rsem)

        @pl.when(n == 0)
        def _():
            start()

        _, _, _, kk, vt, qt = _attn_common(n, q_ref, kc_ref, kp_ref, vc_ref, vp_ref, rq_ref, rp_ref, qw_ref, kw_ref)
        bias = _group_bias(n)
        tiles = []
        for g in range(N_KV_HEADS):
            q4 = (_stack_group(qt[2 * g:2 * g + 2], g % 2) * QK_SCALE).astype(BF16)
            p, _ = _group_softmax(q4, kk[g // 2], _group_sinks(sink_ref, g), bias)
            tiles += _unstack_group(_dot(p.astype(BF16), vt[g // 2]), g % 2)
        za = jnp.concatenate([za0_ref[...], za1_ref[...]], axis=1)
        o_ref[...] = jnp.concatenate(tiles, axis=1) * _silu(za)

        @pl.when(n == nb - 1)
        def _():
            wait()

    return pl.pallas_call(
        body, name="attn_fwd", grid=(nb,), in_specs=_attn_specs(nb) + [ANY] * nt,
        out_specs=[pl.BlockSpec((BLOCK, ATTN_W), lambda n: (n, 0))] + [ANY] * nt,
        out_shape=[jax.ShapeDtypeStruct((t, ATTN_W), F32)]
        + [jax.ShapeDtypeStruct((4,) + s.shape, s.dtype) for s in later_shards],
        scratch_shapes=[pltpu.SemaphoreType.DMA((4 * nt,)), pltpu.SemaphoreType.DMA((4 * nt,))],
        compiler_params=_params(("arbitrary",)),
    )(proj, proj, proj, proj, proj, proj, proj, rope, rope, qw, kw, sinks, *later_shards)


def _attn_bwd(proj, rope, qw, kw, sinks, doa, duz, outgoing):
    t = proj.shape[0]
    nb = t // BLOCK
    last = nb - 1
    nt = len(outgoing)

    def body(q_ref, kc_ref, kp_ref, vc_ref, vp_ref, za0_ref, za1_ref, rq_ref, rp_ref, qw_ref, kw_ref,
             sink_ref, doa_ref, duz_ref, *rest):
        src = rest[:nt]
        dp_ref, gq_ref, gk_ref, gs_ref = rest[nt:nt + 4]
        dst = rest[nt + 4:2 * nt + 4]
        dkk_s, dvv_s, ck_s, cv_s, dq_s, dza_s, ssem, rsem = rest[2 * nt + 4:]
        n = pl.program_id(0)
        start, wait = _bg_scatter_devices(src, dst, ssem, rsem)

        @pl.when(n == 0)
        def _():
            start()
            gq_ref[...] = jnp.zeros_like(gq_ref)
            gk_ref[...] = jnp.zeros_like(gk_ref)
            gs_ref[...] = jnp.zeros_like(gs_ref)
            ck_s[...] = jnp.zeros_like(ck_s)
            cv_s[...] = jnp.zeros_like(cv_s)
            dq_s[...] = jnp.zeros_like(dq_s)
            dza_s[...] = jnp.zeros_like(dza_s)

        dp_ref[:, 0:ATTN_W] = dq_s[...]
        dp_ref[:, ATTN_W + 2 * KV_W:2 * ATTN_W + 2 * KV_W] = dza_s[...]
        dp_ref[:, 2 * ATTN_W + 2 * KV_W:IN_W] = duz_ref[...]

        @pl.when(n == nb)
        def _():
            dkk_s[...] = jnp.zeros_like(dkk_s)
            dvv_s[...] = jnp.zeros_like(dvv_s)

        @pl.when(n < nb)
        def _():
            cos_q, sin_q, qr, kk, vt, qt = _attn_common(n, q_ref, kc_ref, kp_ref, vc_ref, vp_ref, rq_ref, rp_ref,
                                                        qw_ref, kw_ref)
            bias = _group_bias(n)
            za = jnp.concatenate([za0_ref[...], za1_ref[...]], axis=1)
            doa_v = doa_ref[...]
            do_full = doa_v * _silu(za)
            o_tiles, dq_tiles = [], []
            dkk = [jnp.zeros((2 * BLOCK, 128), F32) for _ in range(2)]
            dvv = [jnp.zeros((2 * BLOCK, 128), F32) for _ in range(2)]
            gsink = jnp.zeros((1, 128), F32)
            lane = _lane128()
            for g in range(N_KV_HEADS):
                q_b = (_stack_group(qt[2 * g:2 * g + 2], g % 2) * QK_SCALE).astype(BF16)
                do_b = _stack_group([do_full[:, 128 * i:128 * i + 128] for i in (2 * g, 2 * g + 1)], g % 2).astype(BF16)
                p, psink = _group_softmax(q_b, kk[g // 2], _group_sinks(sink_ref, g), bias)
                p_b = p.astype(BF16)
                dp = _dot_nt(do_b, vt[g // 2])
                delta = jnp.sum(p * dp, axis=1, keepdims=True)
                ds_b = (p * (dp - delta)).astype(BF16)
                sd = psink * delta
                for j in range(4):
                    gsink = gsink + jnp.where(lane == 4 * g + j, -jnp.sum(sd[BLOCK * j:BLOCK * (j + 1)]), 0.0)
                o_tiles += _unstack_group(_dot(p_b, vt[g // 2]), g % 2)
                dq_tiles += [d * QK_SCALE for d in _unstack_group(_dot(ds_b, kk[g // 2]), g % 2)]
                dkk[g // 2] = dkk[g // 2] + _dot_tn(ds_b, q_b)
                dvv[g // 2] = dvv[g // 2] + _dot_tn(p_b, do_b)
            dza_s[...] = (doa_v * jnp.concatenate(o_tiles, axis=1) * _dsilu(za)).astype(BF16)
            qv = q_ref[...]
            gq = jnp.zeros((1, 128), F32)
            out = []
            for i in range(8):
                d, dw = _prep_tile_bwd(qv[:, 128 * i:128 * i + 128], qw_ref[...], cos_q, sin_q, dq_tiles[i], qr[i])
                out.append(d)
                gq = gq + dw
            dq_s[...] = jnp.concatenate(out, axis=1).astype(BF16)
            gq_ref[...] += gq
            gs_ref[...] += gsink
            dkk_s[...] = jnp.concatenate(dkk, axis=1)
            dvv_s[...] = jnp.concatenate(dvv, axis=1)

        cos_p, sin_p = rp_ref[:, 0:128], rp_ref[:, 128:256]
        dk_prev = ck_s[...] + dkk_s[0:BLOCK, :]
        kp = kp_ref[...]
        gk = jnp.zeros((1, 128), F32)
        out = []
        for i in range(2):
            d, dw = _prep_tile_bwd(kp[:, 128 * i:128 * i + 128], kw_ref[...], cos_p, sin_p,
                                   dk_prev[:, 128 * i:128 * i + 128])
            out.append(d)
            gk = gk + dw
        dp_ref[:, ATTN_W:ATTN_W + KV_W] = jnp.concatenate(out, axis=1).astype(BF16)
        dp_ref[:, ATTN_W + KV_W:ATTN_W + 2 * KV_W] = (cv_s[...] + dvv_s[0:BLOCK, :]).astype(BF16)
        gk_ref[...] += gk
        ck_s[...] = dkk_s[BLOCK:2 * BLOCK, :]
        cv_s[...] = dvv_s[BLOCK:2 * BLOCK, :]

        @pl.when(n == nb)
        def _():
            wait()

    qblk = lambda n: (jnp.minimum(n, last), 0)
    kblk = lambda n: (jnp.maximum(n - 1, 0), 0)
    vec = pl.BlockSpec((1, 128), lambda n: (0, 0))
    return pl.pallas_call(
        body, name="attn_bwd", grid=(nb + 1,),
        in_specs=_attn_specs(nb) + [pl.BlockSpec((BLOCK, ATTN_W), qblk), pl.BlockSpec((BLOCK, 2 * SSM_W), kblk)]
        + [ANY] * nt,
        out_specs=[pl.BlockSpec((BLOCK, IN_W), kblk), vec, vec, vec] + [ANY] * nt,
        out_shape=[jax.ShapeDtypeStruct((t, IN_W), BF16), jax.ShapeDtypeStruct((1, 128), F32),
                   jax.ShapeDtypeStruct((1, 128), F32), jax.ShapeDtypeStruct((1, 128), F32)]
        + [jax.ShapeDtypeStruct(a.shape, a.dtype) for a in outgoing],
        scratch_shapes=[pltpu.VMEM((2 * BLOCK, KV_W), F32), pltpu.VMEM((2 * BLOCK, KV_W), F32),
                        pltpu.VMEM((BLOCK, KV_W), F32), pltpu.VMEM((BLOCK, KV_W), F32),
                        pltpu.VMEM((BLOCK, ATTN_W), BF16), pltpu.VMEM((BLOCK, ATTN_W), BF16),
                        pltpu.SemaphoreType.DMA((7 * nt,)), pltpu.SemaphoreType.DMA((7 * nt,))],
        compiler_params=_params(("arbitrary",)),
    )(proj, proj, proj, proj, proj, proj, proj, rope, rope, qw, kw, sinks, doa, duz, *outgoing)


def _cmul(ar, ai, br, bi):
    return ar * br - ai * bi, ar * bi + ai * br


def _zoh(a_re, a_im, delta):
    e = jnp.exp(a_re * delta)
    lr, li = e * jnp.cos(a_im * delta), e * jnp.sin(a_im * delta)
    inv = 1.0 / (a_re * a_re + a_im * a_im)
    fr, fi = _cmul(lr - 1.0, li, a_re * inv, -a_im * inv)
    return lr, li, fr, fi


def _ssm_prep(a_re, a_im, log_step, b_re, b_im, seg_len):
    n_sq = int(round(math.log2(seg_len)))
    assert 2 ** n_sq == seg_len

    def body(ar_ref, ai_ref, ls_ref, arx_ref, aix_ref, br_ref, bi_ref, lr_ref, li_ref, pr_ref, pi_ref, bbr_ref, bbi_ref):
        delta = jnp.exp(ls_ref[...])
        lr, li, _, _ = _zoh(ar_ref[...], ai_ref[...], delta)
        lr_ref[...] = lr
        li_ref[...] = li
        pr, pi = lr, li
        for _ in range(n_sq):
            pr, pi = _cmul(pr, pi, pr, pi)
        pr_ref[...] = pr
        pi_ref[...] = pi
        _, _, fr, fi = _zoh(arx_ref[...], aix_ref[...], delta)
        bbr, bbi = _cmul(fr, fi, br_ref[...], bi_ref[...])
        bbr_ref[...] = bbr
        bbi_ref[...] = bbi

    gp = jax.ShapeDtypeStruct((SSM_G, SSM_P), F32)
    gx = jax.ShapeDtypeStruct((SSM_G, SSM_P * SSM_H), F32)
    return pl.pallas_call(body, name="ssm_prep", out_shape=[gp, gp, gp, gp, gx, gx])(
        a_re, a_im, log_step.reshape(SSM_G, 1), jnp.repeat(a_re, SSM_H, axis=1), jnp.repeat(a_im, SSM_H, axis=1),
        b_re.reshape(SSM_G, SSM_P * SSM_H), b_im.reshape(SSM_G, SSM_P * SSM_H))


def _ssm_param_grads(a_re, a_im, log_step, b_re, b_im, dlam_re, dlam_im, dbb_re, dbb_im):
    def body(ar_ref, ai_ref, ls_ref, arx_ref, aix_ref, br_ref, bi_ref, dlr_ref, dli_ref, dbr_ref, dbi_ref,
             gar_ref, gai_ref, gls_ref, gbr_ref, gbi_ref):
        delta = jnp.exp(ls_ref[...])
        ar, ai = ar_ref[...], ai_ref[...]
        lr, li, fr, fi = _zoh(ar, ai, delta)
        _, _, frx, fix = _zoh(arx_ref[...], aix_ref[...], delta)
        dbr, dbi = dbr_ref[...], dbi_ref[...]
        br, bi = br_ref[...], bi_ref[...]
        gbr, gbi = _cmul(frx, -fix, dbr, dbi)
        gbr_ref[...] = gbr
        gbi_ref[...] = gbi
        tr, ti = _cmul(br, -bi, dbr, dbi)
        row = lax.broadcasted_iota(jnp.int32, (SSM_P * SSM_H, SSM_P), 0)
        col = lax.broadcasted_iota(jnp.int32, (SSM_P * SSM_H, SSM_P), 1)
        fold = (row // SSM_H == col).astype(F32)
        dfr = jnp.dot(tr, fold, precision=lax.Precision.HIGHEST, preferred_element_type=F32)
        dfi = jnp.dot(ti, fold, precision=lax.Precision.HIGHEST, preferred_element_type=F32)
        inv = 1.0 / (ar * ar + ai * ai)
        ilr, ili = ar * inv, -ai * inv
        t1r, t1i = _cmul(dfr, dfi, ilr, -ili)
        dlbr, dlbi = dlr_ref[...] + t1r, dli_ref[...] + t1i
        qr, qi = _cmul(fr, fi, ilr, ili)
        t2r, t2i = _cmul(dfr, dfi, qr, -qi)
        glr, gli = -t2r, -t2i
        dzr, dzi = _cmul(dlbr, dlbi, lr, -li)
        gar_ref[...] = glr + dzr * delta
        gai_ref[...] = gli + dzi * delta
        gls_ref[...] = jnp.sum(dzr * ar + dzi * ai, axis=1, keepdims=True) * delta

    gp = jax.ShapeDtypeStruct((SSM_G, SSM_P), F32)
    gx = jax.ShapeDtypeStruct((SSM_G, SSM_P * SSM_H), F32)
    return pl.pallas_call(body, name="ssm_param_grads",
                          out_shape=[gp, gp, jax.ShapeDtypeStruct((SSM_G, 1), F32), gx, gx])(
        a_re, a_im, log_step.reshape(SSM_G, 1), jnp.repeat(a_re, SSM_H, axis=1), jnp.repeat(a_im, SSM_H, axis=1),
        b_re.reshape(SSM_G, SSM_P * SSM_H), b_im.reshape(SSM_G, SSM_P * SSM_H), dlam_re, dlam_im, dbb_re, dbb_im)


def _block_diag_in(bb):
    w = jnp.tile(bb.reshape(SSM_GB, SSM_ST, SSM_H), (1, 1, 16))
    row = lax.broadcasted_iota(jnp.int32, (1, SSM_ST, SSM_CH), 1) // SSM_P
    col = lax.broadcasted_iota(jnp.int32, (1, SSM_ST, SSM_CH), 2) // SSM_H
    return jnp.where(row == col, w, 0.0)


def _block_diag_out(c):
    w = jnp.tile(c.reshape(SSM_GB, SSM_CH, SSM_P), (1, 1, 16))
    row = lax.broadcasted_iota(jnp.int32, (1, SSM_CH, SSM_ST), 1) // SSM_H
    col = lax.broadcasted_iota(jnp.int32, (1, SSM_CH, SSM_ST), 2) // SSM_P
    return jnp.where(row == col, w, 0.0)


def _permute_rows(a):
    t, c = a.shape
    return a.reshape(N_SEG, t // N_SEG, c).transpose(1, 0, 2).reshape(t, c)


def _unpermute_rows(a):
    t, c = a.shape
    return a.reshape(t // N_SEG, N_SEG, c).transpose(1, 0, 2).reshape(t, c)


def _scan_fwd(src_ref, dst_ref, lam_r_ref, lam_i_ref, init_ref, final_ref, steps):
    for k in range(SSM_ST // SCAN_LW):
        re = pl.ds(k * SCAN_LW, SCAN_LW)
        im = pl.ds(SSM_ST + k * SCAN_LW, SCAN_LW)
        lr, li = lam_r_ref[:, re], lam_i_ref[:, re]

        def step(i, carry, re=re, im=im, lr=lr, li=li):
            hr, hi = carry
            rows = pl.ds(pl.multiple_of(i * 8, 8), 8)
            nr = lr * hr - li * hi + src_ref[rows, re]
            ni = lr * hi + li * hr + src_ref[rows, im]
            if dst_ref is not None:
                dst_ref[rows, re] = nr
                dst_ref[rows, im] = ni
            return nr, ni

        hr, hi = lax.fori_loop(0, steps, step, (init_ref[:, re], init_ref[:, im]), unroll=4)
        final_ref[:, re] = hr
        final_ref[:, im] = hi


def _ssm_specs(t):
    col = lambda g: (0, g)
    gb3 = lambda g: (g, 0, 0)
    return dict(
        rows=pl.BlockSpec((t, SSM_CH), col),
        lam=pl.BlockSpec((None, N_SEG, SSM_ST), gb3),
        w_in=pl.BlockSpec((None, 2 * SSM_ST, SSM_CH), gb3),
        w_out=pl.BlockSpec((None, SSM_CH, 2 * SSM_ST), gb3),
        vec=pl.BlockSpec((1, SSM_CH), col),
    )


def _segment_states(x_ref, pw_r_ref, pw_i_ref, out_ref, reverse):
    re, im = pl.ds(0, SSM_ST), pl.ds(SSM_ST, SSM_ST)
    pr, pi = pw_r_ref[0:1, :], pw_i_ref[0:1, :]
    first = N_SEG - 1 if reverse else 0
    out_ref[first:first + 1, :] = jnp.zeros((1, 2 * SSM_ST), F32)
    order = range(N_SEG - 1, 0, -1) if reverse else range(N_SEG - 1)
    for s in order:
        d = s - 1 if reverse else s + 1
        hr, hi = out_ref[s:s + 1, re], out_ref[s:s + 1, im]
        if reverse:
            nr, ni = pr * hr + pi * hi, pr * hi - pi * hr
        else:
            nr, ni = pr * hr - pi * hi, pr * hi + pi * hr
        out_ref[d:d + 1, re] = nr + x_ref[s:s + 1, re]
        out_ref[d:d + 1, im] = ni + x_ref[s:s + 1, im]


def _ssm_fwd(up, lam_r, lam_i, pw_r, pw_i, w_in, w_out, d_skip):
    t = up.shape[0]
    nch = t // SCAN_ROWS
    steps = SCAN_ROWS // N_SEG
    sp = _ssm_specs(t)

    def body(u_ref, lr_ref, li_ref, pr_ref, pi_ref, wi_ref, wo_ref, d_ref, y_ref, hc_ref, bu_s, car_s, seg_s):
        def load_bu(j):
            rows = pl.ds(pl.multiple_of(j * SCAN_ROWS, SCAN_ROWS), SCAN_ROWS)
            bu_s[...] = _dot_nt(u_ref[rows, :].astype(BF16), wi_ref[...])

        car_s[...] = jnp.zeros_like(car_s)

        def chunk1(j, c):
            load_bu(j)
            _scan_fwd(bu_s, None, lr_ref, li_ref, car_s, car_s, steps)
            return c

        lax.fori_loop(0, nch, chunk1, 0)
        _segment_states(car_s, pr_ref, pi_ref, seg_s, reverse=False)
        car_s[...] = seg_s[...]

        def chunk2(j, c):
            load_bu(j)
            hc_ref[j] = car_s[...]
            _scan_fwd(bu_s, bu_s, lr_ref, li_ref, car_s, car_s, steps)
            rows = pl.ds(pl.multiple_of(j * SCAN_ROWS, SCAN_ROWS), SCAN_ROWS)
            y_ref[rows, :] = _dot_nt(bu_s[...].astype(BF16), wo_ref[...]) + d_ref[...] * u_ref[rows, :]
            return c

        lax.fori_loop(0, nch, chunk2, 0)

    return pl.pallas_call(
        body, name="ssm_fwd", grid=(SSM_GB,),
        in_specs=[sp["rows"], sp["lam"], sp["lam"], sp["lam"], sp["lam"], sp["w_in"], sp["w_out"], sp["vec"]],
        out_specs=[sp["rows"], pl.BlockSpec((None, nch, N_SEG, 2 * SSM_ST), lambda g: (g, 0, 0, 0))],
        out_shape=[jax.ShapeDtypeStruct((t, SSM_W), F32), jax.ShapeDtypeStruct((SSM_GB, nch, N_SEG, 2 * SSM_ST), F32)],
        scratch_shapes=[pltpu.VMEM((SCAN_ROWS, 2 * SSM_ST), F32), pltpu.VMEM((N_SEG, 2 * SSM_ST), F32),
                        pltpu.VMEM((N_SEG, 2 * SSM_ST), F32)],
        compiler_params=_params(("parallel",)),
    )(up, lam_r, lam_i, pw_r, pw_i, w_in, w_out, d_skip)


def _group_blocks(full):
    row_g = lax.broadcasted_iota(jnp.int32, (SSM_CH, SSM_ST), 0) // SSM_H
    col_g = lax.broadcasted_iota(jnp.int32, (SSM_CH, SSM_ST), 1) // SSM_P
    fold = (lax.broadcasted_iota(jnp.int32, (SSM_ST, SSM_P), 0) % SSM_P
            == lax.broadcasted_iota(jnp.int32, (SSM_ST, SSM_P), 1)).astype(F32)
    parts = [jnp.dot(jnp.where(row_g == col_g, full[:, k * SSM_ST:(k + 1) * SSM_ST], 0.0), fold,
                     precision=lax.Precision.HIGHEST, preferred_element_type=F32) for k in range(2)]
    return jnp.concatenate(parts, axis=1)


def _ssm_bwd(up, dyp, hc, lam_r, lam_i, pw_r, pw_i, w_in, w_out, d_skip):
    t = up.shape[0]
    nch = t // SCAN_ROWS
    steps = SCAN_ROWS // N_SEG
    sp = _ssm_specs(t)

    def body(u_ref, dy_ref, hc_ref, lr_ref, li_ref, pr_ref, pi_ref, wi_ref, wo_ref, d_ref,
             du_ref, gbi_ref, gbo_ref, glam_ref, gd_ref, bu_s, h_s, e_s, car_s, seg_s, acc_s, gwi_ref, gwo_ref):
        def chunk_rows(j):
            return pl.ds(pl.multiple_of(j * SCAN_ROWS, SCAN_ROWS), SCAN_ROWS)

        def load_e(j):
            e_s[...] = _dot(dy_ref[chunk_rows(j), :].astype(BF16), wo_ref[...])

        def scan_rev(j, accumulate):
            for k in range(SSM_ST // SCAN_LW):
                re = pl.ds(k * SCAN_LW, SCAN_LW)
                im = pl.ds(SSM_ST + k * SCAN_LW, SCAN_LW)
                lr, li = lr_ref[:, re], li_ref[:, re]

                def step(ii, carry, re=re, im=im, lr=lr, li=li):
                    i = steps - 1 - ii
                    rows = pl.ds(pl.multiple_of(i * 8, 8), 8)
                    if accumulate:
                        gr, gi, ar, ai = carry
                    else:
                        gr, gi = carry
                    nr = lr * gr + li * gi + e_s[rows, re]
                    ni = lr * gi - li * gr + e_s[rows, im]
                    if not accumulate:
                        return nr, ni
                    e_s[rows, re] = nr
                    e_s[rows, im] = ni
                    pr_, pi_ = h_s[rows, re], h_s[rows, im]
                    return nr, ni, ar + nr * pr_ + ni * pi_, ai + ni * pr_ - nr * pi_

                init = (car_s[:, re], car_s[:, im])
                if accumulate:
                    init = init + (acc_s[:, re], acc_s[:, im])
                out = lax.fori_loop(0, steps, step, init, unroll=4)
                car_s[:, re] = out[0]
                car_s[:, im] = out[1]
                if accumulate:
                    acc_s[:, re] = out[2]
                    acc_s[:, im] = out[3]

        car_s[...] = jnp.zeros_like(car_s)

        def pass1(jj, c):
            load_e(nch - 1 - jj)
            scan_rev(nch - 1 - jj, False)
            return c

        lax.fori_loop(0, nch, pass1, 0)
        _segment_states(car_s, pr_ref, pi_ref, seg_s, reverse=True)
        car_s[...] = seg_s[...]
        acc_s[...] = jnp.zeros_like(acc_s)
        gwi_ref[...] = jnp.zeros_like(gwi_ref)
        gwo_ref[...] = jnp.zeros_like(gwo_ref)
        gd_ref[...] = jnp.zeros_like(gd_ref)

        def pass2(jj, c):
            j = nch - 1 - jj
            rows = chunk_rows(j)
            u = u_ref[rows, :]
            dy = dy_ref[rows, :]
            u_b, dy_b = u.astype(BF16), dy.astype(BF16)
            bu_s[...] = _dot_nt(u_b, wi_ref[...])
            h_s[0:N_SEG, :] = hc_ref[j]
            seg_s[...] = hc_ref[j]
            _scan_fwd(bu_s, h_s.at[pl.ds(N_SEG, SCAN_ROWS), :], lr_ref, li_ref, seg_s, seg_s, steps)
            load_e(j)
            scan_rev(j, True)
            g_b = e_s[...].astype(BF16)
            du_ref[rows, :] = (_dot(g_b, wi_ref[...]) + d_ref[...] * dy).astype(du_ref.dtype)
            gwi_ref[...] += _dot_tn(u_b, g_b)
            gwo_ref[...] += _dot_tn(dy_b, h_s[pl.ds(N_SEG, SCAN_ROWS), :].astype(BF16))
            gd_ref[...] += jnp.sum(dy * u, axis=0, keepdims=True)
            return c

        lax.fori_loop(0, nch, pass2, 0)
        glam_ref[...] = jnp.sum(acc_s[...], axis=0, keepdims=True)
        gbi_ref[...] = _group_blocks(gwi_ref[...])
        gbo_ref[...] = _group_blocks(gwo_ref[...])

    mat = pl.BlockSpec((None, SSM_CH, 2 * SSM_P), lambda g: (g, 0, 0))
    return pl.pallas_call(
        body, name="ssm_bwd", grid=(SSM_GB,),
        in_specs=[sp["rows"], sp["rows"], pl.BlockSpec((None, nch, N_SEG, 2 * SSM_ST), lambda g: (g, 0, 0, 0)),
                  sp["lam"], sp["lam"], sp["lam"], sp["lam"], sp["w_in"], sp["w_out"], sp["vec"]],
        out_specs=[sp["rows"], mat, mat, pl.BlockSpec((None, 1, 2 * SSM_ST), lambda g: (g, 0, 0)), sp["vec"]],
        out_shape=[jax.ShapeDtypeStruct((t, SSM_W), BF16), jax.ShapeDtypeStruct((SSM_GB, SSM_CH, 2 * SSM_P), F32),
                   jax.ShapeDtypeStruct((SSM_GB, SSM_CH, 2 * SSM_P), F32),
                   jax.ShapeDtypeStruct((SSM_GB, 1, 2 * SSM_ST), F32), jax.ShapeDtypeStruct((1, SSM_W), F32)],
        scratch_shapes=[pltpu.VMEM((SCAN_ROWS, 2 * SSM_ST), F32), pltpu.VMEM((SCAN_ROWS + N_SEG, 2 * SSM_ST), F32),
                        pltpu.VMEM((SCAN_ROWS, 2 * SSM_ST), F32), pltpu.VMEM((N_SEG, 2 * SSM_ST), F32),
                        pltpu.VMEM((N_SEG, 2 * SSM_ST), F32), pltpu.VMEM((N_SEG, 2 * SSM_ST), F32),
                        pltpu.VMEM((SSM_CH, 2 * SSM_ST), F32), pltpu.VMEM((SSM_CH, 2 * SSM_ST), F32)],
        compiler_params=_params(("parallel",)),
    )(up, dyp, hc, lam_r, lam_i, pw_r, pw_i, w_in, w_out, d_skip)


def _glu_fwd(y, uz, w_glu, b_glu):
    t = y.shape[0]
    tm = 512

    def body(y_ref, z_ref, w_ref, b_ref, o_ref, yg_ref):
        yg = _gelu(y_ref[...])
        yg_b = yg.astype(BF16)
        a = _dot(yg_b, w_ref[...]) + b_ref[...]
        o_ref[...] = yg * _sigmoid(a) * _silu(z_ref[...])
        yg_ref[...] = yg_b

    row = pl.BlockSpec((tm, SSM_W), lambda i: (i, 0))
    zcol = pl.BlockSpec((tm, SSM_W), lambda i: (i, 1))
    return pl.pallas_call(
        body, name="glu_fwd", grid=(t // tm,),
        in_specs=[row, zcol, pl.BlockSpec((SSM_W, SSM_W), lambda i: (0, 0)), pl.BlockSpec((1, SSM_W), lambda i: (0, 0))],
        out_specs=[row, row],
        out_shape=[jax.ShapeDtypeStruct((t, SSM_W), F32), jax.ShapeDtypeStruct((t, SSM_W), BF16)],
        compiler_params=_params(("parallel",)),
    )(y, uz, w_glu, b_glu)


def _glu_bwd(y, uz, dos, w_glu, b_glu):
    t = y.shape[0]
    tm = 512

    def body(y_ref, z_ref, do_ref, w_ref, b_ref, dy_ref, dz_ref, da_ref, gb_ref):
        @pl.when(pl.program_id(0) == 0)
        def _():
            gb_ref[...] = jnp.zeros_like(gb_ref)

        yv, z, do = y_ref[...], z_ref[...], do_ref[...]
        yg = _gelu(yv)
        sg = _sigmoid(_dot(yg.astype(BF16), w_ref[...]) + b_ref[...])
        dy2 = do * _silu(z)
        dz_ref[...] = (do * yg * sg * _dsilu(z)).astype(BF16)
        da = dy2 * yg * sg * (1.0 - sg)
        da_b = da.astype(BF16)
        da_ref[...] = da_b
        gb_ref[...] += jnp.sum(da, axis=0, keepdims=True)
        dyg = dy2 * sg + _dot_nt(da_b, w_ref[...])
        dy_ref[...] = dyg * _dgelu(yv)

    row = pl.BlockSpec((tm, SSM_W), lambda i: (i, 0))
    zcol = pl.BlockSpec((tm, SSM_W), lambda i: (i, 1))
    vec = pl.BlockSpec((1, SSM_W), lambda i: (0, 0))
    return pl.pallas_call(
        body, name="glu_bwd", grid=(t // tm,),
        in_specs=[row, zcol, row, pl.BlockSpec((SSM_W, SSM_W), lambda i: (0, 0)), vec],
        out_specs=[row, row, row, vec],
        out_shape=[jax.ShapeDtypeStruct((t, SSM_W), F32), jax.ShapeDtypeStruct((t, SSM_W), BF16),
                   jax.ShapeDtypeStruct((t, SSM_W), BF16), jax.ShapeDtypeStruct((1, SSM_W), F32)],
        compiler_params=_params(("arbitrary",)),
    )(y, uz, dos, w_glu, b_glu)


def _rms(o):
    return lax.rsqrt(jnp.mean(o * o, axis=1, keepdims=True) + NORM_EPS)


def _outproj(oa, os_, aw, sw, w_out, x, target):
    t = x.shape[0]
    tm = 256

    def body(oa_ref, os_ref, aw_ref, sw_ref, w_ref, x_ref, t_ref, mg_ref, do_ref, ls_ref):
        @pl.when(pl.program_id(0) == 0)
        def _():
            ls_ref[...] = jnp.zeros_like(ls_ref)

        a, s = oa_ref[...], os_ref[...]
        merged = jnp.concatenate([a * _rms(a) * aw_ref[...], s * _rms(s) * sw_ref[...]], axis=1).astype(BF16)
        mg_ref[...] = merged
        err = x_ref[...] + _dot(merged, w_ref[...]) - t_ref[...]
        do_ref[...] = err * (1.0 / D_MODEL)
        ls_ref[...] += jnp.sum(err * err)

    half = pl.BlockSpec((tm, ATTN_W), lambda i: (i, 0))
    full = pl.BlockSpec((tm, D_MODEL), lambda i: (i, 0))
    vec = pl.BlockSpec((1, ATTN_W), lambda i: (0, 0))
    return pl.pallas_call(
        body, name="outproj", grid=(t // tm,),
        in_specs=[half, half, vec, vec, pl.BlockSpec((D_MODEL, D_MODEL), lambda i: (0, 0)), full, full],
        out_specs=[full, full, pl.BlockSpec((8, 128), lambda i: (0, 0))],
        out_shape=[jax.ShapeDtypeStruct((t, D_MODEL), BF16), jax.ShapeDtypeStruct((t, D_MODEL), F32),
                   jax.ShapeDtypeStruct((8, 128), F32)],
        compiler_params=_params(("arbitrary",)),
    )(oa, os_, aw, sw, w_out, x, target)


def _outproj_bwd(dout, oa, os_, aw, sw, w_out):
    t = dout.shape[0]
    tm = 256

    def norm_bwd(o, w, dm):
        r = _rms(o)
        yh = o * r
        gh = dm * w
        return r * (gh - yh * jnp.mean(gh * yh, axis=1, keepdims=True)), jnp.sum(dm * yh, axis=0, keepdims=True)

    def body(do_ref, oa_ref, os_ref, aw_ref, sw_ref, w_ref, da_ref, ds_ref, ga_ref, gs_ref):
        @pl.when(pl.program_id(0) == 0)
        def _():
            ga_ref[...] = jnp.zeros_like(ga_ref)
            gs_ref[...] = jnp.zeros_like(gs_ref)

        dm = _dot_nt(do_ref[...].astype(BF16), w_ref[...])
        da, ga = norm_bwd(oa_ref[...], aw_ref[...], dm[:, :ATTN_W])
        ds, gs = norm_bwd(os_ref[...], sw_ref[...], dm[:, ATTN_W:])
        da_ref[...] = da
        ds_ref[...] = ds
        ga_ref[...] += ga
        gs_ref[...] += gs

    half = pl.BlockSpec((tm, ATTN_W), lambda i: (i, 0))
    full = pl.BlockSpec((tm, D_MODEL), lambda i: (i, 0))
    vec = pl.BlockSpec((1, ATTN_W), lambda i: (0, 0))
    return pl.pallas_call(
        body, name="outproj_bwd", grid=(t // tm,),
        in_specs=[full, half, half, vec, vec, pl.BlockSpec((D_MODEL, D_MODEL), lambda i: (0, 0))],
        out_specs=[half, half, vec, vec],
        out_shape=[jax.ShapeDtypeStruct((t, ATTN_W), F32), jax.ShapeDtypeStruct((t, ATTN_W), F32),
                   jax.ShapeDtypeStruct((1, ATTN_W), F32), jax.ShapeDtypeStruct((1, ATTN_W), F32)],
        compiler_params=_params(("arbitrary",)),
    )(dout, oa, os_, aw, sw, w_out)


def _inproj_bwd(dproj, w_slabs, x, norm_w, dout, outgoing):
    t = x.shape[0]
    tm = 512
    nc = 4
    nt = len(outgoing)
    ni = t // tm

    def body(dp_ref, w_ref, x_ref, nw_ref, do_ref, *rest):
        src, (gx_ref, gw_ref), dst = rest[:nt], rest[nt:nt + 2], rest[nt + 2:2 * nt + 2]
        acc_ref, ssem, rsem = rest[2 * nt + 2:]
        i, j = pl.program_id(0), pl.program_id(1)
        start, wait = _bg_scatter_chips(src, dst, ssem, rsem)

        @pl.when((i == 0) & (j == 0))
        def _():
            start()
            gw_ref[...] = jnp.zeros_like(gw_ref)

        @pl.when(j == 0)
        def _():
            acc_ref[...] = jnp.zeros_like(acc_ref)

        acc_ref[...] += _dot_nt(dp_ref[...], w_ref[...])

        @pl.when(j == nc - 1)
        def _():
            xv = x_ref[...]
            r = lax.rsqrt(jnp.mean(xv * xv, axis=1, keepdims=True) + NORM_EPS)
            yh = xv * r
            dh = acc_ref[...]
            gh = dh * nw_ref[...]
            gx_ref[...] = do_ref[...] + r * (gh - yh * jnp.mean(gh * yh, axis=1, keepdims=True))
            gw_ref[...] += jnp.sum(dh * yh, axis=0, keepdims=True)

        @pl.when((i == ni - 1) & (j == nc - 1))
        def _():
            wait()

    full = pl.BlockSpec((tm, D_MODEL), lambda i, j: (i, 0))
    vec = pl.BlockSpec((1, D_MODEL), lambda i, j: (0, 0))
    return pl.pallas_call(
        body, name="inproj_bwd", grid=(ni, nc),
        in_specs=[pl.BlockSpec((tm, SHARD_W), lambda i, j: (i, j)),
                  pl.BlockSpec((None, D_MODEL, SHARD_W), lambda i, j: (j, 0, 0)), full, vec, full] + [ANY] * nt,
        out_specs=[full, vec] + [ANY] * nt,
        out_shape=[jax.ShapeDtypeStruct((t, D_MODEL), F32), jax.ShapeDtypeStruct((1, D_MODEL), F32)]
        + [jax.ShapeDtypeStruct(a.shape, a.dtype) for a in outgoing],
        scratch_shapes=[pltpu.VMEM((tm, D_MODEL), F32), pltpu.SemaphoreType.DMA((3 * nt,)),
                        pltpu.SemaphoreType.DMA((3 * nt,))],
        compiler_params=_params(("arbitrary", "arbitrary")),
    )(dproj, w_slabs, x, norm_w.reshape(1, D_MODEL), dout, *outgoing)


def _adamw_math(w_ref, g_ref, m_ref, v_ref, d_ref, nm_ref, nv_ref):
    gv = g_ref[...]
    nm = ADAM_B1 * m_ref[...] + (1.0 - ADAM_B1) * gv
    nv = ADAM_B2 * v_ref[...] + (1.0 - ADAM_B2) * (gv * gv)
    m_hat = nm / (1.0 - ADAM_B1 ** ADAM_STEP)
    v_hat = nv / (1.0 - ADAM_B2 ** ADAM_STEP)
    d_ref[...] = -ADAM_LR * (m_hat / (jnp.sqrt(v_hat) + ADAM_EPS) + ADAM_WD * w_ref[...])
    nm_ref[...] = nm
    nv_ref[...] = nv


def _adamw_halves(w, mine, theirs, m, v, c_idx, *, rows, name):
    hr, cols = mine.shape
    nblk = hr // rows

    def body(c_ref, w_ref, a_ref, b_ref, m_ref, v_ref, g_ref, d_ref, nm_ref, nv_ref):
        g_ref[...] = jnp.where(pl.program_id(0) == c_ref[0], a_ref[...], b_ref[...])
        _adamw_math(w_ref, g_ref, m_ref, v_ref, d_ref, nm_ref, nv_ref)

    full = pl.BlockSpec((rows, cols), lambda h, i, c: (h * nblk + i, 0))
    part = pl.BlockSpec((rows, cols), lambda h, i, c: (i, 0))
    shp = jax.ShapeDtypeStruct((2 * hr, cols), F32)
    return pl.pallas_call(
        body, name=name,
        grid_spec=pltpu.PrefetchScalarGridSpec(num_scalar_prefetch=1, grid=(2, nblk),
                                               in_specs=[full, part, part, full, full], out_specs=[full] * 4),
        out_shape=[shp] * 4, compiler_params=_params(("parallel", "parallel")),
    )(c_idx, w, mine, theirs, m, v)


def _adamw(w, g, m, v, *, rows, name):
    r, c = w.shape

    def body(w_ref, g_ref, m_ref, v_ref, d_ref, nm_ref, nv_ref):
        _adamw_math(w_ref, g_ref, m_ref, v_ref, d_ref, nm_ref, nv_ref)

    blk = pl.BlockSpec((rows, c), lambda i: (i, 0))
    shp = jax.ShapeDtypeStruct((r, c), F32)
    return pl.pallas_call(body, name=name, grid=(r // rows,), in_specs=[blk] * 4, out_specs=[blk] * 3,
                          out_shape=[shp] * 3, compiler_params=_params(("parallel",)))(w, g, m, v)


def _remote(src, dst, ssem, rsem, dev):
    return pltpu.make_async_remote_copy(src_ref=src, dst_ref=dst, send_sem=ssem, recv_sem=rsem, device_id=dev,
                                        device_id_type=pl.DeviceIdType.MESH)


def _mesh_pos():
    return lax.axis_index("x"), lax.axis_index("y"), lax.axis_index("c")


def _other_chips(x, y):
    return [(1 - x, y), (x, 1 - y), (1 - x, 1 - y)]


def _flips():
    return [(dx, dy, dc) for dx in (0, 1) for dy in (0, 1) for dc in (0, 1) if (dx, dy, dc) != (0, 0, 0)]


def _background(sends, arrivals):
    def start():
        for cp in sends():
            cp.start()

    def wait():
        for cp in arrivals():
            cp.wait_recv()
        for cp in sends():
            cp.wait_send()

    return start, wait


def _bg_gather(sh, full, ssem, rsem):
    x, y, c = _mesh_pos()
    me = 2 * x + y
    peers = [(px, py, c) for px, py in _other_chips(x, y)] + [(x, y, 1 - c)]
    slots = [2 * px + py for px, py in _other_chips(x, y)] + [me]
    pairs = [(i, k) for i in range(len(sh)) for k in range(4)]
    return _background(
        lambda: [_remote(sh[i], full[i].at[me], ssem.at[4 * i + k], rsem.at[4 * i + k], peers[k]) for i, k in pairs],
        lambda: [_remote(full[i].at[slots[k]], full[i].at[slots[k]], ssem.at[4 * i + k], rsem.at[4 * i + k], peers[k])
                 for i, k in pairs])


def _bg_scatter_devices(src, dst, ssem, rsem):
    x, y, c = _mesh_pos()
    me = 4 * x + 2 * y + c
    peers = []
    for dx, dy, dc in _flips():
        px, py, pc = jnp.bitwise_xor(x, dx), jnp.bitwise_xor(y, dy), jnp.bitwise_xor(c, dc)
        peers.append(((px, py, pc), 4 * px + 2 * py + pc))
    pairs = [(i, k) for i in range(len(src)) for k in range(7)]
    return _background(
        lambda: [_remote(src[i].at[peers[k][1]], dst[i].at[me], ssem.at[7 * i + k], rsem.at[7 * i + k], peers[k][0])
                 for i, k in pairs],
        lambda: [_remote(dst[i].at[peers[k][1]], dst[i].at[peers[k][1]], ssem.at[7 * i + k], rsem.at[7 * i + k],
                         peers[k][0]) for i, k in pairs])


def _bg_scatter_chips(src, dst, ssem, rsem):
    x, y, c = _mesh_pos()
    me = 2 * x + y
    chips = _other_chips(x, y)
    pairs = [(i, k) for i in range(len(src)) for k in range(3)]
    slot = lambda k: 2 * chips[k][0] + chips[k][1]
    return _background(
        lambda: [_remote(src[i].at[slot(k)], dst[i].at[me], ssem.at[3 * i + k], rsem.at[3 * i + k], (*chips[k], c))
                 for i, k in pairs],
        lambda: [_remote(dst[i].at[slot(k)], dst[i].at[slot(k)], ssem.at[3 * i + k], rsem.at[3 * i + k], (*chips[k], c))
                 for i, k in pairs])


def _pair_swap(arrays):
    nt = len(arrays)

    def body(*refs):
        src, dst = refs[:nt], refs[nt:2 * nt]
        ssem, rsem = refs[2 * nt:]
        x, y, c = _mesh_pos()
        cps = [_remote(src[i].at[:, 1 - c], dst[i], ssem.at[i], rsem.at[i], (x, y, 1 - c)) for i in range(nt)]
        for cp in cps:
            cp.start()
        for cp in cps:
            cp.wait_recv()
        for cp in cps:
            cp.wait_send()

    return pl.pallas_call(
        body, name="pair_swap", in_specs=[ANY] * nt, out_specs=[ANY] * nt,
        out_shape=[jax.ShapeDtypeStruct((4,) + a.shape[2:], a.dtype) for a in arrays],
        scratch_shapes=[pltpu.SemaphoreType.DMA((nt,)), pltpu.SemaphoreType.DMA((nt,))],
    )(*arrays)


def _half_swap(arrays):
    nt = len(arrays)

    def body(*refs):
        src, dst = refs[:nt], refs[nt:2 * nt]
        ssem, rsem = refs[2 * nt:]
        x, y, c = _mesh_pos()
        cps = [_remote(src[i], dst[i], ssem.at[i], rsem.at[i], (x, y, 1 - c)) for i in range(nt)]
        for cp in cps:
            cp.start()
        for cp in cps:
            cp.wait_recv()
        for cp in cps:
            cp.wait_send()

    return pl.pallas_call(
        body, name="half_swap", in_specs=[ANY] * nt, out_specs=[ANY] * nt,
        out_shape=[jax.ShapeDtypeStruct(a.shape, a.dtype) for a in arrays],
        scratch_shapes=[pltpu.SemaphoreType.DMA((nt,)), pltpu.SemaphoreType.DMA((nt,))],
    )(*arrays)


def _exchange_slices(src, scatter, name):
    def body(src_ref, dst_ref, ssem, rsem, lsem):
        x, y, c = _mesh_pos()
        me = 4 * x + 2 * y + c
        local = pltpu.make_async_copy(src_ref.at[me] if scatter else src_ref, dst_ref.at[me], lsem)
        local.start()
        cps = []
        for k, (dx, dy, dc) in enumerate(_flips()):
            px, py, pc = jnp.bitwise_xor(x, dx), jnp.bitwise_xor(y, dy), jnp.bitwise_xor(c, dc)
            peer = 4 * px + 2 * py + pc
            cp = _remote(src_ref.at[peer] if scatter else src_ref, dst_ref.at[me], ssem.at[k], rsem.at[k],
                         (px, py, pc))
            cp.start()
            cps.append((cp, peer))
        for k, (cp, peer) in enumerate(cps):
            slot = dst_ref.at[peer]
            _remote(slot, slot, ssem.at[k], rsem.at[k], (x, y, c)).wait_recv()
        for cp, _ in cps:
            cp.wait_send()
        local.wait()

    return pl.pallas_call(
        body, name=name, in_specs=[ANY], out_specs=ANY,
        out_shape=jax.ShapeDtypeStruct((8,) + src.shape[-2:], src.dtype),
        scratch_shapes=[pltpu.SemaphoreType.DMA((7,)), pltpu.SemaphoreType.DMA((7,)), pltpu.SemaphoreType.DMA],
    )(src)


def _add_halves(g, recv, c_idx, *, rows, name):
    _, _, hr, cols = g.shape

    def body(c_ref, g_ref, r_ref, o_ref):
        o_ref[...] = (g_ref[...] + r_ref[...].astype(F32)).astype(BF16)

    return pl.pallas_call(
        body, name=name,
        grid_spec=pltpu.PrefetchScalarGridSpec(
            num_scalar_prefetch=1, grid=(4, hr // rows),
            in_specs=[pl.BlockSpec((None, None, rows, cols), lambda j, i, c: (j, c[0], i, 0)),
                      pl.BlockSpec((None, rows, cols), lambda j, i, c: (j, i, 0))],
            out_specs=pl.BlockSpec((None, rows, cols), lambda j, i, c: (j, i, 0))),
        out_shape=jax.ShapeDtypeStruct((4, hr, cols), BF16),
        compiler_params=_params(("parallel", "parallel")),
    )(c_idx, g, recv)


def _sum_peers(slots, own, idx, *, rows, name):
    n, r, cols = slots.shape

    def body(me_ref, *refs):
        me = me_ref[0]
        mine = refs[n][...].astype(F32)
        acc = None
        for k in range(n):
            term = jnp.where(me == k, mine, refs[k][...].astype(F32))
            acc = term if acc is None else acc + term
        refs[n + 1][...] = acc

    def slot_spec(k):
        return pl.BlockSpec((None, rows, cols), lambda i, me: (jnp.where(me[0] == k, (k + 1) % n, k), i, 0))

    return pl.pallas_call(
        body, name=name,
        grid_spec=pltpu.PrefetchScalarGridSpec(
            num_scalar_prefetch=1, grid=(r // rows,),
            in_specs=[slot_spec(k) for k in range(n)] + [pl.BlockSpec((None, rows, cols), lambda i, me: (me[0], i, 0))],
            out_specs=pl.BlockSpec((rows, cols), lambda i, me: (i, 0))),
        out_shape=jax.ShapeDtypeStruct((r, cols), F32),
        compiler_params=_params(("parallel",)),
    )(idx, *([slots] * n), own)


def _sum_slots(slots, *, rows, name):
    n, r, cols = slots.shape

    def body(s_ref, o_ref):
        acc = s_ref[0].astype(F32)
        for k in range(1, n):
            acc = acc + s_ref[k].astype(F32)
        o_ref[...] = acc

    return pl.pallas_call(
        body, name=name, grid=(r // rows,),
        in_specs=[pl.BlockSpec((n, rows, cols), lambda i: (0, i, 0))],
        out_specs=pl.BlockSpec((rows, cols), lambda i: (i, 0)),
        out_shape=jax.ShapeDtypeStruct((r, cols), F32),
        compiler_params=_params(("parallel",)),
    )(slots)


def _pack_small(d, names, rows):
    flat = jnp.concatenate([d[n].astype(F32).reshape(-1) for n in names])
    return jnp.pad(flat, (0, rows * 128 - flat.shape[0])).reshape(rows, 128)


def _unpack_small(p, names):
    flat = p.reshape(-1)
    out, off = {}, 0
    for n in names:
        size = math.prod(SMALL_SHAPES[n])
        out[n] = flat[off:off + size].reshape(SMALL_SHAPES[n])
        off += size
    return out


def _adamw_3d(w, g, m, v, *, name):
    def body(w_ref, g_ref, m_ref, v_ref, d_ref, nm_ref, nv_ref):
        _adamw_math(w_ref, g_ref, m_ref, v_ref, d_ref, nm_ref, nv_ref)

    blk = pl.BlockSpec((8,) + w.shape[1:], lambda i: (i, 0, 0))
    shp = jax.ShapeDtypeStruct(w.shape, F32)
    return pl.pallas_call(body, name=name, grid=(w.shape[0] // 8,), in_specs=[blk] * 4, out_specs=[blk] * 3,
                          out_shape=[shp] * 3, compiler_params=_params(("parallel",)))(w, g, m, v)


def kernel(x, positions, norm_w, w_in, q_norm_w, k_norm_w, sinks, a_re, a_im, log_step, b_re, b_im, c_re, c_im, d_skip, w_glu, b_glu, attn_out_norm_w, ssm_out_norm_w, w_out, loss_target, m_norm_w, m_w_in, m_q_norm_w, m_k_norm_w, m_sinks, m_a_re, m_a_im, m_log_step, m_b_re, m_b_im, m_c_re, m_c_im, m_d_skip, m_w_glu, m_b_glu, m_attn_out_norm_w, m_ssm_out_norm_w, m_w_out, v_norm_w, v_w_in, v_q_norm_w, v_k_norm_w, v_sinks, v_a_re, v_a_im, v_log_step, v_b_re, v_b_im, v_c_re, v_c_im, v_d_skip, v_w_glu, v_b_glu, v_attn_out_norm_w, v_ssm_out_norm_w, v_w_out):
    small_w = dict(norm_w=norm_w, q_norm_w=q_norm_w, k_norm_w=k_norm_w, sinks=sinks, a_re=a_re, a_im=a_im,
                   log_step=log_step, b_re=b_re, b_im=b_im, c_re=c_re, c_im=c_im, d_skip=d_skip, b_glu=b_glu,
                   attn_out_norm_w=attn_out_norm_w, ssm_out_norm_w=ssm_out_norm_w)
    small_m = dict(norm_w=m_norm_w, q_norm_w=m_q_norm_w, k_norm_w=m_k_norm_w, sinks=m_sinks, a_re=m_a_re, a_im=m_a_im,
                   log_step=m_log_step, b_re=m_b_re, b_im=m_b_im, c_re=m_c_re, c_im=m_c_im, d_skip=m_d_skip,
                   b_glu=m_b_glu, attn_out_norm_w=m_attn_out_norm_w, ssm_out_norm_w=m_ssm_out_norm_w)
    small_v = dict(norm_w=v_norm_w, q_norm_w=v_q_norm_w, k_norm_w=v_k_norm_w, sinks=v_sinks, a_re=v_a_re, a_im=v_a_im,
                   log_step=v_log_step, b_re=v_b_re, b_im=v_b_im, c_re=v_c_re, c_im=v_c_im, d_skip=v_d_skip,
                   b_glu=v_b_glu, attn_out_norm_w=v_attn_out_norm_w, ssm_out_norm_w=v_ssm_out_norm_w)
    c_idx = lax.axis_index("c").astype(jnp.int32).reshape(1)
    chip_idx = (2 * lax.axis_index("x") + lax.axis_index("y")).astype(jnp.int32).reshape(1)
    dev_idx = 2 * chip_idx + c_idx

    xs = x[0]
    tgt = loss_target[0]
    t = xs.shape[0]
    posf = positions[0].astype(F32).reshape(t, 1)

    mx, my = lax.axis_index("x"), lax.axis_index("y")
    slab_order = jnp.stack([2 * mx + my, 2 * (1 - mx) + my, 2 * mx + (1 - my), 2 * (1 - mx) + (1 - my)]).astype(jnp.int32)
    proj, hn, w_in_all = _inproj(xs, norm_w, w_in.astype(BF16), slab_order)
    inv_freq = ROPE_THETA ** (-jnp.arange(0, HEAD_DIM, 2, dtype=F32) / HEAD_DIM)
    rope = _rope_table(posf, jnp.tile(inv_freq, 4).reshape(1, 128))
    qw = jnp.tile(q_norm_w, 2).reshape(1, 128)
    kw = jnp.tile(k_norm_w, 2).reshape(1, 128)
    sink_row = sinks.reshape(1, N_HEADS)
    oa, w_glu_all, w_out_all = _attn_fwd(proj, rope, qw, kw, sink_row, [w_glu.astype(BF16), w_out.astype(BF16)])
    w_glu_b = w_glu_all.reshape(SSM_W, SSM_W)
    w_out_b = w_out_all.reshape(D_MODEL, D_MODEL)

    lam_r, lam_i, pw_r, pw_i, bb_r, bb_i = _ssm_prep(a_re, a_im, log_step, b_re, b_im, t // N_SEG)
    rows8 = lambda a: jnp.broadcast_to(a.reshape(SSM_GB, 1, SSM_ST), (SSM_GB, N_SEG, SSM_ST))
    lam_r8, lam_i8, pw_r8, pw_i8 = rows8(lam_r), rows8(lam_i), rows8(pw_r), rows8(pw_i)
    ssm_w_in = jnp.concatenate([_block_diag_in(bb_r), _block_diag_in(bb_i)], axis=1).astype(BF16)
    ssm_w_out = jnp.concatenate([_block_diag_out(c_re), _block_diag_out(-c_im)], axis=2).astype(BF16)
    d_row = d_skip.reshape(1, SSM_W)
    uz = _permute_rows(proj[:, 2560:])
    yp, hc = _ssm_fwd(uz, lam_r8, lam_i8, pw_r8, pw_i8, ssm_w_in, ssm_w_out, d_row)
    b_glu_row = b_glu.reshape(1, SSM_W)
    osp, ygp = _glu_fwd(yp, uz, w_glu_b, b_glu_row)
    os_ = _unpermute_rows(osp)
    aw = attn_out_norm_w.reshape(1, ATTN_W)
    sw = ssm_out_norm_w.reshape(1, SSM_W)
    merged, dout, sq_err = _outproj(oa, os_, aw, sw, w_out_b, xs, tgt)
    loss = lax.psum(0.5 * sq_err[0, 0] / D_MODEL, MESH_AXES)

    doa, dos, g_aw, g_sw = _outproj_bwd(dout, oa, os_, aw, sw, w_out_b)
    dout_b = dout.astype(BF16)
    (g_w_out_b,) = _matmul_tn(merged, dout_b, tm=512, tn=1024, name="grad_w_out", dtypes=(BF16,))
    dyp, dzsp, dap, g_b_glu = _glu_bwd(yp, uz, _permute_rows(dos), w_glu_b, b_glu_row)
    (g_w_glu_b,) = _matmul_tn(ygp, dap, tm=512, tn=1024, name="grad_w_glu", dtypes=(BF16,))
    dup, g_wi, g_wo, g_lam, g_d = _ssm_bwd(uz, dyp, hc, lam_r8, lam_i8, pw_r8, pw_i8, ssm_w_in, ssm_w_out, d_row)
    early = [g_w_glu_b.reshape(8, 128, SSM_W), g_w_out_b.reshape(8, 256, D_MODEL)]
    duz = _unpermute_rows(jnp.concatenate([dup, dzsp], axis=1))
    dproj, g_qw, g_kw, g_sink, *early_slots = _attn_bwd(proj, rope, qw, kw, sink_row, doa, duz, early)
    g_w_in, g_w_in_b = _matmul_tn(hn, dproj, tm=512, tn=SHARD_W, name="grad_w_in", slabs=True)
    in_shape = (4, 2, D_MODEL // 2, SHARD_W)
    (from_sib,) = _pair_swap([g_w_in_b.reshape(in_shape)])
    pair_in = _add_halves(g_w_in.reshape(in_shape), from_sib, c_idx, rows=128, name="pair_sum")
    grad_x, g_nw, in_slots = _inproj_bwd(dproj, w_in_all, xs, norm_w, dout, [pair_in])

    g_wi = g_wi.reshape(SSM_G, SSM_H, 2 * SSM_P)
    g_wo = g_wo.reshape(SSM_G, SSM_H, 2 * SSM_P)
    g_bb_r = g_wi[:, :, :SSM_P].transpose(0, 2, 1).reshape(SSM_G, SSM_P * SSM_H)
    g_bb_i = g_wi[:, :, SSM_P:].transpose(0, 2, 1).reshape(SSM_G, SSM_P * SSM_H)
    g_a_re, g_a_im, g_ls, g_b_re, g_b_im = _ssm_param_grads(
        a_re, a_im, log_step, b_re, b_im, g_lam[:, 0, :SSM_ST].reshape(SSM_G, SSM_P),
        g_lam[:, 0, SSM_ST:].reshape(SSM_G, SSM_P), g_bb_r, g_bb_i)
    small_g = dict(
        norm_w=g_nw, q_norm_w=g_qw[0, :64] + g_qw[0, 64:], k_norm_w=g_kw[0, :64] + g_kw[0, 64:],
        sinks=g_sink[0, :N_HEADS], a_re=g_a_re, a_im=g_a_im, log_step=g_ls, b_re=g_b_re, b_im=g_b_im,
        c_re=g_wo[:, :, :SSM_P], c_im=-g_wo[:, :, SSM_P:], d_skip=g_d,
        b_glu=g_b_glu, attn_out_norm_w=g_aw, ssm_out_norm_w=g_sw)

    mine = [_sum_peers(in_slots, pair_in, chip_idx, rows=128, name="sum_w_in"),
            _sum_peers(early_slots[0], early[0], dev_idx, rows=128, name="sum_w_glu"),
            _sum_peers(early_slots[1], early[1], dev_idx, rows=128, name="sum_w_out")]
    theirs = _half_swap(mine)
    packed = _pack_small(small_g, SMALL, 8 * PACK_ROWS).reshape(8, PACK_ROWS, 128)
    summed = _sum_slots(_exchange_slices(packed, True, "small_scatter"), rows=PACK_ROWS, name="small_sum")
    small_red = _exchange_slices(summed, False, "small_gather").reshape(8 * PACK_ROWS, 128)

    big = [_adamw_halves(w_in, mine[0], theirs[0], m_w_in, v_w_in, c_idx, rows=256, name="adamw_w_in"),
           _adamw_halves(w_glu, mine[1], theirs[1], m_w_glu, v_w_glu, c_idx, rows=128, name="adamw_w_glu"),
           _adamw_halves(w_out, mine[2], theirs[2], m_w_out, v_w_out, c_idx, rows=256, name="adamw_w_out")]
    g_in_sh, g_glu_sh, g_out_sh = (b[0] for b in big)
    upd = [b[1:] for b in big]
    grads = _unpack_small(small_red, SMALL)
    flat_first = sum(math.prod(SMALL_SHAPES[n]) for n in SMALL_3D) // 128
    sd, sm, sv = _adamw(_pack_small(small_w, SMALL_FLAT, FLAT_ROWS), small_red[flat_first:flat_first + FLAT_ROWS],
                        _pack_small(small_m, SMALL_FLAT, FLAT_ROWS), _pack_small(small_v, SMALL_FLAT, FLAT_ROWS),
                        rows=FLAT_ROWS, name="adamw_small")
    deltas, new_m, new_v = (_unpack_small(a, SMALL_FLAT) for a in (sd, sm, sv))
    for n in SMALL_3D:
        deltas[n], new_m[n], new_v[n] = _adamw_3d(small_w[n], grads[n], small_m[n], small_v[n], name="adamw_" + n)
    grads.update(w_in=g_in_sh, w_glu=g_glu_sh, w_out=g_out_sh)
    for n, (d, m_, v_) in zip(("w_in", "w_glu", "w_out"), upd):
        deltas[n], new_m[n], new_v[n] = d, m_, v_
    order = ["norm_w", "w_in", "q_norm_w", "k_norm_w", "sinks", "a_re", "a_im", "log_step", "b_re", "b_im", "c_re",
             "c_im", "d_skip", "w_glu", "b_glu", "attn_out_norm_w", "ssm_out_norm_w", "w_out"]
    return (loss, grad_x[None], *[grads[n] for n in order], *[deltas[n] for n in order],
            *[new_m[n] for n in order], *[new_v[n] for n in order])
```

```python
import math

import jax
import jax.numpy as jnp
from jax import lax
from jax.experimental import pallas as pl
from jax.experimental.pallas import tpu as pltpu

F32 = jnp.float32
BF16 = jnp.bfloat16

D_MODEL = 2048
ATTN_W = 1024
SSM_W = 1024
HEAD_DIM = 64
N_HEADS = 16
N_KV_HEADS = 4
KV_W = 256
BLOCK = 128
IN_W = 4608
SHARD_W = IN_W // 4
ROPE_THETA = 10000.0
SSM_H = 16
SSM_G = 64
SSM_P = 64
NORM_EPS = 1e-6
ADAM_LR = 0.001
ADAM_B1 = 0.9
ADAM_B2 = 0.999
ADAM_EPS = 1e-08
ADAM_WD = 0.01
ADAM_STEP = 10

N_SEG = 8
SSM_GB = 4
SSM_CH = 256
SSM_ST = 1024
SCAN_ROWS = 256
SCAN_LW = 512
VMEM_LIMIT = 56 * 1024 * 1024
MESH_AXES = ("x", "y", "c")
ANY = pl.BlockSpec(memory_space=pl.ANY)

SMALL_3D = ("b_re", "b_im", "c_re", "c_im")
SMALL_FLAT = ("norm_w", "q_norm_w", "k_norm_w", "sinks", "a_re", "a_im", "log_step", "d_skip", "b_glu",
              "attn_out_norm_w", "ssm_out_norm_w")
SMALL = SMALL_3D + SMALL_FLAT
SMALL_SHAPES = {"norm_w": (2048,), "q_norm_w": (64,), "k_norm_w": (64,), "sinks": (16,), "a_re": (64, 64),
                "a_im": (64, 64), "log_step": (64,), "b_re": (64, 64, 16), "b_im": (64, 64, 16),
                "c_re": (64, 16, 64), "c_im": (64, 16, 64), "d_skip": (1024,), "b_glu": (1024,),
                "attn_out_norm_w": (1024,), "ssm_out_norm_w": (1024,)}
PACK_ROWS = 272
FLAT_ROWS = 120


def _params(sem=None):
    return pltpu.CompilerParams(dimension_semantics=sem, vmem_limit_bytes=VMEM_LIMIT)


def _dot(a, b):
    return jnp.dot(a, b, preferred_element_type=F32)


def _dot_nt(a, b):
    return lax.dot_general(a, b, (((1,), (1,)), ((), ())), preferred_element_type=F32)


def _dot_tn(a, b):
    return lax.dot_general(a, b, (((0,), (0,)), ((), ())), preferred_element_type=F32)


def _sigmoid(x):
    return 1.0 / (1.0 + jnp.exp(-x))


def _silu(x):
    return x * _sigmoid(x)


def _dsilu(x):
    s = _sigmoid(x)
    return s * (1.0 + x * (1.0 - s))


_GELU_C = math.sqrt(2.0 / math.pi)


def _gelu(x):
    return 0.5 * x * (1.0 + jnp.tanh(_GELU_C * (x + 0.044715 * x * x * x)))


def _dgelu(x):
    t = jnp.tanh(_GELU_C * (x + 0.044715 * x * x * x))
    return 0.5 * (1.0 + t) + 0.5 * x * (1.0 - t * t) * _GELU_C * (1.0 + 3.0 * 0.044715 * x * x)


def _matmul_tn(a, b, *, tm, tn, name, slabs=False, dtypes=(F32, BF16)):
    k, m = a.shape
    _, n = b.shape

    def body(a_ref, b_ref, *o_refs):
        acc = _dot_tn(a_ref[...], b_ref[...])
        for o_ref in o_refs:
            o_ref[...] = acc.astype(o_ref.dtype)

    if slabs:
        out_spec = pl.BlockSpec((None, tm, tn), lambda j, i: (j, i, 0))
        shape = (n // tn, m, tn)
    else:
        out_spec = pl.BlockSpec((tm, tn), lambda j, i: (i, j))
        shape = (m, n)
    return pl.pallas_call(
        body, name=name, grid=(n // tn, m // tm),
        in_specs=[pl.BlockSpec((k, tm), lambda j, i: (0, i)), pl.BlockSpec((k, tn), lambda j, i: (0, j))],
        out_specs=[out_spec] * len(dtypes),
        out_shape=[jax.ShapeDtypeStruct(shape, d) for d in dtypes],
        compiler_params=_params(("parallel", "parallel")),
    )(a, b)


def _inproj(x, norm_w, w_sh, order):
    t = x.shape[0]
    tm = 512
    ni = t // tm
    hr = D_MODEL // 2

    def body(ord_ref, x_ref, nw_ref, sh_ref, proj_ref, hn_ref, full_ref, wbuf, hn_s, ssem, rsem, lsem):
        s, i = pl.program_id(0), pl.program_id(1)
        mx, my, c = _mesh_pos()
        me = 2 * mx + my
        sib = (mx, my, 1 - c)
        chips = _other_chips(mx, my)

        def half(which):
            return pl.ds(pl.multiple_of(which * hr, 8), hr)

        def slot(k):
            return 2 * chips[k][0] + chips[k][1]

        def ici(k):
            return _remote(sh_ref.at[half(c)], full_ref.at[me, half(c)], ssem.at[k], rsem.at[k], (*chips[k], c))

        def own():
            return _remote(sh_ref, full_ref.at[me], ssem.at[6], rsem.at[6], sib)

        def landed(k, which, sem):
            ref = full_ref.at[slot(k), half(which)]
            return _remote(ref, ref, ssem.at[sem], rsem.at[sem], sib)

        def fetch(src, b):
            return pltpu.make_async_copy(src, wbuf.at[b], lsem.at[b])

        @pl.when((s == 0) & (i == 0))
        def _():
            for k in range(3):
                ici(k).start()
            own().start()
            cp = fetch(sh_ref, 0)
            cp.start()
            cp.wait()

        for k in range(3):
            @pl.when((s == k) & (i == max(ni - 2, 0)))
            def _(k=k):
                landed(k, c, k).wait_recv()
                landed(k, c, 3 + k).start()
                landed(k, 1 - c, 3 + k).wait_recv()
                fetch(full_ref.at[slot(k)], (k + 1) % 2).start()

            @pl.when((s == k + 1) & (i == 0))
            def _(k=k):
                fetch(full_ref.at[slot(k)], (k + 1) % 2).wait()

        xv = x_ref[...]
        r = lax.rsqrt(jnp.mean(xv * xv, axis=1, keepdims=True) + NORM_EPS)
        hn = (xv * r * nw_ref[...]).astype(BF16)
        proj_ref[...] = _dot(hn, wbuf[s % 2])

        def hn_out(tile):
            return pltpu.make_async_copy(hn_s, hn_ref.at[pl.ds(pl.multiple_of(tile * tm, tm), tm), :], lsem.at[2])

        @pl.when(((s == 0) & (i > 0)) | ((s == 1) & (i == 0)))
        def _():
            hn_out(jnp.where(s == 0, i - 1, ni - 1)).wait()

        @pl.when(s == 0)
        def _():
            hn_s[...] = hn
            hn_out(i).start()

        @pl.when((s == 3) & (i == ni - 1))
        def _():
            mine = full_ref.at[me]
            _remote(mine, mine, ssem.at[6], rsem.at[6], sib).wait_recv()
            for k in range(3):
                ici(k).wait_send()
                landed(k, c, 3 + k).wait_send()
            own().wait_send()

    return pl.pallas_call(
        body, name="inproj",
        grid_spec=pltpu.PrefetchScalarGridSpec(
            num_scalar_prefetch=1, grid=(4, ni),
            in_specs=[pl.BlockSpec((tm, D_MODEL), lambda s, i, o: (i, 0)),
                      pl.BlockSpec((1, D_MODEL), lambda s, i, o: (0, 0)), ANY],
            out_specs=[pl.BlockSpec((tm, SHARD_W), lambda s, i, o: (i, o[s])), ANY, ANY],
            scratch_shapes=[pltpu.VMEM((2, D_MODEL, SHARD_W), BF16), pltpu.VMEM((tm, D_MODEL), BF16),
                            pltpu.SemaphoreType.DMA((7,)), pltpu.SemaphoreType.DMA((7,)),
                            pltpu.SemaphoreType.DMA((3,))]),
        out_shape=[jax.ShapeDtypeStruct((t, IN_W), F32), jax.ShapeDtypeStruct((t, D_MODEL), BF16),
                   jax.ShapeDtypeStruct((4, D_MODEL, SHARD_W), BF16)],
        compiler_params=_params(("arbitrary", "arbitrary")),
    )(order, x, norm_w.reshape(1, D_MODEL), w_sh)


def _lane128():
    return lax.broadcasted_iota(jnp.int32, (1, 128), 1)


def _head_sums(v):
    lo = _lane128() < 64
    s_lo = jnp.sum(jnp.where(lo, v, 0.0), axis=1, keepdims=True)
    s_hi = jnp.sum(jnp.where(lo, 0.0, v), axis=1, keepdims=True)
    return jnp.where(lo, s_lo, s_hi)


def _rot_half(t):
    first = (_lane128() % 64) < 32
    return jnp.where(first, -pltpu.roll(t, 96, 1), pltpu.roll(t, 32, 1))


def _head_rstd(t):
    return lax.rsqrt(_head_sums(t * t) * (1.0 / HEAD_DIM) + NORM_EPS)


def _prep_tile(t, w, cos, sin, r=None):
    r = _head_rstd(t) if r is None else r
    tn = t * r * w
    return tn * cos + _rot_half(tn) * sin


def _prep_tile_bwd(t, w, cos, sin, g, r=None):
    r = _head_rstd(t) if r is None else r
    d_tn = g * cos - _rot_half(g * sin)
    th = t * r
    dw = jnp.sum(d_tn * th, axis=0, keepdims=True)
    gh = d_tn * w
    m = _head_sums(gh * th) * (1.0 / HEAD_DIM)
    return r * (gh - th * m), dw


def _band_mask(n):
    qi = lax.broadcasted_iota(jnp.int32, (BLOCK, 2 * BLOCK), 0) + BLOCK
    ki = lax.broadcasted_iota(jnp.int32, (BLOCK, 2 * BLOCK), 1)
    rel = qi - ki
    return (rel >= 0) & (rel < BLOCK) & ((n > 0) | (ki >= BLOCK))


def _half_select(tile, half):
    lo = _lane128() < 64
    return jnp.where(lo if half == 0 else jnp.logical_not(lo), tile, 0.0)


def _stack_group(tiles, kv_half):
    rows = []
    for t in tiles:
        for half in range(2):
            piece = _half_select(t, half)
            rows.append(piece if half == kv_half else pltpu.roll(piece, 64, 1))
    return jnp.concatenate(rows, axis=0)


def _unstack_group(stacked, kv_half):
    tiles = []
    for i in range(2):
        acc = None
        for half in range(2):
            piece = _half_select(stacked[BLOCK * (2 * i + half):BLOCK * (2 * i + half + 1)], kv_half)
            piece = piece if half == kv_half else pltpu.roll(piece, 64, 1)
            acc = piece if acc is None else acc + piece
        tiles.append(acc)
    return tiles


def _attn_specs(nb):
    last = nb - 1
    qi = lambda n: (jnp.minimum(n, last), 0)
    prev = lambda n: jnp.maximum(n - 1, 0)
    cur = lambda n: jnp.minimum(n, last)
    specs = [
        pl.BlockSpec((BLOCK, ATTN_W), qi),
        pl.BlockSpec((BLOCK, KV_W), lambda n: (cur(n), 4)),
        pl.BlockSpec((BLOCK, KV_W), lambda n: (prev(n), 4)),
        pl.BlockSpec((BLOCK, KV_W), lambda n: (cur(n), 5)),
        pl.BlockSpec((BLOCK, KV_W), lambda n: (prev(n), 5)),
        pl.BlockSpec((BLOCK, 512), lambda n: (cur(n), 3)),
        pl.BlockSpec((BLOCK, 512), lambda n: (cur(n), 4)),
        pl.BlockSpec((BLOCK, 256), lambda n: (cur(n), 0)),
        pl.BlockSpec((BLOCK, 256), lambda n: (prev(n), 0)),
        pl.BlockSpec((1, 128), lambda n: (0, 0)),
        pl.BlockSpec((1, 128), lambda n: (0, 0)),
        pl.BlockSpec((1, N_HEADS), lambda n: (0, 0)),
    ]
    return specs


def _rope_table(posf, invf):
    t = posf.shape[0]

    def body(p_ref, f_ref, o_ref):
        ang = p_ref[...] * f_ref[...]
        o_ref[...] = jnp.concatenate([jnp.cos(ang), jnp.sin(ang)], axis=1)

    return pl.pallas_call(
        body, name="rope_table", grid=(t // 512,),
        in_specs=[pl.BlockSpec((512, 1), lambda i: (i, 0)), pl.BlockSpec((1, 128), lambda i: (0, 0))],
        out_specs=pl.BlockSpec((512, 256), lambda i: (i, 0)),
        out_shape=jax.ShapeDtypeStruct((t, 256), F32), compiler_params=_params(("parallel",)),
    )(posf, invf)


def _attn_common(n, q_ref, kc_ref, kp_ref, vc_ref, vp_ref, rq_ref, rp_ref, qw_ref, kw_ref):
    cos_q, sin_q = rq_ref[:, 0:128], rq_ref[:, 128:256]
    cos_k = jnp.concatenate([rp_ref[:, 0:128], cos_q], axis=0)
    sin_k = jnp.concatenate([rp_ref[:, 128:256], sin_q], axis=0)
    k_raw = jnp.concatenate([kp_ref[...], kc_ref[...]], axis=0)
    vv = jnp.concatenate([vp_ref[...], vc_ref[...]], axis=0).astype(BF16)
    kk = [_prep_tile(k_raw[:, 128 * i:128 * i + 128], kw_ref[...], cos_k, sin_k).astype(BF16) for i in range(2)]
    vt = [vv[:, 128 * i:128 * i + 128] for i in range(2)]
    qv = q_ref[...]
    qr = [_head_rstd(qv[:, 128 * i:128 * i + 128]) for i in range(8)]
    qt = [_prep_tile(qv[:, 128 * i:128 * i + 128], qw_ref[...], cos_q, sin_q, qr[i]) for i in range(8)]
    return cos_q, sin_q, qr, kk, vt, qt


QK_SCALE = 1.0 / math.sqrt(HEAD_DIM)


def _group_sinks(sink_ref, g):
    return jnp.concatenate([jnp.broadcast_to(sink_ref[:, 4 * g + j:4 * g + j + 1], (BLOCK, 1)) for j in range(4)], axis=0)


def _group_softmax(q4, kk_t, sink, bias):
    s = _dot_nt(q4, kk_t) + bias
    m = jnp.maximum(jnp.max(s, axis=1, keepdims=True), sink)
    p = jnp.exp(s - m)
    es = jnp.exp(sink - m)
    inv = 1.0 / (jnp.sum(p, axis=1, keepdims=True) + es)
    return p * inv, es * inv


def _group_bias(n):
    return jnp.concatenate([jnp.where(_band_mask(n), 0.0, -1e30)] * 4, axis=0)


def _attn_fwd(proj, rope, qw, kw, sinks, later_shards):
    t = proj.shape[0]
    nb = t // BLOCK
    nt = len(later_shards)

    def body(q_ref, kc_ref, kp_ref, vc_ref, vp_ref, za0_ref, za1_ref, rq_ref, rp_ref, qw_ref, kw_ref,
             sink_ref, *rest):
        sh, o_ref, full = rest[:nt], rest[nt], rest[nt + 1:2 * nt + 1]
        ssem, rsem = rest[2 * nt + 1:]
        n = pl.program_id(0)
        start, wait = _bg_gather(sh, full, ssem, rsem)

        @pl.when(n == 0)
        def _():
            start()

        _, _, _, kk, vt, qt = _attn_common(n, q_ref, kc_ref, kp_ref, vc_ref, vp_ref, rq_ref, rp_ref, qw_ref, kw_ref)
        bias = _group_bias(n)
        tiles = []
        for g in range(N_KV_HEADS):
            q4 = (_stack_group(qt[2 * g:2 * g + 2], g % 2) * QK_SCALE).astype(BF16)
            p, _ = _group_softmax(q4, kk[g // 2], _group_sinks(sink_ref, g), bias)
            tiles += _unstack_group(_dot(p.astype(BF16), vt[g // 2]), g % 2)
        za = jnp.concatenate([za0_ref[...], za1_ref[...]], axis=1)
        o_ref[...] = jnp.concatenate(tiles, axis=1) * _silu(za)

        @pl.when(n == nb - 1)
        def _():
            wait()

    return pl.pallas_call(
        body, name="attn_fwd", grid=(nb,), in_specs=_attn_specs(nb) + [ANY] * nt,
        out_specs=[pl.BlockSpec((BLOCK, ATTN_W), lambda n: (n, 0))] + [ANY] * nt,
        out_shape=[jax.ShapeDtypeStruct((t, ATTN_W), F32)]
        + [jax.ShapeDtypeStruct((4,) + s.shape, s.dtype) for s in later_shards],
        scratch_shapes=[pltpu.SemaphoreType.DMA((4 * nt,)), pltpu.SemaphoreType.DMA((4 * nt,))],
        compiler_params=_params(("arbitrary",)),
    )(proj, proj, proj, proj, proj, proj, proj, rope, rope, qw, kw, sinks, *later_shards)


def _attn_bwd(proj, rope, qw, kw, sinks, doa, du, dzs, outgoing):
    t = proj.shape[0]
    nb = t // BLOCK
    last = nb - 1
    nt = len(outgoing)

    def body(q_ref, kc_ref, kp_ref, vc_ref, vp_ref, za0_ref, za1_ref, rq_ref, rp_ref, qw_ref, kw_ref,
             sink_ref, doa_ref, du_ref, dzs_ref, *rest):
        src = rest[:nt]
        dp_ref, gq_ref, gk_ref, gs_ref = rest[nt:nt + 4]
        dst = rest[nt + 4:2 * nt + 4]
        dkk_s, dvv_s, ck_s, cv_s, dq_s, dza_s, ssem, rsem = rest[2 * nt + 4:]
        n = pl.program_id(0)
        start, wait = _bg_scatter_devices(src, dst, ssem, rsem)

        @pl.when(n == 0)
        def _():
            start()
            gq_ref[...] = jnp.zeros_like(gq_ref)
            gk_ref[...] = jnp.zeros_like(gk_ref)
            gs_ref[...] = jnp.zeros_like(gs_ref)
            ck_s[...] = jnp.zeros_like(ck_s)
            cv_s[...] = jnp.zeros_like(cv_s)
            dq_s[...] = jnp.zeros_like(dq_s)
            dza_s[...] = jnp.zeros_like(dza_s)

        dp_ref[:, 0:ATTN_W] = dq_s[...]
        dp_ref[:, ATTN_W + 2 * KV_W:2 * ATTN_W + 2 * KV_W] = dza_s[...]
        dp_ref[:, 2 * ATTN_W + 2 * KV_W:IN_W - SSM_W] = du_ref[...]
        dp_ref[:, IN_W - SSM_W:IN_W] = dzs_ref[...]

        @pl.when(n == nb)
        def _():
            dkk_s[...] = jnp.zeros_like(dkk_s)
            dvv_s[...] = jnp.zeros_like(dvv_s)

        @pl.when(n < nb)
        def _():
            cos_q, sin_q, qr, kk, vt, qt = _attn_common(n, q_ref, kc_ref, kp_ref, vc_ref, vp_ref, rq_ref, rp_ref,
                                                        qw_ref, kw_ref)
            bias = _group_bias(n)
            za = jnp.concatenate([za0_ref[...], za1_ref[...]], axis=1)
            doa_v = doa_ref[...]
            do_full = doa_v * _silu(za)
            o_tiles, dq_tiles = [], []
            dkk = [jnp.zeros((2 * BLOCK, 128), F32) for _ in range(2)]
            dvv = [jnp.zeros((2 * BLOCK, 128), F32) for _ in range(2)]
            gsink = jnp.zeros((1, 128), F32)
            lane = _lane128()
            for g in range(N_KV_HEADS):
                q_b = (_stack_group(qt[2 * g:2 * g + 2], g % 2) * QK_SCALE).astype(BF16)
                do_b = _stack_group([do_full[:, 128 * i:128 * i + 128] for i in (2 * g, 2 * g + 1)], g % 2).astype(BF16)
                p, psink = _group_softmax(q_b, kk[g // 2], _group_sinks(sink_ref, g), bias)
                p_b = p.astype(BF16)
                dp = _dot_nt(do_b, vt[g // 2])
                delta = jnp.sum(p * dp, axis=1, keepdims=True)
                ds_b = (p * (dp - delta)).astype(BF16)
                sd = psink * delta
                for j in range(4):
                    gsink = gsink + jnp.where(lane == 4 * g + j, -jnp.sum(sd[BLOCK * j:BLOCK * (j + 1)]), 0.0)
                o_tiles += _unstack_group(_dot(p_b, vt[g // 2]), g % 2)
                dq_tiles += [d * QK_SCALE for d in _unstack_group(_dot(ds_b, kk[g // 2]), g % 2)]
                dkk[g // 2] = dkk[g // 2] + _dot_tn(ds_b, q_b)
                dvv[g // 2] = dvv[g // 2] + _dot_tn(p_b, do_b)
            dza_s[...] = (doa_v * jnp.concatenate(o_tiles, axis=1) * _dsilu(za)).astype(BF16)
            qv = q_ref[...]
            gq = jnp.zeros((1, 128), F32)
            out = []
            for i in range(8):
                d, dw = _prep_tile_bwd(qv[:, 128 * i:128 * i + 128], qw_ref[...], cos_q, sin_q, dq_tiles[i], qr[i])
                out.append(d)
                gq = gq + dw
            dq_s[...] = jnp.concatenate(out, axis=1).astype(BF16)
            gq_ref[...] += gq
            gs_ref[...] += gsink
            dkk_s[...] = jnp.concatenate(dkk, axis=1)
            dvv_s[...] = jnp.concatenate(dvv, axis=1)

        cos_p, sin_p = rp_ref[:, 0:128], rp_ref[:, 128:256]
        dk_prev = ck_s[...] + dkk_s[0:BLOCK, :]
        kp = kp_ref[...]
        gk = jnp.zeros((1, 128), F32)
        out = []
        for i in range(2):
            d, dw = _prep_tile_bwd(kp[:, 128 * i:128 * i + 128], kw_ref[...], cos_p, sin_p,
                                   dk_prev[:, 128 * i:128 * i + 128])
            out.append(d)
            gk = gk + dw
        dp_ref[:, ATTN_W:ATTN_W + KV_W] = jnp.concatenate(out, axis=1).astype(BF16)
        dp_ref[:, ATTN_W + KV_W:ATTN_W + 2 * KV_W] = (cv_s[...] + dvv_s[0:BLOCK, :]).astype(BF16)
        gk_ref[...] += gk
        ck_s[...] = dkk_s[BLOCK:2 * BLOCK, :]
        cv_s[...] = dvv_s[BLOCK:2 * BLOCK, :]

        @pl.when(n == nb)
        def _():
            wait()

    qblk = lambda n: (jnp.minimum(n, last), 0)
    kblk = lambda n: (jnp.maximum(n - 1, 0), 0)
    vec = pl.BlockSpec((1, 128), lambda n: (0, 0))
    return pl.pallas_call(
        body, name="attn_bwd", grid=(nb + 1,),
        in_specs=_attn_specs(nb) + [pl.BlockSpec((BLOCK, ATTN_W), qblk), pl.BlockSpec((BLOCK, SSM_W), kblk),
                                    pl.BlockSpec((BLOCK, SSM_W), kblk)] + [ANY] * nt,
        out_specs=[pl.BlockSpec((BLOCK, IN_W), kblk), vec, vec, vec] + [ANY] * nt,
        out_shape=[jax.ShapeDtypeStruct((t, IN_W), BF16), jax.ShapeDtypeStruct((1, 128), F32),
                   jax.ShapeDtypeStruct((1, 128), F32), jax.ShapeDtypeStruct((1, 128), F32)]
        + [jax.ShapeDtypeStruct(a.shape, a.dtype) for a in outgoing],
        scratch_shapes=[pltpu.VMEM((2 * BLOCK, KV_W), F32), pltpu.VMEM((2 * BLOCK, KV_W), F32),
                        pltpu.VMEM((BLOCK, KV_W), F32), pltpu.VMEM((BLOCK, KV_W), F32),
                        pltpu.VMEM((BLOCK, ATTN_W), BF16), pltpu.VMEM((BLOCK, ATTN_W), BF16),
                        pltpu.SemaphoreType.DMA((7 * nt,)), pltpu.SemaphoreType.DMA((7 * nt,))],
        compiler_params=_params(("arbitrary",)),
    )(proj, proj, proj, proj, proj, proj, proj, rope, rope, qw, kw, sinks, doa, du, dzs, *outgoing)


def _cmul(ar, ai, br, bi):
    return ar * br - ai * bi, ar * bi + ai * br


def _zoh(a_re, a_im, delta):
    e = jnp.exp(a_re * delta)
    lr, li = e * jnp.cos(a_im * delta), e * jnp.sin(a_im * delta)
    inv = 1.0 / (a_re * a_re + a_im * a_im)
    fr, fi = _cmul(lr - 1.0, li, a_re * inv, -a_im * inv)
    return lr, li, fr, fi


def _ssm_prep(a_re, a_im, log_step, b_re, b_im, seg_len):
    n_sq = int(round(math.log2(seg_len)))
    assert 2 ** n_sq == seg_len

    def body(ar_ref, ai_ref, ls_ref, arx_ref, aix_ref, br_ref, bi_ref, lr_ref, li_ref, pr_ref, pi_ref, bbr_ref, bbi_ref):
        delta = jnp.exp(ls_ref[...])
        lr, li, _, _ = _zoh(ar_ref[...], ai_ref[...], delta)
        lr_ref[...] = lr
        li_ref[...] = li
        pr, pi = lr, li
        for _ in range(n_sq):
            pr, pi = _cmul(pr, pi, pr, pi)
        pr_ref[...] = pr
        pi_ref[...] = pi
        _, _, fr, fi = _zoh(arx_ref[...], aix_ref[...], delta)
        bbr, bbi = _cmul(fr, fi, br_ref[...], bi_ref[...])
        bbr_ref[...] = bbr
        bbi_ref[...] = bbi

    gp = jax.ShapeDtypeStruct((SSM_G, SSM_P), F32)
    gx = jax.ShapeDtypeStruct((SSM_G, SSM_P * SSM_H), F32)
    return pl.pallas_call(body, name="ssm_prep", out_shape=[gp, gp, gp, gp, gx, gx])(
        a_re, a_im, log_step.reshape(SSM_G, 1), jnp.repeat(a_re, SSM_H, axis=1), jnp.repeat(a_im, SSM_H, axis=1),
        b_re.reshape(SSM_G, SSM_P * SSM_H), b_im.reshape(SSM_G, SSM_P * SSM_H))


def _ssm_param_grads(a_re, a_im, log_step, b_re, b_im, dlam_re, dlam_im, dbb_re, dbb_im):
    def body(ar_ref, ai_ref, ls_ref, arx_ref, aix_ref, br_ref, bi_ref, dlr_ref, dli_ref, dbr_ref, dbi_ref,
             gar_ref, gai_ref, gls_ref, gbr_ref, gbi_ref):
        delta = jnp.exp(ls_ref[...])
        ar, ai = ar_ref[...], ai_ref[...]
        lr, li, fr, fi = _zoh(ar, ai, delta)
        _, _, frx, fix = _zoh(arx_ref[...], aix_ref[...], delta)
        dbr, dbi = dbr_ref[...], dbi_ref[...]
        br, bi = br_ref[...], bi_ref[...]
        gbr, gbi = _cmul(frx, -fix, dbr, dbi)
        gbr_ref[...] = gbr
        gbi_ref[...] = gbi
        tr, ti = _cmul(br, -bi, dbr, dbi)
        row = lax.broadcasted_iota(jnp.int32, (SSM_P * SSM_H, SSM_P), 0)
        col = lax.broadcasted_iota(jnp.int32, (SSM_P * SSM_H, SSM_P), 1)
        fold = (row // SSM_H == col).astype(F32)
        dfr = jnp.dot(tr, fold, precision=lax.Precision.HIGHEST, preferred_element_type=F32)
        dfi = jnp.dot(ti, fold, precision=lax.Precision.HIGHEST, preferred_element_type=F32)
        inv = 1.0 / (ar * ar + ai * ai)
        ilr, ili = ar * inv, -ai * inv
        t1r, t1i = _cmul(dfr, dfi, ilr, -ili)
        dlbr, dlbi = dlr_ref[...] + t1r, dli_ref[...] + t1i
        qr, qi = _cmul(fr, fi, ilr, ili)
        t2r, t2i = _cmul(dfr, dfi, qr, -qi)
        glr, gli = -t2r, -t2i
        dzr, dzi = _cmul(dlbr, dlbi, lr, -li)
        gar_ref[...] = glr + dzr * delta
        gai_ref[...] = gli + dzi * delta
        gls_ref[...] = jnp.sum(dzr * ar + dzi * ai, axis=1, keepdims=True) * delta

    gp = jax.ShapeDtypeStruct((SSM_G, SSM_P), F32)
    gx = jax.ShapeDtypeStruct((SSM_G, SSM_P * SSM_H), F32)
    return pl.pallas_call(body, name="ssm_param_grads",
                          out_shape=[gp, gp, jax.ShapeDtypeStruct((SSM_G, 1), F32), gx, gx])(
        a_re, a_im, log_step.reshape(SSM_G, 1), jnp.repeat(a_re, SSM_H, axis=1), jnp.repeat(a_im, SSM_H, axis=1),
        b_re.reshape(SSM_G, SSM_P * SSM_H), b_im.reshape(SSM_G, SSM_P * SSM_H), dlam_re, dlam_im, dbb_re, dbb_im)


def _block_diag_in(bb):
    w = jnp.tile(bb.reshape(SSM_GB, SSM_ST, SSM_H), (1, 1, 16))
    row = lax.broadcasted_iota(jnp.int32, (1, SSM_ST, SSM_CH), 1) // SSM_P
    col = lax.broadcasted_iota(jnp.int32, (1, SSM_ST, SSM_CH), 2) // SSM_H
    return jnp.where(row == col, w, 0.0)


def _block_diag_out(c):
    w = jnp.tile(c.reshape(SSM_GB, SSM_CH, SSM_P), (1, 1, 16))
    row = lax.broadcasted_iota(jnp.int32, (1, SSM_CH, SSM_ST), 1) // SSM_H
    col = lax.broadcasted_iota(jnp.int32, (1, SSM_CH, SSM_ST), 2) // SSM_P
    return jnp.where(row == col, w, 0.0)


SEG_ROWS = SCAN_ROWS // N_SEG


def _chunk_perm():
    out_row = lax.broadcasted_iota(jnp.int32, (SCAN_ROWS, SCAN_ROWS), 0)
    in_row = lax.broadcasted_iota(jnp.int32, (SCAN_ROWS, SCAN_ROWS), 1)
    return (out_row == N_SEG * (in_row % SEG_ROWS) + in_row // SEG_ROWS).astype(BF16)


def _chunk_rows(j, seg_len, s):
    return pl.ds(pl.multiple_of(s * seg_len + j * SEG_ROWS, SEG_ROWS), SEG_ROWS)


def _gather_chunk(ref, j, seg_len):
    return jnp.concatenate([ref[_chunk_rows(j, seg_len, s), :] for s in range(N_SEG)], axis=0)


def _scatter_chunk(ref, j, seg_len, val):
    for s in range(N_SEG):
        ref[_chunk_rows(j, seg_len, s), :] = val[s * SEG_ROWS:(s + 1) * SEG_ROWS]


def _interleave(perm, x_b):
    return _dot(perm, x_b).astype(BF16)


def _scan_fwd(src_ref, dst_ref, lam_r_ref, lam_i_ref, init_ref, final_ref, steps):
    for k in range(SSM_ST // SCAN_LW):
        re = pl.ds(k * SCAN_LW, SCAN_LW)
        im = pl.ds(SSM_ST + k * SCAN_LW, SCAN_LW)
        lr, li = lam_r_ref[:, re], lam_i_ref[:, re]

        def step(i, carry, re=re, im=im, lr=lr, li=li):
            hr, hi = carry
            rows = pl.ds(pl.multiple_of(i * 8, 8), 8)
            nr = lr * hr - li * hi + src_ref[rows, re]
            ni = lr * hi + li * hr + src_ref[rows, im]
            if dst_ref is not None:
                dst_ref[rows, re] = nr
                dst_ref[rows, im] = ni
            return nr, ni

        hr, hi = lax.fori_loop(0, steps, step, (init_ref[:, re], init_ref[:, im]), unroll=4)
        final_ref[:, re] = hr
        final_ref[:, im] = hi


def _ssm_specs(t):
    col = lambda g: (0, g)
    gb3 = lambda g: (g, 0, 0)
    return dict(
        rows=pl.BlockSpec((t, SSM_CH), col),
        lam=pl.BlockSpec((None, N_SEG, SSM_ST), gb3),
        w_in=pl.BlockSpec((None, 2 * SSM_ST, SSM_CH), gb3),
        w_out=pl.BlockSpec((None, SSM_CH, 2 * SSM_ST), gb3),
        vec=pl.BlockSpec((1, SSM_CH), col),
    )


def _segment_states(x_ref, pw_r_ref, pw_i_ref, out_ref, reverse):
    re, im = pl.ds(0, SSM_ST), pl.ds(SSM_ST, SSM_ST)
    pr, pi = pw_r_ref[0:1, :], pw_i_ref[0:1, :]
    first = N_SEG - 1 if reverse else 0
    out_ref[first:first + 1, :] = jnp.zeros((1, 2 * SSM_ST), F32)
    order = range(N_SEG - 1, 0, -1) if reverse else range(N_SEG - 1)
    for s in order:
        d = s - 1 if reverse else s + 1
        hr, hi = out_ref[s:s + 1, re], out_ref[s:s + 1, im]
        if reverse:
            nr, ni = pr * hr + pi * hi, pr * hi - pi * hr
        else:
            nr, ni = pr * hr - pi * hi, pr * hi + pi * hr
        out_ref[d:d + 1, re] = nr + x_ref[s:s + 1, re]
        out_ref[d:d + 1, im] = ni + x_ref[s:s + 1, im]


def _ssm_fwd(proj, lam_r, lam_i, pw_r, pw_i, w_in, w_out, d_skip):
    t = proj.shape[0]
    seg_len = t // N_SEG
    nch = t // SCAN_ROWS
    steps = SCAN_ROWS // N_SEG
    sp = _ssm_specs(t)

    def body(u_ref, lr_ref, li_ref, pr_ref, pi_ref, wi_ref, wo_ref, d_ref, y_ref, hc_ref, bu_s, car_s, seg_s, ub_s,
             y0_s, y1_s):
        perm = _chunk_perm()
        car_s[...] = jnp.zeros_like(car_s)

        def chunk1(j, c):
            rows = pl.ds(pl.multiple_of(j * SCAN_ROWS, SCAN_ROWS), SCAN_ROWS)
            ub_s[rows, :] = _interleave(perm, _gather_chunk(u_ref, j, seg_len).astype(BF16))
            bu_s[...] = _dot_nt(ub_s[rows, :], wi_ref[...])
            _scan_fwd(bu_s, None, lr_ref, li_ref, car_s, car_s, steps)
            return c

        lax.fori_loop(0, nch, chunk1, 0)
        _segment_states(car_s, pr_ref, pi_ref, seg_s, reverse=False)
        car_s[...] = seg_s[...]

        def chunk2(j, c):
            rows = pl.ds(pl.multiple_of(j * SCAN_ROWS, SCAN_ROWS), SCAN_ROWS)
            bu_s[...] = _dot_nt(ub_s[rows, :], wi_ref[...])
            hc_ref[j] = car_s[...]
            _scan_fwd(bu_s, bu_s, lr_ref, li_ref, car_s, car_s, steps)
            yv = _dot_nt(bu_s[...].astype(BF16), wo_ref[...])
            y0_s[...] = yv[:, 0:128]
            y1_s[...] = yv[:, 128:256]
            for s in range(N_SEG):
                nat = _chunk_rows(j, seg_len, s)
                sub = pl.ds(s, SEG_ROWS, stride=N_SEG)
                y_ref[nat, :] = jnp.concatenate([y0_s[sub, :], y1_s[sub, :]], axis=1) + d_ref[...] * u_ref[nat, :]
            return c

        lax.fori_loop(0, nch, chunk2, 0)

    u_cols = 2560 // SSM_CH
    return pl.pallas_call(
        body, name="ssm_fwd", grid=(SSM_GB,),
        in_specs=[pl.BlockSpec((t, SSM_CH), lambda g: (0, u_cols + g)), sp["lam"], sp["lam"], sp["lam"], sp["lam"],
                  sp["w_in"], sp["w_out"], sp["vec"]],
        out_specs=[sp["rows"], pl.BlockSpec((None, nch, N_SEG, 2 * SSM_ST), lambda g: (g, 0, 0, 0))],
        out_shape=[jax.ShapeDtypeStruct((t, SSM_W), F32), jax.ShapeDtypeStruct((SSM_GB, nch, N_SEG, 2 * SSM_ST), F32)],
        scratch_shapes=[pltpu.VMEM((SCAN_ROWS, 2 * SSM_ST), F32), pltpu.VMEM((N_SEG, 2 * SSM_ST), F32),
                        pltpu.VMEM((N_SEG, 2 * SSM_ST), F32), pltpu.VMEM((t, SSM_CH), BF16),
                        pltpu.VMEM((SCAN_ROWS, 128), F32), pltpu.VMEM((SCAN_ROWS, 128), F32)],
        compiler_params=_params(("parallel",)),
    )(proj, lam_r, lam_i, pw_r, pw_i, w_in, w_out, d_skip)


def _group_blocks(full):
    row_g = lax.broadcasted_iota(jnp.int32, (SSM_CH, SSM_ST), 0) // SSM_H
    col_g = lax.broadcasted_iota(jnp.int32, (SSM_CH, SSM_ST), 1) // SSM_P
    fold = (lax.broadcasted_iota(jnp.int32, (SSM_ST, SSM_P), 0) % SSM_P
            == lax.broadcasted_iota(jnp.int32, (SSM_ST, SSM_P), 1)).astype(F32)
    parts = [jnp.dot(jnp.where(row_g == col_g, full[:, k * SSM_ST:(k + 1) * SSM_ST], 0.0), fold,
                     precision=lax.Precision.HIGHEST, preferred_element_type=F32) for k in range(2)]
    return jnp.concatenate(parts, axis=1)


def _ssm_bwd(proj, dy, hc, lam_r, lam_i, pw_r, pw_i, w_in, w_out, d_skip):
    t = proj.shape[0]
    seg_len = t // N_SEG
    nch = t // SCAN_ROWS
    steps = SCAN_ROWS // N_SEG
    sp = _ssm_specs(t)

    def body(u_ref, dy_ref, hc_ref, lr_ref, li_ref, pr_ref, pi_ref, wi_ref, wo_ref, d_ref,
             du_ref, gbi_ref, gbo_ref, glam_ref, gd_ref, bu_s, h_s, e_s, car_s, seg_s, acc_s, gwi_ref, gwo_ref,
             dyb_s):
        perm = _chunk_perm()

        def chunk_rows(j):
            return pl.ds(pl.multiple_of(j * SCAN_ROWS, SCAN_ROWS), SCAN_ROWS)

        def interleaved(ref, j):
            return _interleave(perm, _gather_chunk(ref, j, seg_len).astype(BF16))

        def load_e(j):
            dyb_s[chunk_rows(j), :] = interleaved(dy_ref, j)
            e_s[...] = _dot(dyb_s[chunk_rows(j), :], wo_ref[...])

        def scan_rev(j, accumulate):
            for k in range(SSM_ST // SCAN_LW):
                re = pl.ds(k * SCAN_LW, SCAN_LW)
                im = pl.ds(SSM_ST + k * SCAN_LW, SCAN_LW)
                lr, li = lr_ref[:, re], li_ref[:, re]

                def step(ii, carry, re=re, im=im, lr=lr, li=li):
                    i = steps - 1 - ii
                    rows = pl.ds(pl.multiple_of(i * 8, 8), 8)
                    if accumulate:
                        gr, gi, ar, ai = carry
                    else:
                        gr, gi = carry
                    nr = lr * gr + li * gi + e_s[rows, re]
                    ni = lr * gi - li * gr + e_s[rows, im]
                    if not accumulate:
                        return nr, ni
                    e_s[rows, re] = nr
                    e_s[rows, im] = ni
                    pr_, pi_ = h_s[rows, re], h_s[rows, im]
                    return nr, ni, ar + nr * pr_ + ni * pi_, ai + ni * pr_ - nr * pi_

                init = (car_s[:, re], car_s[:, im])
                if accumulate:
                    init = init + (acc_s[:, re], acc_s[:, im])
                out = lax.fori_loop(0, steps, step, init, unroll=4)
                car_s[:, re] = out[0]
                car_s[:, im] = out[1]
                if accumulate:
                    acc_s[:, re] = out[2]
                    acc_s[:, im] = out[3]

        car_s[...] = jnp.zeros_like(car_s)

        def pass1(jj, c):
            load_e(nch - 1 - jj)
            scan_rev(nch - 1 - jj, False)
            return c

        lax.fori_loop(0, nch, pass1, 0)
        _segment_states(car_s, pr_ref, pi_ref, seg_s, reverse=True)
        car_s[...] = seg_s[...]
        acc_s[...] = jnp.zeros_like(acc_s)
        gwi_ref[...] = jnp.zeros_like(gwi_ref)
        gwo_ref[...] = jnp.zeros_like(gwo_ref)
        gd_ref[...] = jnp.zeros_like(gd_ref)

        def pass2(jj, c):
            j = nch - 1 - jj
            u_b, dy_b = interleaved(u_ref, j), dyb_s[chunk_rows(j), :]
            bu_s[...] = _dot_nt(u_b, wi_ref[...])
            h_s[0:N_SEG, :] = hc_ref[j]
            seg_s[...] = hc_ref[j]
            _scan_fwd(bu_s, h_s.at[pl.ds(N_SEG, SCAN_ROWS), :], lr_ref, li_ref, seg_s, seg_s, steps)
            e_s[...] = _dot(dy_b, wo_ref[...])
            scan_rev(j, True)
            g_b = e_s[...].astype(BF16)
            du_b = (_dot(g_b, wi_ref[...]) + d_ref[...] * dy_b.astype(F32)).astype(BF16)
            _scatter_chunk(du_ref, j, seg_len, _dot_tn(perm, du_b).astype(du_ref.dtype))
            gwi_ref[...] += _dot_tn(u_b, g_b)
            gwo_ref[...] += _dot_tn(dy_b, h_s[pl.ds(N_SEG, SCAN_ROWS), :].astype(BF16))
            gd_ref[...] += jnp.sum(_gather_chunk(dy_ref, j, seg_len) * _gather_chunk(u_ref, j, seg_len), axis=0,
                                   keepdims=True)
            return c

        lax.fori_loop(0, nch, pass2, 0)
        glam_ref[...] = jnp.sum(acc_s[...], axis=0, keepdims=True)
        gbi_ref[...] = _group_blocks(gwi_ref[...])
        gbo_ref[...] = _group_blocks(gwo_ref[...])

    mat = pl.BlockSpec((None, SSM_CH, 2 * SSM_P), lambda g: (g, 0, 0))
    u_cols = 2560 // SSM_CH
    return pl.pallas_call(
        body, name="ssm_bwd", grid=(SSM_GB,),
        in_specs=[pl.BlockSpec((t, SSM_CH), lambda g: (0, u_cols + g)), sp["rows"],
                  pl.BlockSpec((None, nch, N_SEG, 2 * SSM_ST), lambda g: (g, 0, 0, 0)),
                  sp["lam"], sp["lam"], sp["lam"], sp["lam"], sp["w_in"], sp["w_out"], sp["vec"]],
        out_specs=[sp["rows"], mat, mat, pl.BlockSpec((None, 1, 2 * SSM_ST), lambda g: (g, 0, 0)), sp["vec"]],
        out_shape=[jax.ShapeDtypeStruct((t, SSM_W), BF16), jax.ShapeDtypeStruct((SSM_GB, SSM_CH, 2 * SSM_P), F32),
                   jax.ShapeDtypeStruct((SSM_GB, SSM_CH, 2 * SSM_P), F32),
                   jax.ShapeDtypeStruct((SSM_GB, 1, 2 * SSM_ST), F32), jax.ShapeDtypeStruct((1, SSM_W), F32)],
        scratch_shapes=[pltpu.VMEM((SCAN_ROWS, 2 * SSM_ST), F32), pltpu.VMEM((SCAN_ROWS + N_SEG, 2 * SSM_ST), F32),
                        pltpu.VMEM((SCAN_ROWS, 2 * SSM_ST), F32), pltpu.VMEM((N_SEG, 2 * SSM_ST), F32),
                        pltpu.VMEM((N_SEG, 2 * SSM_ST), F32), pltpu.VMEM((N_SEG, 2 * SSM_ST), F32),
                        pltpu.VMEM((SSM_CH, 2 * SSM_ST), F32), pltpu.VMEM((SSM_CH, 2 * SSM_ST), F32),
                        pltpu.VMEM((t, SSM_CH), BF16)],
        compiler_params=_params(("parallel",)),
    )(proj, dy, hc, lam_r, lam_i, pw_r, pw_i, w_in, w_out, d_skip)


def _z_ssm_specs(tm):
    return [pl.BlockSpec((tm, 512), lambda i: (i, 7)), pl.BlockSpec((tm, 512), lambda i: (i, 8))]


def _glu_fwd(y, proj, w_glu, b_glu):
    t = y.shape[0]
    tm = 512

    def body(y_ref, z0_ref, z1_ref, w_ref, b_ref, o_ref, yg_ref):
        yg = _gelu(y_ref[...])
        yg_b = yg.astype(BF16)
        a = _dot(yg_b, w_ref[...]) + b_ref[...]
        z = jnp.concatenate([z0_ref[...], z1_ref[...]], axis=1)
        o_ref[...] = yg * _sigmoid(a) * _silu(z)
        yg_ref[...] = yg_b

    row = pl.BlockSpec((tm, SSM_W), lambda i: (i, 0))
    return pl.pallas_call(
        body, name="glu_fwd", grid=(t // tm,),
        in_specs=[row] + _z_ssm_specs(tm) + [pl.BlockSpec((SSM_W, SSM_W), lambda i: (0, 0)),
                                            pl.BlockSpec((1, SSM_W), lambda i: (0, 0))],
        out_specs=[row, row],
        out_shape=[jax.ShapeDtypeStruct((t, SSM_W), F32), jax.ShapeDtypeStruct((t, SSM_W), BF16)],
        compiler_params=_params(("parallel",)),
    )(y, proj, proj, w_glu, b_glu)


def _glu_bwd(y, proj, dos, w_glu, b_glu):
    t = y.shape[0]
    tm = 512

    def body(y_ref, z0_ref, z1_ref, do_ref, w_ref, b_ref, dy_ref, dz_ref, da_ref, gb_ref):
        @pl.when(pl.program_id(0) == 0)
        def _():
            gb_ref[...] = jnp.zeros_like(gb_ref)

        z = jnp.concatenate([z0_ref[...], z1_ref[...]], axis=1)
        yv, do = y_ref[...], do_ref[...]
        yg = _gelu(yv)
        sg = _sigmoid(_dot(yg.astype(BF16), w_ref[...]) + b_ref[...])
        dy2 = do * _silu(z)
        dz_ref[...] = (do * yg * sg * _dsilu(z)).astype(BF16)
        da = dy2 * yg * sg * (1.0 - sg)
        da_b = da.astype(BF16)
        da_ref[...] = da_b
        gb_ref[...] += jnp.sum(da, axis=0, keepdims=True)
        dyg = dy2 * sg + _dot_nt(da_b, w_ref[...])
        dy_ref[...] = dyg * _dgelu(yv)

    row = pl.BlockSpec((tm, SSM_W), lambda i: (i, 0))
    vec = pl.BlockSpec((1, SSM_W), lambda i: (0, 0))
    return pl.pallas_call(
        body, name="glu_bwd", grid=(t // tm,),
        in_specs=[row] + _z_ssm_specs(tm) + [row, pl.BlockSpec((SSM_W, SSM_W), lambda i: (0, 0)), vec],
        out_specs=[row, row, row, vec],
        out_shape=[jax.ShapeDtypeStruct((t, SSM_W), F32), jax.ShapeDtypeStruct((t, SSM_W), BF16),
                   jax.ShapeDtypeStruct((t, SSM_W), BF16), jax.ShapeDtypeStruct((1, SSM_W), F32)],
        compiler_params=_params(("arbitrary",)),
    )(y, proj, proj, dos, w_glu, b_glu)


def _rms(o):
    return lax.rsqrt(jnp.mean(o * o, axis=1, keepdims=True) + NORM_EPS)


def _outproj(oa, os_, aw, sw, w_out, x, target):
    t = x.shape[0]
    tm = 256

    def body(oa_ref, os_ref, aw_ref, sw_ref, w_ref, x_ref, t_ref, mg_ref, do_ref, ls_ref):
        @pl.when(pl.program_id(0) == 0)
        def _():
            ls_ref[...] = jnp.zeros_like(ls_ref)

        a, s = oa_ref[...], os_ref[...]
        merged = jnp.concatenate([a * _rms(a) * aw_ref[...], s * _rms(s) * sw_ref[...]], axis=1).astype(BF16)
        mg_ref[...] = merged
        err = x_ref[...] + _dot(merged, w_ref[...]) - t_ref[...]
        do_ref[...] = err * (1.0 / D_MODEL)
        ls_ref[...] += jnp.sum(err * err)

    half = pl.BlockSpec((tm, ATTN_W), lambda i: (i, 0))
    full = pl.BlockSpec((tm, D_MODEL), lambda i: (i, 0))
    vec = pl.BlockSpec((1, ATTN_W), lambda i: (0, 0))
    return pl.pallas_call(
        body, name="outproj", grid=(t // tm,),
        in_specs=[half, half, vec, vec, pl.BlockSpec((D_MODEL, D_MODEL), lambda i: (0, 0)), full, full],
        out_specs=[full, full, pl.BlockSpec((8, 128), lambda i: (0, 0))],
        out_shape=[jax.ShapeDtypeStruct((t, D_MODEL), BF16), jax.ShapeDtypeStruct((t, D_MODEL), F32),
                   jax.ShapeDtypeStruct((8, 128), F32)],
        compiler_params=_params(("arbitrary",)),
    )(oa, os_, aw, sw, w_out, x, target)


def _outproj_bwd(dout, oa, os_, aw, sw, w_out):
    t = dout.shape[0]
    tm = 256

    def norm_bwd(o, w, dm):
        r = _rms(o)
        yh = o * r
        gh = dm * w
        return r * (gh - yh * jnp.mean(gh * yh, axis=1, keepdims=True)), jnp.sum(dm * yh, axis=0, keepdims=True)

    def body(do_ref, oa_ref, os_ref, aw_ref, sw_ref, w_ref, da_ref, ds_ref, ga_ref, gs_ref):
        @pl.when(pl.program_id(0) == 0)
        def _():
            ga_ref[...] = jnp.zeros_like(ga_ref)
            gs_ref[...] = jnp.zeros_like(gs_ref)

        dm = _dot_nt(do_ref[...].astype(BF16), w_ref[...])
        da, ga = norm_bwd(oa_ref[...], aw_ref[...], dm[:, :ATTN_W])
        ds, gs = norm_bwd(os_ref[...], sw_ref[...], dm[:, ATTN_W:])
        da_ref[...] = da
        ds_ref[...] = ds
        ga_ref[...] += ga
        gs_ref[...] += gs

    half = pl.BlockSpec((tm, ATTN_W), lambda i: (i, 0))
    full = pl.BlockSpec((tm, D_MODEL), lambda i: (i, 0))
    vec = pl.BlockSpec((1, ATTN_W), lambda i: (0, 0))
    return pl.pallas_call(
        body, name="outproj_bwd", grid=(t // tm,),
        in_specs=[full, half, half, vec, vec, pl.BlockSpec((D_MODEL, D_MODEL), lambda i: (0, 0))],
        out_specs=[half, half, vec, vec],
        out_shape=[jax.ShapeDtypeStruct((t, ATTN_W), F32), jax.ShapeDtypeStruct((t, ATTN_W), F32),
                   jax.ShapeDtypeStruct((1, ATTN_W), F32), jax.ShapeDtypeStruct((1, ATTN_W), F32)],
        compiler_params=_params(("arbitrary",)),
    )(dout, oa, os_, aw, sw, w_out)


def _inproj_bwd(dproj, w_slabs, x, norm_w, dout, outgoing):
    t = x.shape[0]
    tm = 512
    nc = 4
    nt = len(outgoing)
    ni = t // tm

    def body(dp_ref, w_ref, x_ref, nw_ref, do_ref, *rest):
        src, (gx_ref, gw_ref), dst = rest[:nt], rest[nt:nt + 2], rest[nt + 2:2 * nt + 2]
        acc_ref, ssem, rsem = rest[2 * nt + 2:]
        i, j = pl.program_id(0), pl.program_id(1)
        start, wait = _bg_scatter_chips(src, dst, ssem, rsem)

        @pl.when((i == 0) & (j == 0))
        def _():
            start()
            gw_ref[...] = jnp.zeros_like(gw_ref)

        @pl.when(j == 0)
        def _():
            acc_ref[...] = jnp.zeros_like(acc_ref)

        acc_ref[...] += _dot_nt(dp_ref[...], w_ref[...])

        @pl.when(j == nc - 1)
        def _():
            xv = x_ref[...]
            r = lax.rsqrt(jnp.mean(xv * xv, axis=1, keepdims=True) + NORM_EPS)
            yh = xv * r
            dh = acc_ref[...]
            gh = dh * nw_ref[...]
            gx_ref[...] = do_ref[...] + r * (gh - yh * jnp.mean(gh * yh, axis=1, keepdims=True))
            gw_ref[...] += jnp.sum(dh * yh, axis=0, keepdims=True)

        @pl.when((i == ni - 1) & (j == nc - 1))
        def _():
            wait()

    full = pl.BlockSpec((tm, D_MODEL), lambda i, j: (i, 0))
    vec = pl.BlockSpec((1, D_MODEL), lambda i, j: (0, 0))
    return pl.pallas_call(
        body, name="inproj_bwd", grid=(ni, nc),
        in_specs=[pl.BlockSpec((tm, SHARD_W), lambda i, j: (i, j)),
                  pl.BlockSpec((None, D_MODEL, SHARD_W), lambda i, j: (j, 0, 0)), full, vec, full] + [ANY] * nt,
        out_specs=[full, vec] + [ANY] * nt,
        out_shape=[jax.ShapeDtypeStruct((t, D_MODEL), F32), jax.ShapeDtypeStruct((1, D_MODEL), F32)]
        + [jax.ShapeDtypeStruct(a.shape, a.dtype) for a in outgoing],
        scratch_shapes=[pltpu.VMEM((tm, D_MODEL), F32), pltpu.SemaphoreType.DMA((3 * nt,)),
                        pltpu.SemaphoreType.DMA((3 * nt,))],
        compiler_params=_params(("arbitrary", "arbitrary")),
    )(dproj, w_slabs, x, norm_w.reshape(1, D_MODEL), dout, *outgoing)


def _adamw_math(w_ref, g_ref, m_ref, v_ref, d_ref, nm_ref, nv_ref):
    gv = g_ref[...]
    nm = ADAM_B1 * m_ref[...] + (1.0 - ADAM_B1) * gv
    nv = ADAM_B2 * v_ref[...] + (1.0 - ADAM_B2) * (gv * gv)
    m_hat = nm / (1.0 - ADAM_B1 ** ADAM_STEP)
    v_hat = nv / (1.0 - ADAM_B2 ** ADAM_STEP)
    d_ref[...] = -ADAM_LR * (m_hat / (jnp.sqrt(v_hat) + ADAM_EPS) + ADAM_WD * w_ref[...])
    nm_ref[...] = nm
    nv_ref[...] = nv


def _adamw_halves(w, mine, theirs, m, v, c_idx, *, rows, name):
    hr, cols = mine.shape
    nblk = hr // rows

    def body(c_ref, w_ref, a_ref, b_ref, m_ref, v_ref, g_ref, d_ref, nm_ref, nv_ref):
        g_ref[...] = jnp.where(pl.program_id(0) == c_ref[0], a_ref[...], b_ref[...])
        _adamw_math(w_ref, g_ref, m_ref, v_ref, d_ref, nm_ref, nv_ref)

    full = pl.BlockSpec((rows, cols), lambda h, i, c: (h * nblk + i, 0))
    part = pl.BlockSpec((rows, cols), lambda h, i, c: (i, 0))
    shp = jax.ShapeDtypeStruct((2 * hr, cols), F32)
    return pl.pallas_call(
        body, name=name,
        grid_spec=pltpu.PrefetchScalarGridSpec(num_scalar_prefetch=1, grid=(2, nblk),
                                               in_specs=[full, part, part, full, full], out_specs=[full] * 4),
        out_shape=[shp] * 4, compiler_params=_params(("parallel", "parallel")),
    )(c_idx, w, mine, theirs, m, v)


def _adamw(w, g, m, v, *, rows, name):
    r, c = w.shape

    def body(w_ref, g_ref, m_ref, v_ref, d_ref, nm_ref, nv_ref):
        _adamw_math(w_ref, g_ref, m_ref, v_ref, d_ref, nm_ref, nv_ref)

    blk = pl.BlockSpec((rows, c), lambda i: (i, 0))
    shp = jax.ShapeDtypeStruct((r, c), F32)
    return pl.pallas_call(body, name=name, grid=(r // rows,), in_specs=[blk] * 4, out_specs=[blk] * 3,
                          out_shape=[shp] * 3, compiler_params=_params(("parallel",)))(w, g, m, v)


def _remote(src, dst, ssem, rsem, dev):
    return pltpu.make_async_remote_copy(src_ref=src, dst_ref=dst, send_sem=ssem, recv_sem=rsem, device_id=dev,
                                        device_id_type=pl.DeviceIdType.MESH)


def _mesh_pos():
    return lax.axis_index("x"), lax.axis_index("y"), lax.axis_index("c")


def _other_chips(x, y):
    return [(1 - x, y), (x, 1 - y), (1 - x, 1 - y)]


def _flips():
    return [(dx, dy, dc) for dx in (0, 1) for dy in (0, 1) for dc in (0, 1) if (dx, dy, dc) != (0, 0, 0)]


def _background(sends, arrivals):
    def start():
        for cp in sends():
            cp.start()

    def wait():
        for cp in arrivals():
            cp.wait_recv()
        for cp in sends():
            cp.wait_send()

    return start, wait


def _bg_gather(sh, full, ssem, rsem):
    x, y, c = _mesh_pos()
    me = 2 * x + y
    peers = [(px, py, c) for px, py in _other_chips(x, y)] + [(x, y, 1 - c)]
    slots = [2 * px + py for px, py in _other_chips(x, y)] + [me]
    pairs = [(i, k) for i in range(len(sh)) for k in range(4)]
    return _background(
        lambda: [_remote(sh[i], full[i].at[me], ssem.at[4 * i + k], rsem.at[4 * i + k], peers[k]) for i, k in pairs],
        lambda: [_remote(full[i].at[slots[k]], full[i].at[slots[k]], ssem.at[4 * i + k], rsem.at[4 * i + k], peers[k])
                 for i, k in pairs])


def _bg_scatter_devices(src, dst, ssem, rsem):
    x, y, c = _mesh_pos()
    me = 4 * x + 2 * y + c
    peers = []
    for dx, dy, dc in _flips():
        px, py, pc = jnp.bitwise_xor(x, dx), jnp.bitwise_xor(y, dy), jnp.bitwise_xor(c, dc)
        peers.append(((px, py, pc), 4 * px + 2 * py + pc))
    pairs = [(i, k) for i in range(len(src)) for k in range(7)]
    return _background(
        lambda: [_remote(src[i].at[peers[k][1]], dst[i].at[me], ssem.at[7 * i + k], rsem.at[7 * i + k], peers[k][0])
                 for i, k in pairs],
        lambda: [_remote(dst[i].at[peers[k][1]], dst[i].at[peers[k][1]], ssem.at[7 * i + k], rsem.at[7 * i + k],
                         peers[k][0]) for i, k in pairs])


def _bg_scatter_chips(src, dst, ssem, rsem):
    x, y, c = _mesh_pos()
    me = 2 * x + y
    chips = _other_chips(x, y)
    pairs = [(i, k) for i in range(len(src)) for k in range(3)]
    slot = lambda k: 2 * chips[k][0] + chips[k][1]
    return _background(
        lambda: [_remote(src[i].at[slot(k)], dst[i].at[me], ssem.at[3 * i + k], rsem.at[3 * i + k], (*chips[k], c))
                 for i, k in pairs],
        lambda: [_remote(dst[i].at[slot(k)], dst[i].at[slot(k)], ssem.at[3 * i + k], rsem.at[3 * i + k], (*chips[k], c))
                 for i, k in pairs])


def _pair_swap(arrays):
    nt = len(arrays)

    def body(*refs):
        src, dst = refs[:nt], refs[nt:2 * nt]
        ssem, rsem = refs[2 * nt:]
        x, y, c = _mesh_pos()
        cps = [_remote(src[i].at[:, 1 - c], dst[i], ssem.at[i], rsem.at[i], (x, y, 1 - c)) for i in range(nt)]
        for cp in cps:
            cp.start()
        for cp in cps:
            cp.wait_recv()
        for cp in cps:
            cp.wait_send()

    return pl.pallas_call(
        body, name="pair_swap", in_specs=[ANY] * nt, out_specs=[ANY] * nt,
        out_shape=[jax.ShapeDtypeStruct((4,) + a.shape[2:], a.dtype) for a in arrays],
        scratch_shapes=[pltpu.SemaphoreType.DMA((nt,)), pltpu.SemaphoreType.DMA((nt,))],
    )(*arrays)


def _half_swap(arrays):
    nt = len(arrays)

    def body(*refs):
        src, dst = refs[:nt], refs[nt:2 * nt]
        ssem, rsem = refs[2 * nt:]
        x, y, c = _mesh_pos()
        cps = [_remote(src[i], dst[i], ssem.at[i], rsem.at[i], (x, y, 1 - c)) for i in range(nt)]
        for cp in cps:
            cp.start()
        for cp in cps:
            cp.wait_recv()
        for cp in cps:
            cp.wait_send()

    return pl.pallas_call(
        body, name="half_swap", in_specs=[ANY] * nt, out_specs=[ANY] * nt,
        out_shape=[jax.ShapeDtypeStruct(a.shape, a.dtype) for a in arrays],
        scratch_shapes=[pltpu.SemaphoreType.DMA((nt,)), pltpu.SemaphoreType.DMA((nt,))],
    )(*arrays)


def _exchange_slices(src, scatter, name):
    def body(src_ref, dst_ref, ssem, rsem, lsem):
        x, y, c = _mesh_pos()
        me = 4 * x + 2 * y + c
        local = pltpu.make_async_copy(src_ref.at[me] if scatter else src_ref, dst_ref.at[me], lsem)
        local.start()
        cps = []
        for k, (dx, dy, dc) in enumerate(_flips()):
            px, py, pc = jnp.bitwise_xor(x, dx), jnp.bitwise_xor(y, dy), jnp.bitwise_xor(c, dc)
            peer = 4 * px + 2 * py + pc
            cp = _remote(src_ref.at[peer] if scatter else src_ref, dst_ref.at[me], ssem.at[k], rsem.at[k],
                         (px, py, pc))
            cp.start()
            cps.append((cp, peer))
        for k, (cp, peer) in enumerate(cps):
            slot = dst_ref.at[peer]
            _remote(slot, slot, ssem.at[k], rsem.at[k], (x, y, c)).wait_recv()
        for cp, _ in cps:
            cp.wait_send()
        local.wait()

    return pl.pallas_call(
        body, name=name, in_specs=[ANY], out_specs=ANY,
        out_shape=jax.ShapeDtypeStruct((8,) + src.shape[-2:], src.dtype),
        scratch_shapes=[pltpu.SemaphoreType.DMA((7,)), pltpu.SemaphoreType.DMA((7,)), pltpu.SemaphoreType.DMA],
    )(src)


def _add_halves(g, recv, c_idx, *, rows, name):
    _, _, hr, cols = g.shape

    def body(c_ref, g_ref, r_ref, o_ref):
        o_ref[...] = (g_ref[...] + r_ref[...].astype(F32)).astype(BF16)

    return pl.pallas_call(
        body, name=name,
        grid_spec=pltpu.PrefetchScalarGridSpec(
            num_scalar_prefetch=1, grid=(4, hr // rows),
            in_specs=[pl.BlockSpec((None, None, rows, cols), lambda j, i, c: (j, c[0], i, 0)),
                      pl.BlockSpec((None, rows, cols), lambda j, i, c: (j, i, 0))],
            out_specs=pl.BlockSpec((None, rows, cols), lambda j, i, c: (j, i, 0))),
        out_shape=jax.ShapeDtypeStruct((4, hr, cols), BF16),
        compiler_params=_params(("parallel", "parallel")),
    )(c_idx, g, recv)


def _sum_peers(slots, own, idx, *, rows, name):
    n, r, cols = slots.shape

    def body(me_ref, *refs):
        me = me_ref[0]
        mine = refs[n][...].astype(F32)
        acc = None
        for k in range(n):
            term = jnp.where(me == k, mine, refs[k][...].astype(F32))
            acc = term if acc is None else acc + term
        refs[n + 1][...] = acc

    def slot_spec(k):
        return pl.BlockSpec((None, rows, cols), lambda i, me: (jnp.where(me[0] == k, (k + 1) % n, k), i, 0))

    return pl.pallas_call(
        body, name=name,
        grid_spec=pltpu.PrefetchScalarGridSpec(
            num_scalar_prefetch=1, grid=(r // rows,),
            in_specs=[slot_spec(k) for k in range(n)] + [pl.BlockSpec((None, rows, cols), lambda i, me: (me[0], i, 0))],
            out_specs=pl.BlockSpec((rows, cols), lambda i, me: (i, 0))),
        out_shape=jax.ShapeDtypeStruct((r, cols), F32),
        compiler_params=_params(("parallel",)),
    )(idx, *([slots] * n), own)


def _sum_slots(slots, *, rows, name):
    n, r, cols = slots.shape

    def body(s_ref, o_ref):
        acc = s_ref[0].astype(F32)
        for k in range(1, n):
            acc = acc + s_ref[k].astype(F32)
        o_ref[...] = acc

    return pl.pallas_call(
        body, name=name, grid=(r // rows,),
        in_specs=[pl.BlockSpec((n, rows, cols), lambda i: (0, i, 0))],
        out_specs=pl.BlockSpec((rows, cols), lambda i: (i, 0)),
        out_shape=jax.ShapeDtypeStruct((r, cols), F32),
        compiler_params=_params(("parallel",)),
    )(slots)


def _pack_small(d, names, rows):
    flat = jnp.concatenate([d[n].astype(F32).reshape(-1) for n in names])
    return jnp.pad(flat, (0, rows * 128 - flat.shape[0])).reshape(rows, 128)


def _unpack_small(p, names):
    flat = p.reshape(-1)
    out, off = {}, 0
    for n in names:
        size = math.prod(SMALL_SHAPES[n])
        out[n] = flat[off:off + size].reshape(SMALL_SHAPES[n])
        off += size
    return out


def _adamw_3d(w, g, m, v, *, name):
    def body(w_ref, g_ref, m_ref, v_ref, d_ref, nm_ref, nv_ref):
        _adamw_math(w_ref, g_ref, m_ref, v_ref, d_ref, nm_ref, nv_ref)

    blk = pl.BlockSpec((8,) + w.shape[1:], lambda i: (i, 0, 0))
    shp = jax.ShapeDtypeStruct(w.shape, F32)
    return pl.pallas_call(body, name=name, grid=(w.shape[0] // 8,), in_specs=[blk] * 4, out_specs=[blk] * 3,
                          out_shape=[shp] * 3, compiler_params=_params(("parallel",)))(w, g, m, v)


def kernel(x, positions, norm_w, w_in, q_norm_w, k_norm_w, sinks, a_re, a_im, log_step, b_re, b_im, c_re, c_im, d_skip, w_glu, b_glu, attn_out_norm_w, ssm_out_norm_w, w_out, loss_target, m_norm_w, m_w_in, m_q_norm_w, m_k_norm_w, m_sinks, m_a_re, m_a_im, m_log_step, m_b_re, m_b_im, m_c_re, m_c_im, m_d_skip, m_w_glu, m_b_glu, m_attn_out_norm_w, m_ssm_out_norm_w, m_w_out, v_norm_w, v_w_in, v_q_norm_w, v_k_norm_w, v_sinks, v_a_re, v_a_im, v_log_step, v_b_re, v_b_im, v_c_re, v_c_im, v_d_skip, v_w_glu, v_b_glu, v_attn_out_norm_w, v_ssm_out_norm_w, v_w_out):
    small_w = dict(norm_w=norm_w, q_norm_w=q_norm_w, k_norm_w=k_norm_w, sinks=sinks, a_re=a_re, a_im=a_im,
                   log_step=log_step, b_re=b_re, b_im=b_im, c_re=c_re, c_im=c_im, d_skip=d_skip, b_glu=b_glu,
                   attn_out_norm_w=attn_out_norm_w, ssm_out_norm_w=ssm_out_norm_w)
    small_m = dict(norm_w=m_norm_w, q_norm_w=m_q_norm_w, k_norm_w=m_k_norm_w, sinks=m_sinks, a_re=m_a_re, a_im=m_a_im,
                   log_step=m_log_step, b_re=m_b_re, b_im=m_b_im, c_re=m_c_re, c_im=m_c_im, d_skip=m_d_skip,
                   b_glu=m_b_glu, attn_out_norm_w=m_attn_out_norm_w, ssm_out_norm_w=m_ssm_out_norm_w)
    small_v = dict(norm_w=v_norm_w, q_norm_w=v_q_norm_w, k_norm_w=v_k_norm_w, sinks=v_sinks, a_re=v_a_re, a_im=v_a_im,
                   log_step=v_log_step, b_re=v_b_re, b_im=v_b_im, c_re=v_c_re, c_im=v_c_im, d_skip=v_d_skip,
                   b_glu=v_b_glu, attn_out_norm_w=v_attn_out_norm_w, ssm_out_norm_w=v_ssm_out_norm_w)
    c_idx = lax.axis_index("c").astype(jnp.int32).reshape(1)
    chip_idx = (2 * lax.axis_index("x") + lax.axis_index("y")).astype(jnp.int32).reshape(1)
    dev_idx = 2 * chip_idx + c_idx

    xs = x[0]
    tgt = loss_target[0]
    t = xs.shape[0]
    posf = positions[0].astype(F32).reshape(t, 1)

    mx, my = lax.axis_index("x"), lax.axis_index("y")
    slab_order = jnp.stack([2 * mx + my, 2 * (1 - mx) + my, 2 * mx + (1 - my), 2 * (1 - mx) + (1 - my)]).astype(jnp.int32)
    proj, hn, w_in_all = _inproj(xs, norm_w, w_in.astype(BF16), slab_order)
    inv_freq = ROPE_THETA ** (-jnp.arange(0, HEAD_DIM, 2, dtype=F32) / HEAD_DIM)
    rope = _rope_table(posf, jnp.tile(inv_freq, 4).reshape(1, 128))
    qw = jnp.tile(q_norm_w, 2).reshape(1, 128)
    kw = jnp.tile(k_norm_w, 2).reshape(1, 128)
    sink_row = sinks.reshape(1, N_HEADS)
    oa, w_glu_all, w_out_all = _attn_fwd(proj, rope, qw, kw, sink_row, [w_glu.astype(BF16), w_out.astype(BF16)])
    w_glu_b = w_glu_all.reshape(SSM_W, SSM_W)
    w_out_b = w_out_all.reshape(D_MODEL, D_MODEL)

    lam_r, lam_i, pw_r, pw_i, bb_r, bb_i = _ssm_prep(a_re, a_im, log_step, b_re, b_im, t // N_SEG)
    rows8 = lambda a: jnp.broadcast_to(a.reshape(SSM_GB, 1, SSM_ST), (SSM_GB, N_SEG, SSM_ST))
    lam_r8, lam_i8, pw_r8, pw_i8 = rows8(lam_r), rows8(lam_i), rows8(pw_r), rows8(pw_i)
    ssm_w_in = jnp.concatenate([_block_diag_in(bb_r), _block_diag_in(bb_i)], axis=1).astype(BF16)
    ssm_w_out = jnp.concatenate([_block_diag_out(c_re), _block_diag_out(-c_im)], axis=2).astype(BF16)
    d_row = d_skip.reshape(1, SSM_W)
    y, hc = _ssm_fwd(proj, lam_r8, lam_i8, pw_r8, pw_i8, ssm_w_in, ssm_w_out, d_row)
    b_glu_row = b_glu.reshape(1, SSM_W)
    os_, yg = _glu_fwd(y, proj, w_glu_b, b_glu_row)
    aw = attn_out_norm_w.reshape(1, ATTN_W)
    sw = ssm_out_norm_w.reshape(1, SSM_W)
    merged, dout, sq_err = _outproj(oa, os_, aw, sw, w_out_b, xs, tgt)
    loss = lax.psum(0.5 * sq_err[0, 0] / D_MODEL, MESH_AXES)

    doa, dos, g_aw, g_sw = _outproj_bwd(dout, oa, os_, aw, sw, w_out_b)
    dout_b = dout.astype(BF16)
    (g_w_out_b,) = _matmul_tn(merged, dout_b, tm=512, tn=1024, name="grad_w_out", dtypes=(BF16,))
    dy, dzs, da, g_b_glu = _glu_bwd(y, proj, dos, w_glu_b, b_glu_row)
    (g_w_glu_b,) = _matmul_tn(yg, da, tm=512, tn=1024, name="grad_w_glu", dtypes=(BF16,))
    du, g_wi, g_wo, g_lam, g_d = _ssm_bwd(proj, dy, hc, lam_r8, lam_i8, pw_r8, pw_i8, ssm_w_in, ssm_w_out, d_row)
    early = [g_w_glu_b.reshape(8, 128, SSM_W), g_w_out_b.reshape(8, 256, D_MODEL)]
    dproj, g_qw, g_kw, g_sink, *early_slots = _attn_bwd(proj, rope, qw, kw, sink_row, doa, du, dzs, early)
    g_w_in, g_w_in_b = _matmul_tn(hn, dproj, tm=512, tn=SHARD_W, name="grad_w_in", slabs=True)
    in_shape = (4, 2, D_MODEL // 2, SHARD_W)
    (from_sib,) = _pair_swap([g_w_in_b.reshape(in_shape)])
    pair_in = _add_halves(g_w_in.reshape(in_shape), from_sib, c_idx, rows=128, name="pair_sum")
    grad_x, g_nw, in_slots = _inproj_bwd(dproj, w_in_all, xs, norm_w, dout, [pair_in])

    g_wi = g_wi.reshape(SSM_G, SSM_H, 2 * SSM_P)
    g_wo = g_wo.reshape(SSM_G, SSM_H, 2 * SSM_P)
    g_bb_r = g_wi[:, :, :SSM_P].transpose(0, 2, 1).reshape(SSM_G, SSM_P * SSM_H)
    g_bb_i = g_wi[:, :, SSM_P:].transpose(0, 2, 1).reshape(SSM_G, SSM_P * SSM_H)
    g_a_re, g_a_im, g_ls, g_b_re, g_b_im = _ssm_param_grads(
        a_re, a_im, log_step, b_re, b_im, g_lam[:, 0, :SSM_ST].reshape(SSM_G, SSM_P),
        g_lam[:, 0, SSM_ST:].reshape(SSM_G, SSM_P), g_bb_r, g_bb_i)
    small_g = dict(
        norm_w=g_nw, q_norm_w=g_qw[0, :64] + g_qw[0, 64:], k_norm_w=g_kw[0, :64] + g_kw[0, 64:],
        sinks=g_sink[0, :N_HEADS], a_re=g_a_re, a_im=g_a_im, log_step=g_ls, b_re=g_b_re, b_im=g_b_im,
        c_re=g_wo[:, :, :SSM_P], c_im=-g_wo[:, :, SSM_P:], d_skip=g_d,
        b_glu=g_b_glu, attn_out_norm_w=g_aw, ssm_out_norm_w=g_sw)

    mine = [_sum_peers(in_slots, pair_in, chip_idx, rows=128, name="sum_w_in"),
            _sum_peers(early_slots[0], early[0], dev_idx, rows=128, name="sum_w_glu"),
            _sum_peers(early_slots[1], early[1], dev_idx, rows=128, name="sum_w_out")]
    theirs = _half_swap(mine)
    packed = _pack_small(small_g, SMALL, 8 * PACK_ROWS).reshape(8, PACK_ROWS, 128)
    summed = _sum_slots(_exchange_slices(packed, True, "small_scatter"), rows=PACK_ROWS, name="small_sum")
    small_red = _exchange_slices(summed, False, "small_gather").reshape(8 * PACK_ROWS, 128)

    big = [_adamw_halves(w_in, mine[0], theirs[0], m_w_in, v_w_in, c_idx, rows=256, name="adamw_w_in"),
           _adamw_halves(w_glu, mine[1], theirs[1], m_w_glu, v_w_glu, c_idx, rows=128, name="adamw_w_glu"),
           _adamw_halves(w_out, mine[2], theirs[2], m_w_out, v_w_out, c_idx, rows=256, name="adamw_w_out")]
    g_in_sh, g_glu_sh, g_out_sh = (b[0] for b in big)
    upd = [b[1:] for b in big]
    grads = _unpack_small(small_red, SMALL)
    flat_first = sum(math.prod(SMALL_SHAPES[n]) for n in SMALL_3D) // 128
    sd, sm, sv = _adamw(_pack_small(small_w, SMALL_FLAT, FLAT_ROWS), small_red[flat_first:flat_first + FLAT_ROWS],
                        _pack_small(small_m, SMALL_FLAT, FLAT_ROWS), _pack_small(small_v, SMALL_FLAT, FLAT_ROWS),
                        rows=FLAT_ROWS, name="adamw_small")
    deltas, new_m, new_v = (_unpack_small(a, SMALL_FLAT) for a in (sd, sm, sv))
    for n in SMALL_3D:
        deltas[n], new_m[n], new_v[n] = _adamw_3d(small_w[n], grads[n], small_m[n], small_v[n], name="adamw_" + n)
    grads.update(w_in=g_in_sh, w_glu=g_glu_sh, w_out=g_out_sh)
    for n, (d, m_, v_) in zip(("w_in", "w_glu", "w_out"), upd):
        deltas[n], new_m[n], new_v[n] = d, m_, v_
    order = ["norm_w", "w_in", "q_norm_w", "k_norm_w", "sinks", "a_re", "a_im", "log_step", "b_re", "b_im", "c_re",
             "c_im", "d_skip", "w_glu", "b_glu", "attn_out_norm_w", "ssm_out_norm_w", "w_out"]
    return (loss, grad_x[None], *[grads[n] for n in order], *[deltas[n] for n in order],
            *[new_m[n] for n in order], *[new_v[n] for n in order])
```

```python
import math

import jax
import jax.numpy as jnp
from jax import lax
from jax.experimental import pallas as pl
from jax.experimental.pallas import tpu as pltpu

F32 = jnp.float32
BF16 = jnp.bfloat16

D_MODEL = 2048
ATTN_W = 1024
SSM_W = 1024
HEAD_DIM = 64
N_HEADS = 16
N_KV_HEADS = 4
KV_W = 256
BLOCK = 128
IN_W = 4608
SHARD_W = IN_W // 4
ROPE_THETA = 10000.0
SSM_H = 16
SSM_G = 64
SSM_P = 64
NORM_EPS = 1e-6
ADAM_LR = 0.001
ADAM_B1 = 0.9
ADAM_B2 = 0.999
ADAM_EPS = 1e-08
ADAM_WD = 0.01
ADAM_STEP = 10

N_SEG = 8
SSM_GB = 4
SSM_CH = 256
SSM_ST = 1024
SCAN_ROWS = 256
SCAN_LW = 512
VMEM_LIMIT = 56 * 1024 * 1024
MESH_AXES = ("x", "y", "c")
ANY = pl.BlockSpec(memory_space=pl.ANY)

SMALL_3D = ("b_re", "b_im", "c_re", "c_im")
SMALL_FLAT = ("norm_w", "q_norm_w", "k_norm_w", "sinks", "a_re", "a_im", "log_step", "d_skip", "b_glu",
              "attn_out_norm_w", "ssm_out_norm_w")
SMALL = SMALL_3D + SMALL_FLAT
SMALL_SHAPES = {"norm_w": (2048,), "q_norm_w": (64,), "k_norm_w": (64,), "sinks": (16,), "a_re": (64, 64),
                "a_im": (64, 64), "log_step": (64,), "b_re": (64, 64, 16), "b_im": (64, 64, 16),
                "c_re": (64, 16, 64), "c_im": (64, 16, 64), "d_skip": (1024,), "b_glu": (1024,),
                "attn_out_norm_w": (1024,), "ssm_out_norm_w": (1024,)}
PACK_ROWS = 272
FLAT_ROWS = 120


def _params(sem=None):
    return pltpu.CompilerParams(dimension_semantics=sem, vmem_limit_bytes=VMEM_LIMIT)


def _dot(a, b):
    return jnp.dot(a, b, preferred_element_type=F32)


def _dot_nt(a, b):
    return lax.dot_general(a, b, (((1,), (1,)), ((), ())), preferred_element_type=F32)


def _dot_tn(a, b):
    return lax.dot_general(a, b, (((0,), (0,)), ((), ())), preferred_element_type=F32)


def _sigmoid(x):
    return 1.0 / (1.0 + jnp.exp(-x))


def _silu(x):
    return x * _sigmoid(x)


def _dsilu(x):
    s = _sigmoid(x)
    return s * (1.0 + x * (1.0 - s))


_GELU_C = math.sqrt(2.0 / math.pi)


def _gelu(x):
    return 0.5 * x * (1.0 + jnp.tanh(_GELU_C * (x + 0.044715 * x * x * x)))


def _dgelu(x):
    t = jnp.tanh(_GELU_C * (x + 0.044715 * x * x * x))
    return 0.5 * (1.0 + t) + 0.5 * x * (1.0 - t * t) * _GELU_C * (1.0 + 3.0 * 0.044715 * x * x)


def _matmul_tn(a, b, *, tm, tn, name, slabs=False, dtypes=(F32, BF16)):
    k, m = a.shape
    _, n = b.shape

    def body(a_ref, b_ref, *o_refs):
        acc = _dot_tn(a_ref[...], b_ref[...])
        for o_ref in o_refs:
            o_ref[...] = acc.astype(o_ref.dtype)

    if slabs:
        out_spec = pl.BlockSpec((None, tm, tn), lambda j, i: (j, i, 0))
        shape = (n // tn, m, tn)
    else:
        out_spec = pl.BlockSpec((tm, tn), lambda j, i: (i, j))
        shape = (m, n)
    return pl.pallas_call(
        body, name=name, grid=(n // tn, m // tm),
        in_specs=[pl.BlockSpec((k, tm), lambda j, i: (0, i)), pl.BlockSpec((k, tn), lambda j, i: (0, j))],
        out_specs=[out_spec] * len(dtypes),
        out_shape=[jax.ShapeDtypeStruct(shape, d) for d in dtypes],
        compiler_params=_params(("parallel", "parallel")),
    )(a, b)


def _inproj(x, norm_w, w_sh, order):
    t = x.shape[0]
    tm = 512
    ni = t // tm
    hr = D_MODEL // 2

    def body(ord_ref, x_ref, nw_ref, sh_ref, proj_ref, hn_ref, full_ref, wbuf, hn_s, ssem, rsem, lsem):
        s, i = pl.program_id(0), pl.program_id(1)
        mx, my, c = _mesh_pos()
        me = 2 * mx + my
        sib = (mx, my, 1 - c)
        chips = _other_chips(mx, my)

        def half(which):
            return pl.ds(pl.multiple_of(which * hr, 8), hr)

        def slot(k):
            return 2 * chips[k][0] + chips[k][1]

        def ici(k):
            return _remote(sh_ref.at[half(c)], full_ref.at[me, half(c)], ssem.at[k], rsem.at[k], (*chips[k], c))

        def own():
            return _remote(sh_ref, full_ref.at[me], ssem.at[6], rsem.at[6], sib)

        def landed(k, which, sem):
            ref = full_ref.at[slot(k), half(which)]
            return _remote(ref, ref, ssem.at[sem], rsem.at[sem], sib)

        def fetch(src, b):
            return pltpu.make_async_copy(src, wbuf.at[b], lsem.at[b])

        @pl.when((s == 0) & (i == 0))
        def _():
            for k in range(3):
                ici(k).start()
            own().start()
            cp = fetch(sh_ref, 0)
            cp.start()
            cp.wait()

        for k in range(3):
            @pl.when((s == k) & (i == max(ni - 2, 0)))
            def _(k=k):
                landed(k, c, k).wait_recv()
                landed(k, c, 3 + k).start()
                landed(k, 1 - c, 3 + k).wait_recv()
                fetch(full_ref.at[slot(k)], (k + 1) % 2).start()

            @pl.when((s == k + 1) & (i == 0))
            def _(k=k):
                fetch(full_ref.at[slot(k)], (k + 1) % 2).wait()

        xv = x_ref[...]
        r = lax.rsqrt(jnp.mean(xv * xv, axis=1, keepdims=True) + NORM_EPS)
        hn = (xv * r * nw_ref[...]).astype(BF16)
        proj_ref[...] = _dot(hn, wbuf[s % 2])

        def hn_out(tile):
            return pltpu.make_async_copy(hn_s, hn_ref.at[pl.ds(pl.multiple_of(tile * tm, tm), tm), :], lsem.at[2])

        @pl.when(((s == 0) & (i > 0)) | ((s == 1) & (i == 0)))
        def _():
            hn_out(jnp.where(s == 0, i - 1, ni - 1)).wait()

        @pl.when(s == 0)
        def _():
            hn_s[...] = hn
            hn_out(i).start()

        @pl.when((s == 3) & (i == ni - 1))
        def _():
            mine = full_ref.at[me]
            _remote(mine, mine, ssem.at[6], rsem.at[6], sib).wait_recv()
            for k in range(3):
                ici(k).wait_send()
                landed(k, c, 3 + k).wait_send()
            own().wait_send()

    return pl.pallas_call(
        body, name="inproj",
        grid_spec=pltpu.PrefetchScalarGridSpec(
            num_scalar_prefetch=1, grid=(4, ni),
            in_specs=[pl.BlockSpec((tm, D_MODEL), lambda s, i, o: (i, 0)),
                      pl.BlockSpec((1, D_MODEL), lambda s, i, o: (0, 0)), ANY],
            out_specs=[pl.BlockSpec((tm, SHARD_W), lambda s, i, o: (i, o[s])), ANY, ANY],
            scratch_shapes=[pltpu.VMEM((2, D_MODEL, SHARD_W), BF16), pltpu.VMEM((tm, D_MODEL), BF16),
                            pltpu.SemaphoreType.DMA((7,)), pltpu.SemaphoreType.DMA((7,)),
                            pltpu.SemaphoreType.DMA((3,))]),
        out_shape=[jax.ShapeDtypeStruct((t, IN_W), F32), jax.ShapeDtypeStruct((t, D_MODEL), BF16),
                   jax.ShapeDtypeStruct((4, D_MODEL, SHARD_W), BF16)],
        compiler_params=_params(("arbitrary", "arbitrary")),
    )(order, x, norm_w.reshape(1, D_MODEL), w_sh)


def _lane128():
    return lax.broadcasted_iota(jnp.int32, (1, 128), 1)


def _head_sums(v):
    lo = _lane128() < 64
    s_lo = jnp.sum(jnp.where(lo, v, 0.0), axis=1, keepdims=True)
    s_hi = jnp.sum(jnp.where(lo, 0.0, v), axis=1, keepdims=True)
    return jnp.where(lo, s_lo, s_hi)


def _rot_half(t):
    first = (_lane128() % 64) < 32
    return jnp.where(first, -pltpu.roll(t, 96, 1), pltpu.roll(t, 32, 1))


def _head_rstd(t):
    return lax.rsqrt(_head_sums(t * t) * (1.0 / HEAD_DIM) + NORM_EPS)


def _prep_tile(t, w, cos, sin, r=None):
    r = _head_rstd(t) if r is None else r
    tn = t * r * w
    return tn * cos + _rot_half(tn) * sin


def _prep_tile_bwd(t, w, cos, sin, g, r=None):
    r = _head_rstd(t) if r is None else r
    d_tn = g * cos - _rot_half(g * sin)
    th = t * r
    dw = jnp.sum(d_tn * th, axis=0, keepdims=True)
    gh = d_tn * w
    m = _head_sums(gh * th) * (1.0 / HEAD_DIM)
    return r * (gh - th * m), dw


def _band_mask(n):
    qi = lax.broadcasted_iota(jnp.int32, (BLOCK, 2 * BLOCK), 0) + BLOCK
    ki = lax.broadcasted_iota(jnp.int32, (BLOCK, 2 * BLOCK), 1)
    rel = qi - ki
    return (rel >= 0) & (rel < BLOCK) & ((n > 0) | (ki >= BLOCK))


def _half_select(tile, half):
    lo = _lane128() < 64
    return jnp.where(lo if half == 0 else jnp.logical_not(lo), tile, 0.0)


def _stack_group(tiles, kv_half):
    rows = []
    for t in tiles:
        for half in range(2):
            piece = _half_select(t, half)
            rows.append(piece if half == kv_half else pltpu.roll(piece, 64, 1))
    return jnp.concatenate(rows, axis=0)


def _unstack_group(stacked, kv_half):
    tiles = []
    for i in range(2):
        acc = None
        for half in range(2):
            piece = _half_select(stacked[BLOCK * (2 * i + half):BLOCK * (2 * i + half + 1)], kv_half)
            piece = piece if half == kv_half else pltpu.roll(piece, 64, 1)
            acc = piece if acc is None else acc + piece
        tiles.append(acc)
    return tiles


def _attn_specs(nb):
    last = nb - 1
    qi = lambda n: (jnp.minimum(n, last), 0)
    prev = lambda n: jnp.maximum(n - 1, 0)
    cur = lambda n: jnp.minimum(n, last)
    specs = [
        pl.BlockSpec((BLOCK, ATTN_W), qi),
        pl.BlockSpec((BLOCK, KV_W), lambda n: (cur(n), 4)),
        pl.BlockSpec((BLOCK, KV_W), lambda n: (prev(n), 4)),
        pl.BlockSpec((BLOCK, KV_W), lambda n: (cur(n), 5)),
        pl.BlockSpec((BLOCK, KV_W), lambda n: (prev(n), 5)),
        pl.BlockSpec((BLOCK, 512), lambda n: (cur(n), 3)),
        pl.BlockSpec((BLOCK, 512), lambda n: (cur(n), 4)),
        pl.BlockSpec((BLOCK, 256), lambda n: (cur(n), 0)),
        pl.BlockSpec((BLOCK, 256), lambda n: (prev(n), 0)),
        pl.BlockSpec((1, 128), lambda n: (0, 0)),
        pl.BlockSpec((1, 128), lambda n: (0, 0)),
        pl.BlockSpec((1, N_HEADS), lambda n: (0, 0)),
    ]
    return specs


def _rope_table(posf, invf):
    t = posf.shape[0]

    def body(p_ref, f_ref, o_ref):
        ang = p_ref[...] * f_ref[...]
        o_ref[...] = jnp.concatenate([jnp.cos(ang), jnp.sin(ang)], axis=1)

    return pl.pallas_call(
        body, name="rope_table", grid=(t // 512,),
        in_specs=[pl.BlockSpec((512, 1), lambda i: (i, 0)), pl.BlockSpec((1, 128), lambda i: (0, 0))],
        out_specs=pl.BlockSpec((512, 256), lambda i: (i, 0)),
        out_shape=jax.ShapeDtypeStruct((t, 256), F32), compiler_params=_params(("parallel",)),
    )(posf, invf)


def _attn_common(n, q_ref, kc_ref, kp_ref, vc_ref, vp_ref, rq_ref, rp_ref, qw_ref, kw_ref):
    cos_q, sin_q = rq_ref[:, 0:128], rq_ref[:, 128:256]
    cos_k = jnp.concatenate([rp_ref[:, 0:128], cos_q], axis=0)
    sin_k = jnp.concatenate([rp_ref[:, 128:256], sin_q], axis=0)
    k_raw = jnp.concatenate([kp_ref[...], kc_ref[...]], axis=0)
    vv = jnp.concatenate([vp_ref[...], vc_ref[...]], axis=0).astype(BF16)
    kk = [_prep_tile(k_raw[:, 128 * i:128 * i + 128], kw_ref[...], cos_k, sin_k).astype(BF16) for i in range(2)]
    vt = [vv[:, 128 * i:128 * i + 128] for i in range(2)]
    qv = q_ref[...]
    qr = [_head_rstd(qv[:, 128 * i:128 * i + 128]) for i in range(8)]
    qt = [_prep_tile(qv[:, 128 * i:128 * i + 128], qw_ref[...], cos_q, sin_q, qr[i]) for i in range(8)]
    return cos_q, sin_q, qr, kk, vt, qt


QK_SCALE = 1.0 / math.sqrt(HEAD_DIM)


def _group_sinks(sink_ref, g):
    return jnp.concatenate([jnp.broadcast_to(sink_ref[:, 4 * g + j:4 * g + j + 1], (BLOCK, 1)) for j in range(4)], axis=0)


def _group_softmax(q4, kk_t, sink, bias):
    s = _dot_nt(q4, kk_t) + bias
    m = jnp.maximum(jnp.max(s, axis=1, keepdims=True), sink)
    p = jnp.exp(s - m)
    es = jnp.exp(sink - m)
    inv = 1.0 / (jnp.sum(p, axis=1, keepdims=True) + es)
    return p * inv, es * inv


def _group_bias(n):
    return jnp.concatenate([jnp.where(_band_mask(n), 0.0, -1e30)] * 4, axis=0)


def _attn_fwd(proj, rope, qw, kw, sinks, later_shards):
    t = proj.shape[0]
    nb = t // BLOCK
    nt = len(later_shards)

    def body(q_ref, kc_ref, kp_ref, vc_ref, vp_ref, za0_ref, za1_ref, rq_ref, rp_ref, qw_ref, kw_ref,
             sink_ref, *rest):
        sh, o_ref, full = rest[:nt], rest[nt], rest[nt + 1:2 * nt + 1]
        ssem, rsem = rest[2 * nt + 1:]
        n = pl.program_id(0)
        start, wait = _bg_gather(sh, full, ssem, rsem)

        @pl.when(n == 0)
        def _():
            start()

        _, _, _, kk, vt, qt = _attn_common(n, q_ref, kc_ref, kp_ref, vc_ref, vp_ref, rq_ref, rp_ref, qw_ref, kw_ref)
        bias = _group_bias(n)
        tiles = []
        for g in range(N_KV_HEADS):
            q4 = (_stack_group(qt[2 * g:2 * g + 2], g % 2) * QK_SCALE).astype(BF16)
            p, _ = _group_softmax(q4, kk[g // 2], _group_sinks(sink_ref, g), bias)
            tiles += _unstack_group(_dot(p.astype(BF16), vt[g // 2]), g % 2)
        za = jnp.concatenate([za0_ref[...], za1_ref[...]], axis=1)
        o_ref[...] = jnp.concatenate(tiles, axis=1) * _silu(za)

        @pl.when(n == nb - 1)
        def _():
            wait()

    return pl.pallas_call(
        body, name="attn_fwd", grid=(nb,), in_specs=_attn_specs(nb) + [ANY] * nt,
        out_specs=[pl.BlockSpec((BLOCK, ATTN_W), lambda n: (n, 0))] + [ANY] * nt,
        out_shape=[jax.ShapeDtypeStruct((t, ATTN_W), F32)]
        + [jax.ShapeDtypeStruct((4,) + s.shape, s.dtype) for s in later_shards],
        scratch_shapes=[pltpu.SemaphoreType.DMA((4 * nt,)), pltpu.SemaphoreType.DMA((4 * nt,))],
        compiler_params=_params(("arbitrary",)),
    )(proj, proj, proj, proj, proj, proj, proj, rope, rope, qw, kw, sinks, *later_shards)


def _attn_bwd(proj, rope, qw, kw, sinks, doa, du, dzs, outgoing):
    t = proj.shape[0]
    nb = t // BLOCK
    last = nb - 1
    nt = len(outgoing)

    def body(q_ref, kc_ref, kp_ref, vc_ref, vp_ref, za0_ref, za1_ref, rq_ref, rp_ref, qw_ref, kw_ref,
             sink_ref, doa_ref, du_ref, dzs_ref, *rest):
        src = rest[:nt]
        dp_ref, gq_ref, gk_ref, gs_ref = rest[nt:nt + 4]
        dst = rest[nt + 4:2 * nt + 4]
        dkk_s, dvv_s, ck_s, cv_s, dq_s, dza_s, ssem, rsem = rest[2 * nt + 4:]
        n = pl.program_id(0)
        start, wait = _bg_scatter_devices(src, dst, ssem, rsem)

        @pl.when(n == 0)
        def _():
            start()
            gq_ref[...] = jnp.zeros_like(gq_ref)
            gk_ref[...] = jnp.zeros_like(gk_ref)
            gs_ref[...] = jnp.zeros_like(gs_ref)
            ck_s[...] = jnp.zeros_like(ck_s)
            cv_s[...] = jnp.zeros_like(cv_s)
            dq_s[...] = jnp.zeros_like(dq_s)
            dza_s[...] = jnp.zeros_like(dza_s)

        dp_ref[:, 0:ATTN_W] = dq_s[...]
        dp_ref[:, ATTN_W + 2 * KV_W:2 * ATTN_W + 2 * KV_W] = dza_s[...]
        dp_ref[:, 2 * ATTN_W + 2 * KV_W:IN_W - SSM_W] = du_ref[...]
        dp_ref[:, IN_W - SSM_W:IN_W] = dzs_ref[...]

        @pl.when(n == nb)
        def _():
            dkk_s[...] = jnp.zeros_like(dkk_s)
            dvv_s[...] = jnp.zeros_like(dvv_s)

        @pl.when(n < nb)
        def _():
            cos_q, sin_q, qr, kk, vt, qt = _attn_common(n, q_ref, kc_ref, kp_ref, vc_ref, vp_ref, rq_ref, rp_ref,
                                                        qw_ref, kw_ref)
            bias = _group_bias(n)
            za = jnp.concatenate([za0_ref[...], za1_ref[...]], axis=1)
            doa_v = doa_ref[...]
            do_full = doa_v * _silu(za)
            o_tiles, dq_tiles = [], []
            dkk = [jnp.zeros((2 * BLOCK, 128), F32) for _ in range(2)]
            dvv = [jnp.zeros((2 * BLOCK, 128), F32) for _ in range(2)]
            gsink = jnp.zeros((1, 128), F32)
            lane = _lane128()
            for g in range(N_KV_HEADS):
                q_b = (_stack_group(qt[2 * g:2 * g + 2], g % 2) * QK_SCALE).astype(BF16)
                do_b = _stack_group([do_full[:, 128 * i:128 * i + 128] for i in (2 * g, 2 * g + 1)], g % 2).astype(BF16)
                p, psink = _group_softmax(q_b, kk[g // 2], _group_sinks(sink_ref, g), bias)
                p_b = p.astype(BF16)
                dp = _dot_nt(do_b, vt[g // 2])
                delta = jnp.sum(p * dp, axis=1, keepdims=True)
                ds_b = (p * (dp - delta)).astype(BF16)
                sd = psink * delta
                for j in range(4):
                    gsink = gsink + jnp.where(lane == 4 * g + j, -jnp.sum(sd[BLOCK * j:BLOCK * (j + 1)]), 0.0)
                o_tiles += _unstack_group(_dot(p_b, vt[g // 2]), g % 2)
                dq_tiles += [d * QK_SCALE for d in _unstack_group(_dot(ds_b, kk[g // 2]), g % 2)]
                dkk[g // 2] = dkk[g // 2] + _dot_tn(ds_b, q_b)
                dvv[g // 2] = dvv[g // 2] + _dot_tn(p_b, do_b)
            dza_s[...] = (doa_v * jnp.concatenate(o_tiles, axis=1) * _dsilu(za)).astype(BF16)
            qv = q_ref[...]
            gq = jnp.zeros((1, 128), F32)
            out = []
            for i in range(8):
                d, dw = _prep_tile_bwd(qv[:, 128 * i:128 * i + 128], qw_ref[...], cos_q, sin_q, dq_tiles[i], qr[i])
                out.append(d)
                gq = gq + dw
            dq_s[...] = jnp.concatenate(out, axis=1).astype(BF16)
            gq_ref[...] += gq
            gs_ref[...] += gsink
            dkk_s[...] = jnp.concatenate(dkk, axis=1)
            dvv_s[...] = jnp.concatenate(dvv, axis=1)

        cos_p, sin_p = rp_ref[:, 0:128], rp_ref[:, 128:256]
        dk_prev = ck_s[...] + dkk_s[0:BLOCK, :]
        kp = kp_ref[...]
        gk = jnp.zeros((1, 128), F32)
        out = []
        for i in range(2):
            d, dw = _prep_tile_bwd(kp[:, 128 * i:128 * i + 128], kw_ref[...], cos_p, sin_p,
                                   dk_prev[:, 128 * i:128 * i + 128])
            out.append(d)
            gk = gk + dw
        dp_ref[:, ATTN_W:ATTN_W + KV_W] = jnp.concatenate(out, axis=1).astype(BF16)
        dp_ref[:, ATTN_W + KV_W:ATTN_W + 2 * KV_W] = (cv_s[...] + dvv_s[0:BLOCK, :]).astype(BF16)
        gk_ref[...] += gk
        ck_s[...] = dkk_s[BLOCK:2 * BLOCK, :]
        cv_s[...] = dvv_s[BLOCK:2 * BLOCK, :]

        @pl.when(n == nb)
        def _():
            wait()

    qblk = lambda n: (jnp.minimum(n, last), 0)
    kblk = lambda n: (jnp.maximum(n - 1, 0), 0)
    vec = pl.BlockSpec((1, 128), lambda n: (0, 0))
    return pl.pallas_call(
        body, name="attn_bwd", grid=(nb + 1,),
        in_specs=_attn_specs(nb) + [pl.BlockSpec((BLOCK, ATTN_W), qblk), pl.BlockSpec((BLOCK, SSM_W), kblk),
                                    pl.BlockSpec((BLOCK, SSM_W), kblk)] + [ANY] * nt,
        out_specs=[pl.BlockSpec((BLOCK, IN_W), kblk), vec, vec, vec] + [ANY] * nt,
        out_shape=[jax.ShapeDtypeStruct((t, IN_W), BF16), jax.ShapeDtypeStruct((1, 128), F32),
                   jax.ShapeDtypeStruct((1, 128), F32), jax.ShapeDtypeStruct((1, 128), F32)]
        + [jax.ShapeDtypeStruct(a.shape, a.dtype) for a in outgoing],
        scratch_shapes=[pltpu.VMEM((2 * BLOCK, KV_W), F32), pltpu.VMEM((2 * BLOCK, KV_W), F32),
                        pltpu.VMEM((BLOCK, KV_W), F32), pltpu.VMEM((BLOCK, KV_W), F32),
                        pltpu.VMEM((BLOCK, ATTN_W), BF16), pltpu.VMEM((BLOCK, ATTN_W), BF16),
                        pltpu.SemaphoreType.DMA((7 * nt,)), pltpu.SemaphoreType.DMA((7 * nt,))],
        compiler_params=_params(("arbitrary",)),
    )(proj, proj, proj, proj, proj, proj, proj, rope, rope, qw, kw, sinks, doa, du, dzs, *outgoing)


def _cmul(ar, ai, br, bi):
    return ar * br - ai * bi, ar * bi + ai * br


def _zoh(a_re, a_im, delta):
    e = jnp.exp(a_re * delta)
    lr, li = e * jnp.cos(a_im * delta), e * jnp.sin(a_im * delta)
    inv = 1.0 / (a_re * a_re + a_im * a_im)
    fr, fi = _cmul(lr - 1.0, li, a_re * inv, -a_im * inv)
    return lr, li, fr, fi


def _ssm_prep(a_re, a_im, log_step, b_re, b_im, seg_len):
    n_sq = int(round(math.log2(seg_len)))
    assert 2 ** n_sq == seg_len

    def body(ar_ref, ai_ref, ls_ref, arx_ref, aix_ref, br_ref, bi_ref, lr_ref, li_ref, pr_ref, pi_ref, bbr_ref, bbi_ref):
        delta = jnp.exp(ls_ref[...])
        lr, li, _, _ = _zoh(ar_ref[...], ai_ref[...], delta)
        lr_ref[...] = lr
        li_ref[...] = li
        pr, pi = lr, li
        for _ in range(n_sq):
            pr, pi = _cmul(pr, pi, pr, pi)
        pr_ref[...] = pr
        pi_ref[...] = pi
        _, _, fr, fi = _zoh(arx_ref[...], aix_ref[...], delta)
        bbr, bbi = _cmul(fr, fi, br_ref[...], bi_ref[...])
        bbr_ref[...] = bbr
        bbi_ref[...] = bbi

    gp = jax.ShapeDtypeStruct((SSM_G, SSM_P), F32)
    gx = jax.ShapeDtypeStruct((SSM_G, SSM_P * SSM_H), F32)
    return pl.pallas_call(body, name="ssm_prep", out_shape=[gp, gp, gp, gp, gx, gx])(
        a_re, a_im, log_step.reshape(SSM_G, 1), jnp.repeat(a_re, SSM_H, axis=1), jnp.repeat(a_im, SSM_H, axis=1),
        b_re.reshape(SSM_G, SSM_P * SSM_H), b_im.reshape(SSM_G, SSM_P * SSM_H))


def _ssm_param_grads(a_re, a_im, log_step, b_re, b_im, dlam_re, dlam_im, dbb_re, dbb_im):
    def body(ar_ref, ai_ref, ls_ref, arx_ref, aix_ref, br_ref, bi_ref, dlr_ref, dli_ref, dbr_ref, dbi_ref,
             gar_ref, gai_ref, gls_ref, gbr_ref, gbi_ref):
        delta = jnp.exp(ls_ref[...])
        ar, ai = ar_ref[...], ai_ref[...]
        lr, li, fr, fi = _zoh(ar, ai, delta)
        _, _, frx, fix = _zoh(arx_ref[...], aix_ref[...], delta)
        dbr, dbi = dbr_ref[...], dbi_ref[...]
        br, bi = br_ref[...], bi_ref[...]
        gbr, gbi = _cmul(frx, -fix, dbr, dbi)
        gbr_ref[...] = gbr
        gbi_ref[...] = gbi
        tr, ti = _cmul(br, -bi, dbr, dbi)
        row = lax.broadcasted_iota(jnp.int32, (SSM_P * SSM_H, SSM_P), 0)
        col = lax.broadcasted_iota(jnp.int32, (SSM_P * SSM_H, SSM_P), 1)
        fold = (row // SSM_H == col).astype(F32)
        dfr = jnp.dot(tr, fold, precision=lax.Precision.HIGHEST, preferred_element_type=F32)
        dfi = jnp.dot(ti, fold, precision=lax.Precision.HIGHEST, preferred_element_type=F32)
        inv = 1.0 / (ar * ar + ai * ai)
        ilr, ili = ar * inv, -ai * inv
        t1r, t1i = _cmul(dfr, dfi, ilr, -ili)
        dlbr, dlbi = dlr_ref[...] + t1r, dli_ref[...] + t1i
        qr, qi = _cmul(fr, fi, ilr, ili)
        t2r, t2i = _cmul(dfr, dfi, qr, -qi)
        glr, gli = -t2r, -t2i
        dzr, dzi = _cmul(dlbr, dlbi, lr, -li)
        gar_ref[...] = glr + dzr * delta
        gai_ref[...] = gli + dzi * delta
        gls_ref[...] = jnp.sum(dzr * ar + dzi * ai, axis=1, keepdims=True) * delta

    gp = jax.ShapeDtypeStruct((SSM_G, SSM_P), F32)
    gx = jax.ShapeDtypeStruct((SSM_G, SSM_P * SSM_H), F32)
    return pl.pallas_call(body, name="ssm_param_grads",
                          out_shape=[gp, gp, jax.ShapeDtypeStruct((SSM_G, 1), F32), gx, gx])(
        a_re, a_im, log_step.reshape(SSM_G, 1), jnp.repeat(a_re, SSM_H, axis=1), jnp.repeat(a_im, SSM_H, axis=1),
        b_re.reshape(SSM_G, SSM_P * SSM_H), b_im.reshape(SSM_G, SSM_P * SSM_H), dlam_re, dlam_im, dbb_re, dbb_im)


def _block_diag_in(bb):
    w = jnp.tile(bb.reshape(SSM_GB, SSM_ST, SSM_H), (1, 1, 16))
    row = lax.broadcasted_iota(jnp.int32, (1, SSM_ST, SSM_CH), 1) // SSM_P
    col = lax.broadcasted_iota(jnp.int32, (1, SSM_ST, SSM_CH), 2) // SSM_H
    return jnp.where(row == col, w, 0.0)


def _block_diag_out(c):
    w = jnp.tile(c.reshape(SSM_GB, SSM_CH, SSM_P), (1, 1, 16))
    row = lax.broadcasted_iota(jnp.int32, (1, SSM_CH, SSM_ST), 1) // SSM_H
    col = lax.broadcasted_iota(jnp.int32, (1, SSM_CH, SSM_ST), 2) // SSM_P
    return jnp.where(row == col, w, 0.0)


SEG_ROWS = SCAN_ROWS // N_SEG


def _chunk_perm():
    out_row = lax.broadcasted_iota(jnp.int32, (SCAN_ROWS, SCAN_ROWS), 0)
    in_row = lax.broadcasted_iota(jnp.int32, (SCAN_ROWS, SCAN_ROWS), 1)
    return (out_row == N_SEG * (in_row % SEG_ROWS) + in_row // SEG_ROWS).astype(BF16)


def _chunk_rows(j, seg_len, s):
    return pl.ds(pl.multiple_of(s * seg_len + j * SEG_ROWS, SEG_ROWS), SEG_ROWS)


def _gather_chunk(ref, j, seg_len):
    return jnp.concatenate([ref[_chunk_rows(j, seg_len, s), :] for s in range(N_SEG)], axis=0)


def _scatter_chunk(ref, j, seg_len, val):
    for s in range(N_SEG):
        ref[_chunk_rows(j, seg_len, s), :] = val[s * SEG_ROWS:(s + 1) * SEG_ROWS]


def _interleave(perm, x_b):
    return _dot(perm, x_b).astype(BF16)


def _scan_fwd(src_ref, dst_ref, lam_r_ref, lam_i_ref, init_ref, final_ref, steps):
    for k in range(SSM_ST // SCAN_LW):
        re = pl.ds(k * SCAN_LW, SCAN_LW)
        im = pl.ds(SSM_ST + k * SCAN_LW, SCAN_LW)
        lr, li = lam_r_ref[:, re], lam_i_ref[:, re]

        def step(i, carry, re=re, im=im, lr=lr, li=li):
            hr, hi = carry
            rows = pl.ds(pl.multiple_of(i * 8, 8), 8)
            nr = lr * hr - li * hi + src_ref[rows, re]
            ni = lr * hi + li * hr + src_ref[rows, im]
            if dst_ref is not None:
                dst_ref[rows, re] = nr
                dst_ref[rows, im] = ni
            return nr, ni

        hr, hi = lax.fori_loop(0, steps, step, (init_ref[:, re], init_ref[:, im]), unroll=True)
        final_ref[:, re] = hr
        final_ref[:, im] = hi


def _ssm_specs(t):
    col = lambda g: (0, g)
    gb3 = lambda g: (g, 0, 0)
    return dict(
        rows=pl.BlockSpec((t, SSM_CH), col),
        lam=pl.BlockSpec((None, N_SEG, SSM_ST), gb3),
        w_in=pl.BlockSpec((None, 2 * SSM_ST, SSM_CH), gb3),
        w_out=pl.BlockSpec((None, SSM_CH, 2 * SSM_ST), gb3),
        vec=pl.BlockSpec((1, SSM_CH), col),
    )


def _segment_states(x_ref, pw_r_ref, pw_i_ref, out_ref, reverse):
    re, im = pl.ds(0, SSM_ST), pl.ds(SSM_ST, SSM_ST)
    pr, pi = pw_r_ref[0:1, :], pw_i_ref[0:1, :]
    first = N_SEG - 1 if reverse else 0
    out_ref[first:first + 1, :] = jnp.zeros((1, 2 * SSM_ST), F32)
    order = range(N_SEG - 1, 0, -1) if reverse else range(N_SEG - 1)
    for s in order:
        d = s - 1 if reverse else s + 1
        hr, hi = out_ref[s:s + 1, re], out_ref[s:s + 1, im]
        if reverse:
            nr, ni = pr * hr + pi * hi, pr * hi - pi * hr
        else:
            nr, ni = pr * hr - pi * hi, pr * hi + pi * hr
        out_ref[d:d + 1, re] = nr + x_ref[s:s + 1, re]
        out_ref[d:d + 1, im] = ni + x_ref[s:s + 1, im]


def _ssm_fwd(proj, lam_r, lam_i, pw_r, pw_i, w_in, w_out, d_skip):
    t = proj.shape[0]
    seg_len = t // N_SEG
    nch = t // SCAN_ROWS
    steps = SCAN_ROWS // N_SEG
    sp = _ssm_specs(t)

    def body(u_ref, lr_ref, li_ref, pr_ref, pi_ref, wi_ref, wo_ref, d_ref, y_ref, hc_ref, bu_s, car_s, seg_s, ub_s,
             y0_s, y1_s):
        perm = _chunk_perm()
        car_s[...] = jnp.zeros_like(car_s)

        def chunk1(j, c):
            rows = pl.ds(pl.multiple_of(j * SCAN_ROWS, SCAN_ROWS), SCAN_ROWS)
            ub_s[rows, :] = _interleave(perm, _gather_chunk(u_ref, j, seg_len).astype(BF16))
            bu_s[...] = _dot_nt(ub_s[rows, :], wi_ref[...])
            _scan_fwd(bu_s, None, lr_ref, li_ref, car_s, car_s, steps)
            return c

        lax.fori_loop(0, nch, chunk1, 0)
        _segment_states(car_s, pr_ref, pi_ref, seg_s, reverse=False)
        car_s[...] = seg_s[...]

        def chunk2(j, c):
            rows = pl.ds(pl.multiple_of(j * SCAN_ROWS, SCAN_ROWS), SCAN_ROWS)
            bu_s[...] = _dot_nt(ub_s[rows, :], wi_ref[...])
            hc_ref[j] = car_s[...]
            _scan_fwd(bu_s, bu_s, lr_ref, li_ref, car_s, car_s, steps)
            yv = _dot_nt(bu_s[...].astype(BF16), wo_ref[...])
            y0_s[...] = yv[:, 0:128]
            y1_s[...] = yv[:, 128:256]
            for s in range(N_SEG):
                nat = _chunk_rows(j, seg_len, s)
                sub = pl.ds(s, SEG_ROWS, stride=N_SEG)
                y_ref[nat, :] = jnp.concatenate([y0_s[sub, :], y1_s[sub, :]], axis=1) + d_ref[...] * u_ref[nat, :]
            return c

        lax.fori_loop(0, nch, chunk2, 0)

    u_cols = 2560 // SSM_CH
    return pl.pallas_call(
        body, name="ssm_fwd", grid=(SSM_GB,),
        in_specs=[pl.BlockSpec((t, SSM_CH), lambda g: (0, u_cols + g)), sp["lam"], sp["lam"], sp["lam"], sp["lam"],
                  sp["w_in"], sp["w_out"], sp["vec"]],
        out_specs=[sp["rows"], pl.BlockSpec((None, nch, N_SEG, 2 * SSM_ST), lambda g: (g, 0, 0, 0))],
        out_shape=[jax.ShapeDtypeStruct((t, SSM_W), F32), jax.ShapeDtypeStruct((SSM_GB, nch, N_SEG, 2 * SSM_ST), F32)],
        scratch_shapes=[pltpu.VMEM((SCAN_ROWS, 2 * SSM_ST), F32), pltpu.VMEM((N_SEG, 2 * SSM_ST), F32),
                        pltpu.VMEM((N_SEG, 2 * SSM_ST), F32), pltpu.VMEM((t, SSM_CH), BF16),
                        pltpu.VMEM((SCAN_ROWS, 128), F32), pltpu.VMEM((SCAN_ROWS, 128), F32)],
        compiler_params=_params(("parallel",)),
    )(proj, lam_r, lam_i, pw_r, pw_i, w_in, w_out, d_skip)


def _group_blocks(full):
    row_g = lax.broadcasted_iota(jnp.int32, (SSM_CH, SSM_ST), 0) // SSM_H
    col_g = lax.broadcasted_iota(jnp.int32, (SSM_CH, SSM_ST), 1) // SSM_P
    fold = (lax.broadcasted_iota(jnp.int32, (SSM_ST, SSM_P), 0) % SSM_P
            == lax.broadcasted_iota(jnp.int32, (SSM_ST, SSM_P), 1)).astype(F32)
    parts = [jnp.dot(jnp.where(row_g == col_g, full[:, k * SSM_ST:(k + 1) * SSM_ST], 0.0), fold,
                     precision=lax.Precision.HIGHEST, preferred_element_type=F32) for k in range(2)]
    return jnp.concatenate(parts, axis=1)


def _ssm_bwd(proj, dy, hc, lam_r, lam_i, pw_r, pw_i, w_in, w_out, d_skip):
    t = proj.shape[0]
    seg_len = t // N_SEG
    nch = t // SCAN_ROWS
    steps = SCAN_ROWS // N_SEG
    sp = _ssm_specs(t)

    def body(u_ref, dy_ref, hc_ref, lr_ref, li_ref, pr_ref, pi_ref, wi_ref, wo_ref, d_ref,
             du_ref, gbi_ref, gbo_ref, glam_ref, gd_ref, bu_s, h_s, e_s, car_s, seg_s, acc_s, gwi_ref, gwo_ref,
             dyb_s):
        perm = _chunk_perm()

        def chunk_rows(j):
            return pl.ds(pl.multiple_of(j * SCAN_ROWS, SCAN_ROWS), SCAN_ROWS)

        def interleaved(ref, j):
            return _interleave(perm, _gather_chunk(ref, j, seg_len).astype(BF16))

        def load_e(j):
            dyb_s[chunk_rows(j), :] = interleaved(dy_ref, j)
            e_s[...] = _dot(dyb_s[chunk_rows(j), :], wo_ref[...])

        def scan_rev(j, accumulate):
            for k in range(SSM_ST // SCAN_LW):
                re = pl.ds(k * SCAN_LW, SCAN_LW)
                im = pl.ds(SSM_ST + k * SCAN_LW, SCAN_LW)
                lr, li = lr_ref[:, re], li_ref[:, re]

                def step(ii, carry, re=re, im=im, lr=lr, li=li):
                    i = steps - 1 - ii
                    rows = pl.ds(pl.multiple_of(i * 8, 8), 8)
                    if accumulate:
                        gr, gi, ar, ai = carry
                    else:
                        gr, gi = carry
                    nr = lr * gr + li * gi + e_s[rows, re]
                    ni = lr * gi - li * gr + e_s[rows, im]
                    if not accumulate:
                        return nr, ni
                    e_s[rows, re] = nr
                    e_s[rows, im] = ni
                    pr_, pi_ = h_s[rows, re], h_s[rows, im]
                    return nr, ni, ar + nr * pr_ + ni * pi_, ai + ni * pr_ - nr * pi_

                init = (car_s[:, re], car_s[:, im])
                if accumulate:
                    init = init + (acc_s[:, re], acc_s[:, im])
                out = lax.fori_loop(0, steps, step, init, unroll=True)
                car_s[:, re] = out[0]
                car_s[:, im] = out[1]
                if accumulate:
                    acc_s[:, re] = out[2]
                    acc_s[:, im] = out[3]

        car_s[...] = jnp.zeros_like(car_s)

        def pass1(jj, c):
            load_e(nch - 1 - jj)
            scan_rev(nch - 1 - jj, False)
            return c

        lax.fori_loop(0, nch, pass1, 0)
        _segment_states(car_s, pr_ref, pi_ref, seg_s, reverse=True)
        car_s[...] = seg_s[...]
        acc_s[...] = jnp.zeros_like(acc_s)
        gwi_ref[...] = jnp.zeros_like(gwi_ref)
        gwo_ref[...] = jnp.zeros_like(gwo_ref)
        gd_ref[...] = jnp.zeros_like(gd_ref)

        def pass2(jj, c):
            j = nch - 1 - jj
            u_b, dy_b = interleaved(u_ref, j), dyb_s[chunk_rows(j), :]
            bu_s[...] = _dot_nt(u_b, wi_ref[...])
            h_s[0:N_SEG, :] = hc_ref[j]
            seg_s[...] = hc_ref[j]
            _scan_fwd(bu_s, h_s.at[pl.ds(N_SEG, SCAN_ROWS), :], lr_ref, li_ref, seg_s, seg_s, steps)
            e_s[...] = _dot(dy_b, wo_ref[...])
            scan_rev(j, True)
            g_b = e_s[...].astype(BF16)
            du_b = (_dot(g_b, wi_ref[...]) + d_ref[...] * dy_b.astype(F32)).astype(BF16)
            _scatter_chunk(du_ref, j, seg_len, _dot_tn(perm, du_b).astype(du_ref.dtype))
            gwi_ref[...] += _dot_tn(u_b, g_b)
            gwo_ref[...] += _dot_tn(dy_b, h_s[pl.ds(N_SEG, SCAN_ROWS), :].astype(BF16))
            gd_ref[...] += jnp.sum(_gather_chunk(dy_ref, j, seg_len) * _gather_chunk(u_ref, j, seg_len), axis=0,
                                   keepdims=True)
            return c

        lax.fori_loop(0, nch, pass2, 0)
        glam_ref[...] = jnp.sum(acc_s[...], axis=0, keepdims=True)
        gbi_ref[...] = _group_blocks(gwi_ref[...])
        gbo_ref[...] = _group_blocks(gwo_ref[...])

    mat = pl.BlockSpec((None, SSM_CH, 2 * SSM_P), lambda g: (g, 0, 0))
    u_cols = 2560 // SSM_CH
    return pl.pallas_call(
        body, name="ssm_bwd", grid=(SSM_GB,),
        in_specs=[pl.BlockSpec((t, SSM_CH), lambda g: (0, u_cols + g)), sp["rows"],
                  pl.BlockSpec((None, nch, N_SEG, 2 * SSM_ST), lambda g: (g, 0, 0, 0)),
                  sp["lam"], sp["lam"], sp["lam"], sp["lam"], sp["w_in"], sp["w_out"], sp["vec"]],
        out_specs=[sp["rows"], mat, mat, pl.BlockSpec((None, 1, 2 * SSM_ST), lambda g: (g, 0, 0)), sp["vec"]],
        out_shape=[jax.ShapeDtypeStruct((t, SSM_W), BF16), jax.ShapeDtypeStruct((SSM_GB, SSM_CH, 2 * SSM_P), F32),
                   jax.ShapeDtypeStruct((SSM_GB, SSM_CH, 2 * SSM_P), F32),
                   jax.ShapeDtypeStruct((SSM_GB, 1, 2 * SSM_ST), F32), jax.ShapeDtypeStruct((1, SSM_W), F32)],
        scratch_shapes=[pltpu.VMEM((SCAN_ROWS, 2 * SSM_ST), F32), pltpu.VMEM((SCAN_ROWS + N_SEG, 2 * SSM_ST), F32),
                        pltpu.VMEM((SCAN_ROWS, 2 * SSM_ST), F32), pltpu.VMEM((N_SEG, 2 * SSM_ST), F32),
                        pltpu.VMEM((N_SEG, 2 * SSM_ST), F32), pltpu.VMEM((N_SEG, 2 * SSM_ST), F32),
                        pltpu.VMEM((SSM_CH, 2 * SSM_ST), F32), pltpu.VMEM((SSM_CH, 2 * SSM_ST), F32),
                        pltpu.VMEM((t, SSM_CH), BF16)],
        compiler_params=_params(("parallel",)),
    )(proj, dy, hc, lam_r, lam_i, pw_r, pw_i, w_in, w_out, d_skip)


def _z_ssm_specs(tm):
    return [pl.BlockSpec((tm, 512), lambda i: (i, 7)), pl.BlockSpec((tm, 512), lambda i: (i, 8))]


def _glu_fwd(y, proj, w_glu, b_glu):
    t = y.shape[0]
    tm = 512

    def body(y_ref, z0_ref, z1_ref, w_ref, b_ref, o_ref, yg_ref):
        yg = _gelu(y_ref[...])
        yg_b = yg.astype(BF16)
        a = _dot(yg_b, w_ref[...]) + b_ref[...]
        z = jnp.concatenate([z0_ref[...], z1_ref[...]], axis=1)
        o_ref[...] = yg * _sigmoid(a) * _silu(z)
        yg_ref[...] = yg_b

    row = pl.BlockSpec((tm, SSM_W), lambda i: (i, 0))
    return pl.pallas_call(
        body, name="glu_fwd", grid=(t // tm,),
        in_specs=[row] + _z_ssm_specs(tm) + [pl.BlockSpec((SSM_W, SSM_W), lambda i: (0, 0)),
                                            pl.BlockSpec((1, SSM_W), lambda i: (0, 0))],
        out_specs=[row, row],
        out_shape=[jax.ShapeDtypeStruct((t, SSM_W), F32), jax.ShapeDtypeStruct((t, SSM_W), BF16)],
        compiler_params=_params(("parallel",)),
    )(y, proj, proj, w_glu, b_glu)


def _glu_bwd(y, proj, dos, w_glu, b_glu):
    t = y.shape[0]
    tm = 512

    def body(y_ref, z0_ref, z1_ref, do_ref, w_ref, b_ref, dy_ref, dz_ref, da_ref, gb_ref):
        @pl.when(pl.program_id(0) == 0)
        def _():
            gb_ref[...] = jnp.zeros_like(gb_ref)

        z = jnp.concatenate([z0_ref[...], z1_ref[...]], axis=1)
        yv, do = y_ref[...], do_ref[...]
        yg = _gelu(yv)
        sg = _sigmoid(_dot(yg.astype(BF16), w_ref[...]) + b_ref[...])
        dy2 = do * _silu(z)
        dz_ref[...] = (do * yg * sg * _dsilu(z)).astype(BF16)
        da = dy2 * yg * sg * (1.0 - sg)
        da_b = da.astype(BF16)
        da_ref[...] = da_b
        gb_ref[...] += jnp.sum(da, axis=0, keepdims=True)
        dyg = dy2 * sg + _dot_nt(da_b, w_ref[...])
        dy_ref[...] = dyg * _dgelu(yv)

    row = pl.BlockSpec((tm, SSM_W), lambda i: (i, 0))
    vec = pl.BlockSpec((1, SSM_W), lambda i: (0, 0))
    return pl.pallas_call(
        body, name="glu_bwd", grid=(t // tm,),
        in_specs=[row] + _z_ssm_specs(tm) + [row, pl.BlockSpec((SSM_W, SSM_W), lambda i: (0, 0)), vec],
        out_specs=[row, row, row, vec],
        out_shape=[jax.ShapeDtypeStruct((t, SSM_W), F32), jax.ShapeDtypeStruct((t, SSM_W), BF16),
                   jax.ShapeDtypeStruct((t, SSM_W), BF16), jax.ShapeDtypeStruct((1, SSM_W), F32)],
        compiler_params=_params(("arbitrary",)),
    )(y, proj, proj, dos, w_glu, b_glu)


def _rms(o):
    return lax.rsqrt(jnp.mean(o * o, axis=1, keepdims=True) + NORM_EPS)


def _outproj(oa, os_, aw, sw, w_out, x, target):
    t = x.shape[0]
    tm = 256

    def body(oa_ref, os_ref, aw_ref, sw_ref, w_ref, x_ref, t_ref, mg_ref, do_ref, ls_ref):
        @pl.when(pl.program_id(0) == 0)
        def _():
            ls_ref[...] = jnp.zeros_like(ls_ref)

        a, s = oa_ref[...], os_ref[...]
        merged = jnp.concatenate([a * _rms(a) * aw_ref[...], s * _rms(s) * sw_ref[...]], axis=1).astype(BF16)
        mg_ref[...] = merged
        err = x_ref[...] + _dot(merged, w_ref[...]) - t_ref[...]
        do_ref[...] = err * (1.0 / D_MODEL)
        ls_ref[...] += jnp.sum(err * err)

    half = pl.BlockSpec((tm, ATTN_W), lambda i: (i, 0))
    full = pl.BlockSpec((tm, D_MODEL), lambda i: (i, 0))
    vec = pl.BlockSpec((1, ATTN_W), lambda i: (0, 0))
    return pl.pallas_call(
        body, name="outproj", grid=(t // tm,),
        in_specs=[half, half, vec, vec, pl.BlockSpec((D_MODEL, D_MODEL), lambda i: (0, 0)), full, full],
        out_specs=[full, full, pl.BlockSpec((8, 128), lambda i: (0, 0))],
        out_shape=[jax.ShapeDtypeStruct((t, D_MODEL), BF16), jax.ShapeDtypeStruct((t, D_MODEL), F32),
                   jax.ShapeDtypeStruct((8, 128), F32)],
        compiler_params=_params(("arbitrary",)),
    )(oa, os_, aw, sw, w_out, x, target)


def _outproj_bwd(dout, oa, os_, aw, sw, w_out):
    t = dout.shape[0]
    tm = 256

    def norm_bwd(o, w, dm):
        r = _rms(o)
        yh = o * r
        gh = dm * w
        return r * (gh - yh * jnp.mean(gh * yh, axis=1, keepdims=True)), jnp.sum(dm * yh, axis=0, keepdims=True)

    def body(do_ref, oa_ref, os_ref, aw_ref, sw_ref, w_ref, da_ref, ds_ref, ga_ref, gs_ref):
        @pl.when(pl.program_id(0) == 0)
        def _():
            ga_ref[...] = jnp.zeros_like(ga_ref)
            gs_ref[...] = jnp.zeros_like(gs_ref)

        dm = _dot_nt(do_ref[...].astype(BF16), w_ref[...])
        da, ga = norm_bwd(oa_ref[...], aw_ref[...], dm[:, :ATTN_W])
        ds, gs = norm_bwd(os_ref[...], sw_ref[...], dm[:, ATTN_W:])
        da_ref[...] = da
        ds_ref[...] = ds
        ga_ref[...] += ga
        gs_ref[...] += gs

    half = pl.BlockSpec((tm, ATTN_W), lambda i: (i, 0))
    full = pl.BlockSpec((tm, D_MODEL), lambda i: (i, 0))
    vec = pl.BlockSpec((1, ATTN_W), lambda i: (0, 0))
    return pl.pallas_call(
        body, name="outproj_bwd", grid=(t // tm,),
        in_specs=[full, half, half, vec, vec, pl.BlockSpec((D_MODEL, D_MODEL), lambda i: (0, 0))],
        out_specs=[half, half, vec, vec],
        out_shape=[jax.ShapeDtypeStruct((t, ATTN_W), F32), jax.ShapeDtypeStruct((t, ATTN_W), F32),
                   jax.ShapeDtypeStruct((1, ATTN_W), F32), jax.ShapeDtypeStruct((1, ATTN_W), F32)],
        compiler_params=_params(("arbitrary",)),
    )(dout, oa, os_, aw, sw, w_out)


def _inproj_bwd(dproj, w_slabs, x, norm_w, dout, outgoing):
    t = x.shape[0]
    tm = 512
    nc = 4
    nt = len(outgoing)
    ni = t // tm

    def body(dp_ref, w_ref, x_ref, nw_ref, do_ref, *rest):
        src, (gx_ref, gw_ref), dst = rest[:nt], rest[nt:nt + 2], rest[nt + 2:2 * nt + 2]
        acc_ref, ssem, rsem = rest[2 * nt + 2:]
        i, j = pl.program_id(0), pl.program_id(1)
        start, wait = _bg_scatter_chips(src, dst, ssem, rsem)

        @pl.when((i == 0) & (j == 0))
        def _():
            start()
            gw_ref[...] = jnp.zeros_like(gw_ref)

        @pl.when(j == 0)
        def _():
            acc_ref[...] = jnp.zeros_like(acc_ref)

        acc_ref[...] += _dot_nt(dp_ref[...], w_ref[...])

        @pl.when(j == nc - 1)
        def _():
            xv = x_ref[...]
            r = lax.rsqrt(jnp.mean(xv * xv, axis=1, keepdims=True) + NORM_EPS)
            yh = xv * r
            dh = acc_ref[...]
            gh = dh * nw_ref[...]
            gx_ref[...] = do_ref[...] + r * (gh - yh * jnp.mean(gh * yh, axis=1, keepdims=True))
            gw_ref[...] += jnp.sum(dh * yh, axis=0, keepdims=True)

        @pl.when((i == ni - 1) & (j == nc - 1))
        def _():
            wait()

    full = pl.BlockSpec((tm, D_MODEL), lambda i, j: (i, 0))
    vec = pl.BlockSpec((1, D_MODEL), lambda i, j: (0, 0))
    return pl.pallas_call(
        body, name="inproj_bwd", grid=(ni, nc),
        in_specs=[pl.BlockSpec((tm, SHARD_W), lambda i, j: (i, j)),
                  pl.BlockSpec((None, D_MODEL, SHARD_W), lambda i, j: (j, 0, 0)), full, vec, full] + [ANY] * nt,
        out_specs=[full, vec] + [ANY] * nt,
        out_shape=[jax.ShapeDtypeStruct((t, D_MODEL), F32), jax.ShapeDtypeStruct((1, D_MODEL), F32)]
        + [jax.ShapeDtypeStruct(a.shape, a.dtype) for a in outgoing],
        scratch_shapes=[pltpu.VMEM((tm, D_MODEL), F32), pltpu.SemaphoreType.DMA((3 * nt,)),
                        pltpu.SemaphoreType.DMA((3 * nt,))],
        compiler_params=_params(("arbitrary", "arbitrary")),
    )(dproj, w_slabs, x, norm_w.reshape(1, D_MODEL), dout, *outgoing)


def _adamw_math(w_ref, g_ref, m_ref, v_ref, d_ref, nm_ref, nv_ref):
    gv = g_ref[...]
    nm = ADAM_B1 * m_ref[...] + (1.0 - ADAM_B1) * gv
    nv = ADAM_B2 * v_ref[...] + (1.0 - ADAM_B2) * (gv * gv)
    m_hat = nm / (1.0 - ADAM_B1 ** ADAM_STEP)
    v_hat = nv / (1.0 - ADAM_B2 ** ADAM_STEP)
    d_ref[...] = -ADAM_LR * (m_hat / (jnp.sqrt(v_hat) + ADAM_EPS) + ADAM_WD * w_ref[...])
    nm_ref[...] = nm
    nv_ref[...] = nv


def _adamw_halves(w, mine, theirs, m, v, c_idx, *, rows, name):
    hr, cols = mine.shape
    nblk = hr // rows

    def body(c_ref, w_ref, a_ref, b_ref, m_ref, v_ref, g_ref, d_ref, nm_ref, nv_ref):
        g_ref[...] = jnp.where(pl.program_id(0) == c_ref[0], a_ref[...], b_ref[...])
        _adamw_math(w_ref, g_ref, m_ref, v_ref, d_ref, nm_ref, nv_ref)

    full = pl.BlockSpec((rows, cols), lambda h, i, c: (h * nblk + i, 0))
    part = pl.BlockSpec((rows, cols), lambda h, i, c: (i, 0))
    shp = jax.ShapeDtypeStruct((2 * hr, cols), F32)
    return pl.pallas_call(
        body, name=name,
        grid_spec=pltpu.PrefetchScalarGridSpec(num_scalar_prefetch=1, grid=(2, nblk),
                                               in_specs=[full, part, part, full, full], out_specs=[full] * 4),
        out_shape=[shp] * 4, compiler_params=_params(("parallel", "parallel")),
    )(c_idx, w, mine, theirs, m, v)


def _adamw(w, g, m, v, *, rows, name):
    r, c = w.shape

    def body(w_ref, g_ref, m_ref, v_ref, d_ref, nm_ref, nv_ref):
        _adamw_math(w_ref, g_ref, m_ref, v_ref, d_ref, nm_ref, nv_ref)

    blk = pl.BlockSpec((rows, c), lambda i: (i, 0))
    shp = jax.ShapeDtypeStruct((r, c), F32)
    return pl.pallas_call(body, name=name, grid=(r // rows,), in_specs=[blk] * 4, out_specs=[blk] * 3,
                          out_shape=[shp] * 3, compiler_params=_params(("parallel",)))(w, g, m, v)


def _remote(src, dst, ssem, rsem, dev):
    return pltpu.make_async_remote_copy(src_ref=src, dst_ref=dst, send_sem=ssem, recv_sem=rsem, device_id=dev,
                                        device_id_type=pl.DeviceIdType.MESH)


def _mesh_pos():
    return lax.axis_index("x"), lax.axis_index("y"), lax.axis_index("c")


def _other_chips(x, y):
    return [(1 - x, y), (x, 1 - y), (1 - x, 1 - y)]


def _flips():
    return [(dx, dy, dc) for dx in (0, 1) for dy in (0, 1) for dc in (0, 1) if (dx, dy, dc) != (0, 0, 0)]


def _background(sends, arrivals):
    def start():
        for cp in sends():
            cp.start()

    def wait():
        for cp in arrivals():
            cp.wait_recv()
        for cp in sends():
            cp.wait_send()

    return start, wait


def _bg_gather(sh, full, ssem, rsem):
    x, y, c = _mesh_pos()
    me = 2 * x + y
    peers = [(px, py, c) for px, py in _other_chips(x, y)] + [(x, y, 1 - c)]
    slots = [2 * px + py for px, py in _other_chips(x, y)] + [me]
    pairs = [(i, k) for i in range(len(sh)) for k in range(4)]
    return _background(
        lambda: [_remote(sh[i], full[i].at[me], ssem.at[4 * i + k], rsem.at[4 * i + k], peers[k]) for i, k in pairs],
        lambda: [_remote(full[i].at[slots[k]], full[i].at[slots[k]], ssem.at[4 * i + k], rsem.at[4 * i + k], peers[k])
                 for i, k in pairs])


def _bg_scatter_devices(src, dst, ssem, rsem):
    x, y, c = _mesh_pos()
    me = 4 * x + 2 * y + c
    peers = []
    for dx, dy, dc in _flips():
        px, py, pc = jnp.bitwise_xor(x, dx), jnp.bitwise_xor(y, dy), jnp.bitwise_xor(c, dc)
        peers.append(((px, py, pc), 4 * px + 2 * py + pc))
    pairs = [(i, k) for i in range(len(src)) for k in range(7)]
    return _background(
        lambda: [_remote(src[i].at[peers[k][1]], dst[i].at[me], ssem.at[7 * i + k], rsem.at[7 * i + k], peers[k][0])
                 for i, k in pairs],
        lambda: [_remote(dst[i].at[peers[k][1]], dst[i].at[peers[k][1]], ssem.at[7 * i + k], rsem.at[7 * i + k],
                         peers[k][0]) for i, k in pairs])


def _bg_scatter_chips(src, dst, ssem, rsem):
    x, y, c = _mesh_pos()
    me = 2 * x + y
    chips = _other_chips(x, y)
    pairs = [(i, k) for i in range(len(src)) for k in range(3)]
    slot = lambda k: 2 * chips[k][0] + chips[k][1]
    return _background(
        lambda: [_remote(src[i].at[slot(k)], dst[i].at[me], ssem.at[3 * i + k], rsem.at[3 * i + k], (*chips[k], c))
                 for i, k in pairs],
        lambda: [_remote(dst[i].at[slot(k)], dst[i].at[slot(k)], ssem.at[3 * i + k], rsem.at[3 * i + k], (*chips[k], c))
                 for i, k in pairs])


def _pair_swap(arrays):
    nt = len(arrays)

    def body(*refs):
        src, dst = refs[:nt], refs[nt:2 * nt]
        ssem, rsem = refs[2 * nt:]
        x, y, c = _mesh_pos()
        cps = [_remote(src[i].at[:, 1 - c], dst[i], ssem.at[i], rsem.at[i], (x, y, 1 - c)) for i in range(nt)]
        for cp in cps:
            cp.start()
        for cp in cps:
            cp.wait_recv()
        for cp in cps:
            cp.wait_send()

    return pl.pallas_call(
        body, name="pair_swap", in_specs=[ANY] * nt, out_specs=[ANY] * nt,
        out_shape=[jax.ShapeDtypeStruct((4,) + a.shape[2:], a.dtype) for a in arrays],
        scratch_shapes=[pltpu.SemaphoreType.DMA((nt,)), pltpu.SemaphoreType.DMA((nt,))],
    )(*arrays)


def _half_swap(arrays):
    nt = len(arrays)

    def body(*refs):
        src, dst = refs[:nt], refs[nt:2 * nt]
        ssem, rsem = refs[2 * nt:]
        x, y, c = _mesh_pos()
        cps = [_remote(src[i], dst[i], ssem.at[i], rsem.at[i], (x, y, 1 - c)) for i in range(nt)]
        for cp in cps:
            cp.start()
        for cp in cps:
            cp.wait_recv()
        for cp in cps:
            cp.wait_send()

    return pl.pallas_call(
        body, name="half_swap", in_specs=[ANY] * nt, out_specs=[ANY] * nt,
        out_shape=[jax.ShapeDtypeStruct(a.shape, a.dtype) for a in arrays],
        scratch_shapes=[pltpu.SemaphoreType.DMA((nt,)), pltpu.SemaphoreType.DMA((nt,))],
    )(*arrays)


def _exchange_slices(src, scatter, name):
    def body(src_ref, dst_ref, ssem, rsem, lsem):
        x, y, c = _mesh_pos()
        me = 4 * x + 2 * y + c
        local = pltpu.make_async_copy(src_ref.at[me] if scatter else src_ref, dst_ref.at[me], lsem)
        local.start()
        cps = []
        for k, (dx, dy, dc) in enumerate(_flips()):
            px, py, pc = jnp.bitwise_xor(x, dx), jnp.bitwise_xor(y, dy), jnp.bitwise_xor(c, dc)
            peer = 4 * px + 2 * py + pc
            cp = _remote(src_ref.at[peer] if scatter else src_ref, dst_ref.at[me], ssem.at[k], rsem.at[k],
                         (px, py, pc))
            cp.start()
            cps.append((cp, peer))
        for k, (cp, peer) in enumerate(cps):
            slot = dst_ref.at[peer]
            _remote(slot, slot, ssem.at[k], rsem.at[k], (x, y, c)).wait_recv()
        for cp, _ in cps:
            cp.wait_send()
        local.wait()

    return pl.pallas_call(
        body, name=name, in_specs=[ANY], out_specs=ANY,
        out_shape=jax.ShapeDtypeStruct((8,) + src.shape[-2:], src.dtype),
        scratch_shapes=[pltpu.SemaphoreType.DMA((7,)), pltpu.SemaphoreType.DMA((7,)), pltpu.SemaphoreType.DMA],
    )(src)


def _add_halves(g, recv, c_idx, *, rows, name):
    _, _, hr, cols = g.shape

    def body(c_ref, g_ref, r_ref, o_ref):
        o_ref[...] = (g_ref[...] + r_ref[...].astype(F32)).astype(BF16)

    return pl.pallas_call(
        body, name=name,
        grid_spec=pltpu.PrefetchScalarGridSpec(
            num_scalar_prefetch=1, grid=(4, hr // rows),
            in_specs=[pl.BlockSpec((None, None, rows, cols), lambda j, i, c: (j, c[0], i, 0)),
                      pl.BlockSpec((None, rows, cols), lambda j, i, c: (j, i, 0))],
            out_specs=pl.BlockSpec((None, rows, cols), lambda j, i, c: (j, i, 0))),
        out_shape=jax.ShapeDtypeStruct((4, hr, cols), BF16),
        compiler_params=_params(("parallel", "parallel")),
    )(c_idx, g, recv)


def _sum_peers(slots, own, idx, *, rows, name):
    n, r, cols = slots.shape

    def body(me_ref, *refs):
        me = me_ref[0]
        mine = refs[n][...].astype(F32)
        acc = None
        for k in range(n):
            term = jnp.where(me == k, mine, refs[k][...].astype(F32))
            acc = term if acc is None else acc + term
        refs[n + 1][...] = acc

    def slot_spec(k):
        return pl.BlockSpec((None, rows, cols), lambda i, me: (jnp.where(me[0] == k, (k + 1) % n, k), i, 0))

    return pl.pallas_call(
        body, name=name,
        grid_spec=pltpu.PrefetchScalarGridSpec(
            num_scalar_prefetch=1, grid=(r // rows,),
            in_specs=[slot_spec(k) for k in range(n)] + [pl.BlockSpec((None, rows, cols), lambda i, me: (me[0], i, 0))],
            out_specs=pl.BlockSpec((rows, cols), lambda i, me: (i, 0))),
        out_shape=jax.ShapeDtypeStruct((r, cols), F32),
        compiler_params=_params(("parallel",)),
    )(idx, *([slots] * n), own)


def _sum_slots(slots, *, rows, name):
    n, r, cols = slots.shape

    def body(s_ref, o_ref):
        acc = s_ref[0].astype(F32)
        for k in range(1, n):
            acc = acc + s_ref[k].astype(F32)
        o_ref[...] = acc

    return pl.pallas_call(
        body, name=name, grid=(r // rows,),
        in_specs=[pl.BlockSpec((n, rows, cols), lambda i: (0, i, 0))],
        out_specs=pl.BlockSpec((rows, cols), lambda i: (i, 0)),
        out_shape=jax.ShapeDtypeStruct((r, cols), F32),
        compiler_params=_params(("parallel",)),
    )(slots)


def _pack_small(d, names, rows):
    flat = jnp.concatenate([d[n].astype(F32).reshape(-1) for n in names])
    return jnp.pad(flat, (0, rows * 128 - flat.shape[0])).reshape(rows, 128)


def _unpack_small(p, names):
    flat = p.reshape(-1)
    out, off = {}, 0
    for n in names:
        size = math.prod(SMALL_SHAPES[n])
        out[n] = flat[off:off + size].reshape(SMALL_SHAPES[n])
        off += size
    return out


def _adamw_3d(w, g, m, v, *, name):
    def body(w_ref, g_ref, m_ref, v_ref, d_ref, nm_ref, nv_ref):
        _adamw_math(w_ref, g_ref, m_ref, v_ref, d_ref, nm_ref, nv_ref)

    blk = pl.BlockSpec((8,) + w.shape[1:], lambda i: (i, 0, 0))
    shp = jax.ShapeDtypeStruct(w.shape, F32)
    return pl.pallas_call(body, name=name, grid=(w.shape[0] // 8,), in_specs=[blk] * 4, out_specs=[blk] * 3,
                          out_shape=[shp] * 3, compiler_params=_params(("parallel",)))(w, g, m, v)


def kernel(x, positions, norm_w, w_in, q_norm_w, k_norm_w, sinks, a_re, a_im, log_step, b_re, b_im, c_re, c_im, d_skip, w_glu, b_glu, attn_out_norm_w, ssm_out_norm_w, w_out, loss_target, m_norm_w, m_w_in, m_q_norm_w, m_k_norm_w, m_sinks, m_a_re, m_a_im, m_log_step, m_b_re, m_b_im, m_c_re, m_c_im, m_d_skip, m_w_glu, m_b_glu, m_attn_out_norm_w, m_ssm_out_norm_w, m_w_out, v_norm_w, v_w_in, v_q_norm_w, v_k_norm_w, v_sinks, v_a_re, v_a_im, v_log_step, v_b_re, v_b_im, v_c_re, v_c_im, v_d_skip, v_w_glu, v_b_glu, v_attn_out_norm_w, v_ssm_out_norm_w, v_w_out):
    small_w = dict(norm_w=norm_w, q_norm_w=q_norm_w, k_norm_w=k_norm_w, sinks=sinks, a_re=a_re, a_im=a_im,
                   log_step=log_step, b_re=b_re, b_im=b_im, c_re=c_re, c_im=c_im, d_skip=d_skip, b_glu=b_glu,
                   attn_out_norm_w=attn_out_norm_w, ssm_out_norm_w=ssm_out_norm_w)
    small_m = dict(norm_w=m_norm_w, q_norm_w=m_q_norm_w, k_norm_w=m_k_norm_w, sinks=m_sinks, a_re=m_a_re, a_im=m_a_im,
                   log_step=m_log_step, b_re=m_b_re, b_im=m_b_im, c_re=m_c_re, c_im=m_c_im, d_skip=m_d_skip,
                   b_glu=m_b_glu, attn_out_norm_w=m_attn_out_norm_w, ssm_out_norm_w=m_ssm_out_norm_w)
    small_v = dict(norm_w=v_norm_w, q_norm_w=v_q_norm_w, k_norm_w=v_k_norm_w, sinks=v_sinks, a_re=v_a_re, a_im=v_a_im,
                   log_step=v_log_step, b_re=v_b_re, b_im=v_b_im, c_re=v_c_re, c_im=v_c_im, d_skip=v_d_skip,
                   b_glu=v_b_glu, attn_out_norm_w=v_attn_out_norm_w, ssm_out_norm_w=v_ssm_out_norm_w)
    c_idx = lax.axis_index("c").astype(jnp.int32).reshape(1)
    chip_idx = (2 * lax.axis_index("x") + lax.axis_index("y")).astype(jnp.int32).reshape(1)
    dev_idx = 2 * chip_idx + c_idx

    xs = x[0]
    tgt = loss_target[0]
    t = xs.shape[0]
    posf = positions[0].astype(F32).reshape(t, 1)

    mx, my = lax.axis_index("x"), lax.axis_index("y")
    slab_order = jnp.stack([2 * mx + my, 2 * (1 - mx) + my, 2 * mx + (1 - my), 2 * (1 - mx) + (1 - my)]).astype(jnp.int32)
    proj, hn, w_in_all = _inproj(xs, norm_w, w_in.astype(BF16), slab_order)
    inv_freq = ROPE_THETA ** (-jnp.arange(0, HEAD_DIM, 2, dtype=F32) / HEAD_DIM)
    rope = _rope_table(posf, jnp.tile(inv_freq, 4).reshape(1, 128))
    qw = jnp.tile(q_norm_w, 2).reshape(1, 128)
    kw = jnp.tile(k_norm_w, 2).reshape(1, 128)
    sink_row = sinks.reshape(1, N_HEADS)
    oa, w_glu_all, w_out_all = _attn_fwd(proj, rope, qw, kw, sink_row, [w_glu.astype(BF16), w_out.astype(BF16)])
    w_glu_b = w_glu_all.reshape(SSM_W, SSM_W)
    w_out_b = w_out_all.reshape(D_MODEL, D_MODEL)

    lam_r, lam_i, pw_r, pw_i, bb_r, bb_i = _ssm_prep(a_re, a_im, log_step, b_re, b_im, t // N_SEG)
    rows8 = lambda a: jnp.broadcast_to(a.reshape(SSM_GB, 1, SSM_ST), (SSM_GB, N_SEG, SSM_ST))
    lam_r8, lam_i8, pw_r8, pw_i8 = rows8(lam_r), rows8(lam_i), rows8(pw_r), rows8(pw_i)
    ssm_w_in = jnp.concatenate([_block_diag_in(bb_r), _block_diag_in(bb_i)], axis=1).astype(BF16)
    ssm_w_out = jnp.concatenate([_block_diag_out(c_re), _block_diag_out(-c_im)], axis=2).astype(BF16)
    d_row = d_skip.reshape(1, SSM_W)
    y, hc = _ssm_fwd(proj, lam_r8, lam_i8, pw_r8, pw_i8, ssm_w_in, ssm_w_out, d_row)
    b_glu_row = b_glu.reshape(1, SSM_W)
    os_, yg = _glu_fwd(y, proj, w_glu_b, b_glu_row)
    aw = attn_out_norm_w.reshape(1, ATTN_W)
    sw = ssm_out_norm_w.reshape(1, SSM_W)
    merged, dout, sq_err = _outproj(oa, os_, aw, sw, w_out_b, xs, tgt)
    loss = lax.psum(0.5 * sq_err[0, 0] / D_MODEL, MESH_AXES)

    doa, dos, g_aw, g_sw = _outproj_bwd(dout, oa, os_, aw, sw, w_out_b)
    dout_b = dout.astype(BF16)
    (g_w_out_b,) = _matmul_tn(merged, dout_b, tm=512, tn=1024, name="grad_w_out", dtypes=(BF16,))
    dy, dzs, da, g_b_glu = _glu_bwd(y, proj, dos, w_glu_b, b_glu_row)
    (g_w_glu_b,) = _matmul_tn(yg, da, tm=512, tn=1024, name="grad_w_glu", dtypes=(BF16,))
    du, g_wi, g_wo, g_lam, g_d = _ssm_bwd(proj, dy, hc, lam_r8, lam_i8, pw_r8, pw_i8, ssm_w_in, ssm_w_out, d_row)
    early = [g_w_glu_b.reshape(8, 128, SSM_W), g_w_out_b.reshape(8, 256, D_MODEL)]
    dproj, g_qw, g_kw, g_sink, *early_slots = _attn_bwd(proj, rope, qw, kw, sink_row, doa, du, dzs, early)
    g_w_in, g_w_in_b = _matmul_tn(hn, dproj, tm=512, tn=SHARD_W, name="grad_w_in", slabs=True)
    in_shape = (4, 2, D_MODEL // 2, SHARD_W)
    (from_sib,) = _pair_swap([g_w_in_b.reshape(in_shape)])
    pair_in = _add_halves(g_w_in.reshape(in_shape), from_sib, c_idx, rows=128, name="pair_sum")
    grad_x, g_nw, in_slots = _inproj_bwd(dproj, w_in_all, xs, norm_w, dout, [pair_in])

    g_wi = g_wi.reshape(SSM_G, SSM_H, 2 * SSM_P)
    g_wo = g_wo.reshape(SSM_G, SSM_H, 2 * SSM_P)
    g_bb_r = g_wi[:, :, :SSM_P].transpose(0, 2, 1).reshape(SSM_G, SSM_P * SSM_H)
    g_bb_i = g_wi[:, :, SSM_P:].transpose(0, 2, 1).reshape(SSM_G, SSM_P * SSM_H)
    g_a_re, g_a_im, g_ls, g_b_re, g_b_im = _ssm_param_grads(
        a_re, a_im, log_step, b_re, b_im, g_lam[:, 0, :SSM_ST].reshape(SSM_G, SSM_P),
        g_lam[:, 0, SSM_ST:].reshape(SSM_G, SSM_P), g_bb_r, g_bb_i)
    small_g = dict(
        norm_w=g_nw, q_norm_w=g_qw[0, :64] + g_qw[0, 64:], k_norm_w=g_kw[0, :64] + g_kw[0, 64:],
        sinks=g_sink[0, :N_HEADS], a_re=g_a_re, a_im=g_a_im, log_step=g_ls, b_re=g_b_re, b_im=g_b_im,
        c_re=g_wo[:, :, :SSM_P], c_im=-g_wo[:, :, SSM_P:], d_skip=g_d,
        b_glu=g_b_glu, attn_out_norm_w=g_aw, ssm_out_norm_w=g_sw)

    mine = [_sum_peers(in_slots, pair_in, chip_idx, rows=128, name="sum_w_in"),
            _sum_peers(early_slots[0], early[0], dev_idx, rows=128, name="sum_w_glu"),
            _sum_peers(early_slots[1], early[1], dev_idx, rows=128, name="sum_w_out")]
    theirs = _half_swap(mine)
    packed = _pack_small(small_g, SMALL, 8 * PACK_ROWS).reshape(8, PACK_ROWS, 128)
    summed = _sum_slots(_exchange_slices(packed, True, "small_scatter"), rows=PACK_ROWS, name="small_sum")
    small_red = _exchange_slices(summed, False, "small_gather").reshape(8 * PACK_ROWS, 128)

    big = [_adamw_halves(w_in, mine[0], theirs[0], m_w_in, v_w_in, c_idx, rows=256, name="adamw_w_in"),
           _adamw_halves(w_glu, mine[1], theirs[1], m_w_glu, v_w_glu, c_idx, rows=128, name="adamw_w_glu"),
           _adamw_halves(w_out, mine[2], theirs[2], m_w_out, v_w_out, c_idx, rows=256, name="adamw_w_out")]
    g_in_sh, g_glu_sh, g_out_sh = (b[0] for b in big)
    upd = [b[1:] for b in big]
    grads = _unpack_small(small_red, SMALL)
    flat_first = sum(math.prod(SMALL_SHAPES[n]) for n in SMALL_3D) // 128
    sd, sm, sv = _adamw(_pack_small(small_w, SMALL_FLAT, FLAT_ROWS), small_red[flat_first:flat_first + FLAT_ROWS],
                        _pack_small(small_m, SMALL_FLAT, FLAT_ROWS), _pack_small(small_v, SMALL_FLAT, FLAT_ROWS),
                        rows=FLAT_ROWS, name="adamw_small")
    deltas, new_m, new_v = (_unpack_small(a, SMALL_FLAT) for a in (sd, sm, sv))
    for n in SMALL_3D:
        deltas[n], new_m[n], new_v[n] = _adamw_3d(small_w[n], grads[n], small_m[n], small_v[n], name="adamw_" + n)
    grads.update(w_in=g_in_sh, w_glu=g_glu_sh, w_out=g_out_sh)
    for n, (d, m_, v_) in zip(("w_in", "w_glu", "w_out"), upd):
        deltas[n], new_m[n], new_v[n] = d, m_, v_
    order = ["norm_w", "w_in", "q_norm_w", "k_norm_w", "sinks", "a_re", "a_im", "log_step", "b_re", "b_im", "c_re",
             "c_im", "d_skip", "w_glu", "b_glu", "attn_out_norm_w", "ssm_out_norm_w", "w_out"]
    return (loss, grad_x[None], *[grads[n] for n in order], *[deltas[n] for n in order],
            *[new_m[n] for n in order], *[new_v[n] for n in order])
```

```python
import math

import jax
import jax.numpy as jnp
from jax import lax
from jax.experimental import pallas as pl
from jax.experimental.pallas import tpu as pltpu

F32 = jnp.float32
BF16 = jnp.bfloat16

D_MODEL = 2048
ATTN_W = 1024
SSM_W = 1024
HEAD_DIM = 64
N_HEADS = 16
N_KV_HEADS = 4
KV_W = 256
BLOCK = 128
IN_W = 4608
SHARD_W = IN_W // 4
ROPE_THETA = 10000.0
SSM_H = 16
SSM_G = 64
SSM_P = 64
NORM_EPS = 1e-6
ADAM_LR = 0.001
ADAM_B1 = 0.9
ADAM_B2 = 0.999
ADAM_EPS = 1e-08
ADAM_WD = 0.01
ADAM_STEP = 10

N_SEG = 8
SSM_GB = 4
SSM_CH = 256
SSM_ST = 1024
SCAN_ROWS = 256
SCAN_LW = 512
VMEM_LIMIT = 56 * 1024 * 1024
MESH_AXES = ("x", "y", "c")
ANY = pl.BlockSpec(memory_space=pl.ANY)

SMALL_3D = ("b_re", "b_im", "c_re", "c_im")
SMALL_FLAT = ("norm_w", "q_norm_w", "k_norm_w", "sinks", "a_re", "a_im", "log_step", "d_skip", "b_glu",
              "attn_out_norm_w", "ssm_out_norm_w")
SMALL = SMALL_3D + SMALL_FLAT
SMALL_SHAPES = {"norm_w": (2048,), "q_norm_w": (64,), "k_norm_w": (64,), "sinks": (16,), "a_re": (64, 64),
                "a_im": (64, 64), "log_step": (64,), "b_re": (64, 64, 16), "b_im": (64, 64, 16),
                "c_re": (64, 16, 64), "c_im": (64, 16, 64), "d_skip": (1024,), "b_glu": (1024,),
                "attn_out_norm_w": (1024,), "ssm_out_norm_w": (1024,)}
PACK_ROWS = 272
FLAT_ROWS = 120


def _params(sem=None):
    return pltpu.CompilerParams(dimension_semantics=sem, vmem_limit_bytes=VMEM_LIMIT)


def _dot(a, b):
    return jnp.dot(a, b, preferred_element_type=F32)


def _dot_nt(a, b):
    return lax.dot_general(a, b, (((1,), (1,)), ((), ())), preferred_element_type=F32)


def _dot_tn(a, b):
    return lax.dot_general(a, b, (((0,), (0,)), ((), ())), preferred_element_type=F32)


def _sigmoid(x):
    return 1.0 / (1.0 + jnp.exp(-x))


def _silu(x):
    return x * _sigmoid(x)


def _dsilu(x):
    s = _sigmoid(x)
    return s * (1.0 + x * (1.0 - s))


_GELU_C = math.sqrt(2.0 / math.pi)


def _gelu(x):
    return 0.5 * x * (1.0 + jnp.tanh(_GELU_C * (x + 0.044715 * x * x * x)))


def _dgelu(x):
    t = jnp.tanh(_GELU_C * (x + 0.044715 * x * x * x))
    return 0.5 * (1.0 + t) + 0.5 * x * (1.0 - t * t) * _GELU_C * (1.0 + 3.0 * 0.044715 * x * x)


def _matmul_tn(a, b, *, tm, tn, name, slabs=False, dtypes=(F32, BF16)):
    k, m = a.shape
    _, n = b.shape

    def body(a_ref, b_ref, *o_refs):
        acc = _dot_tn(a_ref[...], b_ref[...])
        for o_ref in o_refs:
            o_ref[...] = acc.astype(o_ref.dtype)

    if slabs:
        out_spec = pl.BlockSpec((None, tm, tn), lambda j, i: (j, i, 0))
        shape = (n // tn, m, tn)
    else:
        out_spec = pl.BlockSpec((tm, tn), lambda j, i: (i, j))
        shape = (m, n)
    return pl.pallas_call(
        body, name=name, grid=(n // tn, m // tm),
        in_specs=[pl.BlockSpec((k, tm), lambda j, i: (0, i)), pl.BlockSpec((k, tn), lambda j, i: (0, j))],
        out_specs=[out_spec] * len(dtypes),
        out_shape=[jax.ShapeDtypeStruct(shape, d) for d in dtypes],
        compiler_params=_params(("parallel", "parallel")),
    )(a, b)


def _inproj(x, norm_w, w_sh, order):
    t = x.shape[0]
    tm = 512
    ni = t // tm
    hr = D_MODEL // 2

    def body(ord_ref, x_ref, nw_ref, sh_ref, proj_ref, hn_ref, full_ref, wbuf, hn_s, ssem, rsem, lsem):
        s, i = pl.program_id(0), pl.program_id(1)
        mx, my, c = _mesh_pos()
        me = 2 * mx + my
        sib = (mx, my, 1 - c)
        chips = _other_chips(mx, my)

        def half(which):
            return pl.ds(pl.multiple_of(which * hr, 8), hr)

        def slot(k):
            return 2 * chips[k][0] + chips[k][1]

        def ici(k):
            return _remote(sh_ref.at[half(c)], full_ref.at[me, half(c)], ssem.at[k], rsem.at[k], (*chips[k], c))

        def own():
            return _remote(sh_ref, full_ref.at[me], ssem.at[6], rsem.at[6], sib)

        def landed(k, which, sem):
            ref = full_ref.at[slot(k), half(which)]
            return _remote(ref, ref, ssem.at[sem], rsem.at[sem], sib)

        def fetch(src, b):
            return pltpu.make_async_copy(src, wbuf.at[b], lsem.at[b])

        @pl.when((s == 0) & (i == 0))
        def _():
            for k in range(3):
                ici(k).start()
            own().start()
            cp = fetch(sh_ref, 0)
            cp.start()
            cp.wait()

        for k in range(3):
            @pl.when((s == k) & (i == max(ni - 2, 0)))
            def _(k=k):
                landed(k, c, k).wait_recv()
                landed(k, c, 3 + k).start()
                landed(k, 1 - c, 3 + k).wait_recv()
                fetch(full_ref.at[slot(k)], (k + 1) % 2).start()

            @pl.when((s == k + 1) & (i == 0))
            def _(k=k):
                fetch(full_ref.at[slot(k)], (k + 1) % 2).wait()

        xv = x_ref[...]
        r = lax.rsqrt(jnp.mean(xv * xv, axis=1, keepdims=True) + NORM_EPS)
        hn = (xv * r * nw_ref[...]).astype(BF16)
        proj_ref[...] = _dot(hn, wbuf[s % 2])

        def hn_out(tile):
            return pltpu.make_async_copy(hn_s, hn_ref.at[pl.ds(pl.multiple_of(tile * tm, tm), tm), :], lsem.at[2])

        @pl.when(((s == 0) & (i > 0)) | ((s == 1) & (i == 0)))
        def _():
            hn_out(jnp.where(s == 0, i - 1, ni - 1)).wait()

        @pl.when(s == 0)
        def _():
            hn_s[...] = hn
            hn_out(i).start()

        @pl.when((s == 3) & (i == ni - 1))
        def _():
            mine = full_ref.at[me]
            _remote(mine, mine, ssem.at[6], rsem.at[6], sib).wait_recv()
            for k in range(3):
                ici(k).wait_send()
                landed(k, c, 3 + k).wait_send()
            own().wait_send()

    return pl.pallas_call(
        body, name="inproj",
        grid_spec=pltpu.PrefetchScalarGridSpec(
            num_scalar_prefetch=1, grid=(4, ni),
            in_specs=[pl.BlockSpec((tm, D_MODEL), lambda s, i, o: (i, 0)),
                      pl.BlockSpec((1, D_MODEL), lambda s, i, o: (0, 0)), ANY],
            out_specs=[pl.BlockSpec((tm, SHARD_W), lambda s, i, o: (i, o[s])), ANY, ANY],
            scratch_shapes=[pltpu.VMEM((2, D_MODEL, SHARD_W), BF16), pltpu.VMEM((tm, D_MODEL), BF16),
                            pltpu.SemaphoreType.DMA((7,)), pltpu.SemaphoreType.DMA((7,)),
                            pltpu.SemaphoreType.DMA((3,))]),
        out_shape=[jax.ShapeDtypeStruct((t, IN_W), F32), jax.ShapeDtypeStruct((t, D_MODEL), BF16),
                   jax.ShapeDtypeStruct((4, D_MODEL, SHARD_W), BF16)],
        compiler_params=_params(("arbitrary", "arbitrary")),
    )(order, x, norm_w.reshape(1, D_MODEL), w_sh)


def _lane128():
    return lax.broadcasted_iota(jnp.int32, (1, 128), 1)


def _head_sums(v):
    lo = _lane128() < 64
    s_lo = jnp.sum(jnp.where(lo, v, 0.0), axis=1, keepdims=True)
    s_hi = jnp.sum(jnp.where(lo, 0.0, v), axis=1, keepdims=True)
    return jnp.where(lo, s_lo, s_hi)


def _rot_half(t):
    first = (_lane128() % 64) < 32
    return jnp.where(first, -pltpu.roll(t, 96, 1), pltpu.roll(t, 32, 1))


def _head_rstd(t):
    return lax.rsqrt(_head_sums(t * t) * (1.0 / HEAD_DIM) + NORM_EPS)


def _prep_tile(t, w, cos, sin, r=None):
    r = _head_rstd(t) if r is None else r
    tn = t * r * w
    return tn * cos + _rot_half(tn) * sin


def _prep_tile_bwd(t, w, cos, sin, g, r=None):
    r = _head_rstd(t) if r is None else r
    d_tn = g * cos - _rot_half(g * sin)
    th = t * r
    dw = jnp.sum(d_tn * th, axis=0, keepdims=True)
    gh = d_tn * w
    m = _head_sums(gh * th) * (1.0 / HEAD_DIM)
    return r * (gh - th * m), dw


def _band_mask(n):
    qi = lax.broadcasted_iota(jnp.int32, (BLOCK, 2 * BLOCK), 0) + BLOCK
    ki = lax.broadcasted_iota(jnp.int32, (BLOCK, 2 * BLOCK), 1)
    rel = qi - ki
    return (rel >= 0) & (rel < BLOCK) & ((n > 0) | (ki >= BLOCK))


def _half_select(tile, half):
    lo = _lane128() < 64
    return jnp.where(lo if half == 0 else jnp.logical_not(lo), tile, 0.0)


def _stack_group(tiles, kv_half):
    rows = []
    for t in tiles:
        for half in range(2):
            piece = _half_select(t, half)
            rows.append(piece if half == kv_half else pltpu.roll(piece, 64, 1))
    return jnp.concatenate(rows, axis=0)


def _unstack_group(stacked, kv_half):
    tiles = []
    for i in range(2):
        acc = None
        for half in range(2):
            piece = _half_select(stacked[BLOCK * (2 * i + half):BLOCK * (2 * i + half + 1)], kv_half)
            piece = piece if half == kv_half else pltpu.roll(piece, 64, 1)
            acc = piece if acc is None else acc + piece
        tiles.append(acc)
    return tiles


def _attn_specs(nb):
    last = nb - 1
    qi = lambda n: (jnp.minimum(n, last), 0)
    prev = lambda n: jnp.maximum(n - 1, 0)
    cur = lambda n: jnp.minimum(n, last)
    specs = [
        pl.BlockSpec((BLOCK, ATTN_W), qi),
        pl.BlockSpec((BLOCK, KV_W), lambda n: (cur(n), 4)),
        pl.BlockSpec((BLOCK, KV_W), lambda n: (prev(n), 4)),
        pl.BlockSpec((BLOCK, KV_W), lambda n: (cur(n), 5)),
        pl.BlockSpec((BLOCK, KV_W), lambda n: (prev(n), 5)),
        pl.BlockSpec((BLOCK, 512), lambda n: (cur(n), 3)),
        pl.BlockSpec((BLOCK, 512), lambda n: (cur(n), 4)),
        pl.BlockSpec((BLOCK, 256), lambda n: (cur(n), 0)),
        pl.BlockSpec((BLOCK, 256), lambda n: (prev(n), 0)),
        pl.BlockSpec((1, 128), lambda n: (0, 0)),
        pl.BlockSpec((1, 128), lambda n: (0, 0)),
        pl.BlockSpec((1, N_HEADS), lambda n: (0, 0)),
    ]
    return specs


def _rope_table(posf, invf):
    t = posf.shape[0]

    def body(p_ref, f_ref, o_ref):
        ang = p_ref[...] * f_ref[...]
        o_ref[...] = jnp.concatenate([jnp.cos(ang), jnp.sin(ang)], axis=1)

    return pl.pallas_call(
        body, name="rope_table", grid=(t // 512,),
        in_specs=[pl.BlockSpec((512, 1), lambda i: (i, 0)), pl.BlockSpec((1, 128), lambda i: (0, 0))],
        out_specs=pl.BlockSpec((512, 256), lambda i: (i, 0)),
        out_shape=jax.ShapeDtypeStruct((t, 256), F32), compiler_params=_params(("parallel",)),
    )(posf, invf)


def _attn_common(n, q_ref, kc_ref, kp_ref, vc_ref, vp_ref, rq_ref, rp_ref, qw_ref, kw_ref):
    cos_q, sin_q = rq_ref[:, 0:128], rq_ref[:, 128:256]
    cos_k = jnp.concatenate([rp_ref[:, 0:128], cos_q], axis=0)
    sin_k = jnp.concatenate([rp_ref[:, 128:256], sin_q], axis=0)
    k_raw = jnp.concatenate([kp_ref[...], kc_ref[...]], axis=0)
    vv = jnp.concatenate([vp_ref[...], vc_ref[...]], axis=0).astype(BF16)
    kk = [_prep_tile(k_raw[:, 128 * i:128 * i + 128], kw_ref[...], cos_k, sin_k).astype(BF16) for i in range(2)]
    vt = [vv[:, 128 * i:128 * i + 128] for i in range(2)]
    qv = q_ref[...]
    qr = [_head_rstd(qv[:, 128 * i:128 * i + 128]) for i in range(8)]
    qt = [_prep_tile(qv[:, 128 * i:128 * i + 128], qw_ref[...], cos_q, sin_q, qr[i]) for i in range(8)]
    return cos_q, sin_q, qr, kk, vt, qt


QK_SCALE = 1.0 / math.sqrt(HEAD_DIM)


def _group_sinks(sink_ref, g):
    return jnp.concatenate([jnp.broadcast_to(sink_ref[:, 4 * g + j:4 * g + j + 1], (BLOCK, 1)) for j in range(4)], axis=0)


def _group_softmax(q4, kk_t, sink, bias):
    s = _dot_nt(q4, kk_t) + bias
    m = jnp.maximum(jnp.max(s, axis=1, keepdims=True), sink)
    p = jnp.exp(s - m)
    es = jnp.exp(sink - m)
    inv = 1.0 / (jnp.sum(p, axis=1, keepdims=True) + es)
    return p * inv, es * inv


def _group_bias(n):
    return jnp.concatenate([jnp.where(_band_mask(n), 0.0, -1e30)] * 4, axis=0)


def _attn_fwd(proj, rope, qw, kw, sinks, later_shards):
    t = proj.shape[0]
    nb = t // BLOCK
    nt = len(later_shards)

    def body(q_ref, kc_ref, kp_ref, vc_ref, vp_ref, za0_ref, za1_ref, rq_ref, rp_ref, qw_ref, kw_ref,
             sink_ref, *rest):
        sh, o_ref, full = rest[:nt], rest[nt], rest[nt + 1:2 * nt + 1]
        ssem, rsem = rest[2 * nt + 1:]
        n = pl.program_id(0)
        start, wait = _bg_gather(sh, full, ssem, rsem)

        @pl.when(n == 0)
        def _():
            start()

        _, _, _, kk, vt, qt = _attn_common(n, q_ref, kc_ref, kp_ref, vc_ref, vp_ref, rq_ref, rp_ref, qw_ref, kw_ref)
        bias = _group_bias(n)
        tiles = []
        for g in range(N_KV_HEADS):
            q4 = (_stack_group(qt[2 * g:2 * g + 2], g % 2) * QK_SCALE).astype(BF16)
            p, _ = _group_softmax(q4, kk[g // 2], _group_sinks(sink_ref, g), bias)
            tiles += _unstack_group(_dot(p.astype(BF16), vt[g // 2]), g % 2)
        za = jnp.concatenate([za0_ref[...], za1_ref[...]], axis=1)
        o_ref[...] = jnp.concatenate(tiles, axis=1) * _silu(za)

        @pl.when(n == nb - 1)
        def _():
            wait()

    return pl.pallas_call(
        body, name="attn_fwd", grid=(nb,), in_specs=_attn_specs(nb) + [ANY] * nt,
        out_specs=[pl.BlockSpec((BLOCK, ATTN_W), lambda n: (n, 0))] + [ANY] * nt,
        out_shape=[jax.ShapeDtypeStruct((t, ATTN_W), F32)]
        + [jax.ShapeDtypeStruct((4,) + s.shape, s.dtype) for s in later_shards],
        scratch_shapes=[pltpu.SemaphoreType.DMA((4 * nt,)), pltpu.SemaphoreType.DMA((4 * nt,))],
        compiler_params=_params(("arbitrary",)),
    )(proj, proj, proj, proj, proj, proj, proj, rope, rope, qw, kw, sinks, *later_shards)


def _attn_bwd(proj, rope, qw, kw, sinks, doa, du, dzs, outgoing):
    t = proj.shape[0]
    nb = t // BLOCK
    last = nb - 1
    nt = len(outgoing)

    def body(q_ref, kc_ref, kp_ref, vc_ref, vp_ref, za0_ref, za1_ref, rq_ref, rp_ref, qw_ref, kw_ref,
             sink_ref, doa_ref, du_ref, dzs_ref, *rest):
        src = rest[:nt]
        dp_ref, gq_ref, gk_ref, gs_ref = rest[nt:nt + 4]
        dst = rest[nt + 4:2 * nt + 4]
        dkk_s, dvv_s, ck_s, cv_s, dq_s, dza_s, ssem, rsem = rest[2 * nt + 4:]
        n = pl.program_id(0)
        start, wait = _bg_scatter_devices(src, dst, ssem, rsem)

        @pl.when(n == 0)
        def _():
            start()
            gq_ref[...] = jnp.zeros_like(gq_ref)
            gk_ref[...] = jnp.zeros_like(gk_ref)
            gs_ref[...] = jnp.zeros_like(gs_ref)
            ck_s[...] = jnp.zeros_like(ck_s)
            cv_s[...] = jnp.zeros_like(cv_s)
            dq_s[...] = jnp.zeros_like(dq_s)
            dza_s[...] = jnp.zeros_like(dza_s)

        dp_ref[:, 0:ATTN_W] = dq_s[...]
        dp_ref[:, ATTN_W + 2 * KV_W:2 * ATTN_W + 2 * KV_W] = dza_s[...]
        dp_ref[:, 2 * ATTN_W + 2 * KV_W:IN_W - SSM_W] = du_ref[...]
        dp_ref[:, IN_W - SSM_W:IN_W] = dzs_ref[...]

        @pl.when(n == nb)
        def _():
            dkk_s[...] = jnp.zeros_like(dkk_s)
            dvv_s[...] = jnp.zeros_like(dvv_s)

        @pl.when(n < nb)
        def _():
            cos_q, sin_q, qr, kk, vt, qt = _attn_common(n, q_ref, kc_ref, kp_ref, vc_ref, vp_ref, rq_ref, rp_ref,
                                                        qw_ref, kw_ref)
            bias = _group_bias(n)
            za = jnp.concatenate([za0_ref[...], za1_ref[...]], axis=1)
            doa_v = doa_ref[...]
            do_full = doa_v * _silu(za)
            o_tiles, dq_tiles = [], []
            dkk = [jnp.zeros((2 * BLOCK, 128), F32) for _ in range(2)]
            dvv = [jnp.zeros((2 * BLOCK, 128), F32) for _ in range(2)]
            gsink = jnp.zeros((1, 128), F32)
            lane = _lane128()
            for g in range(N_KV_HEADS):
                q_b = (_stack_group(qt[2 * g:2 * g + 2], g % 2) * QK_SCALE).astype(BF16)
                do_b = _stack_group([do_full[:, 128 * i:128 * i + 128] for i in (2 * g, 2 * g + 1)], g % 2).astype(BF16)
                p, psink = _group_softmax(q_b, kk[g // 2], _group_sinks(sink_ref, g), bias)
                p_b = p.astype(BF16)
                dp = _dot_nt(do_b, vt[g // 2])
                delta = jnp.sum(p * dp, axis=1, keepdims=True)
                ds_b = (p * (dp - delta)).astype(BF16)
                sd = psink * delta
                for j in range(4):
                    gsink = gsink + jnp.where(lane == 4 * g + j, -jnp.sum(sd[BLOCK * j:BLOCK * (j + 1)]), 0.0)
                o_tiles += _unstack_group(_dot(p_b, vt[g // 2]), g % 2)
                dq_tiles += [d * QK_SCALE for d in _unstack_group(_dot(ds_b, kk[g // 2]), g % 2)]
                dkk[g // 2] = dkk[g // 2] + _dot_tn(ds_b, q_b)
                dvv[g // 2] = dvv[g // 2] + _dot_tn(p_b, do_b)
            dza_s[...] = (doa_v * jnp.concatenate(o_tiles, axis=1) * _dsilu(za)).astype(BF16)
            qv = q_ref[...]
            gq = jnp.zeros((1, 128), F32)
            out = []
            for i in range(8):
                d, dw = _prep_tile_bwd(qv[:, 128 * i:128 * i + 128], qw_ref[...], cos_q, sin_q, dq_tiles[i], qr[i])
                out.append(d)
                gq = gq + dw
            dq_s[...] = jnp.concatenate(out, axis=1).astype(BF16)
            gq_ref[...] += gq
            gs_ref[...] += gsink
            dkk_s[...] = jnp.concatenate(dkk, axis=1)
            dvv_s[...] = jnp.concatenate(dvv, axis=1)

        cos_p, sin_p = rp_ref[:, 0:128], rp_ref[:, 128:256]
        dk_prev = ck_s[...] + dkk_s[0:BLOCK, :]
        kp = kp_ref[...]
        gk = jnp.zeros((1, 128), F32)
        out = []
        for i in range(2):
            d, dw = _prep_tile_bwd(kp[:, 128 * i:128 * i + 128], kw_ref[...], cos_p, sin_p,
                                   dk_prev[:, 128 * i:128 * i + 128])
            out.append(d)
            gk = gk + dw
        dp_ref[:, ATTN_W:ATTN_W + KV_W] = jnp.concatenate(out, axis=1).astype(BF16)
        dp_ref[:, ATTN_W + KV_W:ATTN_W + 2 * KV_W] = (cv_s[...] + dvv_s[0:BLOCK, :]).astype(BF16)
        gk_ref[...] += gk
        ck_s[...] = dkk_s[BLOCK:2 * BLOCK, :]
        cv_s[...] = dvv_s[BLOCK:2 * BLOCK, :]

        @pl.when(n == nb)
        def _():
            wait()

    qblk = lambda n: (jnp.minimum(n, last), 0)
    kblk = lambda n: (jnp.maximum(n - 1, 0), 0)
    vec = pl.BlockSpec((1, 128), lambda n: (0, 0))
    return pl.pallas_call(
        body, name="attn_bwd", grid=(nb + 1,),
        in_specs=_attn_specs(nb) + [pl.BlockSpec((BLOCK, ATTN_W), qblk), pl.BlockSpec((BLOCK, SSM_W), kblk),
                                    pl.BlockSpec((BLOCK, SSM_W), kblk)] + [ANY] * nt,
        out_specs=[pl.BlockSpec((BLOCK, IN_W), kblk), vec, vec, vec] + [ANY] * nt,
        out_shape=[jax.ShapeDtypeStruct((t, IN_W), BF16), jax.ShapeDtypeStruct((1, 128), F32),
                   jax.ShapeDtypeStruct((1, 128), F32), jax.ShapeDtypeStruct((1, 128), F32)]
        + [jax.ShapeDtypeStruct(a.shape, a.dtype) for a in outgoing],
        scratch_shapes=[pltpu.VMEM((2 * BLOCK, KV_W), F32), pltpu.VMEM((2 * BLOCK, KV_W), F32),
                        pltpu.VMEM((BLOCK, KV_W), F32), pltpu.VMEM((BLOCK, KV_W), F32),
                        pltpu.VMEM((BLOCK, ATTN_W), BF16), pltpu.VMEM((BLOCK, ATTN_W), BF16),
                        pltpu.SemaphoreType.DMA((7 * nt,)), pltpu.SemaphoreType.DMA((7 * nt,))],
        compiler_params=_params(("arbitrary",)),
    )(proj, proj, proj, proj, proj, proj, proj, rope, rope, qw, kw, sinks, doa, du, dzs, *outgoing)


def _cmul(ar, ai, br, bi):
    return ar * br - ai * bi, ar * bi + ai * br


def _zoh(a_re, a_im, delta):
    e = jnp.exp(a_re * delta)
    lr, li = e * jnp.cos(a_im * delta), e * jnp.sin(a_im * delta)
    inv = 1.0 / (a_re * a_re + a_im * a_im)
    fr, fi = _cmul(lr - 1.0, li, a_re * inv, -a_im * inv)
    return lr, li, fr, fi


def _ssm_prep(a_re, a_im, log_step, b_re, b_im, seg_len):
    n_sq = int(round(math.log2(seg_len)))
    assert 2 ** n_sq == seg_len

    def body(ar_ref, ai_ref, ls_ref, arx_ref, aix_ref, br_ref, bi_ref, lr_ref, li_ref, pr_ref, pi_ref, bbr_ref, bbi_ref):
        delta = jnp.exp(ls_ref[...])
        lr, li, _, _ = _zoh(ar_ref[...], ai_ref[...], delta)
        lr_ref[...] = lr
        li_ref[...] = li
        pr, pi = lr, li
        for _ in range(n_sq):
            pr, pi = _cmul(pr, pi, pr, pi)
        pr_ref[...] = pr
        pi_ref[...] = pi
        _, _, fr, fi = _zoh(arx_ref[...], aix_ref[...], delta)
        bbr, bbi = _cmul(fr, fi, br_ref[...], bi_ref[...])
        bbr_ref[...] = bbr
        bbi_ref[...] = bbi

    gp = jax.ShapeDtypeStruct((SSM_G, SSM_P), F32)
    gx = jax.ShapeDtypeStruct((SSM_G, SSM_P * SSM_H), F32)
    return pl.pallas_call(body, name="ssm_prep", out_shape=[gp, gp, gp, gp, gx, gx])(
        a_re, a_im, log_step.reshape(SSM_G, 1), jnp.repeat(a_re, SSM_H, axis=1), jnp.repeat(a_im, SSM_H, axis=1),
        b_re.reshape(SSM_G, SSM_P * SSM_H), b_im.reshape(SSM_G, SSM_P * SSM_H))


def _ssm_param_grads(a_re, a_im, log_step, b_re, b_im, dlam_re, dlam_im, dbb_re, dbb_im):
    def body(ar_ref, ai_ref, ls_ref, arx_ref, aix_ref, br_ref, bi_ref, dlr_ref, dli_ref, dbr_ref, dbi_ref,
             gar_ref, gai_ref, gls_ref, gbr_ref, gbi_ref):
        delta = jnp.exp(ls_ref[...])
        ar, ai = ar_ref[...], ai_ref[...]
        lr, li, fr, fi = _zoh(ar, ai, delta)
        _, _, frx, fix = _zoh(arx_ref[...], aix_ref[...], delta)
        dbr, dbi = dbr_ref[...], dbi_ref[...]
        br, bi = br_ref[...], bi_ref[...]
        gbr, gbi = _cmul(frx, -fix, dbr, dbi)
        gbr_ref[...] = gbr
        gbi_ref[...] = gbi
        tr, ti = _cmul(br, -bi, dbr, dbi)
        row = lax.broadcasted_iota(jnp.int32, (SSM_P * SSM_H, SSM_P), 0)
        col = lax.broadcasted_iota(jnp.int32, (SSM_P * SSM_H, SSM_P), 1)
        fold = (row // SSM_H == col).astype(F32)
        dfr = jnp.dot(tr, fold, precision=lax.Precision.HIGHEST, preferred_element_type=F32)
        dfi = jnp.dot(ti, fold, precision=lax.Precision.HIGHEST, preferred_element_type=F32)
        inv = 1.0 / (ar * ar + ai * ai)
        ilr, ili = ar * inv, -ai * inv
        t1r, t1i = _cmul(dfr, dfi, ilr, -ili)
        dlbr, dlbi = dlr_ref[...] + t1r, dli_ref[...] + t1i
        qr, qi = _cmul(fr, fi, ilr, ili)
        t2r, t2i = _cmul(dfr, dfi, qr, -qi)
        glr, gli = -t2r, -t2i
        dzr, dzi = _cmul(dlbr, dlbi, lr, -li)
        gar_ref[...] = glr + dzr * delta
        gai_ref[...] = gli + dzi * delta
        gls_ref[...] = jnp.sum(dzr * ar + dzi * ai, axis=1, keepdims=True) * delta

    gp = jax.ShapeDtypeStruct((SSM_G, SSM_P), F32)
    gx = jax.ShapeDtypeStruct((SSM_G, SSM_P * SSM_H), F32)
    return pl.pallas_call(body, name="ssm_param_grads",
                          out_shape=[gp, gp, jax.ShapeDtypeStruct((SSM_G, 1), F32), gx, gx])(
        a_re, a_im, log_step.reshape(SSM_G, 1), jnp.repeat(a_re, SSM_H, axis=1), jnp.repeat(a_im, SSM_H, axis=1),
        b_re.reshape(SSM_G, SSM_P * SSM_H), b_im.reshape(SSM_G, SSM_P * SSM_H), dlam_re, dlam_im, dbb_re, dbb_im)


def _block_diag_in(bb):
    w = jnp.tile(bb.reshape(SSM_GB, SSM_ST, SSM_H), (1, 1, 16))
    row = lax.broadcasted_iota(jnp.int32, (1, SSM_ST, SSM_CH), 1) // SSM_P
    col = lax.broadcasted_iota(jnp.int32, (1, SSM_ST, SSM_CH), 2) // SSM_H
    return jnp.where(row == col, w, 0.0)


def _block_diag_out(c):
    w = jnp.tile(c.reshape(SSM_GB, SSM_CH, SSM_P), (1, 1, 16))
    row = lax.broadcasted_iota(jnp.int32, (1, SSM_CH, SSM_ST), 1) // SSM_H
    col = lax.broadcasted_iota(jnp.int32, (1, SSM_CH, SSM_ST), 2) // SSM_P
    return jnp.where(row == col, w, 0.0)


SEG_ROWS = SCAN_ROWS // N_SEG


def _chunk_perm():
    out_row = lax.broadcasted_iota(jnp.int32, (SCAN_ROWS, SCAN_ROWS), 0)
    in_row = lax.broadcasted_iota(jnp.int32, (SCAN_ROWS, SCAN_ROWS), 1)
    return (out_row == N_SEG * (in_row % SEG_ROWS) + in_row // SEG_ROWS).astype(BF16)


def _chunk_rows(j, seg_len, s):
    return pl.ds(pl.multiple_of(s * seg_len + j * SEG_ROWS, SEG_ROWS), SEG_ROWS)


def _gather_chunk(ref, j, seg_len):
    return jnp.concatenate([ref[_chunk_rows(j, seg_len, s), :] for s in range(N_SEG)], axis=0)


def _scatter_chunk(ref, j, seg_len, val):
    for s in range(N_SEG):
        ref[_chunk_rows(j, seg_len, s), :] = val[s * SEG_ROWS:(s + 1) * SEG_ROWS]


def _interleave(perm, x_b):
    return _dot(perm, x_b).astype(BF16)


def _state_pieces():
    return [pl.ds(part * SSM_ST + k * SCAN_LW, SCAN_LW) for k in range(SSM_ST // SCAN_LW) for part in range(2)]


def _expand_states(x_b, w_ref, dst_ref):
    for cols in _state_pieces():
        dst_ref[:, cols] = _dot_nt(x_b, w_ref[cols, :])


def _contract_states(src_ref, w_ref):
    acc = None
    for cols in _state_pieces():
        part = _dot_nt(src_ref[:, cols].astype(BF16), w_ref[:, cols])
        acc = part if acc is None else acc + part
    return acc


def _scan_fwd(src_ref, dst_ref, lam_r_ref, lam_i_ref, init_ref, final_ref, steps):
    for k in range(SSM_ST // SCAN_LW):
        re = pl.ds(k * SCAN_LW, SCAN_LW)
        im = pl.ds(SSM_ST + k * SCAN_LW, SCAN_LW)
        lr, li = lam_r_ref[:, re], lam_i_ref[:, re]

        def step(i, carry, re=re, im=im, lr=lr, li=li):
            hr, hi = carry
            rows = pl.ds(pl.multiple_of(i * 8, 8), 8)
            nr = lr * hr - li * hi + src_ref[rows, re]
            ni = lr * hi + li * hr + src_ref[rows, im]
            if dst_ref is not None:
                dst_ref[rows, re] = nr
                dst_ref[rows, im] = ni
            return nr, ni

        hr, hi = lax.fori_loop(0, steps, step, (init_ref[:, re], init_ref[:, im]), unroll=True)
        final_ref[:, re] = hr
        final_ref[:, im] = hi


def _ssm_specs(t):
    col = lambda g: (0, g)
    gb3 = lambda g: (g, 0, 0)
    return dict(
        rows=pl.BlockSpec((t, SSM_CH), col),
        lam=pl.BlockSpec((None, N_SEG, SSM_ST), gb3),
        w_in=pl.BlockSpec((None, 2 * SSM_ST, SSM_CH), gb3),
        w_out=pl.BlockSpec((None, SSM_CH, 2 * SSM_ST), gb3),
        vec=pl.BlockSpec((1, SSM_CH), col),
    )


def _segment_states(x_ref, pw_r_ref, pw_i_ref, out_ref, reverse):
    re, im = pl.ds(0, SSM_ST), pl.ds(SSM_ST, SSM_ST)
    pr, pi = pw_r_ref[0:1, :], pw_i_ref[0:1, :]
    first = N_SEG - 1 if reverse else 0
    out_ref[first:first + 1, :] = jnp.zeros((1, 2 * SSM_ST), F32)
    order = range(N_SEG - 1, 0, -1) if reverse else range(N_SEG - 1)
    for s in order:
        d = s - 1 if reverse else s + 1
        hr, hi = out_ref[s:s + 1, re], out_ref[s:s + 1, im]
        if reverse:
            nr, ni = pr * hr + pi * hi, pr * hi - pi * hr
        else:
            nr, ni = pr * hr - pi * hi, pr * hi + pi * hr
        out_ref[d:d + 1, re] = nr + x_ref[s:s + 1, re]
        out_ref[d:d + 1, im] = ni + x_ref[s:s + 1, im]


def _ssm_fwd(proj, lam_r, lam_i, pw_r, pw_i, w_in, w_out, d_skip):
    t = proj.shape[0]
    seg_len = t // N_SEG
    nch = t // SCAN_ROWS
    steps = SCAN_ROWS // N_SEG
    sp = _ssm_specs(t)

    def body(u_ref, lr_ref, li_ref, pr_ref, pi_ref, wi_ref, wo_ref, d_ref, y_ref, hc_ref, bu_s, car_s, seg_s, ub_s,
             y0_s, y1_s):
        perm = _chunk_perm()
        car_s[...] = jnp.zeros_like(car_s)

        def chunk1(j, c):
            rows = pl.ds(pl.multiple_of(j * SCAN_ROWS, SCAN_ROWS), SCAN_ROWS)
            u_b = _interleave(perm, _gather_chunk(u_ref, j, seg_len).astype(BF16))
            ub_s[rows, :] = u_b
            _expand_states(u_b, wi_ref, bu_s)
            _scan_fwd(bu_s, None, lr_ref, li_ref, car_s, car_s, steps)
            return c

        lax.fori_loop(0, nch, chunk1, 0)
        _segment_states(car_s, pr_ref, pi_ref, seg_s, reverse=False)
        car_s[...] = seg_s[...]

        def chunk2(j, c):
            rows = pl.ds(pl.multiple_of(j * SCAN_ROWS, SCAN_ROWS), SCAN_ROWS)
            _expand_states(ub_s[rows, :], wi_ref, bu_s)
            hc_ref[j] = car_s[...]
            _scan_fwd(bu_s, bu_s, lr_ref, li_ref, car_s, car_s, steps)
            yv = _contract_states(bu_s, wo_ref)
            y0_s[...] = yv[:, 0:128]
            y1_s[...] = yv[:, 128:256]
            for s in range(N_SEG):
                nat = _chunk_rows(j, seg_len, s)
                sub = pl.ds(s, SEG_ROWS, stride=N_SEG)
                y_ref[nat, :] = jnp.concatenate([y0_s[sub, :], y1_s[sub, :]], axis=1) + d_ref[...] * u_ref[nat, :]
            return c

        lax.fori_loop(0, nch, chunk2, 0)

    u_cols = 2560 // SSM_CH
    return pl.pallas_call(
        body, name="ssm_fwd", grid=(SSM_GB,),
        in_specs=[pl.BlockSpec((t, SSM_CH), lambda g: (0, u_cols + g)), sp["lam"], sp["lam"], sp["lam"], sp["lam"],
                  sp["w_in"], sp["w_out"], sp["vec"]],
        out_specs=[sp["rows"], pl.BlockSpec((None, nch, N_SEG, 2 * SSM_ST), lambda g: (g, 0, 0, 0))],
        out_shape=[jax.ShapeDtypeStruct((t, SSM_W), F32), jax.ShapeDtypeStruct((SSM_GB, nch, N_SEG, 2 * SSM_ST), F32)],
        scratch_shapes=[pltpu.VMEM((SCAN_ROWS, 2 * SSM_ST), F32), pltpu.VMEM((N_SEG, 2 * SSM_ST), F32),
                        pltpu.VMEM((N_SEG, 2 * SSM_ST), F32), pltpu.VMEM((t, SSM_CH), BF16),
                        pltpu.VMEM((SCAN_ROWS, 128), F32), pltpu.VMEM((SCAN_ROWS, 128), F32)],
        compiler_params=_params(("parallel",)),
    )(proj, lam_r, lam_i, pw_r, pw_i, w_in, w_out, d_skip)


def _group_blocks(full):
    row_g = lax.broadcasted_iota(jnp.int32, (SSM_CH, SSM_ST), 0) // SSM_H
    col_g = lax.broadcasted_iota(jnp.int32, (SSM_CH, SSM_ST), 1) // SSM_P
    fold = (lax.broadcasted_iota(jnp.int32, (SSM_ST, SSM_P), 0) % SSM_P
            == lax.broadcasted_iota(jnp.int32, (SSM_ST, SSM_P), 1)).astype(F32)
    parts = [jnp.dot(jnp.where(row_g == col_g, full[:, k * SSM_ST:(k + 1) * SSM_ST], 0.0), fold,
                     precision=lax.Precision.HIGHEST, preferred_element_type=F32) for k in range(2)]
    return jnp.concatenate(parts, axis=1)


def _ssm_bwd(proj, dy, hc, lam_r, lam_i, pw_r, pw_i, w_in, w_out, d_skip):
    t = proj.shape[0]
    seg_len = t // N_SEG
    nch = t // SCAN_ROWS
    steps = SCAN_ROWS // N_SEG
    sp = _ssm_specs(t)

    def body(u_ref, dy_ref, hc_ref, lr_ref, li_ref, pr_ref, pi_ref, wi_ref, wo_ref, d_ref,
             du_ref, gbi_ref, gbo_ref, glam_ref, gd_ref, bu_s, h_s, e_s, car_s, seg_s, acc_s, gwi_ref, gwo_ref,
             dyb_s):
        perm = _chunk_perm()

        def chunk_rows(j):
            return pl.ds(pl.multiple_of(j * SCAN_ROWS, SCAN_ROWS), SCAN_ROWS)

        def interleaved(ref, j):
            return _interleave(perm, _gather_chunk(ref, j, seg_len).astype(BF16))

        def load_e(j):
            dy_b = interleaved(dy_ref, j)
            dyb_s[chunk_rows(j), :] = dy_b
            for cols in _state_pieces():
                e_s[:, cols] = _dot(dy_b, wo_ref[:, cols])

        def scan_rev(j, accumulate):
            for k in range(SSM_ST // SCAN_LW):
                re = pl.ds(k * SCAN_LW, SCAN_LW)
                im = pl.ds(SSM_ST + k * SCAN_LW, SCAN_LW)
                lr, li = lr_ref[:, re], li_ref[:, re]

                def step(ii, carry, re=re, im=im, lr=lr, li=li):
                    i = steps - 1 - ii
                    rows = pl.ds(pl.multiple_of(i * 8, 8), 8)
                    if accumulate:
                        gr, gi, ar, ai = carry
                    else:
                        gr, gi = carry
                    nr = lr * gr + li * gi + e_s[rows, re]
                    ni = lr * gi - li * gr + e_s[rows, im]
                    if not accumulate:
                        return nr, ni
                    e_s[rows, re] = nr
                    e_s[rows, im] = ni
                    pr_, pi_ = h_s[rows, re], h_s[rows, im]
                    return nr, ni, ar + nr * pr_ + ni * pi_, ai + ni * pr_ - nr * pi_

                init = (car_s[:, re], car_s[:, im])
                if accumulate:
                    init = init + (acc_s[:, re], acc_s[:, im])
                out = lax.fori_loop(0, steps, step, init, unroll=True)
                car_s[:, re] = out[0]
                car_s[:, im] = out[1]
                if accumulate:
                    acc_s[:, re] = out[2]
                    acc_s[:, im] = out[3]

        car_s[...] = jnp.zeros_like(car_s)

        def pass1(jj, c):
            load_e(nch - 1 - jj)
            scan_rev(nch - 1 - jj, False)
            return c

        lax.fori_loop(0, nch, pass1, 0)
        _segment_states(car_s, pr_ref, pi_ref, seg_s, reverse=True)
        car_s[...] = seg_s[...]
        acc_s[...] = jnp.zeros_like(acc_s)
        gwi_ref[...] = jnp.zeros_like(gwi_ref)
        gwo_ref[...] = jnp.zeros_like(gwo_ref)
        gd_ref[...] = jnp.zeros_like(gd_ref)

        def pass2(jj, c):
            j = nch - 1 - jj
            u_b, dy_b = interleaved(u_ref, j), dyb_s[chunk_rows(j), :]
            _expand_states(u_b, wi_ref, bu_s)
            h_s[0:N_SEG, :] = hc_ref[j]
            seg_s[...] = hc_ref[j]
            h_now = h_s.at[pl.ds(N_SEG, SCAN_ROWS), :]
            _scan_fwd(bu_s, h_now, lr_ref, li_ref, seg_s, seg_s, steps)
            for cols in _state_pieces():
                e_s[:, cols] = _dot(dy_b, wo_ref[:, cols])
            scan_rev(j, True)
            du = d_ref[...] * dy_b.astype(F32)
            for cols in _state_pieces():
                g_b = e_s[:, cols].astype(BF16)
                du = du + _dot(g_b, wi_ref[cols, :])
                gwi_ref[:, cols] += _dot_tn(u_b, g_b)
                gwo_ref[:, cols] += _dot_tn(dy_b, h_now[:, cols].astype(BF16))
            _scatter_chunk(du_ref, j, seg_len, _dot_tn(perm, du.astype(BF16)).astype(du_ref.dtype))
            gd_ref[...] += jnp.sum(_gather_chunk(dy_ref, j, seg_len) * _gather_chunk(u_ref, j, seg_len), axis=0,
                                   keepdims=True)
            return c

        lax.fori_loop(0, nch, pass2, 0)
        glam_ref[...] = jnp.sum(acc_s[...], axis=0, keepdims=True)
        gbi_ref[...] = _group_blocks(gwi_ref[...])
        gbo_ref[...] = _group_blocks(gwo_ref[...])

    mat = pl.BlockSpec((None, SSM_CH, 2 * SSM_P), lambda g: (g, 0, 0))
    u_cols = 2560 // SSM_CH
    return pl.pallas_call(
        body, name="ssm_bwd", grid=(SSM_GB,),
        in_specs=[pl.BlockSpec((t, SSM_CH), lambda g: (0, u_cols + g)), sp["rows"],
                  pl.BlockSpec((None, nch, N_SEG, 2 * SSM_ST), lambda g: (g, 0, 0, 0)),
                  sp["lam"], sp["lam"], sp["lam"], sp["lam"], sp["w_in"], sp["w_out"], sp["vec"]],
        out_specs=[sp["rows"], mat, mat, pl.BlockSpec((None, 1, 2 * SSM_ST), lambda g: (g, 0, 0)), sp["vec"]],
        out_shape=[jax.ShapeDtypeStruct((t, SSM_W), BF16), jax.ShapeDtypeStruct((SSM_GB, SSM_CH, 2 * SSM_P), F32),
                   jax.ShapeDtypeStruct((SSM_GB, SSM_CH, 2 * SSM_P), F32),
                   jax.ShapeDtypeStruct((SSM_GB, 1, 2 * SSM_ST), F32), jax.ShapeDtypeStruct((1, SSM_W), F32)],
        scratch_shapes=[pltpu.VMEM((SCAN_ROWS, 2 * SSM_ST), F32), pltpu.VMEM((SCAN_ROWS + N_SEG, 2 * SSM_ST), F32),
                        pltpu.VMEM((SCAN_ROWS, 2 * SSM_ST), F32), pltpu.VMEM((N_SEG, 2 * SSM_ST), F32),
                        pltpu.VMEM((N_SEG, 2 * SSM_ST), F32), pltpu.VMEM((N_SEG, 2 * SSM_ST), F32),
                        pltpu.VMEM((SSM_CH, 2 * SSM_ST), F32), pltpu.VMEM((SSM_CH, 2 * SSM_ST), F32),
                        pltpu.VMEM((t, SSM_CH), BF16)],
        compiler_params=_params(("parallel",)),
    )(proj, dy, hc, lam_r, lam_i, pw_r, pw_i, w_in, w_out, d_skip)


def _z_ssm_specs(tm):
    return [pl.BlockSpec((tm, 512), lambda i: (i, 7)), pl.BlockSpec((tm, 512), lambda i: (i, 8))]


def _glu_fwd(y, proj, w_glu, b_glu):
    t = y.shape[0]
    tm = 512

    def body(y_ref, z0_ref, z1_ref, w_ref, b_ref, o_ref, yg_ref):
        yg = _gelu(y_ref[...])
        yg_b = yg.astype(BF16)
        a = _dot(yg_b, w_ref[...]) + b_ref[...]
        z = jnp.concatenate([z0_ref[...], z1_ref[...]], axis=1)
        o_ref[...] = yg * _sigmoid(a) * _silu(z)
        yg_ref[...] = yg_b

    row = pl.BlockSpec((tm, SSM_W), lambda i: (i, 0))
    return pl.pallas_call(
        body, name="glu_fwd", grid=(t // tm,),
        in_specs=[row] + _z_ssm_specs(tm) + [pl.BlockSpec((SSM_W, SSM_W), lambda i: (0, 0)),
                                            pl.BlockSpec((1, SSM_W), lambda i: (0, 0))],
        out_specs=[row, row],
        out_shape=[jax.ShapeDtypeStruct((t, SSM_W), F32), jax.ShapeDtypeStruct((t, SSM_W), BF16)],
        compiler_params=_params(("parallel",)),
    )(y, proj, proj, w_glu, b_glu)


def _glu_bwd(y, proj, dos, w_glu, b_glu):
    t = y.shape[0]
    tm = 512

    def body(y_ref, z0_ref, z1_ref, do_ref, w_ref, b_ref, dy_ref, dz_ref, da_ref, gb_ref):
        @pl.when(pl.program_id(0) == 0)
        def _():
            gb_ref[...] = jnp.zeros_like(gb_ref)

        z = jnp.concatenate([z0_ref[...], z1_ref[...]], axis=1)
        yv, do = y_ref[...], do_ref[...]
        yg = _gelu(yv)
        sg = _sigmoid(_dot(yg.astype(BF16), w_ref[...]) + b_ref[...])
        dy2 = do * _silu(z)
        dz_ref[...] = (do * yg * sg * _dsilu(z)).astype(BF16)
        da = dy2 * yg * sg * (1.0 - sg)
        da_b = da.astype(BF16)
        da_ref[...] = da_b
        gb_ref[...] += jnp.sum(da, axis=0, keepdims=True)
        dyg = dy2 * sg + _dot_nt(da_b, w_ref[...])
        dy_ref[...] = dyg * _dgelu(yv)

    row = pl.BlockSpec((tm, SSM_W), lambda i: (i, 0))
    vec = pl.BlockSpec((1, SSM_W), lambda i: (0, 0))
    return pl.pallas_call(
        body, name="glu_bwd", grid=(t // tm,),
        in_specs=[row] + _z_ssm_specs(tm) + [row, pl.BlockSpec((SSM_W, SSM_W), lambda i: (0, 0)), vec],
        out_specs=[row, row, row, vec],
        out_shape=[jax.ShapeDtypeStruct((t, SSM_W), F32), jax.ShapeDtypeStruct((t, SSM_W), BF16),
                   jax.ShapeDtypeStruct((t, SSM_W), BF16), jax.ShapeDtypeStruct((1, SSM_W), F32)],
        compiler_params=_params(("arbitrary",)),
    )(y, proj, proj, dos, w_glu, b_glu)


def _rms(o):
    return lax.rsqrt(jnp.mean(o * o, axis=1, keepdims=True) + NORM_EPS)


def _outproj(oa, os_, aw, sw, w_out, x, target):
    t = x.shape[0]
    tm = 256

    def body(oa_ref, os_ref, aw_ref, sw_ref, w_ref, x_ref, t_ref, mg_ref, do_ref, ls_ref):
        @pl.when(pl.program_id(0) == 0)
        def _():
            ls_ref[...] = jnp.zeros_like(ls_ref)

        a, s = oa_ref[...], os_ref[...]
        merged = jnp.concatenate([a * _rms(a) * aw_ref[...], s * _rms(s) * sw_ref[...]], axis=1).astype(BF16)
        mg_ref[...] = merged
        err = x_ref[...] + _dot(merged, w_ref[...]) - t_ref[...]
        do_ref[...] = err * (1.0 / D_MODEL)
        ls_ref[...] += jnp.sum(err * err)

    half = pl.BlockSpec((tm, ATTN_W), lambda i: (i, 0))
    full = pl.BlockSpec((tm, D_MODEL), lambda i: (i, 0))
    vec = pl.BlockSpec((1, ATTN_W), lambda i: (0, 0))
    return pl.pallas_call(
        body, name="outproj", grid=(t // tm,),
        in_specs=[half, half, vec, vec, pl.BlockSpec((D_MODEL, D_MODEL), lambda i: (0, 0)), full, full],
        out_specs=[full, full, pl.BlockSpec((8, 128), lambda i: (0, 0))],
        out_shape=[jax.ShapeDtypeStruct((t, D_MODEL), BF16), jax.ShapeDtypeStruct((t, D_MODEL), F32),
                   jax.ShapeDtypeStruct((8, 128), F32)],
        compiler_params=_params(("arbitrary",)),
    )(oa, os_, aw, sw, w_out, x, target)


def _outproj_bwd(dout, oa, os_, aw, sw, w_out):
    t = dout.shape[0]
    tm = 256

    def norm_bwd(o, w, dm):
        r = _rms(o)
        yh = o * r
        gh = dm * w
        return r * (gh - yh * jnp.mean(gh * yh, axis=1, keepdims=True)), jnp.sum(dm * yh, axis=0, keepdims=True)

    def body(do_ref, oa_ref, os_ref, aw_ref, sw_ref, w_ref, da_ref, ds_ref, ga_ref, gs_ref):
        @pl.when(pl.program_id(0) == 0)
        def _():
            ga_ref[...] = jnp.zeros_like(ga_ref)
            gs_ref[...] = jnp.zeros_like(gs_ref)

        dm = _dot_nt(do_ref[...].astype(BF16), w_ref[...])
        da, ga = norm_bwd(oa_ref[...], aw_ref[...], dm[:, :ATTN_W])
        ds, gs = norm_bwd(os_ref[...], sw_ref[...], dm[:, ATTN_W:])
        da_ref[...] = da
        ds_ref[...] = ds
        ga_ref[...] += ga
        gs_ref[...] += gs

    half = pl.BlockSpec((tm, ATTN_W), lambda i: (i, 0))
    full = pl.BlockSpec((tm, D_MODEL), lambda i: (i, 0))
    vec = pl.BlockSpec((1, ATTN_W), lambda i: (0, 0))
    return pl.pallas_call(
        body, name="outproj_bwd", grid=(t // tm,),
        in_specs=[full, half, half, vec, vec, pl.BlockSpec((D_MODEL, D_MODEL), lambda i: (0, 0))],
        out_specs=[half, half, vec, vec],
        out_shape=[jax.ShapeDtypeStruct((t, ATTN_W), F32), jax.ShapeDtypeStruct((t, ATTN_W), F32),
                   jax.ShapeDtypeStruct((1, ATTN_W), F32), jax.ShapeDtypeStruct((1, ATTN_W), F32)],
        compiler_params=_params(("arbitrary",)),
    )(dout, oa, os_, aw, sw, w_out)


def _inproj_bwd(dproj, w_slabs, x, norm_w, dout, outgoing):
    t = x.shape[0]
    tm = 512
    nc = 4
    nt = len(outgoing)
    ni = t // tm

    def body(dp_ref, w_ref, x_ref, nw_ref, do_ref, *rest):
        src, (gx_ref, gw_ref), dst = rest[:nt], rest[nt:nt + 2], rest[nt + 2:2 * nt + 2]
        acc_ref, ssem, rsem = rest[2 * nt + 2:]
        i, j = pl.program_id(0), pl.program_id(1)
        start, wait = _bg_scatter_chips(src, dst, ssem, rsem)

        @pl.when((i == 0) & (j == 0))
        def _():
            start()
            gw_ref[...] = jnp.zeros_like(gw_ref)

        @pl.when(j == 0)
        def _():
            acc_ref[...] = jnp.zeros_like(acc_ref)

        acc_ref[...] += _dot_nt(dp_ref[...], w_ref[...])

        @pl.when(j == nc - 1)
        def _():
            xv = x_ref[...]
            r = lax.rsqrt(jnp.mean(xv * xv, axis=1, keepdims=True) + NORM_EPS)
            yh = xv * r
            dh = acc_ref[...]
            gh = dh * nw_ref[...]
            gx_ref[...] = do_ref[...] + r * (gh - yh * jnp.mean(gh * yh, axis=1, keepdims=True))
            gw_ref[...] += jnp.sum(dh * yh, axis=0, keepdims=True)

        @pl.when((i == ni - 1) & (j == nc - 1))
        def _():
            wait()

    full = pl.BlockSpec((tm, D_MODEL), lambda i, j: (i, 0))
    vec = pl.BlockSpec((1, D_MODEL), lambda i, j: (0, 0))
    return pl.pallas_call(
        body, name="inproj_bwd", grid=(ni, nc),
        in_specs=[pl.BlockSpec((tm, SHARD_W), lambda i, j: (i, j)),
                  pl.BlockSpec((None, D_MODEL, SHARD_W), lambda i, j: (j, 0, 0)), full, vec, full] + [ANY] * nt,
        out_specs=[full, vec] + [ANY] * nt,
        out_shape=[jax.ShapeDtypeStruct((t, D_MODEL), F32), jax.ShapeDtypeStruct((1, D_MODEL), F32)]
        + [jax.ShapeDtypeStruct(a.shape, a.dtype) for a in outgoing],
        scratch_shapes=[pltpu.VMEM((tm, D_MODEL), F32), pltpu.SemaphoreType.DMA((3 * nt,)),
                        pltpu.SemaphoreType.DMA((3 * nt,))],
        compiler_params=_params(("arbitrary", "arbitrary")),
    )(dproj, w_slabs, x, norm_w.reshape(1, D_MODEL), dout, *outgoing)


def _adamw_math(w_ref, g_ref, m_ref, v_ref, d_ref, nm_ref, nv_ref):
    gv = g_ref[...]
    nm = ADAM_B1 * m_ref[...] + (1.0 - ADAM_B1) * gv
    nv = ADAM_B2 * v_ref[...] + (1.0 - ADAM_B2) * (gv * gv)
    m_hat = nm / (1.0 - ADAM_B1 ** ADAM_STEP)
    v_hat = nv / (1.0 - ADAM_B2 ** ADAM_STEP)
    d_ref[...] = -ADAM_LR * (m_hat / (jnp.sqrt(v_hat) + ADAM_EPS) + ADAM_WD * w_ref[...])
    nm_ref[...] = nm
    nv_ref[...] = nv


def _adamw_halves(w, mine, theirs, m, v, c_idx, *, rows, name):
    hr, cols = mine.shape
    nblk = hr // rows

    def body(c_ref, w_ref, a_ref, b_ref, m_ref, v_ref, g_ref, d_ref, nm_ref, nv_ref):
        g_ref[...] = jnp.where(pl.program_id(0) == c_ref[0], a_ref[...], b_ref[...])
        _adamw_math(w_ref, g_ref, m_ref, v_ref, d_ref, nm_ref, nv_ref)

    full = pl.BlockSpec((rows, cols), lambda h, i, c: (h * nblk + i, 0))
    part = pl.BlockSpec((rows, cols), lambda h, i, c: (i, 0))
    shp = jax.ShapeDtypeStruct((2 * hr, cols), F32)
    return pl.pallas_call(
        body, name=name,
        grid_spec=pltpu.PrefetchScalarGridSpec(num_scalar_prefetch=1, grid=(2, nblk),
                                               in_specs=[full, part, part, full, full], out_specs=[full] * 4),
        out_shape=[shp] * 4, compiler_params=_params(("parallel", "parallel")),
    )(c_idx, w, mine, theirs, m, v)


def _adamw(w, g, m, v, *, rows, name):
    r, c = w.shape

    def body(w_ref, g_ref, m_ref, v_ref, d_ref, nm_ref, nv_ref):
        _adamw_math(w_ref, g_ref, m_ref, v_ref, d_ref, nm_ref, nv_ref)

    blk = pl.BlockSpec((rows, c), lambda i: (i, 0))
    shp = jax.ShapeDtypeStruct((r, c), F32)
    return pl.pallas_call(body, name=name, grid=(r // rows,), in_specs=[blk] * 4, out_specs=[blk] * 3,
                          out_shape=[shp] * 3, compiler_params=_params(("parallel",)))(w, g, m, v)


def _remote(src, dst, ssem, rsem, dev):
    return pltpu.make_async_remote_copy(src_ref=src, dst_ref=dst, send_sem=ssem, recv_sem=rsem, device_id=dev,
                                        device_id_type=pl.DeviceIdType.MESH)


def _mesh_pos():
    return lax.axis_index("x"), lax.axis_index("y"), lax.axis_index("c")


def _other_chips(x, y):
    return [(1 - x, y), (x, 1 - y), (1 - x, 1 - y)]


def _flips():
    return [(dx, dy, dc) for dx in (0, 1) for dy in (0, 1) for dc in (0, 1) if (dx, dy, dc) != (0, 0, 0)]


def _background(sends, arrivals):
    def start():
        for cp in sends():
            cp.start()

    def wait():
        for cp in arrivals():
            cp.wait_recv()
        for cp in sends():
            cp.wait_send()

    return start, wait


def _bg_gather(sh, full, ssem, rsem):
    x, y, c = _mesh_pos()
    me = 2 * x + y
    peers = [(px, py, c) for px, py in _other_chips(x, y)] + [(x, y, 1 - c)]
    slots = [2 * px + py for px, py in _other_chips(x, y)] + [me]
    pairs = [(i, k) for i in range(len(sh)) for k in range(4)]
    return _background(
        lambda: [_remote(sh[i], full[i].at[me], ssem.at[4 * i + k], rsem.at[4 * i + k], peers[k]) for i, k in pairs],
        lambda: [_remote(full[i].at[slots[k]], full[i].at[slots[k]], ssem.at[4 * i + k], rsem.at[4 * i + k], peers[k])
                 for i, k in pairs])


def _bg_scatter_devices(src, dst, ssem, rsem):
    x, y, c = _mesh_pos()
    me = 4 * x + 2 * y + c
    peers = []
    for dx, dy, dc in _flips():
        px, py, pc = jnp.bitwise_xor(x, dx), jnp.bitwise_xor(y, dy), jnp.bitwise_xor(c, dc)
        peers.append(((px, py, pc), 4 * px + 2 * py + pc))
    pairs = [(i, k) for i in range(len(src)) for k in range(7)]
    return _background(
        lambda: [_remote(src[i].at[peers[k][1]], dst[i].at[me], ssem.at[7 * i + k], rsem.at[7 * i + k], peers[k][0])
                 for i, k in pairs],
        lambda: [_remote(dst[i].at[peers[k][1]], dst[i].at[peers[k][1]], ssem.at[7 * i + k], rsem.at[7 * i + k],
                         peers[k][0]) for i, k in pairs])


def _bg_scatter_chips(src, dst, ssem, rsem):
    x, y, c = _mesh_pos()
    me = 2 * x + y
    chips = _other_chips(x, y)
    pairs = [(i, k) for i in range(len(src)) for k in range(3)]
    slot = lambda k: 2 * chips[k][0] + chips[k][1]
    return _background(
        lambda: [_remote(src[i].at[slot(k)], dst[i].at[me], ssem.at[3 * i + k], rsem.at[3 * i + k], (*chips[k], c))
                 for i, k in pairs],
        lambda: [_remote(dst[i].at[slot(k)], dst[i].at[slot(k)], ssem.at[3 * i + k], rsem.at[3 * i + k], (*chips[k], c))
                 for i, k in pairs])


def _pair_swap(arrays):
    nt = len(arrays)

    def body(*refs):
        src, dst = refs[:nt], refs[nt:2 * nt]
        ssem, rsem = refs[2 * nt:]
        x, y, c = _mesh_pos()
        cps = [_remote(src[i].at[:, 1 - c], dst[i], ssem.at[i], rsem.at[i], (x, y, 1 - c)) for i in range(nt)]
        for cp in cps:
            cp.start()
        for cp in cps:
            cp.wait_recv()
        for cp in cps:
            cp.wait_send()

    return pl.pallas_call(
        body, name="pair_swap", in_specs=[ANY] * nt, out_specs=[ANY] * nt,
        out_shape=[jax.ShapeDtypeStruct((4,) + a.shape[2:], a.dtype) for a in arrays],
        scratch_shapes=[pltpu.SemaphoreType.DMA((nt,)), pltpu.SemaphoreType.DMA((nt,))],
    )(*arrays)


def _half_swap(arrays):
    nt = len(arrays)

    def body(*refs):
        src, dst = refs[:nt], refs[nt:2 * nt]
        ssem, rsem = refs[2 * nt:]
        x, y, c = _mesh_pos()
        cps = [_remote(src[i], dst[i], ssem.at[i], rsem.at[i], (x, y, 1 - c)) for i in range(nt)]
        for cp in cps:
            cp.start()
        for cp in cps:
            cp.wait_recv()
        for cp in cps:
            cp.wait_send()

    return pl.pallas_call(
        body, name="half_swap", in_specs=[ANY] * nt, out_specs=[ANY] * nt,
        out_shape=[jax.ShapeDtypeStruct(a.shape, a.dtype) for a in arrays],
        scratch_shapes=[pltpu.SemaphoreType.DMA((nt,)), pltpu.SemaphoreType.DMA((nt,))],
    )(*arrays)


def _exchange_slices(src, scatter, name):
    def body(src_ref, dst_ref, ssem, rsem, lsem):
        x, y, c = _mesh_pos()
        me = 4 * x + 2 * y + c
        local = pltpu.make_async_copy(src_ref.at[me] if scatter else src_ref, dst_ref.at[me], lsem)
        local.start()
        cps = []
        for k, (dx, dy, dc) in enumerate(_flips()):
            px, py, pc = jnp.bitwise_xor(x, dx), jnp.bitwise_xor(y, dy), jnp.bitwise_xor(c, dc)
            peer = 4 * px + 2 * py + pc
            cp = _remote(src_ref.at[peer] if scatter else src_ref, dst_ref.at[me], ssem.at[k], rsem.at[k],
                         (px, py, pc))
            cp.start()
            cps.append((cp, peer))
        for k, (cp, peer) in enumerate(cps):
            slot = dst_ref.at[peer]
            _remote(slot, slot, ssem.at[k], rsem.at[k], (x, y, c)).wait_recv()
        for cp, _ in cps:
            cp.wait_send()
        local.wait()

    return pl.pallas_call(
        body, name=name, in_specs=[ANY], out_specs=ANY,
        out_shape=jax.ShapeDtypeStruct((8,) + src.shape[-2:], src.dtype),
        scratch_shapes=[pltpu.SemaphoreType.DMA((7,)), pltpu.SemaphoreType.DMA((7,)), pltpu.SemaphoreType.DMA],
    )(src)


def _add_halves(g, recv, c_idx, *, rows, name):
    _, _, hr, cols = g.shape

    def body(c_ref, g_ref, r_ref, o_ref):
        o_ref[...] = (g_ref[...] + r_ref[...].astype(F32)).astype(BF16)

    return pl.pallas_call(
        body, name=name,
        grid_spec=pltpu.PrefetchScalarGridSpec(
            num_scalar_prefetch=1, grid=(4, hr // rows),
            in_specs=[pl.BlockSpec((None, None, rows, cols), lambda j, i, c: (j, c[0], i, 0)),
                      pl.BlockSpec((None, rows, cols), lambda j, i, c: (j, i, 0))],
            out_specs=pl.BlockSpec((None, rows, cols), lambda j, i, c: (j, i, 0))),
        out_shape=jax.ShapeDtypeStruct((4, hr, cols), BF16),
        compiler_params=_params(("parallel", "parallel")),
    )(c_idx, g, recv)


def _sum_peers(slots, own, idx, *, rows, name):
    n, r, cols = slots.shape

    def body(me_ref, *refs):
        me = me_ref[0]
        mine = refs[n][...].astype(F32)
        acc = None
        for k in range(n):
            term = jnp.where(me == k, mine, refs[k][...].astype(F32))
            acc = term if acc is None else acc + term
        refs[n + 1][...] = acc

    def slot_spec(k):
        return pl.BlockSpec((None, rows, cols), lambda i, me: (jnp.where(me[0] == k, (k + 1) % n, k), i, 0))

    return pl.pallas_call(
        body, name=name,
        grid_spec=pltpu.PrefetchScalarGridSpec(
            num_scalar_prefetch=1, grid=(r // rows,),
            in_specs=[slot_spec(k) for k in range(n)] + [pl.BlockSpec((None, rows, cols), lambda i, me: (me[0], i, 0))],
            out_specs=pl.BlockSpec((rows, cols), lambda i, me: (i, 0))),
        out_shape=jax.ShapeDtypeStruct((r, cols), F32),
        compiler_params=_params(("parallel",)),
    )(idx, *([slots] * n), own)


def _sum_slots(slots, *, rows, name):
    n, r, cols = slots.shape

    def body(s_ref, o_ref):
        acc = s_ref[0].astype(F32)
        for k in range(1, n):
            acc = acc + s_ref[k].astype(F32)
        o_ref[...] = acc

    return pl.pallas_call(
        body, name=name, grid=(r // rows,),
        in_specs=[pl.BlockSpec((n, rows, cols), lambda i: (0, i, 0))],
        out_specs=pl.BlockSpec((rows, cols), lambda i: (i, 0)),
        out_shape=jax.ShapeDtypeStruct((r, cols), F32),
        compiler_params=_params(("parallel",)),
    )(slots)


def _pack_small(d, names, rows):
    flat = jnp.concatenate([d[n].astype(F32).reshape(-1) for n in names])
    return jnp.pad(flat, (0, rows * 128 - flat.shape[0])).reshape(rows, 128)


def _unpack_small(p, names):
    flat = p.reshape(-1)
    out, off = {}, 0
    for n in names:
        size = math.prod(SMALL_SHAPES[n])
        out[n] = flat[off:off + size].reshape(SMALL_SHAPES[n])
        off += size
    return out


def _adamw_3d(w, g, m, v, *, name):
    def body(w_ref, g_ref, m_ref, v_ref, d_ref, nm_ref, nv_ref):
        _adamw_math(w_ref, g_ref, m_ref, v_ref, d_ref, nm_ref, nv_ref)

    blk = pl.BlockSpec((8,) + w.shape[1:], lambda i: (i, 0, 0))
    shp = jax.ShapeDtypeStruct(w.shape, F32)
    return pl.pallas_call(body, name=name, grid=(w.shape[0] // 8,), in_specs=[blk] * 4, out_specs=[blk] * 3,
                          out_shape=[shp] * 3, compiler_params=_params(("parallel",)))(w, g, m, v)


def kernel(x, positions, norm_w, w_in, q_norm_w, k_norm_w, sinks, a_re, a_im, log_step, b_re, b_im, c_re, c_im, d_skip, w_glu, b_glu, attn_out_norm_w, ssm_out_norm_w, w_out, loss_target, m_norm_w, m_w_in, m_q_norm_w, m_k_norm_w, m_sinks, m_a_re, m_a_im, m_log_step, m_b_re, m_b_im, m_c_re, m_c_im, m_d_skip, m_w_glu, m_b_glu, m_attn_out_norm_w, m_ssm_out_norm_w, m_w_out, v_norm_w, v_w_in, v_q_norm_w, v_k_norm_w, v_sinks, v_a_re, v_a_im, v_log_step, v_b_re, v_b_im, v_c_re, v_c_im, v_d_skip, v_w_glu, v_b_glu, v_attn_out_norm_w, v_ssm_out_norm_w, v_w_out):
    small_w = dict(norm_w=norm_w, q_norm_w=q_norm_w, k_norm_w=k_norm_w, sinks=sinks, a_re=a_re, a_im=a_im,
                   log_step=log_step, b_re=b_re, b_im=b_im, c_re=c_re, c_im=c_im, d_skip=d_skip, b_glu=b_glu,
                   attn_out_norm_w=attn_out_norm_w, ssm_out_norm_w=ssm_out_norm_w)
    small_m = dict(norm_w=m_norm_w, q_norm_w=m_q_norm_w, k_norm_w=m_k_norm_w, sinks=m_sinks, a_re=m_a_re, a_im=m_a_im,
                   log_step=m_log_step, b_re=m_b_re, b_im=m_b_im, c_re=m_c_re, c_im=m_c_im, d_skip=m_d_skip,
                   b_glu=m_b_glu, attn_out_norm_w=m_attn_out_norm_w, ssm_out_norm_w=m_ssm_out_norm_w)
    small_v = dict(norm_w=v_norm_w, q_norm_w=v_q_norm_w, k_norm_w=v_k_norm_w, sinks=v_sinks, a_re=v_a_re, a_im=v_a_im,
                   log_step=v_log_step, b_re=v_b_re, b_im=v_b_im, c_re=v_c_re, c_im=v_c_im, d_skip=v_d_skip,
                   b_glu=v_b_glu, attn_out_norm_w=v_attn_out_norm_w, ssm_out_norm_w=v_ssm_out_norm_w)
    c_idx = lax.axis_index("c").astype(jnp.int32).reshape(1)
    chip_idx = (2 * lax.axis_index("x") + lax.axis_index("y")).astype(jnp.int32).reshape(1)
    dev_idx = 2 * chip_idx + c_idx

    xs = x[0]
    tgt = loss_target[0]
    t = xs.shape[0]
    posf = positions[0].astype(F32).reshape(t, 1)

    mx, my = lax.axis_index("x"), lax.axis_index("y")
    slab_order = jnp.stack([2 * mx + my, 2 * (1 - mx) + my, 2 * mx + (1 - my), 2 * (1 - mx) + (1 - my)]).astype(jnp.int32)
    proj, hn, w_in_all = _inproj(xs, norm_w, w_in.astype(BF16), slab_order)
    inv_freq = ROPE_THETA ** (-jnp.arange(0, HEAD_DIM, 2, dtype=F32) / HEAD_DIM)
    rope = _rope_table(posf, jnp.tile(inv_freq, 4).reshape(1, 128))
    qw = jnp.tile(q_norm_w, 2).reshape(1, 128)
    kw = jnp.tile(k_norm_w, 2).reshape(1, 128)
    sink_row = sinks.reshape(1, N_HEADS)
    oa, w_glu_all, w_out_all = _attn_fwd(proj, rope, qw, kw, sink_row, [w_glu.astype(BF16), w_out.astype(BF16)])
    w_glu_b = w_glu_all.reshape(SSM_W, SSM_W)
    w_out_b = w_out_all.reshape(D_MODEL, D_MODEL)

    lam_r, lam_i, pw_r, pw_i, bb_r, bb_i = _ssm_prep(a_re, a_im, log_step, b_re, b_im, t // N_SEG)
    rows8 = lambda a: jnp.broadcast_to(a.reshape(SSM_GB, 1, SSM_ST), (SSM_GB, N_SEG, SSM_ST))
    lam_r8, lam_i8, pw_r8, pw_i8 = rows8(lam_r), rows8(lam_i), rows8(pw_r), rows8(pw_i)
    ssm_w_in = jnp.concatenate([_block_diag_in(bb_r), _block_diag_in(bb_i)], axis=1).astype(BF16)
    ssm_w_out = jnp.concatenate([_block_diag_out(c_re), _block_diag_out(-c_im)], axis=2).astype(BF16)
    d_row = d_skip.reshape(1, SSM_W)
    y, hc = _ssm_fwd(proj, lam_r8, lam_i8, pw_r8, pw_i8, ssm_w_in, ssm_w_out, d_row)
    b_glu_row = b_glu.reshape(1, SSM_W)
    os_, yg = _glu_fwd(y, proj, w_glu_b, b_glu_row)
    aw = attn_out_norm_w.reshape(1, ATTN_W)
    sw = ssm_out_norm_w.reshape(1, SSM_W)
    merged, dout, sq_err = _outproj(oa, os_, aw, sw, w_out_b, xs, tgt)
    loss = lax.psum(0.5 * sq_err[0, 0] / D_MODEL, MESH_AXES)

    doa, dos, g_aw, g_sw = _outproj_bwd(dout, oa, os_, aw, sw, w_out_b)
    dout_b = dout.astype(BF16)
    (g_w_out_b,) = _matmul_tn(merged, dout_b, tm=512, tn=1024, name="grad_w_out", dtypes=(BF16,))
    dy, dzs, da, g_b_glu = _glu_bwd(y, proj, dos, w_glu_b, b_glu_row)
    (g_w_glu_b,) = _matmul_tn(yg, da, tm=512, tn=1024, name="grad_w_glu", dtypes=(BF16,))
    du, g_wi, g_wo, g_lam, g_d = _ssm_bwd(proj, dy, hc, lam_r8, lam_i8, pw_r8, pw_i8, ssm_w_in, ssm_w_out, d_row)
    early = [g_w_glu_b.reshape(8, 128, SSM_W), g_w_out_b.reshape(8, 256, D_MODEL)]
    dproj, g_qw, g_kw, g_sink, *early_slots = _attn_bwd(proj, rope, qw, kw, sink_row, doa, du, dzs, early)
    g_w_in, g_w_in_b = _matmul_tn(hn, dproj, tm=512, tn=SHARD_W, name="grad_w_in", slabs=True)
    in_shape = (4, 2, D_MODEL // 2, SHARD_W)
    (from_sib,) = _pair_swap([g_w_in_b.reshape(in_shape)])
    pair_in = _add_halves(g_w_in.reshape(in_shape), from_sib, c_idx, rows=128, name="pair_sum")
    grad_x, g_nw, in_slots = _inproj_bwd(dproj, w_in_all, xs, norm_w, dout, [pair_in])

    g_wi = g_wi.reshape(SSM_G, SSM_H, 2 * SSM_P)
    g_wo = g_wo.reshape(SSM_G, SSM_H, 2 * SSM_P)
    g_bb_r = g_wi[:, :, :SSM_P].transpose(0, 2, 1).reshape(SSM_G, SSM_P * SSM_H)
    g_bb_i = g_wi[:, :, SSM_P:].transpose(0, 2, 1).reshape(SSM_G, SSM_P * SSM_H)
    g_a_re, g_a_im, g_ls, g_b_re, g_b_im = _ssm_param_grads(
        a_re, a_im, log_step, b_re, b_im, g_lam[:, 0, :SSM_ST].reshape(SSM_G, SSM_P),
        g_lam[:, 0, SSM_ST:].reshape(SSM_G, SSM_P), g_bb_r, g_bb_i)
    small_g = dict(
        norm_w=g_nw, q_norm_w=g_qw[0, :64] + g_qw[0, 64:], k_norm_w=g_kw[0, :64] + g_kw[0, 64:],
        sinks=g_sink[0, :N_HEADS], a_re=g_a_re, a_im=g_a_im, log_step=g_ls, b_re=g_b_re, b_im=g_b_im,
        c_re=g_wo[:, :, :SSM_P], c_im=-g_wo[:, :, SSM_P:], d_skip=g_d,
        b_glu=g_b_glu, attn_out_norm_w=g_aw, ssm_out_norm_w=g_sw)

    mine = [_sum_peers(in_slots, pair_in, chip_idx, rows=128, name="sum_w_in"),
            _sum_peers(early_slots[0], early[0], dev_idx, rows=128, name="sum_w_glu"),
            _sum_peers(early_slots[1], early[1], dev_idx, rows=128, name="sum_w_out")]
    theirs = _half_swap(mine)
    packed = _pack_small(small_g, SMALL, 8 * PACK_ROWS).reshape(8, PACK_ROWS, 128)
    summed = _sum_slots(_exchange_slices(packed, True, "small_scatter"), rows=PACK_ROWS, name="small_sum")
    small_red = _exchange_slices(summed, False, "small_gather").reshape(8 * PACK_ROWS, 128)

    big = [_adamw_halves(w_in, mine[0], theirs[0], m_w_in, v_w_in, c_idx, rows=256, name="adamw_w_in"),
           _adamw_halves(w_glu, mine[1], theirs[1], m_w_glu, v_w_glu, c_idx, rows=128, name="adamw_w_glu"),
           _adamw_halves(w_out, mine[2], theirs[2], m_w_out, v_w_out, c_idx, rows=256, name="adamw_w_out")]
    g_in_sh, g_glu_sh, g_out_sh = (b[0] for b in big)
    upd = [b[1:] for b in big]
    grads = _unpack_small(small_red, SMALL)
    flat_first = sum(math.prod(SMALL_SHAPES[n]) for n in SMALL_3D) // 128
    sd, sm, sv = _adamw(_pack_small(small_w, SMALL_FLAT, FLAT_ROWS), small_red[flat_first:flat_first + FLAT_ROWS],
                        _pack_small(small_m, SMALL_FLAT, FLAT_ROWS), _pack_small(small_v, SMALL_FLAT, FLAT_ROWS),
                        rows=FLAT_ROWS, name="adamw_small")
    deltas, new_m, new_v = (_unpack_small(a, SMALL_FLAT) for a in (sd, sm, sv))
    for n in SMALL_3D:
        deltas[n], new_m[n], new_v[n] = _adamw_3d(small_w[n], grads[n], small_m[n], small_v[n], name="adamw_" + n)
    grads.update(w_in=g_in_sh, w_glu=g_glu_sh, w_out=g_out_sh)
    for n, (d, m_, v_) in zip(("w_in", "w_glu", "w_out"), upd):
        deltas[n], new_m[n], new_v[n] = d, m_, v_
    order = ["norm_w", "w_in", "q_norm_w", "k_norm_w", "sinks", "a_re", "a_im", "log_step", "b_re", "b_im", "c_re",
             "c_im", "d_skip", "w_glu", "b_glu", "attn_out_norm_w", "ssm_out_norm_w", "w_out"]
    return (loss, grad_x[None], *[grads[n] for n in order], *[deltas[n] for n in order],
            *[new_m[n] for n in order], *[new_v[n] for n in order])
```

```python
import math

import jax
import jax.numpy as jnp
from jax import lax
from jax.experimental import pallas as pl
from jax.experimental.pallas import tpu as pltpu

F32 = jnp.float32
BF16 = jnp.bfloat16

D_MODEL = 2048
ATTN_W = 1024
SSM_W = 1024
HEAD_DIM = 64
N_HEADS = 16
N_KV_HEADS = 4
KV_W = 256
BLOCK = 128
IN_W = 4608
SHARD_W = IN_W // 4
ROPE_THETA = 10000.0
SSM_H = 16
SSM_G = 64
SSM_P = 64
NORM_EPS = 1e-6
ADAM_LR = 0.001
ADAM_B1 = 0.9
ADAM_B2 = 0.999
ADAM_EPS = 1e-08
ADAM_WD = 0.01
ADAM_STEP = 10

N_SEG = 8
SSM_GB = 4
SSM_CH = 256
SSM_ST = 1024
SCAN_ROWS = 256
SCAN_LW = 256
VMEM_LIMIT = 56 * 1024 * 1024
MESH_AXES = ("x", "y", "c")
ANY = pl.BlockSpec(memory_space=pl.ANY)

SMALL_3D = ("b_re", "b_im", "c_re", "c_im")
SMALL_FLAT = ("norm_w", "q_norm_w", "k_norm_w", "sinks", "a_re", "a_im", "log_step", "d_skip", "b_glu",
              "attn_out_norm_w", "ssm_out_norm_w")
SMALL = SMALL_3D + SMALL_FLAT
SMALL_SHAPES = {"norm_w": (2048,), "q_norm_w": (64,), "k_norm_w": (64,), "sinks": (16,), "a_re": (64, 64),
                "a_im": (64, 64), "log_step": (64,), "b_re": (64, 64, 16), "b_im": (64, 64, 16),
                "c_re": (64, 16, 64), "c_im": (64, 16, 64), "d_skip": (1024,), "b_glu": (1024,),
                "attn_out_norm_w": (1024,), "ssm_out_norm_w": (1024,)}
PACK_ROWS = 272
FLAT_ROWS = 120


def _params(sem=None):
    return pltpu.CompilerParams(dimension_semantics=sem, vmem_limit_bytes=VMEM_LIMIT)


def _dot(a, b):
    return jnp.dot(a, b, preferred_element_type=F32)


def _dot_nt(a, b):
    return lax.dot_general(a, b, (((1,), (1,)), ((), ())), preferred_element_type=F32)


def _dot_tn(a, b):
    return lax.dot_general(a, b, (((0,), (0,)), ((), ())), preferred_element_type=F32)


def _sigmoid(x):
    return 1.0 / (1.0 + jnp.exp(-x))


def _silu(x):
    return x * _sigmoid(x)


def _dsilu(x):
    s = _sigmoid(x)
    return s * (1.0 + x * (1.0 - s))


_GELU_C = math.sqrt(2.0 / math.pi)


def _gelu(x):
    return 0.5 * x * (1.0 + jnp.tanh(_GELU_C * (x + 0.044715 * x * x * x)))


def _dgelu(x):
    t = jnp.tanh(_GELU_C * (x + 0.044715 * x * x * x))
    return 0.5 * (1.0 + t) + 0.5 * x * (1.0 - t * t) * _GELU_C * (1.0 + 3.0 * 0.044715 * x * x)


def _matmul_tn(a, b, *, tm, tn, name, slabs=False, dtypes=(F32, BF16)):
    k, m = a.shape
    _, n = b.shape

    def body(a_ref, b_ref, *o_refs):
        acc = _dot_tn(a_ref[...], b_ref[...])
        for o_ref in o_refs:
            o_ref[...] = acc.astype(o_ref.dtype)

    if slabs:
        out_spec = pl.BlockSpec((None, tm, tn), lambda j, i: (j, i, 0))
        shape = (n // tn, m, tn)
    else:
        out_spec = pl.BlockSpec((tm, tn), lambda j, i: (i, j))
        shape = (m, n)
    return pl.pallas_call(
        body, name=name, grid=(n // tn, m // tm),
        in_specs=[pl.BlockSpec((k, tm), lambda j, i: (0, i)), pl.BlockSpec((k, tn), lambda j, i: (0, j))],
        out_specs=[out_spec] * len(dtypes),
        out_shape=[jax.ShapeDtypeStruct(shape, d) for d in dtypes],
        compiler_params=_params(("parallel", "parallel")),
    )(a, b)


def _inproj(x, norm_w, w_sh, order):
    t = x.shape[0]
    tm = 512
    ni = t // tm
    hr = D_MODEL // 2

    def body(ord_ref, x_ref, nw_ref, sh_ref, proj_ref, hn_ref, full_ref, wbuf, hn_s, ssem, rsem, lsem):
        s, i = pl.program_id(0), pl.program_id(1)
        mx, my, c = _mesh_pos()
        me = 2 * mx + my
        sib = (mx, my, 1 - c)
        chips = _other_chips(mx, my)

        def half(which):
            return pl.ds(pl.multiple_of(which * hr, 8), hr)

        def slot(k):
            return 2 * chips[k][0] + chips[k][1]

        def ici(k):
            return _remote(sh_ref.at[half(c)], full_ref.at[me, half(c)], ssem.at[k], rsem.at[k], (*chips[k], c))

        def own():
            return _remote(sh_ref, full_ref.at[me], ssem.at[6], rsem.at[6], sib)

        def landed(k, which, sem):
            ref = full_ref.at[slot(k), half(which)]
            return _remote(ref, ref, ssem.at[sem], rsem.at[sem], sib)

        def fetch(src, b):
            return pltpu.make_async_copy(src, wbuf.at[b], lsem.at[b])

        @pl.when((s == 0) & (i == 0))
        def _():
            for k in range(3):
                ici(k).start()
            own().start()
            cp = fetch(sh_ref, 0)
            cp.start()
            cp.wait()

        for k in range(3):
            @pl.when((s == k) & (i == max(ni - 2, 0)))
            def _(k=k):
                landed(k, c, k).wait_recv()
                landed(k, c, 3 + k).start()
                landed(k, 1 - c, 3 + k).wait_recv()
                fetch(full_ref.at[slot(k)], (k + 1) % 2).start()

            @pl.when((s == k + 1) & (i == 0))
            def _(k=k):
                fetch(full_ref.at[slot(k)], (k + 1) % 2).wait()

        xv = x_ref[...]
        r = lax.rsqrt(jnp.mean(xv * xv, axis=1, keepdims=True) + NORM_EPS)
        hn = (xv * r * nw_ref[...]).astype(BF16)
        proj_ref[...] = _dot(hn, wbuf[s % 2])

        def hn_out(tile):
            return pltpu.make_async_copy(hn_s, hn_ref.at[pl.ds(pl.multiple_of(tile * tm, tm), tm), :], lsem.at[2])

        @pl.when(((s == 0) & (i > 0)) | ((s == 1) & (i == 0)))
        def _():
            hn_out(jnp.where(s == 0, i - 1, ni - 1)).wait()

        @pl.when(s == 0)
        def _():
            hn_s[...] = hn
            hn_out(i).start()

        @pl.when((s == 3) & (i == ni - 1))
        def _():
            mine = full_ref.at[me]
            _remote(mine, mine, ssem.at[6], rsem.at[6], sib).wait_recv()
            for k in range(3):
                ici(k).wait_send()
                landed(k, c, 3 + k).wait_send()
            own().wait_send()

    return pl.pallas_call(
        body, name="inproj",
        grid_spec=pltpu.PrefetchScalarGridSpec(
            num_scalar_prefetch=1, grid=(4, ni),
            in_specs=[pl.BlockSpec((tm, D_MODEL), lambda s, i, o: (i, 0)),
                      pl.BlockSpec((1, D_MODEL), lambda s, i, o: (0, 0)), ANY],
            out_specs=[pl.BlockSpec((tm, SHARD_W), lambda s, i, o: (i, o[s])), ANY, ANY],
            scratch_shapes=[pltpu.VMEM((2, D_MODEL, SHARD_W), BF16), pltpu.VMEM((tm, D_MODEL), BF16),
                            pltpu.SemaphoreType.DMA((7,)), pltpu.SemaphoreType.DMA((7,)),
                            pltpu.SemaphoreType.DMA((3,))]),
        out_shape=[jax.ShapeDtypeStruct((t, IN_W), F32), jax.ShapeDtypeStruct((t, D_MODEL), BF16),
                   jax.ShapeDtypeStruct((4, D_MODEL, SHARD_W), BF16)],
        compiler_params=_params(("arbitrary", "arbitrary")),
    )(order, x, norm_w.reshape(1, D_MODEL), w_sh)


def _lane128():
    return lax.broadcasted_iota(jnp.int32, (1, 128), 1)


def _head_sums(v):
    lo = _lane128() < 64
    s_lo = jnp.sum(jnp.where(lo, v, 0.0), axis=1, keepdims=True)
    s_hi = jnp.sum(jnp.where(lo, 0.0, v), axis=1, keepdims=True)
    return jnp.where(lo, s_lo, s_hi)


def _rot_half(t):
    first = (_lane128() % 64) < 32
    return jnp.where(first, -pltpu.roll(t, 96, 1), pltpu.roll(t, 32, 1))


def _head_rstd(t):
    return lax.rsqrt(_head_sums(t * t) * (1.0 / HEAD_DIM) + NORM_EPS)


def _prep_tile(t, w, cos, sin, r=None):
    r = _head_rstd(t) if r is None else r
    tn = t * r * w
    return tn * cos + _rot_half(tn) * sin


def _prep_tile_bwd(t, w, cos, sin, g, r=None):
    r = _head_rstd(t) if r is None else r
    d_tn = g * cos - _rot_half(g * sin)
    th = t * r
    dw = jnp.sum(d_tn * th, axis=0, keepdims=True)
    gh = d_tn * w
    m = _head_sums(gh * th) * (1.0 / HEAD_DIM)
    return r * (gh - th * m), dw


def _band_mask(n):
    qi = lax.broadcasted_iota(jnp.int32, (BLOCK, 2 * BLOCK), 0) + BLOCK
    ki = lax.broadcasted_iota(jnp.int32, (BLOCK, 2 * BLOCK), 1)
    rel = qi - ki
    return (rel >= 0) & (rel < BLOCK) & ((n > 0) | (ki >= BLOCK))


def _half_select(tile, half):
    lo = _lane128() < 64
    return jnp.where(lo if half == 0 else jnp.logical_not(lo), tile, 0.0)


def _stack_group(tiles, kv_half):
    rows = []
    for t in tiles:
        for half in range(2):
            piece = _half_select(t, half)
            rows.append(piece if half == kv_half else pltpu.roll(piece, 64, 1))
    return jnp.concatenate(rows, axis=0)


def _unstack_group(stacked, kv_half):
    tiles = []
    for i in range(2):
        acc = None
        for half in range(2):
            piece = _half_select(stacked[BLOCK * (2 * i + half):BLOCK * (2 * i + half + 1)], kv_half)
            piece = piece if half == kv_half else pltpu.roll(piece, 64, 1)
            acc = piece if acc is None else acc + piece
        tiles.append(acc)
    return tiles


def _stack_heads(tiles):
    zeros = jnp.zeros((4 * BLOCK, 128), F32)
    rows = []
    for g in range(N_KV_HEADS):
        half = _stack_group(tiles[2 * g:2 * g + 2], g % 2)
        rows.append(jnp.concatenate([half, zeros] if g < 2 else [zeros, half], axis=1))
    return jnp.concatenate(rows, axis=0)


def _unstack_heads(stacked):
    tiles = []
    for g in range(N_KV_HEADS):
        lanes = slice(0, 128) if g < 2 else slice(128, 256)
        tiles += _unstack_group(stacked[4 * BLOCK * g:4 * BLOCK * (g + 1), lanes], g % 2)
    return tiles


def _attn_specs(nb):
    last = nb - 1
    qi = lambda n: (jnp.minimum(n, last), 0)
    prev = lambda n: jnp.maximum(n - 1, 0)
    cur = lambda n: jnp.minimum(n, last)
    specs = [
        pl.BlockSpec((BLOCK, ATTN_W), qi),
        pl.BlockSpec((BLOCK, KV_W), lambda n: (cur(n), 4)),
        pl.BlockSpec((BLOCK, KV_W), lambda n: (prev(n), 4)),
        pl.BlockSpec((BLOCK, KV_W), lambda n: (cur(n), 5)),
        pl.BlockSpec((BLOCK, KV_W), lambda n: (prev(n), 5)),
        pl.BlockSpec((BLOCK, 512), lambda n: (cur(n), 3)),
        pl.BlockSpec((BLOCK, 512), lambda n: (cur(n), 4)),
        pl.BlockSpec((BLOCK, 256), lambda n: (cur(n), 0)),
        pl.BlockSpec((BLOCK, 256), lambda n: (prev(n), 0)),
        pl.BlockSpec((1, 128), lambda n: (0, 0)),
        pl.BlockSpec((1, 128), lambda n: (0, 0)),
        pl.BlockSpec((1, N_HEADS), lambda n: (0, 0)),
    ]
    return specs


def _rope_table(posf, invf):
    t = posf.shape[0]

    def body(p_ref, f_ref, o_ref):
        ang = p_ref[...] * f_ref[...]
        o_ref[...] = jnp.concatenate([jnp.cos(ang), jnp.sin(ang)], axis=1)

    return pl.pallas_call(
        body, name="rope_table", grid=(t // 512,),
        in_specs=[pl.BlockSpec((512, 1), lambda i: (i, 0)), pl.BlockSpec((1, 128), lambda i: (0, 0))],
        out_specs=pl.BlockSpec((512, 256), lambda i: (i, 0)),
        out_shape=jax.ShapeDtypeStruct((t, 256), F32), compiler_params=_params(("parallel",)),
    )(posf, invf)


def _attn_common(n, q_ref, kc_ref, kp_ref, vc_ref, vp_ref, rq_ref, rp_ref, qw_ref, kw_ref):
    cos_q, sin_q = rq_ref[:, 0:128], rq_ref[:, 128:256]
    cos_k = jnp.concatenate([rp_ref[:, 0:128], cos_q], axis=0)
    sin_k = jnp.concatenate([rp_ref[:, 128:256], sin_q], axis=0)
    k_raw = jnp.concatenate([kp_ref[...], kc_ref[...]], axis=0)
    vv = jnp.concatenate([vp_ref[...], vc_ref[...]], axis=0).astype(BF16)
    kk = [_prep_tile(k_raw[:, 128 * i:128 * i + 128], kw_ref[...], cos_k, sin_k).astype(BF16) for i in range(2)]
    vt = [vv[:, 128 * i:128 * i + 128] for i in range(2)]
    qv = q_ref[...]
    qr = [_head_rstd(qv[:, 128 * i:128 * i + 128]) for i in range(8)]
    qt = [_prep_tile(qv[:, 128 * i:128 * i + 128], qw_ref[...], cos_q, sin_q, qr[i]) for i in range(8)]
    return cos_q, sin_q, qr, kk, vt, qt


QK_SCALE = 1.0 / math.sqrt(HEAD_DIM)


def _group_sinks(sink_ref, g):
    return jnp.concatenate([jnp.broadcast_to(sink_ref[:, 4 * g + j:4 * g + j + 1], (BLOCK, 1)) for j in range(4)], axis=0)


def _group_softmax(q4, kk_t, sink, bias):
    s = _dot_nt(q4, kk_t) + bias
    m = jnp.maximum(jnp.max(s, axis=1, keepdims=True), sink)
    p = jnp.exp(s - m)
    es = jnp.exp(sink - m)
    inv = 1.0 / (jnp.sum(p, axis=1, keepdims=True) + es)
    return p * inv, es * inv


def _group_bias(n):
    return jnp.concatenate([jnp.where(_band_mask(n), 0.0, -1e30)] * 4, axis=0)


def _attn_fwd(proj, rope, qw, kw, sinks, later_shards):
    t = proj.shape[0]
    nb = t // BLOCK
    nt = len(later_shards)

    def body(q_ref, kc_ref, kp_ref, vc_ref, vp_ref, za0_ref, za1_ref, rq_ref, rp_ref, qw_ref, kw_ref,
             sink_ref, *rest):
        sh, o_ref, full = rest[:nt], rest[nt], rest[nt + 1:2 * nt + 1]
        ssem, rsem = rest[2 * nt + 1:]
        n = pl.program_id(0)
        start, wait = _bg_gather(sh, full, ssem, rsem)

        @pl.when(n == 0)
        def _():
            start()

        _, _, _, kk, vt, qt = _attn_common(n, q_ref, kc_ref, kp_ref, vc_ref, vp_ref, rq_ref, rp_ref, qw_ref, kw_ref)
        bias = jnp.concatenate([_group_bias(n)] * 4, axis=0)
        q16 = (_stack_heads(qt) * QK_SCALE).astype(BF16)
        sink16 = jnp.concatenate([_group_sinks(sink_ref, g) for g in range(N_KV_HEADS)], axis=0)
        p, _ = _group_softmax(q16, jnp.concatenate(kk, axis=1), sink16, bias)
        tiles = _unstack_heads(_dot(p.astype(BF16), jnp.concatenate(vt, axis=1)))
        za = jnp.concatenate([za0_ref[...], za1_ref[...]], axis=1)
        o_ref[...] = jnp.concatenate(tiles, axis=1) * _silu(za)

        @pl.when(n == nb - 1)
        def _():
            wait()

    return pl.pallas_call(
        body, name="attn_fwd", grid=(nb,), in_specs=_attn_specs(nb) + [ANY] * nt,
        out_specs=[pl.BlockSpec((BLOCK, ATTN_W), lambda n: (n, 0))] + [ANY] * nt,
        out_shape=[jax.ShapeDtypeStruct((t, ATTN_W), F32)]
        + [jax.ShapeDtypeStruct((4,) + s.shape, s.dtype) for s in later_shards],
        scratch_shapes=[pltpu.SemaphoreType.DMA((4 * nt,)), pltpu.SemaphoreType.DMA((4 * nt,))],
        compiler_params=_params(("arbitrary",)),
    )(proj, proj, proj, proj, proj, proj, proj, rope, rope, qw, kw, sinks, *later_shards)


def _attn_bwd(proj, rope, qw, kw, sinks, doa, du, dzs, outgoing):
    t = proj.shape[0]
    nb = t // BLOCK
    last = nb - 1
    nt = len(outgoing)

    def body(q_ref, kc_ref, kp_ref, vc_ref, vp_ref, za0_ref, za1_ref, rq_ref, rp_ref, qw_ref, kw_ref,
             sink_ref, doa_ref, du_ref, dzs_ref, *rest):
        src = rest[:nt]
        dp_ref, gq_ref, gk_ref, gs_ref = rest[nt:nt + 4]
        dst = rest[nt + 4:2 * nt + 4]
        dkk_s, dvv_s, ck_s, cv_s, dq_s, dza_s, ssem, rsem = rest[2 * nt + 4:]
        n = pl.program_id(0)
        start, wait = _bg_scatter_devices(src, dst, ssem, rsem)

        @pl.when(n == 0)
        def _():
            start()
            gq_ref[...] = jnp.zeros_like(gq_ref)
            gk_ref[...] = jnp.zeros_like(gk_ref)
            gs_ref[...] = jnp.zeros_like(gs_ref)
            ck_s[...] = jnp.zeros_like(ck_s)
            cv_s[...] = jnp.zeros_like(cv_s)
            dq_s[...] = jnp.zeros_like(dq_s)
            dza_s[...] = jnp.zeros_like(dza_s)

        dp_ref[:, 0:ATTN_W] = dq_s[...]
        dp_ref[:, ATTN_W + 2 * KV_W:2 * ATTN_W + 2 * KV_W] = dza_s[...]
        dp_ref[:, 2 * ATTN_W + 2 * KV_W:IN_W - SSM_W] = du_ref[...]
        dp_ref[:, IN_W - SSM_W:IN_W] = dzs_ref[...]

        @pl.when(n == nb)
        def _():
            dkk_s[...] = jnp.zeros_like(dkk_s)
            dvv_s[...] = jnp.zeros_like(dvv_s)

        @pl.when(n < nb)
        def _():
            cos_q, sin_q, qr, kk, vt, qt = _attn_common(n, q_ref, kc_ref, kp_ref, vc_ref, vp_ref, rq_ref, rp_ref,
                                                        qw_ref, kw_ref)
            bias = jnp.concatenate([_group_bias(n)] * 4, axis=0)
            za = jnp.concatenate([za0_ref[...], za1_ref[...]], axis=1)
            doa_v = doa_ref[...]
            do_full = doa_v * _silu(za)
            k_all, v_all = jnp.concatenate(kk, axis=1), jnp.concatenate(vt, axis=1)
            q_b = (_stack_heads(qt) * QK_SCALE).astype(BF16)
            do_b = _stack_heads([do_full[:, 128 * i:128 * i + 128] for i in range(8)]).astype(BF16)
            sink16 = jnp.concatenate([_group_sinks(sink_ref, g) for g in range(N_KV_HEADS)], axis=0)
            p, psink = _group_softmax(q_b, k_all, sink16, bias)
            p_b = p.astype(BF16)
            dp = _dot_nt(do_b, v_all)
            delta = jnp.sum(p * dp, axis=1, keepdims=True)
            ds_b = (p * (dp - delta)).astype(BF16)
            sd = psink * delta
            gsink = jnp.zeros((1, 128), F32)
            lane = _lane128()
            for h in range(N_HEADS):
                gsink = gsink + jnp.where(lane == h, -jnp.sum(sd[BLOCK * h:BLOCK * (h + 1)]), 0.0)
            o_tiles = _unstack_heads(_dot(p_b, v_all))
            dq_tiles = [d * QK_SCALE for d in _unstack_heads(_dot(ds_b, k_all))]
            dkk = [_dot_tn(ds_b, q_b)]
            dvv = [_dot_tn(p_b, do_b)]
            dza_s[...] = (doa_v * jnp.concatenate(o_tiles, axis=1) * _dsilu(za)).astype(BF16)
            qv = q_ref[...]
            gq = jnp.zeros((1, 128), F32)
            out = []
            for i in range(8):
                d, dw = _prep_tile_bwd(qv[:, 128 * i:128 * i + 128], qw_ref[...], cos_q, sin_q, dq_tiles[i], qr[i])
                out.append(d)
                gq = gq + dw
            dq_s[...] = jnp.concatenate(out, axis=1).astype(BF16)
            gq_ref[...] += gq
            gs_ref[...] += gsink
            dkk_s[...] = jnp.concatenate(dkk, axis=1)
            dvv_s[...] = jnp.concatenate(dvv, axis=1)

        cos_p, sin_p = rp_ref[:, 0:128], rp_ref[:, 128:256]
        dk_prev = ck_s[...] + dkk_s[0:BLOCK, :]
        kp = kp_ref[...]
        gk = jnp.zeros((1, 128), F32)
        out = []
        for i in range(2):
            d, dw = _prep_tile_bwd(kp[:, 128 * i:128 * i + 128], kw_ref[...], cos_p, sin_p,
                                   dk_prev[:, 128 * i:128 * i + 128])
            out.append(d)
            gk = gk + dw
        dp_ref[:, ATTN_W:ATTN_W + KV_W] = jnp.concatenate(out, axis=1).astype(BF16)
        dp_ref[:, ATTN_W + KV_W:ATTN_W + 2 * KV_W] = (cv_s[...] + dvv_s[0:BLOCK, :]).astype(BF16)
        gk_ref[...] += gk
        ck_s[...] = dkk_s[BLOCK:2 * BLOCK, :]
        cv_s[...] = dvv_s[BLOCK:2 * BLOCK, :]

        @pl.when(n == nb)
        def _():
            wait()

    qblk = lambda n: (jnp.minimum(n, last), 0)
    kblk = lambda n: (jnp.maximum(n - 1, 0), 0)
    vec = pl.BlockSpec((1, 128), lambda n: (0, 0))
    return pl.pallas_call(
        body, name="attn_bwd", grid=(nb + 1,),
        in_specs=_attn_specs(nb) + [pl.BlockSpec((BLOCK, ATTN_W), qblk), pl.BlockSpec((BLOCK, SSM_W), kblk),
                                    pl.BlockSpec((BLOCK, SSM_W), kblk)] + [ANY] * nt,
        out_specs=[pl.BlockSpec((BLOCK, IN_W), kblk), vec, vec, vec] + [ANY] * nt,
        out_shape=[jax.ShapeDtypeStruct((t, IN_W), BF16), jax.ShapeDtypeStruct((1, 128), F32),
                   jax.ShapeDtypeStruct((1, 128), F32), jax.ShapeDtypeStruct((1, 128), F32)]
        + [jax.ShapeDtypeStruct(a.shape, a.dtype) for a in outgoing],
        scratch_shapes=[pltpu.VMEM((2 * BLOCK, KV_W), F32), pltpu.VMEM((2 * BLOCK, KV_W), F32),
                        pltpu.VMEM((BLOCK, KV_W), F32), pltpu.VMEM((BLOCK, KV_W), F32),
                        pltpu.VMEM((BLOCK, ATTN_W), BF16), pltpu.VMEM((BLOCK, ATTN_W), BF16),
                        pltpu.SemaphoreType.DMA((7 * nt,)), pltpu.SemaphoreType.DMA((7 * nt,))],
        compiler_params=_params(("arbitrary",)),
    )(proj, proj, proj, proj, proj, proj, proj, rope, rope, qw, kw, sinks, doa, du, dzs, *outgoing)


def _cmul(ar, ai, br, bi):
    return ar * br - ai * bi, ar * bi + ai * br


def _zoh(a_re, a_im, delta):
    e = jnp.exp(a_re * delta)
    lr, li = e * jnp.cos(a_im * delta), e * jnp.sin(a_im * delta)
    inv = 1.0 / (a_re * a_re + a_im * a_im)
    fr, fi = _cmul(lr - 1.0, li, a_re * inv, -a_im * inv)
    return lr, li, fr, fi


def _ssm_prep(a_re, a_im, log_step, b_re, b_im, seg_len):
    n_sq = int(round(math.log2(seg_len)))
    assert 2 ** n_sq == seg_len

    def body(ar_ref, ai_ref, ls_ref, arx_ref, aix_ref, br_ref, bi_ref, lr_ref, li_ref, pr_ref, pi_ref, bbr_ref, bbi_ref):
        delta = jnp.exp(ls_ref[...])
        lr, li, _, _ = _zoh(ar_ref[...], ai_ref[...], delta)
        lr_ref[...] = lr
        li_ref[...] = li
        pr, pi = lr, li
        for _ in range(n_sq):
            pr, pi = _cmul(pr, pi, pr, pi)
        pr_ref[...] = pr
        pi_ref[...] = pi
        _, _, fr, fi = _zoh(arx_ref[...], aix_ref[...], delta)
        bbr, bbi = _cmul(fr, fi, br_ref[...], bi_ref[...])
        bbr_ref[...] = bbr
        bbi_ref[...] = bbi

    gp = jax.ShapeDtypeStruct((SSM_G, SSM_P), F32)
    gx = jax.ShapeDtypeStruct((SSM_G, SSM_P * SSM_H), F32)
    return pl.pallas_call(body, name="ssm_prep", out_shape=[gp, gp, gp, gp, gx, gx])(
        a_re, a_im, log_step.reshape(SSM_G, 1), jnp.repeat(a_re, SSM_H, axis=1), jnp.repeat(a_im, SSM_H, axis=1),
        b_re.reshape(SSM_G, SSM_P * SSM_H), b_im.reshape(SSM_G, SSM_P * SSM_H))


def _ssm_param_grads(a_re, a_im, log_step, b_re, b_im, dlam_re, dlam_im, dbb_re, dbb_im):
    def body(ar_ref, ai_ref, ls_ref, arx_ref, aix_ref, br_ref, bi_ref, dlr_ref, dli_ref, dbr_ref, dbi_ref,
             gar_ref, gai_ref, gls_ref, gbr_ref, gbi_ref):
        delta = jnp.exp(ls_ref[...])
        ar, ai = ar_ref[...], ai_ref[...]
        lr, li, fr, fi = _zoh(ar, ai, delta)
        _, _, frx, fix = _zoh(arx_ref[...], aix_ref[...], delta)
        dbr, dbi = dbr_ref[...], dbi_ref[...]
        br, bi = br_ref[...], bi_ref[...]
        gbr, gbi = _cmul(frx, -fix, dbr, dbi)
        gbr_ref[...] = gbr
        gbi_ref[...] = gbi
        tr, ti = _cmul(br, -bi, dbr, dbi)
        row = lax.broadcasted_iota(jnp.int32, (SSM_P * SSM_H, SSM_P), 0)
        col = lax.broadcasted_iota(jnp.int32, (SSM_P * SSM_H, SSM_P), 1)
        fold = (row // SSM_H == col).astype(F32)
        dfr = jnp.dot(tr, fold, precision=lax.Precision.HIGHEST, preferred_element_type=F32)
        dfi = jnp.dot(ti, fold, precision=lax.Precision.HIGHEST, preferred_element_type=F32)
        inv = 1.0 / (ar * ar + ai * ai)
        ilr, ili = ar * inv, -ai * inv
        t1r, t1i = _cmul(dfr, dfi, ilr, -ili)
        dlbr, dlbi = dlr_ref[...] + t1r, dli_ref[...] + t1i
        qr, qi = _cmul(fr, fi, ilr, ili)
        t2r, t2i = _cmul(dfr, dfi, qr, -qi)
        glr, gli = -t2r, -t2i
        dzr, dzi = _cmul(dlbr, dlbi, lr, -li)
        gar_ref[...] = glr + dzr * delta
        gai_ref[...] = gli + dzi * delta
        gls_ref[...] = jnp.sum(dzr * ar + dzi * ai, axis=1, keepdims=True) * delta

    gp = jax.ShapeDtypeStruct((SSM_G, SSM_P), F32)
    gx = jax.ShapeDtypeStruct((SSM_G, SSM_P * SSM_H), F32)
    return pl.pallas_call(body, name="ssm_param_grads",
                          out_shape=[gp, gp, jax.ShapeDtypeStruct((SSM_G, 1), F32), gx, gx])(
        a_re, a_im, log_step.reshape(SSM_G, 1), jnp.repeat(a_re, SSM_H, axis=1), jnp.repeat(a_im, SSM_H, axis=1),
        b_re.reshape(SSM_G, SSM_P * SSM_H), b_im.reshape(SSM_G, SSM_P * SSM_H), dlam_re, dlam_im, dbb_re, dbb_im)


def _block_diag_in(bb):
    w = jnp.tile(bb.reshape(SSM_GB, SSM_ST, SSM_H), (1, 1, 16))
    row = lax.broadcasted_iota(jnp.int32, (1, SSM_ST, SSM_CH), 1) // SSM_P
    col = lax.broadcasted_iota(jnp.int32, (1, SSM_ST, SSM_CH), 2) // SSM_H
    return jnp.where(row == col, w, 0.0)


def _block_diag_out(c):
    w = jnp.tile(c.reshape(SSM_GB, SSM_CH, SSM_P), (1, 1, 16))
    row = lax.broadcasted_iota(jnp.int32, (1, SSM_CH, SSM_ST), 1) // SSM_H
    col = lax.broadcasted_iota(jnp.int32, (1, SSM_CH, SSM_ST), 2) // SSM_P
    return jnp.where(row == col, w, 0.0)


SEG_ROWS = SCAN_ROWS // N_SEG


def _chunk_perm():
    out_row = lax.broadcasted_iota(jnp.int32, (SCAN_ROWS, SCAN_ROWS), 0)
    in_row = lax.broadcasted_iota(jnp.int32, (SCAN_ROWS, SCAN_ROWS), 1)
    return (out_row == N_SEG * (in_row % SEG_ROWS) + in_row // SEG_ROWS).astype(BF16)


def _chunk_rows(j, seg_len, s):
    return pl.ds(pl.multiple_of(s * seg_len + j * SEG_ROWS, SEG_ROWS), SEG_ROWS)


def _gather_chunk(ref, j, seg_len):
    return jnp.concatenate([ref[_chunk_rows(j, seg_len, s), :] for s in range(N_SEG)], axis=0)


def _scatter_chunk(ref, j, seg_len, val):
    for s in range(N_SEG):
        ref[_chunk_rows(j, seg_len, s), :] = val[s * SEG_ROWS:(s + 1) * SEG_ROWS]


def _interleave(perm, x_b):
    return _dot(perm, x_b).astype(BF16)


def _state_pieces():
    return [pl.ds(part * SSM_ST + k * SCAN_LW, SCAN_LW) for k in range(SSM_ST // SCAN_LW) for part in range(2)]


def _expand_states(x_b, w_ref, dst_ref):
    for cols in _state_pieces():
        dst_ref[:, cols] = _dot_nt(x_b, w_ref[cols, :])


def _contract_states(src_ref, w_ref):
    acc = None
    for cols in _state_pieces():
        part = _dot_nt(src_ref[:, cols].astype(BF16), w_ref[:, cols])
        acc = part if acc is None else acc + part
    return acc


def _scan_fwd(src_ref, dst_ref, lam_r_ref, lam_i_ref, init_ref, final_ref, steps):
    for k in range(SSM_ST // SCAN_LW):
        re = pl.ds(k * SCAN_LW, SCAN_LW)
        im = pl.ds(SSM_ST + k * SCAN_LW, SCAN_LW)
        lr, li = lam_r_ref[:, re], lam_i_ref[:, re]

        def step(i, carry, re=re, im=im, lr=lr, li=li):
            hr, hi = carry
            rows = pl.ds(pl.multiple_of(i * 8, 8), 8)
            nr = lr * hr - li * hi + src_ref[rows, re]
            ni = lr * hi + li * hr + src_ref[rows, im]
            if dst_ref is not None:
                dst_ref[rows, re] = nr
                dst_ref[rows, im] = ni
            return nr, ni

        hr, hi = lax.fori_loop(0, steps, step, (init_ref[:, re], init_ref[:, im]), unroll=True)
        final_ref[:, re] = hr
        final_ref[:, im] = hi


def _ssm_specs(t):
    col = lambda g: (0, g)
    gb3 = lambda g: (g, 0, 0)
    return dict(
        rows=pl.BlockSpec((t, SSM_CH), col),
        lam=pl.BlockSpec((None, N_SEG, SSM_ST), gb3),
        w_in=pl.BlockSpec((None, 2 * SSM_ST, SSM_CH), gb3),
        w_out=pl.BlockSpec((None, SSM_CH, 2 * SSM_ST), gb3),
        vec=pl.BlockSpec((1, SSM_CH), col),
    )


def _segment_states(x_ref, pw_r_ref, pw_i_ref, out_ref, reverse):
    re, im = pl.ds(0, SSM_ST), pl.ds(SSM_ST, SSM_ST)
    pr, pi = pw_r_ref[0:1, :], pw_i_ref[0:1, :]
    first = N_SEG - 1 if reverse else 0
    out_ref[first:first + 1, :] = jnp.zeros((1, 2 * SSM_ST), F32)
    order = range(N_SEG - 1, 0, -1) if reverse else range(N_SEG - 1)
    for s in order:
        d = s - 1 if reverse else s + 1
        hr, hi = out_ref[s:s + 1, re], out_ref[s:s + 1, im]
        if reverse:
            nr, ni = pr * hr + pi * hi, pr * hi - pi * hr
        else:
            nr, ni = pr * hr - pi * hi, pr * hi + pi * hr
        out_ref[d:d + 1, re] = nr + x_ref[s:s + 1, re]
        out_ref[d:d + 1, im] = ni + x_ref[s:s + 1, im]


def _ssm_fwd(proj, lam_r, lam_i, pw_r, pw_i, w_in, w_out, d_skip):
    t = proj.shape[0]
    seg_len = t // N_SEG
    nch = t // SCAN_ROWS
    steps = SCAN_ROWS // N_SEG
    sp = _ssm_specs(t)

    def body(u_ref, lr_ref, li_ref, pr_ref, pi_ref, wi_ref, wo_ref, d_ref, y_ref, hc_ref, bu_s, car_s, seg_s, ub_s,
             y0_s, y1_s):
        perm = _chunk_perm()
        car_s[...] = jnp.zeros_like(car_s)

        def chunk1(j, c):
            rows = pl.ds(pl.multiple_of(j * SCAN_ROWS, SCAN_ROWS), SCAN_ROWS)
            u_b = _interleave(perm, _gather_chunk(u_ref, j, seg_len).astype(BF16))
            ub_s[rows, :] = u_b
            _expand_states(u_b, wi_ref, bu_s)
            _scan_fwd(bu_s, None, lr_ref, li_ref, car_s, car_s, steps)
            return c

        lax.fori_loop(0, nch, chunk1, 0)
        _segment_states(car_s, pr_ref, pi_ref, seg_s, reverse=False)
        car_s[...] = seg_s[...]

        def chunk2(j, c):
            rows = pl.ds(pl.multiple_of(j * SCAN_ROWS, SCAN_ROWS), SCAN_ROWS)
            _expand_states(ub_s[rows, :], wi_ref, bu_s)
            hc_ref[j] = car_s[...]
            _scan_fwd(bu_s, bu_s, lr_ref, li_ref, car_s, car_s, steps)
            yv = _contract_states(bu_s, wo_ref)
            y0_s[...] = yv[:, 0:128]
            y1_s[...] = yv[:, 128:256]
            for s in range(N_SEG):
                nat = _chunk_rows(j, seg_len, s)
                sub = pl.ds(s, SEG_ROWS, stride=N_SEG)
                y_ref[nat, :] = jnp.concatenate([y0_s[sub, :], y1_s[sub, :]], axis=1) + d_ref[...] * u_ref[nat, :]
            return c

        lax.fori_loop(0, nch, chunk2, 0)

    u_cols = 2560 // SSM_CH
    return pl.pallas_call(
        body, name="ssm_fwd", grid=(SSM_GB,),
        in_specs=[pl.BlockSpec((t, SSM_CH), lambda g: (0, u_cols + g)), sp["lam"], sp["lam"], sp["lam"], sp["lam"],
                  sp["w_in"], sp["w_out"], sp["vec"]],
        out_specs=[sp["rows"], pl.BlockSpec((None, nch, N_SEG, 2 * SSM_ST), lambda g: (g, 0, 0, 0))],
        out_shape=[jax.ShapeDtypeStruct((t, SSM_W), F32), jax.ShapeDtypeStruct((SSM_GB, nch, N_SEG, 2 * SSM_ST), F32)],
        scratch_shapes=[pltpu.VMEM((SCAN_ROWS, 2 * SSM_ST), F32), pltpu.VMEM((N_SEG, 2 * SSM_ST), F32),
                        pltpu.VMEM((N_SEG, 2 * SSM_ST), F32), pltpu.VMEM((t, SSM_CH), BF16),
                        pltpu.VMEM((SCAN_ROWS, 128), F32), pltpu.VMEM((SCAN_ROWS, 128), F32)],
        compiler_params=_params(("parallel",)),
    )(proj, lam_r, lam_i, pw_r, pw_i, w_in, w_out, d_skip)


def _group_blocks(full):
    row_g = lax.broadcasted_iota(jnp.int32, (SSM_CH, SSM_ST), 0) // SSM_H
    col_g = lax.broadcasted_iota(jnp.int32, (SSM_CH, SSM_ST), 1) // SSM_P
    fold = (lax.broadcasted_iota(jnp.int32, (SSM_ST, SSM_P), 0) % SSM_P
            == lax.broadcasted_iota(jnp.int32, (SSM_ST, SSM_P), 1)).astype(F32)
    parts = [jnp.dot(jnp.where(row_g == col_g, full[:, k * SSM_ST:(k + 1) * SSM_ST], 0.0), fold,
                     precision=lax.Precision.HIGHEST, preferred_element_type=F32) for k in range(2)]
    return jnp.concatenate(parts, axis=1)


def _ssm_bwd(proj, dy, hc, lam_r, lam_i, pw_r, pw_i, w_in, w_out, d_skip):
    t = proj.shape[0]
    seg_len = t // N_SEG
    nch = t // SCAN_ROWS
    steps = SCAN_ROWS // N_SEG
    sp = _ssm_specs(t)

    def body(u_ref, dy_ref, hc_ref, lr_ref, li_ref, pr_ref, pi_ref, wi_ref, wo_ref, d_ref,
             du_ref, gbi_ref, gbo_ref, glam_ref, gd_ref, bu_s, h_s, e_s, car_s, seg_s, acc_s, gwi_ref, gwo_ref,
             dyb_s):
        perm = _chunk_perm()

        def chunk_rows(j):
            return pl.ds(pl.multiple_of(j * SCAN_ROWS, SCAN_ROWS), SCAN_ROWS)

        def interleaved(ref, j):
            return _interleave(perm, _gather_chunk(ref, j, seg_len).astype(BF16))

        def load_e(j):
            dy_b = interleaved(dy_ref, j)
            dyb_s[chunk_rows(j), :] = dy_b
            for cols in _state_pieces():
                e_s[:, cols] = _dot(dy_b, wo_ref[:, cols])

        def scan_rev(j, accumulate):
            for k in range(SSM_ST // SCAN_LW):
                re = pl.ds(k * SCAN_LW, SCAN_LW)
                im = pl.ds(SSM_ST + k * SCAN_LW, SCAN_LW)
                lr, li = lr_ref[:, re], li_ref[:, re]

                def step(ii, carry, re=re, im=im, lr=lr, li=li):
                    i = steps - 1 - ii
                    rows = pl.ds(pl.multiple_of(i * 8, 8), 8)
                    if accumulate:
                        gr, gi, ar, ai = carry
                    else:
                        gr, gi = carry
                    nr = lr * gr + li * gi + e_s[rows, re]
                    ni = lr * gi - li * gr + e_s[rows, im]
                    if not accumulate:
                        return nr, ni
                    e_s[rows, re] = nr
                    e_s[rows, im] = ni
                    pr_, pi_ = h_s[rows, re], h_s[rows, im]
                    return nr, ni, ar + nr * pr_ + ni * pi_, ai + ni * pr_ - nr * pi_

                init = (car_s[:, re], car_s[:, im])
                if accumulate:
                    init = init + (acc_s[:, re], acc_s[:, im])
                out = lax.fori_loop(0, steps, step, init, unroll=True)
                car_s[:, re] = out[0]
                car_s[:, im] = out[1]
                if accumulate:
                    acc_s[:, re] = out[2]
                    acc_s[:, im] = out[3]

        car_s[...] = jnp.zeros_like(car_s)

        def pass1(jj, c):
            load_e(nch - 1 - jj)
            scan_rev(nch - 1 - jj, False)
            return c

        lax.fori_loop(0, nch, pass1, 0)
        _segment_states(car_s, pr_ref, pi_ref, seg_s, reverse=True)
        car_s[...] = seg_s[...]
        acc_s[...] = jnp.zeros_like(acc_s)
        gwi_ref[...] = jnp.zeros_like(gwi_ref)
        gwo_ref[...] = jnp.zeros_like(gwo_ref)
        gd_ref[...] = jnp.zeros_like(gd_ref)

        def pass2(jj, c):
            j = nch - 1 - jj
            u_b, dy_b = interleaved(u_ref, j), dyb_s[chunk_rows(j), :]
            _expand_states(u_b, wi_ref, bu_s)
            h_s[0:N_SEG, :] = hc_ref[j]
            seg_s[...] = hc_ref[j]
            h_now = h_s.at[pl.ds(N_SEG, SCAN_ROWS), :]
            _scan_fwd(bu_s, h_now, lr_ref, li_ref, seg_s, seg_s, steps)
            for cols in _state_pieces():
                e_s[:, cols] = _dot(dy_b, wo_ref[:, cols])
            scan_rev(j, True)
            du = d_ref[...] * dy_b.astype(F32)
            for cols in _state_pieces():
                g_b = e_s[:, cols].astype(BF16)
                du = du + _dot(g_b, wi_ref[cols, :])
                gwi_ref[:, cols] += _dot_tn(u_b, g_b)
                gwo_ref[:, cols] += _dot_tn(dy_b, h_now[:, cols].astype(BF16))
            _scatter_chunk(du_ref, j, seg_len, _dot_tn(perm, du.astype(BF16)).astype(du_ref.dtype))
            gd_ref[...] += jnp.sum(_gather_chunk(dy_ref, j, seg_len) * _gather_chunk(u_ref, j, seg_len), axis=0,
                                   keepdims=True)
            return c

        lax.fori_loop(0, nch, pass2, 0)
        glam_ref[...] = jnp.sum(acc_s[...], axis=0, keepdims=True)
        gbi_ref[...] = _group_blocks(gwi_ref[...])
        gbo_ref[...] = _group_blocks(gwo_ref[...])

    mat = pl.BlockSpec((None, SSM_CH, 2 * SSM_P), lambda g: (g, 0, 0))
    u_cols = 2560 // SSM_CH
    return pl.pallas_call(
        body, name="ssm_bwd", grid=(SSM_GB,),
        in_specs=[pl.BlockSpec((t, SSM_CH), lambda g: (0, u_cols + g)), sp["rows"],
                  pl.BlockSpec((None, nch, N_SEG, 2 * SSM_ST), lambda g: (g, 0, 0, 0)),
                  sp["lam"], sp["lam"], sp["lam"], sp["lam"], sp["w_in"], sp["w_out"], sp["vec"]],
        out_specs=[sp["rows"], mat, mat, pl.BlockSpec((None, 1, 2 * SSM_ST), lambda g: (g, 0, 0)), sp["vec"]],
        out_shape=[jax.ShapeDtypeStruct((t, SSM_W), BF16), jax.ShapeDtypeStruct((SSM_GB, SSM_CH, 2 * SSM_P), F32),
                   jax.ShapeDtypeStruct((SSM_GB, SSM_CH, 2 * SSM_P), F32),
                   jax.ShapeDtypeStruct((SSM_GB, 1, 2 * SSM_ST), F32), jax.ShapeDtypeStruct((1, SSM_W), F32)],
        scratch_shapes=[pltpu.VMEM((SCAN_ROWS, 2 * SSM_ST), F32), pltpu.VMEM((SCAN_ROWS + N_SEG, 2 * SSM_ST), F32),
                        pltpu.VMEM((SCAN_ROWS, 2 * SSM_ST), F32), pltpu.VMEM((N_SEG, 2 * SSM_ST), F32),
                        pltpu.VMEM((N_SEG, 2 * SSM_ST), F32), pltpu.VMEM((N_SEG, 2 * SSM_ST), F32),
                        pltpu.VMEM((SSM_CH, 2 * SSM_ST), F32), pltpu.VMEM((SSM_CH, 2 * SSM_ST), F32),
                        pltpu.VMEM((t, SSM_CH), BF16)],
        compiler_params=_params(("parallel",)),
    )(proj, dy, hc, lam_r, lam_i, pw_r, pw_i, w_in, w_out, d_skip)


def _z_ssm_specs(tm):
    return [pl.BlockSpec((tm, 512), lambda i: (i, 7)), pl.BlockSpec((tm, 512), lambda i: (i, 8))]


def _glu_fwd(y, proj, w_glu, b_glu):
    t = y.shape[0]
    tm = 512

    def body(y_ref, z0_ref, z1_ref, w_ref, b_ref, o_ref, yg_ref):
        yg = _gelu(y_ref[...])
        yg_b = yg.astype(BF16)
        a = _dot(yg_b, w_ref[...]) + b_ref[...]
        z = jnp.concatenate([z0_ref[...], z1_ref[...]], axis=1)
        o_ref[...] = yg * _sigmoid(a) * _silu(z)
        yg_ref[...] = yg_b

    row = pl.BlockSpec((tm, SSM_W), lambda i: (i, 0))
    return pl.pallas_call(
        body, name="glu_fwd", grid=(t // tm,),
        in_specs=[row] + _z_ssm_specs(tm) + [pl.BlockSpec((SSM_W, SSM_W), lambda i: (0, 0)),
                                            pl.BlockSpec((1, SSM_W), lambda i: (0, 0))],
        out_specs=[row, row],
        out_shape=[jax.ShapeDtypeStruct((t, SSM_W), F32), jax.ShapeDtypeStruct((t, SSM_W), BF16)],
        compiler_params=_params(("parallel",)),
    )(y, proj, proj, w_glu, b_glu)


def _glu_bwd(y, proj, dos, w_glu, b_glu):
    t = y.shape[0]
    tm = 512

    def body(y_ref, z0_ref, z1_ref, do_ref, w_ref, b_ref, dy_ref, dz_ref, da_ref, gb_ref):
        @pl.when(pl.program_id(0) == 0)
        def _():
            gb_ref[...] = jnp.zeros_like(gb_ref)

        z = jnp.concatenate([z0_ref[...], z1_ref[...]], axis=1)
        yv, do = y_ref[...], do_ref[...]
        yg = _gelu(yv)
        sg = _sigmoid(_dot(yg.astype(BF16), w_ref[...]) + b_ref[...])
        dy2 = do * _silu(z)
        dz_ref[...] = (do * yg * sg * _dsilu(z)).astype(BF16)
        da = dy2 * yg * sg * (1.0 - sg)
        da_b = da.astype(BF16)
        da_ref[...] = da_b
        gb_ref[...] += jnp.sum(da, axis=0, keepdims=True)
        dyg = dy2 * sg + _dot_nt(da_b, w_ref[...])
        dy_ref[...] = dyg * _dgelu(yv)

    row = pl.BlockSpec((tm, SSM_W), lambda i: (i, 0))
    vec = pl.BlockSpec((1, SSM_W), lambda i: (0, 0))
    return pl.pallas_call(
        body, name="glu_bwd", grid=(t // tm,),
        in_specs=[row] + _z_ssm_specs(tm) + [row, pl.BlockSpec((SSM_W, SSM_W), lambda i: (0, 0)), vec],
        out_specs=[row, row, row, vec],
        out_shape=[jax.ShapeDtypeStruct((t, SSM_W), F32), jax.ShapeDtypeStruct((t, SSM_W), BF16),
                   jax.ShapeDtypeStruct((t, SSM_W), BF16), jax.ShapeDtypeStruct((1, SSM_W), F32)],
        compiler_params=_params(("arbitrary",)),
    )(y, proj, proj, dos, w_glu, b_glu)


def _rms(o):
    return lax.rsqrt(jnp.mean(o * o, axis=1, keepdims=True) + NORM_EPS)


def _outproj(oa, os_, aw, sw, w_out, x, target):
    t = x.shape[0]
    tm = 256

    def body(oa_ref, os_ref, aw_ref, sw_ref, w_ref, x_ref, t_ref, mg_ref, do_ref, ls_ref):
        @pl.when(pl.program_id(0) == 0)
        def _():
            ls_ref[...] = jnp.zeros_like(ls_ref)

        a, s = oa_ref[...], os_ref[...]
        merged = jnp.concatenate([a * _rms(a) * aw_ref[...], s * _rms(s) * sw_ref[...]], axis=1).astype(BF16)
        mg_ref[...] = merged
        err = x_ref[...] + _dot(merged, w_ref[...]) - t_ref[...]
        do_ref[...] = err * (1.0 / D_MODEL)
        ls_ref[...] += jnp.sum(err * err)

    half = pl.BlockSpec((tm, ATTN_W), lambda i: (i, 0))
    full = pl.BlockSpec((tm, D_MODEL), lambda i: (i, 0))
    vec = pl.BlockSpec((1, ATTN_W), lambda i: (0, 0))
    return pl.pallas_call(
        body, name="outproj", grid=(t // tm,),
        in_specs=[half, half, vec, vec, pl.BlockSpec((D_MODEL, D_MODEL), lambda i: (0, 0)), full, full],
        out_specs=[full, full, pl.BlockSpec((8, 128), lambda i: (0, 0))],
        out_shape=[jax.ShapeDtypeStruct((t, D_MODEL), BF16), jax.ShapeDtypeStruct((t, D_MODEL), F32),
                   jax.ShapeDtypeStruct((8, 128), F32)],
        compiler_params=_params(("arbitrary",)),
    )(oa, os_, aw, sw, w_out, x, target)


def _outproj_bwd(dout, oa, os_, aw, sw, w_out):
    t = dout.shape[0]
    tm = 256

    def norm_bwd(o, w, dm):
        r = _rms(o)
        yh = o * r
        gh = dm * w
        return r * (gh - yh * jnp.mean(gh * yh, axis=1, keepdims=True)), jnp.sum(dm * yh, axis=0, keepdims=True)

    def body(do_ref, oa_ref, os_ref, aw_ref, sw_ref, w_ref, da_ref, ds_ref, ga_ref, gs_ref):
        @pl.when(pl.program_id(0) == 0)
        def _():
            ga_ref[...] = jnp.zeros_like(ga_ref)
            gs_ref[...] = jnp.zeros_like(gs_ref)

        dm = _dot_nt(do_ref[...].astype(BF16), w_ref[...])
        da, ga = norm_bwd(oa_ref[...], aw_ref[...], dm[:, :ATTN_W])
        ds, gs = norm_bwd(os_ref[...], sw_ref[...], dm[:, ATTN_W:])
        da_ref[...] = da
        ds_ref[...] = ds
        ga_ref[...] += ga
        gs_ref[...] += gs

    half = pl.BlockSpec((tm, ATTN_W), lambda i: (i, 0))
    full = pl.BlockSpec((tm, D_MODEL), lambda i: (i, 0))
    vec = pl.BlockSpec((1, ATTN_W), lambda i: (0, 0))
    return pl.pallas_call(
        body, name="outproj_bwd", grid=(t // tm,),
        in_specs=[full, half, half, vec, vec, pl.BlockSpec((D_MODEL, D_MODEL), lambda i: (0, 0))],
        out_specs=[half, half, vec, vec],
        out_shape=[jax.ShapeDtypeStruct((t, ATTN_W), F32), jax.ShapeDtypeStruct((t, ATTN_W), F32),
                   jax.ShapeDtypeStruct((1, ATTN_W), F32), jax.ShapeDtypeStruct((1, ATTN_W), F32)],
        compiler_params=_params(("arbitrary",)),
    )(dout, oa, os_, aw, sw, w_out)


def _inproj_bwd(dproj, w_slabs, x, norm_w, dout, outgoing):
    t = x.shape[0]
    tm = 512
    nc = 4
    nt = len(outgoing)
    ni = t // tm

    def body(dp_ref, w_ref, x_ref, nw_ref, do_ref, *rest):
        src, (gx_ref, gw_ref), dst = rest[:nt], rest[nt:nt + 2], rest[nt + 2:2 * nt + 2]
        acc_ref, ssem, rsem = rest[2 * nt + 2:]
        i, j = pl.program_id(0), pl.program_id(1)
        start, wait = _bg_scatter_chips(src, dst, ssem, rsem)

        @pl.when((i == 0) & (j == 0))
        def _():
            start()
            gw_ref[...] = jnp.zeros_like(gw_ref)

        @pl.when(j == 0)
        def _():
            acc_ref[...] = jnp.zeros_like(acc_ref)

        acc_ref[...] += _dot_nt(dp_ref[...], w_ref[...])

        @pl.when(j == nc - 1)
        def _():
            xv = x_ref[...]
            r = lax.rsqrt(jnp.mean(xv * xv, axis=1, keepdims=True) + NORM_EPS)
            yh = xv * r
            dh = acc_ref[...]
            gh = dh * nw_ref[...]
            gx_ref[...] = do_ref[...] + r * (gh - yh * jnp.mean(gh * yh, axis=1, keepdims=True))
            gw_ref[...] += jnp.sum(dh * yh, axis=0, keepdims=True)

        @pl.when((i == ni - 1) & (j == nc - 1))
        def _():
            wait()

    full = pl.BlockSpec((tm, D_MODEL), lambda i, j: (i, 0))
    vec = pl.BlockSpec((1, D_MODEL), lambda i, j: (0, 0))
    return pl.pallas_call(
        body, name="inproj_bwd", grid=(ni, nc),
        in_specs=[pl.BlockSpec((tm, SHARD_W), lambda i, j: (i, j)),
                  pl.BlockSpec((None, D_MODEL, SHARD_W), lambda i, j: (j, 0, 0)), full, vec, full] + [ANY] * nt,
        out_specs=[full, vec] + [ANY] * nt,
        out_shape=[jax.ShapeDtypeStruct((t, D_MODEL), F32), jax.ShapeDtypeStruct((1, D_MODEL), F32)]
        + [jax.ShapeDtypeStruct(a.shape, a.dtype) for a in outgoing],
        scratch_shapes=[pltpu.VMEM((tm, D_MODEL), F32), pltpu.SemaphoreType.DMA((3 * nt,)),
                        pltpu.SemaphoreType.DMA((3 * nt,))],
        compiler_params=_params(("arbitrary", "arbitrary")),
    )(dproj, w_slabs, x, norm_w.reshape(1, D_MODEL), dout, *outgoing)


def _adamw_math(w_ref, g_ref, m_ref, v_ref, d_ref, nm_ref, nv_ref):
    gv = g_ref[...]
    nm = ADAM_B1 * m_ref[...] + (1.0 - ADAM_B1) * gv
    nv = ADAM_B2 * v_ref[...] + (1.0 - ADAM_B2) * (gv * gv)
    m_hat = nm / (1.0 - ADAM_B1 ** ADAM_STEP)
    v_hat = nv / (1.0 - ADAM_B2 ** ADAM_STEP)
    d_ref[...] = -ADAM_LR * (m_hat / (jnp.sqrt(v_hat) + ADAM_EPS) + ADAM_WD * w_ref[...])
    nm_ref[...] = nm
    nv_ref[...] = nv


def _adamw_halves(w, mine, theirs, m, v, c_idx, *, rows, name):
    hr, cols = mine.shape
    nblk = hr // rows

    def body(c_ref, w_ref, a_ref, b_ref, m_ref, v_ref, g_ref, d_ref, nm_ref, nv_ref):
        g_ref[...] = jnp.where(pl.program_id(0) == c_ref[0], a_ref[...], b_ref[...])
        _adamw_math(w_ref, g_ref, m_ref, v_ref, d_ref, nm_ref, nv_ref)

    full = pl.BlockSpec((rows, cols), lambda h, i, c: (h * nblk + i, 0))
    part = pl.BlockSpec((rows, cols), lambda h, i, c: (i, 0))
    shp = jax.ShapeDtypeStruct((2 * hr, cols), F32)
    return pl.pallas_call(
        body, name=name,
        grid_spec=pltpu.PrefetchScalarGridSpec(num_scalar_prefetch=1, grid=(2, nblk),
                                               in_specs=[full, part, part, full, full], out_specs=[full] * 4),
        out_shape=[shp] * 4, compiler_params=_params(("parallel", "parallel")),
    )(c_idx, w, mine, theirs, m, v)


def _adamw(w, g, m, v, *, rows, name):
    r, c = w.shape

    def body(w_ref, g_ref, m_ref, v_ref, d_ref, nm_ref, nv_ref):
        _adamw_math(w_ref, g_ref, m_ref, v_ref, d_ref, nm_ref, nv_ref)

    blk = pl.BlockSpec((rows, c), lambda i: (i, 0))
    shp = jax.ShapeDtypeStruct((r, c), F32)
    return pl.pallas_call(body, name=name, grid=(r // rows,), in_specs=[blk] * 4, out_specs=[blk] * 3,
                          out_shape=[shp] * 3, compiler_params=_params(("parallel",)))(w, g, m, v)


def _remote(src, dst, ssem, rsem, dev):
    return pltpu.make_async_remote_copy(src_ref=src, dst_ref=dst, send_sem=ssem, recv_sem=rsem, device_id=dev,
                                        device_id_type=pl.DeviceIdType.MESH)


def _mesh_pos():
    return lax.axis_index("x"), lax.axis_index("y"), lax.axis_index("c")


def _other_chips(x, y):
    return [(1 - x, y), (x, 1 - y), (1 - x, 1 - y)]


def _flips():
    return [(dx, dy, dc) for dx in (0, 1) for dy in (0, 1) for dc in (0, 1) if (dx, dy, dc) != (0, 0, 0)]


def _background(sends, arrivals):
    def start():
        for cp in sends():
            cp.start()

    def wait():
        for cp in arrivals():
            cp.wait_recv()
        for cp in sends():
            cp.wait_send()

    return start, wait


def _bg_gather(sh, full, ssem, rsem):
    x, y, c = _mesh_pos()
    me = 2 * x + y
    peers = [(px, py, c) for px, py in _other_chips(x, y)] + [(x, y, 1 - c)]
    slots = [2 * px + py for px, py in _other_chips(x, y)] + [me]
    pairs = [(i, k) for i in range(len(sh)) for k in range(4)]
    return _background(
        lambda: [_remote(sh[i], full[i].at[me], ssem.at[4 * i + k], rsem.at[4 * i + k], peers[k]) for i, k in pairs],
        lambda: [_remote(full[i].at[slots[k]], full[i].at[slots[k]], ssem.at[4 * i + k], rsem.at[4 * i + k], peers[k])
                 for i, k in pairs])


def _bg_scatter_devices(src, dst, ssem, rsem):
    x, y, c = _mesh_pos()
    me = 4 * x + 2 * y + c
    peers = []
    for dx, dy, dc in _flips():
        px, py, pc = jnp.bitwise_xor(x, dx), jnp.bitwise_xor(y, dy), jnp.bitwise_xor(c, dc)
        peers.append(((px, py, pc), 4 * px + 2 * py + pc))
    pairs = [(i, k) for i in range(len(src)) for k in range(7)]
    return _background(
        lambda: [_remote(src[i].at[peers[k][1]], dst[i].at[me], ssem.at[7 * i + k], rsem.at[7 * i + k], peers[k][0])
                 for i, k in pairs],
        lambda: [_remote(dst[i].at[peers[k][1]], dst[i].at[peers[k][1]], ssem.at[7 * i + k], rsem.at[7 * i + k],
                         peers[k][0]) for i, k in pairs])


def _bg_scatter_chips(src, dst, ssem, rsem):
    x, y, c = _mesh_pos()
    me = 2 * x + y
    chips = _other_chips(x, y)
    pairs = [(i, k) for i in range(len(src)) for k in range(3)]
    slot = lambda k: 2 * chips[k][0] + chips[k][1]
    return _background(
        lambda: [_remote(src[i].at[slot(k)], dst[i].at[me], ssem.at[3 * i + k], rsem.at[3 * i + k], (*chips[k], c))
                 for i, k in pairs],
        lambda: [_remote(dst[i].at[slot(k)], dst[i].at[slot(k)], ssem.at[3 * i + k], rsem.at[3 * i + k], (*chips[k], c))
                 for i, k in pairs])


def _pair_swap(arrays):
    nt = len(arrays)

    def body(*refs):
        src, dst = refs[:nt], refs[nt:2 * nt]
        ssem, rsem = refs[2 * nt:]
        x, y, c = _mesh_pos()
        cps = [_remote(src[i].at[:, 1 - c], dst[i], ssem.at[i], rsem.at[i], (x, y, 1 - c)) for i in range(nt)]
        for cp in cps:
            cp.start()
        for cp in cps:
            cp.wait_recv()
        for cp in cps:
            cp.wait_send()

    return pl.pallas_call(
        body, name="pair_swap", in_specs=[ANY] * nt, out_specs=[ANY] * nt,
        out_shape=[jax.ShapeDtypeStruct((4,) + a.shape[2:], a.dtype) for a in arrays],
        scratch_shapes=[pltpu.SemaphoreType.DMA((nt,)), pltpu.SemaphoreType.DMA((nt,))],
    )(*arrays)


def _half_swap(arrays):
    nt = len(arrays)

    def body(*refs):
        src, dst = refs[:nt], refs[nt:2 * nt]
        ssem, rsem = refs[2 * nt:]
        x, y, c = _mesh_pos()
        cps = [_remote(src[i], dst[i], ssem.at[i], rsem.at[i], (x, y, 1 - c)) for i in range(nt)]
        for cp in cps:
            cp.start()
        for cp in cps:
            cp.wait_recv()
        for cp in cps:
            cp.wait_send()

    return pl.pallas_call(
        body, name="half_swap", in_specs=[ANY] * nt, out_specs=[ANY] * nt,
        out_shape=[jax.ShapeDtypeStruct(a.shape, a.dtype) for a in arrays],
        scratch_shapes=[pltpu.SemaphoreType.DMA((nt,)), pltpu.SemaphoreType.DMA((nt,))],
    )(*arrays)


def _exchange_slices(src, scatter, name):
    def body(src_ref, dst_ref, ssem, rsem, lsem):
        x, y, c = _mesh_pos()
        me = 4 * x + 2 * y + c
        local = pltpu.make_async_copy(src_ref.at[me] if scatter else src_ref, dst_ref.at[me], lsem)
        local.start()
        cps = []
        for k, (dx, dy, dc) in enumerate(_flips()):
            px, py, pc = jnp.bitwise_xor(x, dx), jnp.bitwise_xor(y, dy), jnp.bitwise_xor(c, dc)
            peer = 4 * px + 2 * py + pc
            cp = _remote(src_ref.at[peer] if scatter else src_ref, dst_ref.at[me], ssem.at[k], rsem.at[k],
                         (px, py, pc))
            cp.start()
            cps.append((cp, peer))
        for k, (cp, peer) in enumerate(cps):
            slot = dst_ref.at[peer]
            _remote(slot, slot, ssem.at[k], rsem.at[k], (x, y, c)).wait_recv()
        for cp, _ in cps:
            cp.wait_send()
        local.wait()

    return pl.pallas_call(
        body, name=name, in_specs=[ANY], out_specs=ANY,
        out_shape=jax.ShapeDtypeStruct((8,) + src.shape[-2:], src.dtype),
        scratch_shapes=[pltpu.SemaphoreType.DMA((7,)), pltpu.SemaphoreType.DMA((7,)), pltpu.SemaphoreType.DMA],
    )(src)


def _add_halves(g, recv, c_idx, *, rows, name):
    _, _, hr, cols = g.shape

    def body(c_ref, g_ref, r_ref, o_ref):
        o_ref[...] = (g_ref[...] + r_ref[...].astype(F32)).astype(BF16)

    return pl.pallas_call(
        body, name=name,
        grid_spec=pltpu.PrefetchScalarGridSpec(
            num_scalar_prefetch=1, grid=(4, hr // rows),
            in_specs=[pl.BlockSpec((None, None, rows, cols), lambda j, i, c: (j, c[0], i, 0)),
                      pl.BlockSpec((None, rows, cols), lambda j, i, c: (j, i, 0))],
            out_specs=pl.BlockSpec((None, rows, cols), lambda j, i, c: (j, i, 0))),
        out_shape=jax.ShapeDtypeStruct((4, hr, cols), BF16),
        compiler_params=_params(("parallel", "parallel")),
    )(c_idx, g, recv)


def _sum_peers(slots, own, idx, *, rows, name):
    n, r, cols = slots.shape

    def body(me_ref, *refs):
        me = me_ref[0]
        mine = refs[n][...].astype(F32)
        acc = None
        for k in range(n):
            term = jnp.where(me == k, mine, refs[k][...].astype(F32))
            acc = term if acc is None else acc + term
        refs[n + 1][...] = acc

    def slot_spec(k):
        return pl.BlockSpec((None, rows, cols), lambda i, me: (jnp.where(me[0] == k, (k + 1) % n, k), i, 0))

    return pl.pallas_call(
        body, name=name,
        grid_spec=pltpu.PrefetchScalarGridSpec(
            num_scalar_prefetch=1, grid=(r // rows,),
            in_specs=[slot_spec(k) for k in range(n)] + [pl.BlockSpec((None, rows, cols), lambda i, me: (me[0], i, 0))],
            out_specs=pl.BlockSpec((rows, cols), lambda i, me: (i, 0))),
        out_shape=jax.ShapeDtypeStruct((r, cols), F32),
        compiler_params=_params(("parallel",)),
    )(idx, *([slots] * n), own)


def _sum_slots(slots, *, rows, name):
    n, r, cols = slots.shape

    def body(s_ref, o_ref):
        acc = s_ref[0].astype(F32)
        for k in range(1, n):
            acc = acc + s_ref[k].astype(F32)
        o_ref[...] = acc

    return pl.pallas_call(
        body, name=name, grid=(r // rows,),
        in_specs=[pl.BlockSpec((n, rows, cols), lambda i: (0, i, 0))],
        out_specs=pl.BlockSpec((rows, cols), lambda i: (i, 0)),
        out_shape=jax.ShapeDtypeStruct((r, cols), F32),
        compiler_params=_params(("parallel",)),
    )(slots)


def _pack_small(d, names, rows):
    flat = jnp.concatenate([d[n].astype(F32).reshape(-1) for n in names])
    return jnp.pad(flat, (0, rows * 128 - flat.shape[0])).reshape(rows, 128)


def _unpack_small(p, names):
    flat = p.reshape(-1)
    out, off = {}, 0
    for n in names:
        size = math.prod(SMALL_SHAPES[n])
        out[n] = flat[off:off + size].reshape(SMALL_SHAPES[n])
        off += size
    return out


def _adamw_3d(w, g, m, v, *, name):
    def body(w_ref, g_ref, m_ref, v_ref, d_ref, nm_ref, nv_ref):
        _adamw_math(w_ref, g_ref, m_ref, v_ref, d_ref, nm_ref, nv_ref)

    blk = pl.BlockSpec((8,) + w.shape[1:], lambda i: (i, 0, 0))
    shp = jax.ShapeDtypeStruct(w.shape, F32)
    return pl.pallas_call(body, name=name, grid=(w.shape[0] // 8,), in_specs=[blk] * 4, out_specs=[blk] * 3,
                          out_shape=[shp] * 3, compiler_params=_params(("parallel",)))(w, g, m, v)


def kernel(x, positions, norm_w, w_in, q_norm_w, k_norm_w, sinks, a_re, a_im, log_step, b_re, b_im, c_re, c_im, d_skip, w_glu, b_glu, attn_out_norm_w, ssm_out_norm_w, w_out, loss_target, m_norm_w, m_w_in, m_q_norm_w, m_k_norm_w, m_sinks, m_a_re, m_a_im, m_log_step, m_b_re, m_b_im, m_c_re, m_c_im, m_d_skip, m_w_glu, m_b_glu, m_attn_out_norm_w, m_ssm_out_norm_w, m_w_out, v_norm_w, v_w_in, v_q_norm_w, v_k_norm_w, v_sinks, v_a_re, v_a_im, v_log_step, v_b_re, v_b_im, v_c_re, v_c_im, v_d_skip, v_w_glu, v_b_glu, v_attn_out_norm_w, v_ssm_out_norm_w, v_w_out):
    small_w = dict(norm_w=norm_w, q_norm_w=q_norm_w, k_norm_w=k_norm_w, sinks=sinks, a_re=a_re, a_im=a_im,
                   log_step=log_step, b_re=b_re, b_im=b_im, c_re=c_re, c_im=c_im, d_skip=d_skip, b_glu=b_glu,
                   attn_out_norm_w=attn_out_norm_w, ssm_out_norm_w=ssm_out_norm_w)
    small_m = dict(norm_w=m_norm_w, q_norm_w=m_q_norm_w, k_norm_w=m_k_norm_w, sinks=m_sinks, a_re=m_a_re, a_im=m_a_im,
                   log_step=m_log_step, b_re=m_b_re, b_im=m_b_im, c_re=m_c_re, c_im=m_c_im, d_skip=m_d_skip,
                   b_glu=m_b_glu, attn_out_norm_w=m_attn_out_norm_w, ssm_out_norm_w=m_ssm_out_norm_w)
    small_v = dict(norm_w=v_norm_w, q_norm_w=v_q_norm_w, k_norm_w=v_k_norm_w, sinks=v_sinks, a_re=v_a_re, a_im=v_a_im,
                   log_step=v_log_step, b_re=v_b_re, b_im=v_b_im, c_re=v_c_re, c_im=v_c_im, d_skip=v_d_skip,
                   b_glu=v_b_glu, attn_out_norm_w=v_attn_out_norm_w, ssm_out_norm_w=v_ssm_out_norm_w)
    c_idx = lax.axis_index("c").astype(jnp.int32).reshape(1)
    chip_idx = (2 * lax.axis_index("x") + lax.axis_index("y")).astype(jnp.int32).reshape(1)
    dev_idx = 2 * chip_idx + c_idx

    xs = x[0]
    tgt = loss_target[0]
    t = xs.shape[0]
    posf = positions[0].astype(F32).reshape(t, 1)

    mx, my = lax.axis_index("x"), lax.axis_index("y")
    slab_order = jnp.stack([2 * mx + my, 2 * (1 - mx) + my, 2 * mx + (1 - my), 2 * (1 - mx) + (1 - my)]).astype(jnp.int32)
    proj, hn, w_in_all = _inproj(xs, norm_w, w_in.astype(BF16), slab_order)
    inv_freq = ROPE_THETA ** (-jnp.arange(0, HEAD_DIM, 2, dtype=F32) / HEAD_DIM)
    rope = _rope_table(posf, jnp.tile(inv_freq, 4).reshape(1, 128))
    qw = jnp.tile(q_norm_w, 2).reshape(1, 128)
    kw = jnp.tile(k_norm_w, 2).reshape(1, 128)
    sink_row = sinks.reshape(1, N_HEADS)
    oa, w_glu_all, w_out_all = _attn_fwd(proj, rope, qw, kw, sink_row, [w_glu.astype(BF16), w_out.astype(BF16)])
    w_glu_b = w_glu_all.reshape(SSM_W, SSM_W)
    w_out_b = w_out_all.reshape(D_MODEL, D_MODEL)

    lam_r, lam_i, pw_r, pw_i, bb_r, bb_i = _ssm_prep(a_re, a_im, log_step, b_re, b_im, t // N_SEG)
    rows8 = lambda a: jnp.broadcast_to(a.reshape(SSM_GB, 1, SSM_ST), (SSM_GB, N_SEG, SSM_ST))
    lam_r8, lam_i8, pw_r8, pw_i8 = rows8(lam_r), rows8(lam_i), rows8(pw_r), rows8(pw_i)
    ssm_w_in = jnp.concatenate([_block_diag_in(bb_r), _block_diag_in(bb_i)], axis=1).astype(BF16)
    ssm_w_out = jnp.concatenate([_block_diag_out(c_re), _block_diag_out(-c_im)], axis=2).astype(BF16)
    d_row = d_skip.reshape(1, SSM_W)
    y, hc = _ssm_fwd(proj, lam_r8, lam_i8, pw_r8, pw_i8, ssm_w_in, ssm_w_out, d_row)
    b_glu_row = b_glu.reshape(1, SSM_W)
    os_, yg = _glu_fwd(y, proj, w_glu_b, b_glu_row)
    aw = attn_out_norm_w.reshape(1, ATTN_W)
    sw = ssm_out_norm_w.reshape(1, SSM_W)
    merged, dout, sq_err = _outproj(oa, os_, aw, sw, w_out_b, xs, tgt)
    loss = lax.psum(0.5 * sq_err[0, 0] / D_MODEL, MESH_AXES)

    doa, dos, g_aw, g_sw = _outproj_bwd(dout, oa, os_, aw, sw, w_out_b)
    dout_b = dout.astype(BF16)
    (g_w_out_b,) = _matmul_tn(merged, dout_b, tm=512, tn=1024, name="grad_w_out", dtypes=(BF16,))
    dy, dzs, da, g_b_glu = _glu_bwd(y, proj, dos, w_glu_b, b_glu_row)
    (g_w_glu_b,) = _matmul_tn(yg, da, tm=512, tn=1024, name="grad_w_glu", dtypes=(BF16,))
    du, g_wi, g_wo, g_lam, g_d = _ssm_bwd(proj, dy, hc, lam_r8, lam_i8, pw_r8, pw_i8, ssm_w_in, ssm_w_out, d_row)
    early = [g_w_glu_b.reshape(8, 128, SSM_W), g_w_out_b.reshape(8, 256, D_MODEL)]
    dproj, g_qw, g_kw, g_sink, *early_slots = _attn_bwd(proj, rope, qw, kw, sink_row, doa, du, dzs, early)
    g_w_in, g_w_in_b = _matmul_tn(hn, dproj, tm=512, tn=SHARD_W, name="grad_w_in", slabs=True)
    in_shape = (4, 2, D_MODEL // 2, SHARD_W)
    (from_sib,) = _pair_swap([g_w_in_b.reshape(in_shape)])
    pair_in = _add_halves(g_w_in.reshape(in_shape), from_sib, c_idx, rows=512, name="pair_sum")
    grad_x, g_nw, in_slots = _inproj_bwd(dproj, w_in_all, xs, norm_w, dout, [pair_in])

    g_wi = g_wi.reshape(SSM_G, SSM_H, 2 * SSM_P)
    g_wo = g_wo.reshape(SSM_G, SSM_H, 2 * SSM_P)
    g_bb_r = g_wi[:, :, :SSM_P].transpose(0, 2, 1).reshape(SSM_G, SSM_P * SSM_H)
    g_bb_i = g_wi[:, :, SSM_P:].transpose(0, 2, 1).reshape(SSM_G, SSM_P * SSM_H)
    g_a_re, g_a_im, g_ls, g_b_re, g_b_im = _ssm_param_grads(
        a_re, a_im, log_step, b_re, b_im, g_lam[:, 0, :SSM_ST].reshape(SSM_G, SSM_P),
        g_lam[:, 0, SSM_ST:].reshape(SSM_G, SSM_P), g_bb_r, g_bb_i)
    small_g = dict(
        norm_w=g_nw, q_norm_w=g_qw[0, :64] + g_qw[0, 64:], k_norm_w=g_kw[0, :64] + g_kw[0, 64:],
        sinks=g_sink[0, :N_HEADS], a_re=g_a_re, a_im=g_a_im, log_step=g_ls, b_re=g_b_re, b_im=g_b_im,
        c_re=g_wo[:, :, :SSM_P], c_im=-g_wo[:, :, SSM_P:], d_skip=g_d,
        b_glu=g_b_glu, attn_out_norm_w=g_aw, ssm_out_norm_w=g_sw)

    mine = [_sum_peers(in_slots, pair_in, chip_idx, rows=512, name="sum_w_in"),
            _sum_peers(early_slots[0], early[0], dev_idx, rows=128, name="sum_w_glu"),
            _sum_peers(early_slots[1], early[1], dev_idx, rows=128, name="sum_w_out")]
    theirs = _half_swap(mine)
    packed = _pack_small(small_g, SMALL, 8 * PACK_ROWS).reshape(8, PACK_ROWS, 128)
    summed = _sum_slots(_exchange_slices(packed, True, "small_scatter"), rows=PACK_ROWS, name="small_sum")
    small_red = _exchange_slices(summed, False, "small_gather").reshape(8 * PACK_ROWS, 128)

    big = [_adamw_halves(w_in, mine[0], theirs[0], m_w_in, v_w_in, c_idx, rows=256, name="adamw_w_in"),
           _adamw_halves(w_glu, mine[1], theirs[1], m_w_glu, v_w_glu, c_idx, rows=128, name="adamw_w_glu"),
           _adamw_halves(w_out, mine[2], theirs[2], m_w_out, v_w_out, c_idx, rows=256, name="adamw_w_out")]
    g_in_sh, g_glu_sh, g_out_sh = (b[0] for b in big)
    upd = [b[1:] for b in big]
    grads = _unpack_small(small_red, SMALL)
    flat_first = sum(math.prod(SMALL_SHAPES[n]) for n in SMALL_3D) // 128
    sd, sm, sv = _adamw(_pack_small(small_w, SMALL_FLAT, FLAT_ROWS), small_red[flat_first:flat_first + FLAT_ROWS],
                        _pack_small(small_m, SMALL_FLAT, FLAT_ROWS), _pack_small(small_v, SMALL_FLAT, FLAT_ROWS),
                        rows=FLAT_ROWS, name="adamw_small")
    deltas, new_m, new_v = (_unpack_small(a, SMALL_FLAT) for a in (sd, sm, sv))
    for n in SMALL_3D:
        deltas[n], new_m[n], new_v[n] = _adamw_3d(small_w[n], grads[n], small_m[n], small_v[n], name="adamw_" + n)
    grads.update(w_in=g_in_sh, w_glu=g_glu_sh, w_out=g_out_sh)
    for n, (d, m_, v_) in zip(("w_in", "w_glu", "w_out"), upd):
        deltas[n], new_m[n], new_v[n] = d, m_, v_
    order = ["norm_w", "w_in", "q_norm_w", "k_norm_w", "sinks", "a_re", "a_im", "log_step", "b_re", "b_im", "c_re",
             "c_im", "d_skip", "w_glu", "b_glu", "attn_out_norm_w", "ssm_out_norm_w", "w_out"]
    return (loss, grad_x[None], *[grads[n] for n in order], *[deltas[n] for n in order],
            *[new_m[n] for n in order], *[new_v[n] for n in order])
```

```python
import math

import jax
import jax.numpy as jnp
from jax import lax
from jax.experimental import pallas as pl
from jax.experimental.pallas import tpu as pltpu

F32 = jnp.float32
BF16 = jnp.bfloat16

D_MODEL = 2048
ATTN_W = 1024
SSM_W = 1024
HEAD_DIM = 64
N_HEADS = 16
N_KV_HEADS = 4
KV_W = 256
BLOCK = 128
IN_W = 4608
SHARD_W = IN_W // 4
ROPE_THETA = 10000.0
SSM_H = 16
SSM_G = 64
SSM_P = 64
NORM_EPS = 1e-6
ADAM_LR = 0.001
ADAM_B1 = 0.9
ADAM_B2 = 0.999
ADAM_EPS = 1e-08
ADAM_WD = 0.01
ADAM_STEP = 10

N_SEG = 8
SSM_GB = 4
SSM_CH = 256
SSM_ST = 1024
SCAN_ROWS = 512
SCAN_LW = 256
VMEM_LIMIT = 56 * 1024 * 1024
MESH_AXES = ("x", "y", "c")
ANY = pl.BlockSpec(memory_space=pl.ANY)

SMALL_3D = ("b_re", "b_im", "c_re", "c_im")
SMALL_FLAT = ("norm_w", "q_norm_w", "k_norm_w", "sinks", "a_re", "a_im", "log_step", "d_skip", "b_glu",
              "attn_out_norm_w", "ssm_out_norm_w")
SMALL = SMALL_3D + SMALL_FLAT
SMALL_SHAPES = {"norm_w": (2048,), "q_norm_w": (64,), "k_norm_w": (64,), "sinks": (16,), "a_re": (64, 64),
                "a_im": (64, 64), "log_step": (64,), "b_re": (64, 64, 16), "b_im": (64, 64, 16),
                "c_re": (64, 16, 64), "c_im": (64, 16, 64), "d_skip": (1024,), "b_glu": (1024,),
                "attn_out_norm_w": (1024,), "ssm_out_norm_w": (1024,)}
PACK_ROWS = 272
FLAT_ROWS = 120


def _params(sem=None):
    return pltpu.CompilerParams(dimension_semantics=sem, vmem_limit_bytes=VMEM_LIMIT)


def _dot(a, b):
    return jnp.dot(a, b, preferred_element_type=F32)


def _dot_nt(a, b):
    return lax.dot_general(a, b, (((1,), (1,)), ((), ())), preferred_element_type=F32)


def _dot_tn(a, b):
    return lax.dot_general(a, b, (((0,), (0,)), ((), ())), preferred_element_type=F32)


def _sigmoid(x):
    return 1.0 / (1.0 + jnp.exp(-x))


def _silu(x):
    return x * _sigmoid(x)


def _dsilu(x):
    s = _sigmoid(x)
    return s * (1.0 + x * (1.0 - s))


_GELU_C = math.sqrt(2.0 / math.pi)


def _gelu(x):
    return 0.5 * x * (1.0 + jnp.tanh(_GELU_C * (x + 0.044715 * x * x * x)))


def _dgelu(x):
    t = jnp.tanh(_GELU_C * (x + 0.044715 * x * x * x))
    return 0.5 * (1.0 + t) + 0.5 * x * (1.0 - t * t) * _GELU_C * (1.0 + 3.0 * 0.044715 * x * x)


def _matmul_tn(a, b, *, tm, tn, name, slabs=False, dtypes=(F32, BF16)):
    k, m = a.shape
    _, n = b.shape

    def body(a_ref, b_ref, *o_refs):
        acc = _dot_tn(a_ref[...], b_ref[...])
        for o_ref in o_refs:
            o_ref[...] = acc.astype(o_ref.dtype)

    if slabs:
        out_spec = pl.BlockSpec((None, tm, tn), lambda j, i: (j, i, 0))
        shape = (n // tn, m, tn)
    else:
        out_spec = pl.BlockSpec((tm, tn), lambda j, i: (i, j))
        shape = (m, n)
    return pl.pallas_call(
        body, name=name, grid=(n // tn, m // tm),
        in_specs=[pl.BlockSpec((k, tm), lambda j, i: (0, i)), pl.BlockSpec((k, tn), lambda j, i: (0, j))],
        out_specs=[out_spec] * len(dtypes),
        out_shape=[jax.ShapeDtypeStruct(shape, d) for d in dtypes],
        compiler_params=_params(("parallel", "parallel")),
    )(a, b)


def _inproj(x, norm_w, w_sh, order):
    t = x.shape[0]
    tm = 512
    ni = t // tm
    hr = D_MODEL // 2

    def body(ord_ref, x_ref, nw_ref, sh_ref, proj_ref, hn_ref, full_ref, wbuf, hn_s, ssem, rsem, lsem):
        s, i = pl.program_id(0), pl.program_id(1)
        mx, my, c = _mesh_pos()
        me = 2 * mx + my
        sib = (mx, my, 1 - c)
        chips = _other_chips(mx, my)

        def half(which):
            return pl.ds(pl.multiple_of(which * hr, 8), hr)

        def slot(k):
            return 2 * chips[k][0] + chips[k][1]

        def ici(k):
            return _remote(sh_ref.at[half(c)], full_ref.at[me, half(c)], ssem.at[k], rsem.at[k], (*chips[k], c))

        def own():
            return _remote(sh_ref, full_ref.at[me], ssem.at[6], rsem.at[6], sib)

        def landed(k, which, sem):
            ref = full_ref.at[slot(k), half(which)]
            return _remote(ref, ref, ssem.at[sem], rsem.at[sem], sib)

        def fetch(src, b):
            return pltpu.make_async_copy(src, wbuf.at[b], lsem.at[b])

        @pl.when((s == 0) & (i == 0))
        def _():
            for k in range(3):
                ici(k).start()
            own().start()
            cp = fetch(sh_ref, 0)
            cp.start()
            cp.wait()

        for k in range(3):
            @pl.when((s == k) & (i == max(ni - 2, 0)))
            def _(k=k):
                landed(k, c, k).wait_recv()
                landed(k, c, 3 + k).start()
                landed(k, 1 - c, 3 + k).wait_recv()
                fetch(full_ref.at[slot(k)], (k + 1) % 2).start()

            @pl.when((s == k + 1) & (i == 0))
            def _(k=k):
                fetch(full_ref.at[slot(k)], (k + 1) % 2).wait()

        xv = x_ref[...]
        r = lax.rsqrt(jnp.mean(xv * xv, axis=1, keepdims=True) + NORM_EPS)
        hn = (xv * r * nw_ref[...]).astype(BF16)
        proj_ref[...] = _dot(hn, wbuf[s % 2])

        def hn_out(tile):
            return pltpu.make_async_copy(hn_s, hn_ref.at[pl.ds(pl.multiple_of(tile * tm, tm), tm), :], lsem.at[2])

        @pl.when(((s == 0) & (i > 0)) | ((s == 1) & (i == 0)))
        def _():
            hn_out(jnp.where(s == 0, i - 1, ni - 1)).wait()

        @pl.when(s == 0)
        def _():
            hn_s[...] = hn
            hn_out(i).start()

        @pl.when((s == 3) & (i == ni - 1))
        def _():
            mine = full_ref.at[me]
            _remote(mine, mine, ssem.at[6], rsem.at[6], sib).wait_recv()
            for k in range(3):
                ici(k).wait_send()
                landed(k, c, 3 + k).wait_send()
            own().wait_send()

    return pl.pallas_call(
        body, name="inproj",
        grid_spec=pltpu.PrefetchScalarGridSpec(
            num_scalar_prefetch=1, grid=(4, ni),
            in_specs=[pl.BlockSpec((tm, D_MODEL), lambda s, i, o: (i, 0)),
                      pl.BlockSpec((1, D_MODEL), lambda s, i, o: (0, 0)), ANY],
            out_specs=[pl.BlockSpec((tm, SHARD_W), lambda s, i, o: (i, o[s])), ANY, ANY],
            scratch_shapes=[pltpu.VMEM((2, D_MODEL, SHARD_W), BF16), pltpu.VMEM((tm, D_MODEL), BF16),
                            pltpu.SemaphoreType.DMA((7,)), pltpu.SemaphoreType.DMA((7,)),
                            pltpu.SemaphoreType.DMA((3,))]),
        out_shape=[jax.ShapeDtypeStruct((t, IN_W), F32), jax.ShapeDtypeStruct((t, D_MODEL), BF16),
                   jax.ShapeDtypeStruct((4, D_MODEL, SHARD_W), BF16)],
        compiler_params=_params(("arbitrary", "arbitrary")),
    )(order, x, norm_w.reshape(1, D_MODEL), w_sh)


def _lane128():
    return lax.broadcasted_iota(jnp.int32, (1, 128), 1)


def _head_sums(v):
    lo = _lane128() < 64
    s_lo = jnp.sum(jnp.where(lo, v, 0.0), axis=1, keepdims=True)
    s_hi = jnp.sum(jnp.where(lo, 0.0, v), axis=1, keepdims=True)
    return jnp.where(lo, s_lo, s_hi)


def _rot_half(t):
    first = (_lane128() % 64) < 32
    return jnp.where(first, -pltpu.roll(t, 96, 1), pltpu.roll(t, 32, 1))


def _head_rstd(t):
    return lax.rsqrt(_head_sums(t * t) * (1.0 / HEAD_DIM) + NORM_EPS)


def _prep_tile(t, w, cos, sin, r=None):
    r = _head_rstd(t) if r is None else r
    tn = t * r * w
    return tn * cos + _rot_half(tn) * sin


def _prep_tile_bwd(t, w, cos, sin, g, r=None):
    r = _head_rstd(t) if r is None else r
    d_tn = g * cos - _rot_half(g * sin)
    th = t * r
    dw = jnp.sum(d_tn * th, axis=0, keepdims=True)
    gh = d_tn * w
    m = _head_sums(gh * th) * (1.0 / HEAD_DIM)
    return r * (gh - th * m), dw


def _band_mask(n):
    qi = lax.broadcasted_iota(jnp.int32, (BLOCK, 2 * BLOCK), 0) + BLOCK
    ki = lax.broadcasted_iota(jnp.int32, (BLOCK, 2 * BLOCK), 1)
    rel = qi - ki
    return (rel >= 0) & (rel < BLOCK) & ((n > 0) | (ki >= BLOCK))


def _half_select(tile, half):
    lo = _lane128() < 64
    return jnp.where(lo if half == 0 else jnp.logical_not(lo), tile, 0.0)


def _stack_group(tiles, kv_half):
    rows = []
    for t in tiles:
        for half in range(2):
            piece = _half_select(t, half)
            rows.append(piece if half == kv_half else pltpu.roll(piece, 64, 1))
    return jnp.concatenate(rows, axis=0)


def _unstack_group(stacked, kv_half):
    tiles = []
    for i in range(2):
        acc = None
        for half in range(2):
            piece = _half_select(stacked[BLOCK * (2 * i + half):BLOCK * (2 * i + half + 1)], kv_half)
            piece = piece if half == kv_half else pltpu.roll(piece, 64, 1)
            acc = piece if acc is None else acc + piece
        tiles.append(acc)
    return tiles


def _stack_heads(tiles):
    zeros = jnp.zeros((4 * BLOCK, 128), F32)
    rows = []
    for g in range(N_KV_HEADS):
        half = _stack_group(tiles[2 * g:2 * g + 2], g % 2)
        rows.append(jnp.concatenate([half, zeros] if g < 2 else [zeros, half], axis=1))
    return jnp.concatenate(rows, axis=0)


def _unstack_heads(stacked):
    tiles = []
    for g in range(N_KV_HEADS):
        lanes = slice(0, 128) if g < 2 else slice(128, 256)
        tiles += _unstack_group(stacked[4 * BLOCK * g:4 * BLOCK * (g + 1), lanes], g % 2)
    return tiles


def _attn_specs(nb):
    last = nb - 1
    qi = lambda n: (jnp.minimum(n, last), 0)
    prev = lambda n: jnp.maximum(n - 1, 0)
    cur = lambda n: jnp.minimum(n, last)
    specs = [
        pl.BlockSpec((BLOCK, ATTN_W), qi),
        pl.BlockSpec((BLOCK, KV_W), lambda n: (cur(n), 4)),
        pl.BlockSpec((BLOCK, KV_W), lambda n: (prev(n), 4)),
        pl.BlockSpec((BLOCK, KV_W), lambda n: (cur(n), 5)),
        pl.BlockSpec((BLOCK, KV_W), lambda n: (prev(n), 5)),
        pl.BlockSpec((BLOCK, 512), lambda n: (cur(n), 3)),
        pl.BlockSpec((BLOCK, 512), lambda n: (cur(n), 4)),
        pl.BlockSpec((BLOCK, 256), lambda n: (cur(n), 0)),
        pl.BlockSpec((BLOCK, 256), lambda n: (prev(n), 0)),
        pl.BlockSpec((1, 128), lambda n: (0, 0)),
        pl.BlockSpec((1, 128), lambda n: (0, 0)),
        pl.BlockSpec((1, N_HEADS), lambda n: (0, 0)),
    ]
    return specs


def _rope_table(posf, invf):
    t = posf.shape[0]

    def body(p_ref, f_ref, o_ref):
        ang = p_ref[...] * f_ref[...]
        o_ref[...] = jnp.concatenate([jnp.cos(ang), jnp.sin(ang)], axis=1)

    return pl.pallas_call(
        body, name="rope_table", grid=(t // 512,),
        in_specs=[pl.BlockSpec((512, 1), lambda i: (i, 0)), pl.BlockSpec((1, 128), lambda i: (0, 0))],
        out_specs=pl.BlockSpec((512, 256), lambda i: (i, 0)),
        out_shape=jax.ShapeDtypeStruct((t, 256), F32), compiler_params=_params(("parallel",)),
    )(posf, invf)


def _attn_common(n, q_ref, kc_ref, kp_ref, vc_ref, vp_ref, rq_ref, rp_ref, qw_ref, kw_ref):
    cos_q, sin_q = rq_ref[:, 0:128], rq_ref[:, 128:256]
    cos_k = jnp.concatenate([rp_ref[:, 0:128], cos_q], axis=0)
    sin_k = jnp.concatenate([rp_ref[:, 128:256], sin_q], axis=0)
    k_raw = jnp.concatenate([kp_ref[...], kc_ref[...]], axis=0)
    vv = jnp.concatenate([vp_ref[...], vc_ref[...]], axis=0).astype(BF16)
    kk = [_prep_tile(k_raw[:, 128 * i:128 * i + 128], kw_ref[...], cos_k, sin_k).astype(BF16) for i in range(2)]
    vt = [vv[:, 128 * i:128 * i + 128] for i in range(2)]
    qv = q_ref[...]
    qr = [_head_rstd(qv[:, 128 * i:128 * i + 128]) for i in range(8)]
    qt = [_prep_tile(qv[:, 128 * i:128 * i + 128], qw_ref[...], cos_q, sin_q, qr[i]) for i in range(8)]
    return cos_q, sin_q, qr, kk, vt, qt


QK_SCALE = 1.0 / math.sqrt(HEAD_DIM)


def _group_sinks(sink_ref, g):
    return jnp.concatenate([jnp.broadcast_to(sink_ref[:, 4 * g + j:4 * g + j + 1], (BLOCK, 1)) for j in range(4)], axis=0)


def _group_softmax(q4, kk_t, sink, bias):
    s = _dot_nt(q4, kk_t) + bias
    m = jnp.maximum(jnp.max(s, axis=1, keepdims=True), sink)
    p = jnp.exp(s - m)
    es = jnp.exp(sink - m)
    inv = 1.0 / (jnp.sum(p, axis=1, keepdims=True) + es)
    return p * inv, es * inv


def _group_bias(n):
    return jnp.concatenate([jnp.where(_band_mask(n), 0.0, -1e30)] * 4, axis=0)


def _attn_fwd(proj, rope, qw, kw, sinks, later_shards):
    t = proj.shape[0]
    nb = t // BLOCK
    nt = len(later_shards)

    def body(q_ref, kc_ref, kp_ref, vc_ref, vp_ref, za0_ref, za1_ref, rq_ref, rp_ref, qw_ref, kw_ref,
             sink_ref, *rest):
        sh, o_ref, full = rest[:nt], rest[nt], rest[nt + 1:2 * nt + 1]
        ssem, rsem = rest[2 * nt + 1:]
        n = pl.program_id(0)
        start, wait = _bg_gather(sh, full, ssem, rsem)

        @pl.when(n == 0)
        def _():
            start()

        _, _, _, kk, vt, qt = _attn_common(n, q_ref, kc_ref, kp_ref, vc_ref, vp_ref, rq_ref, rp_ref, qw_ref, kw_ref)
        bias = jnp.concatenate([_group_bias(n)] * 4, axis=0)
        q16 = (_stack_heads(qt) * QK_SCALE).astype(BF16)
        sink16 = jnp.concatenate([_group_sinks(sink_ref, g) for g in range(N_KV_HEADS)], axis=0)
        p, _ = _group_softmax(q16, jnp.concatenate(kk, axis=1), sink16, bias)
        tiles = _unstack_heads(_dot(p.astype(BF16), jnp.concatenate(vt, axis=1)))
        za = jnp.concatenate([za0_ref[...], za1_ref[...]], axis=1)
        o_ref[...] = jnp.concatenate(tiles, axis=1) * _silu(za)

        @pl.when(n == nb - 1)
        def _():
            wait()

    return pl.pallas_call(
        body, name="attn_fwd", grid=(nb,), in_specs=_attn_specs(nb) + [ANY] * nt,
        out_specs=[pl.BlockSpec((BLOCK, ATTN_W), lambda n: (n, 0))] + [ANY] * nt,
        out_shape=[jax.ShapeDtypeStruct((t, ATTN_W), F32)]
        + [jax.ShapeDtypeStruct((4,) + s.shape, s.dtype) for s in later_shards],
        scratch_shapes=[pltpu.SemaphoreType.DMA((4 * nt,)), pltpu.SemaphoreType.DMA((4 * nt,))],
        compiler_params=_params(("arbitrary",)),
    )(proj, proj, proj, proj, proj, proj, proj, rope, rope, qw, kw, sinks, *later_shards)


def _attn_bwd(proj, rope, qw, kw, sinks, doa, du, dzs, outgoing):
    t = proj.shape[0]
    nb = t // BLOCK
    last = nb - 1
    nt = len(outgoing)

    def body(q_ref, kc_ref, kp_ref, vc_ref, vp_ref, za0_ref, za1_ref, rq_ref, rp_ref, qw_ref, kw_ref,
             sink_ref, doa_ref, du_ref, dzs_ref, *rest):
        src = rest[:nt]
        dp_ref, gq_ref, gk_ref, gs_ref = rest[nt:nt + 4]
        dst = rest[nt + 4:2 * nt + 4]
        dkk_s, dvv_s, ck_s, cv_s, dq_s, dza_s, ssem, rsem = rest[2 * nt + 4:]
        n = pl.program_id(0)
        start, wait = _bg_scatter_devices(src, dst, ssem, rsem)

        @pl.when(n == 0)
        def _():
            start()
            gq_ref[...] = jnp.zeros_like(gq_ref)
            gk_ref[...] = jnp.zeros_like(gk_ref)
            gs_ref[...] = jnp.zeros_like(gs_ref)
            ck_s[...] = jnp.zeros_like(ck_s)
            cv_s[...] = jnp.zeros_like(cv_s)
            dq_s[...] = jnp.zeros_like(dq_s)
            dza_s[...] = jnp.zeros_like(dza_s)

        dp_ref[:, 0:ATTN_W] = dq_s[...]
        dp_ref[:, ATTN_W + 2 * KV_W:2 * ATTN_W + 2 * KV_W] = dza_s[...]
        dp_ref[:, 2 * ATTN_W + 2 * KV_W:IN_W - SSM_W] = du_ref[...]
        dp_ref[:, IN_W - SSM_W:IN_W] = dzs_ref[...]

        @pl.when(n == nb)
        def _():
            dkk_s[...] = jnp.zeros_like(dkk_s)
            dvv_s[...] = jnp.zeros_like(dvv_s)

        @pl.when(n < nb)
        def _():
            cos_q, sin_q, qr, kk, vt, qt = _attn_common(n, q_ref, kc_ref, kp_ref, vc_ref, vp_ref, rq_ref, rp_ref,
                                                        qw_ref, kw_ref)
            bias = jnp.concatenate([_group_bias(n)] * 4, axis=0)
            za = jnp.concatenate([za0_ref[...], za1_ref[...]], axis=1)
            doa_v = doa_ref[...]
            do_full = doa_v * _silu(za)
            k_all, v_all = jnp.concatenate(kk, axis=1), jnp.concatenate(vt, axis=1)
            q_b = (_stack_heads(qt) * QK_SCALE).astype(BF16)
            do_b = _stack_heads([do_full[:, 128 * i:128 * i + 128] for i in range(8)]).astype(BF16)
            sink16 = jnp.concatenate([_group_sinks(sink_ref, g) for g in range(N_KV_HEADS)], axis=0)
            p, psink = _group_softmax(q_b, k_all, sink16, bias)
            p_b = p.astype(BF16)
            dp = _dot_nt(do_b, v_all)
            delta = jnp.sum(p * dp, axis=1, keepdims=True)
            ds_b = (p * (dp - delta)).astype(BF16)
            sd = psink * delta
            gsink = jnp.zeros((1, 128), F32)
            lane = _lane128()
            for h in range(N_HEADS):
                gsink = gsink + jnp.where(lane == h, -jnp.sum(sd[BLOCK * h:BLOCK * (h + 1)]), 0.0)
            o_tiles = _unstack_heads(_dot(p_b, v_all))
            dq_tiles = [d * QK_SCALE for d in _unstack_heads(_dot(ds_b, k_all))]
            dkk = [_dot_tn(ds_b, q_b)]
            dvv = [_dot_tn(p_b, do_b)]
            dza_s[...] = (doa_v * jnp.concatenate(o_tiles, axis=1) * _dsilu(za)).astype(BF16)
            qv = q_ref[...]
            gq = jnp.zeros((1, 128), F32)
            out = []
            for i in range(8):
                d, dw = _prep_tile_bwd(qv[:, 128 * i:128 * i + 128], qw_ref[...], cos_q, sin_q, dq_tiles[i], qr[i])
                out.append(d)
                gq = gq + dw
            dq_s[...] = jnp.concatenate(out, axis=1).astype(BF16)
            gq_ref[...] += gq
            gs_ref[...] += gsink
            dkk_s[...] = jnp.concatenate(dkk, axis=1)
            dvv_s[...] = jnp.concatenate(dvv, axis=1)

        cos_p, sin_p = rp_ref[:, 0:128], rp_ref[:, 128:256]
        dk_prev = ck_s[...] + dkk_s[0:BLOCK, :]
        kp = kp_ref[...]
        gk = jnp.zeros((1, 128), F32)
        out = []
        for i in range(2):
            d, dw = _prep_tile_bwd(kp[:, 128 * i:128 * i + 128], kw_ref[...], cos_p, sin_p,
                                   dk_prev[:, 128 * i:128 * i + 128])
            out.append(d)
            gk = gk + dw
        dp_ref[:, ATTN_W:ATTN_W + KV_W] = jnp.concatenate(out, axis=1).astype(BF16)
        dp_ref[:, ATTN_W + KV_W:ATTN_W + 2 * KV_W] = (cv_s[...] + dvv_s[0:BLOCK, :]).astype(BF16)
        gk_ref[...] += gk
        ck_s[...] = dkk_s[BLOCK:2 * BLOCK, :]
        cv_s[...] = dvv_s[BLOCK:2 * BLOCK, :]

        @pl.when(n == nb)
        def _():
            wait()

    qblk = lambda n: (jnp.minimum(n, last), 0)
    kblk = lambda n: (jnp.maximum(n - 1, 0), 0)
    vec = pl.BlockSpec((1, 128), lambda n: (0, 0))
    return pl.pallas_call(
        body, name="attn_bwd", grid=(nb + 1,),
        in_specs=_attn_specs(nb) + [pl.BlockSpec((BLOCK, ATTN_W), qblk), pl.BlockSpec((BLOCK, SSM_W), kblk),
                                    pl.BlockSpec((BLOCK, SSM_W), kblk)] + [ANY] * nt,
        out_specs=[pl.BlockSpec((BLOCK, IN_W), kblk), vec, vec, vec] + [ANY] * nt,
        out_shape=[jax.ShapeDtypeStruct((t, IN_W), BF16), jax.ShapeDtypeStruct((1, 128), F32),
                   jax.ShapeDtypeStruct((1, 128), F32), jax.ShapeDtypeStruct((1, 128), F32)]
        + [jax.ShapeDtypeStruct(a.shape, a.dtype) for a in outgoing],
        scratch_shapes=[pltpu.VMEM((2 * BLOCK, KV_W), F32), pltpu.VMEM((2 * BLOCK, KV_W), F32),
                        pltpu.VMEM((BLOCK, KV_W), F32), pltpu.VMEM((BLOCK, KV_W), F32),
                        pltpu.VMEM((BLOCK, ATTN_W), BF16), pltpu.VMEM((BLOCK, ATTN_W), BF16),
                        pltpu.SemaphoreType.DMA((7 * nt,)), pltpu.SemaphoreType.DMA((7 * nt,))],
        compiler_params=_params(("arbitrary",)),
    )(proj, proj, proj, proj, proj, proj, proj, rope, rope, qw, kw, sinks, doa, du, dzs, *outgoing)


def _cmul(ar, ai, br, bi):
    return ar * br - ai * bi, ar * bi + ai * br


def _zoh(a_re, a_im, delta):
    e = jnp.exp(a_re * delta)
    lr, li = e * jnp.cos(a_im * delta), e * jnp.sin(a_im * delta)
    inv = 1.0 / (a_re * a_re + a_im * a_im)
    fr, fi = _cmul(lr - 1.0, li, a_re * inv, -a_im * inv)
    return lr, li, fr, fi


def _ssm_prep(a_re, a_im, log_step, b_re, b_im, seg_len):
    n_sq = int(round(math.log2(seg_len)))
    assert 2 ** n_sq == seg_len

    def body(ar_ref, ai_ref, ls_ref, arx_ref, aix_ref, br_ref, bi_ref, lr_ref, li_ref, pr_ref, pi_ref, bbr_ref, bbi_ref):
        delta = jnp.exp(ls_ref[...])
        lr, li, _, _ = _zoh(ar_ref[...], ai_ref[...], delta)
        lr_ref[...] = lr
        li_ref[...] = li
        pr, pi = lr, li
        for _ in range(n_sq):
            pr, pi = _cmul(pr, pi, pr, pi)
        pr_ref[...] = pr
        pi_ref[...] = pi
        _, _, fr, fi = _zoh(arx_ref[...], aix_ref[...], delta)
        bbr, bbi = _cmul(fr, fi, br_ref[...], bi_ref[...])
        bbr_ref[...] = bbr
        bbi_ref[...] = bbi

    gp = jax.ShapeDtypeStruct((SSM_G, SSM_P), F32)
    gx = jax.ShapeDtypeStruct((SSM_G, SSM_P * SSM_H), F32)
    return pl.pallas_call(body, name="ssm_prep", out_shape=[gp, gp, gp, gp, gx, gx])(
        a_re, a_im, log_step.reshape(SSM_G, 1), jnp.repeat(a_re, SSM_H, axis=1), jnp.repeat(a_im, SSM_H, axis=1),
        b_re.reshape(SSM_G, SSM_P * SSM_H), b_im.reshape(SSM_G, SSM_P * SSM_H))


def _ssm_param_grads(a_re, a_im, log_step, b_re, b_im, dlam_re, dlam_im, dbb_re, dbb_im):
    def body(ar_ref, ai_ref, ls_ref, arx_ref, aix_ref, br_ref, bi_ref, dlr_ref, dli_ref, dbr_ref, dbi_ref,
             gar_ref, gai_ref, gls_ref, gbr_ref, gbi_ref):
        delta = jnp.exp(ls_ref[...])
        ar, ai = ar_ref[...], ai_ref[...]
        lr, li, fr, fi = _zoh(ar, ai, delta)
        _, _, frx, fix = _zoh(arx_ref[...], aix_ref[...], delta)
        dbr, dbi = dbr_ref[...], dbi_ref[...]
        br, bi = br_ref[...], bi_ref[...]
        gbr, gbi = _cmul(frx, -fix, dbr, dbi)
        gbr_ref[...] = gbr
        gbi_ref[...] = gbi
        tr, ti = _cmul(br, -bi, dbr, dbi)
        row = lax.broadcasted_iota(jnp.int32, (SSM_P * SSM_H, SSM_P), 0)
        col = lax.broadcasted_iota(jnp.int32, (SSM_P * SSM_H, SSM_P), 1)
        fold = (row // SSM_H == col).astype(F32)
        dfr = jnp.dot(tr, fold, precision=lax.Precision.HIGHEST, preferred_element_type=F32)
        dfi = jnp.dot(ti, fold, precision=lax.Precision.HIGHEST, preferred_element_type=F32)
        inv = 1.0 / (ar * ar + ai * ai)
        ilr, ili = ar * inv, -ai * inv
        t1r, t1i = _cmul(dfr, dfi, ilr, -ili)
        dlbr, dlbi = dlr_ref[...] + t1r, dli_ref[...] + t1i
        qr, qi = _cmul(fr, fi, ilr, ili)
        t2r, t2i = _cmul(dfr, dfi, qr, -qi)
        glr, gli = -t2r, -t2i
        dzr, dzi = _cmul(dlbr, dlbi, lr, -li)
        gar_ref[...] = glr + dzr * delta
        gai_ref[...] = gli + dzi * delta
        gls_ref[...] = jnp.sum(dzr * ar + dzi * ai, axis=1, keepdims=True) * delta

    gp = jax.ShapeDtypeStruct((SSM_G, SSM_P), F32)
    gx = jax.ShapeDtypeStruct((SSM_G, SSM_P * SSM_H), F32)
    return pl.pallas_call(body, name="ssm_param_grads",
                          out_shape=[gp, gp, jax.ShapeDtypeStruct((SSM_G, 1), F32), gx, gx])(
        a_re, a_im, log_step.reshape(SSM_G, 1), jnp.repeat(a_re, SSM_H, axis=1), jnp.repeat(a_im, SSM_H, axis=1),
        b_re.reshape(SSM_G, SSM_P * SSM_H), b_im.reshape(SSM_G, SSM_P * SSM_H), dlam_re, dlam_im, dbb_re, dbb_im)


def _block_diag_in(bb):
    w = jnp.tile(bb.reshape(SSM_GB, SSM_ST, SSM_H), (1, 1, 16))
    row = lax.broadcasted_iota(jnp.int32, (1, SSM_ST, SSM_CH), 1) // SSM_P
    col = lax.broadcasted_iota(jnp.int32, (1, SSM_ST, SSM_CH), 2) // SSM_H
    return jnp.where(row == col, w, 0.0)


def _block_diag_out(c):
    w = jnp.tile(c.reshape(SSM_GB, SSM_CH, SSM_P), (1, 1, 16))
    row = lax.broadcasted_iota(jnp.int32, (1, SSM_CH, SSM_ST), 1) // SSM_H
    col = lax.broadcasted_iota(jnp.int32, (1, SSM_CH, SSM_ST), 2) // SSM_P
    return jnp.where(row == col, w, 0.0)


SEG_ROWS = SCAN_ROWS // N_SEG


def _chunk_perm():
    out_row = lax.broadcasted_iota(jnp.int32, (SCAN_ROWS, SCAN_ROWS), 0)
    in_row = lax.broadcasted_iota(jnp.int32, (SCAN_ROWS, SCAN_ROWS), 1)
    return (out_row == N_SEG * (in_row % SEG_ROWS) + in_row // SEG_ROWS).astype(BF16)


def _chunk_rows(j, seg_len, s):
    return pl.ds(pl.multiple_of(s * seg_len + j * SEG_ROWS, SEG_ROWS), SEG_ROWS)


def _gather_chunk(ref, j, seg_len):
    return jnp.concatenate([ref[_chunk_rows(j, seg_len, s), :] for s in range(N_SEG)], axis=0)


def _scatter_chunk(ref, j, seg_len, val):
    for s in range(N_SEG):
        ref[_chunk_rows(j, seg_len, s), :] = val[s * SEG_ROWS:(s + 1) * SEG_ROWS]


def _interleave(perm, x_b):
    return _dot(perm, x_b).astype(BF16)


def _state_pieces():
    return [pl.ds(part * SSM_ST + k * SCAN_LW, SCAN_LW) for k in range(SSM_ST // SCAN_LW) for part in range(2)]


def _expand_states(x_b, w_ref, dst_ref):
    for cols in _state_pieces():
        dst_ref[:, cols] = _dot_nt(x_b, w_ref[cols, :])


def _contract_states(src_ref, w_ref):
    acc = None
    for cols in _state_pieces():
        part = _dot_nt(src_ref[:, cols].astype(BF16), w_ref[:, cols])
        acc = part if acc is None else acc + part
    return acc


def _scan_fwd(src_ref, dst_ref, lam_r_ref, lam_i_ref, init_ref, final_ref, steps):
    for k in range(SSM_ST // SCAN_LW):
        re = pl.ds(k * SCAN_LW, SCAN_LW)
        im = pl.ds(SSM_ST + k * SCAN_LW, SCAN_LW)
        lr, li = lam_r_ref[:, re], lam_i_ref[:, re]

        def step(i, carry, re=re, im=im, lr=lr, li=li):
            hr, hi = carry
            rows = pl.ds(pl.multiple_of(i * 8, 8), 8)
            nr = lr * hr - li * hi + src_ref[rows, re]
            ni = lr * hi + li * hr + src_ref[rows, im]
            if dst_ref is not None:
                dst_ref[rows, re] = nr
                dst_ref[rows, im] = ni
            return nr, ni

        hr, hi = lax.fori_loop(0, steps, step, (init_ref[:, re], init_ref[:, im]), unroll=True)
        final_ref[:, re] = hr
        final_ref[:, im] = hi


def _ssm_specs(t):
    col = lambda g: (0, g)
    gb3 = lambda g: (g, 0, 0)
    return dict(
        rows=pl.BlockSpec((t, SSM_CH), col),
        lam=pl.BlockSpec((None, N_SEG, SSM_ST), gb3),
        w_in=pl.BlockSpec((None, 2 * SSM_ST, SSM_CH), gb3),
        w_out=pl.BlockSpec((None, SSM_CH, 2 * SSM_ST), gb3),
        vec=pl.BlockSpec((1, SSM_CH), col),
    )


def _segment_states(x_ref, pw_r_ref, pw_i_ref, out_ref, reverse):
    re, im = pl.ds(0, SSM_ST), pl.ds(SSM_ST, SSM_ST)
    pr, pi = pw_r_ref[0:1, :], pw_i_ref[0:1, :]
    first = N_SEG - 1 if reverse else 0
    out_ref[first:first + 1, :] = jnp.zeros((1, 2 * SSM_ST), F32)
    order = range(N_SEG - 1, 0, -1) if reverse else range(N_SEG - 1)
    for s in order:
        d = s - 1 if reverse else s + 1
        hr, hi = out_ref[s:s + 1, re], out_ref[s:s + 1, im]
        if reverse:
            nr, ni = pr * hr + pi * hi, pr * hi - pi * hr
        else:
            nr, ni = pr * hr - pi * hi, pr * hi + pi * hr
        out_ref[d:d + 1, re] = nr + x_ref[s:s + 1, re]
        out_ref[d:d + 1, im] = ni + x_ref[s:s + 1, im]


def _ssm_fwd(proj, lam_r, lam_i, pw_r, pw_i, w_in, w_out, d_skip):
    t = proj.shape[0]
    seg_len = t // N_SEG
    nch = t // SCAN_ROWS
    steps = SCAN_ROWS // N_SEG
    sp = _ssm_specs(t)

    def body(u_ref, lr_ref, li_ref, pr_ref, pi_ref, wi_ref, wo_ref, d_ref, y_ref, hc_ref, bu_s, car_s, seg_s, ub_s,
             y0_s, y1_s):
        perm = _chunk_perm()
        car_s[...] = jnp.zeros_like(car_s)

        def chunk1(j, c):
            rows = pl.ds(pl.multiple_of(j * SCAN_ROWS, SCAN_ROWS), SCAN_ROWS)
            u_b = _interleave(perm, _gather_chunk(u_ref, j, seg_len).astype(BF16))
            ub_s[rows, :] = u_b
            _expand_states(u_b, wi_ref, bu_s)
            _scan_fwd(bu_s, None, lr_ref, li_ref, car_s, car_s, steps)
            return c

        lax.fori_loop(0, nch, chunk1, 0)
        _segment_states(car_s, pr_ref, pi_ref, seg_s, reverse=False)
        car_s[...] = seg_s[...]

        def chunk2(j, c):
            rows = pl.ds(pl.multiple_of(j * SCAN_ROWS, SCAN_ROWS), SCAN_ROWS)
            _expand_states(ub_s[rows, :], wi_ref, bu_s)
            hc_ref[j] = car_s[...]
            _scan_fwd(bu_s, bu_s, lr_ref, li_ref, car_s, car_s, steps)
            yv = _contract_states(bu_s, wo_ref)
            y0_s[...] = yv[:, 0:128]
            y1_s[...] = yv[:, 128:256]
            for s in range(N_SEG):
                nat = _chunk_rows(j, seg_len, s)
                sub = pl.ds(s, SEG_ROWS, stride=N_SEG)
                y_ref[nat, :] = jnp.concatenate([y0_s[sub, :], y1_s[sub, :]], axis=1) + d_ref[...] * u_ref[nat, :]
            return c

        lax.fori_loop(0, nch, chunk2, 0)

    u_cols = 2560 // SSM_CH
    return pl.pallas_call(
        body, name="ssm_fwd", grid=(SSM_GB,),
        in_specs=[pl.BlockSpec((t, SSM_CH), lambda g: (0, u_cols + g)), sp["lam"], sp["lam"], sp["lam"], sp["lam"],
                  sp["w_in"], sp["w_out"], sp["vec"]],
        out_specs=[sp["rows"], pl.BlockSpec((None, nch, N_SEG, 2 * SSM_ST), lambda g: (g, 0, 0, 0))],
        out_shape=[jax.ShapeDtypeStruct((t, SSM_W), F32), jax.ShapeDtypeStruct((SSM_GB, nch, N_SEG, 2 * SSM_ST), F32)],
        scratch_shapes=[pltpu.VMEM((SCAN_ROWS, 2 * SSM_ST), F32), pltpu.VMEM((N_SEG, 2 * SSM_ST), F32),
                        pltpu.VMEM((N_SEG, 2 * SSM_ST), F32), pltpu.VMEM((t, SSM_CH), BF16),
                        pltpu.VMEM((SCAN_ROWS, 128), F32), pltpu.VMEM((SCAN_ROWS, 128), F32)],
        compiler_params=_params(("parallel",)),
    )(proj, lam_r, lam_i, pw_r, pw_i, w_in, w_out, d_skip)


def _group_blocks(full):
    row_g = lax.broadcasted_iota(jnp.int32, (SSM_CH, SSM_ST), 0) // SSM_H
    col_g = lax.broadcasted_iota(jnp.int32, (SSM_CH, SSM_ST), 1) // SSM_P
    fold = (lax.broadcasted_iota(jnp.int32, (SSM_ST, SSM_P), 0) % SSM_P
            == lax.broadcasted_iota(jnp.int32, (SSM_ST, SSM_P), 1)).astype(F32)
    parts = [jnp.dot(jnp.where(row_g == col_g, full[:, k * SSM_ST:(k + 1) * SSM_ST], 0.0), fold,
                     precision=lax.Precision.HIGHEST, preferred_element_type=F32) for k in range(2)]
    return jnp.concatenate(parts, axis=1)


def _ssm_bwd(proj, dy, hc, lam_r, lam_i, pw_r, pw_i, w_in, w_out, d_skip):
    t = proj.shape[0]
    seg_len = t // N_SEG
    nch = t // SCAN_ROWS
    steps = SCAN_ROWS // N_SEG
    sp = _ssm_specs(t)

    def body(u_ref, dy_ref, hc_ref, lr_ref, li_ref, pr_ref, pi_ref, wi_ref, wo_ref, d_ref,
             du_ref, gbi_ref, gbo_ref, glam_ref, gd_ref, bu_s, h_s, e_s, car_s, seg_s, acc_s, gwi_ref, gwo_ref,
             dyb_s):
        perm = _chunk_perm()

        def chunk_rows(j):
            return pl.ds(pl.multiple_of(j * SCAN_ROWS, SCAN_ROWS), SCAN_ROWS)

        def interleaved(ref, j):
            return _interleave(perm, _gather_chunk(ref, j, seg_len).astype(BF16))

        def load_e(j):
            dy_b = interleaved(dy_ref, j)
            dyb_s[chunk_rows(j), :] = dy_b
            for cols in _state_pieces():
                e_s[:, cols] = _dot(dy_b, wo_ref[:, cols])

        def scan_rev(j, accumulate):
            for k in range(SSM_ST // SCAN_LW):
                re = pl.ds(k * SCAN_LW, SCAN_LW)
                im = pl.ds(SSM_ST + k * SCAN_LW, SCAN_LW)
                lr, li = lr_ref[:, re], li_ref[:, re]

                def step(ii, carry, re=re, im=im, lr=lr, li=li):
                    i = steps - 1 - ii
                    rows = pl.ds(pl.multiple_of(i * 8, 8), 8)
                    if accumulate:
                        gr, gi, ar, ai = carry
                    else:
                        gr, gi = carry
                    nr = lr * gr + li * gi + e_s[rows, re]
                    ni = lr * gi - li * gr + e_s[rows, im]
                    if not accumulate:
                        return nr, ni
                    e_s[rows, re] = nr
                    e_s[rows, im] = ni
                    pr_, pi_ = h_s[rows, re], h_s[rows, im]
                    return nr, ni, ar + nr * pr_ + ni * pi_, ai + ni * pr_ - nr * pi_

                init = (car_s[:, re], car_s[:, im])
                if accumulate:
                    init = init + (acc_s[:, re], acc_s[:, im])
                out = lax.fori_loop(0, steps, step, init, unroll=True)
                car_s[:, re] = out[0]
                car_s[:, im] = out[1]
                if accumulate:
                    acc_s[:, re] = out[2]
                    acc_s[:, im] = out[3]

        car_s[...] = jnp.zeros_like(car_s)

        def pass1(jj, c):
            load_e(nch - 1 - jj)
            scan_rev(nch - 1 - jj, False)
            return c

        lax.fori_loop(0, nch, pass1, 0)
        _segment_states(car_s, pr_ref, pi_ref, seg_s, reverse=True)
        car_s[...] = seg_s[...]
        acc_s[...] = jnp.zeros_like(acc_s)
        gwi_ref[...] = jnp.zeros_like(gwi_ref)
        gwo_ref[...] = jnp.zeros_like(gwo_ref)
        gd_ref[...] = jnp.zeros_like(gd_ref)

        def pass2(jj, c):
            j = nch - 1 - jj
            u_b, dy_b = interleaved(u_ref, j), dyb_s[chunk_rows(j), :]
            _expand_states(u_b, wi_ref, bu_s)
            h_s[0:N_SEG, :] = hc_ref[j]
            seg_s[...] = hc_ref[j]
            h_now = h_s.at[pl.ds(N_SEG, SCAN_ROWS), :]
            _scan_fwd(bu_s, h_now, lr_ref, li_ref, seg_s, seg_s, steps)
            for cols in _state_pieces():
                e_s[:, cols] = _dot(dy_b, wo_ref[:, cols])
            scan_rev(j, True)
            du = d_ref[...] * dy_b.astype(F32)
            for cols in _state_pieces():
                g_b = e_s[:, cols].astype(BF16)
                du = du + _dot(g_b, wi_ref[cols, :])
                gwi_ref[:, cols] += _dot_tn(u_b, g_b)
                gwo_ref[:, cols] += _dot_tn(dy_b, h_now[:, cols].astype(BF16))
            _scatter_chunk(du_ref, j, seg_len, _dot_tn(perm, du.astype(BF16)).astype(du_ref.dtype))
            gd_ref[...] += jnp.sum(_gather_chunk(dy_ref, j, seg_len) * _gather_chunk(u_ref, j, seg_len), axis=0,
                                   keepdims=True)
            return c

        lax.fori_loop(0, nch, pass2, 0)
        glam_ref[...] = jnp.sum(acc_s[...], axis=0, keepdims=True)
        gbi_ref[...] = _group_blocks(gwi_ref[...])
        gbo_ref[...] = _group_blocks(gwo_ref[...])

    mat = pl.BlockSpec((None, SSM_CH, 2 * SSM_P), lambda g: (g, 0, 0))
    u_cols = 2560 // SSM_CH
    return pl.pallas_call(
        body, name="ssm_bwd", grid=(SSM_GB,),
        in_specs=[pl.BlockSpec((t, SSM_CH), lambda g: (0, u_cols + g)), sp["rows"],
                  pl.BlockSpec((None, nch, N_SEG, 2 * SSM_ST), lambda g: (g, 0, 0, 0)),
                  sp["lam"], sp["lam"], sp["lam"], sp["lam"], sp["w_in"], sp["w_out"], sp["vec"]],
        out_specs=[sp["rows"], mat, mat, pl.BlockSpec((None, 1, 2 * SSM_ST), lambda g: (g, 0, 0)), sp["vec"]],
        out_shape=[jax.ShapeDtypeStruct((t, SSM_W), BF16), jax.ShapeDtypeStruct((SSM_GB, SSM_CH, 2 * SSM_P), F32),
                   jax.ShapeDtypeStruct((SSM_GB, SSM_CH, 2 * SSM_P), F32),
                   jax.ShapeDtypeStruct((SSM_GB, 1, 2 * SSM_ST), F32), jax.ShapeDtypeStruct((1, SSM_W), F32)],
        scratch_shapes=[pltpu.VMEM((SCAN_ROWS, 2 * SSM_ST), F32), pltpu.VMEM((SCAN_ROWS + N_SEG, 2 * SSM_ST), F32),
                        pltpu.VMEM((SCAN_ROWS, 2 * SSM_ST), F32), pltpu.VMEM((N_SEG, 2 * SSM_ST), F32),
                        pltpu.VMEM((N_SEG, 2 * SSM_ST), F32), pltpu.VMEM((N_SEG, 2 * SSM_ST), F32),
                        pltpu.VMEM((SSM_CH, 2 * SSM_ST), F32), pltpu.VMEM((SSM_CH, 2 * SSM_ST), F32),
                        pltpu.VMEM((t, SSM_CH), BF16)],
        compiler_params=_params(("parallel",)),
    )(proj, dy, hc, lam_r, lam_i, pw_r, pw_i, w_in, w_out, d_skip)


def _z_ssm_specs(tm):
    return [pl.BlockSpec((tm, 512), lambda i: (i, 7)), pl.BlockSpec((tm, 512), lambda i: (i, 8))]


def _glu_fwd(y, proj, w_glu, b_glu):
    t = y.shape[0]
    tm = 512

    def body(y_ref, z0_ref, z1_ref, w_ref, b_ref, o_ref, yg_ref):
        yg = _gelu(y_ref[...])
        yg_b = yg.astype(BF16)
        a = _dot(yg_b, w_ref[...]) + b_ref[...]
        z = jnp.concatenate([z0_ref[...], z1_ref[...]], axis=1)
        o_ref[...] = yg * _sigmoid(a) * _silu(z)
        yg_ref[...] = yg_b

    row = pl.BlockSpec((tm, SSM_W), lambda i: (i, 0))
    return pl.pallas_call(
        body, name="glu_fwd", grid=(t // tm,),
        in_specs=[row] + _z_ssm_specs(tm) + [pl.BlockSpec((SSM_W, SSM_W), lambda i: (0, 0)),
                                            pl.BlockSpec((1, SSM_W), lambda i: (0, 0))],
        out_specs=[row, row],
        out_shape=[jax.ShapeDtypeStruct((t, SSM_W), F32), jax.ShapeDtypeStruct((t, SSM_W), BF16)],
        compiler_params=_params(("parallel",)),
    )(y, proj, proj, w_glu, b_glu)


def _glu_bwd(y, proj, dos, w_glu, b_glu):
    t = y.shape[0]
    tm = 512

    def body(y_ref, z0_ref, z1_ref, do_ref, w_ref, b_ref, dy_ref, dz_ref, da_ref, gb_ref):
        @pl.when(pl.program_id(0) == 0)
        def _():
            gb_ref[...] = jnp.zeros_like(gb_ref)

        z = jnp.concatenate([z0_ref[...], z1_ref[...]], axis=1)
        yv, do = y_ref[...], do_ref[...]
        yg = _gelu(yv)
        sg = _sigmoid(_dot(yg.astype(BF16), w_ref[...]) + b_ref[...])
        dy2 = do * _silu(z)
        dz_ref[...] = (do * yg * sg * _dsilu(z)).astype(BF16)
        da = dy2 * yg * sg * (1.0 - sg)
        da_b = da.astype(BF16)
        da_ref[...] = da_b
        gb_ref[...] += jnp.sum(da, axis=0, keepdims=True)
        dyg = dy2 * sg + _dot_nt(da_b, w_ref[...])
        dy_ref[...] = dyg * _dgelu(yv)

    row = pl.BlockSpec((tm, SSM_W), lambda i: (i, 0))
    vec = pl.BlockSpec((1, SSM_W), lambda i: (0, 0))
    return pl.pallas_call(
        body, name="glu_bwd", grid=(t // tm,),
        in_specs=[row] + _z_ssm_specs(tm) + [row, pl.BlockSpec((SSM_W, SSM_W), lambda i: (0, 0)), vec],
        out_specs=[row, row, row, vec],
        out_shape=[jax.ShapeDtypeStruct((t, SSM_W), F32), jax.ShapeDtypeStruct((t, SSM_W), BF16),
                   jax.ShapeDtypeStruct((t, SSM_W), BF16), jax.ShapeDtypeStruct((1, SSM_W), F32)],
        compiler_params=_params(("arbitrary",)),
    )(y, proj, proj, dos, w_glu, b_glu)


def _rms(o):
    return lax.rsqrt(jnp.mean(o * o, axis=1, keepdims=True) + NORM_EPS)


def _outproj(oa, os_, aw, sw, w_out, x, target):
    t = x.shape[0]
    tm = 256

    def body(oa_ref, os_ref, aw_ref, sw_ref, w_ref, x_ref, t_ref, mg_ref, do_ref, ls_ref):
        @pl.when(pl.program_id(0) == 0)
        def _():
            ls_ref[...] = jnp.zeros_like(ls_ref)

        a, s = oa_ref[...], os_ref[...]
        merged = jnp.concatenate([a * _rms(a) * aw_ref[...], s * _rms(s) * sw_ref[...]], axis=1).astype(BF16)
        mg_ref[...] = merged
        err = x_ref[...] + _dot(merged, w_ref[...]) - t_ref[...]
        do_ref[...] = err * (1.0 / D_MODEL)
        ls_ref[...] += jnp.sum(err * err)

    half = pl.BlockSpec((tm, ATTN_W), lambda i: (i, 0))
    full = pl.BlockSpec((tm, D_MODEL), lambda i: (i, 0))
    vec = pl.BlockSpec((1, ATTN_W), lambda i: (0, 0))
    return pl.pallas_call(
        body, name="outproj", grid=(t // tm,),
        in_specs=[half, half, vec, vec, pl.BlockSpec((D_MODEL, D_MODEL), lambda i: (0, 0)), full, full],
        out_specs=[full, full, pl.BlockSpec((8, 128), lambda i: (0, 0))],
        out_shape=[jax.ShapeDtypeStruct((t, D_MODEL), BF16), jax.ShapeDtypeStruct((t, D_MODEL), F32),
                   jax.ShapeDtypeStruct((8, 128), F32)],
        compiler_params=_params(("arbitrary",)),
    )(oa, os_, aw, sw, w_out, x, target)


def _outproj_bwd(dout, oa, os_, aw, sw, w_out):
    t = dout.shape[0]
    tm = 256

    def norm_bwd(o, w, dm):
        r = _rms(o)
        yh = o * r
        gh = dm * w
        return r * (gh - yh * jnp.mean(gh * yh, axis=1, keepdims=True)), jnp.sum(dm * yh, axis=0, keepdims=True)

    def body(do_ref, oa_ref, os_ref, aw_ref, sw_ref, w_ref, da_ref, ds_ref, ga_ref, gs_ref):
        @pl.when(pl.program_id(0) == 0)
        def _():
            ga_ref[...] = jnp.zeros_like(ga_ref)
            gs_ref[...] = jnp.zeros_like(gs_ref)

        dm = _dot_nt(do_ref[...].astype(BF16), w_ref[...])
        da, ga = norm_bwd(oa_ref[...], aw_ref[...], dm[:, :ATTN_W])
        ds, gs = norm_bwd(os_ref[...], sw_ref[...], dm[:, ATTN_W:])
        da_ref[...] = da
        ds_ref[...] = ds
        ga_ref[...] += ga
        gs_ref[...] += gs

    half = pl.BlockSpec((tm, ATTN_W), lambda i: (i, 0))
    full = pl.BlockSpec((tm, D_MODEL), lambda i: (i, 0))
    vec = pl.BlockSpec((1, ATTN_W), lambda i: (0, 0))
    return pl.pallas_call(
        body, name="outproj_bwd", grid=(t // tm,),
        in_specs=[full, half, half, vec, vec, pl.BlockSpec((D_MODEL, D_MODEL), lambda i: (0, 0))],
        out_specs=[half, half, vec, vec],
        out_shape=[jax.ShapeDtypeStruct((t, ATTN_W), F32), jax.ShapeDtypeStruct((t, ATTN_W), F32),
                   jax.ShapeDtypeStruct((1, ATTN_W), F32), jax.ShapeDtypeStruct((1, ATTN_W), F32)],
        compiler_params=_params(("arbitrary",)),
    )(dout, oa, os_, aw, sw, w_out)


def _inproj_bwd(dproj, w_slabs, x, norm_w, dout, outgoing):
    t = x.shape[0]
    tm = 512
    nc = 4
    nt = len(outgoing)
    ni = t // tm

    def body(dp_ref, w_ref, x_ref, nw_ref, do_ref, *rest):
        src, (gx_ref, gw_ref), dst = rest[:nt], rest[nt:nt + 2], rest[nt + 2:2 * nt + 2]
        acc_ref, ssem, rsem = rest[2 * nt + 2:]
        i, j = pl.program_id(0), pl.program_id(1)
        start, wait = _bg_scatter_chips(src, dst, ssem, rsem)

        @pl.when((i == 0) & (j == 0))
        def _():
            start()
            gw_ref[...] = jnp.zeros_like(gw_ref)

        @pl.when(j == 0)
        def _():
            acc_ref[...] = jnp.zeros_like(acc_ref)

        acc_ref[...] += _dot_nt(dp_ref[...], w_ref[...])

        @pl.when(j == nc - 1)
        def _():
            xv = x_ref[...]
            r = lax.rsqrt(jnp.mean(xv * xv, axis=1, keepdims=True) + NORM_EPS)
            yh = xv * r
            dh = acc_ref[...]
            gh = dh * nw_ref[...]
            gx_ref[...] = do_ref[...] + r * (gh - yh * jnp.mean(gh * yh, axis=1, keepdims=True))
            gw_ref[...] += jnp.sum(dh * yh, axis=0, keepdims=True)

        @pl.when((i == ni - 1) & (j == nc - 1))
        def _():
            wait()

    full = pl.BlockSpec((tm, D_MODEL), lambda i, j: (i, 0))
    vec = pl.BlockSpec((1, D_MODEL), lambda i, j: (0, 0))
    return pl.pallas_call(
        body, name="inproj_bwd", grid=(ni, nc),
        in_specs=[pl.BlockSpec((tm, SHARD_W), lambda i, j: (i, j)),
                  pl.BlockSpec((None, D_MODEL, SHARD_W), lambda i, j: (j, 0, 0)), full, vec, full] + [ANY] * nt,
        out_specs=[full, vec] + [ANY] * nt,
        out_shape=[jax.ShapeDtypeStruct((t, D_MODEL), F32), jax.ShapeDtypeStruct((1, D_MODEL), F32)]
        + [jax.ShapeDtypeStruct(a.shape, a.dtype) for a in outgoing],
        scratch_shapes=[pltpu.VMEM((tm, D_MODEL), F32), pltpu.SemaphoreType.DMA((3 * nt,)),
                        pltpu.SemaphoreType.DMA((3 * nt,))],
        compiler_params=_params(("arbitrary", "arbitrary")),
    )(dproj, w_slabs, x, norm_w.reshape(1, D_MODEL), dout, *outgoing)


def _adamw_math(w_ref, g_ref, m_ref, v_ref, d_ref, nm_ref, nv_ref):
    gv = g_ref[...]
    nm = ADAM_B1 * m_ref[...] + (1.0 - ADAM_B1) * gv
    nv = ADAM_B2 * v_ref[...] + (1.0 - ADAM_B2) * (gv * gv)
    m_hat = nm / (1.0 - ADAM_B1 ** ADAM_STEP)
    v_hat = nv / (1.0 - ADAM_B2 ** ADAM_STEP)
    d_ref[...] = -ADAM_LR * (m_hat / (jnp.sqrt(v_hat) + ADAM_EPS) + ADAM_WD * w_ref[...])
    nm_ref[...] = nm
    nv_ref[...] = nv


def _adamw_halves(w, mine, theirs, m, v, c_idx, *, rows, name):
    hr, cols = mine.shape
    nblk = hr // rows

    def body(c_ref, w_ref, a_ref, b_ref, m_ref, v_ref, g_ref, d_ref, nm_ref, nv_ref):
        g_ref[...] = jnp.where(pl.program_id(0) == c_ref[0], a_ref[...], b_ref[...])
        _adamw_math(w_ref, g_ref, m_ref, v_ref, d_ref, nm_ref, nv_ref)

    full = pl.BlockSpec((rows, cols), lambda h, i, c: (h * nblk + i, 0))
    part = pl.BlockSpec((rows, cols), lambda h, i, c: (i, 0))
    shp = jax.ShapeDtypeStruct((2 * hr, cols), F32)
    return pl.pallas_call(
        body, name=name,
        grid_spec=pltpu.PrefetchScalarGridSpec(num_scalar_prefetch=1, grid=(2, nblk),
                                               in_specs=[full, part, part, full, full], out_specs=[full] * 4),
        out_shape=[shp] * 4, compiler_params=_params(("parallel", "parallel")),
    )(c_idx, w, mine, theirs, m, v)


def _adamw(w, g, m, v, *, rows, name):
    r, c = w.shape

    def body(w_ref, g_ref, m_ref, v_ref, d_ref, nm_ref, nv_ref):
        _adamw_math(w_ref, g_ref, m_ref, v_ref, d_ref, nm_ref, nv_ref)

    blk = pl.BlockSpec((rows, c), lambda i: (i, 0))
    shp = jax.ShapeDtypeStruct((r, c), F32)
    return pl.pallas_call(body, name=name, grid=(r // rows,), in_specs=[blk] * 4, out_specs=[blk] * 3,
                          out_shape=[shp] * 3, compiler_params=_params(("parallel",)))(w, g, m, v)


def _remote(src, dst, ssem, rsem, dev):
    return pltpu.make_async_remote_copy(src_ref=src, dst_ref=dst, send_sem=ssem, recv_sem=rsem, device_id=dev,
                                        device_id_type=pl.DeviceIdType.MESH)


def _mesh_pos():
    return lax.axis_index("x"), lax.axis_index("y"), lax.axis_index("c")


def _other_chips(x, y):
    return [(1 - x, y), (x, 1 - y), (1 - x, 1 - y)]


def _flips():
    return [(dx, dy, dc) for dx in (0, 1) for dy in (0, 1) for dc in (0, 1) if (dx, dy, dc) != (0, 0, 0)]


def _background(sends, arrivals):
    def start():
        for cp in sends():
            cp.start()

    def wait():
        for cp in arrivals():
            cp.wait_recv()
        for cp in sends():
            cp.wait_send()

    return start, wait


def _bg_gather(sh, full, ssem, rsem):
    x, y, c = _mesh_pos()
    me = 2 * x + y
    peers = [(px, py, c) for px, py in _other_chips(x, y)] + [(x, y, 1 - c)]
    slots = [2 * px + py for px, py in _other_chips(x, y)] + [me]
    pairs = [(i, k) for i in range(len(sh)) for k in range(4)]
    return _background(
        lambda: [_remote(sh[i], full[i].at[me], ssem.at[4 * i + k], rsem.at[4 * i + k], peers[k]) for i, k in pairs],
        lambda: [_remote(full[i].at[slots[k]], full[i].at[slots[k]], ssem.at[4 * i + k], rsem.at[4 * i + k], peers[k])
                 for i, k in pairs])


def _bg_scatter_devices(src, dst, ssem, rsem):
    x, y, c = _mesh_pos()
    me = 4 * x + 2 * y + c
    peers = []
    for dx, dy, dc in _flips():
        px, py, pc = jnp.bitwise_xor(x, dx), jnp.bitwise_xor(y, dy), jnp.bitwise_xor(c, dc)
        peers.append(((px, py, pc), 4 * px + 2 * py + pc))
    pairs = [(i, k) for i in range(len(src)) for k in range(7)]
    return _background(
        lambda: [_remote(src[i].at[peers[k][1]], dst[i].at[me], ssem.at[7 * i + k], rsem.at[7 * i + k], peers[k][0])
                 for i, k in pairs],
        lambda: [_remote(dst[i].at[peers[k][1]], dst[i].at[peers[k][1]], ssem.at[7 * i + k], rsem.at[7 * i + k],
                         peers[k][0]) for i, k in pairs])


def _bg_scatter_chips(src, dst, ssem, rsem):
    x, y, c = _mesh_pos()
    me = 2 * x + y
    chips = _other_chips(x, y)
    pairs = [(i, k) for i in range(len(src)) for k in range(3)]
    slot = lambda k: 2 * chips[k][0] + chips[k][1]
    return _background(
        lambda: [_remote(src[i].at[slot(k)], dst[i].at[me], ssem.at[3 * i + k], rsem.at[3 * i + k], (*chips[k], c))
                 for i, k in pairs],
        lambda: [_remote(dst[i].at[slot(k)], dst[i].at[slot(k)], ssem.at[3 * i + k], rsem.at[3 * i + k], (*chips[k], c))
                 for i, k in pairs])


def _pair_swap(arrays):
    nt = len(arrays)

    def body(*refs):
        src, dst = refs[:nt], refs[nt:2 * nt]
        ssem, rsem = refs[2 * nt:]
        x, y, c = _mesh_pos()
        cps = [_remote(src[i].at[:, 1 - c], dst[i], ssem.at[i], rsem.at[i], (x, y, 1 - c)) for i in range(nt)]
        for cp in cps:
            cp.start()
        for cp in cps:
            cp.wait_recv()
        for cp in cps:
            cp.wait_send()

    return pl.pallas_call(
        body, name="pair_swap", in_specs=[ANY] * nt, out_specs=[ANY] * nt,
        out_shape=[jax.ShapeDtypeStruct((4,) + a.shape[2:], a.dtype) for a in arrays],
        scratch_shapes=[pltpu.SemaphoreType.DMA((nt,)), pltpu.SemaphoreType.DMA((nt,))],
    )(*arrays)


def _half_swap(arrays):
    nt = len(arrays)

    def body(*refs):
        src, dst = refs[:nt], refs[nt:2 * nt]
        ssem, rsem = refs[2 * nt:]
        x, y, c = _mesh_pos()
        cps = [_remote(src[i], dst[i], ssem.at[i], rsem.at[i], (x, y, 1 - c)) for i in range(nt)]
        for cp in cps:
            cp.start()
        for cp in cps:
            cp.wait_recv()
        for cp in cps:
            cp.wait_send()

    return pl.pallas_call(
        body, name="half_swap", in_specs=[ANY] * nt, out_specs=[ANY] * nt,
        out_shape=[jax.ShapeDtypeStruct(a.shape, a.dtype) for a in arrays],
        scratch_shapes=[pltpu.SemaphoreType.DMA((nt,)), pltpu.SemaphoreType.DMA((nt,))],
    )(*arrays)


def _exchange_slices(src, scatter, name):
    def body(src_ref, dst_ref, ssem, rsem, lsem):
        x, y, c = _mesh_pos()
        me = 4 * x + 2 * y + c
        local = pltpu.make_async_copy(src_ref.at[me] if scatter else src_ref, dst_ref.at[me], lsem)
        local.start()
        cps = []
        for k, (dx, dy, dc) in enumerate(_flips()):
            px, py, pc = jnp.bitwise_xor(x, dx), jnp.bitwise_xor(y, dy), jnp.bitwise_xor(c, dc)
            peer = 4 * px + 2 * py + pc
            cp = _remote(src_ref.at[peer] if scatter else src_ref, dst_ref.at[me], ssem.at[k], rsem.at[k],
                         (px, py, pc))
            cp.start()
            cps.append((cp, peer))
        for k, (cp, peer) in enumerate(cps):
            slot = dst_ref.at[peer]
            _remote(slot, slot, ssem.at[k], rsem.at[k], (x, y, c)).wait_recv()
        for cp, _ in cps:
            cp.wait_send()
        local.wait()

    return pl.pallas_call(
        body, name=name, in_specs=[ANY], out_specs=ANY,
        out_shape=jax.ShapeDtypeStruct((8,) + src.shape[-2:], src.dtype),
        scratch_shapes=[pltpu.SemaphoreType.DMA((7,)), pltpu.SemaphoreType.DMA((7,)), pltpu.SemaphoreType.DMA],
    )(src)


def _add_halves(g, recv, c_idx, *, rows, name):
    _, _, hr, cols = g.shape

    def body(c_ref, g_ref, r_ref, o_ref):
        o_ref[...] = (g_ref[...] + r_ref[...].astype(F32)).astype(BF16)

    return pl.pallas_call(
        body, name=name,
        grid_spec=pltpu.PrefetchScalarGridSpec(
            num_scalar_prefetch=1, grid=(4, hr // rows),
            in_specs=[pl.BlockSpec((None, None, rows, cols), lambda j, i, c: (j, c[0], i, 0)),
                      pl.BlockSpec((None, rows, cols), lambda j, i, c: (j, i, 0))],
            out_specs=pl.BlockSpec((None, rows, cols), lambda j, i, c: (j, i, 0))),
        out_shape=jax.ShapeDtypeStruct((4, hr, cols), BF16),
        compiler_params=_params(("parallel", "parallel")),
    )(c_idx, g, recv)


def _sum_peers(slots, own, idx, *, rows, name):
    n, r, cols = slots.shape

    def body(me_ref, *refs):
        me = me_ref[0]
        mine = refs[n][...].astype(F32)
        acc = None
        for k in range(n):
            term = jnp.where(me == k, mine, refs[k][...].astype(F32))
            acc = term if acc is None else acc + term
        refs[n + 1][...] = acc

    def slot_spec(k):
        return pl.BlockSpec((None, rows, cols), lambda i, me: (jnp.where(me[0] == k, (k + 1) % n, k), i, 0))

    return pl.pallas_call(
        body, name=name,
        grid_spec=pltpu.PrefetchScalarGridSpec(
            num_scalar_prefetch=1, grid=(r // rows,),
            in_specs=[slot_spec(k) for k in range(n)] + [pl.BlockSpec((None, rows, cols), lambda i, me: (me[0], i, 0))],
            out_specs=pl.BlockSpec((rows, cols), lambda i, me: (i, 0))),
        out_shape=jax.ShapeDtypeStruct((r, cols), F32),
        compiler_params=_params(("parallel",)),
    )(idx, *([slots] * n), own)


def _sum_slots(slots, *, rows, name):
    n, r, cols = slots.shape

    def body(s_ref, o_ref):
        acc = s_ref[0].astype(F32)
        for k in range(1, n):
            acc = acc + s_ref[k].astype(F32)
        o_ref[...] = acc

    return pl.pallas_call(
        body, name=name, grid=(r // rows,),
        in_specs=[pl.BlockSpec((n, rows, cols), lambda i: (0, i, 0))],
        out_specs=pl.BlockSpec((rows, cols), lambda i: (i, 0)),
        out_shape=jax.ShapeDtypeStruct((r, cols), F32),
        compiler_params=_params(("parallel",)),
    )(slots)


def _pack_small(d, names, rows):
    flat = jnp.concatenate([d[n].astype(F32).reshape(-1) for n in names])
    return jnp.pad(flat, (0, rows * 128 - flat.shape[0])).reshape(rows, 128)


def _unpack_small(p, names):
    flat = p.reshape(-1)
    out, off = {}, 0
    for n in names:
        size = math.prod(SMALL_SHAPES[n])
        out[n] = flat[off:off + size].reshape(SMALL_SHAPES[n])
        off += size
    return out


def _adamw_3d(w, g, m, v, *, name):
    def body(w_ref, g_ref, m_ref, v_ref, d_ref, nm_ref, nv_ref):
        _adamw_math(w_ref, g_ref, m_ref, v_ref, d_ref, nm_ref, nv_ref)

    blk = pl.BlockSpec((8,) + w.shape[1:], lambda i: (i, 0, 0))
    shp = jax.ShapeDtypeStruct(w.shape, F32)
    return pl.pallas_call(body, name=name, grid=(w.shape[0] // 8,), in_specs=[blk] * 4, out_specs=[blk] * 3,
                          out_shape=[shp] * 3, compiler_params=_params(("parallel",)))(w, g, m, v)


def kernel(x, positions, norm_w, w_in, q_norm_w, k_norm_w, sinks, a_re, a_im, log_step, b_re, b_im, c_re, c_im, d_skip, w_glu, b_glu, attn_out_norm_w, ssm_out_norm_w, w_out, loss_target, m_norm_w, m_w_in, m_q_norm_w, m_k_norm_w, m_sinks, m_a_re, m_a_im, m_log_step, m_b_re, m_b_im, m_c_re, m_c_im, m_d_skip, m_w_glu, m_b_glu, m_attn_out_norm_w, m_ssm_out_norm_w, m_w_out, v_norm_w, v_w_in, v_q_norm_w, v_k_norm_w, v_sinks, v_a_re, v_a_im, v_log_step, v_b_re, v_b_im, v_c_re, v_c_im, v_d_skip, v_w_glu, v_b_glu, v_attn_out_norm_w, v_ssm_out_norm_w, v_w_out):
    small_w = dict(norm_w=norm_w, q_norm_w=q_norm_w, k_norm_w=k_norm_w, sinks=sinks, a_re=a_re, a_im=a_im,
                   log_step=log_step, b_re=b_re, b_im=b_im, c_re=c_re, c_im=c_im, d_skip=d_skip, b_glu=b_glu,
                   attn_out_norm_w=attn_out_norm_w, ssm_out_norm_w=ssm_out_norm_w)
    small_m = dict(norm_w=m_norm_w, q_norm_w=m_q_norm_w, k_norm_w=m_k_norm_w, sinks=m_sinks, a_re=m_a_re, a_im=m_a_im,
                   log_step=m_log_step, b_re=m_b_re, b_im=m_b_im, c_re=m_c_re, c_im=m_c_im, d_skip=m_d_skip,
                   b_glu=m_b_glu, attn_out_norm_w=m_attn_out_norm_w, ssm_out_norm_w=m_ssm_out_norm_w)
    small_v = dict(norm_w=v_norm_w, q_norm_w=v_q_norm_w, k_norm_w=v_k_norm_w, sinks=v_sinks, a_re=v_a_re, a_im=v_a_im,
                   log_step=v_log_step, b_re=v_b_re, b_im=v_b_im, c_re=v_c_re, c_im=v_c_im, d_skip=v_d_skip,
                   b_glu=v_b_glu, attn_out_norm_w=v_attn_out_norm_w, ssm_out_norm_w=v_ssm_out_norm_w)
    c_idx = lax.axis_index("c").astype(jnp.int32).reshape(1)
    chip_idx = (2 * lax.axis_index("x") + lax.axis_index("y")).astype(jnp.int32).reshape(1)
    dev_idx = 2 * chip_idx + c_idx

    xs = x[0]
    tgt = loss_target[0]
    t = xs.shape[0]
    posf = positions[0].astype(F32).reshape(t, 1)

    mx, my = lax.axis_index("x"), lax.axis_index("y")
    slab_order = jnp.stack([2 * mx + my, 2 * (1 - mx) + my, 2 * mx + (1 - my), 2 * (1 - mx) + (1 - my)]).astype(jnp.int32)
    proj, hn, w_in_all = _inproj(xs, norm_w, w_in.astype(BF16), slab_order)
    inv_freq = ROPE_THETA ** (-jnp.arange(0, HEAD_DIM, 2, dtype=F32) / HEAD_DIM)
    rope = _rope_table(posf, jnp.tile(inv_freq, 4).reshape(1, 128))
    qw = jnp.tile(q_norm_w, 2).reshape(1, 128)
    kw = jnp.tile(k_norm_w, 2).reshape(1, 128)
    sink_row = sinks.reshape(1, N_HEADS)
    oa, w_glu_all, w_out_all = _attn_fwd(proj, rope, qw, kw, sink_row, [w_glu.astype(BF16), w_out.astype(BF16)])
    w_glu_b = w_glu_all.reshape(SSM_W, SSM_W)
    w_out_b = w_out_all.reshape(D_MODEL, D_MODEL)

    lam_r, lam_i, pw_r, pw_i, bb_r, bb_i = _ssm_prep(a_re, a_im, log_step, b_re, b_im, t // N_SEG)
    rows8 = lambda a: jnp.broadcast_to(a.reshape(SSM_GB, 1, SSM_ST), (SSM_GB, N_SEG, SSM_ST))
    lam_r8, lam_i8, pw_r8, pw_i8 = rows8(lam_r), rows8(lam_i), rows8(pw_r), rows8(pw_i)
    ssm_w_in = jnp.concatenate([_block_diag_in(bb_r), _block_diag_in(bb_i)], axis=1).astype(BF16)
    ssm_w_out = jnp.concatenate([_block_diag_out(c_re), _block_diag_out(-c_im)], axis=2).astype(BF16)
    d_row = d_skip.reshape(1, SSM_W)
    y, hc = _ssm_fwd(proj, lam_r8, lam_i8, pw_r8, pw_i8, ssm_w_in, ssm_w_out, d_row)
    b_glu_row = b_glu.reshape(1, SSM_W)
    os_, yg = _glu_fwd(y, proj, w_glu_b, b_glu_row)
    aw = attn_out_norm_w.reshape(1, ATTN_W)
    sw = ssm_out_norm_w.reshape(1, SSM_W)
    merged, dout, sq_err = _outproj(oa, os_, aw, sw, w_out_b, xs, tgt)
    loss = lax.psum(0.5 * sq_err[0, 0] / D_MODEL, MESH_AXES)

    doa, dos, g_aw, g_sw = _outproj_bwd(dout, oa, os_, aw, sw, w_out_b)
    dout_b = dout.astype(BF16)
    (g_w_out_b,) = _matmul_tn(merged, dout_b, tm=512, tn=1024, name="grad_w_out", dtypes=(BF16,))
    dy, dzs, da, g_b_glu = _glu_bwd(y, proj, dos, w_glu_b, b_glu_row)
    (g_w_glu_b,) = _matmul_tn(yg, da, tm=512, tn=1024, name="grad_w_glu", dtypes=(BF16,))
    du, g_wi, g_wo, g_lam, g_d = _ssm_bwd(proj, dy, hc, lam_r8, lam_i8, pw_r8, pw_i8, ssm_w_in, ssm_w_out, d_row)
    early = [g_w_glu_b.reshape(8, 128, SSM_W), g_w_out_b.reshape(8, 256, D_MODEL)]
    dproj, g_qw, g_kw, g_sink, *early_slots = _attn_bwd(proj, rope, qw, kw, sink_row, doa, du, dzs, early)
    g_w_in, g_w_in_b = _matmul_tn(hn, dproj, tm=512, tn=SHARD_W, name="grad_w_in", slabs=True)
    in_shape = (4, 2, D_MODEL // 2, SHARD_W)
    (from_sib,) = _pair_swap([g_w_in_b.reshape(in_shape)])
    pair_in = _add_halves(g_w_in.reshape(in_shape), from_sib, c_idx, rows=512, name="pair_sum")
    grad_x, g_nw, in_slots = _inproj_bwd(dproj, w_in_all, xs, norm_w, dout, [pair_in])

    g_wi = g_wi.reshape(SSM_G, SSM_H, 2 * SSM_P)
    g_wo = g_wo.reshape(SSM_G, SSM_H, 2 * SSM_P)
    g_bb_r = g_wi[:, :, :SSM_P].transpose(0, 2, 1).reshape(SSM_G, SSM_P * SSM_H)
    g_bb_i = g_wi[:, :, SSM_P:].transpose(0, 2, 1).reshape(SSM_G, SSM_P * SSM_H)
    g_a_re, g_a_im, g_ls, g_b_re, g_b_im = _ssm_param_grads(
        a_re, a_im, log_step, b_re, b_im, g_lam[:, 0, :SSM_ST].reshape(SSM_G, SSM_P),
        g_lam[:, 0, SSM_ST:].reshape(SSM_G, SSM_P), g_bb_r, g_bb_i)
    small_g = dict(
        norm_w=g_nw, q_norm_w=g_qw[0, :64] + g_qw[0, 64:], k_norm_w=g_kw[0, :64] + g_kw[0, 64:],
        sinks=g_sink[0, :N_HEADS], a_re=g_a_re, a_im=g_a_im, log_step=g_ls, b_re=g_b_re, b_im=g_b_im,
        c_re=g_wo[:, :, :SSM_P], c_im=-g_wo[:, :, SSM_P:], d_skip=g_d,
        b_glu=g_b_glu, attn_out_norm_w=g_aw, ssm_out_norm_w=g_sw)

    mine = [_sum_peers(in_slots, pair_in, chip_idx, rows=512, name="sum_w_in"),
            _sum_peers(early_slots[0], early[0], dev_idx, rows=128, name="sum_w_glu"),
            _sum_peers(early_slots[1], early[1], dev_idx, rows=128, name="sum_w_out")]
    theirs = _half_swap(mine)
    packed = _pack_small(small_g, SMALL, 8 * PACK_ROWS).reshape(8, PACK_ROWS, 128)
    summed = _sum_slots(_exchange_slices(packed, True, "small_scatter"), rows=PACK_ROWS, name="small_sum")
    small_red = _exchange_slices(summed, False, "small_gather").reshape(8 * PACK_ROWS, 128)

    big = [_adamw_halves(w_in, mine[0], theirs[0], m_w_in, v_w_in, c_idx, rows=256, name="adamw_w_in"),
           _adamw_halves(w_glu, mine[1], theirs[1], m_w_glu, v_w_glu, c_idx, rows=128, name="adamw_w_glu"),
           _adamw_halves(w_out, mine[2], theirs[2], m_w_out, v_w_out, c_idx, rows=256, name="adamw_w_out")]
    g_in_sh, g_glu_sh, g_out_sh = (b[0] for b in big)
    upd = [b[1:] for b in big]
    grads = _unpack_small(small_red, SMALL)
    flat_first = sum(math.prod(SMALL_SHAPES[n]) for n in SMALL_3D) // 128
    sd, sm, sv = _adamw(_pack_small(small_w, SMALL_FLAT, FLAT_ROWS), small_red[flat_first:flat_first + FLAT_ROWS],
                        _pack_small(small_m, SMALL_FLAT, FLAT_ROWS), _pack_small(small_v, SMALL_FLAT, FLAT_ROWS),
                        rows=FLAT_ROWS, name="adamw_small")
    deltas, new_m, new_v = (_unpack_small(a, SMALL_FLAT) for a in (sd, sm, sv))
    for n in SMALL_3D:
        deltas[n], new_m[n], new_v[n] = _adamw_3d(small_w[n], grads[n], small_m[n], small_v[n], name="adamw_" + n)
    grads.update(w_in=g_in_sh, w_glu=g_glu_sh, w_out=g_out_sh)
    for n, (d, m_, v_) in zip(("w_in", "w_glu", "w_out"), upd):
        deltas[n], new_m[n], new_v[n] = d, m_, v_
    order = ["norm_w", "w_in", "q_norm_w", "k_norm_w", "sinks", "a_re", "a_im", "log_step", "b_re", "b_im", "c_re",
             "c_im", "d_skip", "w_glu", "b_glu", "attn_out_norm_w", "ssm_out_norm_w", "w_out"]
    return (loss, grad_x[None], *[grads[n] for n in order], *[deltas[n] for n in order],
            *[new_m[n] for n in order], *[new_v[n] for n in order])
```

```python
import math

import jax
import jax.numpy as jnp
from jax import lax
from jax.experimental import pallas as pl
from jax.experimental.pallas import tpu as pltpu

F32 = jnp.float32
BF16 = jnp.bfloat16

D_MODEL = 2048
ATTN_W = 1024
SSM_W = 1024
HEAD_DIM = 64
N_HEADS = 16
N_KV_HEADS = 4
KV_W = 256
BLOCK = 128
IN_W = 4608
SHARD_W = IN_W // 4
ROPE_THETA = 10000.0
SSM_H = 16
SSM_G = 64
SSM_P = 64
NORM_EPS = 1e-6
ADAM_LR = 0.001
ADAM_B1 = 0.9
ADAM_B2 = 0.999
ADAM_EPS = 1e-08
ADAM_WD = 0.01
ADAM_STEP = 10

N_SEG = 8
SSM_GB = 4
SSM_CH = 256
SSM_ST = 1024
SCAN_ROWS = 512
SCAN_LW = 256
VMEM_LIMIT = 56 * 1024 * 1024
MESH_AXES = ("x", "y", "c")
ANY = pl.BlockSpec(memory_space=pl.ANY)

SMALL_3D = ("b_re", "b_im", "c_re", "c_im")
SMALL_FLAT = ("norm_w", "q_norm_w", "k_norm_w", "sinks", "a_re", "a_im", "log_step", "d_skip", "b_glu",
              "attn_out_norm_w", "ssm_out_norm_w")
SMALL = SMALL_3D + SMALL_FLAT
SMALL_SHAPES = {"norm_w": (2048,), "q_norm_w": (64,), "k_norm_w": (64,), "sinks": (16,), "a_re": (64, 64),
                "a_im": (64, 64), "log_step": (64,), "b_re": (64, 64, 16), "b_im": (64, 64, 16),
                "c_re": (64, 16, 64), "c_im": (64, 16, 64), "d_skip": (1024,), "b_glu": (1024,),
                "attn_out_norm_w": (1024,), "ssm_out_norm_w": (1024,)}
PACK_ROWS = 272
FLAT_ROWS = 120


def _params(sem=None):
    return pltpu.CompilerParams(dimension_semantics=sem, vmem_limit_bytes=VMEM_LIMIT)


def _dot(a, b):
    return jnp.dot(a, b, preferred_element_type=F32)


def _dot_nt(a, b):
    return lax.dot_general(a, b, (((1,), (1,)), ((), ())), preferred_element_type=F32)


def _dot_tn(a, b):
    return lax.dot_general(a, b, (((0,), (0,)), ((), ())), preferred_element_type=F32)


def _sigmoid(x):
    return 1.0 / (1.0 + jnp.exp(-x))


def _silu(x):
    return x * _sigmoid(x)


def _dsilu(x):
    s = _sigmoid(x)
    return s * (1.0 + x * (1.0 - s))


_GELU_C = math.sqrt(2.0 / math.pi)


def _gelu(x):
    return 0.5 * x * (1.0 + jnp.tanh(_GELU_C * (x + 0.044715 * x * x * x)))


def _dgelu(x):
    t = jnp.tanh(_GELU_C * (x + 0.044715 * x * x * x))
    return 0.5 * (1.0 + t) + 0.5 * x * (1.0 - t * t) * _GELU_C * (1.0 + 3.0 * 0.044715 * x * x)


def _matmul_tn(a, b, *, tm, tn, name, slabs=False, dtypes=(F32, BF16)):
    k, m = a.shape
    _, n = b.shape

    def body(a_ref, b_ref, *o_refs):
        acc = _dot_tn(a_ref[...], b_ref[...])
        for o_ref in o_refs:
            o_ref[...] = acc.astype(o_ref.dtype)

    if slabs:
        out_spec = pl.BlockSpec((None, tm, tn), lambda j, i: (j, i, 0))
        shape = (n // tn, m, tn)
    else:
        out_spec = pl.BlockSpec((tm, tn), lambda j, i: (i, j))
        shape = (m, n)
    return pl.pallas_call(
        body, name=name, grid=(n // tn, m // tm),
        in_specs=[pl.BlockSpec((k, tm), lambda j, i: (0, i)), pl.BlockSpec((k, tn), lambda j, i: (0, j))],
        out_specs=[out_spec] * len(dtypes),
        out_shape=[jax.ShapeDtypeStruct(shape, d) for d in dtypes],
        compiler_params=_params(("parallel", "parallel")),
    )(a, b)


def _inproj(x, norm_w, w_sh, order):
    t = x.shape[0]
    tm = 512
    ni = t // tm
    hr = D_MODEL // 2

    def body(ord_ref, x_ref, nw_ref, sh_ref, proj_ref, hn_ref, full_ref, wbuf, hn_s, ssem, rsem, lsem):
        s, i = pl.program_id(0), pl.program_id(1)
        mx, my, c = _mesh_pos()
        me = 2 * mx + my
        sib = (mx, my, 1 - c)
        chips = _other_chips(mx, my)

        def half(which):
            return pl.ds(pl.multiple_of(which * hr, 8), hr)

        def slot(k):
            return 2 * chips[k][0] + chips[k][1]

        def ici(k):
            return _remote(sh_ref.at[half(c)], full_ref.at[me, half(c)], ssem.at[k], rsem.at[k], (*chips[k], c))

        def own():
            return _remote(sh_ref, full_ref.at[me], ssem.at[6], rsem.at[6], sib)

        def landed(k, which, sem):
            ref = full_ref.at[slot(k), half(which)]
            return _remote(ref, ref, ssem.at[sem], rsem.at[sem], sib)

        def fetch(src, b):
            return pltpu.make_async_copy(src, wbuf.at[b], lsem.at[b])

        @pl.when((s == 0) & (i == 0))
        def _():
            for k in range(3):
                ici(k).start()
            own().start()
            cp = fetch(sh_ref, 0)
            cp.start()
            cp.wait()

        for k in range(3):
            @pl.when((s == k) & (i == max(ni - 2, 0)))
            def _(k=k):
                landed(k, c, k).wait_recv()
                landed(k, c, 3 + k).start()
                landed(k, 1 - c, 3 + k).wait_recv()
                fetch(full_ref.at[slot(k)], (k + 1) % 2).start()

            @pl.when((s == k + 1) & (i == 0))
            def _(k=k):
                fetch(full_ref.at[slot(k)], (k + 1) % 2).wait()

        xv = x_ref[...]
        r = lax.rsqrt(jnp.mean(xv * xv, axis=1, keepdims=True) + NORM_EPS)
        hn = (xv * r * nw_ref[...]).astype(BF16)
        proj_ref[...] = _dot(hn, wbuf[s % 2])

        def hn_out(tile):
            return pltpu.make_async_copy(hn_s, hn_ref.at[pl.ds(pl.multiple_of(tile * tm, tm), tm), :], lsem.at[2])

        @pl.when(((s == 0) & (i > 0)) | ((s == 1) & (i == 0)))
        def _():
            hn_out(jnp.where(s == 0, i - 1, ni - 1)).wait()

        @pl.when(s == 0)
        def _():
            hn_s[...] = hn
            hn_out(i).start()

        @pl.when((s == 3) & (i == ni - 1))
        def _():
            mine = full_ref.at[me]
            _remote(mine, mine, ssem.at[6], rsem.at[6], sib).wait_recv()
            for k in range(3):
                ici(k).wait_send()
                landed(k, c, 3 + k).wait_send()
            own().wait_send()

    return pl.pallas_call(
        body, name="inproj",
        grid_spec=pltpu.PrefetchScalarGridSpec(
            num_scalar_prefetch=1, grid=(4, ni),
            in_specs=[pl.BlockSpec((tm, D_MODEL), lambda s, i, o: (i, 0)),
                      pl.BlockSpec((1, D_MODEL), lambda s, i, o: (0, 0)), ANY],
            out_specs=[pl.BlockSpec((tm, SHARD_W), lambda s, i, o: (i, o[s])), ANY, ANY],
            scratch_shapes=[pltpu.VMEM((2, D_MODEL, SHARD_W), BF16), pltpu.VMEM((tm, D_MODEL), BF16),
                            pltpu.SemaphoreType.DMA((7,)), pltpu.SemaphoreType.DMA((7,)),
                            pltpu.SemaphoreType.DMA((3,))]),
        out_shape=[jax.ShapeDtypeStruct((t, IN_W), F32), jax.ShapeDtypeStruct((t, D_MODEL), BF16),
                   jax.ShapeDtypeStruct((4, D_MODEL, SHARD_W), BF16)],
        compiler_params=_params(("arbitrary", "arbitrary")),
    )(order, x, norm_w.reshape(1, D_MODEL), w_sh)


def _lane128():
    return lax.broadcasted_iota(jnp.int32, (1, 128), 1)


def _head_sums(v):
    lo = _lane128() < 64
    s_lo = jnp.sum(jnp.where(lo, v, 0.0), axis=1, keepdims=True)
    s_hi = jnp.sum(jnp.where(lo, 0.0, v), axis=1, keepdims=True)
    return jnp.where(lo, s_lo, s_hi)


def _rot_half(t):
    first = (_lane128() % 64) < 32
    return jnp.where(first, -pltpu.roll(t, 96, 1), pltpu.roll(t, 32, 1))


def _head_rstd(t):
    return lax.rsqrt(_head_sums(t * t) * (1.0 / HEAD_DIM) + NORM_EPS)


def _prep_tile(t, w, cos, sin, r=None):
    r = _head_rstd(t) if r is None else r
    tn = t * r * w
    return tn * cos + _rot_half(tn) * sin


def _prep_tile_bwd(t, w, cos, sin, g, r=None):
    r = _head_rstd(t) if r is None else r
    d_tn = g * cos - _rot_half(g * sin)
    th = t * r
    dw = jnp.sum(d_tn * th, axis=0, keepdims=True)
    gh = d_tn * w
    m = _head_sums(gh * th) * (1.0 / HEAD_DIM)
    return r * (gh - th * m), dw


def _band_mask(n):
    qi = lax.broadcasted_iota(jnp.int32, (BLOCK, 2 * BLOCK), 0) + BLOCK
    ki = lax.broadcasted_iota(jnp.int32, (BLOCK, 2 * BLOCK), 1)
    rel = qi - ki
    return (rel >= 0) & (rel < BLOCK) & ((n > 0) | (ki >= BLOCK))


def _half_select(tile, half):
    lo = _lane128() < 64
    return jnp.where(lo if half == 0 else jnp.logical_not(lo), tile, 0.0)


def _stack_group(tiles, kv_half):
    rows = []
    for t in tiles:
        for half in range(2):
            piece = _half_select(t, half)
            rows.append(piece if half == kv_half else pltpu.roll(piece, 64, 1))
    return jnp.concatenate(rows, axis=0)


def _unstack_group(stacked, kv_half):
    tiles = []
    for i in range(2):
        acc = None
        for half in range(2):
            piece = _half_select(stacked[BLOCK * (2 * i + half):BLOCK * (2 * i + half + 1)], kv_half)
            piece = piece if half == kv_half else pltpu.roll(piece, 64, 1)
            acc = piece if acc is None else acc + piece
        tiles.append(acc)
    return tiles


def _stack_heads(tiles):
    zeros = jnp.zeros((4 * BLOCK, 128), F32)
    rows = []
    for g in range(N_KV_HEADS):
        half = _stack_group(tiles[2 * g:2 * g + 2], g % 2)
        rows.append(jnp.concatenate([half, zeros] if g < 2 else [zeros, half], axis=1))
    return jnp.concatenate(rows, axis=0)


def _unstack_heads(stacked):
    tiles = []
    for g in range(N_KV_HEADS):
        lanes = slice(0, 128) if g < 2 else slice(128, 256)
        tiles += _unstack_group(stacked[4 * BLOCK * g:4 * BLOCK * (g + 1), lanes], g % 2)
    return tiles


def _attn_specs(nb):
    last = nb - 1
    qi = lambda n: (jnp.minimum(n, last), 0)
    prev = lambda n: jnp.maximum(n - 1, 0)
    cur = lambda n: jnp.minimum(n, last)
    specs = [
        pl.BlockSpec((BLOCK, ATTN_W), qi),
        pl.BlockSpec((BLOCK, KV_W), lambda n: (cur(n), 4)),
        pl.BlockSpec((BLOCK, KV_W), lambda n: (prev(n), 4)),
        pl.BlockSpec((BLOCK, KV_W), lambda n: (cur(n), 5)),
        pl.BlockSpec((BLOCK, KV_W), lambda n: (prev(n), 5)),
        pl.BlockSpec((BLOCK, 512), lambda n: (cur(n), 3)),
        pl.BlockSpec((BLOCK, 512), lambda n: (cur(n), 4)),
        pl.BlockSpec((BLOCK, 256), lambda n: (cur(n), 0)),
        pl.BlockSpec((BLOCK, 256), lambda n: (prev(n), 0)),
        pl.BlockSpec((1, 128), lambda n: (0, 0)),
        pl.BlockSpec((1, 128), lambda n: (0, 0)),
        pl.BlockSpec((1, N_HEADS), lambda n: (0, 0)),
    ]
    return specs


def _rope_table(posf, invf):
    t = posf.shape[0]

    def body(p_ref, f_ref, o_ref):
        ang = p_ref[...] * f_ref[...]
        o_ref[...] = jnp.concatenate([jnp.cos(ang), jnp.sin(ang)], axis=1)

    return pl.pallas_call(
        body, name="rope_table", grid=(t // 512,),
        in_specs=[pl.BlockSpec((512, 1), lambda i: (i, 0)), pl.BlockSpec((1, 128), lambda i: (0, 0))],
        out_specs=pl.BlockSpec((512, 256), lambda i: (i, 0)),
        out_shape=jax.ShapeDtypeStruct((t, 256), F32), compiler_params=_params(("parallel",)),
    )(posf, invf)


def _attn_common(n, q_ref, kc_ref, kp_ref, vc_ref, vp_ref, rq_ref, rp_ref, qw_ref, kw_ref):
    cos_q, sin_q = rq_ref[:, 0:128], rq_ref[:, 128:256]
    cos_k = jnp.concatenate([rp_ref[:, 0:128], cos_q], axis=0)
    sin_k = jnp.concatenate([rp_ref[:, 128:256], sin_q], axis=0)
    k_raw = jnp.concatenate([kp_ref[...], kc_ref[...]], axis=0)
    vv = jnp.concatenate([vp_ref[...], vc_ref[...]], axis=0).astype(BF16)
    kk = [_prep_tile(k_raw[:, 128 * i:128 * i + 128], kw_ref[...], cos_k, sin_k).astype(BF16) for i in range(2)]
    vt = [vv[:, 128 * i:128 * i + 128] for i in range(2)]
    qv = q_ref[...]
    qr = [_head_rstd(qv[:, 128 * i:128 * i + 128]) for i in range(8)]
    qt = [_prep_tile(qv[:, 128 * i:128 * i + 128], qw_ref[...], cos_q, sin_q, qr[i]) for i in range(8)]
    return cos_q, sin_q, qr, kk, vt, qt


QK_SCALE = 1.0 / math.sqrt(HEAD_DIM)


def _group_sinks(sink_ref, g):
    return jnp.concatenate([jnp.broadcast_to(sink_ref[:, 4 * g + j:4 * g + j + 1], (BLOCK, 1)) for j in range(4)], axis=0)


def _group_softmax(q4, kk_t, sink, bias):
    s = _dot_nt(q4, kk_t) + bias
    m = jnp.maximum(jnp.max(s, axis=1, keepdims=True), sink)
    p = jnp.exp(s - m)
    es = jnp.exp(sink - m)
    inv = 1.0 / (jnp.sum(p, axis=1, keepdims=True) + es)
    return p * inv, es * inv


def _group_bias(n):
    return jnp.concatenate([jnp.where(_band_mask(n), 0.0, -1e30)] * 4, axis=0)


def _attn_fwd(proj, rope, qw, kw, sinks, later_shards):
    t = proj.shape[0]
    nb = t // BLOCK
    nt = len(later_shards)

    def body(q_ref, kc_ref, kp_ref, vc_ref, vp_ref, za0_ref, za1_ref, rq_ref, rp_ref, qw_ref, kw_ref,
             sink_ref, *rest):
        sh, o_ref, full = rest[:nt], rest[nt], rest[nt + 1:2 * nt + 1]
        ssem, rsem = rest[2 * nt + 1:]
        n = pl.program_id(0)
        start, wait = _bg_gather(sh, full, ssem, rsem)

        @pl.when(n == 0)
        def _():
            start()

        _, _, _, kk, vt, qt = _attn_common(n, q_ref, kc_ref, kp_ref, vc_ref, vp_ref, rq_ref, rp_ref, qw_ref, kw_ref)
        bias = jnp.concatenate([_group_bias(n)] * 4, axis=0)
        q16 = (_stack_heads(qt) * QK_SCALE).astype(BF16)
        sink16 = jnp.concatenate([_group_sinks(sink_ref, g) for g in range(N_KV_HEADS)], axis=0)
        p, _ = _group_softmax(q16, jnp.concatenate(kk, axis=1), sink16, bias)
        tiles = _unstack_heads(_dot(p.astype(BF16), jnp.concatenate(vt, axis=1)))
        za = jnp.concatenate([za0_ref[...], za1_ref[...]], axis=1)
        o_ref[...] = jnp.concatenate(tiles, axis=1) * _silu(za)

        @pl.when(n == nb - 1)
        def _():
            wait()

    return pl.pallas_call(
        body, name="attn_fwd", grid=(nb,), in_specs=_attn_specs(nb) + [ANY] * nt,
        out_specs=[pl.BlockSpec((BLOCK, ATTN_W), lambda n: (n, 0))] + [ANY] * nt,
        out_shape=[jax.ShapeDtypeStruct((t, ATTN_W), F32)]
        + [jax.ShapeDtypeStruct((4,) + s.shape, s.dtype) for s in later_shards],
        scratch_shapes=[pltpu.SemaphoreType.DMA((4 * nt,)), pltpu.SemaphoreType.DMA((4 * nt,))],
        compiler_params=_params(("arbitrary",)),
    )(proj, proj, proj, proj, proj, proj, proj, rope, rope, qw, kw, sinks, *later_shards)


def _attn_bwd(proj, rope, qw, kw, sinks, doa, du, dzs, outgoing):
    t = proj.shape[0]
    nb = t // BLOCK
    last = nb - 1
    nt = len(outgoing)

    def body(q_ref, kc_ref, kp_ref, vc_ref, vp_ref, za0_ref, za1_ref, rq_ref, rp_ref, qw_ref, kw_ref,
             sink_ref, doa_ref, du_ref, dzs_ref, *rest):
        src = rest[:nt]
        dp_ref, gq_ref, gk_ref, gs_ref = rest[nt:nt + 4]
        dst = rest[nt + 4:2 * nt + 4]
        dkk_s, dvv_s, ck_s, cv_s, dq_s, dza_s, ssem, rsem = rest[2 * nt + 4:]
        n = pl.program_id(0)
        start, wait = _bg_scatter_devices(src, dst, ssem, rsem)

        @pl.when(n == 0)
        def _():
            start()
            gq_ref[...] = jnp.zeros_like(gq_ref)
            gk_ref[...] = jnp.zeros_like(gk_ref)
            gs_ref[...] = jnp.zeros_like(gs_ref)
            ck_s[...] = jnp.zeros_like(ck_s)
            cv_s[...] = jnp.zeros_like(cv_s)
            dq_s[...] = jnp.zeros_like(dq_s)
            dza_s[...] = jnp.zeros_like(dza_s)

        dp_ref[:, 0:ATTN_W] = dq_s[...]
        dp_ref[:, ATTN_W + 2 * KV_W:2 * ATTN_W + 2 * KV_W] = dza_s[...]
        dp_ref[:, 2 * ATTN_W + 2 * KV_W:IN_W - SSM_W] = du_ref[...]
        dp_ref[:, IN_W - SSM_W:IN_W] = dzs_ref[...]

        @pl.when(n == nb)
        def _():
            dkk_s[...] = jnp.zeros_like(dkk_s)
            dvv_s[...] = jnp.zeros_like(dvv_s)

        @pl.when(n < nb)
        def _():
            cos_q, sin_q, qr, kk, vt, qt = _attn_common(n, q_ref, kc_ref, kp_ref, vc_ref, vp_ref, rq_ref, rp_ref,
                                                        qw_ref, kw_ref)
            bias = jnp.concatenate([_group_bias(n)] * 4, axis=0)
            za = jnp.concatenate([za0_ref[...], za1_ref[...]], axis=1)
            doa_v = doa_ref[...]
            do_full = doa_v * _silu(za)
            k_all, v_all = jnp.concatenate(kk, axis=1), jnp.concatenate(vt, axis=1)
            q_b = (_stack_heads(qt) * QK_SCALE).astype(BF16)
            do_b = _stack_heads([do_full[:, 128 * i:128 * i + 128] for i in range(8)]).astype(BF16)
            sink16 = jnp.concatenate([_group_sinks(sink_ref, g) for g in range(N_KV_HEADS)], axis=0)
            p, psink = _group_softmax(q_b, k_all, sink16, bias)
            p_b = p.astype(BF16)
            dp = _dot_nt(do_b, v_all)
            delta = jnp.sum(p * dp, axis=1, keepdims=True)
            ds_b = (p * (dp - delta)).astype(BF16)
            sd = psink * delta
            gsink = jnp.zeros((1, 128), F32)
            lane = _lane128()
            for h in range(N_HEADS):
                gsink = gsink + jnp.where(lane == h, -jnp.sum(sd[BLOCK * h:BLOCK * (h + 1)]), 0.0)
            o_tiles = _unstack_heads(_dot(p_b, v_all))
            dq_tiles = [d * QK_SCALE for d in _unstack_heads(_dot(ds_b, k_all))]
            dkk = [_dot_tn(ds_b, q_b)]
            dvv = [_dot_tn(p_b, do_b)]
            dza_s[...] = (doa_v * jnp.concatenate(o_tiles, axis=1) * _dsilu(za)).astype(BF16)
            qv = q_ref[...]
            gq = jnp.zeros((1, 128), F32)
            out = []
            for i in range(8):
                d, dw = _prep_tile_bwd(qv[:, 128 * i:128 * i + 128], qw_ref[...], cos_q, sin_q, dq_tiles[i], qr[i])
                out.append(d)
                gq = gq + dw
            dq_s[...] = jnp.concatenate(out, axis=1).astype(BF16)
            gq_ref[...] += gq
            gs_ref[...] += gsink
            dkk_s[...] = jnp.concatenate(dkk, axis=1)
            dvv_s[...] = jnp.concatenate(dvv, axis=1)

        cos_p, sin_p = rp_ref[:, 0:128], rp_ref[:, 128:256]
        dk_prev = ck_s[...] + dkk_s[0:BLOCK, :]
        kp = kp_ref[...]
        gk = jnp.zeros((1, 128), F32)
        out = []
        for i in range(2):
            d, dw = _prep_tile_bwd(kp[:, 128 * i:128 * i + 128], kw_ref[...], cos_p, sin_p,
                                   dk_prev[:, 128 * i:128 * i + 128])
            out.append(d)
            gk = gk + dw
        dp_ref[:, ATTN_W:ATTN_W + KV_W] = jnp.concatenate(out, axis=1).astype(BF16)
        dp_ref[:, ATTN_W + KV_W:ATTN_W + 2 * KV_W] = (cv_s[...] + dvv_s[0:BLOCK, :]).astype(BF16)
        gk_ref[...] += gk
        ck_s[...] = dkk_s[BLOCK:2 * BLOCK, :]
        cv_s[...] = dvv_s[BLOCK:2 * BLOCK, :]

        @pl.when(n == nb)
        def _():
            wait()

    qblk = lambda n: (jnp.minimum(n, last), 0)
    kblk = lambda n: (jnp.maximum(n - 1, 0), 0)
    vec = pl.BlockSpec((1, 128), lambda n: (0, 0))
    return pl.pallas_call(
        body, name="attn_bwd", grid=(nb + 1,),
        in_specs=_attn_specs(nb) + [pl.BlockSpec((BLOCK, ATTN_W), qblk), pl.BlockSpec((BLOCK, SSM_W), kblk),
                                    pl.BlockSpec((BLOCK, SSM_W), kblk)] + [ANY] * nt,
        out_specs=[pl.BlockSpec((BLOCK, IN_W), kblk), vec, vec, vec] + [ANY] * nt,
        out_shape=[jax.ShapeDtypeStruct((t, IN_W), BF16), jax.ShapeDtypeStruct((1, 128), F32),
                   jax.ShapeDtypeStruct((1, 128), F32), jax.ShapeDtypeStruct((1, 128), F32)]
        + [jax.ShapeDtypeStruct(a.shape, a.dtype) for a in outgoing],
        scratch_shapes=[pltpu.VMEM((2 * BLOCK, KV_W), F32), pltpu.VMEM((2 * BLOCK, KV_W), F32),
                        pltpu.VMEM((BLOCK, KV_W), F32), pltpu.VMEM((BLOCK, KV_W), F32),
                        pltpu.VMEM((BLOCK, ATTN_W), BF16), pltpu.VMEM((BLOCK, ATTN_W), BF16),
                        pltpu.SemaphoreType.DMA((7 * nt,)), pltpu.SemaphoreType.DMA((7 * nt,))],
        compiler_params=_params(("arbitrary",)),
    )(proj, proj, proj, proj, proj, proj, proj, rope, rope, qw, kw, sinks, doa, du, dzs, *outgoing)


def _cmul(ar, ai, br, bi):
    return ar * br - ai * bi, ar * bi + ai * br


def _zoh(a_re, a_im, delta):
    e = jnp.exp(a_re * delta)
    lr, li = e * jnp.cos(a_im * delta), e * jnp.sin(a_im * delta)
    inv = 1.0 / (a_re * a_re + a_im * a_im)
    fr, fi = _cmul(lr - 1.0, li, a_re * inv, -a_im * inv)
    return lr, li, fr, fi


def _ssm_prep(a_re, a_im, log_step, b_re, b_im, seg_len):
    n_sq = int(round(math.log2(seg_len)))
    assert 2 ** n_sq == seg_len

    def body(ar_ref, ai_ref, ls_ref, arx_ref, aix_ref, br_ref, bi_ref, lr_ref, li_ref, pr_ref, pi_ref, bbr_ref, bbi_ref):
        delta = jnp.exp(ls_ref[...])
        lr, li, _, _ = _zoh(ar_ref[...], ai_ref[...], delta)
        lr_ref[...] = lr
        li_ref[...] = li
        pr, pi = lr, li
        for _ in range(n_sq):
            pr, pi = _cmul(pr, pi, pr, pi)
        pr_ref[...] = pr
        pi_ref[...] = pi
        _, _, fr, fi = _zoh(arx_ref[...], aix_ref[...], delta)
        bbr, bbi = _cmul(fr, fi, br_ref[...], bi_ref[...])
        bbr_ref[...] = bbr
        bbi_ref[...] = bbi

    gp = jax.ShapeDtypeStruct((SSM_G, SSM_P), F32)
    gx = jax.ShapeDtypeStruct((SSM_G, SSM_P * SSM_H), F32)
    return pl.pallas_call(body, name="ssm_prep", out_shape=[gp, gp, gp, gp, gx, gx])(
        a_re, a_im, log_step.reshape(SSM_G, 1), jnp.repeat(a_re, SSM_H, axis=1), jnp.repeat(a_im, SSM_H, axis=1),
        b_re.reshape(SSM_G, SSM_P * SSM_H), b_im.reshape(SSM_G, SSM_P * SSM_H))


def _ssm_param_grads(a_re, a_im, log_step, b_re, b_im, dlam_re, dlam_im, dbb_re, dbb_im):
    def body(ar_ref, ai_ref, ls_ref, arx_ref, aix_ref, br_ref, bi_ref, dlr_ref, dli_ref, dbr_ref, dbi_ref,
             gar_ref, gai_ref, gls_ref, gbr_ref, gbi_ref):
        delta = jnp.exp(ls_ref[...])
        ar, ai = ar_ref[...], ai_ref[...]
        lr, li, fr, fi = _zoh(ar, ai, delta)
        _, _, frx, fix = _zoh(arx_ref[...], aix_ref[...], delta)
        dbr, dbi = dbr_ref[...], dbi_ref[...]
        br, bi = br_ref[...], bi_ref[...]
        gbr, gbi = _cmul(frx, -fix, dbr, dbi)
        gbr_ref[...] = gbr
        gbi_ref[...] = gbi
        tr, ti = _cmul(br, -bi, dbr, dbi)
        row = lax.broadcasted_iota(jnp.int32, (SSM_P * SSM_H, SSM_P), 0)
        col = lax.broadcasted_iota(jnp.int32, (SSM_P * SSM_H, SSM_P), 1)
        fold = (row // SSM_H == col).astype(F32)
        dfr = jnp.dot(tr, fold, precision=lax.Precision.HIGHEST, preferred_element_type=F32)
        dfi = jnp.dot(ti, fold, precision=lax.Precision.HIGHEST, preferred_element_type=F32)
        inv = 1.0 / (ar * ar + ai * ai)
        ilr, ili = ar * inv, -ai * inv
        t1r, t1i = _cmul(dfr, dfi, ilr, -ili)
        dlbr, dlbi = dlr_ref[...] + t1r, dli_ref[...] + t1i
        qr, qi = _cmul(fr, fi, ilr, ili)
        t2r, t2i = _cmul(dfr, dfi, qr, -qi)
        glr, gli = -t2r, -t2i
        dzr, dzi = _cmul(dlbr, dlbi, lr, -li)
        gar_ref[...] = glr + dzr * delta
        gai_ref[...] = gli + dzi * delta
        gls_ref[...] = jnp.sum(dzr * ar + dzi * ai, axis=1, keepdims=True) * delta

    gp = jax.ShapeDtypeStruct((SSM_G, SSM_P), F32)
    gx = jax.ShapeDtypeStruct((SSM_G, SSM_P * SSM_H), F32)
    return pl.pallas_call(body, name="ssm_param_grads",
                          out_shape=[gp, gp, jax.ShapeDtypeStruct((SSM_G, 1), F32), gx, gx])(
        a_re, a_im, log_step.reshape(SSM_G, 1), jnp.repeat(a_re, SSM_H, axis=1), jnp.repeat(a_im, SSM_H, axis=1),
        b_re.reshape(SSM_G, SSM_P * SSM_H), b_im.reshape(SSM_G, SSM_P * SSM_H), dlam_re, dlam_im, dbb_re, dbb_im)


def _block_diag_in(bb):
    w = jnp.tile(bb.reshape(SSM_GB, SSM_ST, SSM_H), (1, 1, 16))
    row = lax.broadcasted_iota(jnp.int32, (1, SSM_ST, SSM_CH), 1) // SSM_P
    col = lax.broadcasted_iota(jnp.int32, (1, SSM_ST, SSM_CH), 2) // SSM_H
    return jnp.where(row == col, w, 0.0)


def _block_diag_out(c):
    w = jnp.tile(c.reshape(SSM_GB, SSM_CH, SSM_P), (1, 1, 16))
    row = lax.broadcasted_iota(jnp.int32, (1, SSM_CH, SSM_ST), 1) // SSM_H
    col = lax.broadcasted_iota(jnp.int32, (1, SSM_CH, SSM_ST), 2) // SSM_P
    return jnp.where(row == col, w, 0.0)


SEG_ROWS = SCAN_ROWS // N_SEG


def _chunk_perm():
    out_row = lax.broadcasted_iota(jnp.int32, (SCAN_ROWS, SCAN_ROWS), 0)
    in_row = lax.broadcasted_iota(jnp.int32, (SCAN_ROWS, SCAN_ROWS), 1)
    return (out_row == N_SEG * (in_row % SEG_ROWS) + in_row // SEG_ROWS).astype(BF16)


def _chunk_rows(j, seg_len, s):
    return pl.ds(pl.multiple_of(s * seg_len + j * SEG_ROWS, SEG_ROWS), SEG_ROWS)


def _gather_chunk(ref, j, seg_len):
    return jnp.concatenate([ref[_chunk_rows(j, seg_len, s), :] for s in range(N_SEG)], axis=0)


def _scatter_chunk(ref, j, seg_len, val):
    for s in range(N_SEG):
        ref[_chunk_rows(j, seg_len, s), :] = val[s * SEG_ROWS:(s + 1) * SEG_ROWS]


def _interleave(perm, x_b):
    return _dot(perm, x_b).astype(BF16)


def _state_pieces():
    return [pl.ds(part * SSM_ST + k * SCAN_LW, SCAN_LW) for k in range(SSM_ST // SCAN_LW) for part in range(2)]


def _expand_states(x_b, w_ref, dst_ref):
    for cols in _state_pieces():
        dst_ref[:, cols] = _dot_nt(x_b, w_ref[cols, :])


def _contract_states(src_ref, w_ref):
    acc = None
    for cols in _state_pieces():
        part = _dot_nt(src_ref[:, cols].astype(BF16), w_ref[:, cols])
        acc = part if acc is None else acc + part
    return acc


def _scan_fwd(src_ref, dst_ref, lam_r_ref, lam_i_ref, init_ref, final_ref, steps):
    for k in range(SSM_ST // SCAN_LW):
        re = pl.ds(k * SCAN_LW, SCAN_LW)
        im = pl.ds(SSM_ST + k * SCAN_LW, SCAN_LW)
        lr, li = lam_r_ref[:, re], lam_i_ref[:, re]

        def step(i, carry, re=re, im=im, lr=lr, li=li):
            hr, hi = carry
            rows = pl.ds(pl.multiple_of(i * 8, 8), 8)
            nr = lr * hr - li * hi + src_ref[rows, re]
            ni = lr * hi + li * hr + src_ref[rows, im]
            if dst_ref is not None:
                dst_ref[rows, re] = nr
                dst_ref[rows, im] = ni
            return nr, ni

        hr, hi = lax.fori_loop(0, steps, step, (init_ref[:, re], init_ref[:, im]), unroll=True)
        final_ref[:, re] = hr
        final_ref[:, im] = hi


def _ssm_specs(t):
    col = lambda g: (0, g)
    gb3 = lambda g: (g, 0, 0)
    return dict(
        rows=pl.BlockSpec((t, SSM_CH), col),
        lam=pl.BlockSpec((None, N_SEG, SSM_ST), gb3),
        w_in=pl.BlockSpec((None, 2 * SSM_ST, SSM_CH), gb3),
        w_out=pl.BlockSpec((None, SSM_CH, 2 * SSM_ST), gb3),
        vec=pl.BlockSpec((1, SSM_CH), col),
    )


def _segment_states(x_ref, pw_r_ref, pw_i_ref, out_ref, reverse):
    re, im = pl.ds(0, SSM_ST), pl.ds(SSM_ST, SSM_ST)
    pr, pi = pw_r_ref[0:1, :], pw_i_ref[0:1, :]
    first = N_SEG - 1 if reverse else 0
    out_ref[first:first + 1, :] = jnp.zeros((1, 2 * SSM_ST), F32)
    order = range(N_SEG - 1, 0, -1) if reverse else range(N_SEG - 1)
    for s in order:
        d = s - 1 if reverse else s + 1
        hr, hi = out_ref[s:s + 1, re], out_ref[s:s + 1, im]
        if reverse:
            nr, ni = pr * hr + pi * hi, pr * hi - pi * hr
        else:
            nr, ni = pr * hr - pi * hi, pr * hi + pi * hr
        out_ref[d:d + 1, re] = nr + x_ref[s:s + 1, re]
        out_ref[d:d + 1, im] = ni + x_ref[s:s + 1, im]


def _ssm_fwd(proj, lam_r, lam_i, pw_r, pw_i, w_in, w_out, d_skip):
    t = proj.shape[0]
    seg_len = t // N_SEG
    nch = t // SCAN_ROWS
    steps = SCAN_ROWS // N_SEG
    sp = _ssm_specs(t)

    def body(u_ref, lr_ref, li_ref, pr_ref, pi_ref, wi_ref, wo_ref, d_ref, y_ref, hc_ref, bu_s, car_s, seg_s, ub_s,
             y0_s, y1_s):
        perm = _chunk_perm()
        car_s[...] = jnp.zeros_like(car_s)

        def chunk1(j, c):
            rows = pl.ds(pl.multiple_of(j * SCAN_ROWS, SCAN_ROWS), SCAN_ROWS)
            u_b = _interleave(perm, _gather_chunk(u_ref, j, seg_len).astype(BF16))
            ub_s[rows, :] = u_b
            _expand_states(u_b, wi_ref, bu_s)
            _scan_fwd(bu_s, None, lr_ref, li_ref, car_s, car_s, steps)
            return c

        lax.fori_loop(0, nch, chunk1, 0)
        _segment_states(car_s, pr_ref, pi_ref, seg_s, reverse=False)
        car_s[...] = seg_s[...]

        def chunk2(j, c):
            rows = pl.ds(pl.multiple_of(j * SCAN_ROWS, SCAN_ROWS), SCAN_ROWS)
            _expand_states(ub_s[rows, :], wi_ref, bu_s)
            hc_ref[j] = car_s[...]
            _scan_fwd(bu_s, bu_s, lr_ref, li_ref, car_s, car_s, steps)
            yv = _contract_states(bu_s, wo_ref)
            y0_s[...] = yv[:, 0:128]
            y1_s[...] = yv[:, 128:256]
            for s in range(N_SEG):
                nat = _chunk_rows(j, seg_len, s)
                sub = pl.ds(s, SEG_ROWS, stride=N_SEG)
                y_ref[nat, :] = jnp.concatenate([y0_s[sub, :], y1_s[sub, :]], axis=1) + d_ref[...] * u_ref[nat, :]
            return c

        lax.fori_loop(0, nch, chunk2, 0)

    u_cols = 2560 // SSM_CH
    return pl.pallas_call(
        body, name="ssm_fwd", grid=(SSM_GB,),
        in_specs=[pl.BlockSpec((t, SSM_CH), lambda g: (0, u_cols + g)), sp["lam"], sp["lam"], sp["lam"], sp["lam"],
                  sp["w_in"], sp["w_out"], sp["vec"]],
        out_specs=[sp["rows"], pl.BlockSpec((None, nch, N_SEG, 2 * SSM_ST), lambda g: (g, 0, 0, 0))],
        out_shape=[jax.ShapeDtypeStruct((t, SSM_W), F32), jax.ShapeDtypeStruct((SSM_GB, nch, N_SEG, 2 * SSM_ST), F32)],
        scratch_shapes=[pltpu.VMEM((SCAN_ROWS, 2 * SSM_ST), F32), pltpu.VMEM((N_SEG, 2 * SSM_ST), F32),
                        pltpu.VMEM((N_SEG, 2 * SSM_ST), F32), pltpu.VMEM((t, SSM_CH), BF16),
                        pltpu.VMEM((SCAN_ROWS, 128), F32), pltpu.VMEM((SCAN_ROWS, 128), F32)],
        compiler_params=_params(("parallel",)),
    )(proj, lam_r, lam_i, pw_r, pw_i, w_in, w_out, d_skip)


def _group_blocks(full):
    row_g = lax.broadcasted_iota(jnp.int32, (SSM_CH, 128), 0) // SSM_H
    lane = lax.broadcasted_iota(jnp.int32, (SSM_CH, 128), 1)
    parts = []
    for k in range(2):
        acc = None
        for v in range(SSM_ST // 128):
            tile = full[:, k * SSM_ST + 128 * v:k * SSM_ST + 128 * (v + 1)]
            tile = jnp.where(row_g == 2 * v + lane // SSM_P, tile, 0.0)
            acc = tile if acc is None else acc + tile
        parts.append(acc + pltpu.roll(acc, SSM_P, 1))
    return jnp.where(lane < SSM_P, parts[0], parts[1])


def _ssm_bwd(proj, dy, hc, lam_r, lam_i, pw_r, pw_i, w_in, w_out, d_skip):
    t = proj.shape[0]
    seg_len = t // N_SEG
    nch = t // SCAN_ROWS
    steps = SCAN_ROWS // N_SEG
    sp = _ssm_specs(t)

    def body(u_ref, dy_ref, hc_ref, lr_ref, li_ref, pr_ref, pi_ref, wi_ref, wo_ref, d_ref,
             du_ref, gbi_ref, gbo_ref, glam_ref, gd_ref, bu_s, h_s, e_s, car_s, seg_s, acc_s, gwi_ref, gwo_ref,
             dyb_s):
        perm = _chunk_perm()

        def chunk_rows(j):
            return pl.ds(pl.multiple_of(j * SCAN_ROWS, SCAN_ROWS), SCAN_ROWS)

        def interleaved(ref, j):
            return _interleave(perm, _gather_chunk(ref, j, seg_len).astype(BF16))

        def load_e(j):
            dy_b = interleaved(dy_ref, j)
            dyb_s[chunk_rows(j), :] = dy_b
            for cols in _state_pieces():
                e_s[:, cols] = _dot(dy_b, wo_ref[:, cols])

        def scan_rev(j, accumulate):
            for k in range(SSM_ST // SCAN_LW):
                re = pl.ds(k * SCAN_LW, SCAN_LW)
                im = pl.ds(SSM_ST + k * SCAN_LW, SCAN_LW)
                lr, li = lr_ref[:, re], li_ref[:, re]

                def step(ii, carry, re=re, im=im, lr=lr, li=li):
                    i = steps - 1 - ii
                    rows = pl.ds(pl.multiple_of(i * 8, 8), 8)
                    if accumulate:
                        gr, gi, ar, ai = carry
                    else:
                        gr, gi = carry
                    nr = lr * gr + li * gi + e_s[rows, re]
                    ni = lr * gi - li * gr + e_s[rows, im]
                    if not accumulate:
                        return nr, ni
                    e_s[rows, re] = nr
                    e_s[rows, im] = ni
                    pr_, pi_ = h_s[rows, re], h_s[rows, im]
                    return nr, ni, ar + nr * pr_ + ni * pi_, ai + ni * pr_ - nr * pi_

                init = (car_s[:, re], car_s[:, im])
                if accumulate:
                    init = init + (acc_s[:, re], acc_s[:, im])
                out = lax.fori_loop(0, steps, step, init, unroll=True)
                car_s[:, re] = out[0]
                car_s[:, im] = out[1]
                if accumulate:
                    acc_s[:, re] = out[2]
                    acc_s[:, im] = out[3]

        car_s[...] = jnp.zeros_like(car_s)

        def pass1(jj, c):
            load_e(nch - 1 - jj)
            scan_rev(nch - 1 - jj, False)
            return c

        lax.fori_loop(0, nch, pass1, 0)
        _segment_states(car_s, pr_ref, pi_ref, seg_s, reverse=True)
        car_s[...] = seg_s[...]
        acc_s[...] = jnp.zeros_like(acc_s)
        gwi_ref[...] = jnp.zeros_like(gwi_ref)
        gwo_ref[...] = jnp.zeros_like(gwo_ref)
        gd_ref[...] = jnp.zeros_like(gd_ref)

        def pass2(jj, c):
            j = nch - 1 - jj
            u_b, dy_b = interleaved(u_ref, j), dyb_s[chunk_rows(j), :]
            _expand_states(u_b, wi_ref, bu_s)
            h_s[0:N_SEG, :] = hc_ref[j]
            seg_s[...] = hc_ref[j]
            h_now = h_s.at[pl.ds(N_SEG, SCAN_ROWS), :]
            _scan_fwd(bu_s, h_now, lr_ref, li_ref, seg_s, seg_s, steps)
            for cols in _state_pieces():
                e_s[:, cols] = _dot(dy_b, wo_ref[:, cols])
            scan_rev(j, True)
            du = d_ref[...] * dy_b.astype(F32)
            for cols in _state_pieces():
                g_b = e_s[:, cols].astype(BF16)
                du = du + _dot(g_b, wi_ref[cols, :])
                gwi_ref[:, cols] += _dot_tn(u_b, g_b)
                gwo_ref[:, cols] += _dot_tn(dy_b, h_now[:, cols].astype(BF16))
            _scatter_chunk(du_ref, j, seg_len, _dot_tn(perm, du.astype(BF16)).astype(du_ref.dtype))
            gd_ref[...] += jnp.sum(_gather_chunk(dy_ref, j, seg_len) * _gather_chunk(u_ref, j, seg_len), axis=0,
                                   keepdims=True)
            return c

        lax.fori_loop(0, nch, pass2, 0)
        glam_ref[...] = jnp.sum(acc_s[...], axis=0, keepdims=True)
        gbi_ref[...] = _group_blocks(gwi_ref[...])
        gbo_ref[...] = _group_blocks(gwo_ref[...])

    mat = pl.BlockSpec((None, SSM_CH, 2 * SSM_P), lambda g: (g, 0, 0))
    u_cols = 2560 // SSM_CH
    return pl.pallas_call(
        body, name="ssm_bwd", grid=(SSM_GB,),
        in_specs=[pl.BlockSpec((t, SSM_CH), lambda g: (0, u_cols + g)), sp["rows"],
                  pl.BlockSpec((None, nch, N_SEG, 2 * SSM_ST), lambda g: (g, 0, 0, 0)),
                  sp["lam"], sp["lam"], sp["lam"], sp["lam"], sp["w_in"], sp["w_out"], sp["vec"]],
        out_specs=[sp["rows"], mat, mat, pl.BlockSpec((None, 1, 2 * SSM_ST), lambda g: (g, 0, 0)), sp["vec"]],
        out_shape=[jax.ShapeDtypeStruct((t, SSM_W), BF16), jax.ShapeDtypeStruct((SSM_GB, SSM_CH, 2 * SSM_P), F32),
                   jax.ShapeDtypeStruct((SSM_GB, SSM_CH, 2 * SSM_P), F32),
                   jax.ShapeDtypeStruct((SSM_GB, 1, 2 * SSM_ST), F32), jax.ShapeDtypeStruct((1, SSM_W), F32)],
        scratch_shapes=[pltpu.VMEM((SCAN_ROWS, 2 * SSM_ST), F32), pltpu.VMEM((SCAN_ROWS + N_SEG, 2 * SSM_ST), F32),
                        pltpu.VMEM((SCAN_ROWS, 2 * SSM_ST), F32), pltpu.VMEM((N_SEG, 2 * SSM_ST), F32),
                        pltpu.VMEM((N_SEG, 2 * SSM_ST), F32), pltpu.VMEM((N_SEG, 2 * SSM_ST), F32),
                        pltpu.VMEM((SSM_CH, 2 * SSM_ST), F32), pltpu.VMEM((SSM_CH, 2 * SSM_ST), F32),
                        pltpu.VMEM((t, SSM_CH), BF16)],
        compiler_params=_params(("parallel",)),
    )(proj, dy, hc, lam_r, lam_i, pw_r, pw_i, w_in, w_out, d_skip)


def _z_ssm_specs(tm):
    return [pl.BlockSpec((tm, 512), lambda i: (i, 7)), pl.BlockSpec((tm, 512), lambda i: (i, 8))]


def _glu_fwd(y, proj, w_glu, b_glu):
    t = y.shape[0]
    tm = 512

    def body(y_ref, z0_ref, z1_ref, w_ref, b_ref, o_ref, yg_ref):
        yg = _gelu(y_ref[...])
        yg_b = yg.astype(BF16)
        a = _dot(yg_b, w_ref[...]) + b_ref[...]
        z = jnp.concatenate([z0_ref[...], z1_ref[...]], axis=1)
        o_ref[...] = yg * _sigmoid(a) * _silu(z)
        yg_ref[...] = yg_b

    row = pl.BlockSpec((tm, SSM_W), lambda i: (i, 0))
    return pl.pallas_call(
        body, name="glu_fwd", grid=(t // tm,),
        in_specs=[row] + _z_ssm_specs(tm) + [pl.BlockSpec((SSM_W, SSM_W), lambda i: (0, 0)),
                                            pl.BlockSpec((1, SSM_W), lambda i: (0, 0))],
        out_specs=[row, row],
        out_shape=[jax.ShapeDtypeStruct((t, SSM_W), F32), jax.ShapeDtypeStruct((t, SSM_W), BF16)],
        compiler_params=_params(("parallel",)),
    )(y, proj, proj, w_glu, b_glu)


def _glu_bwd(y, proj, dos, w_glu, b_glu):
    t = y.shape[0]
    tm = 512

    def body(y_ref, z0_ref, z1_ref, do_ref, w_ref, b_ref, dy_ref, dz_ref, da_ref, gb_ref):
        @pl.when(pl.program_id(0) == 0)
        def _():
            gb_ref[...] = jnp.zeros_like(gb_ref)

        z = jnp.concatenate([z0_ref[...], z1_ref[...]], axis=1)
        yv, do = y_ref[...], do_ref[...]
        yg = _gelu(yv)
        sg = _sigmoid(_dot(yg.astype(BF16), w_ref[...]) + b_ref[...])
        dy2 = do * _silu(z)
        dz_ref[...] = (do * yg * sg * _dsilu(z)).astype(BF16)
        da = dy2 * yg * sg * (1.0 - sg)
        da_b = da.astype(BF16)
        da_ref[...] = da_b
        gb_ref[...] += jnp.sum(da, axis=0, keepdims=True)
        dyg = dy2 * sg + _dot_nt(da_b, w_ref[...])
        dy_ref[...] = dyg * _dgelu(yv)

    row = pl.BlockSpec((tm, SSM_W), lambda i: (i, 0))
    vec = pl.BlockSpec((1, SSM_W), lambda i: (0, 0))
    return pl.pallas_call(
        body, name="glu_bwd", grid=(t // tm,),
        in_specs=[row] + _z_ssm_specs(tm) + [row, pl.BlockSpec((SSM_W, SSM_W), lambda i: (0, 0)), vec],
        out_specs=[row, row, row, vec],
        out_shape=[jax.ShapeDtypeStruct((t, SSM_W), F32), jax.ShapeDtypeStruct((t, SSM_W), BF16),
                   jax.ShapeDtypeStruct((t, SSM_W), BF16), jax.ShapeDtypeStruct((1, SSM_W), F32)],
        compiler_params=_params(("arbitrary",)),
    )(y, proj, proj, dos, w_glu, b_glu)


def _rms(o):
    return lax.rsqrt(jnp.mean(o * o, axis=1, keepdims=True) + NORM_EPS)


def _outproj(oa, os_, aw, sw, w_out, x, target):
    t = x.shape[0]
    tm = 256

    def body(oa_ref, os_ref, aw_ref, sw_ref, w_ref, x_ref, t_ref, mg_ref, do_ref, ls_ref):
        @pl.when(pl.program_id(0) == 0)
        def _():
            ls_ref[...] = jnp.zeros_like(ls_ref)

        a, s = oa_ref[...], os_ref[...]
        merged = jnp.concatenate([a * _rms(a) * aw_ref[...], s * _rms(s) * sw_ref[...]], axis=1).astype(BF16)
        mg_ref[...] = merged
        err = x_ref[...] + _dot(merged, w_ref[...]) - t_ref[...]
        do_ref[...] = err * (1.0 / D_MODEL)
        ls_ref[...] += jnp.sum(err * err)

    half = pl.BlockSpec((tm, ATTN_W), lambda i: (i, 0))
    full = pl.BlockSpec((tm, D_MODEL), lambda i: (i, 0))
    vec = pl.BlockSpec((1, ATTN_W), lambda i: (0, 0))
    return pl.pallas_call(
        body, name="outproj", grid=(t // tm,),
        in_specs=[half, half, vec, vec, pl.BlockSpec((D_MODEL, D_MODEL), lambda i: (0, 0)), full, full],
        out_specs=[full, full, pl.BlockSpec((8, 128), lambda i: (0, 0))],
        out_shape=[jax.ShapeDtypeStruct((t, D_MODEL), BF16), jax.ShapeDtypeStruct((t, D_MODEL), F32),
                   jax.ShapeDtypeStruct((8, 128), F32)],
        compiler_params=_params(("arbitrary",)),
    )(oa, os_, aw, sw, w_out, x, target)


def _outproj_bwd(dout, oa, os_, aw, sw, w_out):
    t = dout.shape[0]
    tm = 256

    def norm_bwd(o, w, dm):
        r = _rms(o)
        yh = o * r
        gh = dm * w
        return r * (gh - yh * jnp.mean(gh * yh, axis=1, keepdims=True)), jnp.sum(dm * yh, axis=0, keepdims=True)

    def body(do_ref, oa_ref, os_ref, aw_ref, sw_ref, w_ref, da_ref, ds_ref, ga_ref, gs_ref):
        @pl.when(pl.program_id(0) == 0)
        def _():
            ga_ref[...] = jnp.zeros_like(ga_ref)
            gs_ref[...] = jnp.zeros_like(gs_ref)

        dm = _dot_nt(do_ref[...].astype(BF16), w_ref[...])
        da, ga = norm_bwd(oa_ref[...], aw_ref[...], dm[:, :ATTN_W])
        ds, gs = norm_bwd(os_ref[...], sw_ref[...], dm[:, ATTN_W:])
        da_ref[...] = da
        ds_ref[...] = ds
        ga_ref[...] += ga
        gs_ref[...] += gs

    half = pl.BlockSpec((tm, ATTN_W), lambda i: (i, 0))
    full = pl.BlockSpec((tm, D_MODEL), lambda i: (i, 0))
    vec = pl.BlockSpec((1, ATTN_W), lambda i: (0, 0))
    return pl.pallas_call(
        body, name="outproj_bwd", grid=(t // tm,),
        in_specs=[full, half, half, vec, vec, pl.BlockSpec((D_MODEL, D_MODEL), lambda i: (0, 0))],
        out_specs=[half, half, vec, vec],
        out_shape=[jax.ShapeDtypeStruct((t, ATTN_W), F32), jax.ShapeDtypeStruct((t, ATTN_W), F32),
                   jax.ShapeDtypeStruct((1, ATTN_W), F32), jax.ShapeDtypeStruct((1, ATTN_W), F32)],
        compiler_params=_params(("arbitrary",)),
    )(dout, oa, os_, aw, sw, w_out)


def _inproj_bwd(dproj, w_slabs, x, norm_w, dout, outgoing):
    t = x.shape[0]
    tm = 512
    nc = 4
    nt = len(outgoing)
    ni = t // tm

    def body(dp_ref, w_ref, x_ref, nw_ref, do_ref, *rest):
        src, (gx_ref, gw_ref), dst = rest[:nt], rest[nt:nt + 2], rest[nt + 2:2 * nt + 2]
        acc_ref, ssem, rsem = rest[2 * nt + 2:]
        i, j = pl.program_id(0), pl.program_id(1)
        start, wait = _bg_scatter_chips(src, dst, ssem, rsem)

        @pl.when((i == 0) & (j == 0))
        def _():
            start()
            gw_ref[...] = jnp.zeros_like(gw_ref)

        @pl.when(j == 0)
        def _():
            acc_ref[...] = jnp.zeros_like(acc_ref)

        acc_ref[...] += _dot_nt(dp_ref[...], w_ref[...])

        @pl.when(j == nc - 1)
        def _():
            xv = x_ref[...]
            r = lax.rsqrt(jnp.mean(xv * xv, axis=1, keepdims=True) + NORM_EPS)
            yh = xv * r
            dh = acc_ref[...]
            gh = dh * nw_ref[...]
            gx_ref[...] = do_ref[...] + r * (gh - yh * jnp.mean(gh * yh, axis=1, keepdims=True))
            gw_ref[...] += jnp.sum(dh * yh, axis=0, keepdims=True)

        @pl.when((i == ni - 1) & (j == nc - 1))
        def _():
            wait()

    full = pl.BlockSpec((tm, D_MODEL), lambda i, j: (i, 0))
    vec = pl.BlockSpec((1, D_MODEL), lambda i, j: (0, 0))
    return pl.pallas_call(
        body, name="inproj_bwd", grid=(ni, nc),
        in_specs=[pl.BlockSpec((tm, SHARD_W), lambda i, j: (i, j)),
                  pl.BlockSpec((None, D_MODEL, SHARD_W), lambda i, j: (j, 0, 0)), full, vec, full] + [ANY] * nt,
        out_specs=[full, vec] + [ANY] * nt,
        out_shape=[jax.ShapeDtypeStruct((t, D_MODEL), F32), jax.ShapeDtypeStruct((1, D_MODEL), F32)]
        + [jax.ShapeDtypeStruct(a.shape, a.dtype) for a in outgoing],
        scratch_shapes=[pltpu.VMEM((tm, D_MODEL), F32), pltpu.SemaphoreType.DMA((3 * nt,)),
                        pltpu.SemaphoreType.DMA((3 * nt,))],
        compiler_params=_params(("arbitrary", "arbitrary")),
    )(dproj, w_slabs, x, norm_w.reshape(1, D_MODEL), dout, *outgoing)


def _adamw_math(w_ref, g_ref, m_ref, v_ref, d_ref, nm_ref, nv_ref):
    gv = g_ref[...]
    nm = ADAM_B1 * m_ref[...] + (1.0 - ADAM_B1) * gv
    nv = ADAM_B2 * v_ref[...] + (1.0 - ADAM_B2) * (gv * gv)
    m_hat = nm / (1.0 - ADAM_B1 ** ADAM_STEP)
    v_hat = nv / (1.0 - ADAM_B2 ** ADAM_STEP)
    d_ref[...] = -ADAM_LR * (m_hat / (jnp.sqrt(v_hat) + ADAM_EPS) + ADAM_WD * w_ref[...])
    nm_ref[...] = nm
    nv_ref[...] = nv


def _adamw_halves(w, mine, theirs, m, v, c_idx, *, rows, name):
    hr, cols = mine.shape
    nblk = hr // rows

    def body(c_ref, w_ref, a_ref, b_ref, m_ref, v_ref, g_ref, d_ref, nm_ref, nv_ref):
        g_ref[...] = jnp.where(pl.program_id(0) == c_ref[0], a_ref[...], b_ref[...])
        _adamw_math(w_ref, g_ref, m_ref, v_ref, d_ref, nm_ref, nv_ref)

    full = pl.BlockSpec((rows, cols), lambda h, i, c: (h * nblk + i, 0))
    part = pl.BlockSpec((rows, cols), lambda h, i, c: (i, 0))
    shp = jax.ShapeDtypeStruct((2 * hr, cols), F32)
    return pl.pallas_call(
        body, name=name,
        grid_spec=pltpu.PrefetchScalarGridSpec(num_scalar_prefetch=1, grid=(2, nblk),
                                               in_specs=[full, part, part, full, full], out_specs=[full] * 4),
        out_shape=[shp] * 4, compiler_params=_params(("parallel", "parallel")),
    )(c_idx, w, mine, theirs, m, v)


def _adamw(w, g, m, v, *, rows, name):
    r, c = w.shape

    def body(w_ref, g_ref, m_ref, v_ref, d_ref, nm_ref, nv_ref):
        _adamw_math(w_ref, g_ref, m_ref, v_ref, d_ref, nm_ref, nv_ref)

    blk = pl.BlockSpec((rows, c), lambda i: (i, 0))
    shp = jax.ShapeDtypeStruct((r, c), F32)
    return pl.pallas_call(body, name=name, grid=(r // rows,), in_specs=[blk] * 4, out_specs=[blk] * 3,
                          out_shape=[shp] * 3, compiler_params=_params(("parallel",)))(w, g, m, v)


def _remote(src, dst, ssem, rsem, dev):
    return pltpu.make_async_remote_copy(src_ref=src, dst_ref=dst, send_sem=ssem, recv_sem=rsem, device_id=dev,
                                        device_id_type=pl.DeviceIdType.MESH)


def _mesh_pos():
    return lax.axis_index("x"), lax.axis_index("y"), lax.axis_index("c")


def _other_chips(x, y):
    return [(1 - x, y), (x, 1 - y), (1 - x, 1 - y)]


def _flips():
    return [(dx, dy, dc) for dx in (0, 1) for dy in (0, 1) for dc in (0, 1) if (dx, dy, dc) != (0, 0, 0)]


def _background(sends, arrivals):
    def start():
        for cp in sends():
            cp.start()

    def wait():
        for cp in arrivals():
            cp.wait_recv()
        for cp in sends():
            cp.wait_send()

    return start, wait


def _bg_gather(sh, full, ssem, rsem):
    x, y, c = _mesh_pos()
    me = 2 * x + y
    peers = [(px, py, c) for px, py in _other_chips(x, y)] + [(x, y, 1 - c)]
    slots = [2 * px + py for px, py in _other_chips(x, y)] + [me]
    pairs = [(i, k) for i in range(len(sh)) for k in range(4)]
    return _background(
        lambda: [_remote(sh[i], full[i].at[me], ssem.at[4 * i + k], rsem.at[4 * i + k], peers[k]) for i, k in pairs],
        lambda: [_remote(full[i].at[slots[k]], full[i].at[slots[k]], ssem.at[4 * i + k], rsem.at[4 * i + k], peers[k])
                 for i, k in pairs])


def _bg_scatter_devices(src, dst, ssem, rsem):
    x, y, c = _mesh_pos()
    me = 4 * x + 2 * y + c
    peers = []
    for dx, dy, dc in _flips():
        px, py, pc = jnp.bitwise_xor(x, dx), jnp.bitwise_xor(y, dy), jnp.bitwise_xor(c, dc)
        peers.append(((px, py, pc), 4 * px + 2 * py + pc))
    pairs = [(i, k) for i in range(len(src)) for k in range(7)]
    return _background(
        lambda: [_remote(src[i].at[peers[k][1]], dst[i].at[me], ssem.at[7 * i + k], rsem.at[7 * i + k], peers[k][0])
                 for i, k in pairs],
        lambda: [_remote(dst[i].at[peers[k][1]], dst[i].at[peers[k][1]], ssem.at[7 * i + k], rsem.at[7 * i + k],
                         peers[k][0]) for i, k in pairs])


def _bg_scatter_chips(src, dst, ssem, rsem):
    x, y, c = _mesh_pos()
    me = 2 * x + y
    chips = _other_chips(x, y)
    pairs = [(i, k) for i in range(len(src)) for k in range(3)]
    slot = lambda k: 2 * chips[k][0] + chips[k][1]
    return _background(
        lambda: [_remote(src[i].at[slot(k)], dst[i].at[me], ssem.at[3 * i + k], rsem.at[3 * i + k], (*chips[k], c))
                 for i, k in pairs],
        lambda: [_remote(dst[i].at[slot(k)], dst[i].at[slot(k)], ssem.at[3 * i + k], rsem.at[3 * i + k], (*chips[k], c))
                 for i, k in pairs])


def _pair_swap(arrays):
    nt = len(arrays)

    def body(*refs):
        src, dst = refs[:nt], refs[nt:2 * nt]
        ssem, rsem = refs[2 * nt:]
        x, y, c = _mesh_pos()
        cps = [_remote(src[i].at[:, 1 - c], dst[i], ssem.at[i], rsem.at[i], (x, y, 1 - c)) for i in range(nt)]
        for cp in cps:
            cp.start()
        for cp in cps:
            cp.wait_recv()
        for cp in cps:
            cp.wait_send()

    return pl.pallas_call(
        body, name="pair_swap", in_specs=[ANY] * nt, out_specs=[ANY] * nt,
        out_shape=[jax.ShapeDtypeStruct((4,) + a.shape[2:], a.dtype) for a in arrays],
        scratch_shapes=[pltpu.SemaphoreType.DMA((nt,)), pltpu.SemaphoreType.DMA((nt,))],
    )(*arrays)


def _half_swap(arrays):
    nt = len(arrays)

    def body(*refs):
        src, dst = refs[:nt], refs[nt:2 * nt]
        ssem, rsem = refs[2 * nt:]
        x, y, c = _mesh_pos()
        cps = [_remote(src[i], dst[i], ssem.at[i], rsem.at[i], (x, y, 1 - c)) for i in range(nt)]
        for cp in cps:
            cp.start()
        for cp in cps:
            cp.wait_recv()
        for cp in cps:
            cp.wait_send()

    return pl.pallas_call(
        body, name="half_swap", in_specs=[ANY] * nt, out_specs=[ANY] * nt,
        out_shape=[jax.ShapeDtypeStruct(a.shape, a.dtype) for a in arrays],
        scratch_shapes=[pltpu.SemaphoreType.DMA((nt,)), pltpu.SemaphoreType.DMA((nt,))],
    )(*arrays)


def _exchange_slices(src, scatter, name):
    def body(src_ref, dst_ref, ssem, rsem, lsem):
        x, y, c = _mesh_pos()
        me = 4 * x + 2 * y + c
        local = pltpu.make_async_copy(src_ref.at[me] if scatter else src_ref, dst_ref.at[me], lsem)
        local.start()
        cps = []
        for k, (dx, dy, dc) in enumerate(_flips()):
            px, py, pc = jnp.bitwise_xor(x, dx), jnp.bitwise_xor(y, dy), jnp.bitwise_xor(c, dc)
            peer = 4 * px + 2 * py + pc
            cp = _remote(src_ref.at[peer] if scatter else src_ref, dst_ref.at[me], ssem.at[k], rsem.at[k],
                         (px, py, pc))
            cp.start()
            cps.append((cp, peer))
        for k, (cp, peer) in enumerate(cps):
            slot = dst_ref.at[peer]
            _remote(slot, slot, ssem.at[k], rsem.at[k], (x, y, c)).wait_recv()
        for cp, _ in cps:
            cp.wait_send()
        local.wait()

    return pl.pallas_call(
        body, name=name, in_specs=[ANY], out_specs=ANY,
        out_shape=jax.ShapeDtypeStruct((8,) + src.shape[-2:], src.dtype),
        scratch_shapes=[pltpu.SemaphoreType.DMA((7,)), pltpu.SemaphoreType.DMA((7,)), pltpu.SemaphoreType.DMA],
    )(src)


def _add_halves(g, recv, c_idx, *, rows, name):
    _, _, hr, cols = g.shape

    def body(c_ref, g_ref, r_ref, o_ref):
        o_ref[...] = (g_ref[...] + r_ref[...].astype(F32)).astype(BF16)

    return pl.pallas_call(
        body, name=name,
        grid_spec=pltpu.PrefetchScalarGridSpec(
            num_scalar_prefetch=1, grid=(4, hr // rows),
            in_specs=[pl.BlockSpec((None, None, rows, cols), lambda j, i, c: (j, c[0], i, 0)),
                      pl.BlockSpec((None, rows, cols), lambda j, i, c: (j, i, 0))],
            out_specs=pl.BlockSpec((None, rows, cols), lambda j, i, c: (j, i, 0))),
        out_shape=jax.ShapeDtypeStruct((4, hr, cols), BF16),
        compiler_params=_params(("parallel", "parallel")),
    )(c_idx, g, recv)


def _sum_peers(slots, own, idx, *, rows, name):
    n, r, cols = slots.shape

    def body(me_ref, *refs):
        me = me_ref[0]
        mine = refs[n][...].astype(F32)
        acc = None
        for k in range(n):
            term = jnp.where(me == k, mine, refs[k][...].astype(F32))
            acc = term if acc is None else acc + term
        refs[n + 1][...] = acc

    def slot_spec(k):
        return pl.BlockSpec((None, rows, cols), lambda i, me: (jnp.where(me[0] == k, (k + 1) % n, k), i, 0))

    return pl.pallas_call(
        body, name=name,
        grid_spec=pltpu.PrefetchScalarGridSpec(
            num_scalar_prefetch=1, grid=(r // rows,),
            in_specs=[slot_spec(k) for k in range(n)] + [pl.BlockSpec((None, rows, cols), lambda i, me: (me[0], i, 0))],
            out_specs=pl.BlockSpec((rows, cols), lambda i, me: (i, 0))),
        out_shape=jax.ShapeDtypeStruct((r, cols), F32),
        compiler_params=_params(("parallel",)),
    )(idx, *([slots] * n), own)


def _sum_slots(slots, *, rows, name):
    n, r, cols = slots.shape

    def body(s_ref, o_ref):
        acc = s_ref[0].astype(F32)
        for k in range(1, n):
            acc = acc + s_ref[k].astype(F32)
        o_ref[...] = acc

    return pl.pallas_call(
        body, name=name, grid=(r // rows,),
        in_specs=[pl.BlockSpec((n, rows, cols), lambda i: (0, i, 0))],
        out_specs=pl.BlockSpec((rows, cols), lambda i: (i, 0)),
        out_shape=jax.ShapeDtypeStruct((r, cols), F32),
        compiler_params=_params(("parallel",)),
    )(slots)


def _pack_small(d, names, rows):
    flat = jnp.concatenate([d[n].astype(F32).reshape(-1) for n in names])
    return jnp.pad(flat, (0, rows * 128 - flat.shape[0])).reshape(rows, 128)


def _unpack_small(p, names):
    flat = p.reshape(-1)
    out, off = {}, 0
    for n in names:
        size = math.prod(SMALL_SHAPES[n])
        out[n] = flat[off:off + size].reshape(SMALL_SHAPES[n])
        off += size
    return out


def _adamw_3d(w, g, m, v, *, name):
    def body(w_ref, g_ref, m_ref, v_ref, d_ref, nm_ref, nv_ref):
        _adamw_math(w_ref, g_ref, m_ref, v_ref, d_ref, nm_ref, nv_ref)

    blk = pl.BlockSpec((8,) + w.shape[1:], lambda i: (i, 0, 0))
    shp = jax.ShapeDtypeStruct(w.shape, F32)
    return pl.pallas_call(body, name=name, grid=(w.shape[0] // 8,), in_specs=[blk] * 4, out_specs=[blk] * 3,
                          out_shape=[shp] * 3, compiler_params=_params(("parallel",)))(w, g, m, v)


def kernel(x, positions, norm_w, w_in, q_norm_w, k_norm_w, sinks, a_re, a_im, log_step, b_re, b_im, c_re, c_im, d_skip, w_glu, b_glu, attn_out_norm_w, ssm_out_norm_w, w_out, loss_target, m_norm_w, m_w_in, m_q_norm_w, m_k_norm_w, m_sinks, m_a_re, m_a_im, m_log_step, m_b_re, m_b_im, m_c_re, m_c_im, m_d_skip, m_w_glu, m_b_glu, m_attn_out_norm_w, m_ssm_out_norm_w, m_w_out, v_norm_w, v_w_in, v_q_norm_w, v_k_norm_w, v_sinks, v_a_re, v_a_im, v_log_step, v_b_re, v_b_im, v_c_re, v_c_im, v_d_skip, v_w_glu, v_b_glu, v_attn_out_norm_w, v_ssm_out_norm_w, v_w_out):
    small_w = dict(norm_w=norm_w, q_norm_w=q_norm_w, k_norm_w=k_norm_w, sinks=sinks, a_re=a_re, a_im=a_im,
                   log_step=log_step, b_re=b_re, b_im=b_im, c_re=c_re, c_im=c_im, d_skip=d_skip, b_glu=b_glu,
                   attn_out_norm_w=attn_out_norm_w, ssm_out_norm_w=ssm_out_norm_w)
    small_m = dict(norm_w=m_norm_w, q_norm_w=m_q_norm_w, k_norm_w=m_k_norm_w, sinks=m_sinks, a_re=m_a_re, a_im=m_a_im,
                   log_step=m_log_step, b_re=m_b_re, b_im=m_b_im, c_re=m_c_re, c_im=m_c_im, d_skip=m_d_skip,
                   b_glu=m_b_glu, attn_out_norm_w=m_attn_out_norm_w, ssm_out_norm_w=m_ssm_out_norm_w)
    small_v = dict(norm_w=v_norm_w, q_norm_w=v_q_norm_w, k_norm_w=v_k_norm_w, sinks=v_sinks, a_re=v_a_re, a_im=v_a_im,
                   log_step=v_log_step, b_re=v_b_re, b_im=v_b_im, c_re=v_c_re, c_im=v_c_im, d_skip=v_d_skip,
                   b_glu=v_b_glu, attn_out_norm_w=v_attn_out_norm_w, ssm_out_norm_w=v_ssm_out_norm_w)
    c_idx = lax.axis_index("c").astype(jnp.int32).reshape(1)
    chip_idx = (2 * lax.axis_index("x") + lax.axis_index("y")).astype(jnp.int32).reshape(1)
    dev_idx = 2 * chip_idx + c_idx

    xs = x[0]
    tgt = loss_target[0]
    t = xs.shape[0]
    posf = positions[0].astype(F32).reshape(t, 1)

    mx, my = lax.axis_index("x"), lax.axis_index("y")
    slab_order = jnp.stack([2 * mx + my, 2 * (1 - mx) + my, 2 * mx + (1 - my), 2 * (1 - mx) + (1 - my)]).astype(jnp.int32)
    proj, hn, w_in_all = _inproj(xs, norm_w, w_in.astype(BF16), slab_order)
    inv_freq = ROPE_THETA ** (-jnp.arange(0, HEAD_DIM, 2, dtype=F32) / HEAD_DIM)
    rope = _rope_table(posf, jnp.tile(inv_freq, 4).reshape(1, 128))
    qw = jnp.tile(q_norm_w, 2).reshape(1, 128)
    kw = jnp.tile(k_norm_w, 2).reshape(1, 128)
    sink_row = sinks.reshape(1, N_HEADS)
    oa, w_glu_all, w_out_all = _attn_fwd(proj, rope, qw, kw, sink_row, [w_glu.astype(BF16), w_out.astype(BF16)])
    w_glu_b = w_glu_all.reshape(SSM_W, SSM_W)
    w_out_b = w_out_all.reshape(D_MODEL, D_MODEL)

    lam_r, lam_i, pw_r, pw_i, bb_r, bb_i = _ssm_prep(a_re, a_im, log_step, b_re, b_im, t // N_SEG)
    rows8 = lambda a: jnp.broadcast_to(a.reshape(SSM_GB, 1, SSM_ST), (SSM_GB, N_SEG, SSM_ST))
    lam_r8, lam_i8, pw_r8, pw_i8 = rows8(lam_r), rows8(lam_i), rows8(pw_r), rows8(pw_i)
    ssm_w_in = jnp.concatenate([_block_diag_in(bb_r), _block_diag_in(bb_i)], axis=1).astype(BF16)
    ssm_w_out = jnp.concatenate([_block_diag_out(c_re), _block_diag_out(-c_im)], axis=2).astype(BF16)
    d_row = d_skip.reshape(1, SSM_W)
    y, hc = _ssm_fwd(proj, lam_r8, lam_i8, pw_r8, pw_i8, ssm_w_in, ssm_w_out, d_row)
    b_glu_row = b_glu.reshape(1, SSM_W)
    os_, yg = _glu_fwd(y, proj, w_glu_b, b_glu_row)
    aw = attn_out_norm_w.reshape(1, ATTN_W)
    sw = ssm_out_norm_w.reshape(1, SSM_W)
    merged, dout, sq_err = _outproj(oa, os_, aw, sw, w_out_b, xs, tgt)
    loss = lax.psum(0.5 * sq_err[0, 0] / D_MODEL, MESH_AXES)

    doa, dos, g_aw, g_sw = _outproj_bwd(dout, oa, os_, aw, sw, w_out_b)
    dout_b = dout.astype(BF16)
    (g_w_out_b,) = _matmul_tn(merged, dout_b, tm=512, tn=1024, name="grad_w_out", dtypes=(BF16,))
    dy, dzs, da, g_b_glu = _glu_bwd(y, proj, dos, w_glu_b, b_glu_row)
    (g_w_glu_b,) = _matmul_tn(yg, da, tm=512, tn=1024, name="grad_w_glu", dtypes=(BF16,))
    du, g_wi, g_wo, g_lam, g_d = _ssm_bwd(proj, dy, hc, lam_r8, lam_i8, pw_r8, pw_i8, ssm_w_in, ssm_w_out, d_row)
    early = [g_w_glu_b.reshape(8, 128, SSM_W), g_w_out_b.reshape(8, 256, D_MODEL)]
    dproj, g_qw, g_kw, g_sink, *early_slots = _attn_bwd(proj, rope, qw, kw, sink_row, doa, du, dzs, early)
    g_w_in, g_w_in_b = _matmul_tn(hn, dproj, tm=512, tn=SHARD_W, name="grad_w_in", slabs=True)
    in_shape = (4, 2, D_MODEL // 2, SHARD_W)
    (from_sib,) = _pair_swap([g_w_in_b.reshape(in_shape)])
    pair_in = _add_halves(g_w_in.reshape(in_shape), from_sib, c_idx, rows=512, name="pair_sum")
    grad_x, g_nw, in_slots = _inproj_bwd(dproj, w_in_all, xs, norm_w, dout, [pair_in])

    g_wi = g_wi.reshape(SSM_G, SSM_H, 2 * SSM_P)
    g_wo = g_wo.reshape(SSM_G, SSM_H, 2 * SSM_P)
    g_bb_r = g_wi[:, :, :SSM_P].transpose(0, 2, 1).reshape(SSM_G, SSM_P * SSM_H)
    g_bb_i = g_wi[:, :, SSM_P:].transpose(0, 2, 1).reshape(SSM_G, SSM_P * SSM_H)
    g_a_re, g_a_im, g_ls, g_b_re, g_b_im = _ssm_param_grads(
        a_re, a_im, log_step, b_re, b_im, g_lam[:, 0, :SSM_ST].reshape(SSM_G, SSM_P),
        g_lam[:, 0, SSM_ST:].reshape(SSM_G, SSM_P), g_bb_r, g_bb_i)
    small_g = dict(
        norm_w=g_nw, q_norm_w=g_qw[0, :64] + g_qw[0, 64:], k_norm_w=g_kw[0, :64] + g_kw[0, 64:],
        sinks=g_sink[0, :N_HEADS], a_re=g_a_re, a_im=g_a_im, log_step=g_ls, b_re=g_b_re, b_im=g_b_im,
        c_re=g_wo[:, :, :SSM_P], c_im=-g_wo[:, :, SSM_P:], d_skip=g_d,
        b_glu=g_b_glu, attn_out_norm_w=g_aw, ssm_out_norm_w=g_sw)

    mine = [_sum_peers(in_slots, pair_in, chip_idx, rows=512, name="sum_w_in"),
            _sum_peers(early_slots[0], early[0], dev_idx, rows=128, name="sum_w_glu"),
            _sum_peers(early_slots[1], early[1], dev_idx, rows=128, name="sum_w_out")]
    theirs = _half_swap(mine)
    packed = _pack_small(small_g, SMALL, 8 * PACK_ROWS).reshape(8, PACK_ROWS, 128)
    summed = _sum_slots(_exchange_slices(packed, True, "small_scatter"), rows=PACK_ROWS, name="small_sum")
    small_red = _exchange_slices(summed, False, "small_gather").reshape(8 * PACK_ROWS, 128)

    big = [_adamw_halves(w_in, mine[0], theirs[0], m_w_in, v_w_in, c_idx, rows=256, name="adamw_w_in"),
           _adamw_halves(w_glu, mine[1], theirs[1], m_w_glu, v_w_glu, c_idx, rows=128, name="adamw_w_glu"),
           _adamw_halves(w_out, mine[2], theirs[2], m_w_out, v_w_out, c_idx, rows=256, name="adamw_w_out")]
    g_in_sh, g_glu_sh, g_out_sh = (b[0] for b in big)
    upd = [b[1:] for b in big]
    grads = _unpack_small(small_red, SMALL)
    flat_first = sum(math.prod(SMALL_SHAPES[n]) for n in SMALL_3D) // 128
    sd, sm, sv = _adamw(_pack_small(small_w, SMALL_FLAT, FLAT_ROWS), small_red[flat_first:flat_first + FLAT_ROWS],
                        _pack_small(small_m, SMALL_FLAT, FLAT_ROWS), _pack_small(small_v, SMALL_FLAT, FLAT_ROWS),
                        rows=FLAT_ROWS, name="adamw_small")
    deltas, new_m, new_v = (_unpack_small(a, SMALL_FLAT) for a in (sd, sm, sv))
    for n in SMALL_3D:
        deltas[n], new_m[n], new_v[n] = _adamw_3d(small_w[n], grads[n], small_m[n], small_v[n], name="adamw_" + n)
    grads.update(w_in=g_in_sh, w_glu=g_glu_sh, w_out=g_out_sh)
    for n, (d, m_, v_) in zip(("w_in", "w_glu", "w_out"), upd):
        deltas[n], new_m[n], new_v[n] = d, m_, v_
    order = ["norm_w", "w_in", "q_norm_w", "k_norm_w", "sinks", "a_re", "a_im", "log_step", "b_re", "b_im", "c_re",
             "c_im", "d_skip", "w_glu", "b_glu", "attn_out_norm_w", "ssm_out_norm_w", "w_out"]
    return (loss, grad_x[None], *[grads[n] for n in order], *[deltas[n] for n in order],
            *[new_m[n] for n in order], *[new_v[n] for n in order])
```

```python
import math

import jax
import jax.numpy as jnp
from jax import lax
from jax.experimental import pallas as pl
from jax.experimental.pallas import tpu as pltpu

F32 = jnp.float32
BF16 = jnp.bfloat16

D_MODEL = 2048
ATTN_W = 1024
SSM_W = 1024
HEAD_DIM = 64
N_HEADS = 16
N_KV_HEADS = 4
KV_W = 256
BLOCK = 128
IN_W = 4608
SHARD_W = IN_W // 4
ROPE_THETA = 10000.0
SSM_H = 16
SSM_G = 64
SSM_P = 64
NORM_EPS = 1e-6
ADAM_LR = 0.001
ADAM_B1 = 0.9
ADAM_B2 = 0.999
ADAM_EPS = 1e-08
ADAM_WD = 0.01
ADAM_STEP = 10

N_SEG = 8
SSM_GB = 4
SSM_CH = 256
SSM_ST = 1024
SCAN_ROWS = 512
SCAN_LW = 256
VMEM_LIMIT = 56 * 1024 * 1024
MESH_AXES = ("x", "y", "c")
ANY = pl.BlockSpec(memory_space=pl.ANY)

SMALL_3D = ("b_re", "b_im", "c_re", "c_im")
SMALL_FLAT = ("norm_w", "q_norm_w", "k_norm_w", "sinks", "a_re", "a_im", "log_step", "d_skip", "b_glu",
              "attn_out_norm_w", "ssm_out_norm_w")
SMALL = SMALL_3D + SMALL_FLAT
SMALL_SHAPES = {"norm_w": (2048,), "q_norm_w": (64,), "k_norm_w": (64,), "sinks": (16,), "a_re": (64, 64),
                "a_im": (64, 64), "log_step": (64,), "b_re": (64, 64, 16), "b_im": (64, 64, 16),
                "c_re": (64, 16, 64), "c_im": (64, 16, 64), "d_skip": (1024,), "b_glu": (1024,),
                "attn_out_norm_w": (1024,), "ssm_out_norm_w": (1024,)}
PACK_ROWS = 272
FLAT_ROWS = 120


def _params(sem=None):
    return pltpu.CompilerParams(dimension_semantics=sem, vmem_limit_bytes=VMEM_LIMIT)


def _dot(a, b):
    return jnp.dot(a, b, preferred_element_type=F32)


def _dot_nt(a, b):
    return lax.dot_general(a, b, (((1,), (1,)), ((), ())), preferred_element_type=F32)


def _dot_tn(a, b):
    return lax.dot_general(a, b, (((0,), (0,)), ((), ())), preferred_element_type=F32)


def _sigmoid(x):
    return 1.0 / (1.0 + jnp.exp(-x))


def _silu(x):
    return x * _sigmoid(x)


def _dsilu(x):
    s = _sigmoid(x)
    return s * (1.0 + x * (1.0 - s))


_GELU_C = math.sqrt(2.0 / math.pi)


def _gelu(x):
    return 0.5 * x * (1.0 + jnp.tanh(_GELU_C * (x + 0.044715 * x * x * x)))


def _dgelu(x):
    t = jnp.tanh(_GELU_C * (x + 0.044715 * x * x * x))
    return 0.5 * (1.0 + t) + 0.5 * x * (1.0 - t * t) * _GELU_C * (1.0 + 3.0 * 0.044715 * x * x)


def _matmul_tn(a, b, *, tm, tn, name, slabs=False, dtypes=(F32, BF16)):
    k, m = a.shape
    _, n = b.shape

    def body(a_ref, b_ref, *o_refs):
        acc = _dot_tn(a_ref[...], b_ref[...])
        for o_ref in o_refs:
            o_ref[...] = acc.astype(o_ref.dtype)

    if slabs:
        out_spec = pl.BlockSpec((None, tm, tn), lambda j, i: (j, i, 0))
        shape = (n // tn, m, tn)
    else:
        out_spec = pl.BlockSpec((tm, tn), lambda j, i: (i, j))
        shape = (m, n)
    return pl.pallas_call(
        body, name=name, grid=(n // tn, m // tm),
        in_specs=[pl.BlockSpec((k, tm), lambda j, i: (0, i)), pl.BlockSpec((k, tn), lambda j, i: (0, j))],
        out_specs=[out_spec] * len(dtypes),
        out_shape=[jax.ShapeDtypeStruct(shape, d) for d in dtypes],
        compiler_params=_params(("parallel", "parallel")),
    )(a, b)


def _inproj(x, norm_w, w_sh, order):
    t = x.shape[0]
    tm = 512
    ni = t // tm
    hr = D_MODEL // 2

    def body(ord_ref, x_ref, nw_ref, sh_ref, proj_ref, hn_ref, full_ref, wbuf, hn_s, ssem, rsem, lsem):
        s, i = pl.program_id(0), pl.program_id(1)
        mx, my, c = _mesh_pos()
        me = 2 * mx + my
        sib = (mx, my, 1 - c)
        chips = _other_chips(mx, my)

        def half(which):
            return pl.ds(pl.multiple_of(which * hr, 8), hr)

        def slot(k):
            return 2 * chips[k][0] + chips[k][1]

        def ici(k):
            return _remote(sh_ref.at[half(c)], full_ref.at[me, half(c)], ssem.at[k], rsem.at[k], (*chips[k], c))

        def own():
            return _remote(sh_ref, full_ref.at[me], ssem.at[6], rsem.at[6], sib)

        def landed(k, which, sem):
            ref = full_ref.at[slot(k), half(which)]
            return _remote(ref, ref, ssem.at[sem], rsem.at[sem], sib)

        def fetch(src, b):
            return pltpu.make_async_copy(src, wbuf.at[b], lsem.at[b])

        @pl.when((s == 0) & (i == 0))
        def _():
            for k in range(3):
                ici(k).start()
            own().start()
            cp = fetch(sh_ref, 0)
            cp.start()
            cp.wait()

        for k in range(3):
            @pl.when((s == k) & (i == max(ni - 2, 0)))
            def _(k=k):
                landed(k, c, k).wait_recv()
                landed(k, c, 3 + k).start()
                landed(k, 1 - c, 3 + k).wait_recv()
                fetch(full_ref.at[slot(k)], (k + 1) % 2).start()

            @pl.when((s == k + 1) & (i == 0))
            def _(k=k):
                fetch(full_ref.at[slot(k)], (k + 1) % 2).wait()

        xv = x_ref[...]
        r = lax.rsqrt(jnp.mean(xv * xv, axis=1, keepdims=True) + NORM_EPS)
        hn = (xv * r * nw_ref[...]).astype(BF16)
        proj_ref[...] = _dot(hn, wbuf[s % 2])

        def hn_out(tile):
            return pltpu.make_async_copy(hn_s, hn_ref.at[pl.ds(pl.multiple_of(tile * tm, tm), tm), :], lsem.at[2])

        @pl.when(((s == 0) & (i > 0)) | ((s == 1) & (i == 0)))
        def _():
            hn_out(jnp.where(s == 0, i - 1, ni - 1)).wait()

        @pl.when(s == 0)
        def _():
            hn_s[...] = hn
            hn_out(i).start()

        @pl.when((s == 3) & (i == ni - 1))
        def _():
            mine = full_ref.at[me]
            _remote(mine, mine, ssem.at[6], rsem.at[6], sib).wait_recv()
            for k in range(3):
                ici(k).wait_send()
                landed(k, c, 3 + k).wait_send()
            own().wait_send()

    return pl.pallas_call(
        body, name="inproj",
        grid_spec=pltpu.PrefetchScalarGridSpec(
            num_scalar_prefetch=1, grid=(4, ni),
            in_specs=[pl.BlockSpec((tm, D_MODEL), lambda s, i, o: (i, 0)),
                      pl.BlockSpec((1, D_MODEL), lambda s, i, o: (0, 0)), ANY],
            out_specs=[pl.BlockSpec((tm, SHARD_W), lambda s, i, o: (i, o[s])), ANY, ANY],
            scratch_shapes=[pltpu.VMEM((2, D_MODEL, SHARD_W), BF16), pltpu.VMEM((tm, D_MODEL), BF16),
                            pltpu.SemaphoreType.DMA((7,)), pltpu.SemaphoreType.DMA((7,)),
                            pltpu.SemaphoreType.DMA((3,))]),
        out_shape=[jax.ShapeDtypeStruct((t, IN_W), F32), jax.ShapeDtypeStruct((t, D_MODEL), BF16),
                   jax.ShapeDtypeStruct((4, D_MODEL, SHARD_W), BF16)],
        compiler_params=_params(("arbitrary", "arbitrary")),
    )(order, x, norm_w.reshape(1, D_MODEL), w_sh)


def _lane128():
    return lax.broadcasted_iota(jnp.int32, (1, 128), 1)


def _head_sums(v):
    lo = _lane128() < 64
    s_lo = jnp.sum(jnp.where(lo, v, 0.0), axis=1, keepdims=True)
    s_hi = jnp.sum(jnp.where(lo, 0.0, v), axis=1, keepdims=True)
    return jnp.where(lo, s_lo, s_hi)


def _rot_half(t):
    first = (_lane128() % 64) < 32
    return jnp.where(first, -pltpu.roll(t, 96, 1), pltpu.roll(t, 32, 1))


def _head_rstd(t):
    return lax.rsqrt(_head_sums(t * t) * (1.0 / HEAD_DIM) + NORM_EPS)


def _prep_tile(t, w, cos, sin, r=None):
    r = _head_rstd(t) if r is None else r
    tn = t * r * w
    return tn * cos + _rot_half(tn) * sin


def _prep_tile_bwd(t, w, cos, sin, g, r=None):
    r = _head_rstd(t) if r is None else r
    d_tn = g * cos - _rot_half(g * sin)
    th = t * r
    dw = jnp.sum(d_tn * th, axis=0, keepdims=True)
    gh = d_tn * w
    m = _head_sums(gh * th) * (1.0 / HEAD_DIM)
    return r * (gh - th * m), dw


def _band_mask(n):
    qi = lax.broadcasted_iota(jnp.int32, (BLOCK, 2 * BLOCK), 0) + BLOCK
    ki = lax.broadcasted_iota(jnp.int32, (BLOCK, 2 * BLOCK), 1)
    rel = qi - ki
    return (rel >= 0) & (rel < BLOCK) & ((n > 0) | (ki >= BLOCK))


def _half_select(tile, half):
    lo = _lane128() < 64
    return jnp.where(lo if half == 0 else jnp.logical_not(lo), tile, 0.0)


def _stack_group(tiles, kv_half):
    rows = []
    for t in tiles:
        for half in range(2):
            piece = _half_select(t, half)
            rows.append(piece if half == kv_half else pltpu.roll(piece, 64, 1))
    return jnp.concatenate(rows, axis=0)


def _unstack_group(stacked, kv_half):
    tiles = []
    for i in range(2):
        acc = None
        for half in range(2):
            piece = _half_select(stacked[BLOCK * (2 * i + half):BLOCK * (2 * i + half + 1)], kv_half)
            piece = piece if half == kv_half else pltpu.roll(piece, 64, 1)
            acc = piece if acc is None else acc + piece
        tiles.append(acc)
    return tiles


def _stack_heads(tiles):
    zeros = jnp.zeros((4 * BLOCK, 128), F32)
    rows = []
    for g in range(N_KV_HEADS):
        half = _stack_group(tiles[2 * g:2 * g + 2], g % 2)
        rows.append(jnp.concatenate([half, zeros] if g < 2 else [zeros, half], axis=1))
    return jnp.concatenate(rows, axis=0)


def _unstack_heads(stacked):
    tiles = []
    for g in range(N_KV_HEADS):
        lanes = slice(0, 128) if g < 2 else slice(128, 256)
        tiles += _unstack_group(stacked[4 * BLOCK * g:4 * BLOCK * (g + 1), lanes], g % 2)
    return tiles


def _attn_specs(nb):
    last = nb - 1
    qi = lambda n: (jnp.minimum(n, last), 0)
    prev = lambda n: jnp.maximum(n - 1, 0)
    cur = lambda n: jnp.minimum(n, last)
    specs = [
        pl.BlockSpec((BLOCK, ATTN_W), qi),
        pl.BlockSpec((BLOCK, KV_W), lambda n: (cur(n), 4)),
        pl.BlockSpec((BLOCK, KV_W), lambda n: (prev(n), 4)),
        pl.BlockSpec((BLOCK, KV_W), lambda n: (cur(n), 5)),
        pl.BlockSpec((BLOCK, KV_W), lambda n: (prev(n), 5)),
        pl.BlockSpec((BLOCK, 512), lambda n: (cur(n), 3)),
        pl.BlockSpec((BLOCK, 512), lambda n: (cur(n), 4)),
        pl.BlockSpec((BLOCK, 256), lambda n: (cur(n), 0)),
        pl.BlockSpec((BLOCK, 256), lambda n: (prev(n), 0)),
        pl.BlockSpec((1, 128), lambda n: (0, 0)),
        pl.BlockSpec((1, 128), lambda n: (0, 0)),
        pl.BlockSpec((1, N_HEADS), lambda n: (0, 0)),
    ]
    return specs


def _rope_table(posf, invf):
    t = posf.shape[0]

    def body(p_ref, f_ref, o_ref):
        ang = p_ref[...] * f_ref[...]
        o_ref[...] = jnp.concatenate([jnp.cos(ang), jnp.sin(ang)], axis=1)

    return pl.pallas_call(
        body, name="rope_table", grid=(t // 512,),
        in_specs=[pl.BlockSpec((512, 1), lambda i: (i, 0)), pl.BlockSpec((1, 128), lambda i: (0, 0))],
        out_specs=pl.BlockSpec((512, 256), lambda i: (i, 0)),
        out_shape=jax.ShapeDtypeStruct((t, 256), F32), compiler_params=_params(("parallel",)),
    )(posf, invf)


def _attn_common(n, q_ref, kc_ref, kp_ref, vc_ref, vp_ref, rq_ref, rp_ref, qw_ref, kw_ref):
    cos_q, sin_q = rq_ref[:, 0:128], rq_ref[:, 128:256]
    cos_k = jnp.concatenate([rp_ref[:, 0:128], cos_q], axis=0)
    sin_k = jnp.concatenate([rp_ref[:, 128:256], sin_q], axis=0)
    k_raw = jnp.concatenate([kp_ref[...], kc_ref[...]], axis=0)
    vv = jnp.concatenate([vp_ref[...], vc_ref[...]], axis=0).astype(BF16)
    kk = [_prep_tile(k_raw[:, 128 * i:128 * i + 128], kw_ref[...], cos_k, sin_k).astype(BF16) for i in range(2)]
    vt = [vv[:, 128 * i:128 * i + 128] for i in range(2)]
    qv = q_ref[...]
    qr = [_head_rstd(qv[:, 128 * i:128 * i + 128]) for i in range(8)]
    qt = [_prep_tile(qv[:, 128 * i:128 * i + 128], qw_ref[...], cos_q, sin_q, qr[i]) for i in range(8)]
    return cos_q, sin_q, qr, kk, vt, qt


QK_SCALE = 1.0 / math.sqrt(HEAD_DIM)


def _group_sinks(sink_ref, g):
    return jnp.concatenate([jnp.broadcast_to(sink_ref[:, 4 * g + j:4 * g + j + 1], (BLOCK, 1)) for j in range(4)], axis=0)


def _group_softmax(q4, kk_t, sink, bias):
    s = _dot_nt(q4, kk_t) + bias
    m = jnp.maximum(jnp.max(s, axis=1, keepdims=True), sink)
    p = jnp.exp(s - m)
    es = jnp.exp(sink - m)
    inv = 1.0 / (jnp.sum(p, axis=1, keepdims=True) + es)
    return p * inv, es * inv


def _group_bias(n):
    return jnp.concatenate([jnp.where(_band_mask(n), 0.0, -1e30)] * 4, axis=0)


def _attn_fwd(proj, rope, qw, kw, sinks, later_shards):
    t = proj.shape[0]
    nb = t // BLOCK
    nt = len(later_shards)

    def body(q_ref, kc_ref, kp_ref, vc_ref, vp_ref, za0_ref, za1_ref, rq_ref, rp_ref, qw_ref, kw_ref,
             sink_ref, *rest):
        sh, o_ref, full = rest[:nt], rest[nt], rest[nt + 1:2 * nt + 1]
        ssem, rsem = rest[2 * nt + 1:]
        n = pl.program_id(0)
        start, wait = _bg_gather(sh, full, ssem, rsem)

        @pl.when(n == 0)
        def _():
            start()

        _, _, _, kk, vt, qt = _attn_common(n, q_ref, kc_ref, kp_ref, vc_ref, vp_ref, rq_ref, rp_ref, qw_ref, kw_ref)
        bias = jnp.concatenate([_group_bias(n)] * 4, axis=0)
        q16 = (_stack_heads(qt) * QK_SCALE).astype(BF16)
        sink16 = jnp.concatenate([_group_sinks(sink_ref, g) for g in range(N_KV_HEADS)], axis=0)
        p, _ = _group_softmax(q16, jnp.concatenate(kk, axis=1), sink16, bias)
        tiles = _unstack_heads(_dot(p.astype(BF16), jnp.concatenate(vt, axis=1)))
        za = jnp.concatenate([za0_ref[...], za1_ref[...]], axis=1)
        o_ref[...] = jnp.concatenate(tiles, axis=1) * _silu(za)

        @pl.when(n == nb - 1)
        def _():
            wait()

    return pl.pallas_call(
        body, name="attn_fwd", grid=(nb,), in_specs=_attn_specs(nb) + [ANY] * nt,
        out_specs=[pl.BlockSpec((BLOCK, ATTN_W), lambda n: (n, 0))] + [ANY] * nt,
        out_shape=[jax.ShapeDtypeStruct((t, ATTN_W), F32)]
        + [jax.ShapeDtypeStruct((4,) + s.shape, s.dtype) for s in later_shards],
        scratch_shapes=[pltpu.SemaphoreType.DMA((4 * nt,)), pltpu.SemaphoreType.DMA((4 * nt,))],
        compiler_params=_params(("arbitrary",)),
    )(proj, proj, proj, proj, proj, proj, proj, rope, rope, qw, kw, sinks, *later_shards)


def _attn_bwd(proj, rope, qw, kw, sinks, doa, du, dzs, outgoing):
    t = proj.shape[0]
    nb = t // BLOCK
    last = nb - 1
    nt = len(outgoing)

    def body(q_ref, kc_ref, kp_ref, vc_ref, vp_ref, za0_ref, za1_ref, rq_ref, rp_ref, qw_ref, kw_ref,
             sink_ref, doa_ref, du_ref, dzs_ref, *rest):
        src = rest[:nt]
        dp_ref, gq_ref, gk_ref, gs_ref = rest[nt:nt + 4]
        dst = rest[nt + 4:2 * nt + 4]
        dkk_s, dvv_s, ck_s, cv_s, dq_s, dza_s, ssem, rsem = rest[2 * nt + 4:]
        n = pl.program_id(0)
        start, wait = _bg_scatter_devices(src, dst, ssem, rsem)

        @pl.when(n == 0)
        def _():
            start()
            gq_ref[...] = jnp.zeros_like(gq_ref)
            gk_ref[...] = jnp.zeros_like(gk_ref)
            gs_ref[...] = jnp.zeros_like(gs_ref)
            ck_s[...] = jnp.zeros_like(ck_s)
            cv_s[...] = jnp.zeros_like(cv_s)
            dq_s[...] = jnp.zeros_like(dq_s)
            dza_s[...] = jnp.zeros_like(dza_s)

        dp_ref[:, 0:ATTN_W] = dq_s[...]
        dp_ref[:, ATTN_W + 2 * KV_W:2 * ATTN_W + 2 * KV_W] = dza_s[...]
        dp_ref[:, 2 * ATTN_W + 2 * KV_W:IN_W - SSM_W] = du_ref[...]
        dp_ref[:, IN_W - SSM_W:IN_W] = dzs_ref[...]

        @pl.when(n == nb)
        def _():
            dkk_s[...] = jnp.zeros_like(dkk_s)
            dvv_s[...] = jnp.zeros_like(dvv_s)

        @pl.when(n < nb)
        def _():
            cos_q, sin_q, qr, kk, vt, qt = _attn_common(n, q_ref, kc_ref, kp_ref, vc_ref, vp_ref, rq_ref, rp_ref,
                                                        qw_ref, kw_ref)
            bias = jnp.concatenate([_group_bias(n)] * 4, axis=0)
            za = jnp.concatenate([za0_ref[...], za1_ref[...]], axis=1)
            doa_v = doa_ref[...]
            do_full = doa_v * _silu(za)
            k_all, v_all = jnp.concatenate(kk, axis=1), jnp.concatenate(vt, axis=1)
            q_b = (_stack_heads(qt) * QK_SCALE).astype(BF16)
            do_b = _stack_heads([do_full[:, 128 * i:128 * i + 128] for i in range(8)]).astype(BF16)
            sink16 = jnp.concatenate([_group_sinks(sink_ref, g) for g in range(N_KV_HEADS)], axis=0)
            p, psink = _group_softmax(q_b, k_all, sink16, bias)
            p_b = p.astype(BF16)
            dp = _dot_nt(do_b, v_all)
            delta = jnp.sum(p * dp, axis=1, keepdims=True)
            ds_b = (p * (dp - delta)).astype(BF16)
            sd = psink * delta
            gsink = jnp.zeros((1, 128), F32)
            lane = _lane128()
            for h in range(N_HEADS):
                gsink = gsink + jnp.where(lane == h, -jnp.sum(sd[BLOCK * h:BLOCK * (h + 1)]), 0.0)
            o_tiles = _unstack_heads(_dot(p_b, v_all))
            dq_tiles = [d * QK_SCALE for d in _unstack_heads(_dot(ds_b, k_all))]
            dkk = [_dot_tn(ds_b, q_b)]
            dvv = [_dot_tn(p_b, do_b)]
            dza_s[...] = (doa_v * jnp.concatenate(o_tiles, axis=1) * _dsilu(za)).astype(BF16)
            qv = q_ref[...]
            gq = jnp.zeros((1, 128), F32)
            out = []
            for i in range(8):
                d, dw = _prep_tile_bwd(qv[:, 128 * i:128 * i + 128], qw_ref[...], cos_q, sin_q, dq_tiles[i], qr[i])
                out.append(d)
                gq = gq + dw
            dq_s[...] = jnp.concatenate(out, axis=1).astype(BF16)
            gq_ref[...] += gq
            gs_ref[...] += gsink
            dkk_s[...] = jnp.concatenate(dkk, axis=1)
            dvv_s[...] = jnp.concatenate(dvv, axis=1)

        cos_p, sin_p = rp_ref[:, 0:128], rp_ref[:, 128:256]
        dk_prev = ck_s[...] + dkk_s[0:BLOCK, :]
        kp = kp_ref[...]
        gk = jnp.zeros((1, 128), F32)
        out = []
        for i in range(2):
            d, dw = _prep_tile_bwd(kp[:, 128 * i:128 * i + 128], kw_ref[...], cos_p, sin_p,
                                   dk_prev[:, 128 * i:128 * i + 128])
            out.append(d)
            gk = gk + dw
        dp_ref[:, ATTN_W:ATTN_W + KV_W] = jnp.concatenate(out, axis=1).astype(BF16)
        dp_ref[:, ATTN_W + KV_W:ATTN_W + 2 * KV_W] = (cv_s[...] + dvv_s[0:BLOCK, :]).astype(BF16)
        gk_ref[...] += gk
        ck_s[...] = dkk_s[BLOCK:2 * BLOCK, :]
        cv_s[...] = dvv_s[BLOCK:2 * BLOCK, :]

        @pl.when(n == nb)
        def _():
            wait()

    qblk = lambda n: (jnp.minimum(n, last), 0)
    kblk = lambda n: (jnp.maximum(n - 1, 0), 0)
    vec = pl.BlockSpec((1, 128), lambda n: (0, 0))
    return pl.pallas_call(
        body, name="attn_bwd", grid=(nb + 1,),
        in_specs=_attn_specs(nb) + [pl.BlockSpec((BLOCK, ATTN_W), qblk), pl.BlockSpec((BLOCK, SSM_W), kblk),
                                    pl.BlockSpec((BLOCK, SSM_W), kblk)] + [ANY] * nt,
        out_specs=[pl.BlockSpec((BLOCK, IN_W), kblk), vec, vec, vec] + [ANY] * nt,
        out_shape=[jax.ShapeDtypeStruct((t, IN_W), BF16), jax.ShapeDtypeStruct((1, 128), F32),
                   jax.ShapeDtypeStruct((1, 128), F32), jax.ShapeDtypeStruct((1, 128), F32)]
        + [jax.ShapeDtypeStruct(a.shape, a.dtype) for a in outgoing],
        scratch_shapes=[pltpu.VMEM((2 * BLOCK, KV_W), F32), pltpu.VMEM((2 * BLOCK, KV_W), F32),
                        pltpu.VMEM((BLOCK, KV_W), F32), pltpu.VMEM((BLOCK, KV_W), F32),
                        pltpu.VMEM((BLOCK, ATTN_W), BF16), pltpu.VMEM((BLOCK, ATTN_W), BF16),
                        pltpu.SemaphoreType.DMA((7 * nt,)), pltpu.SemaphoreType.DMA((7 * nt,))],
        compiler_params=_params(("arbitrary",)),
    )(proj, proj, proj, proj, proj, proj, proj, rope, rope, qw, kw, sinks, doa, du, dzs, *outgoing)


def _cmul(ar, ai, br, bi):
    return ar * br - ai * bi, ar * bi + ai * br


def _zoh(a_re, a_im, delta):
    e = jnp.exp(a_re * delta)
    lr, li = e * jnp.cos(a_im * delta), e * jnp.sin(a_im * delta)
    inv = 1.0 / (a_re * a_re + a_im * a_im)
    fr, fi = _cmul(lr - 1.0, li, a_re * inv, -a_im * inv)
    return lr, li, fr, fi


def _ssm_prep(a_re, a_im, log_step, b_re, b_im, seg_len):
    n_sq = int(round(math.log2(seg_len)))
    assert 2 ** n_sq == seg_len

    def body(ar_ref, ai_ref, ls_ref, arx_ref, aix_ref, br_ref, bi_ref, lr_ref, li_ref, pr_ref, pi_ref, bbr_ref, bbi_ref):
        delta = jnp.exp(ls_ref[...])
        lr, li, _, _ = _zoh(ar_ref[...], ai_ref[...], delta)
        lr_ref[...] = lr
        li_ref[...] = li
        pr, pi = lr, li
        for _ in range(n_sq):
            pr, pi = _cmul(pr, pi, pr, pi)
        pr_ref[...] = pr
        pi_ref[...] = pi
        _, _, fr, fi = _zoh(arx_ref[...], aix_ref[...], delta)
        bbr, bbi = _cmul(fr, fi, br_ref[...], bi_ref[...])
        bbr_ref[...] = bbr
        bbi_ref[...] = bbi

    gp = jax.ShapeDtypeStruct((SSM_G, SSM_P), F32)
    gx = jax.ShapeDtypeStruct((SSM_G, SSM_P * SSM_H), F32)
    return pl.pallas_call(body, name="ssm_prep", out_shape=[gp, gp, gp, gp, gx, gx])(
        a_re, a_im, log_step.reshape(SSM_G, 1), jnp.repeat(a_re, SSM_H, axis=1), jnp.repeat(a_im, SSM_H, axis=1),
        b_re.reshape(SSM_G, SSM_P * SSM_H), b_im.reshape(SSM_G, SSM_P * SSM_H))


def _ssm_param_grads(a_re, a_im, log_step, b_re, b_im, dlam_re, dlam_im, dbb_re, dbb_im):
    def body(ar_ref, ai_ref, ls_ref, arx_ref, aix_ref, br_ref, bi_ref, dlr_ref, dli_ref, dbr_ref, dbi_ref,
             gar_ref, gai_ref, gls_ref, gbr_ref, gbi_ref):
        delta = jnp.exp(ls_ref[...])
        ar, ai = ar_ref[...], ai_ref[...]
        lr, li, fr, fi = _zoh(ar, ai, delta)
        _, _, frx, fix = _zoh(arx_ref[...], aix_ref[...], delta)
        dbr, dbi = dbr_ref[...], dbi_ref[...]
        br, bi = br_ref[...], bi_ref[...]
        gbr, gbi = _cmul(frx, -fix, dbr, dbi)
        gbr_ref[...] = gbr
        gbi_ref[...] = gbi
        tr, ti = _cmul(br, -bi, dbr, dbi)
        row = lax.broadcasted_iota(jnp.int32, (SSM_P * SSM_H, SSM_P), 0)
        col = lax.broadcasted_iota(jnp.int32, (SSM_P * SSM_H, SSM_P), 1)
        fold = (row // SSM_H == col).astype(F32)
        dfr = jnp.dot(tr, fold, precision=lax.Precision.HIGHEST, preferred_element_type=F32)
        dfi = jnp.dot(ti, fold, precision=lax.Precision.HIGHEST, preferred_element_type=F32)
        inv = 1.0 / (ar * ar + ai * ai)
        ilr, ili = ar * inv, -ai * inv
        t1r, t1i = _cmul(dfr, dfi, ilr, -ili)
        dlbr, dlbi = dlr_ref[...] + t1r, dli_ref[...] + t1i
        qr, qi = _cmul(fr, fi, ilr, ili)
        t2r, t2i = _cmul(dfr, dfi, qr, -qi)
        glr, gli = -t2r, -t2i
        dzr, dzi = _cmul(dlbr, dlbi, lr, -li)
        gar_ref[...] = glr + dzr * delta
        gai_ref[...] = gli + dzi * delta
        gls_ref[...] = jnp.sum(dzr * ar + dzi * ai, axis=1, keepdims=True) * delta

    gp = jax.ShapeDtypeStruct((SSM_G, SSM_P), F32)
    gx = jax.ShapeDtypeStruct((SSM_G, SSM_P * SSM_H), F32)
    return pl.pallas_call(body, name="ssm_param_grads",
                          out_shape=[gp, gp, jax.ShapeDtypeStruct((SSM_G, 1), F32), gx, gx])(
        a_re, a_im, log_step.reshape(SSM_G, 1), jnp.repeat(a_re, SSM_H, axis=1), jnp.repeat(a_im, SSM_H, axis=1),
        b_re.reshape(SSM_G, SSM_P * SSM_H), b_im.reshape(SSM_G, SSM_P * SSM_H), dlam_re, dlam_im, dbb_re, dbb_im)


def _block_diag_in(bb):
    w = jnp.tile(bb.reshape(SSM_GB, SSM_ST, SSM_H), (1, 1, 16))
    row = lax.broadcasted_iota(jnp.int32, (1, SSM_ST, SSM_CH), 1) // SSM_P
    col = lax.broadcasted_iota(jnp.int32, (1, SSM_ST, SSM_CH), 2) // SSM_H
    return jnp.where(row == col, w, 0.0)


def _block_diag_out(c):
    w = jnp.tile(c.reshape(SSM_GB, SSM_CH, SSM_P), (1, 1, 16))
    row = lax.broadcasted_iota(jnp.int32, (1, SSM_CH, SSM_ST), 1) // SSM_H
    col = lax.broadcasted_iota(jnp.int32, (1, SSM_CH, SSM_ST), 2) // SSM_P
    return jnp.where(row == col, w, 0.0)


SEG_ROWS = SCAN_ROWS // N_SEG


def _chunk_perm():
    out_row = lax.broadcasted_iota(jnp.int32, (SCAN_ROWS, SCAN_ROWS), 0)
    in_row = lax.broadcasted_iota(jnp.int32, (SCAN_ROWS, SCAN_ROWS), 1)
    return (out_row == N_SEG * (in_row % SEG_ROWS) + in_row // SEG_ROWS).astype(BF16)


def _chunk_rows(j, seg_len, s):
    return pl.ds(pl.multiple_of(s * seg_len + j * SEG_ROWS, SEG_ROWS), SEG_ROWS)


def _gather_chunk(ref, j, seg_len):
    return jnp.concatenate([ref[_chunk_rows(j, seg_len, s), :] for s in range(N_SEG)], axis=0)


def _scatter_chunk(ref, j, seg_len, val):
    for s in range(N_SEG):
        ref[_chunk_rows(j, seg_len, s), :] = val[s * SEG_ROWS:(s + 1) * SEG_ROWS]


def _interleave(perm, x_b):
    return _dot(perm, x_b).astype(BF16)


def _state_pieces():
    return [pl.ds(part * SSM_ST + k * SCAN_LW, SCAN_LW) for k in range(SSM_ST // SCAN_LW) for part in range(2)]


def _expand_states(x_b, w_ref, dst_ref):
    for cols in _state_pieces():
        dst_ref[:, cols] = _dot_nt(x_b, w_ref[cols, :])


def _contract_states(src_ref, w_ref):
    acc = None
    for cols in _state_pieces():
        part = _dot_nt(src_ref[:, cols].astype(BF16), w_ref[:, cols])
        acc = part if acc is None else acc + part
    return acc


def _scan_fwd(src_ref, dst_ref, lam_r_ref, lam_i_ref, init_ref, final_ref, steps):
    for k in range(SSM_ST // SCAN_LW):
        re = pl.ds(k * SCAN_LW, SCAN_LW)
        im = pl.ds(SSM_ST + k * SCAN_LW, SCAN_LW)
        lr, li = lam_r_ref[:, re], lam_i_ref[:, re]

        def step(i, carry, re=re, im=im, lr=lr, li=li):
            hr, hi = carry
            rows = pl.ds(pl.multiple_of(i * 8, 8), 8)
            nr = lr * hr - li * hi + src_ref[rows, re]
            ni = lr * hi + li * hr + src_ref[rows, im]
            if dst_ref is not None:
                dst_ref[rows, re] = nr
                dst_ref[rows, im] = ni
            return nr, ni

        hr, hi = lax.fori_loop(0, steps, step, (init_ref[:, re], init_ref[:, im]), unroll=True)
        final_ref[:, re] = hr
        final_ref[:, im] = hi


def _ssm_specs(t):
    col = lambda g: (0, g)
    gb3 = lambda g: (g, 0, 0)
    return dict(
        rows=pl.BlockSpec((t, SSM_CH), col),
        lam=pl.BlockSpec((None, N_SEG, SSM_ST), gb3),
        w_in=pl.BlockSpec((None, 2 * SSM_ST, SSM_CH), gb3),
        w_out=pl.BlockSpec((None, SSM_CH, 2 * SSM_ST), gb3),
        vec=pl.BlockSpec((1, SSM_CH), col),
    )


def _segment_states(x_ref, pw_r_ref, pw_i_ref, out_ref, reverse):
    re, im = pl.ds(0, SSM_ST), pl.ds(SSM_ST, SSM_ST)
    pr, pi = pw_r_ref[0:1, :], pw_i_ref[0:1, :]
    first = N_SEG - 1 if reverse else 0
    out_ref[first:first + 1, :] = jnp.zeros((1, 2 * SSM_ST), F32)
    order = range(N_SEG - 1, 0, -1) if reverse else range(N_SEG - 1)
    for s in order:
        d = s - 1 if reverse else s + 1
        hr, hi = out_ref[s:s + 1, re], out_ref[s:s + 1, im]
        if reverse:
            nr, ni = pr * hr + pi * hi, pr * hi - pi * hr
        else:
            nr, ni = pr * hr - pi * hi, pr * hi + pi * hr
        out_ref[d:d + 1, re] = nr + x_ref[s:s + 1, re]
        out_ref[d:d + 1, im] = ni + x_ref[s:s + 1, im]


def _ssm_fwd(proj, lam_r, lam_i, pw_r, pw_i, w_in, w_out, d_skip):
    t = proj.shape[0]
    seg_len = t // N_SEG
    nch = t // SCAN_ROWS
    steps = SCAN_ROWS // N_SEG
    sp = _ssm_specs(t)

    def body(u_ref, lr_ref, li_ref, pr_ref, pi_ref, wi_ref, wo_ref, d_ref, y_ref, hc_ref, bu_s, car_s, seg_s, ub_s,
             y0_s, y1_s):
        perm = _chunk_perm()
        car_s[...] = jnp.zeros_like(car_s)

        def chunk1(j, c):
            rows = pl.ds(pl.multiple_of(j * SCAN_ROWS, SCAN_ROWS), SCAN_ROWS)
            u_b = _interleave(perm, _gather_chunk(u_ref, j, seg_len).astype(BF16))
            ub_s[rows, :] = u_b
            _expand_states(u_b, wi_ref, bu_s)
            _scan_fwd(bu_s, None, lr_ref, li_ref, car_s, car_s, steps)
            return c

        lax.fori_loop(0, nch, chunk1, 0)
        _segment_states(car_s, pr_ref, pi_ref, seg_s, reverse=False)
        car_s[...] = seg_s[...]

        def chunk2(j, c):
            rows = pl.ds(pl.multiple_of(j * SCAN_ROWS, SCAN_ROWS), SCAN_ROWS)
            _expand_states(ub_s[rows, :], wi_ref, bu_s)
            hc_ref[j] = car_s[...]
            _scan_fwd(bu_s, bu_s, lr_ref, li_ref, car_s, car_s, steps)
            yv = _contract_states(bu_s, wo_ref)
            y0_s[...] = yv[:, 0:128]
            y1_s[...] = yv[:, 128:256]
            for s in range(N_SEG):
                nat = _chunk_rows(j, seg_len, s)
                sub = pl.ds(s, SEG_ROWS, stride=N_SEG)
                y_ref[nat, :] = jnp.concatenate([y0_s[sub, :], y1_s[sub, :]], axis=1) + d_ref[...] * u_ref[nat, :]
            return c

        lax.fori_loop(0, nch, chunk2, 0)

    u_cols = 2560 // SSM_CH
    return pl.pallas_call(
        body, name="ssm_fwd", grid=(SSM_GB,),
        in_specs=[pl.BlockSpec((t, SSM_CH), lambda g: (0, u_cols + g)), sp["lam"], sp["lam"], sp["lam"], sp["lam"],
                  sp["w_in"], sp["w_out"], sp["vec"]],
        out_specs=[sp["rows"], pl.BlockSpec((None, nch, N_SEG, 2 * SSM_ST), lambda g: (g, 0, 0, 0))],
        out_shape=[jax.ShapeDtypeStruct((t, SSM_W), F32), jax.ShapeDtypeStruct((SSM_GB, nch, N_SEG, 2 * SSM_ST), F32)],
        scratch_shapes=[pltpu.VMEM((SCAN_ROWS, 2 * SSM_ST), F32), pltpu.VMEM((N_SEG, 2 * SSM_ST), F32),
                        pltpu.VMEM((N_SEG, 2 * SSM_ST), F32), pltpu.VMEM((t, SSM_CH), BF16),
                        pltpu.VMEM((SCAN_ROWS, 128), F32), pltpu.VMEM((SCAN_ROWS, 128), F32)],
        compiler_params=_params(("parallel",)),
    )(proj, lam_r, lam_i, pw_r, pw_i, w_in, w_out, d_skip)


def _group_blocks(full):
    row_g = lax.broadcasted_iota(jnp.int32, (SSM_CH, 128), 0) // SSM_H
    lane = lax.broadcasted_iota(jnp.int32, (SSM_CH, 128), 1)
    parts = []
    for k in range(2):
        acc = None
        for v in range(SSM_ST // 128):
            tile = full[:, k * SSM_ST + 128 * v:k * SSM_ST + 128 * (v + 1)]
            tile = jnp.where(row_g == 2 * v + lane // SSM_P, tile, 0.0)
            acc = tile if acc is None else acc + tile
        parts.append(acc + pltpu.roll(acc, SSM_P, 1))
    return jnp.where(lane < SSM_P, parts[0], parts[1])


def _ssm_bwd(proj, dy, hc, lam_r, lam_i, pw_r, pw_i, w_in, w_out, d_skip):
    t = proj.shape[0]
    seg_len = t // N_SEG
    nch = t // SCAN_ROWS
    steps = SCAN_ROWS // N_SEG
    sp = _ssm_specs(t)

    def body(u_ref, dy_ref, hc_ref, lr_ref, li_ref, pr_ref, pi_ref, wi_ref, wo_ref, d_ref,
             du_ref, gbi_ref, gbo_ref, glam_ref, gd_ref, bu_s, h_s, e_s, car_s, seg_s, acc_s, gwi_ref, gwo_ref,
             dyb_s):
        perm = _chunk_perm()

        def chunk_rows(j):
            return pl.ds(pl.multiple_of(j * SCAN_ROWS, SCAN_ROWS), SCAN_ROWS)

        def interleaved(ref, j):
            return _interleave(perm, _gather_chunk(ref, j, seg_len).astype(BF16))

        def load_e(j):
            dy_b = interleaved(dy_ref, j)
            dyb_s[chunk_rows(j), :] = dy_b
            for cols in _state_pieces():
                e_s[:, cols] = _dot(dy_b, wo_ref[:, cols])

        def scan_rev(j, accumulate):
            for k in range(SSM_ST // SCAN_LW):
                re = pl.ds(k * SCAN_LW, SCAN_LW)
                im = pl.ds(SSM_ST + k * SCAN_LW, SCAN_LW)
                lr, li = lr_ref[:, re], li_ref[:, re]

                def step(ii, carry, re=re, im=im, lr=lr, li=li):
                    i = steps - 1 - ii
                    rows = pl.ds(pl.multiple_of(i * 8, 8), 8)
                    if accumulate:
                        gr, gi, ar, ai = carry
                    else:
                        gr, gi = carry
                    nr = lr * gr + li * gi + e_s[rows, re]
                    ni = lr * gi - li * gr + e_s[rows, im]
                    if not accumulate:
                        return nr, ni
                    e_s[rows, re] = nr
                    e_s[rows, im] = ni
                    pr_, pi_ = h_s[rows, re], h_s[rows, im]
                    return nr, ni, ar + nr * pr_ + ni * pi_, ai + ni * pr_ - nr * pi_

                init = (car_s[:, re], car_s[:, im])
                if accumulate:
                    init = init + (acc_s[:, re], acc_s[:, im])
                out = lax.fori_loop(0, steps, step, init, unroll=True)
                car_s[:, re] = out[0]
                car_s[:, im] = out[1]
                if accumulate:
                    acc_s[:, re] = out[2]
                    acc_s[:, im] = out[3]

        car_s[...] = jnp.zeros_like(car_s)

        def pass1(jj, c):
            load_e(nch - 1 - jj)
            scan_rev(nch - 1 - jj, False)
            return c

        lax.fori_loop(0, nch, pass1, 0)
        _segment_states(car_s, pr_ref, pi_ref, seg_s, reverse=True)
        car_s[...] = seg_s[...]
        acc_s[...] = jnp.zeros_like(acc_s)
        gwi_ref[...] = jnp.zeros_like(gwi_ref)
        gwo_ref[...] = jnp.zeros_like(gwo_ref)
        gd_ref[...] = jnp.zeros_like(gd_ref)

        def pass2(jj, c):
            j = nch - 1 - jj
            u_b, dy_b = interleaved(u_ref, j), dyb_s[chunk_rows(j), :]
            _expand_states(u_b, wi_ref, bu_s)
            h_s[0:N_SEG, :] = hc_ref[j]
            seg_s[...] = hc_ref[j]
            h_now = h_s.at[pl.ds(N_SEG, SCAN_ROWS), :]
            _scan_fwd(bu_s, h_now, lr_ref, li_ref, seg_s, seg_s, steps)
            for cols in _state_pieces():
                e_s[:, cols] = _dot(dy_b, wo_ref[:, cols])
            scan_rev(j, True)
            du = d_ref[...] * dy_b.astype(F32)
            for cols in _state_pieces():
                g_b = e_s[:, cols].astype(BF16)
                du = du + _dot(g_b, wi_ref[cols, :])
                gwi_ref[:, cols] += _dot_tn(u_b, g_b)
                gwo_ref[:, cols] += _dot_tn(dy_b, h_now[:, cols].astype(BF16))
            _scatter_chunk(du_ref, j, seg_len, _dot_tn(perm, du.astype(BF16)).astype(du_ref.dtype))
            gd_ref[...] += jnp.sum(_gather_chunk(dy_ref, j, seg_len) * _gather_chunk(u_ref, j, seg_len), axis=0,
                                   keepdims=True)
            return c

        lax.fori_loop(0, nch, pass2, 0)
        glam_ref[...] = jnp.sum(acc_s[...], axis=0, keepdims=True)
        gbi_ref[...] = _group_blocks(gwi_ref[...])
        gbo_ref[...] = _group_blocks(gwo_ref[...])

    mat = pl.BlockSpec((None, SSM_CH, 2 * SSM_P), lambda g: (g, 0, 0))
    u_cols = 2560 // SSM_CH
    return pl.pallas_call(
        body, name="ssm_bwd", grid=(SSM_GB,),
        in_specs=[pl.BlockSpec((t, SSM_CH), lambda g: (0, u_cols + g)), sp["rows"],
                  pl.BlockSpec((None, nch, N_SEG, 2 * SSM_ST), lambda g: (g, 0, 0, 0)),
                  sp["lam"], sp["lam"], sp["lam"], sp["lam"], sp["w_in"], sp["w_out"], sp["vec"]],
        out_specs=[sp["rows"], mat, mat, pl.BlockSpec((None, 1, 2 * SSM_ST), lambda g: (g, 0, 0)), sp["vec"]],
        out_shape=[jax.ShapeDtypeStruct((t, SSM_W), BF16), jax.ShapeDtypeStruct((SSM_GB, SSM_CH, 2 * SSM_P), F32),
                   jax.ShapeDtypeStruct((SSM_GB, SSM_CH, 2 * SSM_P), F32),
                   jax.ShapeDtypeStruct((SSM_GB, 1, 2 * SSM_ST), F32), jax.ShapeDtypeStruct((1, SSM_W), F32)],
        scratch_shapes=[pltpu.VMEM((SCAN_ROWS, 2 * SSM_ST), F32), pltpu.VMEM((SCAN_ROWS + N_SEG, 2 * SSM_ST), F32),
                        pltpu.VMEM((SCAN_ROWS, 2 * SSM_ST), F32), pltpu.VMEM((N_SEG, 2 * SSM_ST), F32),
                        pltpu.VMEM((N_SEG, 2 * SSM_ST), F32), pltpu.VMEM((N_SEG, 2 * SSM_ST), F32),
                        pltpu.VMEM((SSM_CH, 2 * SSM_ST), F32), pltpu.VMEM((SSM_CH, 2 * SSM_ST), F32),
                        pltpu.VMEM((t, SSM_CH), BF16)],
        compiler_params=_params(("parallel",)),
    )(proj, dy, hc, lam_r, lam_i, pw_r, pw_i, w_in, w_out, d_skip)


def _z_ssm_specs(tm):
    return [pl.BlockSpec((tm, 512), lambda i: (i, 7)), pl.BlockSpec((tm, 512), lambda i: (i, 8))]


def _glu_fwd(y, proj, w_glu, b_glu):
    t = y.shape[0]
    tm = 512

    def body(y_ref, z0_ref, z1_ref, w_ref, b_ref, o_ref, yg_ref):
        yg = _gelu(y_ref[...])
        yg_b = yg.astype(BF16)
        a = _dot(yg_b, w_ref[...]) + b_ref[...]
        z = jnp.concatenate([z0_ref[...], z1_ref[...]], axis=1)
        o_ref[...] = yg * _sigmoid(a) * _silu(z)
        yg_ref[...] = yg_b

    row = pl.BlockSpec((tm, SSM_W), lambda i: (i, 0))
    return pl.pallas_call(
        body, name="glu_fwd", grid=(t // tm,),
        in_specs=[row] + _z_ssm_specs(tm) + [pl.BlockSpec((SSM_W, SSM_W), lambda i: (0, 0)),
                                            pl.BlockSpec((1, SSM_W), lambda i: (0, 0))],
        out_specs=[row, row],
        out_shape=[jax.ShapeDtypeStruct((t, SSM_W), F32), jax.ShapeDtypeStruct((t, SSM_W), BF16)],
        compiler_params=_params(("parallel",)),
    )(y, proj, proj, w_glu, b_glu)


def _glu_bwd(y, proj, dos, w_glu, b_glu):
    t = y.shape[0]
    tm = 512

    def body(y_ref, z0_ref, z1_ref, do_ref, w_ref, b_ref, dy_ref, dz_ref, da_ref, gb_ref):
        @pl.when(pl.program_id(0) == 0)
        def _():
            gb_ref[...] = jnp.zeros_like(gb_ref)

        z = jnp.concatenate([z0_ref[...], z1_ref[...]], axis=1)
        yv, do = y_ref[...], do_ref[...]
        yg = _gelu(yv)
        sg = _sigmoid(_dot(yg.astype(BF16), w_ref[...]) + b_ref[...])
        dy2 = do * _silu(z)
        dz_ref[...] = (do * yg * sg * _dsilu(z)).astype(BF16)
        da = dy2 * yg * sg * (1.0 - sg)
        da_b = da.astype(BF16)
        da_ref[...] = da_b
        gb_ref[...] += jnp.sum(da, axis=0, keepdims=True)
        dyg = dy2 * sg + _dot_nt(da_b, w_ref[...])
        dy_ref[...] = dyg * _dgelu(yv)

    row = pl.BlockSpec((tm, SSM_W), lambda i: (i, 0))
    vec = pl.BlockSpec((1, SSM_W), lambda i: (0, 0))
    return pl.pallas_call(
        body, name="glu_bwd", grid=(t // tm,),
        in_specs=[row] + _z_ssm_specs(tm) + [row, pl.BlockSpec((SSM_W, SSM_W), lambda i: (0, 0)), vec],
        out_specs=[row, row, row, vec],
        out_shape=[jax.ShapeDtypeStruct((t, SSM_W), F32), jax.ShapeDtypeStruct((t, SSM_W), BF16),
                   jax.ShapeDtypeStruct((t, SSM_W), BF16), jax.ShapeDtypeStruct((1, SSM_W), F32)],
        compiler_params=_params(("arbitrary",)),
    )(y, proj, proj, dos, w_glu, b_glu)


def _rms(o):
    return lax.rsqrt(jnp.mean(o * o, axis=1, keepdims=True) + NORM_EPS)


def _outproj(oa, os_, aw, sw, w_out, x, target):
    t = x.shape[0]
    tm = 256

    def body(oa_ref, os_ref, aw_ref, sw_ref, w_ref, x_ref, t_ref, mg_ref, do_ref, ls_ref):
        @pl.when(pl.program_id(0) == 0)
        def _():
            ls_ref[...] = jnp.zeros_like(ls_ref)

        a, s = oa_ref[...], os_ref[...]
        merged = jnp.concatenate([a * _rms(a) * aw_ref[...], s * _rms(s) * sw_ref[...]], axis=1).astype(BF16)
        mg_ref[...] = merged
        err = x_ref[...] + _dot(merged, w_ref[...]) - t_ref[...]
        do_ref[...] = err * (1.0 / D_MODEL)
        ls_ref[...] += jnp.sum(err * err)

    half = pl.BlockSpec((tm, ATTN_W), lambda i: (i, 0))
    full = pl.BlockSpec((tm, D_MODEL), lambda i: (i, 0))
    vec = pl.BlockSpec((1, ATTN_W), lambda i: (0, 0))
    return pl.pallas_call(
        body, name="outproj", grid=(t // tm,),
        in_specs=[half, half, vec, vec, pl.BlockSpec((D_MODEL, D_MODEL), lambda i: (0, 0)), full, full],
        out_specs=[full, full, pl.BlockSpec((8, 128), lambda i: (0, 0))],
        out_shape=[jax.ShapeDtypeStruct((t, D_MODEL), BF16), jax.ShapeDtypeStruct((t, D_MODEL), F32),
                   jax.ShapeDtypeStruct((8, 128), F32)],
        compiler_params=_params(("arbitrary",)),
    )(oa, os_, aw, sw, w_out, x, target)


def _outproj_bwd(dout, oa, os_, aw, sw, w_out):
    t = dout.shape[0]
    tm = 256

    def norm_bwd(o, w, dm):
        r = _rms(o)
        yh = o * r
        gh = dm * w
        return r * (gh - yh * jnp.mean(gh * yh, axis=1, keepdims=True)), jnp.sum(dm * yh, axis=0, keepdims=True)

    def body(do_ref, oa_ref, os_ref, aw_ref, sw_ref, w_ref, da_ref, ds_ref, ga_ref, gs_ref):
        @pl.when(pl.program_id(0) == 0)
        def _():
            ga_ref[...] = jnp.zeros_like(ga_ref)
            gs_ref[...] = jnp.zeros_like(gs_ref)

        dm = _dot_nt(do_ref[...].astype(BF16), w_ref[...])
        da, ga = norm_bwd(oa_ref[...], aw_ref[...], dm[:, :ATTN_W])
        ds, gs = norm_bwd(os_ref[...], sw_ref[...], dm[:, ATTN_W:])
        da_ref[...] = da
        ds_ref[...] = ds
        ga_ref[...] += ga
        gs_ref[...] += gs

    half = pl.BlockSpec((tm, ATTN_W), lambda i: (i, 0))
    full = pl.BlockSpec((tm, D_MODEL), lambda i: (i, 0))
    vec = pl.BlockSpec((1, ATTN_W), lambda i: (0, 0))
    return pl.pallas_call(
        body, name="outproj_bwd", grid=(t // tm,),
        in_specs=[full, half, half, vec, vec, pl.BlockSpec((D_MODEL, D_MODEL), lambda i: (0, 0))],
        out_specs=[half, half, vec, vec],
        out_shape=[jax.ShapeDtypeStruct((t, ATTN_W), F32), jax.ShapeDtypeStruct((t, ATTN_W), F32),
                   jax.ShapeDtypeStruct((1, ATTN_W), F32), jax.ShapeDtypeStruct((1, ATTN_W), F32)],
        compiler_params=_params(("arbitrary",)),
    )(dout, oa, os_, aw, sw, w_out)


def _inproj_bwd(dproj, w_slabs, x, norm_w, dout, outgoing):
    t = x.shape[0]
    tm = 512
    nc = 4
    nt = len(outgoing)
    ni = t // tm

    def body(dp_ref, w_ref, x_ref, nw_ref, do_ref, *rest):
        src, (gx_ref, gw_ref), dst = rest[:nt], rest[nt:nt + 2], rest[nt + 2:2 * nt + 2]
        acc_ref, ssem, rsem = rest[2 * nt + 2:]
        i, j = pl.program_id(0), pl.program_id(1)
        start, wait = _bg_scatter_chips(src, dst, ssem, rsem)

        @pl.when((i == 0) & (j == 0))
        def _():
            start()
            gw_ref[...] = jnp.zeros_like(gw_ref)

        @pl.when(j == 0)
        def _():
            acc_ref[...] = jnp.zeros_like(acc_ref)

        acc_ref[...] += _dot_nt(dp_ref[...], w_ref[...])

        @pl.when(j == nc - 1)
        def _():
            xv = x_ref[...]
            r = lax.rsqrt(jnp.mean(xv * xv, axis=1, keepdims=True) + NORM_EPS)
            yh = xv * r
            dh = acc_ref[...]
            gh = dh * nw_ref[...]
            gx_ref[...] = do_ref[...] + r * (gh - yh * jnp.mean(gh * yh, axis=1, keepdims=True))
            gw_ref[...] += jnp.sum(dh * yh, axis=0, keepdims=True)

        @pl.when((i == ni - 1) & (j == nc - 1))
        def _():
            wait()

    full = pl.BlockSpec((tm, D_MODEL), lambda i, j: (i, 0))
    vec = pl.BlockSpec((1, D_MODEL), lambda i, j: (0, 0))
    return pl.pallas_call(
        body, name="inproj_bwd", grid=(ni, nc),
        in_specs=[pl.BlockSpec((tm, SHARD_W), lambda i, j: (i, j)),
                  pl.BlockSpec((None, D_MODEL, SHARD_W), lambda i, j: (j, 0, 0)), full, vec, full] + [ANY] * nt,
        out_specs=[full, vec] + [ANY] * nt,
        out_shape=[jax.ShapeDtypeStruct((t, D_MODEL), F32), jax.ShapeDtypeStruct((1, D_MODEL), F32)]
        + [jax.ShapeDtypeStruct(a.shape, a.dtype) for a in outgoing],
        scratch_shapes=[pltpu.VMEM((tm, D_MODEL), F32), pltpu.SemaphoreType.DMA((3 * nt,)),
                        pltpu.SemaphoreType.DMA((3 * nt,))],
        compiler_params=_params(("arbitrary", "arbitrary")),
    )(dproj, w_slabs, x, norm_w.reshape(1, D_MODEL), dout, *outgoing)


def _adamw_math(w_ref, g_ref, m_ref, v_ref, d_ref, nm_ref, nv_ref):
    gv = g_ref[...]
    nm = ADAM_B1 * m_ref[...] + (1.0 - ADAM_B1) * gv
    nv = ADAM_B2 * v_ref[...] + (1.0 - ADAM_B2) * (gv * gv)
    m_hat = nm / (1.0 - ADAM_B1 ** ADAM_STEP)
    v_hat = nv / (1.0 - ADAM_B2 ** ADAM_STEP)
    d_ref[...] = -ADAM_LR * (m_hat / (jnp.sqrt(v_hat) + ADAM_EPS) + ADAM_WD * w_ref[...])
    nm_ref[...] = nm
    nv_ref[...] = nv


def _adamw_halves(w, mine, theirs, m, v, c_idx, *, rows, name):
    hr, cols = mine.shape
    nblk = hr // rows

    def body(c_ref, w_ref, a_ref, b_ref, m_ref, v_ref, g_ref, d_ref, nm_ref, nv_ref):
        g_ref[...] = jnp.where(pl.program_id(0) == c_ref[0], a_ref[...], b_ref[...])
        _adamw_math(w_ref, g_ref, m_ref, v_ref, d_ref, nm_ref, nv_ref)

    full = pl.BlockSpec((rows, cols), lambda h, i, c: (h * nblk + i, 0))
    part = pl.BlockSpec((rows, cols), lambda h, i, c: (i, 0))
    shp = jax.ShapeDtypeStruct((2 * hr, cols), F32)
    return pl.pallas_call(
        body, name=name,
        grid_spec=pltpu.PrefetchScalarGridSpec(num_scalar_prefetch=1, grid=(2, nblk),
                                               in_specs=[full, part, part, full, full], out_specs=[full] * 4),
        out_shape=[shp] * 4, compiler_params=_params(("parallel", "parallel")),
    )(c_idx, w, mine, theirs, m, v)


def _adamw(w, g, m, v, *, rows, name):
    r, c = w.shape

    def body(w_ref, g_ref, m_ref, v_ref, d_ref, nm_ref, nv_ref):
        _adamw_math(w_ref, g_ref, m_ref, v_ref, d_ref, nm_ref, nv_ref)

    blk = pl.BlockSpec((rows, c), lambda i: (i, 0))
    shp = jax.ShapeDtypeStruct((r, c), F32)
    return pl.pallas_call(body, name=name, grid=(r // rows,), in_specs=[blk] * 4, out_specs=[blk] * 3,
                          out_shape=[shp] * 3, compiler_params=_params(("parallel",)))(w, g, m, v)


def _remote(src, dst, ssem, rsem, dev):
    return pltpu.make_async_remote_copy(src_ref=src, dst_ref=dst, send_sem=ssem, recv_sem=rsem, device_id=dev,
                                        device_id_type=pl.DeviceIdType.MESH)


def _mesh_pos():
    return lax.axis_index("x"), lax.axis_index("y"), lax.axis_index("c")


def _other_chips(x, y):
    return [(1 - x, y), (x, 1 - y), (1 - x, 1 - y)]


def _flips():
    return [(dx, dy, dc) for dx in (0, 1) for dy in (0, 1) for dc in (0, 1) if (dx, dy, dc) != (0, 0, 0)]


def _background(sends, arrivals):
    def start():
        for cp in sends():
            cp.start()

    def wait():
        for cp in arrivals():
            cp.wait_recv()
        for cp in sends():
            cp.wait_send()

    return start, wait


def _bg_gather(sh, full, ssem, rsem):
    x, y, c = _mesh_pos()
    me = 2 * x + y
    peers = [(px, py, c) for px, py in _other_chips(x, y)] + [(x, y, 1 - c)]
    slots = [2 * px + py for px, py in _other_chips(x, y)] + [me]
    pairs = [(i, k) for i in range(len(sh)) for k in range(4)]
    return _background(
        lambda: [_remote(sh[i], full[i].at[me], ssem.at[4 * i + k], rsem.at[4 * i + k], peers[k]) for i, k in pairs],
        lambda: [_remote(full[i].at[slots[k]], full[i].at[slots[k]], ssem.at[4 * i + k], rsem.at[4 * i + k], peers[k])
                 for i, k in pairs])


def _bg_scatter_devices(src, dst, ssem, rsem):
    x, y, c = _mesh_pos()
    me = 4 * x + 2 * y + c
    peers = []
    for dx, dy, dc in _flips():
        px, py, pc = jnp.bitwise_xor(x, dx), jnp.bitwise_xor(y, dy), jnp.bitwise_xor(c, dc)
        peers.append(((px, py, pc), 4 * px + 2 * py + pc))
    pairs = [(i, k) for i in range(len(src)) for k in range(7)]
    return _background(
        lambda: [_remote(src[i].at[peers[k][1]], dst[i].at[me], ssem.at[7 * i + k], rsem.at[7 * i + k], peers[k][0])
                 for i, k in pairs],
        lambda: [_remote(dst[i].at[peers[k][1]], dst[i].at[peers[k][1]], ssem.at[7 * i + k], rsem.at[7 * i + k],
                         peers[k][0]) for i, k in pairs])


def _bg_scatter_chips(src, dst, ssem, rsem):
    x, y, c = _mesh_pos()
    me = 2 * x + y
    chips = _other_chips(x, y)
    pairs = [(i, k) for i in range(len(src)) for k in range(3)]
    slot = lambda k: 2 * chips[k][0] + chips[k][1]
    return _background(
        lambda: [_remote(src[i].at[slot(k)], dst[i].at[me], ssem.at[3 * i + k], rsem.at[3 * i + k], (*chips[k], c))
                 for i, k in pairs],
        lambda: [_remote(dst[i].at[slot(k)], dst[i].at[slot(k)], ssem.at[3 * i + k], rsem.at[3 * i + k], (*chips[k], c))
                 for i, k in pairs])


def _pair_swap(arrays):
    nt = len(arrays)

    def body(*refs):
        src, dst = refs[:nt], refs[nt:2 * nt]
        ssem, rsem = refs[2 * nt:]
        x, y, c = _mesh_pos()
        cps = [_remote(src[i].at[:, 1 - c], dst[i], ssem.at[i], rsem.at[i], (x, y, 1 - c)) for i in range(nt)]
        for cp in cps:
            cp.start()
        for cp in cps:
            cp.wait_recv()
        for cp in cps:
            cp.wait_send()

    return pl.pallas_call(
        body, name="pair_swap", in_specs=[ANY] * nt, out_specs=[ANY] * nt,
        out_shape=[jax.ShapeDtypeStruct((4,) + a.shape[2:], a.dtype) for a in arrays],
        scratch_shapes=[pltpu.SemaphoreType.DMA((nt,)), pltpu.SemaphoreType.DMA((nt,))],
    )(*arrays)


def _half_swap_and_scatter(arrays, packed):
    nt = len(arrays)

    def body(*refs):
        src, pk = refs[:nt], refs[nt]
        dst, slots = refs[nt + 1:2 * nt + 1], refs[2 * nt + 1]
        ssem, rsem, lsem = refs[2 * nt + 2:]
        x, y, c = _mesh_pos()
        me = 4 * x + 2 * y + c
        local = pltpu.make_async_copy(pk.at[me], slots.at[me], lsem)
        local.start()
        cps = [_remote(src[i], dst[i], ssem.at[i], rsem.at[i], (x, y, 1 - c)) for i in range(nt)]
        peers = []
        for k, (dx, dy, dc) in enumerate(_flips()):
            px, py, pc = jnp.bitwise_xor(x, dx), jnp.bitwise_xor(y, dy), jnp.bitwise_xor(c, dc)
            peers.append(4 * px + 2 * py + pc)
            cps.append(_remote(pk.at[peers[k]], slots.at[me], ssem.at[nt + k], rsem.at[nt + k], (px, py, pc)))
        for cp in cps:
            cp.start()
        for i in range(nt):
            cps[i].wait_recv()
        for k, peer in enumerate(peers):
            landed = slots.at[peer]
            _remote(landed, landed, ssem.at[nt + k], rsem.at[nt + k], (x, y, c)).wait_recv()
        for cp in cps:
            cp.wait_send()
        local.wait()

    return pl.pallas_call(
        body, name="half_swap_small_scatter", in_specs=[ANY] * (nt + 1), out_specs=[ANY] * (nt + 1),
        out_shape=[jax.ShapeDtypeStruct(a.shape, a.dtype) for a in arrays]
        + [jax.ShapeDtypeStruct(packed.shape, packed.dtype)],
        scratch_shapes=[pltpu.SemaphoreType.DMA((nt + 7,)), pltpu.SemaphoreType.DMA((nt + 7,)),
                        pltpu.SemaphoreType.DMA],
    )(*arrays, packed)


def _half_swap(arrays):
    nt = len(arrays)

    def body(*refs):
        src, dst = refs[:nt], refs[nt:2 * nt]
        ssem, rsem = refs[2 * nt:]
        x, y, c = _mesh_pos()
        cps = [_remote(src[i], dst[i], ssem.at[i], rsem.at[i], (x, y, 1 - c)) for i in range(nt)]
        for cp in cps:
            cp.start()
        for cp in cps:
            cp.wait_recv()
        for cp in cps:
            cp.wait_send()

    return pl.pallas_call(
        body, name="half_swap", in_specs=[ANY] * nt, out_specs=[ANY] * nt,
        out_shape=[jax.ShapeDtypeStruct(a.shape, a.dtype) for a in arrays],
        scratch_shapes=[pltpu.SemaphoreType.DMA((nt,)), pltpu.SemaphoreType.DMA((nt,))],
    )(*arrays)


def _exchange_slices(src, scatter, name):
    def body(src_ref, dst_ref, ssem, rsem, lsem):
        x, y, c = _mesh_pos()
        me = 4 * x + 2 * y + c
        local = pltpu.make_async_copy(src_ref.at[me] if scatter else src_ref, dst_ref.at[me], lsem)
        local.start()
        cps = []
        for k, (dx, dy, dc) in enumerate(_flips()):
            px, py, pc = jnp.bitwise_xor(x, dx), jnp.bitwise_xor(y, dy), jnp.bitwise_xor(c, dc)
            peer = 4 * px + 2 * py + pc
            cp = _remote(src_ref.at[peer] if scatter else src_ref, dst_ref.at[me], ssem.at[k], rsem.at[k],
                         (px, py, pc))
            cp.start()
            cps.append((cp, peer))
        for k, (cp, peer) in enumerate(cps):
            slot = dst_ref.at[peer]
            _remote(slot, slot, ssem.at[k], rsem.at[k], (x, y, c)).wait_recv()
        for cp, _ in cps:
            cp.wait_send()
        local.wait()

    return pl.pallas_call(
        body, name=name, in_specs=[ANY], out_specs=ANY,
        out_shape=jax.ShapeDtypeStruct((8,) + src.shape[-2:], src.dtype),
        scratch_shapes=[pltpu.SemaphoreType.DMA((7,)), pltpu.SemaphoreType.DMA((7,)), pltpu.SemaphoreType.DMA],
    )(src)


def _add_halves(g, recv, c_idx, *, rows, name):
    _, _, hr, cols = g.shape

    def body(c_ref, g_ref, r_ref, o_ref):
        o_ref[...] = (g_ref[...] + r_ref[...].astype(F32)).astype(BF16)

    return pl.pallas_call(
        body, name=name,
        grid_spec=pltpu.PrefetchScalarGridSpec(
            num_scalar_prefetch=1, grid=(4, hr // rows),
            in_specs=[pl.BlockSpec((None, None, rows, cols), lambda j, i, c: (j, c[0], i, 0)),
                      pl.BlockSpec((None, rows, cols), lambda j, i, c: (j, i, 0))],
            out_specs=pl.BlockSpec((None, rows, cols), lambda j, i, c: (j, i, 0))),
        out_shape=jax.ShapeDtypeStruct((4, hr, cols), BF16),
        compiler_params=_params(("parallel", "parallel")),
    )(c_idx, g, recv)


def _sum_peers(slots, own, idx, *, rows, name):
    n, r, cols = slots.shape

    def body(me_ref, *refs):
        me = me_ref[0]
        mine = refs[n][...].astype(F32)
        acc = None
        for k in range(n):
            term = jnp.where(me == k, mine, refs[k][...].astype(F32))
            acc = term if acc is None else acc + term
        refs[n + 1][...] = acc

    def slot_spec(k):
        return pl.BlockSpec((None, rows, cols), lambda i, me: (jnp.where(me[0] == k, (k + 1) % n, k), i, 0))

    return pl.pallas_call(
        body, name=name,
        grid_spec=pltpu.PrefetchScalarGridSpec(
            num_scalar_prefetch=1, grid=(r // rows,),
            in_specs=[slot_spec(k) for k in range(n)] + [pl.BlockSpec((None, rows, cols), lambda i, me: (me[0], i, 0))],
            out_specs=pl.BlockSpec((rows, cols), lambda i, me: (i, 0))),
        out_shape=jax.ShapeDtypeStruct((r, cols), F32),
        compiler_params=_params(("parallel",)),
    )(idx, *([slots] * n), own)


def _sum_slots(slots, *, rows, name):
    n, r, cols = slots.shape

    def body(s_ref, o_ref):
        acc = s_ref[0].astype(F32)
        for k in range(1, n):
            acc = acc + s_ref[k].astype(F32)
        o_ref[...] = acc

    return pl.pallas_call(
        body, name=name, grid=(r // rows,),
        in_specs=[pl.BlockSpec((n, rows, cols), lambda i: (0, i, 0))],
        out_specs=pl.BlockSpec((rows, cols), lambda i: (i, 0)),
        out_shape=jax.ShapeDtypeStruct((r, cols), F32),
        compiler_params=_params(("parallel",)),
    )(slots)


def _pack_small(d, names, rows):
    flat = jnp.concatenate([d[n].astype(F32).reshape(-1) for n in names])
    return jnp.pad(flat, (0, rows * 128 - flat.shape[0])).reshape(rows, 128)


def _unpack_small(p, names):
    flat = p.reshape(-1)
    out, off = {}, 0
    for n in names:
        size = math.prod(SMALL_SHAPES[n])
        out[n] = flat[off:off + size].reshape(SMALL_SHAPES[n])
        off += size
    return out


def _adamw_3d(w, g, m, v, *, name):
    def body(w_ref, g_ref, m_ref, v_ref, d_ref, nm_ref, nv_ref):
        _adamw_math(w_ref, g_ref, m_ref, v_ref, d_ref, nm_ref, nv_ref)

    blk = pl.BlockSpec((8,) + w.shape[1:], lambda i: (i, 0, 0))
    shp = jax.ShapeDtypeStruct(w.shape, F32)
    return pl.pallas_call(body, name=name, grid=(w.shape[0] // 8,), in_specs=[blk] * 4, out_specs=[blk] * 3,
                          out_shape=[shp] * 3, compiler_params=_params(("parallel",)))(w, g, m, v)


def kernel(x, positions, norm_w, w_in, q_norm_w, k_norm_w, sinks, a_re, a_im, log_step, b_re, b_im, c_re, c_im, d_skip, w_glu, b_glu, attn_out_norm_w, ssm_out_norm_w, w_out, loss_target, m_norm_w, m_w_in, m_q_norm_w, m_k_norm_w, m_sinks, m_a_re, m_a_im, m_log_step, m_b_re, m_b_im, m_c_re, m_c_im, m_d_skip, m_w_glu, m_b_glu, m_attn_out_norm_w, m_ssm_out_norm_w, m_w_out, v_norm_w, v_w_in, v_q_norm_w, v_k_norm_w, v_sinks, v_a_re, v_a_im, v_log_step, v_b_re, v_b_im, v_c_re, v_c_im, v_d_skip, v_w_glu, v_b_glu, v_attn_out_norm_w, v_ssm_out_norm_w, v_w_out):
    small_w = dict(norm_w=norm_w, q_norm_w=q_norm_w, k_norm_w=k_norm_w, sinks=sinks, a_re=a_re, a_im=a_im,
                   log_step=log_step, b_re=b_re, b_im=b_im, c_re=c_re, c_im=c_im, d_skip=d_skip, b_glu=b_glu,
                   attn_out_norm_w=attn_out_norm_w, ssm_out_norm_w=ssm_out_norm_w)
    small_m = dict(norm_w=m_norm_w, q_norm_w=m_q_norm_w, k_norm_w=m_k_norm_w, sinks=m_sinks, a_re=m_a_re, a_im=m_a_im,
                   log_step=m_log_step, b_re=m_b_re, b_im=m_b_im, c_re=m_c_re, c_im=m_c_im, d_skip=m_d_skip,
                   b_glu=m_b_glu, attn_out_norm_w=m_attn_out_norm_w, ssm_out_norm_w=m_ssm_out_norm_w)
    small_v = dict(norm_w=v_norm_w, q_norm_w=v_q_norm_w, k_norm_w=v_k_norm_w, sinks=v_sinks, a_re=v_a_re, a_im=v_a_im,
                   log_step=v_log_step, b_re=v_b_re, b_im=v_b_im, c_re=v_c_re, c_im=v_c_im, d_skip=v_d_skip,
                   b_glu=v_b_glu, attn_out_norm_w=v_attn_out_norm_w, ssm_out_norm_w=v_ssm_out_norm_w)
    c_idx = lax.axis_index("c").astype(jnp.int32).reshape(1)
    chip_idx = (2 * lax.axis_index("x") + lax.axis_index("y")).astype(jnp.int32).reshape(1)
    dev_idx = 2 * chip_idx + c_idx

    xs = x[0]
    tgt = loss_target[0]
    t = xs.shape[0]
    posf = positions[0].astype(F32).reshape(t, 1)

    mx, my = lax.axis_index("x"), lax.axis_index("y")
    slab_order = jnp.stack([2 * mx + my, 2 * (1 - mx) + my, 2 * mx + (1 - my), 2 * (1 - mx) + (1 - my)]).astype(jnp.int32)
    proj, hn, w_in_all = _inproj(xs, norm_w, w_in.astype(BF16), slab_order)
    inv_freq = ROPE_THETA ** (-jnp.arange(0, HEAD_DIM, 2, dtype=F32) / HEAD_DIM)
    rope = _rope_table(posf, jnp.tile(inv_freq, 4).reshape(1, 128))
    qw = jnp.tile(q_norm_w, 2).reshape(1, 128)
    kw = jnp.tile(k_norm_w, 2).reshape(1, 128)
    sink_row = sinks.reshape(1, N_HEADS)
    oa, w_glu_all, w_out_all = _attn_fwd(proj, rope, qw, kw, sink_row, [w_glu.astype(BF16), w_out.astype(BF16)])
    w_glu_b = w_glu_all.reshape(SSM_W, SSM_W)
    w_out_b = w_out_all.reshape(D_MODEL, D_MODEL)

    lam_r, lam_i, pw_r, pw_i, bb_r, bb_i = _ssm_prep(a_re, a_im, log_step, b_re, b_im, t // N_SEG)
    rows8 = lambda a: jnp.broadcast_to(a.reshape(SSM_GB, 1, SSM_ST), (SSM_GB, N_SEG, SSM_ST))
    lam_r8, lam_i8, pw_r8, pw_i8 = rows8(lam_r), rows8(lam_i), rows8(pw_r), rows8(pw_i)
    ssm_w_in = jnp.concatenate([_block_diag_in(bb_r), _block_diag_in(bb_i)], axis=1).astype(BF16)
    ssm_w_out = jnp.concatenate([_block_diag_out(c_re), _block_diag_out(-c_im)], axis=2).astype(BF16)
    d_row = d_skip.reshape(1, SSM_W)
    y, hc = _ssm_fwd(proj, lam_r8, lam_i8, pw_r8, pw_i8, ssm_w_in, ssm_w_out, d_row)
    b_glu_row = b_glu.reshape(1, SSM_W)
    os_, yg = _glu_fwd(y, proj, w_glu_b, b_glu_row)
    aw = attn_out_norm_w.reshape(1, ATTN_W)
    sw = ssm_out_norm_w.reshape(1, SSM_W)
    merged, dout, sq_err = _outproj(oa, os_, aw, sw, w_out_b, xs, tgt)
    loss = lax.psum(0.5 * sq_err[0, 0] / D_MODEL, MESH_AXES)

    doa, dos, g_aw, g_sw = _outproj_bwd(dout, oa, os_, aw, sw, w_out_b)
    dout_b = dout.astype(BF16)
    (g_w_out_b,) = _matmul_tn(merged, dout_b, tm=512, tn=1024, name="grad_w_out", dtypes=(BF16,))
    dy, dzs, da, g_b_glu = _glu_bwd(y, proj, dos, w_glu_b, b_glu_row)
    (g_w_glu_b,) = _matmul_tn(yg, da, tm=512, tn=1024, name="grad_w_glu", dtypes=(BF16,))
    du, g_wi, g_wo, g_lam, g_d = _ssm_bwd(proj, dy, hc, lam_r8, lam_i8, pw_r8, pw_i8, ssm_w_in, ssm_w_out, d_row)
    early = [g_w_glu_b.reshape(8, 128, SSM_W), g_w_out_b.reshape(8, 256, D_MODEL)]
    dproj, g_qw, g_kw, g_sink, *early_slots = _attn_bwd(proj, rope, qw, kw, sink_row, doa, du, dzs, early)
    g_w_in, g_w_in_b = _matmul_tn(hn, dproj, tm=512, tn=SHARD_W, name="grad_w_in", slabs=True)
    in_shape = (4, 2, D_MODEL // 2, SHARD_W)
    (from_sib,) = _pair_swap([g_w_in_b.reshape(in_shape)])
    pair_in = _add_halves(g_w_in.reshape(in_shape), from_sib, c_idx, rows=512, name="pair_sum")
    grad_x, g_nw, in_slots = _inproj_bwd(dproj, w_in_all, xs, norm_w, dout, [pair_in])

    g_wi = g_wi.reshape(SSM_G, SSM_H, 2 * SSM_P)
    g_wo = g_wo.reshape(SSM_G, SSM_H, 2 * SSM_P)
    g_bb_r = g_wi[:, :, :SSM_P].transpose(0, 2, 1).reshape(SSM_G, SSM_P * SSM_H)
    g_bb_i = g_wi[:, :, SSM_P:].transpose(0, 2, 1).reshape(SSM_G, SSM_P * SSM_H)
    g_a_re, g_a_im, g_ls, g_b_re, g_b_im = _ssm_param_grads(
        a_re, a_im, log_step, b_re, b_im, g_lam[:, 0, :SSM_ST].reshape(SSM_G, SSM_P),
        g_lam[:, 0, SSM_ST:].reshape(SSM_G, SSM_P), g_bb_r, g_bb_i)
    small_g = dict(
        norm_w=g_nw, q_norm_w=g_qw[0, :64] + g_qw[0, 64:], k_norm_w=g_kw[0, :64] + g_kw[0, 64:],
        sinks=g_sink[0, :N_HEADS], a_re=g_a_re, a_im=g_a_im, log_step=g_ls, b_re=g_b_re, b_im=g_b_im,
        c_re=g_wo[:, :, :SSM_P], c_im=-g_wo[:, :, SSM_P:], d_skip=g_d,
        b_glu=g_b_glu, attn_out_norm_w=g_aw, ssm_out_norm_w=g_sw)

    mine = [_sum_peers(in_slots, pair_in, chip_idx, rows=512, name="sum_w_in"),
            _sum_peers(early_slots[0], early[0], dev_idx, rows=128, name="sum_w_glu"),
            _sum_peers(early_slots[1], early[1], dev_idx, rows=128, name="sum_w_out")]
    packed = _pack_small(small_g, SMALL, 8 * PACK_ROWS).reshape(8, PACK_ROWS, 128)
    *theirs, small_slots = _half_swap_and_scatter(mine, packed)
    summed = _sum_slots(small_slots, rows=PACK_ROWS, name="small_sum")
    small_red = _exchange_slices(summed, False, "small_gather").reshape(8 * PACK_ROWS, 128)

    big = [_adamw_halves(w_in, mine[0], theirs[0], m_w_in, v_w_in, c_idx, rows=256, name="adamw_w_in"),
           _adamw_halves(w_glu, mine[1], theirs[1], m_w_glu, v_w_glu, c_idx, rows=128, name="adamw_w_glu"),
           _adamw_halves(w_out, mine[2], theirs[2], m_w_out, v_w_out, c_idx, rows=256, name="adamw_w_out")]
    g_in_sh, g_glu_sh, g_out_sh = (b[0] for b in big)
    upd = [b[1:] for b in big]
    grads = _unpack_small(small_red, SMALL)
    flat_first = sum(math.prod(SMALL_SHAPES[n]) for n in SMALL_3D) // 128
    sd, sm, sv = _adamw(_pack_small(small_w, SMALL_FLAT, FLAT_ROWS), small_red[flat_first:flat_first + FLAT_ROWS],
                        _pack_small(small_m, SMALL_FLAT, FLAT_ROWS), _pack_small(small_v, SMALL_FLAT, FLAT_ROWS),
                        rows=FLAT_ROWS, name="adamw_small")
    deltas, new_m, new_v = (_unpack_small(a, SMALL_FLAT) for a in (sd, sm, sv))
    for n in SMALL_3D:
        deltas[n], new_m[n], new_v[n] = _adamw_3d(small_w[n], grads[n], small_m[n], small_v[n], name="adamw_" + n)
    grads.update(w_in=g_in_sh, w_glu=g_glu_sh, w_out=g_out_sh)
    for n, (d, m_, v_) in zip(("w_in", "w_glu", "w_out"), upd):
        deltas[n], new_m[n], new_v[n] = d, m_, v_
    order = ["norm_w", "w_in", "q_norm_w", "k_norm_w", "sinks", "a_re", "a_im", "log_step", "b_re", "b_im", "c_re",
             "c_im", "d_skip", "w_glu", "b_glu", "attn_out_norm_w", "ssm_out_norm_w", "w_out"]
    return (loss, grad_x[None], *[grads[n] for n in order], *[deltas[n] for n in order],
            *[new_m[n] for n in order], *[new_v[n] for n in order])
```
